```python
import math
import numpy as np
import jax, jax.numpy as jnp
from jax import lax

D_MODEL = 1024
BATCH = 16
SEQ = 256
DEPTH = 2
DEC_BATCH = 2
DEC_SEQ = 1024
PAST_LEN = 512

GRID_W = 64
Q_BLOCK = 128
EPS = 1e-6
ROPE_BASE = 10000.0

MLA_HEADS = 8
MLA_Q_RANK = 256
MLA_KV_RANK = 128
MLA_NOPE = 64
MLA_ROPE = 32
MLA_V = 64
MLA_QK = MLA_NOPE + MLA_ROPE
RET_HEADS = 4
RET_DK = 64
RET_DV = 128
RET_CHUNK = 64
DIFF_HEADS = 4
DIFF_DH = 64
DIFF_W = DIFF_HEADS * 2 * DIFF_DH
RWKV_HEADS = 8
RWKV_HS = 64
RWKV_W = RWKV_HEADS * RWKV_HS
RWKV_W_LORA = 64
RWKV_A_LORA = 64
RWKV_G_LORA = 128
D_FF = 2816

EVEN_SIZES = (MLA_Q_RANK, MLA_KV_RANK, MLA_ROPE, RET_HEADS * RET_DK, RET_HEADS * RET_DK, RET_HEADS * RET_DV, RET_HEADS * RET_DV)
EVEN_IN = sum(EVEN_SIZES)
RWKV_SIZES = (RWKV_W, RWKV_W, RWKV_W, RWKV_W_LORA, RWKV_W_LORA, RWKV_A_LORA, RWKV_A_LORA, RWKV_G_LORA)
RWKV_IN = sum(RWKV_SIZES)
ODD_SIZES = (DIFF_W, DIFF_W, DIFF_W, RWKV_IN)
ODD_IN = sum(ODD_SIZES)

kernel_name = 'hybrid_diffusion_prefix_trunk_step'


def rms_norm(x, g):
    xf = x.astype(jnp.float32)
    y = xf * lax.rsqrt(jnp.mean(jnp.square(xf), axis=-1, keepdims=True) + EPS)
    return (y * g.astype(jnp.float32)).astype(x.dtype)


def split_cols(x, sizes):
    idx = np.cumsum(sizes)[:-1].tolist()
    return jnp.split(x, idx, axis=-1)


def modulation(cond, ada_w, ada_b):
    m = (jax.nn.silu(cond) @ ada_w + ada_b)[:, None, :]
    return jnp.split(m, 6, axis=-1)


def axial_rope(row, col, rot_dim):
    n_freq = rot_dim // 4
    inv = ROPE_BASE ** (-jnp.arange(n_freq, dtype=jnp.float32) / n_freq)
    ang = jnp.concatenate([row.astype(jnp.float32)[:, None] * inv, col.astype(jnp.float32)[:, None] * inv], -1)
    return jnp.cos(ang), jnp.sin(ang)


def apply_rope(x, cos, sin):
    shape = (x.shape[1],) + (1,) * (x.ndim - 3) + (cos.shape[-1],)
    cos, sin = cos.reshape(shape), sin.reshape(shape)
    xf = x.astype(jnp.float32)
    x1, x2 = xf[..., 0::2], xf[..., 1::2]
    out = jnp.stack([x1 * cos - x2 * sin, x1 * sin + x2 * cos], -1).reshape(x.shape)
    return out.astype(x.dtype)


def rope_tail(x, cos, sin):
    r = 2 * cos.shape[-1]
    return jnp.concatenate([x[..., :-r], apply_rope(x[..., -r:], cos, sin)], -1)


def centred_neighbours(x):
    xp = jnp.pad(x, ((0, 0), (1, 1), (0, 0)))
    return 0.5 * (xp[:, :-2] + xp[:, 2:])


def dwconv3(x, w, b):
    xp = jnp.pad(x, ((0, 0), (1, 1), (0, 0)))
    return xp[:, :-2] * w[0] + x * w[1] + xp[:, 2:] * w[2] + b


def attend_blocks(q, k, v):
    b, nq, h, dq = q.shape
    nb = nq // Q_BLOCK
    scale = dq ** -0.5
    qb = jnp.moveaxis(q.reshape(b, nb, Q_BLOCK, h, dq), 1, 0)

    def one_block(qi):
        s = jnp.einsum('bqhd,bkhd->bhqk', qi, k).astype(jnp.float32) * scale
        p = jax.nn.softmax(s, axis=-1).astype(v.dtype)
        return jnp.einsum('bhqk,bkhe->bqhe', p, v)

    o = lax.map(one_block, qb)
    return jnp.moveaxis(o, 0, 1).reshape(b, nq, h, v.shape[-1])


def diff_attend_blocks(q, k, v, lam):
    b, nq, h, _, d = q.shape
    nb = nq // Q_BLOCK
    scale = d ** -0.5
    qb = jnp.moveaxis(q.reshape(b, nb, Q_BLOCK, h, 2, d), 1, 0)

    def one_block(qi):
        s = jnp.einsum('bqhcd,bkhcd->bchqk', qi, k).astype(jnp.float32) * scale
        p = jax.nn.softmax(s, axis=-1)
        w = (p[:, 0] - lam * p[:, 1]).astype(v.dtype)
        return jnp.einsum('bhqk,bkhe->bqhe', w, v)

    o = lax.map(one_block, qb)
    return jnp.moveaxis(o, 0, 1).reshape(b, nq, h, v.shape[-1])


def retention_chunks(q, k, v, log_g, s0):
    b, n, h, _ = q.shape
    dv = v.shape[-1]
    nc = n // RET_CHUNK

    def chunks(t):
        return jnp.moveaxis(t.astype(jnp.float32).reshape(b, nc, RET_CHUNK, h, t.shape[-1]), 1, 0)

    pos = jnp.arange(RET_CHUNK, dtype=jnp.float32)
    diff = pos[:, None] - pos[None, :]
    intra = jnp.where(diff >= 0, jnp.exp(log_g[:, None, None] * jnp.maximum(diff, 0.0)), 0.0)
    q_dec = jnp.exp(log_g[None, :] * (pos[:, None] + 1.0))
    k_dec = jnp.exp(log_g[None, :] * (RET_CHUNK - 1.0 - pos[:, None]))
    c_dec = jnp.exp(log_g * RET_CHUNK)

    def step(state, inp):
        qc, kc, vc = inp
        scores = jnp.einsum('bihd,bjhd->bhij', qc, kc) * intra
        o = jnp.einsum('bhij,bjhe->bihe', scores, vc) + jnp.einsum('bihd,bhde->bihe', qc * q_dec[:, :, None], state)
        state = state * c_dec[:, None, None] + jnp.einsum('bjhd,bjhe->bhde', kc * k_dec[:, :, None], vc)
        return state, o

    state, o = lax.scan(step, s0.astype(jnp.float32), (chunks(q), chunks(k), chunks(v)))
    return jnp.moveaxis(o, 0, 1).reshape(b, n, h, dv), state


def retention_bidir(q, k, v, decay_logit, s0f, s0b):
    log_g = -jax.nn.softplus(-decay_logit.astype(jnp.float32))
    o_f, s_f = retention_chunks(q, k, v, log_g[0], s0f)
    o_b, s_b = retention_chunks(q[:, ::-1], k[:, ::-1], v[:, ::-1], log_g[1], s0b)
    return o_f + o_b[:, ::-1], s_f, s_b


def rwkv7_scan(r, decay, k, v, kk, a, s0):
    def step(state, inp):
        r_t, w_t, k_t, v_t, kk_t, a_t = inp
        sa = jnp.einsum('bhvk,bhk->bhv', state, kk_t)
        state = (state * w_t[:, :, None, :] - sa[..., None] * (kk_t * a_t)[:, :, None, :]
                 + v_t[..., None] * k_t[:, :, None, :])
        return state, jnp.einsum('bhvk,bhk->bhv', state, r_t)

    xs = tuple(jnp.moveaxis(t, 1, 0) for t in (r, decay, k, v, kk, a))
    state, y = lax.scan(step, s0.astype(jnp.float32), xs)
    return jnp.moveaxis(y, 0, 1), state


def rwkv_inputs(p, mu, w0, w_up, a0, a_up, g_up, k_k, k_a):
    b, n, _ = p.shape
    p = p + (centred_neighbours(p) - p) * mu
    r, k, v, wdf, wdb, adf, adb, gd = split_cols(p, RWKV_SIZES)
    hs = (b, n, RWKV_HEADS, RWKV_HS)
    r, k, v = (t.astype(jnp.float32).reshape(hs) for t in (r, k, v))
    kk = k * k_k.reshape(RWKV_HEADS, RWKV_HS)
    kk = kk * lax.rsqrt(jnp.sum(kk * kk, axis=-1, keepdims=True) + EPS)
    g = jax.nn.sigmoid(gd) @ g_up
    dirs = []
    for d, (wd, ad) in enumerate(((wdf, adf), (wdb, adb))):
        pre = (w0[d] + jnp.tanh(wd) @ w_up[d]).astype(jnp.float32)
        decay = jnp.exp(-jnp.exp(-jax.nn.softplus(-pre) - 0.5)).reshape(hs)
        a = jax.nn.sigmoid((a0[d] + ad @ a_up[d]).astype(jnp.float32)).reshape(hs)
        k_d = k * (1.0 + (a - 1.0) * k_a.reshape(RWKV_HEADS, RWKV_HS))
        dirs.append((decay, a, k_d))
    return r, v, kk, g, dirs


def rwkv_mix(rw, s0f, s0b, r_k, gn):
    r, v, kk, g, dirs = rw
    (w_f, a_f, k_f), (w_b, a_b, k_b) = dirs
    b, n = r.shape[:2]
    y_f, s_f = rwkv7_scan(r, w_f, k_f, v, kk, a_f, s0f)
    y_b, s_b = rwkv7_scan(r[:, ::-1], w_b[:, ::-1], k_b[:, ::-1], v[:, ::-1], kk[:, ::-1], a_b[:, ::-1], s0b)
    y = rms_norm(y_f + y_b[:, ::-1], gn.reshape(RWKV_HEADS, RWKV_HS))
    bonus = (jnp.sum(r * k_f * r_k, -1, keepdims=True) + jnp.sum(r * k_b * r_k, -1, keepdims=True)) * v
    return (y + bonus).reshape(b, n, RWKV_W) * g, s_f, s_b


def mla_keys_values(ckv, krope, w_ukv, kn):
    b, n, _ = ckv.shape
    kv = (ckv @ w_ukv).reshape(b, n, MLA_HEADS, MLA_NOPE + MLA_V)
    k_nope, v = kv[..., :MLA_NOPE], kv[..., MLA_NOPE:]
    k_rope = jnp.broadcast_to(krope[:, :, None, :], (b, n, MLA_HEADS, MLA_ROPE))
    return rms_norm(jnp.concatenate([k_nope, k_rope], -1), kn), v


def even_inputs(h, pe):
    w_in, q_norm, kv_norm, w_uq, w_ukv, qn, kn = pe[:7]
    b, n, _ = h.shape
    cq, ckv, krope, rq, rk, rv, rg = split_cols(h @ w_in, EVEN_SIZES)
    q = rms_norm((rms_norm(cq, q_norm) @ w_uq).reshape(b, n, MLA_HEADS, MLA_QK), qn)
    ckv = rms_norm(ckv, kv_norm)
    k, v = mla_keys_values(ckv, krope, w_ukv, kn)
    rq = rq.reshape(b, n, RET_HEADS, RET_DK)
    rk = rk.reshape(b, n, RET_HEADS, RET_DK) * (RET_DK ** -0.5)
    rv = rv.reshape(b, n, RET_HEADS, RET_DV)
    return q, k, v, ckv, krope, rq, rk, rv, rg


def even_output(mla_o, ret_o, rg, ret_gn, dtype):
    b, n = mla_o.shape[:2]
    ret = jax.nn.silu(rg) * rms_norm(ret_o, ret_gn.reshape(RET_HEADS, RET_DV)).reshape(b, n, -1)
    return jnp.concatenate([mla_o.reshape(b, n, -1).astype(dtype), ret.astype(dtype)], -1)


def even_context(h, pe):
    q, k, v, ckv, krope, rq, rk, rv, rg = even_inputs(h, pe)
    mla_o = attend_blocks(q, k, v)
    s0 = jnp.zeros((h.shape[0], RET_HEADS, RET_DK, RET_DV), jnp.float32)
    ret_o, s_f, s_b = retention_bidir(rq, rk, rv, pe[7], s0, s0)
    out = even_output(mla_o, ret_o, rg, pe[8], h.dtype)
    return out, ckv, krope, jnp.stack([s_f, s_b], 1).astype(h.dtype)


def even_latent(h, ckv_c, krope_c, st, pe, cos, sin):
    q, k, v, _, _, rq, rk, rv, rg = even_inputs(h, pe)
    q, k = rope_tail(q, cos, sin), rope_tail(k, cos, sin)
    k_c, v_c = mla_keys_values(ckv_c, krope_c, pe[4], pe[6])
    mla_o = attend_blocks(q, jnp.concatenate([k, k_c], 1), jnp.concatenate([v, v_c], 1))
    ret_o, _, _ = retention_bidir(rq, rk, rv, pe[7], st[:, 0], st[:, 1])
    return even_output(mla_o, ret_o, rg, pe[8], h.dtype)


def odd_inputs(h, po):
    w_in, qn, kn = po[:3]
    b, n, _ = h.shape
    dq, dk, dv, p = split_cols(h @ w_in, ODD_SIZES)
    q = rms_norm(dq.reshape(b, n, DIFF_HEADS, 2, DIFF_DH), qn)
    k = rms_norm(dk.reshape(b, n, DIFF_HEADS, 2, DIFF_DH), kn)
    v = dv.reshape(b, n, DIFF_HEADS, 2 * DIFF_DH)
    rw = rwkv_inputs(p, *po[5:13])
    return q, k, v, rw


def diff_lambda(lam_vec, lam_init):
    lv = lam_vec.astype(jnp.float32)
    return jnp.exp(jnp.sum(lv[0] * lv[1])) - jnp.exp(jnp.sum(lv[2] * lv[3])) + lam_init


def odd_output(diff_o, rw_o, diff_gn, lam_init, dtype):
    b, n = diff_o.shape[:2]
    d = rms_norm(diff_o, diff_gn.reshape(DIFF_HEADS, 2 * DIFF_DH)) * (1.0 - lam_init)
    return jnp.concatenate([d.reshape(b, n, -1).astype(dtype), rw_o.astype(dtype)], -1)


def odd_context(h, po, lam_init):
    q, k, v, rw = odd_inputs(h, po)
    diff_o = diff_attend_blocks(q, k, v, diff_lambda(po[3], lam_init))
    s0 = jnp.zeros((h.shape[0], RWKV_HEADS, RWKV_HS, RWKV_HS), jnp.float32)
    rw_o, s_f, s_b = rwkv_mix(rw, s0, s0, po[13], po[14])
    out = odd_output(diff_o, rw_o, po[4], lam_init, h.dtype)
    return out, k, v, jnp.stack([s_f, s_b], 1).astype(h.dtype)


def odd_latent(h, k_c, v_c, st, po, lam_init, cos, sin):
    q, k, v, rw = odd_inputs(h, po)
    q, k = apply_rope(q, cos, sin), apply_rope(k, cos, sin)
    diff_o = diff_attend_blocks(q, jnp.concatenate([k, k_c], 1), jnp.concatenate([v, v_c], 1),
                                diff_lambda(po[3], lam_init))
    rw_o, _, _ = rwkv_mix(rw, st[:, 0], st[:, 1], po[13], po[14])
    return odd_output(diff_o, rw_o, po[4], lam_init, h.dtype)


def conv_ffn(h, up, cw, cb, down):
    u = dwconv3(h @ up, cw, cb)
    a, b = jnp.split(u, 2, axis=-1)
    return (jax.nn.silu(a) * b) @ down


def setup_inputs(seed: int = 0) -> dict:
    key = jax.random.key(seed)
    kit = iter(jax.random.split(key, 64))

    def nrm(shape, s=1.0):
        return s * jax.random.normal(next(kit), shape, jnp.float32)

    def gain(shape):
        return 1.0 + 0.02 * jax.random.normal(next(kit), shape, jnp.float32)

    ne, no = (DEPTH + 1) // 2, DEPTH // 2
    d = D_MODEL
    eps = 2.0 ** (-5.0 - jnp.arange(RET_HEADS, dtype=jnp.float32))
    ret_base = jnp.log((1.0 - eps) / eps)
    conv_centre = jnp.zeros((3, 1), jnp.float32).at[1].set(1.0)
    return {
        'x_prompt': nrm((BATCH, SEQ, d)),
        'x_sample': nrm((DEC_BATCH, DEC_SEQ, d)),
        'cache_mla_ckv': nrm((DEC_BATCH, ne, PAST_LEN, MLA_KV_RANK)),
        'cache_mla_krope': nrm((DEC_BATCH, ne, PAST_LEN, MLA_ROPE)),
        'state_ret': nrm((DEC_BATCH, ne, 2, RET_HEADS, RET_DK, RET_DV)),
        'cache_diff_k': nrm((DEC_BATCH, no, PAST_LEN, DIFF_HEADS, 2, DIFF_DH)),
        'cache_diff_v': nrm((DEC_BATCH, no, PAST_LEN, DIFF_HEADS, 2 * DIFF_DH)),
        'state_rwkv': nrm((DEC_BATCH, no, 2, RWKV_HEADS, RWKV_HS, RWKV_HS), 0.3),
        'c': nrm((DEC_BATCH, d)),
        'c_ctx': nrm((d,)),
        'ada_w': nrm((DEPTH, d, 6 * d), 0.5 * d ** -0.5),
        'ada_b': nrm((DEPTH, 6 * d), 0.01),
        'norm_mix_g': gain((DEPTH, d)),
        'norm_ffn_g': gain((DEPTH, d)),
        'w_out': nrm((DEPTH, d, d), d ** -0.5),
        'ffn_up': nrm((DEPTH, d, 2 * D_FF), d ** -0.5),
        'ffn_conv_w': conv_centre + nrm((DEPTH, 3, 2 * D_FF), 0.3),
        'ffn_conv_b': nrm((DEPTH, 2 * D_FF), 0.01),
        'ffn_down': nrm((DEPTH, D_FF, d), D_FF ** -0.5),
        'a_w_in': nrm((ne, d, EVEN_IN), d ** -0.5),
        'mla_q_norm': gain((ne, MLA_Q_RANK)),
        'mla_kv_norm': gain((ne, MLA_KV_RANK)),
        'mla_w_uq': nrm((ne, MLA_Q_RANK, MLA_HEADS * MLA_QK), MLA_Q_RANK ** -0.5),
        'mla_w_ukv': nrm((ne, MLA_KV_RANK, MLA_HEADS * (MLA_NOPE + MLA_V)), MLA_KV_RANK ** -0.5),
        'mla_qn': gain((ne, MLA_QK)),
        'mla_kn': gain((ne, MLA_QK)),
        'ret_decay': ret_base + nrm((ne, 2, RET_HEADS), 0.1),
        'ret_gn': gain((ne, RET_HEADS * RET_DV)),
        'b_w_in': nrm((no, d, ODD_IN), d ** -0.5),
        'diff_qn': gain((no, DIFF_DH)),
        'diff_kn': gain((no, DIFF_DH)),
        'diff_lam': nrm((no, 4, DIFF_DH), 0.1),
        'diff_gn': gain((no, DIFF_W)),
        'rwkv_mu': jax.random.uniform(next(kit), (no, RWKV_IN), jnp.float32),
        'rwkv_w0': -2.0 + nrm((no, 2, RWKV_W)),
        'rwkv_w_up': nrm((no, 2, RWKV_W_LORA, RWKV_W), 0.5 * RWKV_W_LORA ** -0.5),
        'rwkv_a0': nrm((no, 2, RWKV_W), 0.5),
        'rwkv_a_up': nrm((no, 2, RWKV_A_LORA, RWKV_W), RWKV_A_LORA ** -0.5),
        'rwkv_g_up': nrm((no, RWKV_G_LORA, RWKV_W), RWKV_G_LORA ** -0.5),
        'rwkv_k_k': 0.85 + nrm((no, RWKV_W), 0.05),
        'rwkv_k_a': gain((no, RWKV_W)),
        'rwkv_r_k': nrm((no, RWKV_HEADS, RWKV_HS), 0.1),
        'rwkv_gn': gain((no, RWKV_W)),
    }


def reference(x_prompt, x_sample, cache_mla_ckv, cache_mla_krope, state_ret, cache_diff_k, cache_diff_v, state_rwkv,
              c, c_ctx, ada_w, ada_b, norm_mix_g, norm_ffn_g, w_out, ffn_up, ffn_conv_w, ffn_conv_b, ffn_down,
              a_w_in, mla_q_norm, mla_kv_norm, mla_w_uq, mla_w_ukv, mla_qn, mla_kn, ret_decay, ret_gn,
              b_w_in, diff_qn, diff_kn, diff_lam, diff_gn, rwkv_mu, rwkv_w0, rwkv_w_up, rwkv_a0, rwkv_a_up,
              rwkv_g_up, rwkv_k_k, rwkv_k_a, rwkv_r_k, rwkv_gn):
    n_lat = x_sample.shape[1]
    rows = n_lat // GRID_W
    t = jnp.arange(rows * GRID_W)
    row, col = t // GRID_W, t % GRID_W
    cos_m, sin_m = axial_rope(row, col, MLA_ROPE)
    cos_d, sin_d = axial_rope(row, col, DIFF_DH)
    cond_ctx = c_ctx[None, :]
    xc, xl = x_prompt, x_sample
    new_ckv, new_krope, new_ret, new_dk, new_dv, new_rwkv = [], [], [], [], [], []
    for l in range(DEPTH):
        sh_c, sc_c, ga_c, shf_c, scf_c, gf_c = modulation(cond_ctx, ada_w[l], ada_b[l])
        sh_l, sc_l, ga_l, shf_l, scf_l, gf_l = modulation(c, ada_w[l], ada_b[l])
        hc = rms_norm(xc, norm_mix_g[l]) * (1.0 + sc_c) + sh_c
        hl = rms_norm(xl, norm_mix_g[l]) * (1.0 + sc_l) + sh_l
        j = l // 2
        if l % 2 == 0:
            pe = (a_w_in[j], mla_q_norm[j], mla_kv_norm[j], mla_w_uq[j], mla_w_ukv[j], mla_qn[j], mla_kn[j],
                  ret_decay[j], ret_gn[j])
            oc, ckv, krope, st = even_context(hc, pe)
            ol = even_latent(hl, cache_mla_ckv[:, j], cache_mla_krope[:, j], state_ret[:, j], pe, cos_m, sin_m)
            new_ckv.append(ckv)
            new_krope.append(krope)
            new_ret.append(st)
        else:
            lam_init = 0.8 - 0.6 * math.exp(-0.3 * l)
            po = (b_w_in[j], diff_qn[j], diff_kn[j], diff_lam[j], diff_gn[j], rwkv_mu[j], rwkv_w0[j], rwkv_w_up[j],
                  rwkv_a0[j], rwkv_a_up[j], rwkv_g_up[j], rwkv_k_k[j], rwkv_k_a[j], rwkv_r_k[j], rwkv_gn[j])
            oc, k_new, v_new, st = odd_context(hc, po, lam_init)
            ol = odd_latent(hl, cache_diff_k[:, j], cache_diff_v[:, j], state_rwkv[:, j], po, lam_init, cos_d, sin_d)
            new_dk.append(k_new)
            new_dv.append(v_new)
            new_rwkv.append(st)
        xc = xc + ga_c * (oc @ w_out[l])
        xl = xl + ga_l * (ol @ w_out[l])
        hc = rms_norm(xc, norm_ffn_g[l]) * (1.0 + scf_c) + shf_c
        hl = rms_norm(xl, norm_ffn_g[l]) * (1.0 + scf_l) + shf_l
        xc = xc + gf_c * conv_ffn(hc, ffn_up[l], ffn_conv_w[l], ffn_conv_b[l], ffn_down[l])
        xl = xl + gf_l * conv_ffn(hl, ffn_up[l], ffn_conv_w[l], ffn_conv_b[l], ffn_down[l])
    return (xc, xl, jnp.stack(new_ckv, 1), jnp.stack(new_krope, 1), jnp.stack(new_ret, 1),
            jnp.stack(new_dk, 1), jnp.stack(new_dv, 1), jnp.stack(new_rwkv, 1))
```

```python
import functools
import math

import numpy as np
import jax
import jax.numpy as jnp
from jax import lax
from jax.experimental import pallas as pl
from jax.experimental.pallas import tpu as pltpu

D_MODEL = 1024
BATCH = 16
SEQ = 256
DEPTH = 2
DEC_BATCH = 2
DEC_SEQ = 1024
PAST_LEN = 512
GRID_W = 64
EPS = 1e-6
ROPE_BASE = 10000.0

MLA_HEADS = 8
MLA_Q_RANK = 256
MLA_KV_RANK = 128
MLA_NOPE = 64
MLA_ROPE = 32
MLA_V = 64
MLA_QK = MLA_NOPE + MLA_ROPE
RET_HEADS = 4
RET_DK = 64
RET_DV = 128
DIFF_HEADS = 4
DIFF_DH = 64
DIFF_W = DIFF_HEADS * 2 * DIFF_DH
RWKV_HEADS = 8
RWKV_HS = 64
RWKV_W = RWKV_HEADS * RWKV_HS
RWKV_W_LORA = 64
RWKV_A_LORA = 64
RWKV_G_LORA = 128
D_FF = 2816

N_CTX = BATCH * SEQ
N_LAT = DEC_BATCH * DEC_SEQ
N_TOK = N_CTX + N_LAT
N_GROUPS = 1 + DEC_BATCH

LANES = 128
VMEM_LIMIT = 56 * 1024 * 1024

_PREC = lax.Precision.HIGHEST
F32 = jnp.float32


def _dot(a, b):
    return jnp.dot(a, b, precision=_PREC, preferred_element_type=F32)


def _dot_nt(a, b):
    return lax.dot_general(a, b, (((1,), (1,)), ((), ())), precision=_PREC, preferred_element_type=F32)


def _dot_tn(a, b):
    return lax.dot_general(a, b, (((0,), (0,)), ((), ())), precision=_PREC, preferred_element_type=F32)


def _params(*sem):
    return pltpu.CompilerParams(dimension_semantics=sem, vmem_limit_bytes=VMEM_LIMIT)


def _sigmoid(x):
    return 1.0 / (1.0 + jnp.exp(-x))


def _silu(x):
    return x * _sigmoid(x)


def _softplus(x):
    return jnp.maximum(x, 0.0) + jnp.log(1.0 + jnp.exp(-jnp.abs(x)))


def _rms(x, n):
    return x * lax.rsqrt(jnp.sum(x * x, axis=-1, keepdims=True) * (1.0 / n) + EPS)


def _lane_lo(shape):
    return lax.broadcasted_iota(jnp.int32, shape, len(shape) - 1) < 64


def _seg64_sum(x):
    lo = _lane_lo(x.shape)
    s_lo = jnp.sum(jnp.where(lo, x, 0.0), axis=-1, keepdims=True)
    s_hi = jnp.sum(jnp.where(lo, 0.0, x), axis=-1, keepdims=True)
    return jnp.where(lo, s_lo, s_hi)


def _seq_neighbours(p, tile, tile_rows):
    seq_mask = jnp.where(tile * tile_rows < N_CTX, SEQ - 1, DEC_SEQ - 1)
    pos = lax.broadcasted_iota(jnp.int32, (tile_rows, 1), 0) & seq_mask
    prev = jnp.where(pos == 0, 0.0, pltpu.roll(p, 1, axis=0))
    nxt = jnp.where(pos == seq_mask, 0.0, pltpu.roll(p, tile_rows - 1, axis=0))
    return prev, nxt


def _group_of_tile(i, tile_rows):
    row = i * tile_rows
    return jnp.where(row < N_CTX, 0, 1 + (row - N_CTX) // DEC_SEQ)


def _modulation_kernel(c_ref, w_ref, b_ref, o_ref):
    o_ref[0] = _dot(_silu(c_ref[...]), w_ref[0]) + b_ref[0]


def _modulation(cond8, ada_w, ada_b):
    tn = 512
    n = 6 * D_MODEL
    out = pl.pallas_call(
        _modulation_kernel,
        grid=(DEPTH, n // tn),
        in_specs=[pl.BlockSpec((8, D_MODEL), lambda l, j: (0, 0)),
                  pl.BlockSpec((1, D_MODEL, tn), lambda l, j: (l, 0, j)),
                  pl.BlockSpec((1, 1, tn), lambda l, j: (l, 0, j))],
        out_specs=pl.BlockSpec((1, 8, tn), lambda l, j: (l, 0, j)),
        out_shape=jax.ShapeDtypeStruct((DEPTH, 8, n), F32),
        compiler_params=_params("parallel", "parallel"),
        name="modulation",
    )(cond8, ada_w, ada_b.reshape(DEPTH, 1, n))
    m = out[:, :N_GROUPS].reshape(DEPTH, N_GROUPS, 6, D_MODEL)
    return jnp.pad(m, ((0, 0), (0, 0), (0, 2), (0, 0)))


_TM_SEQ = 1024


def _norm_mod(x, g, mod, off):
    return _rms(x, D_MODEL) * g * (1.0 + mod[off + 1:off + 2, :]) + mod[off:off + 1, :]


def _inproj_kernel(x_ref, g_ref, mod_ref, w_ref, mu_ref, o_ref, h_ref, *, shift):
    i = pl.program_id(0)

    @pl.when(pl.program_id(1) == 0)
    def _():
        h_ref[...] = _norm_mod(x_ref[...], g_ref[...], mod_ref[0], 0)

    p = _dot(h_ref[...], w_ref[...])
    if shift:
        prev, nxt = _seq_neighbours(p, i, _TM_SEQ)
        p = p + (0.5 * (prev + nxt) - p) * mu_ref[...]
    o_ref[...] = p


def _inproj(x, g, mod, w, mu, tn, shift):
    n = w.shape[1]
    tm = _TM_SEQ
    return pl.pallas_call(
        functools.partial(_inproj_kernel, shift=shift),
        grid=(N_TOK // tm, n // tn),
        in_specs=[pl.BlockSpec((tm, D_MODEL), lambda i, j: (i, 0)),
                  pl.BlockSpec((1, D_MODEL), lambda i, j: (0, 0)),
                  pl.BlockSpec((1, 8, D_MODEL), lambda i, j: (_group_of_tile(i, tm), 0, 0)),
                  pl.BlockSpec((D_MODEL, tn), lambda i, j: (0, j)),
                  pl.BlockSpec((1, tn), lambda i, j: (0, j))],
        out_specs=pl.BlockSpec((tm, tn), lambda i, j: (i, j)),
        out_shape=jax.ShapeDtypeStruct((N_TOK, n), F32),
        scratch_shapes=[pltpu.VMEM((tm, D_MODEL), F32)],
        compiler_params=_params("parallel", "arbitrary"),
        name="inproj_shift" if shift else "inproj",
    )(x, g, mod, w, mu)


def _resid_kernel(a_ref, w_ref, x_ref, mod_ref, o_ref, *, gate_row):
    o_ref[...] = x_ref[...] + mod_ref[0, gate_row:gate_row + 1, :] * _dot(a_ref[...], w_ref[...])


def _resid_proj(a, w, x, mod, gate_row):
    tm, tn = 512, 512
    k = a.shape[1]
    return pl.pallas_call(
        functools.partial(_resid_kernel, gate_row=gate_row),
        grid=(N_TOK // tm, D_MODEL // tn),
        in_specs=[pl.BlockSpec((tm, k), lambda i, j: (i, 0)),
                  pl.BlockSpec((k, tn), lambda i, j: (0, j)),
                  pl.BlockSpec((tm, tn), lambda i, j: (i, j)),
                  pl.BlockSpec((1, 8, tn), lambda i, j: (_group_of_tile(i, tm), 0, j))],
        out_specs=pl.BlockSpec((tm, tn), lambda i, j: (i, j)),
        out_shape=jax.ShapeDtypeStruct((N_TOK, D_MODEL), F32),
        compiler_params=_params("parallel", "parallel"),
        name="resid_proj",
    )(a, w, x, mod)


def _ffn_up_kernel(x_ref, g_ref, mod_ref, wa_ref, wb_ref, cwa_ref, cwb_ref, cba_ref, cbb_ref, o_ref, h_ref):
    i = pl.program_id(0)

    @pl.when(pl.program_id(1) == 0)
    def _():
        h_ref[...] = _norm_mod(x_ref[...], g_ref[...], mod_ref[0], 3)

    h = h_ref[...]

    def conv(w_ref, cw_ref, cb_ref):
        u = _dot(h, w_ref[...])
        prev, nxt = _seq_neighbours(u, i, _TM_SEQ)
        return prev * cw_ref[0:1, :] + u * cw_ref[1:2, :] + nxt * cw_ref[2:3, :] + cb_ref[...]

    o_ref[...] = _silu(conv(wa_ref, cwa_ref, cba_ref)) * conv(wb_ref, cwb_ref, cbb_ref)


def _ffn_up(x, g, mod, up, cw, cb):
    tm, tn = _TM_SEQ, 256
    nb = D_FF // tn
    cb = cb.reshape(1, 2 * D_FF)
    return pl.pallas_call(
        _ffn_up_kernel,
        grid=(N_TOK // tm, nb),
        in_specs=[pl.BlockSpec((tm, D_MODEL), lambda i, j: (i, 0)),
                  pl.BlockSpec((1, D_MODEL), lambda i, j: (0, 0)),
                  pl.BlockSpec((1, 8, D_MODEL), lambda i, j: (_group_of_tile(i, tm), 0, 0)),
                  pl.BlockSpec((D_MODEL, tn), lambda i, j: (0, j)),
                  pl.BlockSpec((D_MODEL, tn), lambda i, j: (0, j + nb)),
                  pl.BlockSpec((3, tn), lambda i, j: (0, j)),
                  pl.BlockSpec((3, tn), lambda i, j: (0, j + nb)),
                  pl.BlockSpec((1, tn), lambda i, j: (0, j)),
                  pl.BlockSpec((1, tn), lambda i, j: (0, j + nb))],
        out_specs=pl.BlockSpec((tm, tn), lambda i, j: (i, j)),
        out_shape=jax.ShapeDtypeStruct((N_TOK, D_FF), F32),
        scratch_shapes=[pltpu.VMEM((tm, D_MODEL), F32)],
        compiler_params=_params("parallel", "arbitrary"),
        name="ffn_up",
    )(x, g, mod, up, up, cw, cw, cb, cb)


def _rope(y, c, s1, s2, half):
    return y * c + pltpu.roll(y, half, axis=1) * s1 + pltpu.roll(y, LANES - half, axis=1) * s2


def _rope_tables(cos, sin, first, n_pairs, groups):
    n = cos.shape[0]
    c = jnp.ones((n, LANES), F32)
    s1 = jnp.zeros((n, LANES), F32)
    s2 = jnp.zeros((n, LANES), F32)
    for g0 in groups:
        a = g0 + first
        c = c.at[:, a:a + n_pairs].set(cos).at[:, a + n_pairs:a + 2 * n_pairs].set(cos)
        s1 = s1.at[:, a + n_pairs:a + 2 * n_pairs].set(sin)
        s2 = s2.at[:, a:a + n_pairs].set(-sin)

    def all_rows(t, ident):
        ctx = jnp.broadcast_to(ident, (N_CTX, LANES))
        return jnp.concatenate([ctx] + [t] * DEC_BATCH, axis=0)

    return (all_rows(c, jnp.ones((1, LANES), F32)), all_rows(s1, jnp.zeros((1, LANES), F32)),
            all_rows(s2, jnp.zeros((1, LANES), F32)))


def _axial_angles(rot_dim):
    t = np.arange(DEC_SEQ)
    row, col = t // GRID_W, t % GRID_W
    n_freq = rot_dim // 4
    inv = jnp.asarray(ROPE_BASE, F32) ** (-jnp.arange(n_freq, dtype=F32) / n_freq)
    ang = jnp.concatenate([jnp.asarray(row, F32)[:, None] * inv, jnp.asarray(col, F32)[:, None] * inv], -1)
    return jnp.cos(ang), jnp.sin(ang)


def _deinterleave(n):
    return np.concatenate([np.arange(0, n, 2), np.arange(1, n, 2)])


def _mla_q_kernel(cq_ref, qnorm_ref, w_ref, qn_ref, c_ref, s1_ref, s2_ref, o_ref):
    xn = _rms(cq_ref[...], MLA_Q_RANK) * qnorm_ref[...]
    y = _dot(xn, w_ref[...])
    c, s1, s2 = c_ref[...], s1_ref[...], s2_ref[...]
    for h in range(MLA_HEADS):
        yh = y[:, h * LANES:(h + 1) * LANES]
        yh = _rms(yh, MLA_QK) * qn_ref[...]
        o_ref[:, h * LANES:(h + 1) * LANES] = _rope(yh, c, s1, s2, MLA_ROPE // 2)


def _mla_q(p, q_norm, w_uq_p, qn_p, tabs):
    tm = 512
    hw = MLA_HEADS * LANES
    tab_spec = pl.BlockSpec((tm, LANES), lambda i: (i, 0))
    return pl.pallas_call(
        _mla_q_kernel,
        grid=(N_TOK // tm,),
        in_specs=[pl.BlockSpec((tm, MLA_Q_RANK), lambda i: (i, 0)),
                  pl.BlockSpec((1, MLA_Q_RANK), lambda i: (0, 0)),
                  pl.BlockSpec((MLA_Q_RANK, hw), lambda i: (0, 0)),
                  pl.BlockSpec((1, LANES), lambda i: (0, 0)),
                  tab_spec, tab_spec, tab_spec],
        out_specs=pl.BlockSpec((tm, hw), lambda i: (i, 0)),
        out_shape=jax.ShapeDtypeStruct((N_TOK, hw), F32),
        compiler_params=_params("parallel"),
        name="mla_q",
    )(p, q_norm, w_uq_p, qn_p, *tabs)


def _mla_kv_kernel(ckv_ref, kr_ref, kvn_ref, wk_ref, wv_ref, kn_ref, c_ref, s1_ref, s2_ref,
                   k_ref, v_ref, ckvn_ref, *, norm_ckv):
    ckv = ckv_ref[...]
    if norm_ckv:
        ckv = _rms(ckv, MLA_KV_RANK) * kvn_ref[...]
    ckvn_ref[...] = ckv
    kk = _dot(ckv, wk_ref[...])
    v_ref[...] = _dot(ckv, wv_ref[...])
    kr = kr_ref[...]
    c, s1, s2 = c_ref[...], s1_ref[...], s2_ref[...]
    for h in range(MLA_HEADS):
        kh = kk[:, h * LANES:(h + 1) * LANES] + kr
        kh = _rms(kh, MLA_QK) * kn_ref[...]
        k_ref[:, h * LANES:(h + 1) * LANES] = _rope(kh, c, s1, s2, MLA_ROPE // 2)


def _mla_kv(ckv_src, ckv_blk, kr_src, kr_blk, kv_norm, wk_p, wv_p, kn_p, tabs, n_rows, norm_ckv):
    tm = 512
    hw = MLA_HEADS * LANES
    tab_spec = pl.BlockSpec((tm, LANES), lambda i: (i, 0))
    return pl.pallas_call(
        functools.partial(_mla_kv_kernel, norm_ckv=norm_ckv),
        grid=(n_rows // tm,),
        in_specs=[pl.BlockSpec((tm, LANES), lambda i: (i, ckv_blk)),
                  pl.BlockSpec((tm, LANES), lambda i: (i, kr_blk)),
                  pl.BlockSpec((1, LANES), lambda i: (0, 0)),
                  pl.BlockSpec((MLA_KV_RANK, hw), lambda i: (0, 0)),
                  pl.BlockSpec((MLA_KV_RANK, hw), lambda i: (0, 0)),
                  pl.BlockSpec((1, LANES), lambda i: (0, 0)),
                  tab_spec, tab_spec, tab_spec],
        out_specs=[pl.BlockSpec((tm, hw), lambda i: (i, 0)),
                   pl.BlockSpec((tm, hw), lambda i: (i, 0)),
                   pl.BlockSpec((tm, LANES), lambda i: (i, 0))],
        out_shape=[jax.ShapeDtypeStruct((n_rows, hw), F32),
                   jax.ShapeDtypeStruct((n_rows, hw), F32),
                   jax.ShapeDtypeStruct((n_rows, LANES), F32)],
        compiler_params=_params("parallel"),
        name="mla_kv",
    )(ckv_src, kr_src, kv_norm, wk_p, wv_p, kn_p, *tabs)


def _softmax_rows(s):
    p = jnp.exp(s - jnp.max(s, axis=-1, keepdims=True))
    return p, jnp.sum(p, axis=-1, keepdims=True)


def _mla_attn_kernel(q_ref, k_ref, v_ref, o_ref):
    scale = MLA_QK ** -0.5
    outs = []
    for h in range(2):
        sl = slice(h * LANES, (h + 1) * LANES)
        p, l = _softmax_rows(_dot_nt(q_ref[:, sl], k_ref[:, sl]) * scale)
        outs.append(_dot(p, v_ref[:, sl]) / l)
    o_ref[...] = outs[0] + pltpu.roll(outs[1], MLA_V, axis=1)


def _mla_attn(q, k, v, batch, nq, nk, q_row0, tq):
    nqb = nq // tq
    qb0 = q_row0 // tq
    return pl.pallas_call(
        _mla_attn_kernel,
        grid=(batch, MLA_HEADS // 2, nqb),
        in_specs=[pl.BlockSpec((tq, 2 * LANES), lambda b, h, i: (qb0 + b * nqb + i, h)),
                  pl.BlockSpec((nk, 2 * LANES), lambda b, h, i: (b, h)),
                  pl.BlockSpec((nk, 2 * LANES), lambda b, h, i: (b, h))],
        out_specs=pl.BlockSpec((tq, LANES), lambda b, h, i: (b * nqb + i, h)),
        out_shape=jax.ShapeDtypeStruct((batch * nq, MLA_HEADS * MLA_V), F32),
        compiler_params=_params("parallel", "parallel", "arbitrary"),
        name="mla_attn",
    )(q, k, v)


def _ret_kernel(lg_ref, q_ref, k_ref, v_ref, rg_ref, s0_ref, gn_ref, o_ref, st_ref, *, n, tq):
    b, pair, qi = pl.program_id(0), pl.program_id(1), pl.program_id(2)
    q = q_ref[...]
    k = k_ref[...] * (RET_DK ** -0.5)
    lo = _lane_lo((1, LANES))
    row = (qi * tq + lax.broadcasted_iota(jnp.int32, (tq, 1), 0)).astype(F32)
    col = lax.broadcasted_iota(jnp.int32, (1, n), 1).astype(F32)
    diff = row - col
    for h in range(2):
        lgf = lg_ref[0, 2 * pair + h]
        lgb = lg_ref[1, 2 * pair + h]
        mask = lo if h == 0 else jnp.logical_not(lo)
        qh = jnp.where(mask, q, 0.0)
        vh = v_ref[:, h * LANES:(h + 1) * LANES]
        decay = (jnp.where(diff >= 0, jnp.exp(lgf * jnp.maximum(diff, 0.0)), 0.0)
                 + jnp.where(diff <= 0, jnp.exp(lgb * jnp.maximum(-diff, 0.0)), 0.0))
        o = _dot(_dot_nt(qh, k) * decay, vh)
        o = o + _dot(qh * jnp.exp(lgf * (row + 1.0)), s0_ref[0, 0])
        o = o + _dot(qh * jnp.exp(lgb * (n - row)), s0_ref[0, 1])
        y = _rms(o, RET_DV) * gn_ref[:, h * LANES:(h + 1) * LANES]
        o_ref[:, h * LANES:(h + 1) * LANES] = _silu(rg_ref[:, h * LANES:(h + 1) * LANES]) * y

    @pl.when(qi == 0)
    def _():
        pos = lax.broadcasted_iota(jnp.int32, (n, 1), 0).astype(F32)
        for d in range(2):
            acc = None
            for h in range(2):
                lg = lg_ref[d, 2 * pair + h]
                mask = lo if h == 0 else jnp.logical_not(lo)
                expo = (n - 1.0 - pos) if d == 0 else pos
                kd = jnp.where(mask, k * jnp.exp(lg * expo), 0.0)
                term = _dot_tn(kd, v_ref[:, h * LANES:(h + 1) * LANES])
                acc = term if acc is None else acc + term
            lg_rows = jnp.where(lax.broadcasted_iota(jnp.int32, (LANES, 1), 0) < 64,
                                lg_ref[d, 2 * pair], lg_ref[d, 2 * pair + 1])
            st_ref[0, d] = acc + s0_ref[0, d] * jnp.exp(lg_rows * n)


def _retention(log_g, p, q_blk, k_blk, v_blk, g_blk, s0, gn, batch, n, row0, tq):
    nqb = n // tq
    qb0 = row0 // tq
    kb0 = row0 // n
    pairs = RET_HEADS // 2
    return pl.pallas_call(
        functools.partial(_ret_kernel, n=n, tq=tq),
        grid=(batch, pairs, nqb),
        in_specs=[pl.BlockSpec(memory_space=pltpu.SMEM),
                  pl.BlockSpec((tq, LANES), lambda b, h, i: (qb0 + b * nqb + i, q_blk + h)),
                  pl.BlockSpec((n, LANES), lambda b, h, i: (kb0 + b, k_blk + h)),
                  pl.BlockSpec((n, 2 * LANES), lambda b, h, i: (kb0 + b, v_blk // 2 + h)),
                  pl.BlockSpec((tq, 2 * LANES), lambda b, h, i: (qb0 + b * nqb + i, g_blk // 2 + h)),
                  pl.BlockSpec((1, 2, LANES, LANES), lambda b, h, i: (b, 0, h, 0)),
                  pl.BlockSpec((1, 2 * LANES), lambda b, h, i: (0, h))],
        out_specs=[pl.BlockSpec((tq, 2 * LANES), lambda b, h, i: (b * nqb + i, h)),
                   pl.BlockSpec((1, 2, LANES, LANES), lambda b, h, i: (b, 0, h, 0))],
        out_shape=[jax.ShapeDtypeStruct((batch * n, RET_HEADS * RET_DV), F32),
                   jax.ShapeDtypeStruct((batch, 2, RET_HEADS * RET_DK, RET_DV), F32)],
        compiler_params=_params("parallel", "parallel", "arbitrary"),
        name="retention",
    )(log_g, p, p, p, p, s0, gn)


def _diff_qk_kernel(q_ref, k_ref, qn_ref, kn_ref, c_ref, s1_ref, s2_ref, qo_ref, ko_ref):
    c, s1, s2 = c_ref[...], s1_ref[...], s2_ref[...]
    for src, gain, dst in ((q_ref, qn_ref, qo_ref), (k_ref, kn_ref, ko_ref)):
        for h in range(DIFF_HEADS):
            sl = slice(h * LANES, (h + 1) * LANES)
            y = src[:, sl]
            y = y * lax.rsqrt(_seg64_sum(y * y) * (1.0 / DIFF_DH) + EPS) * gain[...]
            dst[:, sl] = _rope(y, c, s1, s2, DIFF_DH // 2)


def _diff_qk(p, qn_p, kn_p, tabs):
    tm = 512
    tab_spec = pl.BlockSpec((tm, LANES), lambda i: (i, 0))
    return pl.pallas_call(
        _diff_qk_kernel,
        grid=(N_TOK // tm,),
        in_specs=[pl.BlockSpec((tm, DIFF_W), lambda i: (i, 0)),
                  pl.BlockSpec((tm, DIFF_W), lambda i: (i, 1)),
                  pl.BlockSpec((1, LANES), lambda i: (0, 0)),
                  pl.BlockSpec((1, LANES), lambda i: (0, 0)),
                  tab_spec, tab_spec, tab_spec],
        out_specs=[pl.BlockSpec((tm, DIFF_W), lambda i: (i, 0)),
                   pl.BlockSpec((tm, DIFF_W), lambda i: (i, 0))],
        out_shape=[jax.ShapeDtypeStruct((N_TOK, DIFF_W), F32)] * 2,
        compiler_params=_params("parallel"),
        name="diff_qk",
    )(p, p, qn_p, kn_p, *tabs)


def _diff_attn_kernel(lam_ref, q_ref, k_ref, v_ref, gn_ref, o_ref, *, lam_init):
    lv = lam_ref[...]
    lam = (jnp.exp(jnp.sum(lv[0:1] * lv[1:2], axis=-1, keepdims=True))
           - jnp.exp(jnp.sum(lv[2:3] * lv[3:4], axis=-1, keepdims=True)) + lam_init)
    scale = DIFF_DH ** -0.5
    q = q_ref[...]
    k = k_ref[...]
    lo = _lane_lo((1, LANES))
    p1, l1 = _softmax_rows(_dot_nt(jnp.where(lo, q, 0.0), k) * scale)
    p2, l2 = _softmax_rows(_dot_nt(jnp.where(lo, 0.0, q), k) * scale)
    w = p1 / l1 - lam * (p2 / l2)
    o = _dot(w, v_ref[...])
    o_ref[...] = _rms(o, 2 * DIFF_DH) * gn_ref[...] * (1.0 - lam_init)


def _diff_attn(lam, q, k, v, gn, batch, nq, nk, q_row0, tq, lam_init):
    nqb = nq // tq
    qb0 = q_row0 // tq
    return pl.pallas_call(
        functools.partial(_diff_attn_kernel, lam_init=lam_init),
        grid=(batch, DIFF_HEADS, nqb),
        in_specs=[pl.BlockSpec((4, DIFF_DH), lambda b, h, i: (0, 0)),
                  pl.BlockSpec((tq, LANES), lambda b, h, i: (qb0 + b * nqb + i, h)),
                  pl.BlockSpec((nk, LANES), lambda b, h, i: (b, h)),
                  pl.BlockSpec((nk, LANES), lambda b, h, i: (b, h)),
                  pl.BlockSpec((1, LANES), lambda b, h, i: (0, h))],
        out_specs=pl.BlockSpec((tq, LANES), lambda b, h, i: (b * nqb + i, h)),
        out_shape=jax.ShapeDtypeStruct((batch * nq, DIFF_W), F32),
        compiler_params=_params("parallel", "parallel", "arbitrary"),
        name="diff_attn",
    )(lam, q, k, v, gn)


def _seg64_sum_wide(x):
    return jnp.concatenate([_seg64_sum(x[:, j * LANES:(j + 1) * LANES]) for j in range(x.shape[1] // LANES)], axis=1)


def _rwkv_pre_kernel(r_ref, k_ref, lo_ref, wup_ref, aup_ref, gup_ref, w0_ref, a0_ref, kk_ref, ka_ref, rk_ref,
                     kkn_ref, g_ref, bonus_ref, w_ref, kd_ref, b_ref):
    r = r_ref[...]
    k = k_ref[...]
    lora = lo_ref[...]
    kk = k * kk_ref[...]
    kkn = kk * lax.rsqrt(_seg64_sum_wide(kk * kk) + EPS)
    kkn_ref[...] = kkn
    g_ref[...] = _dot(_sigmoid(lora[:, 2 * LANES:3 * LANES]), gup_ref[...])
    pre = w0_ref[...] + _dot(jnp.tanh(lora[:, 0:LANES]), wup_ref[...])
    w_ref[...] = jnp.exp(-jnp.exp(-_softplus(-pre) - 0.5))
    a = _sigmoid(a0_ref[...] + _dot(lora[:, LANES:2 * LANES], aup_ref[...]))
    bonus = None
    for d in range(2):
        sl = slice(d * RWKV_W, (d + 1) * RWKV_W)
        a_d = a[:, sl]
        k_d = k * (1.0 + (a_d - 1.0) * ka_ref[...])
        kd_ref[:, sl] = k_d
        b_ref[:, sl] = kkn * a_d
        t = _seg64_sum_wide(r * k_d * rk_ref[...])
        bonus = t if bonus is None else bonus + t
    bonus_ref[...] = bonus


def _rwkv_pre(p, r_blk, k_blk, lo_blk, wup_bd, aup_bd, gup, w0, a0, k_k, k_a, r_k):
    tm = 256
    w = RWKV_W
    row = lambda n: pl.BlockSpec((1, n), lambda i: (0, 0))
    full = lambda a, b: pl.BlockSpec((a, b), lambda i: (0, 0))
    return pl.pallas_call(
        _rwkv_pre_kernel,
        grid=(N_TOK // tm,),
        in_specs=[pl.BlockSpec((tm, w), lambda i: (i, r_blk)),
                  pl.BlockSpec((tm, w), lambda i: (i, k_blk)),
                  pl.BlockSpec((tm, 3 * LANES), lambda i: (i, lo_blk)),
                  full(LANES, 2 * w), full(LANES, 2 * w), full(LANES, w),
                  row(2 * w), row(2 * w), row(w), row(w), row(w)],
        out_specs=[pl.BlockSpec((tm, w), lambda i: (i, 0))] * 3 + [pl.BlockSpec((tm, 2 * w), lambda i: (i, 0))] * 3,
        out_shape=[jax.ShapeDtypeStruct((N_TOK, w), F32)] * 3 + [jax.ShapeDtypeStruct((N_TOK, 2 * w), F32)] * 3,
        compiler_params=_params("parallel"),
        name="rwkv_pre",
    )(p, p, p, wup_bd, aup_bd, gup, w0, a0, k_k, k_a, r_k)


_SCAN_TB = 32


def _rwkv_scan_kernel(r_ref, w_ref, k_ref, v_ref, kk_ref, b_ref, s0_ref, y_ref, st_ref, s_ref):
    tb = pl.program_id(1)

    @pl.when(tb == 0)
    def _():
        s_ref[...] = s0_ref[0]

    def step(t, carry):
        sa = [None, None]
        for k in range(RWKV_HS):
            term = s_ref[k] * kk_ref[0, t, pl.ds(k, 1), :]
            sa[k % 2] = term if sa[k % 2] is None else sa[k % 2] + term
        sa = sa[0] + sa[1]
        v_t = v_ref[0, t]
        y = [None, None]
        for k in range(RWKV_HS):
            s_new = (s_ref[k] * w_ref[0, t, pl.ds(k, 1), :] - sa * b_ref[0, t, pl.ds(k, 1), :]
                     + v_t * k_ref[0, t, pl.ds(k, 1), :])
            s_ref[k] = s_new
            term = s_new * r_ref[0, t, pl.ds(k, 1), :]
            y[k % 2] = term if y[k % 2] is None else y[k % 2] + term
        y_ref[0, t] = y[0] + y[1]
        return carry

    lax.fori_loop(0, _SCAN_TB, step, 0)

    @pl.when(tb == pl.num_programs(1) - 1)
    def _():
        st_ref[0] = s_ref[...]


def _rwkv_scan(r, w, k, v, kk, b, s0):
    g, t = r.shape[0], r.shape[1]
    hs = RWKV_HS
    x_spec = pl.BlockSpec((1, _SCAN_TB, hs, LANES), lambda gi, ti: (gi, ti, 0, 0))
    s_spec = pl.BlockSpec((1, hs, hs, LANES), lambda gi, ti: (gi, 0, 0, 0))
    return pl.pallas_call(
        _rwkv_scan_kernel,
        grid=(g, t // _SCAN_TB),
        in_specs=[x_spec] * 6 + [s_spec],
        out_specs=[x_spec, s_spec],
        out_shape=[jax.ShapeDtypeStruct((g, t, hs, LANES), F32),
                   jax.ShapeDtypeStruct((g, hs, hs, LANES), F32)],
        scratch_shapes=[pltpu.VMEM((hs, hs, LANES), F32)],
        compiler_params=_params("parallel", "arbitrary"),
        name="rwkv_scan",
    )(r, w, k, v, kk, b, s0)


def _rwkv_post_kernel(yf_ref, yb_ref, bonus_ref, v_ref, g_ref, gn_ref, o_ref):
    y = yf_ref[...] + yb_ref[...]
    y = y * lax.rsqrt(_seg64_sum_wide(y * y) * (1.0 / RWKV_HS) + EPS) * gn_ref[...]
    o_ref[...] = (y + bonus_ref[...] * v_ref[...]) * g_ref[...]


def _rwkv_post(yf, yb, bonus, p, v_blk, g, gn):
    tm = 512
    w = RWKV_W
    spec = pl.BlockSpec((tm, w), lambda i: (i, 0))
    return pl.pallas_call(
        _rwkv_post_kernel,
        grid=(N_TOK // tm,),
        in_specs=[spec, spec, spec, pl.BlockSpec((tm, w), lambda i: (i, v_blk)), spec,
                  pl.BlockSpec((1, w), lambda i: (0, 0))],
        out_specs=spec,
        out_shape=jax.ShapeDtypeStruct((N_TOK, w), F32),
        compiler_params=_params("parallel"),
        name="rwkv_post",
    )(yf, yb, bonus, p, g, gn)


def _to_scan_layout(x, reverse):
    ctx = x[:N_CTX].reshape(BATCH, SEQ, RWKV_HEADS, RWKV_HS)
    lat = x[N_CTX:].reshape(DEC_BATCH, DEC_SEQ, RWKV_HEADS, RWKV_HS)
    if reverse:
        ctx, lat = ctx[:, ::-1], lat[:, ::-1]
    ctx = jnp.transpose(ctx, (1, 3, 0, 2)).reshape(SEQ, RWKV_HS, BATCH * RWKV_HEADS)
    lat = jnp.transpose(lat, (1, 3, 0, 2)).reshape(DEC_SEQ, RWKV_HS, DEC_BATCH * RWKV_HEADS)
    return ctx, lat


def _scan_inputs(fwd, bwd):
    cf, lf = _to_scan_layout(fwd, False)
    cb, lb = _to_scan_layout(bwd, True)
    ctx = jnp.stack([cf, cb], 0)
    lat = jnp.concatenate([lf, lb], -1)
    lat = jnp.pad(lat, ((0, 0), (0, 0), (0, LANES - lat.shape[-1])))[None]
    return ctx, lat


def _from_scan_layout(y_ctx, y_lat):
    nl = DEC_BATCH * RWKV_HEADS
    outs = []
    for d in range(2):
        c = jnp.transpose(y_ctx[d].reshape(SEQ, RWKV_HS, BATCH, RWKV_HEADS), (2, 0, 3, 1))
        l = jnp.transpose(y_lat[0, :, :, d * nl:(d + 1) * nl].reshape(DEC_SEQ, RWKV_HS, DEC_BATCH, RWKV_HEADS),
                          (2, 0, 3, 1))
        if d == 1:
            c, l = c[:, ::-1], l[:, ::-1]
        outs.append(jnp.concatenate([c.reshape(N_CTX, RWKV_W), l.reshape(N_LAT, RWKV_W)], 0))
    return outs


def _even_layer(x, mod, g_mix, w_in, q_norm, kv_norm, w_uq, w_ukv, qn, kn, ret_decay, ret_gn,
                cache_ckv, cache_krope, state_ret, tabs_m):
    perm_r = _deinterleave(MLA_ROPE)
    cq, ckv, krope, rq, rk, rv, rg = jnp.split(w_in, np.cumsum(
        (MLA_Q_RANK, MLA_KV_RANK, MLA_ROPE, RET_HEADS * RET_DK, RET_HEADS * RET_DK, RET_HEADS * RET_DV))[:].tolist(),
        axis=1)
    z = lambda n: jnp.zeros((D_MODEL, n), F32)
    w_p = jnp.concatenate([cq, ckv, z(MLA_NOPE), krope[:, perm_r], z(LANES - MLA_QK), rq, rk, rv, rg], axis=1)
    p = _inproj(x, g_mix, mod, w_p, jnp.zeros((1, w_p.shape[1]), F32), 512, False)
    CKV_BLK, KR_BLK, RQ_BLK, RK_BLK, RV_BLK, RG_BLK = 2, 3, 4, 6, 8, 12

    def head_pad(w, n_head, d_head, cols):
        w = w.reshape(w.shape[0], n_head, d_head)[:, :, cols]
        return jnp.pad(w, ((0, 0), (0, 0), (0, LANES - len(cols)))).reshape(w.shape[0], n_head * LANES)

    qk_cols = np.concatenate([np.arange(MLA_NOPE), MLA_NOPE + perm_r])
    w_uq_p = head_pad(w_uq, MLA_HEADS, MLA_QK, qk_cols)
    wk_p = head_pad(w_ukv, MLA_HEADS, MLA_NOPE + MLA_V, np.arange(MLA_NOPE))
    wv_p = head_pad(w_ukv, MLA_HEADS, MLA_NOPE + MLA_V, MLA_NOPE + np.arange(MLA_V))
    qn_p = jnp.pad(qn[qk_cols], (0, LANES - MLA_QK))[None]
    kn_p = jnp.pad(kn[qk_cols], (0, LANES - MLA_QK))[None]

    q = _mla_q(p, q_norm[None], w_uq_p, qn_p, tabs_m)
    k, v, ckvn = _mla_kv(p, CKV_BLK, p, KR_BLK, kv_norm[None], wk_p, wv_p, kn_p, tabs_m, N_TOK, True)

    n_c = DEC_BATCH * PAST_LEN
    kr_c = jnp.pad(cache_krope.reshape(n_c, MLA_ROPE)[:, perm_r], ((0, 0), (MLA_NOPE, LANES - MLA_QK)))
    ident = (jnp.ones((n_c, LANES), F32), jnp.zeros((n_c, LANES), F32), jnp.zeros((n_c, LANES), F32))
    k_c, v_c, _ = _mla_kv(cache_ckv.reshape(n_c, MLA_KV_RANK), 0, kr_c, 0, kv_norm[None], wk_p, wv_p, kn_p,
                          ident, n_c, False)

    hw = MLA_HEADS * LANES

    def with_cache(own, cache):
        own = own[N_CTX:].reshape(DEC_BATCH, DEC_SEQ, hw)
        return jnp.concatenate([own, cache.reshape(DEC_BATCH, PAST_LEN, hw)], 1).reshape(-1, hw)

    o_ctx = _mla_attn(q, k, v, BATCH, SEQ, SEQ, 0, SEQ)
    o_lat = _mla_attn(q, with_cache(k, k_c), with_cache(v, v_c), DEC_BATCH, DEC_SEQ, DEC_SEQ + PAST_LEN, N_CTX, 256)

    log_g = -_softplus(-ret_decay)
    gn = ret_gn[None]
    s0_ctx = jnp.zeros((BATCH, 2, RET_HEADS * RET_DK, RET_DV), F32)
    r_ctx, st_ctx = _retention(log_g, p, RQ_BLK, RK_BLK, RV_BLK, RG_BLK, s0_ctx, gn, BATCH, SEQ, 0, SEQ)
    s0_lat = state_ret.reshape(DEC_BATCH, 2, RET_HEADS * RET_DK, RET_DV)
    r_lat, _ = _retention(log_g, p, RQ_BLK, RK_BLK, RV_BLK, RG_BLK, s0_lat, gn, DEC_BATCH, DEC_SEQ, N_CTX, 256)

    mix = jnp.concatenate([jnp.concatenate([o_ctx, o_lat], 0), jnp.concatenate([r_ctx, r_lat], 0)], 1)
    new_ckv = ckvn[:N_CTX].reshape(BATCH, SEQ, MLA_KV_RANK)
    new_krope = p[:N_CTX, KR_BLK * LANES + MLA_NOPE:KR_BLK * LANES + MLA_QK][:, np.argsort(perm_r)]
    new_krope = new_krope.reshape(BATCH, SEQ, MLA_ROPE)
    new_ret = st_ctx.reshape(BATCH, 2, RET_HEADS, RET_DK, RET_DV)
    return mix, new_ckv, new_krope, new_ret


def _odd_layer(x, mod, g_mix, w_in, qn, kn, lam, diff_gn, mu, w0, w_up, a0, a_up, g_up, k_k, k_a, r_k, gn,
               cache_k, cache_v, state_rwkv, tabs_d, lam_init):
    perm = _deinterleave(DIFF_DH)
    qk_perm = (np.arange(2 * DIFF_W).reshape(-1, DIFF_DH)[:, perm]).reshape(-1)
    w_p = jnp.concatenate([w_in[:, qk_perm], w_in[:, 2 * DIFF_W:]], axis=1)
    n_in = w_p.shape[1]
    mu_full = jnp.concatenate([jnp.zeros((3 * DIFF_W,), F32), mu])[None]
    p = _inproj(x, g_mix, mod, w_p, mu_full, 384, True)
    DV_BLK, R_BLK, K_BLK, V_BLK = 2, 3, 4, 5
    LO_BLK = (6 * RWKV_W) // (3 * LANES)

    qn_p = jnp.tile(qn[perm], 2)[None]
    kn_p = jnp.tile(kn[perm], 2)[None]
    q, k = _diff_qk(p, qn_p, kn_p, tabs_d)
    v = p[:, DV_BLK * DIFF_W:(DV_BLK + 1) * DIFF_W]

    n_c = DEC_BATCH * PAST_LEN
    k_c = cache_k.reshape(n_c, DIFF_HEADS * 2, DIFF_DH)[:, :, perm].reshape(DEC_BATCH, PAST_LEN, DIFF_W)
    v_c = cache_v.reshape(DEC_BATCH, PAST_LEN, DIFF_W)

    def with_cache(own, cache):
        return jnp.concatenate([own[N_CTX:].reshape(DEC_BATCH, DEC_SEQ, DIFF_W), cache], 1).reshape(-1, DIFF_W)

    dgn = diff_gn[None]
    o_ctx = _diff_attn(lam, q, k, v, dgn, BATCH, SEQ, SEQ, 0, SEQ, lam_init)
    o_lat = _diff_attn(lam, q, with_cache(k, k_c), with_cache(v, v_c), dgn, DEC_BATCH, DEC_SEQ, DEC_SEQ + PAST_LEN,
                       N_CTX, 256, lam_init)

    zero = jnp.zeros((RWKV_W_LORA, RWKV_W), F32)
    wup_bd = jnp.concatenate([jnp.concatenate([w_up[0], zero], 1), jnp.concatenate([zero, w_up[1]], 1)], 0)
    aup_bd = jnp.concatenate([jnp.concatenate([a_up[0], zero], 1), jnp.concatenate([zero, a_up[1]], 1)], 0)
    kkn, g, bonus, w, kd, b = _rwkv_pre(p, R_BLK, K_BLK, LO_BLK, wup_bd, aup_bd, g_up, w0.reshape(1, -1),
                                         a0.reshape(1, -1), k_k[None], k_a[None], r_k.reshape(1, -1))

    r = p[:, R_BLK * RWKV_W:(R_BLK + 1) * RWKV_W]
    vv = p[:, V_BLK * RWKV_W:(V_BLK + 1) * RWKV_W]
    W = RWKV_W
    ins_ctx, ins_lat = zip(*[_scan_inputs(f, bk) for f, bk in
                             ((r, r), (w[:, :W], w[:, W:]), (kd[:, :W], kd[:, W:]), (vv, vv), (kkn, kkn),
                              (b[:, :W], b[:, W:]))])
    s0_ctx = jnp.zeros((2, RWKV_HS, RWKV_HS, LANES), F32)
    s0_lat = jnp.transpose(state_rwkv, (4, 3, 1, 0, 2)).reshape(RWKV_HS, RWKV_HS, 2 * DEC_BATCH * RWKV_HEADS)
    s0_lat = jnp.pad(s0_lat, ((0, 0), (0, 0), (0, LANES - s0_lat.shape[-1])))[None]
    y_ctx, st_ctx = _rwkv_scan(*ins_ctx, s0_ctx)
    y_lat, _ = _rwkv_scan(*ins_lat, s0_lat)
    yf, yb = _from_scan_layout(y_ctx, y_lat)
    rw_o = _rwkv_post(yf, yb, bonus, p, V_BLK, g, gn[None])

    mix = jnp.concatenate([jnp.concatenate([o_ctx, o_lat], 0), rw_o], 1)
    inv = np.argsort(perm)
    new_dk = k[:N_CTX].reshape(BATCH, SEQ, DIFF_HEADS, 2, DIFF_DH)[..., inv]
    new_dv = v[:N_CTX].reshape(BATCH, SEQ, DIFF_HEADS, 2 * DIFF_DH)
    new_rwkv = jnp.transpose(st_ctx.reshape(2, RWKV_HS, RWKV_HS, BATCH, RWKV_HEADS), (3, 0, 4, 2, 1))
    return mix, new_dk, new_dv, new_rwkv


def kernel(x_prompt, x_sample, cache_mla_ckv, cache_mla_krope, state_ret, cache_diff_k, cache_diff_v, state_rwkv,
           c, c_ctx, ada_w, ada_b, norm_mix_g, norm_ffn_g, w_out, ffn_up, ffn_conv_w, ffn_conv_b, ffn_down,
           a_w_in, mla_q_norm, mla_kv_norm, mla_w_uq, mla_w_ukv, mla_qn, mla_kn, ret_decay, ret_gn,
           b_w_in, diff_qn, diff_kn, diff_lam, diff_gn, rwkv_mu, rwkv_w0, rwkv_w_up, rwkv_a0, rwkv_a_up,
           rwkv_g_up, rwkv_k_k, rwkv_k_a, rwkv_r_k, rwkv_gn):
    x = jnp.concatenate([x_prompt.reshape(N_CTX, D_MODEL), x_sample.reshape(N_LAT, D_MODEL)], 0)
    cond8 = jnp.pad(jnp.concatenate([c_ctx[None], c], 0), ((0, 8 - N_GROUPS), (0, 0)))
    mod = _modulation(cond8, ada_w, ada_b)

    cos_m, sin_m = _axial_angles(MLA_ROPE)
    cos_d, sin_d = _axial_angles(DIFF_DH)
    tabs_m = _rope_tables(cos_m, sin_m, MLA_NOPE, MLA_ROPE // 2, (0,))
    tabs_d = _rope_tables(cos_d, sin_d, 0, DIFF_DH // 2, (0, DIFF_DH))

    outs = {}
    for l in range(DEPTH):
        j = l // 2
        g_mix = norm_mix_g[l][None]
        if l % 2 == 0:
            mix, outs["ckv"], outs["krope"], outs["ret"] = _even_layer(
                x, mod[l], g_mix, a_w_in[j], mla_q_norm[j], mla_kv_norm[j], mla_w_uq[j], mla_w_ukv[j], mla_qn[j],
                mla_kn[j], ret_decay[j], ret_gn[j], cache_mla_ckv[:, j], cache_mla_krope[:, j], state_ret[:, j],
                tabs_m)
        else:
            lam_init = 0.8 - 0.6 * math.exp(-0.3 * l)
            mix, outs["dk"], outs["dv"], outs["rwkv"] = _odd_layer(
                x, mod[l], g_mix, b_w_in[j], diff_qn[j], diff_kn[j], diff_lam[j], diff_gn[j], rwkv_mu[j],
                rwkv_w0[j], rwkv_w_up[j], rwkv_a0[j], rwkv_a_up[j], rwkv_g_up[j], rwkv_k_k[j], rwkv_k_a[j],
                rwkv_r_k[j], rwkv_gn[j], cache_diff_k[:, j], cache_diff_v[:, j], state_rwkv[:, j], tabs_d, lam_init)
        x = _resid_proj(mix, w_out[l], x, mod[l], 2)
        act = _ffn_up(x, norm_ffn_g[l][None], mod[l], ffn_up[l], ffn_conv_w[l], ffn_conv_b[l])
        x = _resid_proj(act, ffn_down[l], x, mod[l], 5)

    y_prompt = x[:N_CTX].reshape(BATCH, SEQ, D_MODEL)
    y_sample = x[N_CTX:].reshape(DEC_BATCH, DEC_SEQ, D_MODEL)
    return (y_prompt, y_sample, outs["ckv"][:, None], outs["krope"][:, None], outs["ret"][:, None],
            outs["dk"][:, None], outs["dv"][:, None], outs["rwkv"][:, None])
```

```python
import functools
import math

import numpy as np
import jax
import jax.numpy as jnp
from jax import lax
from jax.experimental import pallas as pl
from jax.experimental.pallas import tpu as pltpu

D_MODEL = 1024
BATCH = 16
SEQ = 256
DEPTH = 2
DEC_BATCH = 2
DEC_SEQ = 1024
PAST_LEN = 512
GRID_W = 64
EPS = 1e-6
ROPE_BASE = 10000.0

MLA_HEADS = 8
MLA_Q_RANK = 256
MLA_KV_RANK = 128
MLA_NOPE = 64
MLA_ROPE = 32
MLA_V = 64
MLA_QK = MLA_NOPE + MLA_ROPE
RET_HEADS = 4
RET_DK = 64
RET_DV = 128
DIFF_HEADS = 4
DIFF_DH = 64
DIFF_W = DIFF_HEADS * 2 * DIFF_DH
RWKV_HEADS = 8
RWKV_HS = 64
RWKV_W = RWKV_HEADS * RWKV_HS
RWKV_W_LORA = 64
RWKV_A_LORA = 64
RWKV_G_LORA = 128
D_FF = 2816

N_CTX = BATCH * SEQ
N_LAT = DEC_BATCH * DEC_SEQ
N_TOK = N_CTX + N_LAT
N_GROUPS = 1 + DEC_BATCH

LANES = 128
VMEM_LIMIT = 56 * 1024 * 1024

_PREC = lax.Precision.HIGHEST
F32 = jnp.float32


def _dot(a, b):
    return jnp.dot(a, b, precision=_PREC, preferred_element_type=F32)


def _dot_nt(a, b):
    return lax.dot_general(a, b, (((1,), (1,)), ((), ())), precision=_PREC, preferred_element_type=F32)


def _dot_tn(a, b):
    return lax.dot_general(a, b, (((0,), (0,)), ((), ())), precision=_PREC, preferred_element_type=F32)


BF16 = jnp.bfloat16


def _split_weight(w):
    k, n = w.shape
    tk = 512 if k % 512 == 0 else 256
    tn = 512 if n % 512 == 0 else 384 if n % 384 == 0 else 256
    out = pl.pallas_call(
        _split_weight_kernel,
        grid=(k // tk, n // tn),
        in_specs=[pl.BlockSpec((tk, tn), lambda i, j: (i, j))],
        out_specs=pl.BlockSpec((3, tk, tn), lambda i, j: (0, i, j)),
        out_shape=jax.ShapeDtypeStruct((3, k, n), BF16),
        compiler_params=_params("parallel", "parallel"),
        name="split_weight",
    )(w)
    return out.reshape(3 * k, n)


def _split_weight_kernel(w_ref, o_ref):
    w = w_ref[...]
    hi = w.astype(BF16)
    o_ref[0] = hi
    o_ref[1] = hi
    o_ref[2] = (w - hi.astype(F32)).astype(BF16)


def _store_split(dst_ref, a, k_total, k0=0):
    k = a.shape[1]
    hi = a.astype(BF16)
    lo = (a - hi.astype(F32)).astype(BF16)
    dst_ref[:, k0:k0 + k] = hi
    dst_ref[:, k_total + k0:k_total + k0 + k] = lo
    dst_ref[:, 2 * k_total + k0:2 * k_total + k0 + k] = hi


def _dot_split(a3, w3):
    return jnp.dot(a3, w3, preferred_element_type=F32)


def _params(*sem):
    return pltpu.CompilerParams(dimension_semantics=sem, vmem_limit_bytes=VMEM_LIMIT)


def _sigmoid(x):
    return 1.0 / (1.0 + jnp.exp(-x))


def _silu(x):
    return x * _sigmoid(x)


def _softplus(x):
    return jnp.maximum(x, 0.0) + jnp.log(1.0 + jnp.exp(-jnp.abs(x)))


def _rms(x, n):
    return x * lax.rsqrt(jnp.sum(x * x, axis=-1, keepdims=True) * (1.0 / n) + EPS)


def _lane_lo(shape):
    return lax.broadcasted_iota(jnp.int32, shape, len(shape) - 1) < 64


def _seg64_sum(x):
    lo = _lane_lo(x.shape)
    s_lo = jnp.sum(jnp.where(lo, x, 0.0), axis=-1, keepdims=True)
    s_hi = jnp.sum(jnp.where(lo, 0.0, x), axis=-1, keepdims=True)
    return jnp.where(lo, s_lo, s_hi)


def _seq_neighbours(p, tile, tile_rows):
    seq_mask = jnp.where(tile * tile_rows < N_CTX, SEQ - 1, DEC_SEQ - 1)
    pos = lax.broadcasted_iota(jnp.int32, (tile_rows, 1), 0) & seq_mask
    prev = jnp.where(pos == 0, 0.0, pltpu.roll(p, 1, axis=0))
    nxt = jnp.where(pos == seq_mask, 0.0, pltpu.roll(p, tile_rows - 1, axis=0))
    return prev, nxt


def _group_of_tile(i, tile_rows):
    row = i * tile_rows
    return jnp.where(row < N_CTX, 0, 1 + (row - N_CTX) // DEC_SEQ)


def _modulation_kernel(c_ref, w_ref, b_ref, o_ref):
    o_ref[0] = _dot(_silu(c_ref[...]), w_ref[0]) + b_ref[0]


def _modulation(cond8, ada_w, ada_b):
    tn = 512
    n = 6 * D_MODEL
    out = pl.pallas_call(
        _modulation_kernel,
        grid=(DEPTH, n // tn),
        in_specs=[pl.BlockSpec((8, D_MODEL), lambda l, j: (0, 0)),
                  pl.BlockSpec((1, D_MODEL, tn), lambda l, j: (l, 0, j)),
                  pl.BlockSpec((1, 1, tn), lambda l, j: (l, 0, j))],
        out_specs=pl.BlockSpec((1, 8, tn), lambda l, j: (l, 0, j)),
        out_shape=jax.ShapeDtypeStruct((DEPTH, 8, n), F32),
        compiler_params=_params("parallel", "parallel"),
        name="modulation",
    )(cond8, ada_w, ada_b.reshape(DEPTH, 1, n))
    m = out[:, :N_GROUPS].reshape(DEPTH, N_GROUPS, 6, D_MODEL)
    return jnp.pad(m, ((0, 0), (0, 0), (0, 2), (0, 0)))


_TM_SEQ = 1024


def _norm_mod(x, g, mod, off):
    return _rms(x, D_MODEL) * g * (1.0 + mod[off + 1:off + 2, :]) + mod[off:off + 1, :]


def _inproj_kernel(x_ref, g_ref, mod_ref, w_ref, mu_ref, o_ref, h_ref, *, shift):
    i = pl.program_id(0)

    @pl.when(pl.program_id(1) == 0)
    def _():
        _store_split(h_ref, _norm_mod(x_ref[...], g_ref[...], mod_ref[0], 0), D_MODEL)

    p = _dot_split(h_ref[...], w_ref[...])
    if shift:
        prev, nxt = _seq_neighbours(p, i, _TM_SEQ)
        p = p + (0.5 * (prev + nxt) - p) * mu_ref[...]
    o_ref[...] = p


def _inproj(x, g, mod, w, mu, tn, shift):
    n = w.shape[1]
    tm = _TM_SEQ
    return pl.pallas_call(
        functools.partial(_inproj_kernel, shift=shift),
        grid=(N_TOK // tm, n // tn),
        in_specs=[pl.BlockSpec((tm, D_MODEL), lambda i, j: (i, 0)),
                  pl.BlockSpec((1, D_MODEL), lambda i, j: (0, 0)),
                  pl.BlockSpec((1, 8, D_MODEL), lambda i, j: (_group_of_tile(i, tm), 0, 0)),
                  pl.BlockSpec((3 * D_MODEL, tn), lambda i, j: (0, j)),
                  pl.BlockSpec((1, tn), lambda i, j: (0, j))],
        out_specs=pl.BlockSpec((tm, tn), lambda i, j: (i, j)),
        out_shape=jax.ShapeDtypeStruct((N_TOK, n), F32),
        scratch_shapes=[pltpu.VMEM((tm, 3 * D_MODEL), BF16)],
        compiler_params=_params("parallel", "arbitrary"),
        name="inproj_shift" if shift else "inproj",
    )(x, g, mod, _split_weight(w), mu)


def _resid_kernel(*refs, gate_row, n_act):
    a_refs = refs[:n_act]
    w_ref, x_ref, mod_ref, o_ref, a3_ref = refs[n_act:]
    k_total = a3_ref.shape[1] // 3

    @pl.when(pl.program_id(1) == 0)
    def _():
        k0 = 0
        for a_ref in a_refs:
            _store_split(a3_ref, a_ref[...], k_total, k0)
            k0 += a_ref.shape[1]

    o_ref[...] = x_ref[...] + mod_ref[0, gate_row:gate_row + 1, :] * _dot_split(a3_ref[...], w_ref[...])


def _resid_proj(acts, w, x, mod, gate_row):
    tm, tn = 512, 256
    k = sum(a.shape[1] for a in acts)
    return pl.pallas_call(
        functools.partial(_resid_kernel, gate_row=gate_row, n_act=len(acts)),
        grid=(N_TOK // tm, D_MODEL // tn),
        in_specs=[pl.BlockSpec((tm, a.shape[1]), lambda i, j: (i, 0)) for a in acts]
        + [pl.BlockSpec((3 * k, tn), lambda i, j: (0, j)),
           pl.BlockSpec((tm, tn), lambda i, j: (i, j)),
           pl.BlockSpec((1, 8, tn), lambda i, j: (_group_of_tile(i, tm), 0, j))],
        out_specs=pl.BlockSpec((tm, tn), lambda i, j: (i, j)),
        out_shape=jax.ShapeDtypeStruct((N_TOK, D_MODEL), F32),
        scratch_shapes=[pltpu.VMEM((tm, 3 * k), BF16)],
        compiler_params=_params("parallel", "arbitrary"),
        name="resid_proj",
    )(*acts, _split_weight(w), x, mod)


def _ffn_up_kernel(x_ref, g_ref, mod_ref, wa_ref, wb_ref, cwa_ref, cwb_ref, cba_ref, cbb_ref, o_ref, h_ref):
    i = pl.program_id(0)

    @pl.when(pl.program_id(1) == 0)
    def _():
        _store_split(h_ref, _norm_mod(x_ref[...], g_ref[...], mod_ref[0], 3), D_MODEL)

    h = h_ref[...]

    def conv(w_ref, cw_ref, cb_ref):
        u = _dot_split(h, w_ref[...])
        prev, nxt = _seq_neighbours(u, i, _TM_SEQ)
        return prev * cw_ref[0:1, :] + u * cw_ref[1:2, :] + nxt * cw_ref[2:3, :] + cb_ref[...]

    o_ref[...] = _silu(conv(wa_ref, cwa_ref, cba_ref)) * conv(wb_ref, cwb_ref, cbb_ref)


def _ffn_up(x, g, mod, up, cw, cb):
    tm, tn = _TM_SEQ, 256
    nb = D_FF // tn
    cb = cb.reshape(1, 2 * D_FF)
    up3 = _split_weight(up)
    return pl.pallas_call(
        _ffn_up_kernel,
        grid=(N_TOK // tm, nb),
        in_specs=[pl.BlockSpec((tm, D_MODEL), lambda i, j: (i, 0)),
                  pl.BlockSpec((1, D_MODEL), lambda i, j: (0, 0)),
                  pl.BlockSpec((1, 8, D_MODEL), lambda i, j: (_group_of_tile(i, tm), 0, 0)),
                  pl.BlockSpec((3 * D_MODEL, tn), lambda i, j: (0, j)),
                  pl.BlockSpec((3 * D_MODEL, tn), lambda i, j: (0, j + nb)),
                  pl.BlockSpec((3, tn), lambda i, j: (0, j)),
                  pl.BlockSpec((3, tn), lambda i, j: (0, j + nb)),
                  pl.BlockSpec((1, tn), lambda i, j: (0, j)),
                  pl.BlockSpec((1, tn), lambda i, j: (0, j + nb))],
        out_specs=pl.BlockSpec((tm, tn), lambda i, j: (i, j)),
        out_shape=jax.ShapeDtypeStruct((N_TOK, D_FF), F32),
        scratch_shapes=[pltpu.VMEM((tm, 3 * D_MODEL), BF16)],
        compiler_params=_params("parallel", "arbitrary"),
        name="ffn_up",
    )(x, g, mod, up3, up3, cw, cw, cb, cb)


def _rope(y, c, s1, s2, half):
    return y * c + pltpu.roll(y, half, axis=1) * s1 + pltpu.roll(y, LANES - half, axis=1) * s2


def _rope_tables(cos, sin, first, n_pairs, groups):
    n = cos.shape[0]
    c = jnp.ones((n, LANES), F32)
    s1 = jnp.zeros((n, LANES), F32)
    s2 = jnp.zeros((n, LANES), F32)
    for g0 in groups:
        a = g0 + first
        c = c.at[:, a:a + n_pairs].set(cos).at[:, a + n_pairs:a + 2 * n_pairs].set(cos)
        s1 = s1.at[:, a + n_pairs:a + 2 * n_pairs].set(sin)
        s2 = s2.at[:, a:a + n_pairs].set(-sin)

    def all_rows(t, ident):
        ctx = jnp.broadcast_to(ident, (N_CTX, LANES))
        return jnp.concatenate([ctx] + [t] * DEC_BATCH, axis=0)

    return (all_rows(c, jnp.ones((1, LANES), F32)), all_rows(s1, jnp.zeros((1, LANES), F32)),
            all_rows(s2, jnp.zeros((1, LANES), F32)))


def _axial_angles(rot_dim):
    t = np.arange(DEC_SEQ)
    row, col = t // GRID_W, t % GRID_W
    n_freq = rot_dim // 4
    inv = jnp.asarray(ROPE_BASE, F32) ** (-jnp.arange(n_freq, dtype=F32) / n_freq)
    ang = jnp.concatenate([jnp.asarray(row, F32)[:, None] * inv, jnp.asarray(col, F32)[:, None] * inv], -1)
    return jnp.cos(ang), jnp.sin(ang)


def _deinterleave(n):
    return np.concatenate([np.arange(0, n, 2), np.arange(1, n, 2)])


def _mla_q_kernel(cq_ref, qnorm_ref, w_ref, qn_ref, c_ref, s1_ref, s2_ref, o_ref):
    xn = _rms(cq_ref[...], MLA_Q_RANK) * qnorm_ref[...]
    y = _dot(xn, w_ref[...])
    c, s1, s2 = c_ref[...], s1_ref[...], s2_ref[...]
    for h in range(MLA_HEADS):
        yh = y[:, h * LANES:(h + 1) * LANES]
        yh = _rms(yh, MLA_QK) * qn_ref[...]
        o_ref[:, h * LANES:(h + 1) * LANES] = _rope(yh, c, s1, s2, MLA_ROPE // 2)


def _mla_q(p, q_norm, w_uq_p, qn_p, tabs):
    tm = 512
    hw = MLA_HEADS * LANES
    tab_spec = pl.BlockSpec((tm, LANES), lambda i: (i, 0))
    return pl.pallas_call(
        _mla_q_kernel,
        grid=(N_TOK // tm,),
        in_specs=[pl.BlockSpec((tm, MLA_Q_RANK), lambda i: (i, 0)),
                  pl.BlockSpec((1, MLA_Q_RANK), lambda i: (0, 0)),
                  pl.BlockSpec((MLA_Q_RANK, hw), lambda i: (0, 0)),
                  pl.BlockSpec((1, LANES), lambda i: (0, 0)),
                  tab_spec, tab_spec, tab_spec],
        out_specs=pl.BlockSpec((tm, hw), lambda i: (i, 0)),
        out_shape=jax.ShapeDtypeStruct((N_TOK, hw), F32),
        compiler_params=_params("parallel"),
        name="mla_q",
    )(p, q_norm, w_uq_p, qn_p, *tabs)


def _mla_kv_kernel(ckv_ref, kr_ref, kvn_ref, wk_ref, wv_ref, kn_ref, c_ref, s1_ref, s2_ref,
                   k_ref, v_ref, ckvn_ref, *, norm_ckv):
    ckv = ckv_ref[...]
    if norm_ckv:
        ckv = _rms(ckv, MLA_KV_RANK) * kvn_ref[...]
    ckvn_ref[...] = ckv
    kk = _dot(ckv, wk_ref[...])
    v_ref[...] = _dot(ckv, wv_ref[...])
    kr = kr_ref[...]
    c, s1, s2 = c_ref[...], s1_ref[...], s2_ref[...]
    for h in range(MLA_HEADS):
        kh = kk[:, h * LANES:(h + 1) * LANES] + kr
        kh = _rms(kh, MLA_QK) * kn_ref[...]
        k_ref[:, h * LANES:(h + 1) * LANES] = _rope(kh, c, s1, s2, MLA_ROPE // 2)


def _mla_kv(ckv_src, ckv_blk, kr_src, kr_blk, kv_norm, wk_p, wv_p, kn_p, tabs, n_rows, norm_ckv):
    tm = 512
    hw = MLA_HEADS * LANES
    tab_spec = pl.BlockSpec((tm, LANES), lambda i: (i, 0))
    return pl.pallas_call(
        functools.partial(_mla_kv_kernel, norm_ckv=norm_ckv),
        grid=(n_rows // tm,),
        in_specs=[pl.BlockSpec((tm, LANES), lambda i: (i, ckv_blk)),
                  pl.BlockSpec((tm, LANES), lambda i: (i, kr_blk)),
                  pl.BlockSpec((1, LANES), lambda i: (0, 0)),
                  pl.BlockSpec((MLA_KV_RANK, hw), lambda i: (0, 0)),
                  pl.BlockSpec((MLA_KV_RANK, hw), lambda i: (0, 0)),
                  pl.BlockSpec((1, LANES), lambda i: (0, 0)),
                  tab_spec, tab_spec, tab_spec],
        out_specs=[pl.BlockSpec((tm, hw), lambda i: (i, 0)),
                   pl.BlockSpec((tm, hw), lambda i: (i, 0)),
                   pl.BlockSpec((tm, LANES), lambda i: (i, 0))],
        out_shape=[jax.ShapeDtypeStruct((n_rows, hw), F32),
                   jax.ShapeDtypeStruct((n_rows, hw), F32),
                   jax.ShapeDtypeStruct((n_rows, LANES), F32)],
        compiler_params=_params("parallel"),
        name="mla_kv",
    )(ckv_src, kr_src, kv_norm, wk_p, wv_p, kn_p, *tabs)


def _softmax_rows(s):
    p = jnp.exp(s - jnp.max(s, axis=-1, keepdims=True))
    return p, jnp.sum(p, axis=-1, keepdims=True)


def _mla_attn_kernel(q_ref, k_ref, v_ref, o_ref):
    scale = MLA_QK ** -0.5
    outs = []
    for h in range(2):
        sl = slice(h * LANES, (h + 1) * LANES)
        p, l = _softmax_rows(_dot_nt(q_ref[:, sl], k_ref[:, sl]) * scale)
        outs.append(_dot(p, v_ref[:, sl]) / l)
    o_ref[...] = outs[0] + pltpu.roll(outs[1], MLA_V, axis=1)


def _mla_attn(q, k, v, batch, nq, nk, q_row0, tq):
    nqb = nq // tq
    qb0 = q_row0 // tq
    return pl.pallas_call(
        _mla_attn_kernel,
        grid=(batch, MLA_HEADS // 2, nqb),
        in_specs=[pl.BlockSpec((tq, 2 * LANES), lambda b, h, i: (qb0 + b * nqb + i, h)),
                  pl.BlockSpec((nk, 2 * LANES), lambda b, h, i: (b, h)),
                  pl.BlockSpec((nk, 2 * LANES), lambda b, h, i: (b, h))],
        out_specs=pl.BlockSpec((tq, LANES), lambda b, h, i: (b * nqb + i, h)),
        out_shape=jax.ShapeDtypeStruct((batch * nq, MLA_HEADS * MLA_V), F32),
        compiler_params=_params("parallel", "parallel", "arbitrary"),
        name="mla_attn",
    )(q, k, v)


def _ret_kernel(lg_ref, q_ref, k_ref, v_ref, rg_ref, s0_ref, gn_ref, o_ref, st_ref, *, n, tq):
    b, pair, qi = pl.program_id(0), pl.program_id(1), pl.program_id(2)
    q = q_ref[...]
    k = k_ref[...] * (RET_DK ** -0.5)
    lo = _lane_lo((1, LANES))
    row = (qi * tq + lax.broadcasted_iota(jnp.int32, (tq, 1), 0)).astype(F32)
    col = lax.broadcasted_iota(jnp.int32, (1, n), 1).astype(F32)
    diff = row - col
    for h in range(2):
        lgf = lg_ref[0, 2 * pair + h]
        lgb = lg_ref[1, 2 * pair + h]
        mask = lo if h == 0 else jnp.logical_not(lo)
        qh = jnp.where(mask, q, 0.0)
        vh = v_ref[:, h * LANES:(h + 1) * LANES]
        decay = (jnp.where(diff >= 0, jnp.exp(lgf * jnp.maximum(diff, 0.0)), 0.0)
                 + jnp.where(diff <= 0, jnp.exp(lgb * jnp.maximum(-diff, 0.0)), 0.0))
        o = _dot(_dot_nt(qh, k) * decay, vh)
        o = o + _dot(qh * jnp.exp(lgf * (row + 1.0)), s0_ref[0, 0])
        o = o + _dot(qh * jnp.exp(lgb * (n - row)), s0_ref[0, 1])
        y = _rms(o, RET_DV) * gn_ref[:, h * LANES:(h + 1) * LANES]
        o_ref[:, h * LANES:(h + 1) * LANES] = _silu(rg_ref[:, h * LANES:(h + 1) * LANES]) * y

    @pl.when(qi == 0)
    def _():
        pos = lax.broadcasted_iota(jnp.int32, (n, 1), 0).astype(F32)
        for d in range(2):
            acc = None
            for h in range(2):
                lg = lg_ref[d, 2 * pair + h]
                mask = lo if h == 0 else jnp.logical_not(lo)
                expo = (n - 1.0 - pos) if d == 0 else pos
                kd = jnp.where(mask, k * jnp.exp(lg * expo), 0.0)
                term = _dot_tn(kd, v_ref[:, h * LANES:(h + 1) * LANES])
                acc = term if acc is None else acc + term
            lg_rows = jnp.where(lax.broadcasted_iota(jnp.int32, (LANES, 1), 0) < 64,
                                lg_ref[d, 2 * pair], lg_ref[d, 2 * pair + 1])
            st_ref[0, d] = acc + s0_ref[0, d] * jnp.exp(lg_rows * n)


def _retention(log_g, p, q_blk, k_blk, v_blk, g_blk, s0, gn, batch, n, row0, tq):
    nqb = n // tq
    qb0 = row0 // tq
    kb0 = row0 // n
    pairs = RET_HEADS // 2
    return pl.pallas_call(
        functools.partial(_ret_kernel, n=n, tq=tq),
        grid=(batch, pairs, nqb),
        in_specs=[pl.BlockSpec(memory_space=pltpu.SMEM),
                  pl.BlockSpec((tq, LANES), lambda b, h, i: (qb0 + b * nqb + i, q_blk + h)),
                  pl.BlockSpec((n, LANES), lambda b, h, i: (kb0 + b, k_blk + h)),
                  pl.BlockSpec((n, 2 * LANES), lambda b, h, i: (kb0 + b, v_blk // 2 + h)),
                  pl.BlockSpec((tq, 2 * LANES), lambda b, h, i: (qb0 + b * nqb + i, g_blk // 2 + h)),
                  pl.BlockSpec((1, 2, LANES, LANES), lambda b, h, i: (b, 0, h, 0)),
                  pl.BlockSpec((1, 2 * LANES), lambda b, h, i: (0, h))],
        out_specs=[pl.BlockSpec((tq, 2 * LANES), lambda b, h, i: (b * nqb + i, h)),
                   pl.BlockSpec((1, 2, LANES, LANES), lambda b, h, i: (b, 0, h, 0))],
        out_shape=[jax.ShapeDtypeStruct((batch * n, RET_HEADS * RET_DV), F32),
                   jax.ShapeDtypeStruct((batch, 2, RET_HEADS * RET_DK, RET_DV), F32)],
        compiler_params=_params("parallel", "parallel", "arbitrary"),
        name="retention",
    )(log_g, p, p, p, p, s0, gn)


def _diff_qk_kernel(q_ref, k_ref, qn_ref, kn_ref, c_ref, s1_ref, s2_ref, qo_ref, ko_ref):
    c, s1, s2 = c_ref[...], s1_ref[...], s2_ref[...]
    for src, gain, dst in ((q_ref, qn_ref, qo_ref), (k_ref, kn_ref, ko_ref)):
        for h in range(DIFF_HEADS):
            sl = slice(h * LANES, (h + 1) * LANES)
            y = src[:, sl]
            y = y * lax.rsqrt(_seg64_sum(y * y) * (1.0 / DIFF_DH) + EPS) * gain[...]
            dst[:, sl] = _rope(y, c, s1, s2, DIFF_DH // 2)


def _diff_qk(p, qn_p, kn_p, tabs):
    tm = 512
    tab_spec = pl.BlockSpec((tm, LANES), lambda i: (i, 0))
    return pl.pallas_call(
        _diff_qk_kernel,
        grid=(N_TOK // tm,),
        in_specs=[pl.BlockSpec((tm, DIFF_W), lambda i: (i, 0)),
                  pl.BlockSpec((tm, DIFF_W), lambda i: (i, 1)),
                  pl.BlockSpec((1, LANES), lambda i: (0, 0)),
                  pl.BlockSpec((1, LANES), lambda i: (0, 0)),
                  tab_spec, tab_spec, tab_spec],
        out_specs=[pl.BlockSpec((tm, DIFF_W), lambda i: (i, 0)),
                   pl.BlockSpec((tm, DIFF_W), lambda i: (i, 0))],
        out_shape=[jax.ShapeDtypeStruct((N_TOK, DIFF_W), F32)] * 2,
        compiler_params=_params("parallel"),
        name="diff_qk",
    )(p, p, qn_p, kn_p, *tabs)


def _diff_attn_kernel(lam_ref, q_ref, k_ref, v_ref, gn_ref, o_ref, *, lam_init):
    lv = lam_ref[...]
    lam = (jnp.exp(jnp.sum(lv[0:1] * lv[1:2], axis=-1, keepdims=True))
           - jnp.exp(jnp.sum(lv[2:3] * lv[3:4], axis=-1, keepdims=True)) + lam_init)
    scale = DIFF_DH ** -0.5
    q = q_ref[...]
    k = k_ref[...]
    lo = _lane_lo((1, LANES))
    p1, l1 = _softmax_rows(_dot_nt(jnp.where(lo, q, 0.0), k) * scale)
    p2, l2 = _softmax_rows(_dot_nt(jnp.where(lo, 0.0, q), k) * scale)
    w = p1 / l1 - lam * (p2 / l2)
    o = _dot(w, v_ref[...])
    o_ref[...] = _rms(o, 2 * DIFF_DH) * gn_ref[...] * (1.0 - lam_init)


def _diff_attn(lam, q, k, v, gn, batch, nq, nk, q_row0, tq, lam_init):
    nqb = nq // tq
    qb0 = q_row0 // tq
    return pl.pallas_call(
        functools.partial(_diff_attn_kernel, lam_init=lam_init),
        grid=(batch, DIFF_HEADS, nqb),
        in_specs=[pl.BlockSpec((4, DIFF_DH), lambda b, h, i: (0, 0)),
                  pl.BlockSpec((tq, LANES), lambda b, h, i: (qb0 + b * nqb + i, h)),
                  pl.BlockSpec((nk, LANES), lambda b, h, i: (b, h)),
                  pl.BlockSpec((nk, LANES), lambda b, h, i: (b, h)),
                  pl.BlockSpec((1, LANES), lambda b, h, i: (0, h))],
        out_specs=pl.BlockSpec((tq, LANES), lambda b, h, i: (b * nqb + i, h)),
        out_shape=jax.ShapeDtypeStruct((batch * nq, DIFF_W), F32),
        compiler_params=_params("parallel", "parallel", "arbitrary"),
        name="diff_attn",
    )(lam, q, k, v, gn)


def _seg64_sum_wide(x):
    return jnp.concatenate([_seg64_sum(x[:, j * LANES:(j + 1) * LANES]) for j in range(x.shape[1] // LANES)], axis=1)


def _rwkv_pre_kernel(r_ref, k_ref, lo_ref, wup_ref, aup_ref, gup_ref, w0_ref, a0_ref, kk_ref, ka_ref, rk_ref,
                     kkn_ref, g_ref, bonus_ref, w_ref, kd_ref, b_ref):
    r = r_ref[...]
    k = k_ref[...]
    lora = lo_ref[...]
    kk = k * kk_ref[...]
    kkn = kk * lax.rsqrt(_seg64_sum_wide(kk * kk) + EPS)
    kkn_ref[...] = kkn
    g_ref[...] = _dot(_sigmoid(lora[:, 2 * LANES:3 * LANES]), gup_ref[...])
    pre = w0_ref[...] + _dot(jnp.tanh(lora[:, 0:LANES]), wup_ref[...])
    w_ref[...] = jnp.exp(-jnp.exp(-_softplus(-pre) - 0.5))
    a = _sigmoid(a0_ref[...] + _dot(lora[:, LANES:2 * LANES], aup_ref[...]))
    bonus = None
    for d in range(2):
        sl = slice(d * RWKV_W, (d + 1) * RWKV_W)
        a_d = a[:, sl]
        k_d = k * (1.0 + (a_d - 1.0) * ka_ref[...])
        kd_ref[:, sl] = k_d
        b_ref[:, sl] = kkn * a_d
        t = _seg64_sum_wide(r * k_d * rk_ref[...])
        bonus = t if bonus is None else bonus + t
    bonus_ref[...] = bonus


def _rwkv_pre(p, r_blk, k_blk, lo_blk, wup_bd, aup_bd, gup, w0, a0, k_k, k_a, r_k):
    tm = 256
    w = RWKV_W
    row = lambda n: pl.BlockSpec((1, n), lambda i: (0, 0))
    full = lambda a, b: pl.BlockSpec((a, b), lambda i: (0, 0))
    return pl.pallas_call(
        _rwkv_pre_kernel,
        grid=(N_TOK // tm,),
        in_specs=[pl.BlockSpec((tm, w), lambda i: (i, r_blk)),
                  pl.BlockSpec((tm, w), lambda i: (i, k_blk)),
                  pl.BlockSpec((tm, 3 * LANES), lambda i: (i, lo_blk)),
                  full(LANES, 2 * w), full(LANES, 2 * w), full(LANES, w),
                  row(2 * w), row(2 * w), row(w), row(w), row(w)],
        out_specs=[pl.BlockSpec((tm, w), lambda i: (i, 0))] * 3 + [pl.BlockSpec((tm, 2 * w), lambda i: (i, 0))] * 3,
        out_shape=[jax.ShapeDtypeStruct((N_TOK, w), F32)] * 3 + [jax.ShapeDtypeStruct((N_TOK, 2 * w), F32)] * 3,
        compiler_params=_params("parallel"),
        name="rwkv_pre",
    )(p, p, p, wup_bd, aup_bd, gup, w0, a0, k_k, k_a, r_k)


_SCAN_CHUNK = 32


def _rwkv_step(s_ref, kk, w, kd, b, r, v_at):
    nv = s_ref.shape[1]
    chunk = min(_SCAN_CHUNK, nv)
    ys = []
    for c0 in range(0, nv, chunk):
        sa = [None, None]
        for k in range(RWKV_HS):
            term = s_ref[k, c0:c0 + chunk, :] * kk(k)
            sa[k % 2] = term if sa[k % 2] is None else sa[k % 2] + term
        sa = sa[0] + sa[1]
        vc = v_at(c0, chunk)
        y = [None, None]
        for k in range(RWKV_HS):
            s_new = s_ref[k, c0:c0 + chunk, :] * w(k) - sa * b(k) + vc * kd(k)
            s_ref[k, c0:c0 + chunk, :] = s_new
            term = s_new * r(k)
            y[k % 2] = term if y[k % 2] is None else y[k % 2] + term
        ys.append(y[0] + y[1])
    return ys[0] if len(ys) == 1 else jnp.concatenate(ys, axis=0)


_CTX_TB = 16


def _rwkv_scan_ctx_kernel(r_ref, v_ref, kk_ref, w_ref, k_ref, b_ref, y_ref, st_ref,
                          s_ref, xr, xv, xkk, xw, xk, xb, ys):
    d = pl.program_id(0)
    tb = pl.program_id(1)
    n_state = BATCH * RWKV_HEADS

    @pl.when(tb == 0)
    def _():
        s_ref[...] = jnp.zeros_like(s_ref)

    def load_t(t, carry):
        for src, dst in ((r_ref, xr), (v_ref, xv), (kk_ref, xkk), (w_ref, xw), (k_ref, xk), (b_ref, xb)):
            dst[t] = src[:, t].reshape(n_state, RWKV_HS).T
        return carry

    lax.fori_loop(0, _CTX_TB, load_t, 0)

    def step(i, carry):
        t = jnp.where(d == 0, i, _CTX_TB - 1 - i)
        row = lambda ref: (lambda k: ref[t, pl.ds(k, 1), :])
        ys[t] = _rwkv_step(s_ref, row(xkk), row(xw), row(xk), row(xb), row(xr),
                           lambda c0, n: xv[t, pl.ds(c0, n), :])
        return carry

    lax.fori_loop(0, _CTX_TB, step, 0)

    def store_t(t, carry):
        y_ref[:, t] = ys[t].T.reshape(BATCH, RWKV_HEADS, RWKV_HS)
        return carry

    lax.fori_loop(0, _CTX_TB, store_t, 0)

    @pl.when(tb == pl.num_programs(1) - 1)
    def _():
        st_ref[0] = s_ref[...]


def _rwkv_scan_ctx(p4, r_blk, v_blk, kkn4, w4, kd4, b4):
    nt = SEQ // _CTX_TB
    hs = RWKV_HS
    tblk = lambda d, tb: jnp.where(d == 0, tb, nt - 1 - tb)
    blk = (BATCH, _CTX_TB, RWKV_HEADS, hs)
    shared = lambda hb: pl.BlockSpec(blk, lambda d, tb: (0, tblk(d, tb), hb, 0))
    per_dir = pl.BlockSpec(blk, lambda d, tb: (0, tblk(d, tb), d, 0))
    x_scr = pltpu.VMEM((_CTX_TB, hs, LANES), F32)
    return pl.pallas_call(
        _rwkv_scan_ctx_kernel,
        grid=(2, nt),
        in_specs=[shared(r_blk), shared(v_blk), shared(0), per_dir, per_dir, per_dir],
        out_specs=[per_dir, pl.BlockSpec((1, hs, hs, LANES), lambda d, tb: (d, 0, 0, 0))],
        out_shape=[jax.ShapeDtypeStruct((BATCH, SEQ, 2 * RWKV_HEADS, hs), F32),
                   jax.ShapeDtypeStruct((2, hs, hs, LANES), F32)],
        scratch_shapes=[pltpu.VMEM((hs, hs, LANES), F32)] + [x_scr] * 7,
        compiler_params=_params("parallel", "arbitrary"),
        name="rwkv_scan_ctx",
    )(p4, p4, kkn4, w4, kd4, b4)


_LAT_TB = 32
_LAT_VSPLIT = 4


def _rwkv_scan_lat_kernel(kt_ref, v_ref, s0_ref, y_ref, s_ref):
    @pl.when(pl.program_id(0) == 0)
    def _():
        s_ref[...] = s0_ref[...]

    def step(t, carry):
        row = lambda a: (lambda k: kt_ref[a, t, pl.ds(k, 1), :])
        y_ref[t] = _rwkv_step(s_ref, row(0), row(1), row(2), row(3), row(4),
                              lambda c0, n: v_ref[t, pl.ds(c0, n), :])
        return carry

    lax.fori_loop(0, _LAT_TB, step, 0)


def _rwkv_scan_lat(kt, v, s0):
    hs = RWKV_HS
    nv = hs // _LAT_VSPLIT
    return pl.pallas_call(
        _rwkv_scan_lat_kernel,
        grid=(DEC_SEQ // _LAT_TB,),
        in_specs=[pl.BlockSpec((5, _LAT_TB, hs, LANES), lambda tb: (0, tb, 0, 0)),
                  pl.BlockSpec((_LAT_TB, nv, LANES), lambda tb: (tb, 0, 0)),
                  pl.BlockSpec((hs, nv, LANES), lambda tb: (0, 0, 0))],
        out_specs=pl.BlockSpec((_LAT_TB, nv, LANES), lambda tb: (tb, 0, 0)),
        out_shape=jax.ShapeDtypeStruct((DEC_SEQ, nv, LANES), F32),
        scratch_shapes=[pltpu.VMEM((hs, nv, LANES), F32)],
        compiler_params=_params("arbitrary"),
        name="rwkv_scan_lat",
    )(kt, v, s0)


def _rwkv_post_kernel(yf_ref, yb_ref, bonus_ref, v_ref, g_ref, gn_ref, o_ref):
    y = yf_ref[...] + yb_ref[...]
    y = y * lax.rsqrt(_seg64_sum_wide(y * y) * (1.0 / RWKV_HS) + EPS) * gn_ref[...]
    o_ref[...] = (y + bonus_ref[...] * v_ref[...]) * g_ref[...]


def _rwkv_post(y, bonus, p, v_blk, g, gn):
    tm = 512
    w = RWKV_W
    spec = pl.BlockSpec((tm, w), lambda i: (i, 0))
    return pl.pallas_call(
        _rwkv_post_kernel,
        grid=(N_TOK // tm,),
        in_specs=[spec, pl.BlockSpec((tm, w), lambda i: (i, 1)), spec,
                  pl.BlockSpec((tm, w), lambda i: (i, v_blk)), spec,
                  pl.BlockSpec((1, w), lambda i: (0, 0))],
        out_specs=spec,
        out_shape=jax.ShapeDtypeStruct((N_TOK, w), F32),
        compiler_params=_params("parallel"),
        name="rwkv_post",
    )(y, y, bonus, p, g, gn)


_LAT_STATES = 2 * DEC_BATCH * RWKV_HEADS


def _lat_states(fwd, bwd):
    shape = (DEC_BATCH, DEC_SEQ, RWKV_HEADS, RWKV_HS)
    st = jnp.stack([fwd[N_CTX:].reshape(shape), bwd[N_CTX:].reshape(shape)[:, ::-1]], 0)
    return jnp.transpose(st, (2, 0, 1, 3, 4)).reshape(DEC_SEQ, _LAT_STATES, RWKV_HS)


def _lat_value_layout(x):
    lead = x.shape[:-2]
    nv = RWKV_HS // _LAT_VSPLIT
    x = x.reshape(lead + (_LAT_STATES, _LAT_VSPLIT, nv))
    n = len(lead)
    return jnp.transpose(x, tuple(range(n)) + (n + 2, n + 1, n)).reshape(lead + (nv, LANES))


def _lat_from_value_layout(y):
    nv = RWKV_HS // _LAT_VSPLIT
    y = jnp.transpose(y.reshape(DEC_SEQ, nv, _LAT_VSPLIT, _LAT_STATES), (0, 3, 2, 1))
    y = y.reshape(DEC_SEQ, 2, DEC_BATCH, RWKV_HEADS, RWKV_HS)
    y = jnp.transpose(y, (1, 2, 0, 3, 4)).reshape(2, DEC_BATCH, DEC_SEQ, RWKV_W)
    return y[0].reshape(N_LAT, RWKV_W), y[1, :, ::-1].reshape(N_LAT, RWKV_W)


def _even_layer(x, mod, g_mix, w_in, q_norm, kv_norm, w_uq, w_ukv, qn, kn, ret_decay, ret_gn,
                cache_ckv, cache_krope, state_ret, tabs_m):
    perm_r = _deinterleave(MLA_ROPE)
    cq, ckv, krope, rq, rk, rv, rg = jnp.split(w_in, np.cumsum(
        (MLA_Q_RANK, MLA_KV_RANK, MLA_ROPE, RET_HEADS * RET_DK, RET_HEADS * RET_DK, RET_HEADS * RET_DV))[:].tolist(),
        axis=1)
    z = lambda n: jnp.zeros((D_MODEL, n), F32)
    w_p = jnp.concatenate([cq, ckv, z(MLA_NOPE), krope[:, perm_r], z(LANES - MLA_QK), rq, rk, rv, rg], axis=1)
    p = _inproj(x, g_mix, mod, w_p, jnp.zeros((1, w_p.shape[1]), F32), 512, False)
    CKV_BLK, KR_BLK, RQ_BLK, RK_BLK, RV_BLK, RG_BLK = 2, 3, 4, 6, 8, 12

    def head_pad(w, n_head, d_head, cols):
        w = w.reshape(w.shape[0], n_head, d_head)[:, :, cols]
        return jnp.pad(w, ((0, 0), (0, 0), (0, LANES - len(cols)))).reshape(w.shape[0], n_head * LANES)

    qk_cols = np.concatenate([np.arange(MLA_NOPE), MLA_NOPE + perm_r])
    w_uq_p = head_pad(w_uq, MLA_HEADS, MLA_QK, qk_cols)
    wk_p = head_pad(w_ukv, MLA_HEADS, MLA_NOPE + MLA_V, np.arange(MLA_NOPE))
    wv_p = head_pad(w_ukv, MLA_HEADS, MLA_NOPE + MLA_V, MLA_NOPE + np.arange(MLA_V))
    qn_p = jnp.pad(qn[qk_cols], (0, LANES - MLA_QK))[None]
    kn_p = jnp.pad(kn[qk_cols], (0, LANES - MLA_QK))[None]

    q = _mla_q(p, q_norm[None], w_uq_p, qn_p, tabs_m)
    k, v, ckvn = _mla_kv(p, CKV_BLK, p, KR_BLK, kv_norm[None], wk_p, wv_p, kn_p, tabs_m, N_TOK, True)

    n_c = DEC_BATCH * PAST_LEN
    kr_c = jnp.pad(cache_krope.reshape(n_c, MLA_ROPE)[:, perm_r], ((0, 0), (MLA_NOPE, LANES - MLA_QK)))
    ident = (jnp.ones((n_c, LANES), F32), jnp.zeros((n_c, LANES), F32), jnp.zeros((n_c, LANES), F32))
    k_c, v_c, _ = _mla_kv(cache_ckv.reshape(n_c, MLA_KV_RANK), 0, kr_c, 0, kv_norm[None], wk_p, wv_p, kn_p,
                          ident, n_c, False)

    hw = MLA_HEADS * LANES

    def with_cache(own, cache):
        own = own[N_CTX:].reshape(DEC_BATCH, DEC_SEQ, hw)
        return jnp.concatenate([own, cache.reshape(DEC_BATCH, PAST_LEN, hw)], 1).reshape(-1, hw)

    o_ctx = _mla_attn(q, k, v, BATCH, SEQ, SEQ, 0, SEQ)
    o_lat = _mla_attn(q, with_cache(k, k_c), with_cache(v, v_c), DEC_BATCH, DEC_SEQ, DEC_SEQ + PAST_LEN, N_CTX, 256)

    log_g = -_softplus(-ret_decay)
    gn = ret_gn[None]
    s0_ctx = jnp.zeros((BATCH, 2, RET_HEADS * RET_DK, RET_DV), F32)
    r_ctx, st_ctx = _retention(log_g, p, RQ_BLK, RK_BLK, RV_BLK, RG_BLK, s0_ctx, gn, BATCH, SEQ, 0, SEQ)
    s0_lat = state_ret.reshape(DEC_BATCH, 2, RET_HEADS * RET_DK, RET_DV)
    r_lat, _ = _retention(log_g, p, RQ_BLK, RK_BLK, RV_BLK, RG_BLK, s0_lat, gn, DEC_BATCH, DEC_SEQ, N_CTX, 256)

    mix = [jnp.concatenate([o_ctx, o_lat], 0), jnp.concatenate([r_ctx, r_lat], 0)]
    new_ckv = ckvn[:N_CTX].reshape(BATCH, SEQ, MLA_KV_RANK)
    new_krope = p[:N_CTX, KR_BLK * LANES + MLA_NOPE:KR_BLK * LANES + MLA_QK][:, np.argsort(perm_r)]
    new_krope = new_krope.reshape(BATCH, SEQ, MLA_ROPE)
    new_ret = st_ctx.reshape(BATCH, 2, RET_HEADS, RET_DK, RET_DV)
    return mix, new_ckv, new_krope, new_ret


def _odd_layer(x, mod, g_mix, w_in, qn, kn, lam, diff_gn, mu, w0, w_up, a0, a_up, g_up, k_k, k_a, r_k, gn,
               cache_k, cache_v, state_rwkv, tabs_d, lam_init):
    perm = _deinterleave(DIFF_DH)
    qk_perm = (np.arange(2 * DIFF_W).reshape(-1, DIFF_DH)[:, perm]).reshape(-1)
    w_p = jnp.concatenate([w_in[:, qk_perm], w_in[:, 2 * DIFF_W:]], axis=1)
    n_in = w_p.shape[1]
    mu_full = jnp.concatenate([jnp.zeros((3 * DIFF_W,), F32), mu])[None]
    p = _inproj(x, g_mix, mod, w_p, mu_full, 384, True)
    DV_BLK, R_BLK, K_BLK, V_BLK = 2, 3, 4, 5
    LO_BLK = (6 * RWKV_W) // (3 * LANES)

    qn_p = jnp.tile(qn[perm], 2)[None]
    kn_p = jnp.tile(kn[perm], 2)[None]
    q, k = _diff_qk(p, qn_p, kn_p, tabs_d)
    v = p[:, DV_BLK * DIFF_W:(DV_BLK + 1) * DIFF_W]

    n_c = DEC_BATCH * PAST_LEN
    k_c = cache_k.reshape(n_c, DIFF_HEADS * 2, DIFF_DH)[:, :, perm].reshape(DEC_BATCH, PAST_LEN, DIFF_W)
    v_c = cache_v.reshape(DEC_BATCH, PAST_LEN, DIFF_W)

    def with_cache(own, cache):
        return jnp.concatenate([own[N_CTX:].reshape(DEC_BATCH, DEC_SEQ, DIFF_W), cache], 1).reshape(-1, DIFF_W)

    dgn = diff_gn[None]
    o_ctx = _diff_attn(lam, q, k, v, dgn, BATCH, SEQ, SEQ, 0, SEQ, lam_init)
    o_lat = _diff_attn(lam, q, with_cache(k, k_c), with_cache(v, v_c), dgn, DEC_BATCH, DEC_SEQ, DEC_SEQ + PAST_LEN,
                       N_CTX, 256, lam_init)

    zero = jnp.zeros((RWKV_W_LORA, RWKV_W), F32)
    wup_bd = jnp.concatenate([jnp.concatenate([w_up[0], zero], 1), jnp.concatenate([zero, w_up[1]], 1)], 0)
    aup_bd = jnp.concatenate([jnp.concatenate([a_up[0], zero], 1), jnp.concatenate([zero, a_up[1]], 1)], 0)
    kkn, g, bonus, w, kd, b = _rwkv_pre(p, R_BLK, K_BLK, LO_BLK, wup_bd, aup_bd, g_up, w0.reshape(1, -1),
                                         a0.reshape(1, -1), k_k[None], k_a[None], r_k.reshape(1, -1))

    W = RWKV_W
    n_seq = N_TOK // SEQ
    view4 = lambda a: a.reshape(n_seq, SEQ, a.shape[1] // RWKV_HS, RWKV_HS)
    y_ctx, st_ctx = _rwkv_scan_ctx(view4(p), R_BLK, V_BLK, view4(kkn), view4(w), view4(kd), view4(b))

    r = p[:, R_BLK * W:(R_BLK + 1) * W]
    vv = p[:, V_BLK * W:(V_BLK + 1) * W]
    kt = jnp.stack([_lat_states(kkn, kkn), _lat_states(w[:, :W], w[:, W:]), _lat_states(kd[:, :W], kd[:, W:]),
                    _lat_states(b[:, :W], b[:, W:]), _lat_states(r, r)], 0)
    kt = jnp.tile(jnp.swapaxes(kt, 2, 3), (1, 1, 1, _LAT_VSPLIT))
    v_lat = _lat_value_layout(_lat_states(vv, vv))
    s0_lat = jnp.transpose(state_rwkv, (4, 1, 0, 2, 3)).reshape(RWKV_HS, _LAT_STATES, RWKV_HS)
    y_lat = _rwkv_scan_lat(kt, v_lat, _lat_value_layout(s0_lat))
    y = jnp.concatenate([y_ctx.reshape(N_CTX, 2 * W), jnp.concatenate(_lat_from_value_layout(y_lat), 1)], 0)
    rw_o = _rwkv_post(y, bonus, p, V_BLK, g, gn[None])

    mix = [jnp.concatenate([o_ctx, o_lat], 0), rw_o]
    inv = np.argsort(perm)
    new_dk = k[:N_CTX].reshape(BATCH, SEQ, DIFF_HEADS, 2, DIFF_DH)[..., inv]
    new_dv = v[:N_CTX].reshape(BATCH, SEQ, DIFF_HEADS, 2 * DIFF_DH)
    new_rwkv = jnp.transpose(st_ctx.reshape(2, RWKV_HS, RWKV_HS, BATCH, RWKV_HEADS), (3, 0, 4, 2, 1))
    return mix, new_dk, new_dv, new_rwkv


def kernel(x_prompt, x_sample, cache_mla_ckv, cache_mla_krope, state_ret, cache_diff_k, cache_diff_v, state_rwkv,
           c, c_ctx, ada_w, ada_b, norm_mix_g, norm_ffn_g, w_out, ffn_up, ffn_conv_w, ffn_conv_b, ffn_down,
           a_w_in, mla_q_norm, mla_kv_norm, mla_w_uq, mla_w_ukv, mla_qn, mla_kn, ret_decay, ret_gn,
           b_w_in, diff_qn, diff_kn, diff_lam, diff_gn, rwkv_mu, rwkv_w0, rwkv_w_up, rwkv_a0, rwkv_a_up,
           rwkv_g_up, rwkv_k_k, rwkv_k_a, rwkv_r_k, rwkv_gn):
    x = jnp.concatenate([x_prompt.reshape(N_CTX, D_MODEL), x_sample.reshape(N_LAT, D_MODEL)], 0)
    cond8 = jnp.pad(jnp.concatenate([c_ctx[None], c], 0), ((0, 8 - N_GROUPS), (0, 0)))
    mod = _modulation(cond8, ada_w, ada_b)

    cos_m, sin_m = _axial_angles(MLA_ROPE)
    cos_d, sin_d = _axial_angles(DIFF_DH)
    tabs_m = _rope_tables(cos_m, sin_m, MLA_NOPE, MLA_ROPE // 2, (0,))
    tabs_d = _rope_tables(cos_d, sin_d, 0, DIFF_DH // 2, (0, DIFF_DH))

    outs = {}
    for l in range(DEPTH):
        j = l // 2
        g_mix = norm_mix_g[l][None]
        if l % 2 == 0:
            mix, outs["ckv"], outs["krope"], outs["ret"] = _even_layer(
                x, mod[l], g_mix, a_w_in[j], mla_q_norm[j], mla_kv_norm[j], mla_w_uq[j], mla_w_ukv[j], mla_qn[j],
                mla_kn[j], ret_decay[j], ret_gn[j], cache_mla_ckv[:, j], cache_mla_krope[:, j], state_ret[:, j],
                tabs_m)
        else:
            lam_init = 0.8 - 0.6 * math.exp(-0.3 * l)
            mix, outs["dk"], outs["dv"], outs["rwkv"] = _odd_layer(
                x, mod[l], g_mix, b_w_in[j], diff_qn[j], diff_kn[j], diff_lam[j], diff_gn[j], rwkv_mu[j],
                rwkv_w0[j], rwkv_w_up[j], rwkv_a0[j], rwkv_a_up[j], rwkv_g_up[j], rwkv_k_k[j], rwkv_k_a[j],
                rwkv_r_k[j], rwkv_gn[j], cache_diff_k[:, j], cache_diff_v[:, j], state_rwkv[:, j], tabs_d, lam_init)
        x = _resid_proj(mix, w_out[l], x, mod[l], 2)
        act = _ffn_up(x, norm_ffn_g[l][None], mod[l], ffn_up[l], ffn_conv_w[l], ffn_conv_b[l])
        x = _resid_proj([act], ffn_down[l], x, mod[l], 5)

    y_prompt = x[:N_CTX].reshape(BATCH, SEQ, D_MODEL)
    y_sample = x[N_CTX:].reshape(DEC_BATCH, DEC_SEQ, D_MODEL)
    return (y_prompt, y_sample, outs["ckv"][:, None], outs["krope"][:, None], outs["ret"][:, None],
            outs["dk"][:, None], outs["dv"][:, None], outs["rwkv"][:, None])
```

```python
import functools
import math

import numpy as np
import jax
import jax.numpy as jnp
from jax import lax
from jax.experimental import pallas as pl
from jax.experimental.pallas import tpu as pltpu

D_MODEL = 1024
BATCH = 16
SEQ = 256
DEPTH = 2
DEC_BATCH = 2
DEC_SEQ = 1024
PAST_LEN = 512
GRID_W = 64
EPS = 1e-6
ROPE_BASE = 10000.0

MLA_HEADS = 8
MLA_Q_RANK = 256
MLA_KV_RANK = 128
MLA_NOPE = 64
MLA_ROPE = 32
MLA_V = 64
MLA_QK = MLA_NOPE + MLA_ROPE
RET_HEADS = 4
RET_DK = 64
RET_DV = 128
DIFF_HEADS = 4
DIFF_DH = 64
DIFF_W = DIFF_HEADS * 2 * DIFF_DH
RWKV_HEADS = 8
RWKV_HS = 64
RWKV_W = RWKV_HEADS * RWKV_HS
RWKV_W_LORA = 64
RWKV_A_LORA = 64
RWKV_G_LORA = 128
D_FF = 2816

N_CTX = BATCH * SEQ
N_LAT = DEC_BATCH * DEC_SEQ
N_TOK = N_CTX + N_LAT
N_GROUPS = 1 + DEC_BATCH

LANES = 128
VMEM_LIMIT = 56 * 1024 * 1024

_PREC = lax.Precision.HIGHEST
F32 = jnp.float32


def _dot(a, b):
    return jnp.dot(a, b, precision=_PREC, preferred_element_type=F32)


def _dot_nt(a, b):
    return lax.dot_general(a, b, (((1,), (1,)), ((), ())), precision=_PREC, preferred_element_type=F32)


def _dot_tn(a, b):
    return lax.dot_general(a, b, (((0,), (0,)), ((), ())), precision=_PREC, preferred_element_type=F32)


BF16 = jnp.bfloat16


_PARTS = 1


def _split_weight(w):
    k, n = w.shape
    tk = 512 if k % 512 == 0 else 256
    tn = 512 if n % 512 == 0 else 384 if n % 384 == 0 else 256
    out = pl.pallas_call(
        _split_weight_kernel,
        grid=(k // tk, n // tn),
        in_specs=[pl.BlockSpec((tk, tn), lambda i, j: (i, j))],
        out_specs=pl.BlockSpec((_PARTS, tk, tn), lambda i, j: (0, i, j)),
        out_shape=jax.ShapeDtypeStruct((_PARTS, k, n), BF16),
        compiler_params=_params("parallel", "parallel"),
        name="split_weight",
    )(w)
    return out.reshape(_PARTS * k, n)


def _split_weight_kernel(w_ref, o_ref):
    w = w_ref[...]
    hi = w.astype(BF16)
    o_ref[0] = hi
    if _PARTS == 3:
        o_ref[1] = hi
        o_ref[2] = (w - hi.astype(F32)).astype(BF16)


def _store_split(dst_ref, a, k_total, k0=0):
    k = a.shape[1]
    hi = a.astype(BF16)
    dst_ref[:, k0:k0 + k] = hi
    if _PARTS == 3:
        dst_ref[:, k_total + k0:k_total + k0 + k] = (a - hi.astype(F32)).astype(BF16)
        dst_ref[:, 2 * k_total + k0:2 * k_total + k0 + k] = hi


def _dot_split(a3, w3):
    return jnp.dot(a3, w3, preferred_element_type=F32)


def _dot_bf16(a, b):
    return jnp.dot(a.astype(BF16), b.astype(BF16), preferred_element_type=F32)


def _dot_nt_bf16(a, b):
    return lax.dot_general(a.astype(BF16), b.astype(BF16), (((1,), (1,)), ((), ())), preferred_element_type=F32)


def _params(*sem):
    return pltpu.CompilerParams(dimension_semantics=sem, vmem_limit_bytes=VMEM_LIMIT)


def _sigmoid(x):
    return 1.0 / (1.0 + jnp.exp(-x))


def _silu(x):
    return x * _sigmoid(x)


def _softplus(x):
    return jnp.maximum(x, 0.0) + jnp.log(1.0 + jnp.exp(-jnp.abs(x)))


def _rms(x, n):
    return x * lax.rsqrt(jnp.sum(x * x, axis=-1, keepdims=True) * (1.0 / n) + EPS)


def _lane_lo(shape):
    return lax.broadcasted_iota(jnp.int32, shape, len(shape) - 1) < 64


def _seg64_sum(x):
    lo = _lane_lo(x.shape)
    s_lo = jnp.sum(jnp.where(lo, x, 0.0), axis=-1, keepdims=True)
    s_hi = jnp.sum(jnp.where(lo, 0.0, x), axis=-1, keepdims=True)
    return jnp.where(lo, s_lo, s_hi)


def _seq_neighbours(p, tile, tile_rows):
    seq_mask = jnp.where(tile * tile_rows < N_CTX, SEQ - 1, DEC_SEQ - 1)
    pos = lax.broadcasted_iota(jnp.int32, (tile_rows, 1), 0) & seq_mask
    prev = jnp.where(pos == 0, 0.0, pltpu.roll(p, 1, axis=0))
    nxt = jnp.where(pos == seq_mask, 0.0, pltpu.roll(p, tile_rows - 1, axis=0))
    return prev, nxt


def _group_of_tile(i, tile_rows):
    row = i * tile_rows
    return jnp.where(row < N_CTX, 0, 1 + (row - N_CTX) // DEC_SEQ)


def _modulation_kernel(c_ref, w_ref, b_ref, o_ref):
    o_ref[0] = _dot(_silu(c_ref[...]), w_ref[0]) + b_ref[0]


def _modulation(cond8, ada_w, ada_b):
    tn = 512
    n = 6 * D_MODEL
    out = pl.pallas_call(
        _modulation_kernel,
        grid=(DEPTH, n // tn),
        in_specs=[pl.BlockSpec((8, D_MODEL), lambda l, j: (0, 0)),
                  pl.BlockSpec((1, D_MODEL, tn), lambda l, j: (l, 0, j)),
                  pl.BlockSpec((1, 1, tn), lambda l, j: (l, 0, j))],
        out_specs=pl.BlockSpec((1, 8, tn), lambda l, j: (l, 0, j)),
        out_shape=jax.ShapeDtypeStruct((DEPTH, 8, n), F32),
        compiler_params=_params("parallel", "parallel"),
        name="modulation",
    )(cond8, ada_w, ada_b.reshape(DEPTH, 1, n))
    m = out[:, :N_GROUPS].reshape(DEPTH, N_GROUPS, 6, D_MODEL)
    return jnp.pad(m, ((0, 0), (0, 0), (0, 2), (0, 0)))


_TM_SEQ = 1024


def _norm_mod(x, g, mod, off):
    return _rms(x, D_MODEL) * g * (1.0 + mod[off + 1:off + 2, :]) + mod[off:off + 1, :]


def _inproj_kernel(x_ref, g_ref, mod_ref, w_ref, mu_ref, o_ref, h_ref, *, shift):
    i = pl.program_id(0)

    @pl.when(pl.program_id(1) == 0)
    def _():
        _store_split(h_ref, _norm_mod(x_ref[...], g_ref[...], mod_ref[0], 0), D_MODEL)

    p = _dot_split(h_ref[...], w_ref[...])
    if shift:
        prev, nxt = _seq_neighbours(p, i, _TM_SEQ)
        p = p + (0.5 * (prev + nxt) - p) * mu_ref[...]
    o_ref[...] = p


def _inproj(x, g, mod, w, mu, tn, shift):
    n = w.shape[1]
    tm = _TM_SEQ
    return pl.pallas_call(
        functools.partial(_inproj_kernel, shift=shift),
        grid=(N_TOK // tm, n // tn),
        in_specs=[pl.BlockSpec((tm, D_MODEL), lambda i, j: (i, 0)),
                  pl.BlockSpec((1, D_MODEL), lambda i, j: (0, 0)),
                  pl.BlockSpec((1, 8, D_MODEL), lambda i, j: (_group_of_tile(i, tm), 0, 0)),
                  pl.BlockSpec((_PARTS * D_MODEL, tn), lambda i, j: (0, j)),
                  pl.BlockSpec((1, tn), lambda i, j: (0, j))],
        out_specs=pl.BlockSpec((tm, tn), lambda i, j: (i, j)),
        out_shape=jax.ShapeDtypeStruct((N_TOK, n), F32),
        scratch_shapes=[pltpu.VMEM((tm, _PARTS * D_MODEL), BF16)],
        compiler_params=_params("parallel", "arbitrary"),
        name="inproj_shift" if shift else "inproj",
    )(x, g, mod, _split_weight(w), mu)


def _resid_kernel(*refs, gate_row, n_act):
    a_refs = refs[:n_act]
    w_ref, x_ref, mod_ref, o_ref, a3_ref = refs[n_act:]
    k_total = a3_ref.shape[1] // _PARTS

    @pl.when(pl.program_id(1) == 0)
    def _():
        k0 = 0
        for a_ref in a_refs:
            _store_split(a3_ref, a_ref[...], k_total, k0)
            k0 += a_ref.shape[1]

    o_ref[...] = x_ref[...] + mod_ref[0, gate_row:gate_row + 1, :] * _dot_split(a3_ref[...], w_ref[...])


def _resid_proj(acts, w, x, mod, gate_row):
    tm, tn = 512, 256
    k = sum(a.shape[1] for a in acts)
    return pl.pallas_call(
        functools.partial(_resid_kernel, gate_row=gate_row, n_act=len(acts)),
        grid=(N_TOK // tm, D_MODEL // tn),
        in_specs=[pl.BlockSpec((tm, a.shape[1]), lambda i, j: (i, 0)) for a in acts]
        + [pl.BlockSpec((_PARTS * k, tn), lambda i, j: (0, j)),
           pl.BlockSpec((tm, tn), lambda i, j: (i, j)),
           pl.BlockSpec((1, 8, tn), lambda i, j: (_group_of_tile(i, tm), 0, j))],
        out_specs=pl.BlockSpec((tm, tn), lambda i, j: (i, j)),
        out_shape=jax.ShapeDtypeStruct((N_TOK, D_MODEL), F32),
        scratch_shapes=[pltpu.VMEM((tm, _PARTS * k), BF16)],
        compiler_params=_params("parallel", "arbitrary"),
        name="resid_proj",
    )(*acts, _split_weight(w), x, mod)


def _ffn_up_kernel(x_ref, g_ref, mod_ref, wa_ref, wb_ref, cwa_ref, cwb_ref, cba_ref, cbb_ref, o_ref, h_ref):
    i = pl.program_id(0)

    @pl.when(pl.program_id(1) == 0)
    def _():
        _store_split(h_ref, _norm_mod(x_ref[...], g_ref[...], mod_ref[0], 3), D_MODEL)

    h = h_ref[...]

    def conv(w_ref, cw_ref, cb_ref):
        u = _dot_split(h, w_ref[...])
        prev, nxt = _seq_neighbours(u, i, _TM_SEQ)
        return prev * cw_ref[0:1, :] + u * cw_ref[1:2, :] + nxt * cw_ref[2:3, :] + cb_ref[...]

    o_ref[...] = _silu(conv(wa_ref, cwa_ref, cba_ref)) * conv(wb_ref, cwb_ref, cbb_ref)


def _ffn_up(x, g, mod, up, cw, cb):
    tm, tn = _TM_SEQ, 256
    nb = D_FF // tn
    cb = cb.reshape(1, 2 * D_FF)
    up3 = _split_weight(up)
    return pl.pallas_call(
        _ffn_up_kernel,
        grid=(N_TOK // tm, nb),
        in_specs=[pl.BlockSpec((tm, D_MODEL), lambda i, j: (i, 0)),
                  pl.BlockSpec((1, D_MODEL), lambda i, j: (0, 0)),
                  pl.BlockSpec((1, 8, D_MODEL), lambda i, j: (_group_of_tile(i, tm), 0, 0)),
                  pl.BlockSpec((_PARTS * D_MODEL, tn), lambda i, j: (0, j)),
                  pl.BlockSpec((_PARTS * D_MODEL, tn), lambda i, j: (0, j + nb)),
                  pl.BlockSpec((3, tn), lambda i, j: (0, j)),
                  pl.BlockSpec((3, tn), lambda i, j: (0, j + nb)),
                  pl.BlockSpec((1, tn), lambda i, j: (0, j)),
                  pl.BlockSpec((1, tn), lambda i, j: (0, j + nb))],
        out_specs=pl.BlockSpec((tm, tn), lambda i, j: (i, j)),
        out_shape=jax.ShapeDtypeStruct((N_TOK, D_FF), F32),
        scratch_shapes=[pltpu.VMEM((tm, _PARTS * D_MODEL), BF16)],
        compiler_params=_params("parallel", "arbitrary"),
        name="ffn_up",
    )(x, g, mod, up3, up3, cw, cw, cb, cb)


def _rope(y, c, s1, s2, half):
    return y * c + pltpu.roll(y, half, axis=1) * s1 + pltpu.roll(y, LANES - half, axis=1) * s2


def _rope_tables(cos, sin, first, n_pairs, groups):
    n = cos.shape[0]
    c = jnp.ones((n, LANES), F32)
    s1 = jnp.zeros((n, LANES), F32)
    s2 = jnp.zeros((n, LANES), F32)
    for g0 in groups:
        a = g0 + first
        c = c.at[:, a:a + n_pairs].set(cos).at[:, a + n_pairs:a + 2 * n_pairs].set(cos)
        s1 = s1.at[:, a + n_pairs:a + 2 * n_pairs].set(sin)
        s2 = s2.at[:, a:a + n_pairs].set(-sin)

    def all_rows(t, ident):
        ctx = jnp.broadcast_to(ident, (N_CTX, LANES))
        return jnp.concatenate([ctx] + [t] * DEC_BATCH, axis=0)

    return (all_rows(c, jnp.ones((1, LANES), F32)), all_rows(s1, jnp.zeros((1, LANES), F32)),
            all_rows(s2, jnp.zeros((1, LANES), F32)))


def _axial_angles(rot_dim):
    t = np.arange(DEC_SEQ)
    row, col = t // GRID_W, t % GRID_W
    n_freq = rot_dim // 4
    inv = jnp.asarray(ROPE_BASE, F32) ** (-jnp.arange(n_freq, dtype=F32) / n_freq)
    ang = jnp.concatenate([jnp.asarray(row, F32)[:, None] * inv, jnp.asarray(col, F32)[:, None] * inv], -1)
    return jnp.cos(ang), jnp.sin(ang)


def _deinterleave(n):
    return np.concatenate([np.arange(0, n, 2), np.arange(1, n, 2)])


def _mla_q_kernel(cq_ref, qnorm_ref, w_ref, qn_ref, c_ref, s1_ref, s2_ref, o_ref):
    xn = _rms(cq_ref[...], MLA_Q_RANK) * qnorm_ref[...]
    y = _dot(xn, w_ref[...])
    c, s1, s2 = c_ref[...], s1_ref[...], s2_ref[...]
    for h in range(MLA_HEADS):
        yh = y[:, h * LANES:(h + 1) * LANES]
        yh = _rms(yh, MLA_QK) * qn_ref[...]
        o_ref[:, h * LANES:(h + 1) * LANES] = _rope(yh, c, s1, s2, MLA_ROPE // 2)


def _mla_q(p, q_norm, w_uq_p, qn_p, tabs):
    tm = 512
    hw = MLA_HEADS * LANES
    tab_spec = pl.BlockSpec((tm, LANES), lambda i: (i, 0))
    return pl.pallas_call(
        _mla_q_kernel,
        grid=(N_TOK // tm,),
        in_specs=[pl.BlockSpec((tm, MLA_Q_RANK), lambda i: (i, 0)),
                  pl.BlockSpec((1, MLA_Q_RANK), lambda i: (0, 0)),
                  pl.BlockSpec((MLA_Q_RANK, hw), lambda i: (0, 0)),
                  pl.BlockSpec((1, LANES), lambda i: (0, 0)),
                  tab_spec, tab_spec, tab_spec],
        out_specs=pl.BlockSpec((tm, hw), lambda i: (i, 0)),
        out_shape=jax.ShapeDtypeStruct((N_TOK, hw), F32),
        compiler_params=_params("parallel"),
        name="mla_q",
    )(p, q_norm, w_uq_p, qn_p, *tabs)


def _mla_kv_kernel(ckv_ref, kr_ref, kvn_ref, wk_ref, wv_ref, kn_ref, c_ref, s1_ref, s2_ref,
                   k_ref, v_ref, ckvn_ref, *, norm_ckv):
    ckv = ckv_ref[...]
    if norm_ckv:
        ckv = _rms(ckv, MLA_KV_RANK) * kvn_ref[...]
    ckvn_ref[...] = ckv
    kk = _dot(ckv, wk_ref[...])
    v_ref[...] = _dot(ckv, wv_ref[...])
    kr = kr_ref[...]
    c, s1, s2 = c_ref[...], s1_ref[...], s2_ref[...]
    for h in range(MLA_HEADS):
        kh = kk[:, h * LANES:(h + 1) * LANES] + kr
        kh = _rms(kh, MLA_QK) * kn_ref[...]
        k_ref[:, h * LANES:(h + 1) * LANES] = _rope(kh, c, s1, s2, MLA_ROPE // 2)


def _mla_kv(ckv_src, ckv_blk, kr_src, kr_blk, kv_norm, wk_p, wv_p, kn_p, tabs, n_rows, norm_ckv):
    tm = 512
    hw = MLA_HEADS * LANES
    tab_spec = pl.BlockSpec((tm, LANES), lambda i: (i, 0))
    return pl.pallas_call(
        functools.partial(_mla_kv_kernel, norm_ckv=norm_ckv),
        grid=(n_rows // tm,),
        in_specs=[pl.BlockSpec((tm, LANES), lambda i: (i, ckv_blk)),
                  pl.BlockSpec((tm, LANES), lambda i: (i, kr_blk)),
                  pl.BlockSpec((1, LANES), lambda i: (0, 0)),
                  pl.BlockSpec((MLA_KV_RANK, hw), lambda i: (0, 0)),
                  pl.BlockSpec((MLA_KV_RANK, hw), lambda i: (0, 0)),
                  pl.BlockSpec((1, LANES), lambda i: (0, 0)),
                  tab_spec, tab_spec, tab_spec],
        out_specs=[pl.BlockSpec((tm, hw), lambda i: (i, 0)),
                   pl.BlockSpec((tm, hw), lambda i: (i, 0)),
                   pl.BlockSpec((tm, LANES), lambda i: (i, 0))],
        out_shape=[jax.ShapeDtypeStruct((n_rows, hw), F32),
                   jax.ShapeDtypeStruct((n_rows, hw), F32),
                   jax.ShapeDtypeStruct((n_rows, LANES), F32)],
        compiler_params=_params("parallel"),
        name="mla_kv",
    )(ckv_src, kr_src, kv_norm, wk_p, wv_p, kn_p, *tabs)


def _softmax_rows(s):
    p = jnp.exp(s - jnp.max(s, axis=-1, keepdims=True))
    return p, jnp.sum(p, axis=-1, keepdims=True)


def _mla_attn_kernel(q_ref, k_ref, v_ref, o_ref):
    scale = MLA_QK ** -0.5
    outs = []
    for h in range(2):
        sl = slice(h * LANES, (h + 1) * LANES)
        p, l = _softmax_rows(_dot_nt_bf16(q_ref[:, sl], k_ref[:, sl]) * scale)
        outs.append(_dot_bf16(p, v_ref[:, sl]) / l)
    o_ref[...] = outs[0] + pltpu.roll(outs[1], MLA_V, axis=1)


def _mla_attn(q, k, v, batch, nq, nk, q_row0, tq):
    nqb = nq // tq
    qb0 = q_row0 // tq
    return pl.pallas_call(
        _mla_attn_kernel,
        grid=(batch, MLA_HEADS // 2, nqb),
        in_specs=[pl.BlockSpec((tq, 2 * LANES), lambda b, h, i: (qb0 + b * nqb + i, h)),
                  pl.BlockSpec((nk, 2 * LANES), lambda b, h, i: (b, h)),
                  pl.BlockSpec((nk, 2 * LANES), lambda b, h, i: (b, h))],
        out_specs=pl.BlockSpec((tq, LANES), lambda b, h, i: (b * nqb + i, h)),
        out_shape=jax.ShapeDtypeStruct((batch * nq, MLA_HEADS * MLA_V), F32),
        compiler_params=_params("parallel", "parallel", "arbitrary"),
        name="mla_attn",
    )(q, k, v)


def _ret_kernel(lg_ref, q_ref, k_ref, v_ref, rg_ref, s0_ref, gn_ref, o_ref, st_ref, *, n, tq):
    b, pair, qi = pl.program_id(0), pl.program_id(1), pl.program_id(2)
    q = q_ref[...]
    k = k_ref[...] * (RET_DK ** -0.5)
    lo = _lane_lo((1, LANES))
    row = (qi * tq + lax.broadcasted_iota(jnp.int32, (tq, 1), 0)).astype(F32)
    col = lax.broadcasted_iota(jnp.int32, (1, n), 1).astype(F32)
    diff = row - col
    for h in range(2):
        lgf = lg_ref[0, 2 * pair + h]
        lgb = lg_ref[1, 2 * pair + h]
        mask = lo if h == 0 else jnp.logical_not(lo)
        qh = jnp.where(mask, q, 0.0)
        vh = v_ref[:, h * LANES:(h + 1) * LANES]
        decay = (jnp.where(diff >= 0, jnp.exp(lgf * jnp.maximum(diff, 0.0)), 0.0)
                 + jnp.where(diff <= 0, jnp.exp(lgb * jnp.maximum(-diff, 0.0)), 0.0))
        o = _dot(_dot_nt(qh, k) * decay, vh)
        o = o + _dot(qh * jnp.exp(lgf * (row + 1.0)), s0_ref[0, 0])
        o = o + _dot(qh * jnp.exp(lgb * (n - row)), s0_ref[0, 1])
        y = _rms(o, RET_DV) * gn_ref[:, h * LANES:(h + 1) * LANES]
        o_ref[:, h * LANES:(h + 1) * LANES] = _silu(rg_ref[:, h * LANES:(h + 1) * LANES]) * y

    @pl.when(qi == 0)
    def _():
        pos = lax.broadcasted_iota(jnp.int32, (n, 1), 0).astype(F32)
        for d in range(2):
            acc = None
            for h in range(2):
                lg = lg_ref[d, 2 * pair + h]
                mask = lo if h == 0 else jnp.logical_not(lo)
                expo = (n - 1.0 - pos) if d == 0 else pos
                kd = jnp.where(mask, k * jnp.exp(lg * expo), 0.0)
                term = _dot_tn(kd, v_ref[:, h * LANES:(h + 1) * LANES])
                acc = term if acc is None else acc + term
            lg_rows = jnp.where(lax.broadcasted_iota(jnp.int32, (LANES, 1), 0) < 64,
                                lg_ref[d, 2 * pair], lg_ref[d, 2 * pair + 1])
            st_ref[0, d] = acc + s0_ref[0, d] * jnp.exp(lg_rows * n)


def _retention(log_g, p, q_blk, k_blk, v_blk, g_blk, s0, gn, batch, n, row0, tq):
    nqb = n // tq
    qb0 = row0 // tq
    kb0 = row0 // n
    pairs = RET_HEADS // 2
    return pl.pallas_call(
        functools.partial(_ret_kernel, n=n, tq=tq),
        grid=(batch, pairs, nqb),
        in_specs=[pl.BlockSpec(memory_space=pltpu.SMEM),
                  pl.BlockSpec((tq, LANES), lambda b, h, i: (qb0 + b * nqb + i, q_blk + h)),
                  pl.BlockSpec((n, LANES), lambda b, h, i: (kb0 + b, k_blk + h)),
                  pl.BlockSpec((n, 2 * LANES), lambda b, h, i: (kb0 + b, v_blk // 2 + h)),
                  pl.BlockSpec((tq, 2 * LANES), lambda b, h, i: (qb0 + b * nqb + i, g_blk // 2 + h)),
                  pl.BlockSpec((1, 2, LANES, LANES), lambda b, h, i: (b, 0, h, 0)),
                  pl.BlockSpec((1, 2 * LANES), lambda b, h, i: (0, h))],
        out_specs=[pl.BlockSpec((tq, 2 * LANES), lambda b, h, i: (b * nqb + i, h)),
                   pl.BlockSpec((1, 2, LANES, LANES), lambda b, h, i: (b, 0, h, 0))],
        out_shape=[jax.ShapeDtypeStruct((batch * n, RET_HEADS * RET_DV), F32),
                   jax.ShapeDtypeStruct((batch, 2, RET_HEADS * RET_DK, RET_DV), F32)],
        compiler_params=_params("parallel", "parallel", "arbitrary"),
        name="retention",
    )(log_g, p, p, p, p, s0, gn)


def _diff_qk_kernel(q_ref, k_ref, qn_ref, kn_ref, c_ref, s1_ref, s2_ref, qo_ref, ko_ref):
    c, s1, s2 = c_ref[...], s1_ref[...], s2_ref[...]
    for src, gain, dst in ((q_ref, qn_ref, qo_ref), (k_ref, kn_ref, ko_ref)):
        for h in range(DIFF_HEADS):
            sl = slice(h * LANES, (h + 1) * LANES)
            y = src[:, sl]
            y = y * lax.rsqrt(_seg64_sum(y * y) * (1.0 / DIFF_DH) + EPS) * gain[...]
            dst[:, sl] = _rope(y, c, s1, s2, DIFF_DH // 2)


def _diff_qk(p, qn_p, kn_p, tabs):
    tm = 512
    tab_spec = pl.BlockSpec((tm, LANES), lambda i: (i, 0))
    return pl.pallas_call(
        _diff_qk_kernel,
        grid=(N_TOK // tm,),
        in_specs=[pl.BlockSpec((tm, DIFF_W), lambda i: (i, 0)),
                  pl.BlockSpec((tm, DIFF_W), lambda i: (i, 1)),
                  pl.BlockSpec((1, LANES), lambda i: (0, 0)),
                  pl.BlockSpec((1, LANES), lambda i: (0, 0)),
                  tab_spec, tab_spec, tab_spec],
        out_specs=[pl.BlockSpec((tm, DIFF_W), lambda i: (i, 0)),
                   pl.BlockSpec((tm, DIFF_W), lambda i: (i, 0))],
        out_shape=[jax.ShapeDtypeStruct((N_TOK, DIFF_W), F32)] * 2,
        compiler_params=_params("parallel"),
        name="diff_qk",
    )(p, p, qn_p, kn_p, *tabs)


def _diff_attn_kernel(lam_ref, q_ref, k_ref, v_ref, gn_ref, o_ref, *, lam_init):
    lv = lam_ref[...]
    lam = (jnp.exp(jnp.sum(lv[0:1] * lv[1:2], axis=-1, keepdims=True))
           - jnp.exp(jnp.sum(lv[2:3] * lv[3:4], axis=-1, keepdims=True)) + lam_init)
    scale = DIFF_DH ** -0.5
    q = q_ref[...]
    k = k_ref[...]
    lo = _lane_lo((1, LANES))
    kb = k.astype(BF16)
    p1, l1 = _softmax_rows(_dot_nt_bf16(jnp.where(lo, q, 0.0), kb) * scale)
    p2, l2 = _softmax_rows(_dot_nt_bf16(jnp.where(lo, 0.0, q), kb) * scale)
    w = p1 / l1 - lam * (p2 / l2)
    o = _dot_bf16(w, v_ref[...])
    o_ref[...] = _rms(o, 2 * DIFF_DH) * gn_ref[...] * (1.0 - lam_init)


def _diff_attn(lam, q, k, v, gn, batch, nq, nk, q_row0, tq, lam_init):
    nqb = nq // tq
    qb0 = q_row0 // tq
    return pl.pallas_call(
        functools.partial(_diff_attn_kernel, lam_init=lam_init),
        grid=(batch, DIFF_HEADS, nqb),
        in_specs=[pl.BlockSpec((4, DIFF_DH), lambda b, h, i: (0, 0)),
                  pl.BlockSpec((tq, LANES), lambda b, h, i: (qb0 + b * nqb + i, h)),
                  pl.BlockSpec((nk, LANES), lambda b, h, i: (b, h)),
                  pl.BlockSpec((nk, LANES), lambda b, h, i: (b, h)),
                  pl.BlockSpec((1, LANES), lambda b, h, i: (0, h))],
        out_specs=pl.BlockSpec((tq, LANES), lambda b, h, i: (b * nqb + i, h)),
        out_shape=jax.ShapeDtypeStruct((batch * nq, DIFF_W), F32),
        compiler_params=_params("parallel", "parallel", "arbitrary"),
        name="diff_attn",
    )(lam, q, k, v, gn)


def _seg64_sum_wide(x):
    return jnp.concatenate([_seg64_sum(x[:, j * LANES:(j + 1) * LANES]) for j in range(x.shape[1] // LANES)], axis=1)


def _rwkv_pre_kernel(r_ref, k_ref, lo_ref, wup_ref, aup_ref, gup_ref, w0_ref, a0_ref, kk_ref, ka_ref, rk_ref,
                     kkn_ref, g_ref, bonus_ref, w_ref, kd_ref, b_ref):
    r = r_ref[...]
    k = k_ref[...]
    lora = lo_ref[...]
    kk = k * kk_ref[...]
    kkn = kk * lax.rsqrt(_seg64_sum_wide(kk * kk) + EPS)
    kkn_ref[...] = kkn
    g_ref[...] = _dot(_sigmoid(lora[:, 2 * LANES:3 * LANES]), gup_ref[...])
    pre = w0_ref[...] + _dot(jnp.tanh(lora[:, 0:LANES]), wup_ref[...])
    w_ref[...] = jnp.exp(-jnp.exp(-_softplus(-pre) - 0.5))
    a = _sigmoid(a0_ref[...] + _dot(lora[:, LANES:2 * LANES], aup_ref[...]))
    bonus = None
    for d in range(2):
        sl = slice(d * RWKV_W, (d + 1) * RWKV_W)
        a_d = a[:, sl]
        k_d = k * (1.0 + (a_d - 1.0) * ka_ref[...])
        kd_ref[:, sl] = k_d
        b_ref[:, sl] = kkn * a_d
        t = _seg64_sum_wide(r * k_d * rk_ref[...])
        bonus = t if bonus is None else bonus + t
    bonus_ref[...] = bonus


def _rwkv_pre(p, r_blk, k_blk, lo_blk, wup_bd, aup_bd, gup, w0, a0, k_k, k_a, r_k):
    tm = 256
    w = RWKV_W
    row = lambda n: pl.BlockSpec((1, n), lambda i: (0, 0))
    full = lambda a, b: pl.BlockSpec((a, b), lambda i: (0, 0))
    return pl.pallas_call(
        _rwkv_pre_kernel,
        grid=(N_TOK // tm,),
        in_specs=[pl.BlockSpec((tm, w), lambda i: (i, r_blk)),
                  pl.BlockSpec((tm, w), lambda i: (i, k_blk)),
                  pl.BlockSpec((tm, 3 * LANES), lambda i: (i, lo_blk)),
                  full(LANES, 2 * w), full(LANES, 2 * w), full(LANES, w),
                  row(2 * w), row(2 * w), row(w), row(w), row(w)],
        out_specs=[pl.BlockSpec((tm, w), lambda i: (i, 0))] * 3 + [pl.BlockSpec((tm, 2 * w), lambda i: (i, 0))] * 3,
        out_shape=[jax.ShapeDtypeStruct((N_TOK, w), F32)] * 3 + [jax.ShapeDtypeStruct((N_TOK, 2 * w), F32)] * 3,
        compiler_params=_params("parallel"),
        name="rwkv_pre",
    )(p, p, p, wup_bd, aup_bd, gup, w0, a0, k_k, k_a, r_k)


_SCAN_CHUNK = 32


def _rwkv_step(s_ref, kk, w, kd, b, r, v_at):
    nv = s_ref.shape[1]
    chunk = min(_SCAN_CHUNK, nv)
    ys = []
    for c0 in range(0, nv, chunk):
        sa = [None, None]
        for k in range(RWKV_HS):
            term = s_ref[k, c0:c0 + chunk, :] * kk(k)
            sa[k % 2] = term if sa[k % 2] is None else sa[k % 2] + term
        sa = sa[0] + sa[1]
        vc = v_at(c0, chunk)
        y = [None, None]
        for k in range(RWKV_HS):
            s_new = s_ref[k, c0:c0 + chunk, :] * w(k) - sa * b(k) + vc * kd(k)
            s_ref[k, c0:c0 + chunk, :] = s_new
            term = s_new * r(k)
            y[k % 2] = term if y[k % 2] is None else y[k % 2] + term
        ys.append(y[0] + y[1])
    return ys[0] if len(ys) == 1 else jnp.concatenate(ys, axis=0)


_CTX_TB = 16


def _rwkv_scan_ctx_kernel(r_ref, v_ref, kk_ref, w_ref, k_ref, b_ref, y_ref, st_ref,
                          s_ref, xr, xv, xkk, xw, xk, xb, ys):
    d = pl.program_id(0)
    tb = pl.program_id(1)
    n_state = BATCH * RWKV_HEADS

    @pl.when(tb == 0)
    def _():
        s_ref[...] = jnp.zeros_like(s_ref)

    def load_t(t, carry):
        for src, dst in ((r_ref, xr), (v_ref, xv), (kk_ref, xkk), (w_ref, xw), (k_ref, xk), (b_ref, xb)):
            dst[t] = src[:, t].reshape(n_state, RWKV_HS).T
        return carry

    lax.fori_loop(0, _CTX_TB, load_t, 0)

    def step(i, carry):
        t = jnp.where(d == 0, i, _CTX_TB - 1 - i)
        row = lambda ref: (lambda k: ref[t, pl.ds(k, 1), :])
        ys[t] = _rwkv_step(s_ref, row(xkk), row(xw), row(xk), row(xb), row(xr),
                           lambda c0, n: xv[t, pl.ds(c0, n), :])
        return carry

    lax.fori_loop(0, _CTX_TB, step, 0)

    def store_t(t, carry):
        y_ref[:, t] = ys[t].T.reshape(BATCH, RWKV_HEADS, RWKV_HS)
        return carry

    lax.fori_loop(0, _CTX_TB, store_t, 0)

    @pl.when(tb == pl.num_programs(1) - 1)
    def _():
        st_ref[0] = s_ref[...]


def _rwkv_scan_ctx(p4, r_blk, v_blk, kkn4, w4, kd4, b4):
    nt = SEQ // _CTX_TB
    hs = RWKV_HS
    tblk = lambda d, tb: jnp.where(d == 0, tb, nt - 1 - tb)
    blk = (BATCH, _CTX_TB, RWKV_HEADS, hs)
    shared = lambda hb: pl.BlockSpec(blk, lambda d, tb: (0, tblk(d, tb), hb, 0))
    per_dir = pl.BlockSpec(blk, lambda d, tb: (0, tblk(d, tb), d, 0))
    x_scr = pltpu.VMEM((_CTX_TB, hs, LANES), F32)
    return pl.pallas_call(
        _rwkv_scan_ctx_kernel,
        grid=(2, nt),
        in_specs=[shared(r_blk), shared(v_blk), shared(0), per_dir, per_dir, per_dir],
        out_specs=[per_dir, pl.BlockSpec((1, hs, hs, LANES), lambda d, tb: (d, 0, 0, 0))],
        out_shape=[jax.ShapeDtypeStruct((BATCH, SEQ, 2 * RWKV_HEADS, hs), F32),
                   jax.ShapeDtypeStruct((2, hs, hs, LANES), F32)],
        scratch_shapes=[pltpu.VMEM((hs, hs, LANES), F32)] + [x_scr] * 7,
        compiler_params=_params("parallel", "arbitrary"),
        name="rwkv_scan_ctx",
    )(p4, p4, kkn4, w4, kd4, b4)


_LAT_TB = 32
_LAT_VSPLIT = 4


def _rwkv_scan_lat_kernel(kt_ref, v_ref, s0_ref, y_ref, s_ref):
    @pl.when(pl.program_id(0) == 0)
    def _():
        s_ref[...] = s0_ref[...]

    def step(t, carry):
        row = lambda a: (lambda k: kt_ref[a, t, pl.ds(k, 1), :])
        y_ref[t] = _rwkv_step(s_ref, row(0), row(1), row(2), row(3), row(4),
                              lambda c0, n: v_ref[t, pl.ds(c0, n), :])
        return carry

    lax.fori_loop(0, _LAT_TB, step, 0)


def _rwkv_scan_lat(kt, v, s0):
    hs = RWKV_HS
    nv = hs // _LAT_VSPLIT
    return pl.pallas_call(
        _rwkv_scan_lat_kernel,
        grid=(DEC_SEQ // _LAT_TB,),
        in_specs=[pl.BlockSpec((5, _LAT_TB, hs, LANES), lambda tb: (0, tb, 0, 0)),
                  pl.BlockSpec((_LAT_TB, nv, LANES), lambda tb: (tb, 0, 0)),
                  pl.BlockSpec((hs, nv, LANES), lambda tb: (0, 0, 0))],
        out_specs=pl.BlockSpec((_LAT_TB, nv, LANES), lambda tb: (tb, 0, 0)),
        out_shape=jax.ShapeDtypeStruct((DEC_SEQ, nv, LANES), F32),
        scratch_shapes=[pltpu.VMEM((hs, nv, LANES), F32)],
        compiler_params=_params("arbitrary"),
        name="rwkv_scan_lat",
    )(kt, v, s0)


def _rwkv_post_kernel(yf_ref, yb_ref, bonus_ref, v_ref, g_ref, gn_ref, o_ref):
    y = yf_ref[...] + yb_ref[...]
    y = y * lax.rsqrt(_seg64_sum_wide(y * y) * (1.0 / RWKV_HS) + EPS) * gn_ref[...]
    o_ref[...] = (y + bonus_ref[...] * v_ref[...]) * g_ref[...]


def _rwkv_post(y, bonus, p, v_blk, g, gn):
    tm = 512
    w = RWKV_W
    spec = pl.BlockSpec((tm, w), lambda i: (i, 0))
    return pl.pallas_call(
        _rwkv_post_kernel,
        grid=(N_TOK // tm,),
        in_specs=[spec, pl.BlockSpec((tm, w), lambda i: (i, 1)), spec,
                  pl.BlockSpec((tm, w), lambda i: (i, v_blk)), spec,
                  pl.BlockSpec((1, w), lambda i: (0, 0))],
        out_specs=spec,
        out_shape=jax.ShapeDtypeStruct((N_TOK, w), F32),
        compiler_params=_params("parallel"),
        name="rwkv_post",
    )(y, y, bonus, p, g, gn)


_LAT_STATES = 2 * DEC_BATCH * RWKV_HEADS


def _lat_states(fwd, bwd):
    shape = (DEC_BATCH, DEC_SEQ, RWKV_HEADS, RWKV_HS)
    st = jnp.stack([fwd[N_CTX:].reshape(shape), bwd[N_CTX:].reshape(shape)[:, ::-1]], 0)
    return jnp.transpose(st, (2, 0, 1, 3, 4)).reshape(DEC_SEQ, _LAT_STATES, RWKV_HS)


def _lat_value_layout(x):
    lead = x.shape[:-2]
    nv = RWKV_HS // _LAT_VSPLIT
    x = x.reshape(lead + (_LAT_STATES, _LAT_VSPLIT, nv))
    n = len(lead)
    return jnp.transpose(x, tuple(range(n)) + (n + 2, n + 1, n)).reshape(lead + (nv, LANES))


def _lat_from_value_layout(y):
    nv = RWKV_HS // _LAT_VSPLIT
    y = jnp.transpose(y.reshape(DEC_SEQ, nv, _LAT_VSPLIT, _LAT_STATES), (0, 3, 2, 1))
    y = y.reshape(DEC_SEQ, 2, DEC_BATCH, RWKV_HEADS, RWKV_HS)
    y = jnp.transpose(y, (1, 2, 0, 3, 4)).reshape(2, DEC_BATCH, DEC_SEQ, RWKV_W)
    return y[0].reshape(N_LAT, RWKV_W), y[1, :, ::-1].reshape(N_LAT, RWKV_W)


def _even_layer(x, mod, g_mix, w_in, q_norm, kv_norm, w_uq, w_ukv, qn, kn, ret_decay, ret_gn,
                cache_ckv, cache_krope, state_ret, tabs_m):
    perm_r = _deinterleave(MLA_ROPE)
    cq, ckv, krope, rq, rk, rv, rg = jnp.split(w_in, np.cumsum(
        (MLA_Q_RANK, MLA_KV_RANK, MLA_ROPE, RET_HEADS * RET_DK, RET_HEADS * RET_DK, RET_HEADS * RET_DV))[:].tolist(),
        axis=1)
    z = lambda n: jnp.zeros((D_MODEL, n), F32)
    w_p = jnp.concatenate([cq, ckv, z(MLA_NOPE), krope[:, perm_r], z(LANES - MLA_QK), rq, rk, rv, rg], axis=1)
    p = _inproj(x, g_mix, mod, w_p, jnp.zeros((1, w_p.shape[1]), F32), 512, False)
    CKV_BLK, KR_BLK, RQ_BLK, RK_BLK, RV_BLK, RG_BLK = 2, 3, 4, 6, 8, 12

    def head_pad(w, n_head, d_head, cols):
        w = w.reshape(w.shape[0], n_head, d_head)[:, :, cols]
        return jnp.pad(w, ((0, 0), (0, 0), (0, LANES - len(cols)))).reshape(w.shape[0], n_head * LANES)

    qk_cols = np.concatenate([np.arange(MLA_NOPE), MLA_NOPE + perm_r])
    w_uq_p = head_pad(w_uq, MLA_HEADS, MLA_QK, qk_cols)
    wk_p = head_pad(w_ukv, MLA_HEADS, MLA_NOPE + MLA_V, np.arange(MLA_NOPE))
    wv_p = head_pad(w_ukv, MLA_HEADS, MLA_NOPE + MLA_V, MLA_NOPE + np.arange(MLA_V))
    qn_p = jnp.pad(qn[qk_cols], (0, LANES - MLA_QK))[None]
    kn_p = jnp.pad(kn[qk_cols], (0, LANES - MLA_QK))[None]

    q = _mla_q(p, q_norm[None], w_uq_p, qn_p, tabs_m)
    k, v, ckvn = _mla_kv(p, CKV_BLK, p, KR_BLK, kv_norm[None], wk_p, wv_p, kn_p, tabs_m, N_TOK, True)

    n_c = DEC_BATCH * PAST_LEN
    kr_c = jnp.pad(cache_krope.reshape(n_c, MLA_ROPE)[:, perm_r], ((0, 0), (MLA_NOPE, LANES - MLA_QK)))
    ident = (jnp.ones((n_c, LANES), F32), jnp.zeros((n_c, LANES), F32), jnp.zeros((n_c, LANES), F32))
    k_c, v_c, _ = _mla_kv(cache_ckv.reshape(n_c, MLA_KV_RANK), 0, kr_c, 0, kv_norm[None], wk_p, wv_p, kn_p,
                          ident, n_c, False)

    hw = MLA_HEADS * LANES

    def with_cache(own, cache):
        own = own[N_CTX:].reshape(DEC_BATCH, DEC_SEQ, hw)
        return jnp.concatenate([own, cache.reshape(DEC_BATCH, PAST_LEN, hw)], 1).reshape(-1, hw)

    o_ctx = _mla_attn(q, k, v, BATCH, SEQ, SEQ, 0, SEQ)
    o_lat = _mla_attn(q, with_cache(k, k_c), with_cache(v, v_c), DEC_BATCH, DEC_SEQ, DEC_SEQ + PAST_LEN, N_CTX, 256)

    log_g = -_softplus(-ret_decay)
    gn = ret_gn[None]
    s0_ctx = jnp.zeros((BATCH, 2, RET_HEADS * RET_DK, RET_DV), F32)
    r_ctx, st_ctx = _retention(log_g, p, RQ_BLK, RK_BLK, RV_BLK, RG_BLK, s0_ctx, gn, BATCH, SEQ, 0, SEQ)
    s0_lat = state_ret.reshape(DEC_BATCH, 2, RET_HEADS * RET_DK, RET_DV)
    r_lat, _ = _retention(log_g, p, RQ_BLK, RK_BLK, RV_BLK, RG_BLK, s0_lat, gn, DEC_BATCH, DEC_SEQ, N_CTX, 256)

    mix = [jnp.concatenate([o_ctx, o_lat], 0), jnp.concatenate([r_ctx, r_lat], 0)]
    new_ckv = ckvn[:N_CTX].reshape(BATCH, SEQ, MLA_KV_RANK)
    new_krope = p[:N_CTX, KR_BLK * LANES + MLA_NOPE:KR_BLK * LANES + MLA_QK][:, np.argsort(perm_r)]
    new_krope = new_krope.reshape(BATCH, SEQ, MLA_ROPE)
    new_ret = st_ctx.reshape(BATCH, 2, RET_HEADS, RET_DK, RET_DV)
    return mix, new_ckv, new_krope, new_ret


def _odd_layer(x, mod, g_mix, w_in, qn, kn, lam, diff_gn, mu, w0, w_up, a0, a_up, g_up, k_k, k_a, r_k, gn,
               cache_k, cache_v, state_rwkv, tabs_d, lam_init):
    perm = _deinterleave(DIFF_DH)
    qk_perm = (np.arange(2 * DIFF_W).reshape(-1, DIFF_DH)[:, perm]).reshape(-1)
    w_p = jnp.concatenate([w_in[:, qk_perm], w_in[:, 2 * DIFF_W:]], axis=1)
    n_in = w_p.shape[1]
    mu_full = jnp.concatenate([jnp.zeros((3 * DIFF_W,), F32), mu])[None]
    p = _inproj(x, g_mix, mod, w_p, mu_full, 384, True)
    DV_BLK, R_BLK, K_BLK, V_BLK = 2, 3, 4, 5
    LO_BLK = (6 * RWKV_W) // (3 * LANES)

    qn_p = jnp.tile(qn[perm], 2)[None]
    kn_p = jnp.tile(kn[perm], 2)[None]
    q, k = _diff_qk(p, qn_p, kn_p, tabs_d)
    v = p[:, DV_BLK * DIFF_W:(DV_BLK + 1) * DIFF_W]

    n_c = DEC_BATCH * PAST_LEN
    k_c = cache_k.reshape(n_c, DIFF_HEADS * 2, DIFF_DH)[:, :, perm].reshape(DEC_BATCH, PAST_LEN, DIFF_W)
    v_c = cache_v.reshape(DEC_BATCH, PAST_LEN, DIFF_W)

    def with_cache(own, cache):
        return jnp.concatenate([own[N_CTX:].reshape(DEC_BATCH, DEC_SEQ, DIFF_W), cache], 1).reshape(-1, DIFF_W)

    dgn = diff_gn[None]
    o_ctx = _diff_attn(lam, q, k, v, dgn, BATCH, SEQ, SEQ, 0, SEQ, lam_init)
    o_lat = _diff_attn(lam, q, with_cache(k, k_c), with_cache(v, v_c), dgn, DEC_BATCH, DEC_SEQ, DEC_SEQ + PAST_LEN,
                       N_CTX, 256, lam_init)

    zero = jnp.zeros((RWKV_W_LORA, RWKV_W), F32)
    wup_bd = jnp.concatenate([jnp.concatenate([w_up[0], zero], 1), jnp.concatenate([zero, w_up[1]], 1)], 0)
    aup_bd = jnp.concatenate([jnp.concatenate([a_up[0], zero], 1), jnp.concatenate([zero, a_up[1]], 1)], 0)
    kkn, g, bonus, w, kd, b = _rwkv_pre(p, R_BLK, K_BLK, LO_BLK, wup_bd, aup_bd, g_up, w0.reshape(1, -1),
                                         a0.reshape(1, -1), k_k[None], k_a[None], r_k.reshape(1, -1))

    W = RWKV_W
    n_seq = N_TOK // SEQ
    view4 = lambda a: a.reshape(n_seq, SEQ, a.shape[1] // RWKV_HS, RWKV_HS)
    y_ctx, st_ctx = _rwkv_scan_ctx(view4(p), R_BLK, V_BLK, view4(kkn), view4(w), view4(kd), view4(b))

    r = p[:, R_BLK * W:(R_BLK + 1) * W]
    vv = p[:, V_BLK * W:(V_BLK + 1) * W]
    kt = jnp.stack([_lat_states(kkn, kkn), _lat_states(w[:, :W], w[:, W:]), _lat_states(kd[:, :W], kd[:, W:]),
                    _lat_states(b[:, :W], b[:, W:]), _lat_states(r, r)], 0)
    kt = jnp.tile(jnp.swapaxes(kt, 2, 3), (1, 1, 1, _LAT_VSPLIT))
    v_lat = _lat_value_layout(_lat_states(vv, vv))
    s0_lat = jnp.transpose(state_rwkv, (4, 1, 0, 2, 3)).reshape(RWKV_HS, _LAT_STATES, RWKV_HS)
    y_lat = _rwkv_scan_lat(kt, v_lat, _lat_value_layout(s0_lat))
    y = jnp.concatenate([y_ctx.reshape(N_CTX, 2 * W), jnp.concatenate(_lat_from_value_layout(y_lat), 1)], 0)
    rw_o = _rwkv_post(y, bonus, p, V_BLK, g, gn[None])

    mix = [jnp.concatenate([o_ctx, o_lat], 0), rw_o]
    inv = np.argsort(perm)
    new_dk = k[:N_CTX].reshape(BATCH, SEQ, DIFF_HEADS, 2, DIFF_DH)[..., inv]
    new_dv = v[:N_CTX].reshape(BATCH, SEQ, DIFF_HEADS, 2 * DIFF_DH)
    new_rwkv = jnp.transpose(st_ctx.reshape(2, RWKV_HS, RWKV_HS, BATCH, RWKV_HEADS), (3, 0, 4, 2, 1))
    return mix, new_dk, new_dv, new_rwkv


def kernel(x_prompt, x_sample, cache_mla_ckv, cache_mla_krope, state_ret, cache_diff_k, cache_diff_v, state_rwkv,
           c, c_ctx, ada_w, ada_b, norm_mix_g, norm_ffn_g, w_out, ffn_up, ffn_conv_w, ffn_conv_b, ffn_down,
           a_w_in, mla_q_norm, mla_kv_norm, mla_w_uq, mla_w_ukv, mla_qn, mla_kn, ret_decay, ret_gn,
           b_w_in, diff_qn, diff_kn, diff_lam, diff_gn, rwkv_mu, rwkv_w0, rwkv_w_up, rwkv_a0, rwkv_a_up,
           rwkv_g_up, rwkv_k_k, rwkv_k_a, rwkv_r_k, rwkv_gn):
    x = jnp.concatenate([x_prompt.reshape(N_CTX, D_MODEL), x_sample.reshape(N_LAT, D_MODEL)], 0)
    cond8 = jnp.pad(jnp.concatenate([c_ctx[None], c], 0), ((0, 8 - N_GROUPS), (0, 0)))
    mod = _modulation(cond8, ada_w, ada_b)

    cos_m, sin_m = _axial_angles(MLA_ROPE)
    cos_d, sin_d = _axial_angles(DIFF_DH)
    tabs_m = _rope_tables(cos_m, sin_m, MLA_NOPE, MLA_ROPE // 2, (0,))
    tabs_d = _rope_tables(cos_d, sin_d, 0, DIFF_DH // 2, (0, DIFF_DH))

    outs = {}
    for l in range(DEPTH):
        j = l // 2
        g_mix = norm_mix_g[l][None]
        if l % 2 == 0:
            mix, outs["ckv"], outs["krope"], outs["ret"] = _even_layer(
                x, mod[l], g_mix, a_w_in[j], mla_q_norm[j], mla_kv_norm[j], mla_w_uq[j], mla_w_ukv[j], mla_qn[j],
                mla_kn[j], ret_decay[j], ret_gn[j], cache_mla_ckv[:, j], cache_mla_krope[:, j], state_ret[:, j],
                tabs_m)
        else:
            lam_init = 0.8 - 0.6 * math.exp(-0.3 * l)
            mix, outs["dk"], outs["dv"], outs["rwkv"] = _odd_layer(
                x, mod[l], g_mix, b_w_in[j], diff_qn[j], diff_kn[j], diff_lam[j], diff_gn[j], rwkv_mu[j],
                rwkv_w0[j], rwkv_w_up[j], rwkv_a0[j], rwkv_a_up[j], rwkv_g_up[j], rwkv_k_k[j], rwkv_k_a[j],
                rwkv_r_k[j], rwkv_gn[j], cache_diff_k[:, j], cache_diff_v[:, j], state_rwkv[:, j], tabs_d, lam_init)
        x = _resid_proj(mix, w_out[l], x, mod[l], 2)
        act = _ffn_up(x, norm_ffn_g[l][None], mod[l], ffn_up[l], ffn_conv_w[l], ffn_conv_b[l])
        x = _resid_proj([act], ffn_down[l], x, mod[l], 5)

    y_prompt = x[:N_CTX].reshape(BATCH, SEQ, D_MODEL)
    y_sample = x[N_CTX:].reshape(DEC_BATCH, DEC_SEQ, D_MODEL)
    return (y_prompt, y_sample, outs["ckv"][:, None], outs["krope"][:, None], outs["ret"][:, None],
            outs["dk"][:, None], outs["dv"][:, None], outs["rwkv"][:, None])
```

```python
import functools
import math

import numpy as np
import jax
import jax.numpy as jnp
from jax import lax
from jax.experimental import pallas as pl
from jax.experimental.pallas import tpu as pltpu

D_MODEL = 1024
BATCH = 16
SEQ = 256
DEPTH = 2
DEC_BATCH = 2
DEC_SEQ = 1024
PAST_LEN = 512
GRID_W = 64
EPS = 1e-6
ROPE_BASE = 10000.0

MLA_HEADS = 8
MLA_Q_RANK = 256
MLA_KV_RANK = 128
MLA_NOPE = 64
MLA_ROPE = 32
MLA_V = 64
MLA_QK = MLA_NOPE + MLA_ROPE
RET_HEADS = 4
RET_DK = 64
RET_DV = 128
DIFF_HEADS = 4
DIFF_DH = 64
DIFF_W = DIFF_HEADS * 2 * DIFF_DH
RWKV_HEADS = 8
RWKV_HS = 64
RWKV_W = RWKV_HEADS * RWKV_HS
RWKV_W_LORA = 64
RWKV_A_LORA = 64
RWKV_G_LORA = 128
D_FF = 2816

N_CTX = BATCH * SEQ
N_LAT = DEC_BATCH * DEC_SEQ
N_TOK = N_CTX + N_LAT
N_GROUPS = 1 + DEC_BATCH

LANES = 128
VMEM_LIMIT = 56 * 1024 * 1024

_PREC = lax.Precision.HIGHEST
F32 = jnp.float32


def _dot(a, b):
    return jnp.dot(a, b, precision=_PREC, preferred_element_type=F32)


def _dot_nt(a, b):
    return lax.dot_general(a, b, (((1,), (1,)), ((), ())), precision=_PREC, preferred_element_type=F32)


def _dot_tn(a, b):
    return lax.dot_general(a, b, (((0,), (0,)), ((), ())), precision=_PREC, preferred_element_type=F32)


BF16 = jnp.bfloat16


_PARTS = 1


def _split_weight(w):
    k, n = w.shape
    tk = 512 if k % 512 == 0 else 256
    tn = 512 if n % 512 == 0 else 384 if n % 384 == 0 else 256
    out = pl.pallas_call(
        _split_weight_kernel,
        grid=(k // tk, n // tn),
        in_specs=[pl.BlockSpec((tk, tn), lambda i, j: (i, j))],
        out_specs=pl.BlockSpec((_PARTS, tk, tn), lambda i, j: (0, i, j)),
        out_shape=jax.ShapeDtypeStruct((_PARTS, k, n), BF16),
        compiler_params=_params("parallel", "parallel"),
        name="split_weight",
    )(w)
    return out.reshape(_PARTS * k, n)


def _split_weight_kernel(w_ref, o_ref):
    w = w_ref[...]
    hi = w.astype(BF16)
    o_ref[0] = hi
    if _PARTS == 3:
        o_ref[1] = hi
        o_ref[2] = (w - hi.astype(F32)).astype(BF16)


def _store_split(dst_ref, a, k_total, k0=0):
    k = a.shape[1]
    hi = a.astype(BF16)
    dst_ref[:, k0:k0 + k] = hi
    if _PARTS == 3:
        dst_ref[:, k_total + k0:k_total + k0 + k] = (a - hi.astype(F32)).astype(BF16)
        dst_ref[:, 2 * k_total + k0:2 * k_total + k0 + k] = hi


def _dot_split(a3, w3):
    return jnp.dot(a3, w3, preferred_element_type=F32)


def _dot_bf16(a, b):
    return jnp.dot(a.astype(BF16), b.astype(BF16), preferred_element_type=F32)


def _dot_nt_bf16(a, b):
    return lax.dot_general(a.astype(BF16), b.astype(BF16), (((1,), (1,)), ((), ())), preferred_element_type=F32)


def _params(*sem):
    return pltpu.CompilerParams(dimension_semantics=sem, vmem_limit_bytes=VMEM_LIMIT)


def _sigmoid(x):
    return 1.0 / (1.0 + jnp.exp(-x))


def _silu(x):
    return x * _sigmoid(x)


def _softplus(x):
    return jnp.maximum(x, 0.0) + jnp.log(1.0 + jnp.exp(-jnp.abs(x)))


def _rms(x, n):
    return x * lax.rsqrt(jnp.sum(x * x, axis=-1, keepdims=True) * (1.0 / n) + EPS)


def _lane_lo(shape):
    return lax.broadcasted_iota(jnp.int32, shape, len(shape) - 1) < 64


def _seg64_sum(x):
    lo = _lane_lo(x.shape)
    s_lo = jnp.sum(jnp.where(lo, x, 0.0), axis=-1, keepdims=True)
    s_hi = jnp.sum(jnp.where(lo, 0.0, x), axis=-1, keepdims=True)
    return jnp.where(lo, s_lo, s_hi)


def _seq_neighbours(p, tile, tile_rows):
    seq_mask = jnp.where(tile * tile_rows < N_CTX, SEQ - 1, DEC_SEQ - 1)
    pos = lax.broadcasted_iota(jnp.int32, (tile_rows, 1), 0) & seq_mask
    prev = jnp.where(pos == 0, 0.0, pltpu.roll(p, 1, axis=0))
    nxt = jnp.where(pos == seq_mask, 0.0, pltpu.roll(p, tile_rows - 1, axis=0))
    return prev, nxt


def _group_of_tile(i, tile_rows):
    row = i * tile_rows
    return jnp.where(row < N_CTX, 0, 1 + (row - N_CTX) // DEC_SEQ)


def _modulation_kernel(c_ref, w_ref, b_ref, o_ref):
    o_ref[0] = _dot(_silu(c_ref[...]), w_ref[0]) + b_ref[0]


def _modulation(cond8, ada_w, ada_b):
    tn = 512
    n = 6 * D_MODEL
    out = pl.pallas_call(
        _modulation_kernel,
        grid=(DEPTH, n // tn),
        in_specs=[pl.BlockSpec((8, D_MODEL), lambda l, j: (0, 0)),
                  pl.BlockSpec((1, D_MODEL, tn), lambda l, j: (l, 0, j)),
                  pl.BlockSpec((1, 1, tn), lambda l, j: (l, 0, j))],
        out_specs=pl.BlockSpec((1, 8, tn), lambda l, j: (l, 0, j)),
        out_shape=jax.ShapeDtypeStruct((DEPTH, 8, n), F32),
        compiler_params=_params("parallel", "parallel"),
        name="modulation",
    )(cond8, ada_w, ada_b.reshape(DEPTH, 1, n))
    m = out[:, :N_GROUPS].reshape(DEPTH, N_GROUPS, 6, D_MODEL)
    return jnp.pad(m, ((0, 0), (0, 0), (0, 2), (0, 0)))


_TM_SEQ = 1024


def _norm_mod(x, g, mod, off):
    return _rms(x, D_MODEL) * g * (1.0 + mod[off + 1:off + 2, :]) + mod[off:off + 1, :]


def _inproj_kernel(x_ref, g_ref, mod_ref, w_ref, mu_ref, o_ref, h_ref, *, shift):
    i = pl.program_id(0)

    @pl.when(pl.program_id(1) == 0)
    def _():
        _store_split(h_ref, _norm_mod(x_ref[...], g_ref[...], mod_ref[0], 0), D_MODEL)

    p = _dot_split(h_ref[...], w_ref[...])
    if shift:
        prev, nxt = _seq_neighbours(p, i, _TM_SEQ)
        p = p + (0.5 * (prev + nxt) - p) * mu_ref[...]
    o_ref[...] = p


def _inproj(x, g, mod, w, mu, tn, shift):
    n = w.shape[1]
    tm = _TM_SEQ
    return pl.pallas_call(
        functools.partial(_inproj_kernel, shift=shift),
        grid=(N_TOK // tm, n // tn),
        in_specs=[pl.BlockSpec((tm, D_MODEL), lambda i, j: (i, 0)),
                  pl.BlockSpec((1, D_MODEL), lambda i, j: (0, 0)),
                  pl.BlockSpec((1, 8, D_MODEL), lambda i, j: (_group_of_tile(i, tm), 0, 0)),
                  pl.BlockSpec((_PARTS * D_MODEL, tn), lambda i, j: (0, j)),
                  pl.BlockSpec((1, tn), lambda i, j: (0, j))],
        out_specs=pl.BlockSpec((tm, tn), lambda i, j: (i, j)),
        out_shape=jax.ShapeDtypeStruct((N_TOK, n), F32),
        scratch_shapes=[pltpu.VMEM((tm, _PARTS * D_MODEL), BF16)],
        compiler_params=_params("parallel", "arbitrary"),
        name="inproj_shift" if shift else "inproj",
    )(x, g, mod, _split_weight(w), mu)


def _resid_kernel(*refs, gate_row, n_act):
    a_refs = refs[:n_act]
    w_ref, x_ref, mod_ref, o_ref, a3_ref = refs[n_act:]
    k_total = a3_ref.shape[1] // _PARTS

    @pl.when(pl.program_id(1) == 0)
    def _():
        k0 = 0
        for a_ref in a_refs:
            _store_split(a3_ref, a_ref[...], k_total, k0)
            k0 += a_ref.shape[1]

    o_ref[...] = x_ref[...] + mod_ref[0, gate_row:gate_row + 1, :] * _dot_split(a3_ref[...], w_ref[...])


def _resid_proj(acts, w, x, mod, gate_row):
    tm, tn = 512, 256
    k = sum(a.shape[1] for a in acts)
    return pl.pallas_call(
        functools.partial(_resid_kernel, gate_row=gate_row, n_act=len(acts)),
        grid=(N_TOK // tm, D_MODEL // tn),
        in_specs=[pl.BlockSpec((tm, a.shape[1]), lambda i, j: (i, 0)) for a in acts]
        + [pl.BlockSpec((_PARTS * k, tn), lambda i, j: (0, j)),
           pl.BlockSpec((tm, tn), lambda i, j: (i, j)),
           pl.BlockSpec((1, 8, tn), lambda i, j: (_group_of_tile(i, tm), 0, j))],
        out_specs=pl.BlockSpec((tm, tn), lambda i, j: (i, j)),
        out_shape=jax.ShapeDtypeStruct((N_TOK, D_MODEL), F32),
        scratch_shapes=[pltpu.VMEM((tm, _PARTS * k), BF16)],
        compiler_params=_params("parallel", "arbitrary"),
        name="resid_proj",
    )(*acts, _split_weight(w), x, mod)


def _ffn_up_kernel(x_ref, g_ref, mod_ref, wa_ref, wb_ref, cwa_ref, cwb_ref, cba_ref, cbb_ref, o_ref, h_ref):
    i = pl.program_id(0)

    @pl.when(pl.program_id(1) == 0)
    def _():
        _store_split(h_ref, _norm_mod(x_ref[...], g_ref[...], mod_ref[0], 3), D_MODEL)

    h = h_ref[...]

    def conv(w_ref, cw_ref, cb_ref):
        u = _dot_split(h, w_ref[...])
        prev, nxt = _seq_neighbours(u, i, _TM_SEQ)
        return prev * cw_ref[0:1, :] + u * cw_ref[1:2, :] + nxt * cw_ref[2:3, :] + cb_ref[...]

    o_ref[...] = _silu(conv(wa_ref, cwa_ref, cba_ref)) * conv(wb_ref, cwb_ref, cbb_ref)


def _ffn_up(x, g, mod, up, cw, cb):
    tm, tn = _TM_SEQ, 256
    nb = D_FF // tn
    cb = cb.reshape(1, 2 * D_FF)
    up3 = _split_weight(up)
    return pl.pallas_call(
        _ffn_up_kernel,
        grid=(N_TOK // tm, nb),
        in_specs=[pl.BlockSpec((tm, D_MODEL), lambda i, j: (i, 0)),
                  pl.BlockSpec((1, D_MODEL), lambda i, j: (0, 0)),
                  pl.BlockSpec((1, 8, D_MODEL), lambda i, j: (_group_of_tile(i, tm), 0, 0)),
                  pl.BlockSpec((_PARTS * D_MODEL, tn), lambda i, j: (0, j)),
                  pl.BlockSpec((_PARTS * D_MODEL, tn), lambda i, j: (0, j + nb)),
                  pl.BlockSpec((3, tn), lambda i, j: (0, j)),
                  pl.BlockSpec((3, tn), lambda i, j: (0, j + nb)),
                  pl.BlockSpec((1, tn), lambda i, j: (0, j)),
                  pl.BlockSpec((1, tn), lambda i, j: (0, j + nb))],
        out_specs=pl.BlockSpec((tm, tn), lambda i, j: (i, j)),
        out_shape=jax.ShapeDtypeStruct((N_TOK, D_FF), F32),
        scratch_shapes=[pltpu.VMEM((tm, _PARTS * D_MODEL), BF16)],
        compiler_params=_params("parallel", "arbitrary"),
        name="ffn_up",
    )(x, g, mod, up3, up3, cw, cw, cb, cb)


def _rope(y, c, s1, s2, half):
    return y * c + pltpu.roll(y, half, axis=1) * s1 + pltpu.roll(y, LANES - half, axis=1) * s2


def _rope_tables(cos, sin, first, n_pairs, groups):
    n = cos.shape[0]
    c = jnp.ones((n, LANES), F32)
    s1 = jnp.zeros((n, LANES), F32)
    s2 = jnp.zeros((n, LANES), F32)
    for g0 in groups:
        a = g0 + first
        c = c.at[:, a:a + n_pairs].set(cos).at[:, a + n_pairs:a + 2 * n_pairs].set(cos)
        s1 = s1.at[:, a + n_pairs:a + 2 * n_pairs].set(sin)
        s2 = s2.at[:, a:a + n_pairs].set(-sin)

    def all_rows(t, ident):
        ctx = jnp.broadcast_to(ident, (N_CTX, LANES))
        return jnp.concatenate([ctx] + [t] * DEC_BATCH, axis=0)

    return (all_rows(c, jnp.ones((1, LANES), F32)), all_rows(s1, jnp.zeros((1, LANES), F32)),
            all_rows(s2, jnp.zeros((1, LANES), F32)))


def _axial_angles(rot_dim):
    t = np.arange(DEC_SEQ)
    row, col = t // GRID_W, t % GRID_W
    n_freq = rot_dim // 4
    inv = jnp.asarray(ROPE_BASE, F32) ** (-jnp.arange(n_freq, dtype=F32) / n_freq)
    ang = jnp.concatenate([jnp.asarray(row, F32)[:, None] * inv, jnp.asarray(col, F32)[:, None] * inv], -1)
    return jnp.cos(ang), jnp.sin(ang)


def _deinterleave(n):
    return np.concatenate([np.arange(0, n, 2), np.arange(1, n, 2)])


def _mla_q_kernel(cq_ref, qnorm_ref, w_ref, qn_ref, c_ref, s1_ref, s2_ref, o_ref):
    xn = _rms(cq_ref[...], MLA_Q_RANK) * qnorm_ref[...]
    y = _dot(xn, w_ref[...])
    c, s1, s2 = c_ref[...], s1_ref[...], s2_ref[...]
    for h in range(MLA_HEADS):
        yh = y[:, h * LANES:(h + 1) * LANES]
        yh = _rms(yh, MLA_QK) * qn_ref[...]
        o_ref[:, h * LANES:(h + 1) * LANES] = _rope(yh, c, s1, s2, MLA_ROPE // 2)


def _mla_q(p, q_norm, w_uq_p, qn_p, tabs):
    tm = 512
    hw = MLA_HEADS * LANES
    tab_spec = pl.BlockSpec((tm, LANES), lambda i: (i, 0))
    return pl.pallas_call(
        _mla_q_kernel,
        grid=(N_TOK // tm,),
        in_specs=[pl.BlockSpec((tm, MLA_Q_RANK), lambda i: (i, 0)),
                  pl.BlockSpec((1, MLA_Q_RANK), lambda i: (0, 0)),
                  pl.BlockSpec((MLA_Q_RANK, hw), lambda i: (0, 0)),
                  pl.BlockSpec((1, LANES), lambda i: (0, 0)),
                  tab_spec, tab_spec, tab_spec],
        out_specs=pl.BlockSpec((tm, hw), lambda i: (i, 0)),
        out_shape=jax.ShapeDtypeStruct((N_TOK, hw), F32),
        compiler_params=_params("parallel"),
        name="mla_q",
    )(p, q_norm, w_uq_p, qn_p, *tabs)


def _mla_kv_kernel(ckv_ref, kr_ref, kvn_ref, wk_ref, wv_ref, kn_ref, c_ref, s1_ref, s2_ref,
                   k_ref, v_ref, ckvn_ref, *, norm_ckv):
    ckv = ckv_ref[...]
    if norm_ckv:
        ckv = _rms(ckv, MLA_KV_RANK) * kvn_ref[...]
    ckvn_ref[...] = ckv
    kk = _dot(ckv, wk_ref[...])
    v_ref[...] = _dot(ckv, wv_ref[...])
    kr = kr_ref[...]
    c, s1, s2 = c_ref[...], s1_ref[...], s2_ref[...]
    for h in range(MLA_HEADS):
        kh = kk[:, h * LANES:(h + 1) * LANES] + kr
        kh = _rms(kh, MLA_QK) * kn_ref[...]
        k_ref[:, h * LANES:(h + 1) * LANES] = _rope(kh, c, s1, s2, MLA_ROPE // 2)


def _mla_kv(ckv_src, ckv_blk, kr_src, kr_blk, kv_norm, wk_p, wv_p, kn_p, tabs, n_rows, norm_ckv):
    tm = 512
    hw = MLA_HEADS * LANES
    tab_spec = pl.BlockSpec((tm, LANES), lambda i: (i, 0))
    return pl.pallas_call(
        functools.partial(_mla_kv_kernel, norm_ckv=norm_ckv),
        grid=(n_rows // tm,),
        in_specs=[pl.BlockSpec((tm, LANES), lambda i: (i, ckv_blk)),
                  pl.BlockSpec((tm, LANES), lambda i: (i, kr_blk)),
                  pl.BlockSpec((1, LANES), lambda i: (0, 0)),
                  pl.BlockSpec((MLA_KV_RANK, hw), lambda i: (0, 0)),
                  pl.BlockSpec((MLA_KV_RANK, hw), lambda i: (0, 0)),
                  pl.BlockSpec((1, LANES), lambda i: (0, 0)),
                  tab_spec, tab_spec, tab_spec],
        out_specs=[pl.BlockSpec((tm, hw), lambda i: (i, 0)),
                   pl.BlockSpec((tm, hw), lambda i: (i, 0)),
                   pl.BlockSpec((tm, LANES), lambda i: (i, 0))],
        out_shape=[jax.ShapeDtypeStruct((n_rows, hw), F32),
                   jax.ShapeDtypeStruct((n_rows, hw), F32),
                   jax.ShapeDtypeStruct((n_rows, LANES), F32)],
        compiler_params=_params("parallel"),
        name="mla_kv",
    )(ckv_src, kr_src, kv_norm, wk_p, wv_p, kn_p, *tabs)


def _softmax_rows(s):
    p = jnp.exp(s - jnp.max(s, axis=-1, keepdims=True))
    return p, jnp.sum(p, axis=-1, keepdims=True)


def _mla_attn_kernel(q_ref, k_ref, v_ref, o_ref):
    scale = MLA_QK ** -0.5
    outs = []
    for h in range(2):
        sl = slice(h * LANES, (h + 1) * LANES)
        p, l = _softmax_rows(_dot_nt_bf16(q_ref[:, sl], k_ref[:, sl]) * scale)
        outs.append(_dot_bf16(p, v_ref[:, sl]) / l)
    o_ref[...] = outs[0] + pltpu.roll(outs[1], MLA_V, axis=1)


def _mla_attn(q, k, v, batch, nq, nk, q_row0, tq):
    nqb = nq // tq
    qb0 = q_row0 // tq
    return pl.pallas_call(
        _mla_attn_kernel,
        grid=(batch, MLA_HEADS // 2, nqb),
        in_specs=[pl.BlockSpec((tq, 2 * LANES), lambda b, h, i: (qb0 + b * nqb + i, h)),
                  pl.BlockSpec((nk, 2 * LANES), lambda b, h, i: (b, h)),
                  pl.BlockSpec((nk, 2 * LANES), lambda b, h, i: (b, h))],
        out_specs=pl.BlockSpec((tq, LANES), lambda b, h, i: (b * nqb + i, h)),
        out_shape=jax.ShapeDtypeStruct((batch * nq, MLA_HEADS * MLA_V), F32),
        compiler_params=_params("parallel", "parallel", "arbitrary"),
        name="mla_attn",
    )(q, k, v)


def _ret_kernel(lg_ref, q_ref, k_ref, v_ref, rg_ref, s0_ref, gn_ref, o_ref, st_ref, *, n, tq):
    b, pair, qi = pl.program_id(0), pl.program_id(1), pl.program_id(2)
    q = q_ref[...]
    k = k_ref[...] * (RET_DK ** -0.5)
    lo = _lane_lo((1, LANES))
    row = (qi * tq + lax.broadcasted_iota(jnp.int32, (tq, 1), 0)).astype(F32)
    col = lax.broadcasted_iota(jnp.int32, (1, n), 1).astype(F32)
    diff = row - col
    for h in range(2):
        lgf = lg_ref[0, 2 * pair + h]
        lgb = lg_ref[1, 2 * pair + h]
        mask = lo if h == 0 else jnp.logical_not(lo)
        qh = jnp.where(mask, q, 0.0)
        vh = v_ref[:, h * LANES:(h + 1) * LANES]
        decay = (jnp.where(diff >= 0, jnp.exp(lgf * jnp.maximum(diff, 0.0)), 0.0)
                 + jnp.where(diff <= 0, jnp.exp(lgb * jnp.maximum(-diff, 0.0)), 0.0))
        o = _dot(_dot_nt(qh, k) * decay, vh)
        o = o + _dot(qh * jnp.exp(lgf * (row + 1.0)), s0_ref[0, 0])
        o = o + _dot(qh * jnp.exp(lgb * (n - row)), s0_ref[0, 1])
        y = _rms(o, RET_DV) * gn_ref[:, h * LANES:(h + 1) * LANES]
        o_ref[:, h * LANES:(h + 1) * LANES] = _silu(rg_ref[:, h * LANES:(h + 1) * LANES]) * y

    @pl.when(qi == 0)
    def _():
        pos = lax.broadcasted_iota(jnp.int32, (n, 1), 0).astype(F32)
        for d in range(2):
            acc = None
            for h in range(2):
                lg = lg_ref[d, 2 * pair + h]
                mask = lo if h == 0 else jnp.logical_not(lo)
                expo = (n - 1.0 - pos) if d == 0 else pos
                kd = jnp.where(mask, k * jnp.exp(lg * expo), 0.0)
                term = _dot_tn(kd, v_ref[:, h * LANES:(h + 1) * LANES])
                acc = term if acc is None else acc + term
            lg_rows = jnp.where(lax.broadcasted_iota(jnp.int32, (LANES, 1), 0) < 64,
                                lg_ref[d, 2 * pair], lg_ref[d, 2 * pair + 1])
            st_ref[0, d] = acc + s0_ref[0, d] * jnp.exp(lg_rows * n)


def _retention(log_g, p, q_blk, k_blk, v_blk, g_blk, s0, gn, batch, n, row0, tq):
    nqb = n // tq
    qb0 = row0 // tq
    kb0 = row0 // n
    pairs = RET_HEADS // 2
    return pl.pallas_call(
        functools.partial(_ret_kernel, n=n, tq=tq),
        grid=(batch, pairs, nqb),
        in_specs=[pl.BlockSpec(memory_space=pltpu.SMEM),
                  pl.BlockSpec((tq, LANES), lambda b, h, i: (qb0 + b * nqb + i, q_blk + h)),
                  pl.BlockSpec((n, LANES), lambda b, h, i: (kb0 + b, k_blk + h)),
                  pl.BlockSpec((n, 2 * LANES), lambda b, h, i: (kb0 + b, v_blk // 2 + h)),
                  pl.BlockSpec((tq, 2 * LANES), lambda b, h, i: (qb0 + b * nqb + i, g_blk // 2 + h)),
                  pl.BlockSpec((1, 2, LANES, LANES), lambda b, h, i: (b, 0, h, 0)),
                  pl.BlockSpec((1, 2 * LANES), lambda b, h, i: (0, h))],
        out_specs=[pl.BlockSpec((tq, 2 * LANES), lambda b, h, i: (b * nqb + i, h)),
                   pl.BlockSpec((1, 2, LANES, LANES), lambda b, h, i: (b, 0, h, 0))],
        out_shape=[jax.ShapeDtypeStruct((batch * n, RET_HEADS * RET_DV), F32),
                   jax.ShapeDtypeStruct((batch, 2, RET_HEADS * RET_DK, RET_DV), F32)],
        compiler_params=_params("parallel", "parallel", "arbitrary"),
        name="retention",
    )(log_g, p, p, p, p, s0, gn)


def _diff_qk_kernel(q_ref, k_ref, qn_ref, kn_ref, c_ref, s1_ref, s2_ref, qo_ref, ko_ref):
    c, s1, s2 = c_ref[...], s1_ref[...], s2_ref[...]
    for src, gain, dst in ((q_ref, qn_ref, qo_ref), (k_ref, kn_ref, ko_ref)):
        for h in range(DIFF_HEADS):
            sl = slice(h * LANES, (h + 1) * LANES)
            y = src[:, sl]
            y = y * lax.rsqrt(_seg64_sum(y * y) * (1.0 / DIFF_DH) + EPS) * gain[...]
            dst[:, sl] = _rope(y, c, s1, s2, DIFF_DH // 2)


def _diff_qk(p, qn_p, kn_p, tabs):
    tm = 512
    tab_spec = pl.BlockSpec((tm, LANES), lambda i: (i, 0))
    return pl.pallas_call(
        _diff_qk_kernel,
        grid=(N_TOK // tm,),
        in_specs=[pl.BlockSpec((tm, DIFF_W), lambda i: (i, 0)),
                  pl.BlockSpec((tm, DIFF_W), lambda i: (i, 1)),
                  pl.BlockSpec((1, LANES), lambda i: (0, 0)),
                  pl.BlockSpec((1, LANES), lambda i: (0, 0)),
                  tab_spec, tab_spec, tab_spec],
        out_specs=[pl.BlockSpec((tm, DIFF_W), lambda i: (i, 0)),
                   pl.BlockSpec((tm, DIFF_W), lambda i: (i, 0))],
        out_shape=[jax.ShapeDtypeStruct((N_TOK, DIFF_W), F32)] * 2,
        compiler_params=_params("parallel"),
        name="diff_qk",
    )(p, p, qn_p, kn_p, *tabs)


def _diff_attn_kernel(lam_ref, q_ref, k_ref, v_ref, gn_ref, o_ref, *, lam_init):
    lv = lam_ref[...]
    lam = (jnp.exp(jnp.sum(lv[0:1] * lv[1:2], axis=-1, keepdims=True))
           - jnp.exp(jnp.sum(lv[2:3] * lv[3:4], axis=-1, keepdims=True)) + lam_init)
    scale = DIFF_DH ** -0.5
    q = q_ref[...]
    k = k_ref[...]
    lo = _lane_lo((1, LANES))
    kb = k.astype(BF16)
    p1, l1 = _softmax_rows(_dot_nt_bf16(jnp.where(lo, q, 0.0), kb) * scale)
    p2, l2 = _softmax_rows(_dot_nt_bf16(jnp.where(lo, 0.0, q), kb) * scale)
    w = p1 / l1 - lam * (p2 / l2)
    o = _dot_bf16(w, v_ref[...])
    o_ref[...] = _rms(o, 2 * DIFF_DH) * gn_ref[...] * (1.0 - lam_init)


def _diff_attn(lam, q, k, v, gn, batch, nq, nk, q_row0, tq, lam_init):
    nqb = nq // tq
    qb0 = q_row0 // tq
    return pl.pallas_call(
        functools.partial(_diff_attn_kernel, lam_init=lam_init),
        grid=(batch, DIFF_HEADS, nqb),
        in_specs=[pl.BlockSpec((4, DIFF_DH), lambda b, h, i: (0, 0)),
                  pl.BlockSpec((tq, LANES), lambda b, h, i: (qb0 + b * nqb + i, h)),
                  pl.BlockSpec((nk, LANES), lambda b, h, i: (b, h)),
                  pl.BlockSpec((nk, LANES), lambda b, h, i: (b, h)),
                  pl.BlockSpec((1, LANES), lambda b, h, i: (0, h))],
        out_specs=pl.BlockSpec((tq, LANES), lambda b, h, i: (b * nqb + i, h)),
        out_shape=jax.ShapeDtypeStruct((batch * nq, DIFF_W), F32),
        compiler_params=_params("parallel", "parallel", "arbitrary"),
        name="diff_attn",
    )(lam, q, k, v, gn)


def _seg64_sum_wide(x):
    return jnp.concatenate([_seg64_sum(x[:, j * LANES:(j + 1) * LANES]) for j in range(x.shape[1] // LANES)], axis=1)


_Q_R, _Q_V, _Q_KK = 0, 1, 2
_Q_DIR = 3
_N_Q = _Q_DIR + 2 * 3


def _rwkv_pre_kernel(r_ref, k_ref, v_ref, lo_ref, wup_ref, aup_ref, gup_ref, w0_ref, a0_ref, kk_ref, ka_ref, rk_ref,
                     pk_ref, g_ref, bonus_ref):
    W = RWKV_W
    col = lambda q: slice(q * W, (q + 1) * W)
    r = r_ref[...]
    k = k_ref[...]
    lora = lo_ref[...]
    kk = k * kk_ref[...]
    kkn = kk * lax.rsqrt(_seg64_sum_wide(kk * kk) + EPS)
    pk_ref[:, col(_Q_R)] = r
    pk_ref[:, col(_Q_V)] = v_ref[...]
    pk_ref[:, col(_Q_KK)] = kkn
    g_ref[...] = _dot(_sigmoid(lora[:, 2 * LANES:3 * LANES]), gup_ref[...])
    pre = w0_ref[...] + _dot(jnp.tanh(lora[:, 0:LANES]), wup_ref[...])
    decay = jnp.exp(-jnp.exp(-_softplus(-pre) - 0.5))
    a = _sigmoid(a0_ref[...] + _dot(lora[:, LANES:2 * LANES], aup_ref[...]))
    bonus = None
    for d in range(2):
        a_d = a[:, col(d)]
        k_d = k * (1.0 + (a_d - 1.0) * ka_ref[...])
        pk_ref[:, col(_Q_DIR + 3 * d)] = decay[:, col(d)]
        pk_ref[:, col(_Q_DIR + 3 * d + 1)] = k_d
        pk_ref[:, col(_Q_DIR + 3 * d + 2)] = kkn * a_d
        t = _seg64_sum_wide(r * k_d * rk_ref[...])
        bonus = t if bonus is None else bonus + t
    bonus_ref[...] = bonus


def _rwkv_pre(p, r_blk, k_blk, v_blk, lo_blk, wup_bd, aup_bd, gup, w0, a0, k_k, k_a, r_k):
    tm = 256
    w = RWKV_W
    row = lambda n: pl.BlockSpec((1, n), lambda i: (0, 0))
    full = lambda a, b: pl.BlockSpec((a, b), lambda i: (0, 0))
    return pl.pallas_call(
        _rwkv_pre_kernel,
        grid=(N_TOK // tm,),
        in_specs=[pl.BlockSpec((tm, w), lambda i: (i, r_blk)),
                  pl.BlockSpec((tm, w), lambda i: (i, k_blk)),
                  pl.BlockSpec((tm, w), lambda i: (i, v_blk)),
                  pl.BlockSpec((tm, 3 * LANES), lambda i: (i, lo_blk)),
                  full(LANES, 2 * w), full(LANES, 2 * w), full(LANES, w),
                  row(2 * w), row(2 * w), row(w), row(w), row(w)],
        out_specs=[pl.BlockSpec((tm, _N_Q * w), lambda i: (i, 0))] + [pl.BlockSpec((tm, w), lambda i: (i, 0))] * 2,
        out_shape=[jax.ShapeDtypeStruct((N_TOK, _N_Q * w), F32)] + [jax.ShapeDtypeStruct((N_TOK, w), F32)] * 2,
        compiler_params=_params("parallel"),
        name="rwkv_pre",
    )(p, p, p, p, wup_bd, aup_bd, gup, w0, a0, k_k, k_a, r_k)


_SCAN_CHUNK = 32
_SCAN_UNROLL = 16


def _rwkv_first_sa(s_ref, sa_ref, kk, n_k):
    nv = s_ref.shape[1]
    chunk = min(_SCAN_CHUNK, nv)
    for c0 in range(0, nv, chunk):
        def body(k, acc):
            return acc + s_ref[k, c0:c0 + chunk, :] * kk(k)
        sa_ref[c0:c0 + chunk, :] = lax.fori_loop(0, n_k, body, jnp.zeros((chunk, LANES), F32), unroll=_SCAN_UNROLL)


def _rwkv_step(s_ref, sa_ref, kk_next, w, kd, b, r, v_at, n_k):
    nv = s_ref.shape[1]
    chunk = min(_SCAN_CHUNK, nv)
    ys = []
    for c0 in range(0, nv, chunk):
        sa = sa_ref[c0:c0 + chunk, :]
        vc = v_at(c0, chunk)

        def body(k, acc):
            y_acc, sa_acc = acc
            s_new = s_ref[k, c0:c0 + chunk, :] * w(k) - sa * b(k) + vc * kd(k)
            s_ref[k, c0:c0 + chunk, :] = s_new
            return y_acc + s_new * r(k), sa_acc + s_new * kk_next(k)

        zero = jnp.zeros((chunk, LANES), F32)
        y_acc, sa_acc = lax.fori_loop(0, n_k, body, (zero, zero), unroll=_SCAN_UNROLL)
        sa_ref[c0:c0 + chunk, :] = sa_acc
        ys.append(y_acc)
    return ys[0] if len(ys) == 1 else jnp.concatenate(ys, axis=0)


_CTX_TB = 32


def _rwkv_scan_ctx_kernel(xs_ref, xd_ref, y_ref, st_ref, s_ref, sa_ref):
    d = pl.program_id(0)
    tb = pl.program_id(1)
    step_t = lambda i: jnp.where(d == 0, i, _CTX_TB - 1 - i)

    @pl.when(tb == 0)
    def _():
        s_ref[...] = jnp.zeros_like(s_ref)

    t0 = step_t(0)
    _rwkv_first_sa(s_ref, sa_ref, lambda k: xs_ref[t0, _Q_KK, pl.ds(k, 1), :], RWKV_HS)

    def step(i, carry):
        t = step_t(i)
        tn = step_t(jnp.minimum(i + 1, _CTX_TB - 1))
        shared = lambda q, tt: (lambda k: xs_ref[tt, q, pl.ds(k, 1), :])
        per_dir = lambda q: (lambda k: xd_ref[t, q, pl.ds(k, 1), :])
        y_ref[t, 0] = _rwkv_step(s_ref, sa_ref, shared(_Q_KK, tn), per_dir(0), per_dir(1), per_dir(2),
                                 shared(_Q_R, t), lambda c0, n: xs_ref[t, _Q_V, pl.ds(c0, n), :], RWKV_HS)
        return carry

    lax.fori_loop(0, _CTX_TB, step, 0)

    @pl.when(tb == pl.num_programs(1) - 1)
    def _():
        st_ref[0] = s_ref[...]


def _rwkv_scan_ctx(pkt):
    nt = SEQ // _CTX_TB
    hs = RWKV_HS
    tblk = lambda d, tb: jnp.where(d == 0, tb, nt - 1 - tb)
    return pl.pallas_call(
        _rwkv_scan_ctx_kernel,
        grid=(2, nt),
        in_specs=[pl.BlockSpec((_CTX_TB, 3, hs, LANES), lambda d, tb: (tblk(d, tb), 0, 0, 0)),
                  pl.BlockSpec((_CTX_TB, 3, hs, LANES), lambda d, tb: (tblk(d, tb), 1 + d, 0, 0))],
        out_specs=[pl.BlockSpec((_CTX_TB, 1, hs, LANES), lambda d, tb: (tblk(d, tb), d, 0, 0)),
                   pl.BlockSpec((1, hs, hs, LANES), lambda d, tb: (d, 0, 0, 0))],
        out_shape=[jax.ShapeDtypeStruct((SEQ, 2, hs, LANES), F32),
                   jax.ShapeDtypeStruct((2, hs, hs, LANES), F32)],
        scratch_shapes=[pltpu.VMEM((hs, hs, LANES), F32), pltpu.VMEM((hs, LANES), F32)],
        compiler_params=_params("parallel", "arbitrary"),
        name="rwkv_scan_ctx",
    )(pkt, pkt)


_LAT_TB = 32
_LAT_VSPLIT = 4
_LAT_STATES = 2 * DEC_BATCH * RWKV_HEADS
_LAT_VROWS = RWKV_HS // _LAT_VSPLIT


def _rwkv_scan_lat_kernel(kt_ref, v_ref, s0_ref, y_ref, s_ref, sa_ref):
    @pl.when(pl.program_id(0) == 0)
    def _():
        s_ref[...] = s0_ref[...]

    _rwkv_first_sa(s_ref, sa_ref, lambda k: kt_ref[0, 0, pl.ds(k, 1), :], RWKV_HS)

    def step(t, carry):
        tn = jnp.minimum(t + 1, _LAT_TB - 1)
        row = lambda q, tt: (lambda k: kt_ref[tt, q, pl.ds(k, 1), :])
        y_ref[t] = _rwkv_step(s_ref, sa_ref, row(0, tn), row(1, t), row(2, t), row(3, t), row(4, t),
                              lambda c0, n: v_ref[t, pl.ds(c0, n), :], RWKV_HS)
        return carry

    lax.fori_loop(0, _LAT_TB, step, 0)


def _rwkv_scan_lat(kt, v, s0):
    hs = RWKV_HS
    nv = _LAT_VROWS
    return pl.pallas_call(
        _rwkv_scan_lat_kernel,
        grid=(DEC_SEQ // _LAT_TB,),
        in_specs=[pl.BlockSpec((_LAT_TB, 5, hs, LANES), lambda tb: (tb, 0, 0, 0)),
                  pl.BlockSpec((_LAT_TB, nv, LANES), lambda tb: (tb, 0, 0)),
                  pl.BlockSpec((hs, nv, LANES), lambda tb: (0, 0, 0))],
        out_specs=pl.BlockSpec((_LAT_TB, nv, LANES), lambda tb: (tb, 0, 0)),
        out_shape=jax.ShapeDtypeStruct((DEC_SEQ, nv, LANES), F32),
        scratch_shapes=[pltpu.VMEM((hs, nv, LANES), F32), pltpu.VMEM((nv, LANES), F32)],
        compiler_params=_params("arbitrary"),
        name="rwkv_scan_lat",
    )(kt, v, s0)


def _rwkv_post_kernel(yf_ref, yb_ref, bonus_ref, v_ref, g_ref, gn_ref, o_ref):
    y = yf_ref[...] + yb_ref[...]
    y = y * lax.rsqrt(_seg64_sum_wide(y * y) * (1.0 / RWKV_HS) + EPS) * gn_ref[...]
    o_ref[...] = (y + bonus_ref[...] * v_ref[...]) * g_ref[...]


def _rwkv_post(y, bonus, p, v_blk, g, gn):
    tm = 512
    w = RWKV_W
    spec = pl.BlockSpec((tm, w), lambda i: (i, 0))
    return pl.pallas_call(
        _rwkv_post_kernel,
        grid=(N_TOK // tm,),
        in_specs=[spec, pl.BlockSpec((tm, w), lambda i: (i, 1)), spec,
                  pl.BlockSpec((tm, w), lambda i: (i, v_blk)), spec,
                  pl.BlockSpec((1, w), lambda i: (0, 0))],
        out_specs=spec,
        out_shape=jax.ShapeDtypeStruct((N_TOK, w), F32),
        compiler_params=_params("parallel"),
        name="rwkv_post",
    )(y, y, bonus, p, g, gn)


def _ctx_scan_operands(pk):
    x = pk[:N_CTX].reshape(BATCH, SEQ, _N_Q, RWKV_HEADS, RWKV_HS)
    return jnp.transpose(x, (1, 2, 4, 0, 3)).reshape(SEQ, _N_Q, RWKV_HS, LANES)


def _ctx_scan_result(y):
    y = y.reshape(SEQ, 2, RWKV_HS, BATCH, RWKV_HEADS)
    return jnp.transpose(y, (3, 0, 1, 4, 2)).reshape(N_CTX, 2 * RWKV_W)


def _value_split_layout(x):
    lead = x.shape[:-2]
    n = len(lead)
    x = x.reshape(lead + (_LAT_STATES, _LAT_VSPLIT, _LAT_VROWS))
    return jnp.transpose(x, tuple(range(n)) + (n + 2, n + 1, n)).reshape(lead + (_LAT_VROWS, LANES))


def _lat_scan_operands(pk):
    x = pk[N_CTX:].reshape(DEC_BATCH, DEC_SEQ, _N_Q, RWKV_HEADS, RWKV_HS)
    order = lambda d: np.array([_Q_KK, _Q_DIR + 3 * d, _Q_DIR + 3 * d + 1, _Q_DIR + 3 * d + 2, _Q_R, _Q_V])
    st = jnp.stack([x[:, :, order(0)], x[:, ::-1, order(1)]], 0)
    st = jnp.transpose(st, (2, 3, 0, 1, 4, 5)).reshape(DEC_SEQ, 6, _LAT_STATES, RWKV_HS)
    kt = jnp.tile(jnp.swapaxes(st[:, :5], 2, 3), (1, 1, 1, _LAT_VSPLIT))
    return kt, _value_split_layout(st[:, 5])


def _lat_scan_result(y):
    y = jnp.transpose(y.reshape(DEC_SEQ, _LAT_VROWS, _LAT_VSPLIT, _LAT_STATES), (0, 3, 2, 1))
    y = y.reshape(DEC_SEQ, 2, DEC_BATCH, RWKV_HEADS, RWKV_HS)
    y = jnp.transpose(y, (1, 2, 0, 3, 4)).reshape(2, DEC_BATCH, DEC_SEQ, RWKV_W)
    return jnp.concatenate([y[0], y[1, :, ::-1]], -1).reshape(N_LAT, 2 * RWKV_W)


def _even_layer(x, mod, g_mix, w_in, q_norm, kv_norm, w_uq, w_ukv, qn, kn, ret_decay, ret_gn,
                cache_ckv, cache_krope, state_ret, tabs_m):
    perm_r = _deinterleave(MLA_ROPE)
    cq, ckv, krope, rq, rk, rv, rg = jnp.split(w_in, np.cumsum(
        (MLA_Q_RANK, MLA_KV_RANK, MLA_ROPE, RET_HEADS * RET_DK, RET_HEADS * RET_DK, RET_HEADS * RET_DV))[:].tolist(),
        axis=1)
    z = lambda n: jnp.zeros((D_MODEL, n), F32)
    w_p = jnp.concatenate([cq, ckv, z(MLA_NOPE), krope[:, perm_r], z(LANES - MLA_QK), rq, rk, rv, rg], axis=1)
    p = _inproj(x, g_mix, mod, w_p, jnp.zeros((1, w_p.shape[1]), F32), 512, False)
    CKV_BLK, KR_BLK, RQ_BLK, RK_BLK, RV_BLK, RG_BLK = 2, 3, 4, 6, 8, 12

    def head_pad(w, n_head, d_head, cols):
        w = w.reshape(w.shape[0], n_head, d_head)[:, :, cols]
        return jnp.pad(w, ((0, 0), (0, 0), (0, LANES - len(cols)))).reshape(w.shape[0], n_head * LANES)

    qk_cols = np.concatenate([np.arange(MLA_NOPE), MLA_NOPE + perm_r])
    w_uq_p = head_pad(w_uq, MLA_HEADS, MLA_QK, qk_cols)
    wk_p = head_pad(w_ukv, MLA_HEADS, MLA_NOPE + MLA_V, np.arange(MLA_NOPE))
    wv_p = head_pad(w_ukv, MLA_HEADS, MLA_NOPE + MLA_V, MLA_NOPE + np.arange(MLA_V))
    qn_p = jnp.pad(qn[qk_cols], (0, LANES - MLA_QK))[None]
    kn_p = jnp.pad(kn[qk_cols], (0, LANES - MLA_QK))[None]

    q = _mla_q(p, q_norm[None], w_uq_p, qn_p, tabs_m)
    k, v, ckvn = _mla_kv(p, CKV_BLK, p, KR_BLK, kv_norm[None], wk_p, wv_p, kn_p, tabs_m, N_TOK, True)

    n_c = DEC_BATCH * PAST_LEN
    kr_c = jnp.pad(cache_krope.reshape(n_c, MLA_ROPE)[:, perm_r], ((0, 0), (MLA_NOPE, LANES - MLA_QK)))
    ident = (jnp.ones((n_c, LANES), F32), jnp.zeros((n_c, LANES), F32), jnp.zeros((n_c, LANES), F32))
    k_c, v_c, _ = _mla_kv(cache_ckv.reshape(n_c, MLA_KV_RANK), 0, kr_c, 0, kv_norm[None], wk_p, wv_p, kn_p,
                          ident, n_c, False)

    hw = MLA_HEADS * LANES

    def with_cache(own, cache):
        own = own[N_CTX:].reshape(DEC_BATCH, DEC_SEQ, hw)
        return jnp.concatenate([own, cache.reshape(DEC_BATCH, PAST_LEN, hw)], 1).reshape(-1, hw)

    o_ctx = _mla_attn(q, k, v, BATCH, SEQ, SEQ, 0, SEQ)
    o_lat = _mla_attn(q, with_cache(k, k_c), with_cache(v, v_c), DEC_BATCH, DEC_SEQ, DEC_SEQ + PAST_LEN, N_CTX, 256)

    log_g = -_softplus(-ret_decay)
    gn = ret_gn[None]
    s0_ctx = jnp.zeros((BATCH, 2, RET_HEADS * RET_DK, RET_DV), F32)
    r_ctx, st_ctx = _retention(log_g, p, RQ_BLK, RK_BLK, RV_BLK, RG_BLK, s0_ctx, gn, BATCH, SEQ, 0, SEQ)
    s0_lat = state_ret.reshape(DEC_BATCH, 2, RET_HEADS * RET_DK, RET_DV)
    r_lat, _ = _retention(log_g, p, RQ_BLK, RK_BLK, RV_BLK, RG_BLK, s0_lat, gn, DEC_BATCH, DEC_SEQ, N_CTX, 256)

    mix = [jnp.concatenate([o_ctx, o_lat], 0), jnp.concatenate([r_ctx, r_lat], 0)]
    new_ckv = ckvn[:N_CTX].reshape(BATCH, SEQ, MLA_KV_RANK)
    new_krope = p[:N_CTX, KR_BLK * LANES + MLA_NOPE:KR_BLK * LANES + MLA_QK][:, np.argsort(perm_r)]
    new_krope = new_krope.reshape(BATCH, SEQ, MLA_ROPE)
    new_ret = st_ctx.reshape(BATCH, 2, RET_HEADS, RET_DK, RET_DV)
    return mix, new_ckv, new_krope, new_ret


def _odd_layer(x, mod, g_mix, w_in, qn, kn, lam, diff_gn, mu, w0, w_up, a0, a_up, g_up, k_k, k_a, r_k, gn,
               cache_k, cache_v, state_rwkv, tabs_d, lam_init):
    perm = _deinterleave(DIFF_DH)
    qk_perm = (np.arange(2 * DIFF_W).reshape(-1, DIFF_DH)[:, perm]).reshape(-1)
    w_p = jnp.concatenate([w_in[:, qk_perm], w_in[:, 2 * DIFF_W:]], axis=1)
    n_in = w_p.shape[1]
    mu_full = jnp.concatenate([jnp.zeros((3 * DIFF_W,), F32), mu])[None]
    p = _inproj(x, g_mix, mod, w_p, mu_full, 384, True)
    DV_BLK, R_BLK, K_BLK, V_BLK = 2, 3, 4, 5
    LO_BLK = (6 * RWKV_W) // (3 * LANES)

    qn_p = jnp.tile(qn[perm], 2)[None]
    kn_p = jnp.tile(kn[perm], 2)[None]
    q, k = _diff_qk(p, qn_p, kn_p, tabs_d)
    v = p[:, DV_BLK * DIFF_W:(DV_BLK + 1) * DIFF_W]

    n_c = DEC_BATCH * PAST_LEN
    k_c = cache_k.reshape(n_c, DIFF_HEADS * 2, DIFF_DH)[:, :, perm].reshape(DEC_BATCH, PAST_LEN, DIFF_W)
    v_c = cache_v.reshape(DEC_BATCH, PAST_LEN, DIFF_W)

    def with_cache(own, cache):
        return jnp.concatenate([own[N_CTX:].reshape(DEC_BATCH, DEC_SEQ, DIFF_W), cache], 1).reshape(-1, DIFF_W)

    dgn = diff_gn[None]
    o_ctx = _diff_attn(lam, q, k, v, dgn, BATCH, SEQ, SEQ, 0, SEQ, lam_init)
    o_lat = _diff_attn(lam, q, with_cache(k, k_c), with_cache(v, v_c), dgn, DEC_BATCH, DEC_SEQ, DEC_SEQ + PAST_LEN,
                       N_CTX, 256, lam_init)

    zero = jnp.zeros((RWKV_W_LORA, RWKV_W), F32)
    wup_bd = jnp.concatenate([jnp.concatenate([w_up[0], zero], 1), jnp.concatenate([zero, w_up[1]], 1)], 0)
    aup_bd = jnp.concatenate([jnp.concatenate([a_up[0], zero], 1), jnp.concatenate([zero, a_up[1]], 1)], 0)
    pk, g, bonus = _rwkv_pre(p, R_BLK, K_BLK, V_BLK, LO_BLK, wup_bd, aup_bd, g_up, w0.reshape(1, -1),
                             a0.reshape(1, -1), k_k[None], k_a[None], r_k.reshape(1, -1))

    y_ctx, st_ctx = _rwkv_scan_ctx(_ctx_scan_operands(pk))
    kt, v_lat = _lat_scan_operands(pk)
    s0_lat = jnp.transpose(state_rwkv, (4, 1, 0, 2, 3)).reshape(RWKV_HS, _LAT_STATES, RWKV_HS)
    y_lat = _rwkv_scan_lat(kt, v_lat, _value_split_layout(s0_lat))
    y = jnp.concatenate([_ctx_scan_result(y_ctx), _lat_scan_result(y_lat)], 0)
    rw_o = _rwkv_post(y, bonus, p, V_BLK, g, gn[None])

    mix = [jnp.concatenate([o_ctx, o_lat], 0), rw_o]
    inv = np.argsort(perm)
    new_dk = k[:N_CTX].reshape(BATCH, SEQ, DIFF_HEADS, 2, DIFF_DH)[..., inv]
    new_dv = v[:N_CTX].reshape(BATCH, SEQ, DIFF_HEADS, 2 * DIFF_DH)
    new_rwkv = jnp.transpose(st_ctx.reshape(2, RWKV_HS, RWKV_HS, BATCH, RWKV_HEADS), (3, 0, 4, 2, 1))
    return mix, new_dk, new_dv, new_rwkv


def kernel(x_prompt, x_sample, cache_mla_ckv, cache_mla_krope, state_ret, cache_diff_k, cache_diff_v, state_rwkv,
           c, c_ctx, ada_w, ada_b, norm_mix_g, norm_ffn_g, w_out, ffn_up, ffn_conv_w, ffn_conv_b, ffn_down,
           a_w_in, mla_q_norm, mla_kv_norm, mla_w_uq, mla_w_ukv, mla_qn, mla_kn, ret_decay, ret_gn,
           b_w_in, diff_qn, diff_kn, diff_lam, diff_gn, rwkv_mu, rwkv_w0, rwkv_w_up, rwkv_a0, rwkv_a_up,
           rwkv_g_up, rwkv_k_k, rwkv_k_a, rwkv_r_k, rwkv_gn):
    x = jnp.concatenate([x_prompt.reshape(N_CTX, D_MODEL), x_sample.reshape(N_LAT, D_MODEL)], 0)
    cond8 = jnp.pad(jnp.concatenate([c_ctx[None], c], 0), ((0, 8 - N_GROUPS), (0, 0)))
    mod = _modulation(cond8, ada_w, ada_b)

    cos_m, sin_m = _axial_angles(MLA_ROPE)
    cos_d, sin_d = _axial_angles(DIFF_DH)
    tabs_m = _rope_tables(cos_m, sin_m, MLA_NOPE, MLA_ROPE // 2, (0,))
    tabs_d = _rope_tables(cos_d, sin_d, 0, DIFF_DH // 2, (0, DIFF_DH))

    outs = {}
    for l in range(DEPTH):
        j = l // 2
        g_mix = norm_mix_g[l][None]
        if l % 2 == 0:
            mix, outs["ckv"], outs["krope"], outs["ret"] = _even_layer(
                x, mod[l], g_mix, a_w_in[j], mla_q_norm[j], mla_kv_norm[j], mla_w_uq[j], mla_w_ukv[j], mla_qn[j],
                mla_kn[j], ret_decay[j], ret_gn[j], cache_mla_ckv[:, j], cache_mla_krope[:, j], state_ret[:, j],
                tabs_m)
        else:
            lam_init = 0.8 - 0.6 * math.exp(-0.3 * l)
            mix, outs["dk"], outs["dv"], outs["rwkv"] = _odd_layer(
                x, mod[l], g_mix, b_w_in[j], diff_qn[j], diff_kn[j], diff_lam[j], diff_gn[j], rwkv_mu[j],
                rwkv_w0[j], rwkv_w_up[j], rwkv_a0[j], rwkv_a_up[j], rwkv_g_up[j], rwkv_k_k[j], rwkv_k_a[j],
                rwkv_r_k[j], rwkv_gn[j], cache_diff_k[:, j], cache_diff_v[:, j], state_rwkv[:, j], tabs_d, lam_init)
        x = _resid_proj(mix, w_out[l], x, mod[l], 2)
        act = _ffn_up(x, norm_ffn_g[l][None], mod[l], ffn_up[l], ffn_conv_w[l], ffn_conv_b[l])
        x = _resid_proj([act], ffn_down[l], x, mod[l], 5)

    y_prompt = x[:N_CTX].reshape(BATCH, SEQ, D_MODEL)
    y_sample = x[N_CTX:].reshape(DEC_BATCH, DEC_SEQ, D_MODEL)
    return (y_prompt, y_sample, outs["ckv"][:, None], outs["krope"][:, None], outs["ret"][:, None],
            outs["dk"][:, None], outs["dv"][:, None], outs["rwkv"][:, None])
```

```python
import functools
import math

import numpy as np
import jax
import jax.numpy as jnp
from jax import lax
from jax.experimental import pallas as pl
from jax.experimental.pallas import tpu as pltpu

D_MODEL = 1024
BATCH = 16
SEQ = 256
DEPTH = 2
DEC_BATCH = 2
DEC_SEQ = 1024
PAST_LEN = 512
GRID_W = 64
EPS = 1e-6
ROPE_BASE = 10000.0

MLA_HEADS = 8
MLA_Q_RANK = 256
MLA_KV_RANK = 128
MLA_NOPE = 64
MLA_ROPE = 32
MLA_V = 64
MLA_QK = MLA_NOPE + MLA_ROPE
RET_HEADS = 4
RET_DK = 64
RET_DV = 128
DIFF_HEADS = 4
DIFF_DH = 64
DIFF_W = DIFF_HEADS * 2 * DIFF_DH
RWKV_HEADS = 8
RWKV_HS = 64
RWKV_W = RWKV_HEADS * RWKV_HS
RWKV_W_LORA = 64
RWKV_A_LORA = 64
RWKV_G_LORA = 128
D_FF = 2816

N_CTX = BATCH * SEQ
N_LAT = DEC_BATCH * DEC_SEQ
N_TOK = N_CTX + N_LAT
N_GROUPS = 1 + DEC_BATCH

LANES = 128
VMEM_LIMIT = 56 * 1024 * 1024

_PREC = lax.Precision.HIGHEST
F32 = jnp.float32


def _dot(a, b):
    return jnp.dot(a, b, precision=_PREC, preferred_element_type=F32)


def _dot_nt(a, b):
    return lax.dot_general(a, b, (((1,), (1,)), ((), ())), precision=_PREC, preferred_element_type=F32)


def _dot_tn(a, b):
    return lax.dot_general(a, b, (((0,), (0,)), ((), ())), precision=_PREC, preferred_element_type=F32)


BF16 = jnp.bfloat16


_PARTS = 1


def _split_weight(w):
    k, n = w.shape
    tk = 512 if k % 512 == 0 else 256
    tn = 512 if n % 512 == 0 else 384 if n % 384 == 0 else 256
    out = pl.pallas_call(
        _split_weight_kernel,
        grid=(k // tk, n // tn),
        in_specs=[pl.BlockSpec((tk, tn), lambda i, j: (i, j))],
        out_specs=pl.BlockSpec((_PARTS, tk, tn), lambda i, j: (0, i, j)),
        out_shape=jax.ShapeDtypeStruct((_PARTS, k, n), BF16),
        compiler_params=_params("parallel", "parallel"),
        name="split_weight",
    )(w)
    return out.reshape(_PARTS * k, n)


def _split_weight_kernel(w_ref, o_ref):
    w = w_ref[...]
    hi = w.astype(BF16)
    o_ref[0] = hi
    if _PARTS == 3:
        o_ref[1] = hi
        o_ref[2] = (w - hi.astype(F32)).astype(BF16)


def _store_split(dst_ref, a, k_total, k0=0):
    k = a.shape[1]
    hi = a.astype(BF16)
    dst_ref[:, k0:k0 + k] = hi
    if _PARTS == 3:
        dst_ref[:, k_total + k0:k_total + k0 + k] = (a - hi.astype(F32)).astype(BF16)
        dst_ref[:, 2 * k_total + k0:2 * k_total + k0 + k] = hi


def _dot_split(a3, w3):
    return jnp.dot(a3, w3, preferred_element_type=F32)


def _dot_bf16(a, b):
    return jnp.dot(a.astype(BF16), b.astype(BF16), preferred_element_type=F32)


def _dot_nt_bf16(a, b):
    return lax.dot_general(a.astype(BF16), b.astype(BF16), (((1,), (1,)), ((), ())), preferred_element_type=F32)


def _params(*sem):
    return pltpu.CompilerParams(dimension_semantics=sem, vmem_limit_bytes=VMEM_LIMIT)


def _sigmoid(x):
    return 1.0 / (1.0 + jnp.exp(-x))


def _silu(x):
    return x * _sigmoid(x)


def _softplus(x):
    return jnp.maximum(x, 0.0) + jnp.log(1.0 + jnp.exp(-jnp.abs(x)))


def _rms(x, n):
    return x * lax.rsqrt(jnp.sum(x * x, axis=-1, keepdims=True) * (1.0 / n) + EPS)


def _lane_lo(shape):
    return lax.broadcasted_iota(jnp.int32, shape, len(shape) - 1) < 64


def _seg64_sum(x):
    lo = _lane_lo(x.shape)
    s_lo = jnp.sum(jnp.where(lo, x, 0.0), axis=-1, keepdims=True)
    s_hi = jnp.sum(jnp.where(lo, 0.0, x), axis=-1, keepdims=True)
    return jnp.where(lo, s_lo, s_hi)


def _seq_neighbours(p, tile, tile_rows):
    seq_mask = jnp.where(tile * tile_rows < N_CTX, SEQ - 1, DEC_SEQ - 1)
    pos = lax.broadcasted_iota(jnp.int32, (tile_rows, 1), 0) & seq_mask
    prev = jnp.where(pos == 0, 0.0, pltpu.roll(p, 1, axis=0))
    nxt = jnp.where(pos == seq_mask, 0.0, pltpu.roll(p, tile_rows - 1, axis=0))
    return prev, nxt


def _group_of_tile(i, tile_rows):
    row = i * tile_rows
    return jnp.where(row < N_CTX, 0, 1 + (row - N_CTX) // DEC_SEQ)


def _modulation_kernel(c_ref, w_ref, b_ref, o_ref):
    o_ref[0] = _dot(_silu(c_ref[...]), w_ref[0]) + b_ref[0]


def _modulation(cond8, ada_w, ada_b):
    tn = 512
    n = 6 * D_MODEL
    out = pl.pallas_call(
        _modulation_kernel,
        grid=(DEPTH, n // tn),
        in_specs=[pl.BlockSpec((8, D_MODEL), lambda l, j: (0, 0)),
                  pl.BlockSpec((1, D_MODEL, tn), lambda l, j: (l, 0, j)),
                  pl.BlockSpec((1, 1, tn), lambda l, j: (l, 0, j))],
        out_specs=pl.BlockSpec((1, 8, tn), lambda l, j: (l, 0, j)),
        out_shape=jax.ShapeDtypeStruct((DEPTH, 8, n), F32),
        compiler_params=_params("parallel", "parallel"),
        name="modulation",
    )(cond8, ada_w, ada_b.reshape(DEPTH, 1, n))
    m = out[:, :N_GROUPS].reshape(DEPTH, N_GROUPS, 6, D_MODEL)
    return jnp.pad(m, ((0, 0), (0, 0), (0, 2), (0, 0)))


_TM_SEQ = 1024


def _norm_mod(x, g, mod, off):
    return _rms(x, D_MODEL) * g * (1.0 + mod[off + 1:off + 2, :]) + mod[off:off + 1, :]


def _inproj_kernel(x_ref, g_ref, mod_ref, w_ref, mu_ref, o_ref, h_ref, *, shift):
    i = pl.program_id(0)

    @pl.when(pl.program_id(1) == 0)
    def _():
        _store_split(h_ref, _norm_mod(x_ref[...], g_ref[...], mod_ref[0], 0), D_MODEL)

    p = _dot_split(h_ref[...], w_ref[...])
    if shift:
        prev, nxt = _seq_neighbours(p, i, _TM_SEQ)
        p = p + (0.5 * (prev + nxt) - p) * mu_ref[...]
    o_ref[...] = p


def _inproj(x, g, mod, w, mu, tn, shift):
    n = w.shape[1]
    tm = _TM_SEQ
    return pl.pallas_call(
        functools.partial(_inproj_kernel, shift=shift),
        grid=(N_TOK // tm, n // tn),
        in_specs=[pl.BlockSpec((tm, D_MODEL), lambda i, j: (i, 0)),
                  pl.BlockSpec((1, D_MODEL), lambda i, j: (0, 0)),
                  pl.BlockSpec((1, 8, D_MODEL), lambda i, j: (_group_of_tile(i, tm), 0, 0)),
                  pl.BlockSpec((_PARTS * D_MODEL, tn), lambda i, j: (0, j)),
                  pl.BlockSpec((1, tn), lambda i, j: (0, j))],
        out_specs=pl.BlockSpec((tm, tn), lambda i, j: (i, j)),
        out_shape=jax.ShapeDtypeStruct((N_TOK, n), F32),
        scratch_shapes=[pltpu.VMEM((tm, _PARTS * D_MODEL), BF16)],
        compiler_params=_params("parallel", "arbitrary"),
        name="inproj_shift" if shift else "inproj",
    )(x, g, mod, _split_weight(w), mu)


def _resid_kernel(*refs, gate_row, n_act):
    a_refs = refs[:n_act]
    w_ref, x_ref, mod_ref, o_ref, a3_ref = refs[n_act:]
    k_total = a3_ref.shape[1] // _PARTS

    @pl.when(pl.program_id(1) == 0)
    def _():
        k0 = 0
        for a_ref in a_refs:
            _store_split(a3_ref, a_ref[...], k_total, k0)
            k0 += a_ref.shape[1]

    o_ref[...] = x_ref[...] + mod_ref[0, gate_row:gate_row + 1, :] * _dot_split(a3_ref[...], w_ref[...])


def _resid_proj(acts, w, x, mod, gate_row):
    tm, tn = 512, 256
    k = sum(a.shape[1] for a in acts)
    return pl.pallas_call(
        functools.partial(_resid_kernel, gate_row=gate_row, n_act=len(acts)),
        grid=(N_TOK // tm, D_MODEL // tn),
        in_specs=[pl.BlockSpec((tm, a.shape[1]), lambda i, j: (i, 0)) for a in acts]
        + [pl.BlockSpec((_PARTS * k, tn), lambda i, j: (0, j)),
           pl.BlockSpec((tm, tn), lambda i, j: (i, j)),
           pl.BlockSpec((1, 8, tn), lambda i, j: (_group_of_tile(i, tm), 0, j))],
        out_specs=pl.BlockSpec((tm, tn), lambda i, j: (i, j)),
        out_shape=jax.ShapeDtypeStruct((N_TOK, D_MODEL), F32),
        scratch_shapes=[pltpu.VMEM((tm, _PARTS * k), BF16)],
        compiler_params=_params("parallel", "arbitrary"),
        name="resid_proj",
    )(*acts, _split_weight(w), x, mod)


def _ffn_up_kernel(x_ref, g_ref, mod_ref, wa_ref, wb_ref, cwa_ref, cwb_ref, cba_ref, cbb_ref, o_ref, h_ref):
    i = pl.program_id(0)

    @pl.when(pl.program_id(1) == 0)
    def _():
        _store_split(h_ref, _norm_mod(x_ref[...], g_ref[...], mod_ref[0], 3), D_MODEL)

    h = h_ref[...]

    def conv(w_ref, cw_ref, cb_ref):
        u = _dot_split(h, w_ref[...])
        prev, nxt = _seq_neighbours(u, i, _TM_SEQ)
        return prev * cw_ref[0:1, :] + u * cw_ref[1:2, :] + nxt * cw_ref[2:3, :] + cb_ref[...]

    o_ref[...] = _silu(conv(wa_ref, cwa_ref, cba_ref)) * conv(wb_ref, cwb_ref, cbb_ref)


def _ffn_up(x, g, mod, up, cw, cb):
    tm, tn = _TM_SEQ, 256
    nb = D_FF // tn
    cb = cb.reshape(1, 2 * D_FF)
    up3 = _split_weight(up)
    return pl.pallas_call(
        _ffn_up_kernel,
        grid=(N_TOK // tm, nb),
        in_specs=[pl.BlockSpec((tm, D_MODEL), lambda i, j: (i, 0)),
                  pl.BlockSpec((1, D_MODEL), lambda i, j: (0, 0)),
                  pl.BlockSpec((1, 8, D_MODEL), lambda i, j: (_group_of_tile(i, tm), 0, 0)),
                  pl.BlockSpec((_PARTS * D_MODEL, tn), lambda i, j: (0, j)),
                  pl.BlockSpec((_PARTS * D_MODEL, tn), lambda i, j: (0, j + nb)),
                  pl.BlockSpec((3, tn), lambda i, j: (0, j)),
                  pl.BlockSpec((3, tn), lambda i, j: (0, j + nb)),
                  pl.BlockSpec((1, tn), lambda i, j: (0, j)),
                  pl.BlockSpec((1, tn), lambda i, j: (0, j + nb))],
        out_specs=pl.BlockSpec((tm, tn), lambda i, j: (i, j)),
        out_shape=jax.ShapeDtypeStruct((N_TOK, D_FF), F32),
        scratch_shapes=[pltpu.VMEM((tm, _PARTS * D_MODEL), BF16)],
        compiler_params=_params("parallel", "arbitrary"),
        name="ffn_up",
    )(x, g, mod, up3, up3, cw, cw, cb, cb)


def _rope(y, c, s1, s2, half):
    return y * c + pltpu.roll(y, half, axis=1) * s1 + pltpu.roll(y, LANES - half, axis=1) * s2


def _rope_tables(cos, sin, first, n_pairs, groups):
    n = cos.shape[0]
    c = jnp.ones((n, LANES), F32)
    s1 = jnp.zeros((n, LANES), F32)
    s2 = jnp.zeros((n, LANES), F32)
    for g0 in groups:
        a = g0 + first
        c = c.at[:, a:a + n_pairs].set(cos).at[:, a + n_pairs:a + 2 * n_pairs].set(cos)
        s1 = s1.at[:, a + n_pairs:a + 2 * n_pairs].set(sin)
        s2 = s2.at[:, a:a + n_pairs].set(-sin)

    def all_rows(t, ident):
        ctx = jnp.broadcast_to(ident, (N_CTX, LANES))
        return jnp.concatenate([ctx] + [t] * DEC_BATCH, axis=0)

    return (all_rows(c, jnp.ones((1, LANES), F32)), all_rows(s1, jnp.zeros((1, LANES), F32)),
            all_rows(s2, jnp.zeros((1, LANES), F32)))


def _axial_angles(rot_dim):
    t = np.arange(DEC_SEQ)
    row, col = t // GRID_W, t % GRID_W
    n_freq = rot_dim // 4
    inv = jnp.asarray(ROPE_BASE, F32) ** (-jnp.arange(n_freq, dtype=F32) / n_freq)
    ang = jnp.concatenate([jnp.asarray(row, F32)[:, None] * inv, jnp.asarray(col, F32)[:, None] * inv], -1)
    return jnp.cos(ang), jnp.sin(ang)


def _deinterleave(n):
    return np.concatenate([np.arange(0, n, 2), np.arange(1, n, 2)])


def _mla_q_kernel(cq_ref, qnorm_ref, w_ref, qn_ref, c_ref, s1_ref, s2_ref, o_ref):
    xn = _rms(cq_ref[...], MLA_Q_RANK) * qnorm_ref[...]
    y = _dot(xn, w_ref[...])
    c, s1, s2 = c_ref[...], s1_ref[...], s2_ref[...]
    for h in range(MLA_HEADS):
        yh = y[:, h * LANES:(h + 1) * LANES]
        yh = _rms(yh, MLA_QK) * qn_ref[...]
        o_ref[:, h * LANES:(h + 1) * LANES] = _rope(yh, c, s1, s2, MLA_ROPE // 2)


def _mla_q(p, q_norm, w_uq_p, qn_p, tabs):
    tm = 512
    hw = MLA_HEADS * LANES
    tab_spec = pl.BlockSpec((tm, LANES), lambda i: (i, 0))
    return pl.pallas_call(
        _mla_q_kernel,
        grid=(N_TOK // tm,),
        in_specs=[pl.BlockSpec((tm, MLA_Q_RANK), lambda i: (i, 0)),
                  pl.BlockSpec((1, MLA_Q_RANK), lambda i: (0, 0)),
                  pl.BlockSpec((MLA_Q_RANK, hw), lambda i: (0, 0)),
                  pl.BlockSpec((1, LANES), lambda i: (0, 0)),
                  tab_spec, tab_spec, tab_spec],
        out_specs=pl.BlockSpec((tm, hw), lambda i: (i, 0)),
        out_shape=jax.ShapeDtypeStruct((N_TOK, hw), F32),
        compiler_params=_params("parallel"),
        name="mla_q",
    )(p, q_norm, w_uq_p, qn_p, *tabs)


def _mla_kv_kernel(ckv_ref, kr_ref, kvn_ref, wk_ref, wv_ref, kn_ref, c_ref, s1_ref, s2_ref,
                   k_ref, v_ref, ckvn_ref, *, norm_ckv):
    ckv = ckv_ref[...]
    if norm_ckv:
        ckv = _rms(ckv, MLA_KV_RANK) * kvn_ref[...]
    ckvn_ref[...] = ckv
    kk = _dot(ckv, wk_ref[...])
    v_ref[...] = _dot(ckv, wv_ref[...])
    kr = kr_ref[...]
    c, s1, s2 = c_ref[...], s1_ref[...], s2_ref[...]
    for h in range(MLA_HEADS):
        kh = kk[:, h * LANES:(h + 1) * LANES] + kr
        kh = _rms(kh, MLA_QK) * kn_ref[...]
        k_ref[:, h * LANES:(h + 1) * LANES] = _rope(kh, c, s1, s2, MLA_ROPE // 2)


def _mla_kv(ckv_src, ckv_blk, kr_src, kr_blk, kv_norm, wk_p, wv_p, kn_p, tabs, n_rows, norm_ckv):
    tm = 512
    hw = MLA_HEADS * LANES
    tab_spec = pl.BlockSpec((tm, LANES), lambda i: (i, 0))
    return pl.pallas_call(
        functools.partial(_mla_kv_kernel, norm_ckv=norm_ckv),
        grid=(n_rows // tm,),
        in_specs=[pl.BlockSpec((tm, LANES), lambda i: (i, ckv_blk)),
                  pl.BlockSpec((tm, LANES), lambda i: (i, kr_blk)),
                  pl.BlockSpec((1, LANES), lambda i: (0, 0)),
                  pl.BlockSpec((MLA_KV_RANK, hw), lambda i: (0, 0)),
                  pl.BlockSpec((MLA_KV_RANK, hw), lambda i: (0, 0)),
                  pl.BlockSpec((1, LANES), lambda i: (0, 0)),
                  tab_spec, tab_spec, tab_spec],
        out_specs=[pl.BlockSpec((tm, hw), lambda i: (i, 0)),
                   pl.BlockSpec((tm, hw), lambda i: (i, 0)),
                   pl.BlockSpec((tm, LANES), lambda i: (i, 0))],
        out_shape=[jax.ShapeDtypeStruct((n_rows, hw), F32),
                   jax.ShapeDtypeStruct((n_rows, hw), F32),
                   jax.ShapeDtypeStruct((n_rows, LANES), F32)],
        compiler_params=_params("parallel"),
        name="mla_kv",
    )(ckv_src, kr_src, kv_norm, wk_p, wv_p, kn_p, *tabs)


def _softmax_rows(s):
    p = jnp.exp(s - jnp.max(s, axis=-1, keepdims=True))
    return p, jnp.sum(p, axis=-1, keepdims=True)


def _mla_attn_kernel(q_ref, k_ref, v_ref, o_ref):
    scale = MLA_QK ** -0.5
    outs = []
    for h in range(2):
        sl = slice(h * LANES, (h + 1) * LANES)
        p, l = _softmax_rows(_dot_nt_bf16(q_ref[:, sl], k_ref[:, sl]) * scale)
        outs.append(_dot_bf16(p, v_ref[:, sl]) / l)
    o_ref[...] = outs[0] + pltpu.roll(outs[1], MLA_V, axis=1)


def _mla_attn(q, k, v, batch, nq, nk, q_row0, tq):
    nqb = nq // tq
    qb0 = q_row0 // tq
    return pl.pallas_call(
        _mla_attn_kernel,
        grid=(batch, MLA_HEADS // 2, nqb),
        in_specs=[pl.BlockSpec((tq, 2 * LANES), lambda b, h, i: (qb0 + b * nqb + i, h)),
                  pl.BlockSpec((nk, 2 * LANES), lambda b, h, i: (b, h)),
                  pl.BlockSpec((nk, 2 * LANES), lambda b, h, i: (b, h))],
        out_specs=pl.BlockSpec((tq, LANES), lambda b, h, i: (b * nqb + i, h)),
        out_shape=jax.ShapeDtypeStruct((batch * nq, MLA_HEADS * MLA_V), F32),
        compiler_params=_params("parallel", "parallel", "arbitrary"),
        name="mla_attn",
    )(q, k, v)


def _ret_kernel(lg_ref, q_ref, k_ref, v_ref, rg_ref, s0_ref, gn_ref, o_ref, st_ref, *, n, tq):
    b, pair, qi = pl.program_id(0), pl.program_id(1), pl.program_id(2)
    q = q_ref[...]
    k = k_ref[...] * (RET_DK ** -0.5)
    lo = _lane_lo((1, LANES))
    row = (qi * tq + lax.broadcasted_iota(jnp.int32, (tq, 1), 0)).astype(F32)
    col = lax.broadcasted_iota(jnp.int32, (1, n), 1).astype(F32)
    diff = row - col
    for h in range(2):
        lgf = lg_ref[0, 2 * pair + h]
        lgb = lg_ref[1, 2 * pair + h]
        mask = lo if h == 0 else jnp.logical_not(lo)
        qh = jnp.where(mask, q, 0.0)
        vh = v_ref[:, h * LANES:(h + 1) * LANES]
        decay = (jnp.where(diff >= 0, jnp.exp(lgf * jnp.maximum(diff, 0.0)), 0.0)
                 + jnp.where(diff <= 0, jnp.exp(lgb * jnp.maximum(-diff, 0.0)), 0.0))
        o = _dot(_dot_nt(qh, k) * decay, vh)
        o = o + _dot(qh * jnp.exp(lgf * (row + 1.0)), s0_ref[0, 0])
        o = o + _dot(qh * jnp.exp(lgb * (n - row)), s0_ref[0, 1])
        y = _rms(o, RET_DV) * gn_ref[:, h * LANES:(h + 1) * LANES]
        o_ref[:, h * LANES:(h + 1) * LANES] = _silu(rg_ref[:, h * LANES:(h + 1) * LANES]) * y

    @pl.when(qi == 0)
    def _():
        pos = lax.broadcasted_iota(jnp.int32, (n, 1), 0).astype(F32)
        for d in range(2):
            acc = None
            for h in range(2):
                lg = lg_ref[d, 2 * pair + h]
                mask = lo if h == 0 else jnp.logical_not(lo)
                expo = (n - 1.0 - pos) if d == 0 else pos
                kd = jnp.where(mask, k * jnp.exp(lg * expo), 0.0)
                term = _dot_tn(kd, v_ref[:, h * LANES:(h + 1) * LANES])
                acc = term if acc is None else acc + term
            lg_rows = jnp.where(lax.broadcasted_iota(jnp.int32, (LANES, 1), 0) < 64,
                                lg_ref[d, 2 * pair], lg_ref[d, 2 * pair + 1])
            st_ref[0, d] = acc + s0_ref[0, d] * jnp.exp(lg_rows * n)


def _retention(log_g, p, q_blk, k_blk, v_blk, g_blk, s0, gn, batch, n, row0, tq):
    nqb = n // tq
    qb0 = row0 // tq
    kb0 = row0 // n
    pairs = RET_HEADS // 2
    return pl.pallas_call(
        functools.partial(_ret_kernel, n=n, tq=tq),
        grid=(batch, pairs, nqb),
        in_specs=[pl.BlockSpec(memory_space=pltpu.SMEM),
                  pl.BlockSpec((tq, LANES), lambda b, h, i: (qb0 + b * nqb + i, q_blk + h)),
                  pl.BlockSpec((n, LANES), lambda b, h, i: (kb0 + b, k_blk + h)),
                  pl.BlockSpec((n, 2 * LANES), lambda b, h, i: (kb0 + b, v_blk // 2 + h)),
                  pl.BlockSpec((tq, 2 * LANES), lambda b, h, i: (qb0 + b * nqb + i, g_blk // 2 + h)),
                  pl.BlockSpec((1, 2, LANES, LANES), lambda b, h, i: (b, 0, h, 0)),
                  pl.BlockSpec((1, 2 * LANES), lambda b, h, i: (0, h))],
        out_specs=[pl.BlockSpec((tq, 2 * LANES), lambda b, h, i: (b * nqb + i, h)),
                   pl.BlockSpec((1, 2, LANES, LANES), lambda b, h, i: (b, 0, h, 0))],
        out_shape=[jax.ShapeDtypeStruct((batch * n, RET_HEADS * RET_DV), F32),
                   jax.ShapeDtypeStruct((batch, 2, RET_HEADS * RET_DK, RET_DV), F32)],
        compiler_params=_params("parallel", "parallel", "arbitrary"),
        name="retention",
    )(log_g, p, p, p, p, s0, gn)


def _diff_qk_kernel(q_ref, k_ref, qn_ref, kn_ref, c_ref, s1_ref, s2_ref, qo_ref, ko_ref):
    c, s1, s2 = c_ref[...], s1_ref[...], s2_ref[...]
    for src, gain, dst in ((q_ref, qn_ref, qo_ref), (k_ref, kn_ref, ko_ref)):
        for h in range(DIFF_HEADS):
            sl = slice(h * LANES, (h + 1) * LANES)
            y = src[:, sl]
            y = y * lax.rsqrt(_seg64_sum(y * y) * (1.0 / DIFF_DH) + EPS) * gain[...]
            dst[:, sl] = _rope(y, c, s1, s2, DIFF_DH // 2)


def _diff_qk(p, qn_p, kn_p, tabs):
    tm = 512
    tab_spec = pl.BlockSpec((tm, LANES), lambda i: (i, 0))
    return pl.pallas_call(
        _diff_qk_kernel,
        grid=(N_TOK // tm,),
        in_specs=[pl.BlockSpec((tm, DIFF_W), lambda i: (i, 0)),
                  pl.BlockSpec((tm, DIFF_W), lambda i: (i, 1)),
                  pl.BlockSpec((1, LANES), lambda i: (0, 0)),
                  pl.BlockSpec((1, LANES), lambda i: (0, 0)),
                  tab_spec, tab_spec, tab_spec],
        out_specs=[pl.BlockSpec((tm, DIFF_W), lambda i: (i, 0)),
                   pl.BlockSpec((tm, DIFF_W), lambda i: (i, 0))],
        out_shape=[jax.ShapeDtypeStruct((N_TOK, DIFF_W), F32)] * 2,
        compiler_params=_params("parallel"),
        name="diff_qk",
    )(p, p, qn_p, kn_p, *tabs)


def _diff_attn_kernel(lam_ref, q_ref, k_ref, v_ref, gn_ref, o_ref, *, lam_init):
    lv = lam_ref[...]
    lam = (jnp.exp(jnp.sum(lv[0:1] * lv[1:2], axis=-1, keepdims=True))
           - jnp.exp(jnp.sum(lv[2:3] * lv[3:4], axis=-1, keepdims=True)) + lam_init)
    scale = DIFF_DH ** -0.5
    q = q_ref[...]
    k = k_ref[...]
    lo = _lane_lo((1, LANES))
    kb = k.astype(BF16)
    p1, l1 = _softmax_rows(_dot_nt_bf16(jnp.where(lo, q, 0.0), kb) * scale)
    p2, l2 = _softmax_rows(_dot_nt_bf16(jnp.where(lo, 0.0, q), kb) * scale)
    w = p1 / l1 - lam * (p2 / l2)
    o = _dot_bf16(w, v_ref[...])
    o_ref[...] = _rms(o, 2 * DIFF_DH) * gn_ref[...] * (1.0 - lam_init)


def _diff_attn(lam, q, k, v, gn, batch, nq, nk, q_row0, tq, lam_init):
    nqb = nq // tq
    qb0 = q_row0 // tq
    return pl.pallas_call(
        functools.partial(_diff_attn_kernel, lam_init=lam_init),
        grid=(batch, DIFF_HEADS, nqb),
        in_specs=[pl.BlockSpec((4, DIFF_DH), lambda b, h, i: (0, 0)),
                  pl.BlockSpec((tq, LANES), lambda b, h, i: (qb0 + b * nqb + i, h)),
                  pl.BlockSpec((nk, LANES), lambda b, h, i: (b, h)),
                  pl.BlockSpec((nk, LANES), lambda b, h, i: (b, h)),
                  pl.BlockSpec((1, LANES), lambda b, h, i: (0, h))],
        out_specs=pl.BlockSpec((tq, LANES), lambda b, h, i: (b * nqb + i, h)),
        out_shape=jax.ShapeDtypeStruct((batch * nq, DIFF_W), F32),
        compiler_params=_params("parallel", "parallel", "arbitrary"),
        name="diff_attn",
    )(lam, q, k, v, gn)


def _seg64_sum_wide(x):
    return jnp.concatenate([_seg64_sum(x[:, j * LANES:(j + 1) * LANES]) for j in range(x.shape[1] // LANES)], axis=1)


_Q_R, _Q_V, _Q_KK = 0, 1, 2
_Q_DIR = 3
_N_Q = _Q_DIR + 2 * 3


def _rwkv_pre_kernel(r_ref, k_ref, v_ref, lo_ref, wup_ref, aup_ref, gup_ref, w0_ref, a0_ref, kk_ref, ka_ref, rk_ref,
                     pk_ref, g_ref, bonus_ref):
    W = RWKV_W
    col = lambda q: slice(q * W, (q + 1) * W)
    r = r_ref[...]
    k = k_ref[...]
    lora = lo_ref[...]
    kk = k * kk_ref[...]
    kkn = kk * lax.rsqrt(_seg64_sum_wide(kk * kk) + EPS)
    pk_ref[:, col(_Q_R)] = r
    pk_ref[:, col(_Q_V)] = v_ref[...]
    pk_ref[:, col(_Q_KK)] = kkn
    g_ref[...] = _dot(_sigmoid(lora[:, 2 * LANES:3 * LANES]), gup_ref[...])
    pre = w0_ref[...] + _dot(jnp.tanh(lora[:, 0:LANES]), wup_ref[...])
    decay = jnp.exp(-jnp.exp(-_softplus(-pre) - 0.5))
    a = _sigmoid(a0_ref[...] + _dot(lora[:, LANES:2 * LANES], aup_ref[...]))
    bonus = None
    for d in range(2):
        a_d = a[:, col(d)]
        k_d = k * (1.0 + (a_d - 1.0) * ka_ref[...])
        pk_ref[:, col(_Q_DIR + 3 * d)] = decay[:, col(d)]
        pk_ref[:, col(_Q_DIR + 3 * d + 1)] = k_d
        pk_ref[:, col(_Q_DIR + 3 * d + 2)] = kkn * a_d
        t = _seg64_sum_wide(r * k_d * rk_ref[...])
        bonus = t if bonus is None else bonus + t
    bonus_ref[...] = bonus


def _rwkv_pre(p, r_blk, k_blk, v_blk, lo_blk, wup_bd, aup_bd, gup, w0, a0, k_k, k_a, r_k):
    tm = 256
    w = RWKV_W
    row = lambda n: pl.BlockSpec((1, n), lambda i: (0, 0))
    full = lambda a, b: pl.BlockSpec((a, b), lambda i: (0, 0))
    return pl.pallas_call(
        _rwkv_pre_kernel,
        grid=(N_TOK // tm,),
        in_specs=[pl.BlockSpec((tm, w), lambda i: (i, r_blk)),
                  pl.BlockSpec((tm, w), lambda i: (i, k_blk)),
                  pl.BlockSpec((tm, w), lambda i: (i, v_blk)),
                  pl.BlockSpec((tm, 3 * LANES), lambda i: (i, lo_blk)),
                  full(LANES, 2 * w), full(LANES, 2 * w), full(LANES, w),
                  row(2 * w), row(2 * w), row(w), row(w), row(w)],
        out_specs=[pl.BlockSpec((tm, _N_Q * w), lambda i: (i, 0))] + [pl.BlockSpec((tm, w), lambda i: (i, 0))] * 2,
        out_shape=[jax.ShapeDtypeStruct((N_TOK, _N_Q * w), F32)] + [jax.ShapeDtypeStruct((N_TOK, w), F32)] * 2,
        compiler_params=_params("parallel"),
        name="rwkv_pre",
    )(p, p, p, p, wup_bd, aup_bd, gup, w0, a0, k_k, k_a, r_k)


_SCAN_CHUNK = 32
_SCAN_UNROLL = 16


def _rwkv_first_sa(s_ref, sa_ref, kk, n_k):
    nv = s_ref.shape[1]
    chunk = min(_SCAN_CHUNK, nv)
    for c0 in range(0, nv, chunk):
        def body(k, acc):
            return acc + s_ref[k, c0:c0 + chunk, :] * kk(k)
        sa_ref[c0:c0 + chunk, :] = lax.fori_loop(0, n_k, body, jnp.zeros((chunk, LANES), F32), unroll=_SCAN_UNROLL)


def _rwkv_step(s_ref, sa_ref, kk_next, w, kd, b, r, v_at, n_k):
    nv = s_ref.shape[1]
    chunk = min(_SCAN_CHUNK, nv)
    ys = []
    for c0 in range(0, nv, chunk):
        sa = sa_ref[c0:c0 + chunk, :]
        vc = v_at(c0, chunk)

        def body(k, acc):
            y_acc, sa_acc = acc
            s_new = s_ref[k, c0:c0 + chunk, :] * w(k) - sa * b(k) + vc * kd(k)
            s_ref[k, c0:c0 + chunk, :] = s_new
            return y_acc + s_new * r(k), sa_acc + s_new * kk_next(k)

        zero = jnp.zeros((chunk, LANES), F32)
        y_acc, sa_acc = lax.fori_loop(0, n_k, body, (zero, zero), unroll=_SCAN_UNROLL)
        sa_ref[c0:c0 + chunk, :] = sa_acc
        ys.append(y_acc)
    return ys[0] if len(ys) == 1 else jnp.concatenate(ys, axis=0)


_CTX_TB = 32


def _rwkv_scan_ctx_kernel(xs_ref, xd_ref, y_ref, st_ref, s_ref, sa_ref):
    d = pl.program_id(0)
    tb = pl.program_id(1)
    step_t = lambda i: jnp.where(d == 0, i, _CTX_TB - 1 - i)

    @pl.when(tb == 0)
    def _():
        s_ref[...] = jnp.zeros_like(s_ref)

    t0 = step_t(0)
    _rwkv_first_sa(s_ref, sa_ref, lambda k: xs_ref[t0, _Q_KK, pl.ds(k, 1), :], RWKV_HS)

    def step(i, carry):
        t = step_t(i)
        tn = step_t(jnp.minimum(i + 1, _CTX_TB - 1))
        shared = lambda q, tt: (lambda k: xs_ref[tt, q, pl.ds(k, 1), :])
        per_dir = lambda q: (lambda k: xd_ref[t, q, pl.ds(k, 1), :])
        y_ref[t, 0] = _rwkv_step(s_ref, sa_ref, shared(_Q_KK, tn), per_dir(0), per_dir(1), per_dir(2),
                                 shared(_Q_R, t), lambda c0, n: xs_ref[t, _Q_V, pl.ds(c0, n), :], RWKV_HS)
        return carry

    lax.fori_loop(0, _CTX_TB, step, 0)

    @pl.when(tb == pl.num_programs(1) - 1)
    def _():
        st_ref[0] = s_ref[...]


def _rwkv_scan_ctx(pkt):
    nt = SEQ // _CTX_TB
    hs = RWKV_HS
    tblk = lambda d, tb: jnp.where(d == 0, tb, nt - 1 - tb)
    return pl.pallas_call(
        _rwkv_scan_ctx_kernel,
        grid=(2, nt),
        in_specs=[pl.BlockSpec((_CTX_TB, 3, hs, LANES), lambda d, tb: (tblk(d, tb), 0, 0, 0)),
                  pl.BlockSpec((_CTX_TB, 3, hs, LANES), lambda d, tb: (tblk(d, tb), 1 + d, 0, 0))],
        out_specs=[pl.BlockSpec((_CTX_TB, 1, hs, LANES), lambda d, tb: (tblk(d, tb), d, 0, 0)),
                   pl.BlockSpec((1, hs, hs, LANES), lambda d, tb: (d, 0, 0, 0))],
        out_shape=[jax.ShapeDtypeStruct((SEQ, 2, hs, LANES), F32),
                   jax.ShapeDtypeStruct((2, hs, hs, LANES), F32)],
        scratch_shapes=[pltpu.VMEM((hs, hs, LANES), F32), pltpu.VMEM((hs, LANES), F32)],
        compiler_params=_params("parallel", "arbitrary"),
        name="rwkv_scan_ctx",
    )(pkt, pkt)


_LAT_TB = 32
_LAT_VSPLIT = 4
_LAT_STATES = 2 * DEC_BATCH * RWKV_HEADS
_LAT_VROWS = RWKV_HS // _LAT_VSPLIT


_LAT_NQ = 6


def _rwkv_scan_lat_kernel(x_ref, s0_ref, y_ref, s_ref, sa_ref, kt_ref, v_ref, ys_ref):
    @pl.when(pl.program_id(0) == 0)
    def _():
        s_ref[...] = s0_ref[...]

    group = lax.broadcasted_iota(jnp.int32, (_LAT_VROWS, LANES), 1) // _LAT_STATES

    def load_t(t, carry):
        for pair in range(_LAT_NQ // 2):
            xt = jnp.concatenate([x_ref[t, pair]] * _LAT_VSPLIT, axis=0).T
            kt_ref[t, 2 * pair] = xt[:RWKV_HS]
            if 2 * pair + 1 < _LAT_NQ - 1:
                kt_ref[t, 2 * pair + 1] = xt[RWKV_HS:]
            else:
                v = jnp.zeros((_LAT_VROWS, LANES), F32)
                for g in range(_LAT_VSPLIT):
                    r0 = RWKV_HS + g * _LAT_VROWS
                    v = jnp.where(group == g, xt[r0:r0 + _LAT_VROWS, :], v)
                v_ref[t] = v
        return carry

    lax.fori_loop(0, _LAT_TB, load_t, 0, unroll=4)

    _rwkv_first_sa(s_ref, sa_ref, lambda k: kt_ref[0, 0, pl.ds(k, 1), :], RWKV_HS)

    def step(t, carry):
        tn = jnp.minimum(t + 1, _LAT_TB - 1)
        row = lambda q, tt: (lambda k: kt_ref[tt, q, pl.ds(k, 1), :])
        ys_ref[t] = _rwkv_step(s_ref, sa_ref, row(0, tn), row(1, t), row(2, t), row(3, t), row(4, t),
                               lambda c0, n: v_ref[t, pl.ds(c0, n), :], RWKV_HS)
        return carry

    lax.fori_loop(0, _LAT_TB, step, 0)

    def store_t(i, carry):
        rows = [jnp.where(group == g, ys_ref[2 * i + j], 0.0) for j in range(2) for g in range(_LAT_VSPLIT)]
        z = jnp.concatenate(rows, axis=0).T
        y_ref[i] = (z[0:_LAT_STATES] + z[_LAT_STATES:2 * _LAT_STATES]
                    + z[2 * _LAT_STATES:3 * _LAT_STATES] + z[3 * _LAT_STATES:4 * _LAT_STATES])
        return carry

    lax.fori_loop(0, _LAT_TB // 2, store_t, 0, unroll=4)


def _rwkv_scan_lat(x, s0):
    hs = RWKV_HS
    nv = _LAT_VROWS
    return pl.pallas_call(
        _rwkv_scan_lat_kernel,
        grid=(DEC_SEQ // _LAT_TB,),
        in_specs=[pl.BlockSpec((_LAT_TB, _LAT_NQ // 2, _LAT_STATES, 2 * hs), lambda tb: (tb, 0, 0, 0)),
                  pl.BlockSpec((hs, nv, LANES), lambda tb: (0, 0, 0))],
        out_specs=pl.BlockSpec((_LAT_TB // 2, _LAT_STATES, 2 * hs), lambda tb: (tb, 0, 0)),
        out_shape=jax.ShapeDtypeStruct((DEC_SEQ // 2, _LAT_STATES, 2 * hs), F32),
        scratch_shapes=[pltpu.VMEM((hs, nv, LANES), F32), pltpu.VMEM((nv, LANES), F32),
                        pltpu.VMEM((_LAT_TB, _LAT_NQ - 1, hs, LANES), F32),
                        pltpu.VMEM((_LAT_TB, nv, LANES), F32), pltpu.VMEM((_LAT_TB, nv, LANES), F32)],
        compiler_params=_params("arbitrary"),
        name="rwkv_scan_lat",
    )(x, s0)


def _rwkv_post_kernel(yf_ref, yb_ref, bonus_ref, v_ref, g_ref, gn_ref, o_ref):
    y = yf_ref[...] + yb_ref[...]
    y = y * lax.rsqrt(_seg64_sum_wide(y * y) * (1.0 / RWKV_HS) + EPS) * gn_ref[...]
    o_ref[...] = (y + bonus_ref[...] * v_ref[...]) * g_ref[...]


def _rwkv_post(y, bonus, p, v_blk, g, gn):
    tm = 512
    w = RWKV_W
    spec = pl.BlockSpec((tm, w), lambda i: (i, 0))
    return pl.pallas_call(
        _rwkv_post_kernel,
        grid=(N_TOK // tm,),
        in_specs=[spec, pl.BlockSpec((tm, w), lambda i: (i, 1)), spec,
                  pl.BlockSpec((tm, w), lambda i: (i, v_blk)), spec,
                  pl.BlockSpec((1, w), lambda i: (0, 0))],
        out_specs=spec,
        out_shape=jax.ShapeDtypeStruct((N_TOK, w), F32),
        compiler_params=_params("parallel"),
        name="rwkv_post",
    )(y, y, bonus, p, g, gn)


def _ctx_scan_operands(pk):
    x = pk[:N_CTX].reshape(BATCH, SEQ, _N_Q, RWKV_HEADS, RWKV_HS)
    return jnp.transpose(x, (1, 2, 4, 0, 3)).reshape(SEQ, _N_Q, RWKV_HS, LANES)


def _ctx_scan_result(y):
    y = y.reshape(SEQ, 2, RWKV_HS, BATCH, RWKV_HEADS)
    return jnp.transpose(y, (3, 0, 1, 4, 2)).reshape(N_CTX, 2 * RWKV_W)


def _value_split_layout(x):
    lead = x.shape[:-2]
    n = len(lead)
    x = x.reshape(lead + (_LAT_STATES, _LAT_VSPLIT, _LAT_VROWS))
    return jnp.transpose(x, tuple(range(n)) + (n + 2, n + 1, n)).reshape(lead + (_LAT_VROWS, LANES))


def _lat_scan_operands(pk):
    x = pk[N_CTX:]
    W = RWKV_W

    def direction(d):
        qs = (_Q_KK, _Q_DIR + 3 * d, _Q_DIR + 3 * d + 1, _Q_DIR + 3 * d + 2, _Q_R, _Q_V)
        cols = jnp.stack([x[:, q * W:(q + 1) * W] for q in qs], 1).reshape(DEC_BATCH, DEC_SEQ, _LAT_NQ, W)
        return cols if d == 0 else cols[:, ::-1]

    st = jnp.stack([direction(0), direction(1)], 0)
    st = st.reshape(2, DEC_BATCH, DEC_SEQ, _LAT_NQ // 2, 2, RWKV_HEADS, RWKV_HS)
    st = jnp.transpose(st, (2, 3, 0, 1, 5, 4, 6))
    return st.reshape(DEC_SEQ, _LAT_NQ // 2, _LAT_STATES, 2 * RWKV_HS)


def _lat_scan_result(y):
    y = jnp.swapaxes(y.reshape(DEC_SEQ // 2, _LAT_STATES, 2, RWKV_HS), 1, 2)
    y = y.reshape(DEC_SEQ, 2, DEC_BATCH, RWKV_HEADS, RWKV_HS)
    y = jnp.transpose(y, (1, 2, 0, 3, 4)).reshape(2, DEC_BATCH, DEC_SEQ, RWKV_W)
    return jnp.concatenate([y[0], y[1, :, ::-1]], -1).reshape(N_LAT, 2 * RWKV_W)


def _even_layer(x, mod, g_mix, w_in, q_norm, kv_norm, w_uq, w_ukv, qn, kn, ret_decay, ret_gn,
                cache_ckv, cache_krope, state_ret, tabs_m):
    perm_r = _deinterleave(MLA_ROPE)
    cq, ckv, krope, rq, rk, rv, rg = jnp.split(w_in, np.cumsum(
        (MLA_Q_RANK, MLA_KV_RANK, MLA_ROPE, RET_HEADS * RET_DK, RET_HEADS * RET_DK, RET_HEADS * RET_DV))[:].tolist(),
        axis=1)
    z = lambda n: jnp.zeros((D_MODEL, n), F32)
    w_p = jnp.concatenate([cq, ckv, z(MLA_NOPE), krope[:, perm_r], z(LANES - MLA_QK), rq, rk, rv, rg], axis=1)
    p = _inproj(x, g_mix, mod, w_p, jnp.zeros((1, w_p.shape[1]), F32), 512, False)
    CKV_BLK, KR_BLK, RQ_BLK, RK_BLK, RV_BLK, RG_BLK = 2, 3, 4, 6, 8, 12

    def head_pad(w, n_head, d_head, cols):
        w = w.reshape(w.shape[0], n_head, d_head)[:, :, cols]
        return jnp.pad(w, ((0, 0), (0, 0), (0, LANES - len(cols)))).reshape(w.shape[0], n_head * LANES)

    qk_cols = np.concatenate([np.arange(MLA_NOPE), MLA_NOPE + perm_r])
    w_uq_p = head_pad(w_uq, MLA_HEADS, MLA_QK, qk_cols)
    wk_p = head_pad(w_ukv, MLA_HEADS, MLA_NOPE + MLA_V, np.arange(MLA_NOPE))
    wv_p = head_pad(w_ukv, MLA_HEADS, MLA_NOPE + MLA_V, MLA_NOPE + np.arange(MLA_V))
    qn_p = jnp.pad(qn[qk_cols], (0, LANES - MLA_QK))[None]
    kn_p = jnp.pad(kn[qk_cols], (0, LANES - MLA_QK))[None]

    q = _mla_q(p, q_norm[None], w_uq_p, qn_p, tabs_m)
    k, v, ckvn = _mla_kv(p, CKV_BLK, p, KR_BLK, kv_norm[None], wk_p, wv_p, kn_p, tabs_m, N_TOK, True)

    n_c = DEC_BATCH * PAST_LEN
    kr_c = jnp.pad(cache_krope.reshape(n_c, MLA_ROPE)[:, perm_r], ((0, 0), (MLA_NOPE, LANES - MLA_QK)))
    ident = (jnp.ones((n_c, LANES), F32), jnp.zeros((n_c, LANES), F32), jnp.zeros((n_c, LANES), F32))
    k_c, v_c, _ = _mla_kv(cache_ckv.reshape(n_c, MLA_KV_RANK), 0, kr_c, 0, kv_norm[None], wk_p, wv_p, kn_p,
                          ident, n_c, False)

    hw = MLA_HEADS * LANES

    def with_cache(own, cache):
        own = own[N_CTX:].reshape(DEC_BATCH, DEC_SEQ, hw)
        return jnp.concatenate([own, cache.reshape(DEC_BATCH, PAST_LEN, hw)], 1).reshape(-1, hw)

    o_ctx = _mla_attn(q, k, v, BATCH, SEQ, SEQ, 0, SEQ)
    o_lat = _mla_attn(q, with_cache(k, k_c), with_cache(v, v_c), DEC_BATCH, DEC_SEQ, DEC_SEQ + PAST_LEN, N_CTX, 256)

    log_g = -_softplus(-ret_decay)
    gn = ret_gn[None]
    s0_ctx = jnp.zeros((BATCH, 2, RET_HEADS * RET_DK, RET_DV), F32)
    r_ctx, st_ctx = _retention(log_g, p, RQ_BLK, RK_BLK, RV_BLK, RG_BLK, s0_ctx, gn, BATCH, SEQ, 0, SEQ)
    s0_lat = state_ret.reshape(DEC_BATCH, 2, RET_HEADS * RET_DK, RET_DV)
    r_lat, _ = _retention(log_g, p, RQ_BLK, RK_BLK, RV_BLK, RG_BLK, s0_lat, gn, DEC_BATCH, DEC_SEQ, N_CTX, 256)

    mix = [jnp.concatenate([o_ctx, o_lat], 0), jnp.concatenate([r_ctx, r_lat], 0)]
    new_ckv = ckvn[:N_CTX].reshape(BATCH, SEQ, MLA_KV_RANK)
    new_krope = p[:N_CTX, KR_BLK * LANES + MLA_NOPE:KR_BLK * LANES + MLA_QK][:, np.argsort(perm_r)]
    new_krope = new_krope.reshape(BATCH, SEQ, MLA_ROPE)
    new_ret = st_ctx.reshape(BATCH, 2, RET_HEADS, RET_DK, RET_DV)
    return mix, new_ckv, new_krope, new_ret


def _odd_layer(x, mod, g_mix, w_in, qn, kn, lam, diff_gn, mu, w0, w_up, a0, a_up, g_up, k_k, k_a, r_k, gn,
               cache_k, cache_v, state_rwkv, tabs_d, lam_init):
    perm = _deinterleave(DIFF_DH)
    qk_perm = (np.arange(2 * DIFF_W).reshape(-1, DIFF_DH)[:, perm]).reshape(-1)
    w_p = jnp.concatenate([w_in[:, qk_perm], w_in[:, 2 * DIFF_W:]], axis=1)
    n_in = w_p.shape[1]
    mu_full = jnp.concatenate([jnp.zeros((3 * DIFF_W,), F32), mu])[None]
    p = _inproj(x, g_mix, mod, w_p, mu_full, 384, True)
    DV_BLK, R_BLK, K_BLK, V_BLK = 2, 3, 4, 5
    LO_BLK = (6 * RWKV_W) // (3 * LANES)

    qn_p = jnp.tile(qn[perm], 2)[None]
    kn_p = jnp.tile(kn[perm], 2)[None]
    q, k = _diff_qk(p, qn_p, kn_p, tabs_d)
    v = p[:, DV_BLK * DIFF_W:(DV_BLK + 1) * DIFF_W]

    n_c = DEC_BATCH * PAST_LEN
    k_c = cache_k.reshape(n_c, DIFF_HEADS * 2, DIFF_DH)[:, :, perm].reshape(DEC_BATCH, PAST_LEN, DIFF_W)
    v_c = cache_v.reshape(DEC_BATCH, PAST_LEN, DIFF_W)

    def with_cache(own, cache):
        return jnp.concatenate([own[N_CTX:].reshape(DEC_BATCH, DEC_SEQ, DIFF_W), cache], 1).reshape(-1, DIFF_W)

    dgn = diff_gn[None]
    o_ctx = _diff_attn(lam, q, k, v, dgn, BATCH, SEQ, SEQ, 0, SEQ, lam_init)
    o_lat = _diff_attn(lam, q, with_cache(k, k_c), with_cache(v, v_c), dgn, DEC_BATCH, DEC_SEQ, DEC_SEQ + PAST_LEN,
                       N_CTX, 256, lam_init)

    zero = jnp.zeros((RWKV_W_LORA, RWKV_W), F32)
    wup_bd = jnp.concatenate([jnp.concatenate([w_up[0], zero], 1), jnp.concatenate([zero, w_up[1]], 1)], 0)
    aup_bd = jnp.concatenate([jnp.concatenate([a_up[0], zero], 1), jnp.concatenate([zero, a_up[1]], 1)], 0)
    pk, g, bonus = _rwkv_pre(p, R_BLK, K_BLK, V_BLK, LO_BLK, wup_bd, aup_bd, g_up, w0.reshape(1, -1),
                             a0.reshape(1, -1), k_k[None], k_a[None], r_k.reshape(1, -1))

    y_ctx, st_ctx = _rwkv_scan_ctx(_ctx_scan_operands(pk))
    s0_lat = jnp.transpose(state_rwkv, (4, 1, 0, 2, 3)).reshape(RWKV_HS, _LAT_STATES, RWKV_HS)
    y_lat = _rwkv_scan_lat(_lat_scan_operands(pk), _value_split_layout(s0_lat))
    y = jnp.concatenate([_ctx_scan_result(y_ctx), _lat_scan_result(y_lat)], 0)
    rw_o = _rwkv_post(y, bonus, p, V_BLK, g, gn[None])

    mix = [jnp.concatenate([o_ctx, o_lat], 0), rw_o]
    inv = np.argsort(perm)
    new_dk = k[:N_CTX].reshape(BATCH, SEQ, DIFF_HEADS, 2, DIFF_DH)[..., inv]
    new_dv = v[:N_CTX].reshape(BATCH, SEQ, DIFF_HEADS, 2 * DIFF_DH)
    new_rwkv = jnp.transpose(st_ctx.reshape(2, RWKV_HS, RWKV_HS, BATCH, RWKV_HEADS), (3, 0, 4, 2, 1))
    return mix, new_dk, new_dv, new_rwkv


def kernel(x_prompt, x_sample, cache_mla_ckv, cache_mla_krope, state_ret, cache_diff_k, cache_diff_v, state_rwkv,
           c, c_ctx, ada_w, ada_b, norm_mix_g, norm_ffn_g, w_out, ffn_up, ffn_conv_w, ffn_conv_b, ffn_down,
           a_w_in, mla_q_norm, mla_kv_norm, mla_w_uq, mla_w_ukv, mla_qn, mla_kn, ret_decay, ret_gn,
           b_w_in, diff_qn, diff_kn, diff_lam, diff_gn, rwkv_mu, rwkv_w0, rwkv_w_up, rwkv_a0, rwkv_a_up,
           rwkv_g_up, rwkv_k_k, rwkv_k_a, rwkv_r_k, rwkv_gn):
    x = jnp.concatenate([x_prompt.reshape(N_CTX, D_MODEL), x_sample.reshape(N_LAT, D_MODEL)], 0)
    cond8 = jnp.pad(jnp.concatenate([c_ctx[None], c], 0), ((0, 8 - N_GROUPS), (0, 0)))
    mod = _modulation(cond8, ada_w, ada_b)

    cos_m, sin_m = _axial_angles(MLA_ROPE)
    cos_d, sin_d = _axial_angles(DIFF_DH)
    tabs_m = _rope_tables(cos_m, sin_m, MLA_NOPE, MLA_ROPE // 2, (0,))
    tabs_d = _rope_tables(cos_d, sin_d, 0, DIFF_DH // 2, (0, DIFF_DH))

    outs = {}
    for l in range(DEPTH):
        j = l // 2
        g_mix = norm_mix_g[l][None]
        if l % 2 == 0:
            mix, outs["ckv"], outs["krope"], outs["ret"] = _even_layer(
                x, mod[l], g_mix, a_w_in[j], mla_q_norm[j], mla_kv_norm[j], mla_w_uq[j], mla_w_ukv[j], mla_qn[j],
                mla_kn[j], ret_decay[j], ret_gn[j], cache_mla_ckv[:, j], cache_mla_krope[:, j], state_ret[:, j],
                tabs_m)
        else:
            lam_init = 0.8 - 0.6 * math.exp(-0.3 * l)
            mix, outs["dk"], outs["dv"], outs["rwkv"] = _odd_layer(
                x, mod[l], g_mix, b_w_in[j], diff_qn[j], diff_kn[j], diff_lam[j], diff_gn[j], rwkv_mu[j],
                rwkv_w0[j], rwkv_w_up[j], rwkv_a0[j], rwkv_a_up[j], rwkv_g_up[j], rwkv_k_k[j], rwkv_k_a[j],
                rwkv_r_k[j], rwkv_gn[j], cache_diff_k[:, j], cache_diff_v[:, j], state_rwkv[:, j], tabs_d, lam_init)
        x = _resid_proj(mix, w_out[l], x, mod[l], 2)
        act = _ffn_up(x, norm_ffn_g[l][None], mod[l], ffn_up[l], ffn_conv_w[l], ffn_conv_b[l])
        x = _resid_proj([act], ffn_down[l], x, mod[l], 5)

    y_prompt = x[:N_CTX].reshape(BATCH, SEQ, D_MODEL)
    y_sample = x[N_CTX:].reshape(DEC_BATCH, DEC_SEQ, D_MODEL)
    return (y_prompt, y_sample, outs["ckv"][:, None], outs["krope"][:, None], outs["ret"][:, None],
            outs["dk"][:, None], outs["dv"][:, None], outs["rwkv"][:, None])
```

```python
import functools
import math

import numpy as np
import jax
import jax.numpy as jnp
from jax import lax
from jax.experimental import pallas as pl
from jax.experimental.pallas import tpu as pltpu

D_MODEL = 1024
BATCH = 16
SEQ = 256
DEPTH = 2
DEC_BATCH = 2
DEC_SEQ = 1024
PAST_LEN = 512
GRID_W = 64
EPS = 1e-6
ROPE_BASE = 10000.0

MLA_HEADS = 8
MLA_Q_RANK = 256
MLA_KV_RANK = 128
MLA_NOPE = 64
MLA_ROPE = 32
MLA_V = 64
MLA_QK = MLA_NOPE + MLA_ROPE
RET_HEADS = 4
RET_DK = 64
RET_DV = 128
DIFF_HEADS = 4
DIFF_DH = 64
DIFF_W = DIFF_HEADS * 2 * DIFF_DH
RWKV_HEADS = 8
RWKV_HS = 64
RWKV_W = RWKV_HEADS * RWKV_HS
RWKV_W_LORA = 64
RWKV_A_LORA = 64
RWKV_G_LORA = 128
D_FF = 2816

N_CTX = BATCH * SEQ
N_LAT = DEC_BATCH * DEC_SEQ
N_TOK = N_CTX + N_LAT
N_GROUPS = 1 + DEC_BATCH

LANES = 128
VMEM_LIMIT = 56 * 1024 * 1024

_PREC = lax.Precision.HIGHEST
F32 = jnp.float32


def _dot(a, b):
    return jnp.dot(a, b, precision=_PREC, preferred_element_type=F32)


def _dot_nt(a, b):
    return lax.dot_general(a, b, (((1,), (1,)), ((), ())), precision=_PREC, preferred_element_type=F32)


def _dot_tn(a, b):
    return lax.dot_general(a, b, (((0,), (0,)), ((), ())), precision=_PREC, preferred_element_type=F32)


BF16 = jnp.bfloat16


_PARTS = 1


def _split_weight(w):
    k, n = w.shape
    tk = 512 if k % 512 == 0 else 256
    tn = 512 if n % 512 == 0 else 384 if n % 384 == 0 else 256
    out = pl.pallas_call(
        _split_weight_kernel,
        grid=(k // tk, n // tn),
        in_specs=[pl.BlockSpec((tk, tn), lambda i, j: (i, j))],
        out_specs=pl.BlockSpec((_PARTS, tk, tn), lambda i, j: (0, i, j)),
        out_shape=jax.ShapeDtypeStruct((_PARTS, k, n), BF16),
        compiler_params=_params("parallel", "parallel"),
        name="split_weight",
    )(w)
    return out.reshape(_PARTS * k, n)


def _split_weight_kernel(w_ref, o_ref):
    w = w_ref[...]
    hi = w.astype(BF16)
    o_ref[0] = hi
    if _PARTS == 3:
        o_ref[1] = hi
        o_ref[2] = (w - hi.astype(F32)).astype(BF16)


def _store_split(dst_ref, a, k_total, k0=0):
    k = a.shape[1]
    hi = a.astype(BF16)
    dst_ref[:, k0:k0 + k] = hi
    if _PARTS == 3:
        dst_ref[:, k_total + k0:k_total + k0 + k] = (a - hi.astype(F32)).astype(BF16)
        dst_ref[:, 2 * k_total + k0:2 * k_total + k0 + k] = hi


def _dot_split(a3, w3):
    return jnp.dot(a3, w3, preferred_element_type=F32)


def _dot_bf16(a, b):
    return jnp.dot(a.astype(BF16), b.astype(BF16), preferred_element_type=F32)


def _dot_nt_bf16(a, b):
    return lax.dot_general(a.astype(BF16), b.astype(BF16), (((1,), (1,)), ((), ())), preferred_element_type=F32)


def _params(*sem):
    return pltpu.CompilerParams(dimension_semantics=sem, vmem_limit_bytes=VMEM_LIMIT)


def _sigmoid(x):
    return 1.0 / (1.0 + jnp.exp(-x))


def _silu(x):
    return x * _sigmoid(x)


def _softplus(x):
    return jnp.maximum(x, 0.0) + jnp.log(1.0 + jnp.exp(-jnp.abs(x)))


def _rms(x, n):
    return x * lax.rsqrt(jnp.sum(x * x, axis=-1, keepdims=True) * (1.0 / n) + EPS)


def _lane_lo(shape):
    return lax.broadcasted_iota(jnp.int32, shape, len(shape) - 1) < 64


def _seg64_sum(x):
    lo = _lane_lo(x.shape)
    s_lo = jnp.sum(jnp.where(lo, x, 0.0), axis=-1, keepdims=True)
    s_hi = jnp.sum(jnp.where(lo, 0.0, x), axis=-1, keepdims=True)
    return jnp.where(lo, s_lo, s_hi)


def _seq_neighbours(p, tile, tile_rows):
    seq_mask = jnp.where(tile * tile_rows < N_CTX, SEQ - 1, DEC_SEQ - 1)
    pos = lax.broadcasted_iota(jnp.int32, (tile_rows, 1), 0) & seq_mask
    prev = jnp.where(pos == 0, 0.0, pltpu.roll(p, 1, axis=0))
    nxt = jnp.where(pos == seq_mask, 0.0, pltpu.roll(p, tile_rows - 1, axis=0))
    return prev, nxt


def _group_of_tile(i, tile_rows):
    row = i * tile_rows
    return jnp.where(row < N_CTX, 0, 1 + (row - N_CTX) // DEC_SEQ)


def _modulation_kernel(c_ref, w_ref, b_ref, o_ref):
    o_ref[0] = _dot(_silu(c_ref[...]), w_ref[0]) + b_ref[0]


def _modulation(cond8, ada_w, ada_b):
    tn = 512
    n = 6 * D_MODEL
    out = pl.pallas_call(
        _modulation_kernel,
        grid=(DEPTH, n // tn),
        in_specs=[pl.BlockSpec((8, D_MODEL), lambda l, j: (0, 0)),
                  pl.BlockSpec((1, D_MODEL, tn), lambda l, j: (l, 0, j)),
                  pl.BlockSpec((1, 1, tn), lambda l, j: (l, 0, j))],
        out_specs=pl.BlockSpec((1, 8, tn), lambda l, j: (l, 0, j)),
        out_shape=jax.ShapeDtypeStruct((DEPTH, 8, n), F32),
        compiler_params=_params("parallel", "parallel"),
        name="modulation",
    )(cond8, ada_w, ada_b.reshape(DEPTH, 1, n))
    m = out[:, :N_GROUPS].reshape(DEPTH, N_GROUPS, 6, D_MODEL)
    return jnp.pad(m, ((0, 0), (0, 0), (0, 2), (0, 0)))


_TM_SEQ = 1024


def _norm_mod(x, g, mod, off):
    return _rms(x, D_MODEL) * g * (1.0 + mod[off + 1:off + 2, :]) + mod[off:off + 1, :]


def _inproj_kernel(x_ref, g_ref, mod_ref, w_ref, mu_ref, o_ref, h_ref, *, shift):
    i = pl.program_id(0)

    @pl.when(pl.program_id(1) == 0)
    def _():
        _store_split(h_ref, _norm_mod(x_ref[...], g_ref[...], mod_ref[0], 0), D_MODEL)

    p = _dot_split(h_ref[...], w_ref[...])
    if shift:
        prev, nxt = _seq_neighbours(p, i, _TM_SEQ)
        p = p + (0.5 * (prev + nxt) - p) * mu_ref[...]
    o_ref[...] = p


def _inproj(x, g, mod, w, mu, tn, shift):
    n = w.shape[1]
    tm = _TM_SEQ
    return pl.pallas_call(
        functools.partial(_inproj_kernel, shift=shift),
        grid=(N_TOK // tm, n // tn),
        in_specs=[pl.BlockSpec((tm, D_MODEL), lambda i, j: (i, 0)),
                  pl.BlockSpec((1, D_MODEL), lambda i, j: (0, 0)),
                  pl.BlockSpec((1, 8, D_MODEL), lambda i, j: (_group_of_tile(i, tm), 0, 0)),
                  pl.BlockSpec((_PARTS * D_MODEL, tn), lambda i, j: (0, j)),
                  pl.BlockSpec((1, tn), lambda i, j: (0, j))],
        out_specs=pl.BlockSpec((tm, tn), lambda i, j: (i, j)),
        out_shape=jax.ShapeDtypeStruct((N_TOK, n), F32),
        scratch_shapes=[pltpu.VMEM((tm, _PARTS * D_MODEL), BF16)],
        compiler_params=_params("parallel", "arbitrary"),
        name="inproj_shift" if shift else "inproj",
    )(x, g, mod, _split_weight(w), mu)


def _resid_kernel(*refs, gate_row, n_act):
    a_refs = refs[:n_act]
    w_ref, x_ref, mod_ref, o_ref, a3_ref = refs[n_act:]
    k_total = a3_ref.shape[1] // _PARTS

    @pl.when(pl.program_id(1) == 0)
    def _():
        k0 = 0
        for a_ref in a_refs:
            _store_split(a3_ref, a_ref[...], k_total, k0)
            k0 += a_ref.shape[1]

    o_ref[...] = x_ref[...] + mod_ref[0, gate_row:gate_row + 1, :] * _dot_split(a3_ref[...], w_ref[...])


def _resid_proj(acts, w, x, mod, gate_row):
    tm, tn = 512, 256
    k = sum(a.shape[1] for a in acts)
    return pl.pallas_call(
        functools.partial(_resid_kernel, gate_row=gate_row, n_act=len(acts)),
        grid=(N_TOK // tm, D_MODEL // tn),
        in_specs=[pl.BlockSpec((tm, a.shape[1]), lambda i, j: (i, 0)) for a in acts]
        + [pl.BlockSpec((_PARTS * k, tn), lambda i, j: (0, j)),
           pl.BlockSpec((tm, tn), lambda i, j: (i, j)),
           pl.BlockSpec((1, 8, tn), lambda i, j: (_group_of_tile(i, tm), 0, j))],
        out_specs=pl.BlockSpec((tm, tn), lambda i, j: (i, j)),
        out_shape=jax.ShapeDtypeStruct((N_TOK, D_MODEL), F32),
        scratch_shapes=[pltpu.VMEM((tm, _PARTS * k), BF16)],
        compiler_params=_params("parallel", "arbitrary"),
        name="resid_proj",
    )(*acts, _split_weight(w), x, mod)


def _ffn_up_kernel(x_ref, g_ref, mod_ref, wa_ref, wb_ref, cwa_ref, cwb_ref, cba_ref, cbb_ref, o_ref, h_ref):
    i = pl.program_id(0)

    @pl.when(pl.program_id(1) == 0)
    def _():
        _store_split(h_ref, _norm_mod(x_ref[...], g_ref[...], mod_ref[0], 3), D_MODEL)

    h = h_ref[...]

    def conv(w_ref, cw_ref, cb_ref):
        u = _dot_split(h, w_ref[...])
        prev, nxt = _seq_neighbours(u, i, _TM_SEQ)
        return prev * cw_ref[0:1, :] + u * cw_ref[1:2, :] + nxt * cw_ref[2:3, :] + cb_ref[...]

    o_ref[...] = _silu(conv(wa_ref, cwa_ref, cba_ref)) * conv(wb_ref, cwb_ref, cbb_ref)


def _ffn_up(x, g, mod, up, cw, cb):
    tm, tn = _TM_SEQ, 256
    nb = D_FF // tn
    cb = cb.reshape(1, 2 * D_FF)
    up3 = _split_weight(up)
    return pl.pallas_call(
        _ffn_up_kernel,
        grid=(N_TOK // tm, nb),
        in_specs=[pl.BlockSpec((tm, D_MODEL), lambda i, j: (i, 0)),
                  pl.BlockSpec((1, D_MODEL), lambda i, j: (0, 0)),
                  pl.BlockSpec((1, 8, D_MODEL), lambda i, j: (_group_of_tile(i, tm), 0, 0)),
                  pl.BlockSpec((_PARTS * D_MODEL, tn), lambda i, j: (0, j)),
                  pl.BlockSpec((_PARTS * D_MODEL, tn), lambda i, j: (0, j + nb)),
                  pl.BlockSpec((3, tn), lambda i, j: (0, j)),
                  pl.BlockSpec((3, tn), lambda i, j: (0, j + nb)),
                  pl.BlockSpec((1, tn), lambda i, j: (0, j)),
                  pl.BlockSpec((1, tn), lambda i, j: (0, j + nb))],
        out_specs=pl.BlockSpec((tm, tn), lambda i, j: (i, j)),
        out_shape=jax.ShapeDtypeStruct((N_TOK, D_FF), F32),
        scratch_shapes=[pltpu.VMEM((tm, _PARTS * D_MODEL), BF16)],
        compiler_params=_params("parallel", "arbitrary"),
        name="ffn_up",
    )(x, g, mod, up3, up3, cw, cw, cb, cb)


def _rope(y, c, s1, s2, half):
    return y * c + pltpu.roll(y, half, axis=1) * s1 + pltpu.roll(y, LANES - half, axis=1) * s2


def _rope_tables(cos, sin, first, n_pairs, groups):
    n = cos.shape[0]
    c = jnp.ones((n, LANES), F32)
    s1 = jnp.zeros((n, LANES), F32)
    s2 = jnp.zeros((n, LANES), F32)
    for g0 in groups:
        a = g0 + first
        c = c.at[:, a:a + n_pairs].set(cos).at[:, a + n_pairs:a + 2 * n_pairs].set(cos)
        s1 = s1.at[:, a + n_pairs:a + 2 * n_pairs].set(sin)
        s2 = s2.at[:, a:a + n_pairs].set(-sin)

    def all_rows(t, ident):
        ctx = jnp.broadcast_to(ident, (N_CTX, LANES))
        return jnp.concatenate([ctx] + [t] * DEC_BATCH, axis=0)

    return (all_rows(c, jnp.ones((1, LANES), F32)), all_rows(s1, jnp.zeros((1, LANES), F32)),
            all_rows(s2, jnp.zeros((1, LANES), F32)))


def _axial_angles(rot_dim):
    t = np.arange(DEC_SEQ)
    row, col = t // GRID_W, t % GRID_W
    n_freq = rot_dim // 4
    inv = jnp.asarray(ROPE_BASE, F32) ** (-jnp.arange(n_freq, dtype=F32) / n_freq)
    ang = jnp.concatenate([jnp.asarray(row, F32)[:, None] * inv, jnp.asarray(col, F32)[:, None] * inv], -1)
    return jnp.cos(ang), jnp.sin(ang)


def _deinterleave(n):
    return np.concatenate([np.arange(0, n, 2), np.arange(1, n, 2)])


def _mla_q_kernel(cq_ref, qnorm_ref, w_ref, qn_ref, c_ref, s1_ref, s2_ref, o_ref):
    xn = _rms(cq_ref[...], MLA_Q_RANK) * qnorm_ref[...]
    y = _dot(xn, w_ref[...])
    c, s1, s2 = c_ref[...], s1_ref[...], s2_ref[...]
    for h in range(MLA_HEADS):
        yh = y[:, h * LANES:(h + 1) * LANES]
        yh = _rms(yh, MLA_QK) * qn_ref[...]
        o_ref[:, h * LANES:(h + 1) * LANES] = _rope(yh, c, s1, s2, MLA_ROPE // 2)


def _mla_q(p, q_norm, w_uq_p, qn_p, tabs):
    tm = 512
    hw = MLA_HEADS * LANES
    tab_spec = pl.BlockSpec((tm, LANES), lambda i: (i, 0))
    return pl.pallas_call(
        _mla_q_kernel,
        grid=(N_TOK // tm,),
        in_specs=[pl.BlockSpec((tm, MLA_Q_RANK), lambda i: (i, 0)),
                  pl.BlockSpec((1, MLA_Q_RANK), lambda i: (0, 0)),
                  pl.BlockSpec((MLA_Q_RANK, hw), lambda i: (0, 0)),
                  pl.BlockSpec((1, LANES), lambda i: (0, 0)),
                  tab_spec, tab_spec, tab_spec],
        out_specs=pl.BlockSpec((tm, hw), lambda i: (i, 0)),
        out_shape=jax.ShapeDtypeStruct((N_TOK, hw), F32),
        compiler_params=_params("parallel"),
        name="mla_q",
    )(p, q_norm, w_uq_p, qn_p, *tabs)


def _mla_kv_kernel(ckv_ref, kr_ref, kvn_ref, wk_ref, wv_ref, kn_ref, c_ref, s1_ref, s2_ref,
                   k_ref, v_ref, ckvn_ref, *, norm_ckv):
    ckv = ckv_ref[...]
    if norm_ckv:
        ckv = _rms(ckv, MLA_KV_RANK) * kvn_ref[...]
    ckvn_ref[...] = ckv
    kk = _dot(ckv, wk_ref[...])
    v_ref[...] = _dot(ckv, wv_ref[...])
    kr = kr_ref[...]
    c, s1, s2 = c_ref[...], s1_ref[...], s2_ref[...]
    for h in range(MLA_HEADS):
        kh = kk[:, h * LANES:(h + 1) * LANES] + kr
        kh = _rms(kh, MLA_QK) * kn_ref[...]
        k_ref[:, h * LANES:(h + 1) * LANES] = _rope(kh, c, s1, s2, MLA_ROPE // 2)


def _mla_kv(ckv_src, ckv_blk, kr_src, kr_blk, kv_norm, wk_p, wv_p, kn_p, tabs, n_rows, norm_ckv):
    tm = 512
    hw = MLA_HEADS * LANES
    tab_spec = pl.BlockSpec((tm, LANES), lambda i: (i, 0))
    return pl.pallas_call(
        functools.partial(_mla_kv_kernel, norm_ckv=norm_ckv),
        grid=(n_rows // tm,),
        in_specs=[pl.BlockSpec((tm, LANES), lambda i: (i, ckv_blk)),
                  pl.BlockSpec((tm, LANES), lambda i: (i, kr_blk)),
                  pl.BlockSpec((1, LANES), lambda i: (0, 0)),
                  pl.BlockSpec((MLA_KV_RANK, hw), lambda i: (0, 0)),
                  pl.BlockSpec((MLA_KV_RANK, hw), lambda i: (0, 0)),
                  pl.BlockSpec((1, LANES), lambda i: (0, 0)),
                  tab_spec, tab_spec, tab_spec],
        out_specs=[pl.BlockSpec((tm, hw), lambda i: (i, 0)),
                   pl.BlockSpec((tm, hw), lambda i: (i, 0)),
                   pl.BlockSpec((tm, LANES), lambda i: (i, 0))],
        out_shape=[jax.ShapeDtypeStruct((n_rows, hw), F32),
                   jax.ShapeDtypeStruct((n_rows, hw), F32),
                   jax.ShapeDtypeStruct((n_rows, LANES), F32)],
        compiler_params=_params("parallel"),
        name="mla_kv",
    )(ckv_src, kr_src, kv_norm, wk_p, wv_p, kn_p, *tabs)


def _softmax_rows(s):
    p = jnp.exp(s - jnp.max(s, axis=-1, keepdims=True))
    return p, jnp.sum(p, axis=-1, keepdims=True)


def _mla_attn_kernel(q_ref, k_ref, v_ref, o_ref):
    scale = MLA_QK ** -0.5
    outs = []
    for h in range(2):
        sl = slice(h * LANES, (h + 1) * LANES)
        p, l = _softmax_rows(_dot_nt_bf16(q_ref[:, sl], k_ref[:, sl]) * scale)
        outs.append(_dot_bf16(p, v_ref[:, sl]) / l)
    o_ref[...] = outs[0] + pltpu.roll(outs[1], MLA_V, axis=1)


def _mla_attn(q, k, v, batch, nq, nk, q_row0, tq):
    nqb = nq // tq
    qb0 = q_row0 // tq
    return pl.pallas_call(
        _mla_attn_kernel,
        grid=(batch, MLA_HEADS // 2, nqb),
        in_specs=[pl.BlockSpec((tq, 2 * LANES), lambda b, h, i: (qb0 + b * nqb + i, h)),
                  pl.BlockSpec((nk, 2 * LANES), lambda b, h, i: (b, h)),
                  pl.BlockSpec((nk, 2 * LANES), lambda b, h, i: (b, h))],
        out_specs=pl.BlockSpec((tq, LANES), lambda b, h, i: (b * nqb + i, h)),
        out_shape=jax.ShapeDtypeStruct((batch * nq, MLA_HEADS * MLA_V), F32),
        compiler_params=_params("parallel", "parallel", "arbitrary"),
        name="mla_attn",
    )(q, k, v)


def _ret_kernel(lg_ref, q_ref, k_ref, v_ref, rg_ref, s0_ref, gn_ref, o_ref, st_ref, *, n, tq):
    b, pair, qi = pl.program_id(0), pl.program_id(1), pl.program_id(2)
    q = q_ref[...]
    k = k_ref[...] * (RET_DK ** -0.5)
    lo = _lane_lo((1, LANES))
    row = (qi * tq + lax.broadcasted_iota(jnp.int32, (tq, 1), 0)).astype(F32)
    col = lax.broadcasted_iota(jnp.int32, (1, n), 1).astype(F32)
    diff = row - col
    for h in range(2):
        lgf = lg_ref[0, 2 * pair + h]
        lgb = lg_ref[1, 2 * pair + h]
        mask = lo if h == 0 else jnp.logical_not(lo)
        qh = jnp.where(mask, q, 0.0)
        vh = v_ref[:, h * LANES:(h + 1) * LANES]
        decay = (jnp.where(diff >= 0, jnp.exp(lgf * jnp.maximum(diff, 0.0)), 0.0)
                 + jnp.where(diff <= 0, jnp.exp(lgb * jnp.maximum(-diff, 0.0)), 0.0))
        o = _dot(_dot_nt(qh, k) * decay, vh)
        o = o + _dot(qh * jnp.exp(lgf * (row + 1.0)), s0_ref[0, 0])
        o = o + _dot(qh * jnp.exp(lgb * (n - row)), s0_ref[0, 1])
        y = _rms(o, RET_DV) * gn_ref[:, h * LANES:(h + 1) * LANES]
        o_ref[:, h * LANES:(h + 1) * LANES] = _silu(rg_ref[:, h * LANES:(h + 1) * LANES]) * y

    @pl.when(qi == 0)
    def _():
        pos = lax.broadcasted_iota(jnp.int32, (n, 1), 0).astype(F32)
        for d in range(2):
            acc = None
            for h in range(2):
                lg = lg_ref[d, 2 * pair + h]
                mask = lo if h == 0 else jnp.logical_not(lo)
                expo = (n - 1.0 - pos) if d == 0 else pos
                kd = jnp.where(mask, k * jnp.exp(lg * expo), 0.0)
                term = _dot_tn(kd, v_ref[:, h * LANES:(h + 1) * LANES])
                acc = term if acc is None else acc + term
            lg_rows = jnp.where(lax.broadcasted_iota(jnp.int32, (LANES, 1), 0) < 64,
                                lg_ref[d, 2 * pair], lg_ref[d, 2 * pair + 1])
            st_ref[0, d] = acc + s0_ref[0, d] * jnp.exp(lg_rows * n)


def _retention(log_g, p, q_blk, k_blk, v_blk, g_blk, s0, gn, batch, n, row0, tq):
    nqb = n // tq
    qb0 = row0 // tq
    kb0 = row0 // n
    pairs = RET_HEADS // 2
    return pl.pallas_call(
        functools.partial(_ret_kernel, n=n, tq=tq),
        grid=(batch, pairs, nqb),
        in_specs=[pl.BlockSpec(memory_space=pltpu.SMEM),
                  pl.BlockSpec((tq, LANES), lambda b, h, i: (qb0 + b * nqb + i, q_blk + h)),
                  pl.BlockSpec((n, LANES), lambda b, h, i: (kb0 + b, k_blk + h)),
                  pl.BlockSpec((n, 2 * LANES), lambda b, h, i: (kb0 + b, v_blk // 2 + h)),
                  pl.BlockSpec((tq, 2 * LANES), lambda b, h, i: (qb0 + b * nqb + i, g_blk // 2 + h)),
                  pl.BlockSpec((1, 2, LANES, LANES), lambda b, h, i: (b, 0, h, 0)),
                  pl.BlockSpec((1, 2 * LANES), lambda b, h, i: (0, h))],
        out_specs=[pl.BlockSpec((tq, 2 * LANES), lambda b, h, i: (b * nqb + i, h)),
                   pl.BlockSpec((1, 2, LANES, LANES), lambda b, h, i: (b, 0, h, 0))],
        out_shape=[jax.ShapeDtypeStruct((batch * n, RET_HEADS * RET_DV), F32),
                   jax.ShapeDtypeStruct((batch, 2, RET_HEADS * RET_DK, RET_DV), F32)],
        compiler_params=_params("parallel", "parallel", "arbitrary"),
        name="retention",
    )(log_g, p, p, p, p, s0, gn)


def _diff_qk_kernel(q_ref, k_ref, qn_ref, kn_ref, c_ref, s1_ref, s2_ref, qo_ref, ko_ref):
    c, s1, s2 = c_ref[...], s1_ref[...], s2_ref[...]
    for src, gain, dst in ((q_ref, qn_ref, qo_ref), (k_ref, kn_ref, ko_ref)):
        for h in range(DIFF_HEADS):
            sl = slice(h * LANES, (h + 1) * LANES)
            y = src[:, sl]
            y = y * lax.rsqrt(_seg64_sum(y * y) * (1.0 / DIFF_DH) + EPS) * gain[...]
            dst[:, sl] = _rope(y, c, s1, s2, DIFF_DH // 2)


def _diff_qk(p, qn_p, kn_p, tabs):
    tm = 512
    tab_spec = pl.BlockSpec((tm, LANES), lambda i: (i, 0))
    return pl.pallas_call(
        _diff_qk_kernel,
        grid=(N_TOK // tm,),
        in_specs=[pl.BlockSpec((tm, DIFF_W), lambda i: (i, 0)),
                  pl.BlockSpec((tm, DIFF_W), lambda i: (i, 1)),
                  pl.BlockSpec((1, LANES), lambda i: (0, 0)),
                  pl.BlockSpec((1, LANES), lambda i: (0, 0)),
                  tab_spec, tab_spec, tab_spec],
        out_specs=[pl.BlockSpec((tm, DIFF_W), lambda i: (i, 0)),
                   pl.BlockSpec((tm, DIFF_W), lambda i: (i, 0))],
        out_shape=[jax.ShapeDtypeStruct((N_TOK, DIFF_W), F32)] * 2,
        compiler_params=_params("parallel"),
        name="diff_qk",
    )(p, p, qn_p, kn_p, *tabs)


def _diff_attn_kernel(lam_ref, q_ref, k_ref, v_ref, gn_ref, o_ref, *, lam_init):
    lv = lam_ref[...]
    lam = (jnp.exp(jnp.sum(lv[0:1] * lv[1:2], axis=-1, keepdims=True))
           - jnp.exp(jnp.sum(lv[2:3] * lv[3:4], axis=-1, keepdims=True)) + lam_init)
    scale = DIFF_DH ** -0.5
    q = q_ref[...]
    k = k_ref[...]
    lo = _lane_lo((1, LANES))
    kb = k.astype(BF16)
    p1, l1 = _softmax_rows(_dot_nt_bf16(jnp.where(lo, q, 0.0), kb) * scale)
    p2, l2 = _softmax_rows(_dot_nt_bf16(jnp.where(lo, 0.0, q), kb) * scale)
    w = p1 / l1 - lam * (p2 / l2)
    o = _dot_bf16(w, v_ref[...])
    o_ref[...] = _rms(o, 2 * DIFF_DH) * gn_ref[...] * (1.0 - lam_init)


def _diff_attn(lam, q, k, v, gn, batch, nq, nk, q_row0, tq, lam_init):
    nqb = nq // tq
    qb0 = q_row0 // tq
    return pl.pallas_call(
        functools.partial(_diff_attn_kernel, lam_init=lam_init),
        grid=(batch, DIFF_HEADS, nqb),
        in_specs=[pl.BlockSpec((4, DIFF_DH), lambda b, h, i: (0, 0)),
                  pl.BlockSpec((tq, LANES), lambda b, h, i: (qb0 + b * nqb + i, h)),
                  pl.BlockSpec((nk, LANES), lambda b, h, i: (b, h)),
                  pl.BlockSpec((nk, LANES), lambda b, h, i: (b, h)),
                  pl.BlockSpec((1, LANES), lambda b, h, i: (0, h))],
        out_specs=pl.BlockSpec((tq, LANES), lambda b, h, i: (b * nqb + i, h)),
        out_shape=jax.ShapeDtypeStruct((batch * nq, DIFF_W), F32),
        compiler_params=_params("parallel", "parallel", "arbitrary"),
        name="diff_attn",
    )(lam, q, k, v, gn)


def _seg64_sum_wide(x):
    return jnp.concatenate([_seg64_sum(x[:, j * LANES:(j + 1) * LANES]) for j in range(x.shape[1] // LANES)], axis=1)


_Q_R, _Q_V, _Q_KK = 0, 1, 2
_Q_DIR = 3
_N_Q = _Q_DIR + 2 * 3


_N_CTX_TILES_PRE = N_CTX // 256
_LAT_SLOTS = 3


def _rwkv_pre_kernel(r_ref, k_ref, v_ref, lo_ref, wup_ref, aup_ref, gup_ref, w0_ref, a0_ref, kk_ref, ka_ref, rk_ref,
                     pk_ref, lat_ref, g_ref, bonus_ref):
    i = pl.program_id(0)
    W = RWKV_W
    col = lambda q: slice(q * W, (q + 1) * W)
    r = r_ref[...]
    k = k_ref[...]
    v = v_ref[...]
    lora = lo_ref[...]
    kk = k * kk_ref[...]
    kkn = kk * lax.rsqrt(_seg64_sum_wide(kk * kk) + EPS)
    g_ref[...] = _dot(_sigmoid(lora[:, 2 * LANES:3 * LANES]), gup_ref[...])
    pre = w0_ref[...] + _dot(jnp.tanh(lora[:, 0:LANES]), wup_ref[...])
    decay = jnp.exp(-jnp.exp(-_softplus(-pre) - 0.5))
    a = _sigmoid(a0_ref[...] + _dot(lora[:, LANES:2 * LANES], aup_ref[...]))
    per_dir = []
    bonus = None
    for d in range(2):
        a_d = a[:, col(d)]
        k_d = k * (1.0 + (a_d - 1.0) * ka_ref[...])
        per_dir.append((decay[:, col(d)], k_d, kkn * a_d))
        t = _seg64_sum_wide(r * k_d * rk_ref[...])
        bonus = t if bonus is None else bonus + t
    bonus_ref[...] = bonus

    @pl.when(i < _N_CTX_TILES_PRE)
    def _():
        pk_ref[:, col(_Q_R)] = r
        pk_ref[:, col(_Q_V)] = v
        pk_ref[:, col(_Q_KK)] = kkn
        for d in range(2):
            for j in range(3):
                pk_ref[:, col(_Q_DIR + 3 * d + j)] = per_dir[d][j]

    @pl.when(i >= _N_CTX_TILES_PRE)
    def _():
        lo = _lane_lo((1, LANES))
        for d in range(2):
            w_d, k_d, b_d = per_dir[d]
            for s, (x1, x2) in enumerate(((kkn, w_d), (k_d, b_d), (r, v))):
                for h in range(RWKV_HEADS):
                    blk = slice((h // 2) * LANES, (h // 2 + 1) * LANES)
                    if h % 2 == 0:
                        out = jnp.where(lo, x1[:, blk], pltpu.roll(x2[:, blk], RWKV_HS, axis=1))
                    else:
                        out = jnp.where(lo, pltpu.roll(x1[:, blk], RWKV_HS, axis=1), x2[:, blk])
                    lat_ref[:, (d * _LAT_SLOTS + s) * RWKV_HEADS + h, :] = out


def _rwkv_pre(p, r_blk, k_blk, v_blk, lo_blk, wup_bd, aup_bd, gup, w0, a0, k_k, k_a, r_k):
    tm = 256
    w = RWKV_W
    n_ctx = _N_CTX_TILES_PRE
    n_j = 2 * _LAT_SLOTS * RWKV_HEADS
    row = lambda n: pl.BlockSpec((1, n), lambda i: (0, 0))
    full = lambda a, b: pl.BlockSpec((a, b), lambda i: (0, 0))
    return pl.pallas_call(
        _rwkv_pre_kernel,
        grid=(N_TOK // tm,),
        in_specs=[pl.BlockSpec((tm, w), lambda i: (i, r_blk)),
                  pl.BlockSpec((tm, w), lambda i: (i, k_blk)),
                  pl.BlockSpec((tm, w), lambda i: (i, v_blk)),
                  pl.BlockSpec((tm, 3 * LANES), lambda i: (i, lo_blk)),
                  full(LANES, 2 * w), full(LANES, 2 * w), full(LANES, w),
                  row(2 * w), row(2 * w), row(w), row(w), row(w)],
        out_specs=[pl.BlockSpec((tm, _N_Q * w), lambda i: (jnp.minimum(i, n_ctx - 1), 0)),
                   pl.BlockSpec((tm, n_j, LANES), lambda i: (jnp.maximum(i - n_ctx, 0), 0, 0)),
                   pl.BlockSpec((tm, w), lambda i: (i, 0)), pl.BlockSpec((tm, w), lambda i: (i, 0))],
        out_shape=[jax.ShapeDtypeStruct((N_CTX, _N_Q * w), F32), jax.ShapeDtypeStruct((N_LAT, n_j, LANES), F32),
                   jax.ShapeDtypeStruct((N_TOK, w), F32), jax.ShapeDtypeStruct((N_TOK, w), F32)],
        compiler_params=_params("arbitrary"),
        name="rwkv_pre",
    )(p, p, p, p, wup_bd, aup_bd, gup, w0, a0, k_k, k_a, r_k)


_SCAN_CHUNK = 32
_SCAN_UNROLL = 16


def _rwkv_first_sa(s_ref, sa_ref, kk, n_k):
    nv = s_ref.shape[1]
    chunk = min(_SCAN_CHUNK, nv)
    for c0 in range(0, nv, chunk):
        def body(k, acc):
            return acc + s_ref[k, c0:c0 + chunk, :] * kk(k)
        sa_ref[c0:c0 + chunk, :] = lax.fori_loop(0, n_k, body, jnp.zeros((chunk, LANES), F32), unroll=_SCAN_UNROLL)


def _rwkv_step(s_ref, sa_ref, kk_next, w, kd, b, r, v_at, n_k):
    nv = s_ref.shape[1]
    chunk = min(_SCAN_CHUNK, nv)
    ys = []
    for c0 in range(0, nv, chunk):
        sa = sa_ref[c0:c0 + chunk, :]
        vc = v_at(c0, chunk)

        def body(k, acc):
            y_acc, sa_acc = acc
            s_new = s_ref[k, c0:c0 + chunk, :] * w(k) - sa * b(k) + vc * kd(k)
            s_ref[k, c0:c0 + chunk, :] = s_new
            return y_acc + s_new * r(k), sa_acc + s_new * kk_next(k)

        zero = jnp.zeros((chunk, LANES), F32)
        y_acc, sa_acc = lax.fori_loop(0, n_k, body, (zero, zero), unroll=_SCAN_UNROLL)
        sa_ref[c0:c0 + chunk, :] = sa_acc
        ys.append(y_acc)
    return ys[0] if len(ys) == 1 else jnp.concatenate(ys, axis=0)


_CTX_TB = 32


def _rwkv_scan_ctx_kernel(xs_ref, xd_ref, y_ref, st_ref, s_ref, sa_ref):
    d = pl.program_id(0)
    tb = pl.program_id(1)
    step_t = lambda i: jnp.where(d == 0, i, _CTX_TB - 1 - i)

    @pl.when(tb == 0)
    def _():
        s_ref[...] = jnp.zeros_like(s_ref)

    t0 = step_t(0)
    _rwkv_first_sa(s_ref, sa_ref, lambda k: xs_ref[t0, _Q_KK, pl.ds(k, 1), :], RWKV_HS)

    def step(i, carry):
        t = step_t(i)
        tn = step_t(jnp.minimum(i + 1, _CTX_TB - 1))
        shared = lambda q, tt: (lambda k: xs_ref[tt, q, pl.ds(k, 1), :])
        per_dir = lambda q: (lambda k: xd_ref[t, q, pl.ds(k, 1), :])
        y_ref[t, 0] = _rwkv_step(s_ref, sa_ref, shared(_Q_KK, tn), per_dir(0), per_dir(1), per_dir(2),
                                 shared(_Q_R, t), lambda c0, n: xs_ref[t, _Q_V, pl.ds(c0, n), :], RWKV_HS)
        return carry

    lax.fori_loop(0, _CTX_TB, step, 0)

    @pl.when(tb == pl.num_programs(1) - 1)
    def _():
        st_ref[0] = s_ref[...]


def _rwkv_scan_ctx(pkt):
    nt = SEQ // _CTX_TB
    hs = RWKV_HS
    tblk = lambda d, tb: jnp.where(d == 0, tb, nt - 1 - tb)
    return pl.pallas_call(
        _rwkv_scan_ctx_kernel,
        grid=(2, nt),
        in_specs=[pl.BlockSpec((_CTX_TB, 3, hs, LANES), lambda d, tb: (tblk(d, tb), 0, 0, 0)),
                  pl.BlockSpec((_CTX_TB, 3, hs, LANES), lambda d, tb: (tblk(d, tb), 1 + d, 0, 0))],
        out_specs=[pl.BlockSpec((_CTX_TB, 1, hs, LANES), lambda d, tb: (tblk(d, tb), d, 0, 0)),
                   pl.BlockSpec((1, hs, hs, LANES), lambda d, tb: (d, 0, 0, 0))],
        out_shape=[jax.ShapeDtypeStruct((SEQ, 2, hs, LANES), F32),
                   jax.ShapeDtypeStruct((2, hs, hs, LANES), F32)],
        scratch_shapes=[pltpu.VMEM((hs, hs, LANES), F32), pltpu.VMEM((hs, LANES), F32)],
        compiler_params=_params("parallel", "arbitrary"),
        name="rwkv_scan_ctx",
    )(pkt, pkt)


_LAT_TB = 32
_LAT_VSPLIT = 4
_LAT_STATES = 2 * DEC_BATCH * RWKV_HEADS
_LAT_VROWS = RWKV_HS // _LAT_VSPLIT


def _rwkv_scan_lat_kernel(xf_ref, xb_ref, s0_ref, yf_ref, yb_ref, s_ref, sa_ref, kt_ref, v_ref, ys_ref):
    @pl.when(pl.program_id(0) == 0)
    def _():
        s_ref[...] = s0_ref[...]

    group = lax.broadcasted_iota(jnp.int32, (_LAT_VROWS, LANES), 1) // _LAT_STATES
    nh = RWKV_HEADS

    def load_t(t, carry):
        tr = _LAT_TB - 1 - t
        for s in range(_LAT_SLOTS):
            heads = slice(s * nh, (s + 1) * nh)
            x = jnp.concatenate([xf_ref[0, t, heads, :], xf_ref[1, t, heads, :],
                                 xb_ref[0, tr, heads, :], xb_ref[1, tr, heads, :]], axis=0)
            xt = jnp.concatenate([x] * _LAT_VSPLIT, axis=0).T
            kt_ref[t, 2 * s] = xt[:RWKV_HS]
            if s < _LAT_SLOTS - 1:
                kt_ref[t, 2 * s + 1] = xt[RWKV_HS:]
            else:
                v = jnp.zeros((_LAT_VROWS, LANES), F32)
                for g in range(_LAT_VSPLIT):
                    r0 = RWKV_HS + g * _LAT_VROWS
                    v = jnp.where(group == g, xt[r0:r0 + _LAT_VROWS, :], v)
                v_ref[t] = v
        return carry

    lax.fori_loop(0, _LAT_TB, load_t, 0, unroll=4)

    _rwkv_first_sa(s_ref, sa_ref, lambda k: kt_ref[0, 0, pl.ds(k, 1), :], RWKV_HS)

    def step(t, carry):
        tn = jnp.minimum(t + 1, _LAT_TB - 1)
        row = lambda q, tt: (lambda k: kt_ref[tt, q, pl.ds(k, 1), :])
        ys_ref[t] = _rwkv_step(s_ref, sa_ref, row(0, tn), row(1, t), row(2, t), row(3, t), row(4, t),
                               lambda c0, n: v_ref[t, pl.ds(c0, n), :], RWKV_HS)
        return carry

    lax.fori_loop(0, _LAT_TB, step, 0)

    def store_t(i, carry):
        rows = [jnp.where(group == g, ys_ref[2 * i + j], 0.0) for j in range(2) for g in range(_LAT_VSPLIT)]
        z = jnp.concatenate(rows, axis=0).T
        y = (z[0:_LAT_STATES] + z[_LAT_STATES:2 * _LAT_STATES]
             + z[2 * _LAT_STATES:3 * _LAT_STATES] + z[3 * _LAT_STATES:4 * _LAT_STATES])
        yb = pltpu.roll(y[_LAT_STATES // 2:], RWKV_HS, axis=1)
        for b in range(DEC_BATCH):
            yf_ref[b, i] = y[b * nh:(b + 1) * nh]
            yb_ref[b, _LAT_TB // 2 - 1 - i] = yb[b * nh:(b + 1) * nh]
        return carry

    lax.fori_loop(0, _LAT_TB // 2, store_t, 0, unroll=4)


def _rwkv_scan_lat(lat, s0):
    hs = RWKV_HS
    nv = _LAT_VROWS
    nt = DEC_SEQ // _LAT_TB
    nj = _LAT_SLOTS * RWKV_HEADS
    x4 = lat.reshape(DEC_BATCH, DEC_SEQ, 2 * nj, LANES)
    y_shape = jax.ShapeDtypeStruct((DEC_BATCH, DEC_SEQ // 2, RWKV_HEADS, 2 * hs), F32)
    y_blk = (DEC_BATCH, _LAT_TB // 2, RWKV_HEADS, 2 * hs)
    return pl.pallas_call(
        _rwkv_scan_lat_kernel,
        grid=(nt,),
        in_specs=[pl.BlockSpec((DEC_BATCH, _LAT_TB, nj, LANES), lambda tb: (0, tb, 0, 0)),
                  pl.BlockSpec((DEC_BATCH, _LAT_TB, nj, LANES), lambda tb: (0, nt - 1 - tb, 1, 0)),
                  pl.BlockSpec((hs, nv, LANES), lambda tb: (0, 0, 0))],
        out_specs=[pl.BlockSpec(y_blk, lambda tb: (0, tb, 0, 0)),
                   pl.BlockSpec(y_blk, lambda tb: (0, nt - 1 - tb, 0, 0))],
        out_shape=[y_shape, y_shape],
        scratch_shapes=[pltpu.VMEM((hs, nv, LANES), F32), pltpu.VMEM((nv, LANES), F32),
                        pltpu.VMEM((_LAT_TB, 2 * _LAT_SLOTS - 1, hs, LANES), F32),
                        pltpu.VMEM((_LAT_TB, nv, LANES), F32), pltpu.VMEM((_LAT_TB, nv, LANES), F32)],
        compiler_params=_params("arbitrary"),
        name="rwkv_scan_lat",
    )(x4, x4, s0)


def _rwkv_post_kernel(yf_ref, yb_ref, bonus_ref, v_ref, g_ref, gn_ref, o_ref):
    y = yf_ref[...] + yb_ref[...]
    y = y * lax.rsqrt(_seg64_sum_wide(y * y) * (1.0 / RWKV_HS) + EPS) * gn_ref[...]
    o_ref[...] = (y + bonus_ref[...] * v_ref[...]) * g_ref[...]


def _rwkv_post(y, bonus, p, v_blk, g, gn):
    tm = 512
    w = RWKV_W
    spec = pl.BlockSpec((tm, w), lambda i: (i, 0))
    return pl.pallas_call(
        _rwkv_post_kernel,
        grid=(N_TOK // tm,),
        in_specs=[spec, pl.BlockSpec((tm, w), lambda i: (i, 1)), spec,
                  pl.BlockSpec((tm, w), lambda i: (i, v_blk)), spec,
                  pl.BlockSpec((1, w), lambda i: (0, 0))],
        out_specs=spec,
        out_shape=jax.ShapeDtypeStruct((N_TOK, w), F32),
        compiler_params=_params("parallel"),
        name="rwkv_post",
    )(y, y, bonus, p, g, gn)


def _ctx_scan_operands(pk):
    x = pk.reshape(BATCH, SEQ, _N_Q, RWKV_HEADS, RWKV_HS)
    return jnp.transpose(x, (1, 2, 4, 0, 3)).reshape(SEQ, _N_Q, RWKV_HS, LANES)


def _ctx_scan_result(y):
    y = y.reshape(SEQ, 2, RWKV_HS, BATCH, RWKV_HEADS)
    return jnp.transpose(y, (3, 0, 1, 4, 2)).reshape(N_CTX, 2 * RWKV_W)


def _value_split_layout(x):
    lead = x.shape[:-2]
    n = len(lead)
    x = x.reshape(lead + (_LAT_STATES, _LAT_VSPLIT, _LAT_VROWS))
    return jnp.transpose(x, tuple(range(n)) + (n + 2, n + 1, n)).reshape(lead + (_LAT_VROWS, LANES))


def _lat_scan_result(yf, yb):
    def rows(y):
        y = y.reshape(DEC_BATCH, DEC_SEQ // 2, RWKV_HEADS, 2, RWKV_HS)
        return jnp.transpose(y, (0, 1, 3, 2, 4)).reshape(N_LAT, RWKV_W)
    return jnp.concatenate([rows(yf), rows(yb)], -1)


def _even_layer(x, mod, g_mix, w_in, q_norm, kv_norm, w_uq, w_ukv, qn, kn, ret_decay, ret_gn,
                cache_ckv, cache_krope, state_ret, tabs_m):
    perm_r = _deinterleave(MLA_ROPE)
    cq, ckv, krope, rq, rk, rv, rg = jnp.split(w_in, np.cumsum(
        (MLA_Q_RANK, MLA_KV_RANK, MLA_ROPE, RET_HEADS * RET_DK, RET_HEADS * RET_DK, RET_HEADS * RET_DV))[:].tolist(),
        axis=1)
    z = lambda n: jnp.zeros((D_MODEL, n), F32)
    w_p = jnp.concatenate([cq, ckv, z(MLA_NOPE), krope[:, perm_r], z(LANES - MLA_QK), rq, rk, rv, rg], axis=1)
    p = _inproj(x, g_mix, mod, w_p, jnp.zeros((1, w_p.shape[1]), F32), 512, False)
    CKV_BLK, KR_BLK, RQ_BLK, RK_BLK, RV_BLK, RG_BLK = 2, 3, 4, 6, 8, 12

    def head_pad(w, n_head, d_head, cols):
        w = w.reshape(w.shape[0], n_head, d_head)[:, :, cols]
        return jnp.pad(w, ((0, 0), (0, 0), (0, LANES - len(cols)))).reshape(w.shape[0], n_head * LANES)

    qk_cols = np.concatenate([np.arange(MLA_NOPE), MLA_NOPE + perm_r])
    w_uq_p = head_pad(w_uq, MLA_HEADS, MLA_QK, qk_cols)
    wk_p = head_pad(w_ukv, MLA_HEADS, MLA_NOPE + MLA_V, np.arange(MLA_NOPE))
    wv_p = head_pad(w_ukv, MLA_HEADS, MLA_NOPE + MLA_V, MLA_NOPE + np.arange(MLA_V))
    qn_p = jnp.pad(qn[qk_cols], (0, LANES - MLA_QK))[None]
    kn_p = jnp.pad(kn[qk_cols], (0, LANES - MLA_QK))[None]

    q = _mla_q(p, q_norm[None], w_uq_p, qn_p, tabs_m)
    k, v, ckvn = _mla_kv(p, CKV_BLK, p, KR_BLK, kv_norm[None], wk_p, wv_p, kn_p, tabs_m, N_TOK, True)

    n_c = DEC_BATCH * PAST_LEN
    kr_c = jnp.pad(cache_krope.reshape(n_c, MLA_ROPE)[:, perm_r], ((0, 0), (MLA_NOPE, LANES - MLA_QK)))
    ident = (jnp.ones((n_c, LANES), F32), jnp.zeros((n_c, LANES), F32), jnp.zeros((n_c, LANES), F32))
    k_c, v_c, _ = _mla_kv(cache_ckv.reshape(n_c, MLA_KV_RANK), 0, kr_c, 0, kv_norm[None], wk_p, wv_p, kn_p,
                          ident, n_c, False)

    hw = MLA_HEADS * LANES

    def with_cache(own, cache):
        own = own[N_CTX:].reshape(DEC_BATCH, DEC_SEQ, hw)
        return jnp.concatenate([own, cache.reshape(DEC_BATCH, PAST_LEN, hw)], 1).reshape(-1, hw)

    o_ctx = _mla_attn(q, k, v, BATCH, SEQ, SEQ, 0, SEQ)
    o_lat = _mla_attn(q, with_cache(k, k_c), with_cache(v, v_c), DEC_BATCH, DEC_SEQ, DEC_SEQ + PAST_LEN, N_CTX, 256)

    log_g = -_softplus(-ret_decay)
    gn = ret_gn[None]
    s0_ctx = jnp.zeros((BATCH, 2, RET_HEADS * RET_DK, RET_DV), F32)
    r_ctx, st_ctx = _retention(log_g, p, RQ_BLK, RK_BLK, RV_BLK, RG_BLK, s0_ctx, gn, BATCH, SEQ, 0, SEQ)
    s0_lat = state_ret.reshape(DEC_BATCH, 2, RET_HEADS * RET_DK, RET_DV)
    r_lat, _ = _retention(log_g, p, RQ_BLK, RK_BLK, RV_BLK, RG_BLK, s0_lat, gn, DEC_BATCH, DEC_SEQ, N_CTX, 256)

    mix = [jnp.concatenate([o_ctx, o_lat], 0), jnp.concatenate([r_ctx, r_lat], 0)]
    new_ckv = ckvn[:N_CTX].reshape(BATCH, SEQ, MLA_KV_RANK)
    new_krope = p[:N_CTX, KR_BLK * LANES + MLA_NOPE:KR_BLK * LANES + MLA_QK][:, np.argsort(perm_r)]
    new_krope = new_krope.reshape(BATCH, SEQ, MLA_ROPE)
    new_ret = st_ctx.reshape(BATCH, 2, RET_HEADS, RET_DK, RET_DV)
    return mix, new_ckv, new_krope, new_ret


def _odd_layer(x, mod, g_mix, w_in, qn, kn, lam, diff_gn, mu, w0, w_up, a0, a_up, g_up, k_k, k_a, r_k, gn,
               cache_k, cache_v, state_rwkv, tabs_d, lam_init):
    perm = _deinterleave(DIFF_DH)
    qk_perm = (np.arange(2 * DIFF_W).reshape(-1, DIFF_DH)[:, perm]).reshape(-1)
    w_p = jnp.concatenate([w_in[:, qk_perm], w_in[:, 2 * DIFF_W:]], axis=1)
    n_in = w_p.shape[1]
    mu_full = jnp.concatenate([jnp.zeros((3 * DIFF_W,), F32), mu])[None]
    p = _inproj(x, g_mix, mod, w_p, mu_full, 384, True)
    DV_BLK, R_BLK, K_BLK, V_BLK = 2, 3, 4, 5
    LO_BLK = (6 * RWKV_W) // (3 * LANES)

    qn_p = jnp.tile(qn[perm], 2)[None]
    kn_p = jnp.tile(kn[perm], 2)[None]
    q, k = _diff_qk(p, qn_p, kn_p, tabs_d)
    v = p[:, DV_BLK * DIFF_W:(DV_BLK + 1) * DIFF_W]

    n_c = DEC_BATCH * PAST_LEN
    k_c = cache_k.reshape(n_c, DIFF_HEADS * 2, DIFF_DH)[:, :, perm].reshape(DEC_BATCH, PAST_LEN, DIFF_W)
    v_c = cache_v.reshape(DEC_BATCH, PAST_LEN, DIFF_W)

    def with_cache(own, cache):
        return jnp.concatenate([own[N_CTX:].reshape(DEC_BATCH, DEC_SEQ, DIFF_W), cache], 1).reshape(-1, DIFF_W)

    dgn = diff_gn[None]
    o_ctx = _diff_attn(lam, q, k, v, dgn, BATCH, SEQ, SEQ, 0, SEQ, lam_init)
    o_lat = _diff_attn(lam, q, with_cache(k, k_c), with_cache(v, v_c), dgn, DEC_BATCH, DEC_SEQ, DEC_SEQ + PAST_LEN,
                       N_CTX, 256, lam_init)

    zero = jnp.zeros((RWKV_W_LORA, RWKV_W), F32)
    wup_bd = jnp.concatenate([jnp.concatenate([w_up[0], zero], 1), jnp.concatenate([zero, w_up[1]], 1)], 0)
    aup_bd = jnp.concatenate([jnp.concatenate([a_up[0], zero], 1), jnp.concatenate([zero, a_up[1]], 1)], 0)
    pk, lat, g, bonus = _rwkv_pre(p, R_BLK, K_BLK, V_BLK, LO_BLK, wup_bd, aup_bd, g_up, w0.reshape(1, -1),
                             a0.reshape(1, -1), k_k[None], k_a[None], r_k.reshape(1, -1))

    y_ctx, st_ctx = _rwkv_scan_ctx(_ctx_scan_operands(pk))
    s0_lat = jnp.transpose(state_rwkv, (4, 1, 0, 2, 3)).reshape(RWKV_HS, _LAT_STATES, RWKV_HS)
    y_lat = _rwkv_scan_lat(lat, _value_split_layout(s0_lat))
    y = jnp.concatenate([_ctx_scan_result(y_ctx), _lat_scan_result(*y_lat)], 0)
    rw_o = _rwkv_post(y, bonus, p, V_BLK, g, gn[None])

    mix = [jnp.concatenate([o_ctx, o_lat], 0), rw_o]
    inv = np.argsort(perm)
    new_dk = k[:N_CTX].reshape(BATCH, SEQ, DIFF_HEADS, 2, DIFF_DH)[..., inv]
    new_dv = v[:N_CTX].reshape(BATCH, SEQ, DIFF_HEADS, 2 * DIFF_DH)
    new_rwkv = jnp.transpose(st_ctx.reshape(2, RWKV_HS, RWKV_HS, BATCH, RWKV_HEADS), (3, 0, 4, 2, 1))
    return mix, new_dk, new_dv, new_rwkv


def kernel(x_prompt, x_sample, cache_mla_ckv, cache_mla_krope, state_ret, cache_diff_k, cache_diff_v, state_rwkv,
           c, c_ctx, ada_w, ada_b, norm_mix_g, norm_ffn_g, w_out, ffn_up, ffn_conv_w, ffn_conv_b, ffn_down,
           a_w_in, mla_q_norm, mla_kv_norm, mla_w_uq, mla_w_ukv, mla_qn, mla_kn, ret_decay, ret_gn,
           b_w_in, diff_qn, diff_kn, diff_lam, diff_gn, rwkv_mu, rwkv_w0, rwkv_w_up, rwkv_a0, rwkv_a_up,
           rwkv_g_up, rwkv_k_k, rwkv_k_a, rwkv_r_k, rwkv_gn):
    x = jnp.concatenate([x_prompt.reshape(N_CTX, D_MODEL), x_sample.reshape(N_LAT, D_MODEL)], 0)
    cond8 = jnp.pad(jnp.concatenate([c_ctx[None], c], 0), ((0, 8 - N_GROUPS), (0, 0)))
    mod = _modulation(cond8, ada_w, ada_b)

    cos_m, sin_m = _axial_angles(MLA_ROPE)
    cos_d, sin_d = _axial_angles(DIFF_DH)
    tabs_m = _rope_tables(cos_m, sin_m, MLA_NOPE, MLA_ROPE // 2, (0,))
    tabs_d = _rope_tables(cos_d, sin_d, 0, DIFF_DH // 2, (0, DIFF_DH))

    outs = {}
    for l in range(DEPTH):
        j = l // 2
        g_mix = norm_mix_g[l][None]
        if l % 2 == 0:
            mix, outs["ckv"], outs["krope"], outs["ret"] = _even_layer(
                x, mod[l], g_mix, a_w_in[j], mla_q_norm[j], mla_kv_norm[j], mla_w_uq[j], mla_w_ukv[j], mla_qn[j],
                mla_kn[j], ret_decay[j], ret_gn[j], cache_mla_ckv[:, j], cache_mla_krope[:, j], state_ret[:, j],
                tabs_m)
        else:
            lam_init = 0.8 - 0.6 * math.exp(-0.3 * l)
            mix, outs["dk"], outs["dv"], outs["rwkv"] = _odd_layer(
                x, mod[l], g_mix, b_w_in[j], diff_qn[j], diff_kn[j], diff_lam[j], diff_gn[j], rwkv_mu[j],
                rwkv_w0[j], rwkv_w_up[j], rwkv_a0[j], rwkv_a_up[j], rwkv_g_up[j], rwkv_k_k[j], rwkv_k_a[j],
                rwkv_r_k[j], rwkv_gn[j], cache_diff_k[:, j], cache_diff_v[:, j], state_rwkv[:, j], tabs_d, lam_init)
        x = _resid_proj(mix, w_out[l], x, mod[l], 2)
        act = _ffn_up(x, norm_ffn_g[l][None], mod[l], ffn_up[l], ffn_conv_w[l], ffn_conv_b[l])
        x = _resid_proj([act], ffn_down[l], x, mod[l], 5)

    y_prompt = x[:N_CTX].reshape(BATCH, SEQ, D_MODEL)
    y_sample = x[N_CTX:].reshape(DEC_BATCH, DEC_SEQ, D_MODEL)
    return (y_prompt, y_sample, outs["ckv"][:, None], outs["krope"][:, None], outs["ret"][:, None],
            outs["dk"][:, None], outs["dv"][:, None], outs["rwkv"][:, None])
```

```python
import functools
import math

import numpy as np
import jax
import jax.numpy as jnp
from jax import lax
from jax.experimental import pallas as pl
from jax.experimental.pallas import tpu as pltpu

D_MODEL = 1024
BATCH = 16
SEQ = 256
DEPTH = 2
DEC_BATCH = 2
DEC_SEQ = 1024
PAST_LEN = 512
GRID_W = 64
EPS = 1e-6
ROPE_BASE = 10000.0

MLA_HEADS = 8
MLA_Q_RANK = 256
MLA_KV_RANK = 128
MLA_NOPE = 64
MLA_ROPE = 32
MLA_V = 64
MLA_QK = MLA_NOPE + MLA_ROPE
RET_HEADS = 4
RET_DK = 64
RET_DV = 128
DIFF_HEADS = 4
DIFF_DH = 64
DIFF_W = DIFF_HEADS * 2 * DIFF_DH
RWKV_HEADS = 8
RWKV_HS = 64
RWKV_W = RWKV_HEADS * RWKV_HS
RWKV_W_LORA = 64
RWKV_A_LORA = 64
RWKV_G_LORA = 128
D_FF = 2816

N_CTX = BATCH * SEQ
N_LAT = DEC_BATCH * DEC_SEQ
N_TOK = N_CTX + N_LAT
N_GROUPS = 1 + DEC_BATCH

LANES = 128
VMEM_LIMIT = 56 * 1024 * 1024

_PREC = lax.Precision.HIGHEST
F32 = jnp.float32


def _dot_tn(a, b):
    return lax.dot_general(a, b, (((0,), (0,)), ((), ())), precision=_PREC, preferred_element_type=F32)


BF16 = jnp.bfloat16


def _dot_bf16(a, b):
    return jnp.dot(a.astype(BF16), b.astype(BF16), preferred_element_type=F32)


def _dot_nt_bf16(a, b):
    return lax.dot_general(a.astype(BF16), b.astype(BF16), (((1,), (1,)), ((), ())), preferred_element_type=F32)


def _params(*sem):
    return pltpu.CompilerParams(dimension_semantics=sem, vmem_limit_bytes=VMEM_LIMIT)


def _sigmoid(x):
    return 1.0 / (1.0 + jnp.exp(-x))


def _silu(x):
    return x * _sigmoid(x)


def _softplus(x):
    return jnp.maximum(x, 0.0) + jnp.log(1.0 + jnp.exp(-jnp.abs(x)))


def _rms(x, n):
    return x * lax.rsqrt(jnp.sum(x * x, axis=-1, keepdims=True) * (1.0 / n) + EPS)


def _lane_lo(shape):
    return lax.broadcasted_iota(jnp.int32, shape, len(shape) - 1) < 64


def _seg64_sum(x):
    lo = _lane_lo(x.shape)
    s_lo = jnp.sum(jnp.where(lo, x, 0.0), axis=-1, keepdims=True)
    s_hi = jnp.sum(jnp.where(lo, 0.0, x), axis=-1, keepdims=True)
    return jnp.where(lo, s_lo, s_hi)


def _seq_neighbours(p, tile, tile_rows):
    seq_mask = jnp.where(tile * tile_rows < N_CTX, SEQ - 1, DEC_SEQ - 1)
    pos = lax.broadcasted_iota(jnp.int32, (tile_rows, 1), 0) & seq_mask
    prev = jnp.where(pos == 0, 0.0, pltpu.roll(p, 1, axis=0))
    nxt = jnp.where(pos == seq_mask, 0.0, pltpu.roll(p, tile_rows - 1, axis=0))
    return prev, nxt


def _group_of_tile(i, tile_rows):
    row = i * tile_rows
    return jnp.where(row < N_CTX, 0, 1 + (row - N_CTX) // DEC_SEQ)


def _modulation_kernel(c_ref, w_ref, b_ref, o_ref):
    o_ref[0] = _dot_bf16(_silu(c_ref[...]), w_ref[0]) + b_ref[0]


def _modulation(cond8, ada_w, ada_b):
    tn = 512
    n = 6 * D_MODEL
    out = pl.pallas_call(
        _modulation_kernel,
        grid=(DEPTH, n // tn),
        in_specs=[pl.BlockSpec((8, D_MODEL), lambda l, j: (0, 0)),
                  pl.BlockSpec((1, D_MODEL, tn), lambda l, j: (l, 0, j)),
                  pl.BlockSpec((1, 1, tn), lambda l, j: (l, 0, j))],
        out_specs=pl.BlockSpec((1, 8, tn), lambda l, j: (l, 0, j)),
        out_shape=jax.ShapeDtypeStruct((DEPTH, 8, n), F32),
        compiler_params=_params("parallel", "parallel"),
        name="modulation",
    )(cond8, ada_w, ada_b.reshape(DEPTH, 1, n))
    m = out[:, :N_GROUPS].reshape(DEPTH, N_GROUPS, 6, D_MODEL)
    return jnp.pad(m, ((0, 0), (0, 0), (0, 2), (0, 0)))


_TM_SEQ = 1024


def _norm_mod(x, g, mod, off):
    return _rms(x, D_MODEL) * g * (1.0 + mod[off + 1:off + 2, :]) + mod[off:off + 1, :]


def _inproj_kernel(x_ref, g_ref, mod_ref, w_ref, mu_ref, o_ref, h_ref, *, shift):
    i = pl.program_id(0)

    @pl.when(pl.program_id(1) == 0)
    def _():
        h_ref[...] = _norm_mod(x_ref[...], g_ref[...], mod_ref[0], 0).astype(BF16)

    p = _dot_bf16(h_ref[...], w_ref[...])
    if shift:
        prev, nxt = _seq_neighbours(p, i, _TM_SEQ)
        p = p + (0.5 * (prev + nxt) - p) * mu_ref[...]
    o_ref[...] = p


def _inproj(x, g, mod, w, mu, tn, shift):
    n = w.shape[1]
    tm = _TM_SEQ
    return pl.pallas_call(
        functools.partial(_inproj_kernel, shift=shift),
        grid=(N_TOK // tm, n // tn),
        in_specs=[pl.BlockSpec((tm, D_MODEL), lambda i, j: (i, 0)),
                  pl.BlockSpec((1, D_MODEL), lambda i, j: (0, 0)),
                  pl.BlockSpec((1, 8, D_MODEL), lambda i, j: (_group_of_tile(i, tm), 0, 0)),
                  pl.BlockSpec((D_MODEL, tn), lambda i, j: (0, j)),
                  pl.BlockSpec((1, tn), lambda i, j: (0, j))],
        out_specs=pl.BlockSpec((tm, tn), lambda i, j: (i, j)),
        out_shape=jax.ShapeDtypeStruct((N_TOK, n), F32),
        scratch_shapes=[pltpu.VMEM((tm, D_MODEL), BF16)],
        compiler_params=_params("parallel", "arbitrary"),
        name="inproj_shift" if shift else "inproj",
    )(x, g, mod, w, mu)


def _resid_kernel(*refs, gate_row, n_act):
    a_refs = refs[:n_act]
    w_ref, x_ref, mod_ref, o_ref, a_bf_ref = refs[n_act:]

    @pl.when(pl.program_id(1) == 0)
    def _():
        k0 = 0
        for a_ref in a_refs:
            a_bf_ref[:, k0:k0 + a_ref.shape[1]] = a_ref[...].astype(BF16)
            k0 += a_ref.shape[1]

    o_ref[...] = x_ref[...] + mod_ref[0, gate_row:gate_row + 1, :] * _dot_bf16(a_bf_ref[...], w_ref[...])


def _resid_proj(acts, w, x, mod, gate_row):
    tm, tn = 512, 256
    k = sum(a.shape[1] for a in acts)
    return pl.pallas_call(
        functools.partial(_resid_kernel, gate_row=gate_row, n_act=len(acts)),
        grid=(N_TOK // tm, D_MODEL // tn),
        in_specs=[pl.BlockSpec((tm, a.shape[1]), lambda i, j: (i, 0)) for a in acts]
        + [pl.BlockSpec((k, tn), lambda i, j: (0, j)),
           pl.BlockSpec((tm, tn), lambda i, j: (i, j)),
           pl.BlockSpec((1, 8, tn), lambda i, j: (_group_of_tile(i, tm), 0, j))],
        out_specs=pl.BlockSpec((tm, tn), lambda i, j: (i, j)),
        out_shape=jax.ShapeDtypeStruct((N_TOK, D_MODEL), F32),
        scratch_shapes=[pltpu.VMEM((tm, k), BF16)],
        compiler_params=_params("parallel", "arbitrary"),
        name="resid_proj",
    )(*acts, w, x, mod)


def _ffn_up_kernel(x_ref, g_ref, mod_ref, wa_ref, wb_ref, cwa_ref, cwb_ref, cba_ref, cbb_ref, o_ref, h_ref):
    i = pl.program_id(0)

    @pl.when(pl.program_id(1) == 0)
    def _():
        h_ref[...] = _norm_mod(x_ref[...], g_ref[...], mod_ref[0], 3).astype(BF16)

    h = h_ref[...]

    def conv(w_ref, cw_ref, cb_ref):
        u = _dot_bf16(h, w_ref[...])
        prev, nxt = _seq_neighbours(u, i, _TM_SEQ)
        return prev * cw_ref[0:1, :] + u * cw_ref[1:2, :] + nxt * cw_ref[2:3, :] + cb_ref[...]

    o_ref[...] = _silu(conv(wa_ref, cwa_ref, cba_ref)) * conv(wb_ref, cwb_ref, cbb_ref)


def _ffn_up(x, g, mod, up, cw, cb):
    tm, tn = _TM_SEQ, 256
    nb = D_FF // tn
    cb = cb.reshape(1, 2 * D_FF)
    return pl.pallas_call(
        _ffn_up_kernel,
        grid=(N_TOK // tm, nb),
        in_specs=[pl.BlockSpec((tm, D_MODEL), lambda i, j: (i, 0)),
                  pl.BlockSpec((1, D_MODEL), lambda i, j: (0, 0)),
                  pl.BlockSpec((1, 8, D_MODEL), lambda i, j: (_group_of_tile(i, tm), 0, 0)),
                  pl.BlockSpec((D_MODEL, tn), lambda i, j: (0, j)),
                  pl.BlockSpec((D_MODEL, tn), lambda i, j: (0, j + nb)),
                  pl.BlockSpec((3, tn), lambda i, j: (0, j)),
                  pl.BlockSpec((3, tn), lambda i, j: (0, j + nb)),
                  pl.BlockSpec((1, tn), lambda i, j: (0, j)),
                  pl.BlockSpec((1, tn), lambda i, j: (0, j + nb))],
        out_specs=pl.BlockSpec((tm, tn), lambda i, j: (i, j)),
        out_shape=jax.ShapeDtypeStruct((N_TOK, D_FF), F32),
        scratch_shapes=[pltpu.VMEM((tm, D_MODEL), BF16)],
        compiler_params=_params("parallel", "arbitrary"),
        name="ffn_up",
    )(x, g, mod, up, up, cw, cw, cb, cb)


def _rope(y, c, s1, s2, half):
    return y * c + pltpu.roll(y, half, axis=1) * s1 + pltpu.roll(y, LANES - half, axis=1) * s2


def _rope_tables(cos, sin, first, n_pairs, groups):
    n = cos.shape[0]
    c = jnp.ones((n, LANES), F32)
    s1 = jnp.zeros((n, LANES), F32)
    s2 = jnp.zeros((n, LANES), F32)
    for g0 in groups:
        a = g0 + first
        c = c.at[:, a:a + n_pairs].set(cos).at[:, a + n_pairs:a + 2 * n_pairs].set(cos)
        s1 = s1.at[:, a + n_pairs:a + 2 * n_pairs].set(sin)
        s2 = s2.at[:, a:a + n_pairs].set(-sin)

    def all_rows(t, ident):
        ctx = jnp.broadcast_to(ident, (N_CTX, LANES))
        return jnp.concatenate([ctx] + [t] * DEC_BATCH, axis=0)

    return (all_rows(c, jnp.ones((1, LANES), F32)), all_rows(s1, jnp.zeros((1, LANES), F32)),
            all_rows(s2, jnp.zeros((1, LANES), F32)))


def _axial_angles(rot_dim):
    t = np.arange(DEC_SEQ)
    row, col = t // GRID_W, t % GRID_W
    n_freq = rot_dim // 4
    inv = jnp.asarray(ROPE_BASE, F32) ** (-jnp.arange(n_freq, dtype=F32) / n_freq)
    ang = jnp.concatenate([jnp.asarray(row, F32)[:, None] * inv, jnp.asarray(col, F32)[:, None] * inv], -1)
    return jnp.cos(ang), jnp.sin(ang)


def _deinterleave(n):
    return np.concatenate([np.arange(0, n, 2), np.arange(1, n, 2)])


def _mla_q_kernel(cq_ref, qnorm_ref, w_ref, qn_ref, c_ref, s1_ref, s2_ref, o_ref):
    xn = _rms(cq_ref[...], MLA_Q_RANK) * qnorm_ref[...]
    y = _dot_bf16(xn, w_ref[...])
    c, s1, s2 = c_ref[...], s1_ref[...], s2_ref[...]
    for h in range(MLA_HEADS):
        yh = y[:, h * LANES:(h + 1) * LANES]
        yh = _rms(yh, MLA_QK) * qn_ref[...]
        o_ref[:, h * LANES:(h + 1) * LANES] = _rope(yh, c, s1, s2, MLA_ROPE // 2)


def _mla_q(p, q_norm, w_uq_p, qn_p, tabs):
    tm = 512
    hw = MLA_HEADS * LANES
    tab_spec = pl.BlockSpec((tm, LANES), lambda i: (i, 0))
    return pl.pallas_call(
        _mla_q_kernel,
        grid=(N_TOK // tm,),
        in_specs=[pl.BlockSpec((tm, MLA_Q_RANK), lambda i: (i, 0)),
                  pl.BlockSpec((1, MLA_Q_RANK), lambda i: (0, 0)),
                  pl.BlockSpec((MLA_Q_RANK, hw), lambda i: (0, 0)),
                  pl.BlockSpec((1, LANES), lambda i: (0, 0)),
                  tab_spec, tab_spec, tab_spec],
        out_specs=pl.BlockSpec((tm, hw), lambda i: (i, 0)),
        out_shape=jax.ShapeDtypeStruct((N_TOK, hw), F32),
        compiler_params=_params("parallel"),
        name="mla_q",
    )(p, q_norm, w_uq_p, qn_p, *tabs)


def _mla_kv_kernel(ckv_ref, kr_ref, kvn_ref, wk_ref, wv_ref, kn_ref, c_ref, s1_ref, s2_ref,
                   k_ref, v_ref, ckvn_ref, *, norm_ckv):
    ckv = ckv_ref[...]
    if norm_ckv:
        ckv = _rms(ckv, MLA_KV_RANK) * kvn_ref[...]
    ckvn_ref[...] = ckv
    ckv_bf = ckv.astype(BF16)
    kk = _dot_bf16(ckv_bf, wk_ref[...])
    v_ref[...] = _dot_bf16(ckv_bf, wv_ref[...])
    kr = kr_ref[...]
    c, s1, s2 = c_ref[...], s1_ref[...], s2_ref[...]
    for h in range(MLA_HEADS):
        kh = kk[:, h * LANES:(h + 1) * LANES] + kr
        kh = _rms(kh, MLA_QK) * kn_ref[...]
        k_ref[:, h * LANES:(h + 1) * LANES] = _rope(kh, c, s1, s2, MLA_ROPE // 2)


def _mla_kv(ckv_src, ckv_blk, kr_src, kr_blk, kv_norm, wk_p, wv_p, kn_p, tabs, n_rows, norm_ckv):
    tm = 512
    hw = MLA_HEADS * LANES
    tab_spec = pl.BlockSpec((tm, LANES), lambda i: (i, 0))
    return pl.pallas_call(
        functools.partial(_mla_kv_kernel, norm_ckv=norm_ckv),
        grid=(n_rows // tm,),
        in_specs=[pl.BlockSpec((tm, LANES), lambda i: (i, ckv_blk)),
                  pl.BlockSpec((tm, LANES), lambda i: (i, kr_blk)),
                  pl.BlockSpec((1, LANES), lambda i: (0, 0)),
                  pl.BlockSpec((MLA_KV_RANK, hw), lambda i: (0, 0)),
                  pl.BlockSpec((MLA_KV_RANK, hw), lambda i: (0, 0)),
                  pl.BlockSpec((1, LANES), lambda i: (0, 0)),
                  tab_spec, tab_spec, tab_spec],
        out_specs=[pl.BlockSpec((tm, hw), lambda i: (i, 0)),
                   pl.BlockSpec((tm, hw), lambda i: (i, 0)),
                   pl.BlockSpec((tm, LANES), lambda i: (i, 0))],
        out_shape=[jax.ShapeDtypeStruct((n_rows, hw), F32),
                   jax.ShapeDtypeStruct((n_rows, hw), F32),
                   jax.ShapeDtypeStruct((n_rows, LANES), F32)],
        compiler_params=_params("parallel"),
        name="mla_kv",
    )(ckv_src, kr_src, kv_norm, wk_p, wv_p, kn_p, *tabs)


def _softmax_rows(s):
    p = jnp.exp(s - jnp.max(s, axis=-1, keepdims=True))
    return p, jnp.sum(p, axis=-1, keepdims=True)


def _mla_attn_kernel(q_ref, k_ref, v_ref, o_ref):
    scale = MLA_QK ** -0.5
    outs = []
    for h in range(2):
        sl = slice(h * LANES, (h + 1) * LANES)
        p, l = _softmax_rows(_dot_nt_bf16(q_ref[:, sl], k_ref[:, sl]) * scale)
        outs.append(_dot_bf16(p, v_ref[:, sl]) / l)
    o_ref[...] = outs[0] + pltpu.roll(outs[1], MLA_V, axis=1)


def _mla_attn(q, k, v, batch, nq, nk, q_row0, tq):
    nqb = nq // tq
    qb0 = q_row0 // tq
    return pl.pallas_call(
        _mla_attn_kernel,
        grid=(batch, MLA_HEADS // 2, nqb),
        in_specs=[pl.BlockSpec((tq, 2 * LANES), lambda b, h, i: (qb0 + b * nqb + i, h)),
                  pl.BlockSpec((nk, 2 * LANES), lambda b, h, i: (b, h)),
                  pl.BlockSpec((nk, 2 * LANES), lambda b, h, i: (b, h))],
        out_specs=pl.BlockSpec((tq, LANES), lambda b, h, i: (b * nqb + i, h)),
        out_shape=jax.ShapeDtypeStruct((batch * nq, MLA_HEADS * MLA_V), F32),
        compiler_params=_params("parallel", "parallel", "arbitrary"),
        name="mla_attn",
    )(q, k, v)


def _ret_kernel(lg_ref, q_ref, k_ref, v_ref, rg_ref, s0_ref, gn_ref, o_ref, st_ref, *, n, tq):
    b, pair, qi = pl.program_id(0), pl.program_id(1), pl.program_id(2)
    q = q_ref[...]
    k = k_ref[...] * (RET_DK ** -0.5)
    lo = _lane_lo((1, LANES))
    row = (qi * tq + lax.broadcasted_iota(jnp.int32, (tq, 1), 0)).astype(F32)
    col = lax.broadcasted_iota(jnp.int32, (1, n), 1).astype(F32)
    diff = row - col
    for h in range(2):
        lgf = lg_ref[0, 2 * pair + h]
        lgb = lg_ref[1, 2 * pair + h]
        mask = lo if h == 0 else jnp.logical_not(lo)
        qh = jnp.where(mask, q, 0.0)
        vh = v_ref[:, h * LANES:(h + 1) * LANES]
        decay = (jnp.where(diff >= 0, jnp.exp(lgf * jnp.maximum(diff, 0.0)), 0.0)
                 + jnp.where(diff <= 0, jnp.exp(lgb * jnp.maximum(-diff, 0.0)), 0.0))
        o = _dot_bf16(_dot_nt_bf16(qh, k) * decay, vh)
        o = o + _dot_bf16(qh * jnp.exp(lgf * (row + 1.0)), s0_ref[0, 0])
        o = o + _dot_bf16(qh * jnp.exp(lgb * (n - row)), s0_ref[0, 1])
        y = _rms(o, RET_DV) * gn_ref[:, h * LANES:(h + 1) * LANES]
        o_ref[:, h * LANES:(h + 1) * LANES] = _silu(rg_ref[:, h * LANES:(h + 1) * LANES]) * y

    @pl.when(qi == 0)
    def _():
        pos = lax.broadcasted_iota(jnp.int32, (n, 1), 0).astype(F32)
        for d in range(2):
            acc = None
            for h in range(2):
                lg = lg_ref[d, 2 * pair + h]
                mask = lo if h == 0 else jnp.logical_not(lo)
                expo = (n - 1.0 - pos) if d == 0 else pos
                kd = jnp.where(mask, k * jnp.exp(lg * expo), 0.0)
                term = _dot_tn(kd, v_ref[:, h * LANES:(h + 1) * LANES])
                acc = term if acc is None else acc + term
            lg_rows = jnp.where(lax.broadcasted_iota(jnp.int32, (LANES, 1), 0) < 64,
                                lg_ref[d, 2 * pair], lg_ref[d, 2 * pair + 1])
            st_ref[0, d] = acc + s0_ref[0, d] * jnp.exp(lg_rows * n)


def _retention(log_g, p, q_blk, k_blk, v_blk, g_blk, s0, gn, batch, n, row0, tq):
    nqb = n // tq
    qb0 = row0 // tq
    kb0 = row0 // n
    pairs = RET_HEADS // 2
    return pl.pallas_call(
        functools.partial(_ret_kernel, n=n, tq=tq),
        grid=(batch, pairs, nqb),
        in_specs=[pl.BlockSpec(memory_space=pltpu.SMEM),
                  pl.BlockSpec((tq, LANES), lambda b, h, i: (qb0 + b * nqb + i, q_blk + h)),
                  pl.BlockSpec((n, LANES), lambda b, h, i: (kb0 + b, k_blk + h)),
                  pl.BlockSpec((n, 2 * LANES), lambda b, h, i: (kb0 + b, v_blk // 2 + h)),
                  pl.BlockSpec((tq, 2 * LANES), lambda b, h, i: (qb0 + b * nqb + i, g_blk // 2 + h)),
                  pl.BlockSpec((1, 2, LANES, LANES), lambda b, h, i: (b, 0, h, 0)),
                  pl.BlockSpec((1, 2 * LANES), lambda b, h, i: (0, h))],
        out_specs=[pl.BlockSpec((tq, 2 * LANES), lambda b, h, i: (b * nqb + i, h)),
                   pl.BlockSpec((1, 2, LANES, LANES), lambda b, h, i: (b, 0, h, 0))],
        out_shape=[jax.ShapeDtypeStruct((batch * n, RET_HEADS * RET_DV), F32),
                   jax.ShapeDtypeStruct((batch, 2, RET_HEADS * RET_DK, RET_DV), F32)],
        compiler_params=_params("parallel", "parallel", "arbitrary"),
        name="retention",
    )(log_g, p, p, p, p, s0, gn)


def _diff_qk_kernel(q_ref, k_ref, qn_ref, kn_ref, c_ref, s1_ref, s2_ref, qo_ref, ko_ref):
    c, s1, s2 = c_ref[...], s1_ref[...], s2_ref[...]
    for src, gain, dst in ((q_ref, qn_ref, qo_ref), (k_ref, kn_ref, ko_ref)):
        for h in range(DIFF_HEADS):
            sl = slice(h * LANES, (h + 1) * LANES)
            y = src[:, sl]
            y = y * lax.rsqrt(_seg64_sum(y * y) * (1.0 / DIFF_DH) + EPS) * gain[...]
            dst[:, sl] = _rope(y, c, s1, s2, DIFF_DH // 2)


def _diff_qk(p, qn_p, kn_p, tabs):
    tm = 512
    tab_spec = pl.BlockSpec((tm, LANES), lambda i: (i, 0))
    return pl.pallas_call(
        _diff_qk_kernel,
        grid=(N_TOK // tm,),
        in_specs=[pl.BlockSpec((tm, DIFF_W), lambda i: (i, 0)),
                  pl.BlockSpec((tm, DIFF_W), lambda i: (i, 1)),
                  pl.BlockSpec((1, LANES), lambda i: (0, 0)),
                  pl.BlockSpec((1, LANES), lambda i: (0, 0)),
                  tab_spec, tab_spec, tab_spec],
        out_specs=[pl.BlockSpec((tm, DIFF_W), lambda i: (i, 0)),
                   pl.BlockSpec((tm, DIFF_W), lambda i: (i, 0))],
        out_shape=[jax.ShapeDtypeStruct((N_TOK, DIFF_W), F32)] * 2,
        compiler_params=_params("parallel"),
        name="diff_qk",
    )(p, p, qn_p, kn_p, *tabs)


def _diff_attn_kernel(lam_ref, q_ref, k_ref, v_ref, gn_ref, o_ref, *, lam_init):
    lv = lam_ref[...]
    lam = (jnp.exp(jnp.sum(lv[0:1] * lv[1:2], axis=-1, keepdims=True))
           - jnp.exp(jnp.sum(lv[2:3] * lv[3:4], axis=-1, keepdims=True)) + lam_init)
    scale = DIFF_DH ** -0.5
    q = q_ref[...]
    k = k_ref[...]
    lo = _lane_lo((1, LANES))
    kb = k.astype(BF16)
    p1, l1 = _softmax_rows(_dot_nt_bf16(jnp.where(lo, q, 0.0), kb) * scale)
    p2, l2 = _softmax_rows(_dot_nt_bf16(jnp.where(lo, 0.0, q), kb) * scale)
    w = p1 / l1 - lam * (p2 / l2)
    o = _dot_bf16(w, v_ref[...])
    o_ref[...] = _rms(o, 2 * DIFF_DH) * gn_ref[...] * (1.0 - lam_init)


def _diff_attn(lam, q, k, v, gn, batch, nq, nk, q_row0, tq, lam_init):
    nqb = nq // tq
    qb0 = q_row0 // tq
    return pl.pallas_call(
        functools.partial(_diff_attn_kernel, lam_init=lam_init),
        grid=(batch, DIFF_HEADS, nqb),
        in_specs=[pl.BlockSpec((4, DIFF_DH), lambda b, h, i: (0, 0)),
                  pl.BlockSpec((tq, LANES), lambda b, h, i: (qb0 + b * nqb + i, h)),
                  pl.BlockSpec((nk, LANES), lambda b, h, i: (b, h)),
                  pl.BlockSpec((nk, LANES), lambda b, h, i: (b, h)),
                  pl.BlockSpec((1, LANES), lambda b, h, i: (0, h))],
        out_specs=pl.BlockSpec((tq, LANES), lambda b, h, i: (b * nqb + i, h)),
        out_shape=jax.ShapeDtypeStruct((batch * nq, DIFF_W), F32),
        compiler_params=_params("parallel", "parallel", "arbitrary"),
        name="diff_attn",
    )(lam, q, k, v, gn)


def _seg64_sum_wide(x):
    return jnp.concatenate([_seg64_sum(x[:, j * LANES:(j + 1) * LANES]) for j in range(x.shape[1] // LANES)], axis=1)


_Q_R, _Q_V, _Q_KK = 0, 1, 2
_Q_DIR = 3
_N_Q = _Q_DIR + 2 * 3


_N_CTX_TILES_PRE = N_CTX // 256
_LAT_SLOTS = 3


def _rwkv_pre_kernel(r_ref, k_ref, v_ref, lo_ref, wup_ref, aup_ref, gup_ref, w0_ref, a0_ref, kk_ref, ka_ref, rk_ref,
                     pk_ref, lat_ref, g_ref, bonus_ref):
    i = pl.program_id(0)
    W = RWKV_W
    col = lambda q: slice(q * W, (q + 1) * W)
    r = r_ref[...]
    k = k_ref[...]
    v = v_ref[...]
    lora = lo_ref[...]
    kk = k * kk_ref[...]
    kkn = kk * lax.rsqrt(_seg64_sum_wide(kk * kk) + EPS)
    g_ref[...] = _dot_bf16(_sigmoid(lora[:, 2 * LANES:3 * LANES]), gup_ref[...])
    pre = w0_ref[...] + _dot_bf16(jnp.tanh(lora[:, 0:LANES]), wup_ref[...])
    decay = jnp.exp(-jnp.exp(-_softplus(-pre) - 0.5))
    a = _sigmoid(a0_ref[...] + _dot_bf16(lora[:, LANES:2 * LANES], aup_ref[...]))
    per_dir = []
    bonus = None
    for d in range(2):
        a_d = a[:, col(d)]
        k_d = k * (1.0 + (a_d - 1.0) * ka_ref[...])
        per_dir.append((decay[:, col(d)], k_d, kkn * a_d))
        t = _seg64_sum_wide(r * k_d * rk_ref[...])
        bonus = t if bonus is None else bonus + t
    bonus_ref[...] = bonus

    @pl.when(i < _N_CTX_TILES_PRE)
    def _():
        pk_ref[:, col(_Q_R)] = r
        pk_ref[:, col(_Q_V)] = v
        pk_ref[:, col(_Q_KK)] = kkn
        for d in range(2):
            for j in range(3):
                pk_ref[:, col(_Q_DIR + 3 * d + j)] = per_dir[d][j]

    @pl.when(i >= _N_CTX_TILES_PRE)
    def _():
        lo = _lane_lo((1, LANES))
        for d in range(2):
            w_d, k_d, b_d = per_dir[d]
            for s, (x1, x2) in enumerate(((kkn, w_d), (k_d, b_d), (r, v))):
                for h in range(RWKV_HEADS):
                    blk = slice((h // 2) * LANES, (h // 2 + 1) * LANES)
                    if h % 2 == 0:
                        out = jnp.where(lo, x1[:, blk], pltpu.roll(x2[:, blk], RWKV_HS, axis=1))
                    else:
                        out = jnp.where(lo, pltpu.roll(x1[:, blk], RWKV_HS, axis=1), x2[:, blk])
                    lat_ref[:, (d * _LAT_SLOTS + s) * RWKV_HEADS + h, :] = out


def _rwkv_pre(p, r_blk, k_blk, v_blk, lo_blk, wup_bd, aup_bd, gup, w0, a0, k_k, k_a, r_k):
    tm = 256
    w = RWKV_W
    n_ctx = _N_CTX_TILES_PRE
    n_j = 2 * _LAT_SLOTS * RWKV_HEADS
    row = lambda n: pl.BlockSpec((1, n), lambda i: (0, 0))
    full = lambda a, b: pl.BlockSpec((a, b), lambda i: (0, 0))
    return pl.pallas_call(
        _rwkv_pre_kernel,
        grid=(N_TOK // tm,),
        in_specs=[pl.BlockSpec((tm, w), lambda i: (i, r_blk)),
                  pl.BlockSpec((tm, w), lambda i: (i, k_blk)),
                  pl.BlockSpec((tm, w), lambda i: (i, v_blk)),
                  pl.BlockSpec((tm, 3 * LANES), lambda i: (i, lo_blk)),
                  full(LANES, 2 * w), full(LANES, 2 * w), full(LANES, w),
                  row(2 * w), row(2 * w), row(w), row(w), row(w)],
        out_specs=[pl.BlockSpec((tm, _N_Q * w), lambda i: (jnp.minimum(i, n_ctx - 1), 0)),
                   pl.BlockSpec((tm, n_j, LANES), lambda i: (jnp.maximum(i - n_ctx, 0), 0, 0)),
                   pl.BlockSpec((tm, w), lambda i: (i, 0)), pl.BlockSpec((tm, w), lambda i: (i, 0))],
        out_shape=[jax.ShapeDtypeStruct((N_CTX, _N_Q * w), F32), jax.ShapeDtypeStruct((N_LAT, n_j, LANES), F32),
                   jax.ShapeDtypeStruct((N_TOK, w), F32), jax.ShapeDtypeStruct((N_TOK, w), F32)],
        compiler_params=_params("arbitrary"),
        name="rwkv_pre",
    )(p, p, p, p, wup_bd, aup_bd, gup, w0, a0, k_k, k_a, r_k)


_SCAN_CHUNK = 32
_SCAN_UNROLL = 16


def _rwkv_first_sa(s_ref, sa_ref, kk, n_k):
    nv = s_ref.shape[1]
    chunk = min(_SCAN_CHUNK, nv)
    for c0 in range(0, nv, chunk):
        def body(k, acc):
            return acc + s_ref[k, c0:c0 + chunk, :] * kk(k)
        sa_ref[c0:c0 + chunk, :] = lax.fori_loop(0, n_k, body, jnp.zeros((chunk, LANES), F32), unroll=_SCAN_UNROLL)


def _rwkv_step(s_ref, sa_ref, kk_next, w, kd, b, r, v_at, n_k):
    nv = s_ref.shape[1]
    chunk = min(_SCAN_CHUNK, nv)
    ys = []
    for c0 in range(0, nv, chunk):
        sa = sa_ref[c0:c0 + chunk, :]
        vc = v_at(c0, chunk)

        def body(k, acc):
            y_acc, sa_acc = acc
            s_new = s_ref[k, c0:c0 + chunk, :] * w(k) - sa * b(k) + vc * kd(k)
            s_ref[k, c0:c0 + chunk, :] = s_new
            return y_acc + s_new * r(k), sa_acc + s_new * kk_next(k)

        zero = jnp.zeros((chunk, LANES), F32)
        y_acc, sa_acc = lax.fori_loop(0, n_k, body, (zero, zero), unroll=_SCAN_UNROLL)
        sa_ref[c0:c0 + chunk, :] = sa_acc
        ys.append(y_acc)
    return ys[0] if len(ys) == 1 else jnp.concatenate(ys, axis=0)


_CTX_TB = 32


def _rwkv_scan_ctx_kernel(xs_ref, xd_ref, y_ref, st_ref, s_ref, sa_ref):
    d = pl.program_id(0)
    tb = pl.program_id(1)
    step_t = lambda i: jnp.where(d == 0, i, _CTX_TB - 1 - i)

    @pl.when(tb == 0)
    def _():
        s_ref[...] = jnp.zeros_like(s_ref)

    t0 = step_t(0)
    _rwkv_first_sa(s_ref, sa_ref, lambda k: xs_ref[t0, _Q_KK, pl.ds(k, 1), :], RWKV_HS)

    def step(i, carry):
        t = step_t(i)
        tn = step_t(jnp.minimum(i + 1, _CTX_TB - 1))
        shared = lambda q, tt: (lambda k: xs_ref[tt, q, pl.ds(k, 1), :])
        per_dir = lambda q: (lambda k: xd_ref[t, q, pl.ds(k, 1), :])
        y_ref[t, 0] = _rwkv_step(s_ref, sa_ref, shared(_Q_KK, tn), per_dir(0), per_dir(1), per_dir(2),
                                 shared(_Q_R, t), lambda c0, n: xs_ref[t, _Q_V, pl.ds(c0, n), :], RWKV_HS)
        return carry

    lax.fori_loop(0, _CTX_TB, step, 0)

    @pl.when(tb == pl.num_programs(1) - 1)
    def _():
        st_ref[0] = s_ref[...]


def _rwkv_scan_ctx(pkt):
    nt = SEQ // _CTX_TB
    hs = RWKV_HS
    tblk = lambda d, tb: jnp.where(d == 0, tb, nt - 1 - tb)
    return pl.pallas_call(
        _rwkv_scan_ctx_kernel,
        grid=(2, nt),
        in_specs=[pl.BlockSpec((_CTX_TB, 3, hs, LANES), lambda d, tb: (tblk(d, tb), 0, 0, 0)),
                  pl.BlockSpec((_CTX_TB, 3, hs, LANES), lambda d, tb: (tblk(d, tb), 1 + d, 0, 0))],
        out_specs=[pl.BlockSpec((_CTX_TB, 1, hs, LANES), lambda d, tb: (tblk(d, tb), d, 0, 0)),
                   pl.BlockSpec((1, hs, hs, LANES), lambda d, tb: (d, 0, 0, 0))],
        out_shape=[jax.ShapeDtypeStruct((SEQ, 2, hs, LANES), F32),
                   jax.ShapeDtypeStruct((2, hs, hs, LANES), F32)],
        scratch_shapes=[pltpu.VMEM((hs, hs, LANES), F32), pltpu.VMEM((hs, LANES), F32)],
        compiler_params=_params("parallel", "arbitrary"),
        name="rwkv_scan_ctx",
    )(pkt, pkt)


_LAT_TB = 32
_LAT_VSPLIT = 4
_LAT_STATES = 2 * DEC_BATCH * RWKV_HEADS
_LAT_VROWS = RWKV_HS // _LAT_VSPLIT


def _rwkv_scan_lat_kernel(xf_ref, xb_ref, s0_ref, yf_ref, yb_ref, s_ref, sa_ref, kt_ref, v_ref, ys_ref):
    @pl.when(pl.program_id(0) == 0)
    def _():
        s_ref[...] = s0_ref[...]

    group = lax.broadcasted_iota(jnp.int32, (_LAT_VROWS, LANES), 1) // _LAT_STATES
    nh = RWKV_HEADS

    def load_t(t, carry):
        tr = _LAT_TB - 1 - t
        for s in range(_LAT_SLOTS):
            heads = slice(s * nh, (s + 1) * nh)
            x = jnp.concatenate([xf_ref[0, t, heads, :], xf_ref[1, t, heads, :],
                                 xb_ref[0, tr, heads, :], xb_ref[1, tr, heads, :]], axis=0)
            xt = jnp.concatenate([x] * _LAT_VSPLIT, axis=0).T
            kt_ref[t, 2 * s] = xt[:RWKV_HS]
            if s < _LAT_SLOTS - 1:
                kt_ref[t, 2 * s + 1] = xt[RWKV_HS:]
            else:
                v = jnp.zeros((_LAT_VROWS, LANES), F32)
                for g in range(_LAT_VSPLIT):
                    r0 = RWKV_HS + g * _LAT_VROWS
                    v = jnp.where(group == g, xt[r0:r0 + _LAT_VROWS, :], v)
                v_ref[t] = v
        return carry

    lax.fori_loop(0, _LAT_TB, load_t, 0, unroll=4)

    _rwkv_first_sa(s_ref, sa_ref, lambda k: kt_ref[0, 0, pl.ds(k, 1), :], RWKV_HS)

    def step(t, carry):
        tn = jnp.minimum(t + 1, _LAT_TB - 1)
        row = lambda q, tt: (lambda k: kt_ref[tt, q, pl.ds(k, 1), :])
        ys_ref[t] = _rwkv_step(s_ref, sa_ref, row(0, tn), row(1, t), row(2, t), row(3, t), row(4, t),
                               lambda c0, n: v_ref[t, pl.ds(c0, n), :], RWKV_HS)
        return carry

    lax.fori_loop(0, _LAT_TB, step, 0)

    def store_t(i, carry):
        rows = [jnp.where(group == g, ys_ref[2 * i + j], 0.0) for j in range(2) for g in range(_LAT_VSPLIT)]
        z = jnp.concatenate(rows, axis=0).T
        y = (z[0:_LAT_STATES] + z[_LAT_STATES:2 * _LAT_STATES]
             + z[2 * _LAT_STATES:3 * _LAT_STATES] + z[3 * _LAT_STATES:4 * _LAT_STATES])
        yb = pltpu.roll(y[_LAT_STATES // 2:], RWKV_HS, axis=1)
        for b in range(DEC_BATCH):
            yf_ref[b, i] = y[b * nh:(b + 1) * nh]
            yb_ref[b, _LAT_TB // 2 - 1 - i] = yb[b * nh:(b + 1) * nh]
        return carry

    lax.fori_loop(0, _LAT_TB // 2, store_t, 0, unroll=4)


def _rwkv_scan_lat(lat, s0):
    hs = RWKV_HS
    nv = _LAT_VROWS
    nt = DEC_SEQ // _LAT_TB
    nj = _LAT_SLOTS * RWKV_HEADS
    x4 = lat.reshape(DEC_BATCH, DEC_SEQ, 2 * nj, LANES)
    y_shape = jax.ShapeDtypeStruct((DEC_BATCH, DEC_SEQ // 2, RWKV_HEADS, 2 * hs), F32)
    y_blk = (DEC_BATCH, _LAT_TB // 2, RWKV_HEADS, 2 * hs)
    return pl.pallas_call(
        _rwkv_scan_lat_kernel,
        grid=(nt,),
        in_specs=[pl.BlockSpec((DEC_BATCH, _LAT_TB, nj, LANES), lambda tb: (0, tb, 0, 0)),
                  pl.BlockSpec((DEC_BATCH, _LAT_TB, nj, LANES), lambda tb: (0, nt - 1 - tb, 1, 0)),
                  pl.BlockSpec((hs, nv, LANES), lambda tb: (0, 0, 0))],
        out_specs=[pl.BlockSpec(y_blk, lambda tb: (0, tb, 0, 0)),
                   pl.BlockSpec(y_blk, lambda tb: (0, nt - 1 - tb, 0, 0))],
        out_shape=[y_shape, y_shape],
        scratch_shapes=[pltpu.VMEM((hs, nv, LANES), F32), pltpu.VMEM((nv, LANES), F32),
                        pltpu.VMEM((_LAT_TB, 2 * _LAT_SLOTS - 1, hs, LANES), F32),
                        pltpu.VMEM((_LAT_TB, nv, LANES), F32), pltpu.VMEM((_LAT_TB, nv, LANES), F32)],
        compiler_params=_params("arbitrary"),
        name="rwkv_scan_lat",
    )(x4, x4, s0)


def _rwkv_post_kernel(yf_ref, yb_ref, bonus_ref, v_ref, g_ref, gn_ref, o_ref):
    y = yf_ref[...] + yb_ref[...]
    y = y * lax.rsqrt(_seg64_sum_wide(y * y) * (1.0 / RWKV_HS) + EPS) * gn_ref[...]
    o_ref[...] = (y + bonus_ref[...] * v_ref[...]) * g_ref[...]


def _rwkv_post(y, bonus, p, v_blk, g, gn):
    tm = 512
    w = RWKV_W
    spec = pl.BlockSpec((tm, w), lambda i: (i, 0))
    return pl.pallas_call(
        _rwkv_post_kernel,
        grid=(N_TOK // tm,),
        in_specs=[spec, pl.BlockSpec((tm, w), lambda i: (i, 1)), spec,
                  pl.BlockSpec((tm, w), lambda i: (i, v_blk)), spec,
                  pl.BlockSpec((1, w), lambda i: (0, 0))],
        out_specs=spec,
        out_shape=jax.ShapeDtypeStruct((N_TOK, w), F32),
        compiler_params=_params("parallel"),
        name="rwkv_post",
    )(y, y, bonus, p, g, gn)


def _ctx_scan_operands(pk):
    x = pk.reshape(BATCH, SEQ, _N_Q, RWKV_HEADS, RWKV_HS)
    return jnp.transpose(x, (1, 2, 4, 0, 3)).reshape(SEQ, _N_Q, RWKV_HS, LANES)


def _ctx_scan_result(y):
    y = y.reshape(SEQ, 2, RWKV_HS, BATCH, RWKV_HEADS)
    return jnp.transpose(y, (3, 0, 1, 4, 2)).reshape(N_CTX, 2 * RWKV_W)


def _value_split_layout(x):
    lead = x.shape[:-2]
    n = len(lead)
    x = x.reshape(lead + (_LAT_STATES, _LAT_VSPLIT, _LAT_VROWS))
    return jnp.transpose(x, tuple(range(n)) + (n + 2, n + 1, n)).reshape(lead + (_LAT_VROWS, LANES))


def _lat_scan_result(yf, yb):
    def rows(y):
        y = y.reshape(DEC_BATCH, DEC_SEQ // 2, RWKV_HEADS, 2, RWKV_HS)
        return jnp.transpose(y, (0, 1, 3, 2, 4)).reshape(N_LAT, RWKV_W)
    return jnp.concatenate([rows(yf), rows(yb)], -1)


def _even_layer(x, mod, g_mix, w_in, q_norm, kv_norm, w_uq, w_ukv, qn, kn, ret_decay, ret_gn,
                cache_ckv, cache_krope, state_ret, tabs_m):
    perm_r = _deinterleave(MLA_ROPE)
    cq, ckv, krope, rq, rk, rv, rg = jnp.split(w_in, np.cumsum(
        (MLA_Q_RANK, MLA_KV_RANK, MLA_ROPE, RET_HEADS * RET_DK, RET_HEADS * RET_DK, RET_HEADS * RET_DV))[:].tolist(),
        axis=1)
    z = lambda n: jnp.zeros((D_MODEL, n), F32)
    w_p = jnp.concatenate([cq, ckv, z(MLA_NOPE), krope[:, perm_r], z(LANES - MLA_QK), rq, rk, rv, rg], axis=1)
    p = _inproj(x, g_mix, mod, w_p, jnp.zeros((1, w_p.shape[1]), F32), 512, False)
    CKV_BLK, KR_BLK, RQ_BLK, RK_BLK, RV_BLK, RG_BLK = 2, 3, 4, 6, 8, 12

    def head_pad(w, n_head, d_head, cols):
        w = w.reshape(w.shape[0], n_head, d_head)[:, :, cols]
        return jnp.pad(w, ((0, 0), (0, 0), (0, LANES - len(cols)))).reshape(w.shape[0], n_head * LANES)

    qk_cols = np.concatenate([np.arange(MLA_NOPE), MLA_NOPE + perm_r])
    w_uq_p = head_pad(w_uq, MLA_HEADS, MLA_QK, qk_cols)
    wk_p = head_pad(w_ukv, MLA_HEADS, MLA_NOPE + MLA_V, np.arange(MLA_NOPE))
    wv_p = head_pad(w_ukv, MLA_HEADS, MLA_NOPE + MLA_V, MLA_NOPE + np.arange(MLA_V))
    qn_p = jnp.pad(qn[qk_cols], (0, LANES - MLA_QK))[None]
    kn_p = jnp.pad(kn[qk_cols], (0, LANES - MLA_QK))[None]

    q = _mla_q(p, q_norm[None], w_uq_p, qn_p, tabs_m)
    k, v, ckvn = _mla_kv(p, CKV_BLK, p, KR_BLK, kv_norm[None], wk_p, wv_p, kn_p, tabs_m, N_TOK, True)

    n_c = DEC_BATCH * PAST_LEN
    kr_c = jnp.pad(cache_krope.reshape(n_c, MLA_ROPE)[:, perm_r], ((0, 0), (MLA_NOPE, LANES - MLA_QK)))
    ident = (jnp.ones((n_c, LANES), F32), jnp.zeros((n_c, LANES), F32), jnp.zeros((n_c, LANES), F32))
    k_c, v_c, _ = _mla_kv(cache_ckv.reshape(n_c, MLA_KV_RANK), 0, kr_c, 0, kv_norm[None], wk_p, wv_p, kn_p,
                          ident, n_c, False)

    hw = MLA_HEADS * LANES

    def with_cache(own, cache):
        own = own[N_CTX:].reshape(DEC_BATCH, DEC_SEQ, hw)
        return jnp.concatenate([own, cache.reshape(DEC_BATCH, PAST_LEN, hw)], 1).reshape(-1, hw)

    o_ctx = _mla_attn(q, k, v, BATCH, SEQ, SEQ, 0, SEQ)
    o_lat = _mla_attn(q, with_cache(k, k_c), with_cache(v, v_c), DEC_BATCH, DEC_SEQ, DEC_SEQ + PAST_LEN, N_CTX, 256)

    log_g = -_softplus(-ret_decay)
    gn = ret_gn[None]
    s0_ctx = jnp.zeros((BATCH, 2, RET_HEADS * RET_DK, RET_DV), F32)
    r_ctx, st_ctx = _retention(log_g, p, RQ_BLK, RK_BLK, RV_BLK, RG_BLK, s0_ctx, gn, BATCH, SEQ, 0, SEQ)
    s0_lat = state_ret.reshape(DEC_BATCH, 2, RET_HEADS * RET_DK, RET_DV)
    r_lat, _ = _retention(log_g, p, RQ_BLK, RK_BLK, RV_BLK, RG_BLK, s0_lat, gn, DEC_BATCH, DEC_SEQ, N_CTX, 256)

    mix = [jnp.concatenate([o_ctx, o_lat], 0), jnp.concatenate([r_ctx, r_lat], 0)]
    new_ckv = ckvn[:N_CTX].reshape(BATCH, SEQ, MLA_KV_RANK)
    new_krope = p[:N_CTX, KR_BLK * LANES + MLA_NOPE:KR_BLK * LANES + MLA_QK][:, np.argsort(perm_r)]
    new_krope = new_krope.reshape(BATCH, SEQ, MLA_ROPE)
    new_ret = st_ctx.reshape(BATCH, 2, RET_HEADS, RET_DK, RET_DV)
    return mix, new_ckv, new_krope, new_ret


def _odd_layer(x, mod, g_mix, w_in, qn, kn, lam, diff_gn, mu, w0, w_up, a0, a_up, g_up, k_k, k_a, r_k, gn,
               cache_k, cache_v, state_rwkv, tabs_d, lam_init):
    perm = _deinterleave(DIFF_DH)
    qk_perm = (np.arange(2 * DIFF_W).reshape(-1, DIFF_DH)[:, perm]).reshape(-1)
    w_p = jnp.concatenate([w_in[:, qk_perm], w_in[:, 2 * DIFF_W:]], axis=1)
    n_in = w_p.shape[1]
    mu_full = jnp.concatenate([jnp.zeros((3 * DIFF_W,), F32), mu])[None]
    p = _inproj(x, g_mix, mod, w_p, mu_full, 384, True)
    DV_BLK, R_BLK, K_BLK, V_BLK = 2, 3, 4, 5
    LO_BLK = (6 * RWKV_W) // (3 * LANES)

    qn_p = jnp.tile(qn[perm], 2)[None]
    kn_p = jnp.tile(kn[perm], 2)[None]
    q, k = _diff_qk(p, qn_p, kn_p, tabs_d)
    v = p[:, DV_BLK * DIFF_W:(DV_BLK + 1) * DIFF_W]

    n_c = DEC_BATCH * PAST_LEN
    k_c = cache_k.reshape(n_c, DIFF_HEADS * 2, DIFF_DH)[:, :, perm].reshape(DEC_BATCH, PAST_LEN, DIFF_W)
    v_c = cache_v.reshape(DEC_BATCH, PAST_LEN, DIFF_W)

    def with_cache(own, cache):
        return jnp.concatenate([own[N_CTX:].reshape(DEC_BATCH, DEC_SEQ, DIFF_W), cache], 1).reshape(-1, DIFF_W)

    dgn = diff_gn[None]
    o_ctx = _diff_attn(lam, q, k, v, dgn, BATCH, SEQ, SEQ, 0, SEQ, lam_init)
    o_lat = _diff_attn(lam, q, with_cache(k, k_c), with_cache(v, v_c), dgn, DEC_BATCH, DEC_SEQ, DEC_SEQ + PAST_LEN,
                       N_CTX, 256, lam_init)

    zero = jnp.zeros((RWKV_W_LORA, RWKV_W), F32)
    wup_bd = jnp.concatenate([jnp.concatenate([w_up[0], zero], 1), jnp.concatenate([zero, w_up[1]], 1)], 0)
    aup_bd = jnp.concatenate([jnp.concatenate([a_up[0], zero], 1), jnp.concatenate([zero, a_up[1]], 1)], 0)
    pk, lat, g, bonus = _rwkv_pre(p, R_BLK, K_BLK, V_BLK, LO_BLK, wup_bd, aup_bd, g_up, w0.reshape(1, -1),
                             a0.reshape(1, -1), k_k[None], k_a[None], r_k.reshape(1, -1))

    y_ctx, st_ctx = _rwkv_scan_ctx(_ctx_scan_operands(pk))
    s0_lat = jnp.transpose(state_rwkv, (4, 1, 0, 2, 3)).reshape(RWKV_HS, _LAT_STATES, RWKV_HS)
    y_lat = _rwkv_scan_lat(lat, _value_split_layout(s0_lat))
    y = jnp.concatenate([_ctx_scan_result(y_ctx), _lat_scan_result(*y_lat)], 0)
    rw_o = _rwkv_post(y, bonus, p, V_BLK, g, gn[None])

    mix = [jnp.concatenate([o_ctx, o_lat], 0), rw_o]
    inv = np.argsort(perm)
    new_dk = k[:N_CTX].reshape(BATCH, SEQ, DIFF_HEADS, 2, DIFF_DH)[..., inv]
    new_dv = v[:N_CTX].reshape(BATCH, SEQ, DIFF_HEADS, 2 * DIFF_DH)
    new_rwkv = jnp.transpose(st_ctx.reshape(2, RWKV_HS, RWKV_HS, BATCH, RWKV_HEADS), (3, 0, 4, 2, 1))
    return mix, new_dk, new_dv, new_rwkv


def kernel(x_prompt, x_sample, cache_mla_ckv, cache_mla_krope, state_ret, cache_diff_k, cache_diff_v, state_rwkv,
           c, c_ctx, ada_w, ada_b, norm_mix_g, norm_ffn_g, w_out, ffn_up, ffn_conv_w, ffn_conv_b, ffn_down,
           a_w_in, mla_q_norm, mla_kv_norm, mla_w_uq, mla_w_ukv, mla_qn, mla_kn, ret_decay, ret_gn,
           b_w_in, diff_qn, diff_kn, diff_lam, diff_gn, rwkv_mu, rwkv_w0, rwkv_w_up, rwkv_a0, rwkv_a_up,
           rwkv_g_up, rwkv_k_k, rwkv_k_a, rwkv_r_k, rwkv_gn):
    x = jnp.concatenate([x_prompt.reshape(N_CTX, D_MODEL), x_sample.reshape(N_LAT, D_MODEL)], 0)
    cond8 = jnp.pad(jnp.concatenate([c_ctx[None], c], 0), ((0, 8 - N_GROUPS), (0, 0)))
    mod = _modulation(cond8, ada_w, ada_b)

    cos_m, sin_m = _axial_angles(MLA_ROPE)
    cos_d, sin_d = _axial_angles(DIFF_DH)
    tabs_m = _rope_tables(cos_m, sin_m, MLA_NOPE, MLA_ROPE // 2, (0,))
    tabs_d = _rope_tables(cos_d, sin_d, 0, DIFF_DH // 2, (0, DIFF_DH))

    outs = {}
    for l in range(DEPTH):
        j = l // 2
        g_mix = norm_mix_g[l][None]
        if l % 2 == 0:
            mix, outs["ckv"], outs["krope"], outs["ret"] = _even_layer(
                x, mod[l], g_mix, a_w_in[j], mla_q_norm[j], mla_kv_norm[j], mla_w_uq[j], mla_w_ukv[j], mla_qn[j],
                mla_kn[j], ret_decay[j], ret_gn[j], cache_mla_ckv[:, j], cache_mla_krope[:, j], state_ret[:, j],
                tabs_m)
        else:
            lam_init = 0.8 - 0.6 * math.exp(-0.3 * l)
            mix, outs["dk"], outs["dv"], outs["rwkv"] = _odd_layer(
                x, mod[l], g_mix, b_w_in[j], diff_qn[j], diff_kn[j], diff_lam[j], diff_gn[j], rwkv_mu[j],
                rwkv_w0[j], rwkv_w_up[j], rwkv_a0[j], rwkv_a_up[j], rwkv_g_up[j], rwkv_k_k[j], rwkv_k_a[j],
                rwkv_r_k[j], rwkv_gn[j], cache_diff_k[:, j], cache_diff_v[:, j], state_rwkv[:, j], tabs_d, lam_init)
        x = _resid_proj(mix, w_out[l], x, mod[l], 2)
        act = _ffn_up(x, norm_ffn_g[l][None], mod[l], ffn_up[l], ffn_conv_w[l], ffn_conv_b[l])
        x = _resid_proj([act], ffn_down[l], x, mod[l], 5)

    y_prompt = x[:N_CTX].reshape(BATCH, SEQ, D_MODEL)
    y_sample = x[N_CTX:].reshape(DEC_BATCH, DEC_SEQ, D_MODEL)
    return (y_prompt, y_sample, outs["ckv"][:, None], outs["krope"][:, None], outs["ret"][:, None],
            outs["dk"][:, None], outs["dv"][:, None], outs["rwkv"][:, None])
```

```python
import functools
import math

import numpy as np
import jax
import jax.numpy as jnp
from jax import lax
from jax.experimental import pallas as pl
from jax.experimental.pallas import tpu as pltpu

D_MODEL = 1024
BATCH = 16
SEQ = 256
DEPTH = 2
DEC_BATCH = 2
DEC_SEQ = 1024
PAST_LEN = 512
GRID_W = 64
EPS = 1e-6
ROPE_BASE = 10000.0

MLA_HEADS = 8
MLA_Q_RANK = 256
MLA_KV_RANK = 128
MLA_NOPE = 64
MLA_ROPE = 32
MLA_V = 64
MLA_QK = MLA_NOPE + MLA_ROPE
RET_HEADS = 4
RET_DK = 64
RET_DV = 128
DIFF_HEADS = 4
DIFF_DH = 64
DIFF_W = DIFF_HEADS * 2 * DIFF_DH
RWKV_HEADS = 8
RWKV_HS = 64
RWKV_W = RWKV_HEADS * RWKV_HS
RWKV_W_LORA = 64
RWKV_A_LORA = 64
RWKV_G_LORA = 128
D_FF = 2816

N_CTX = BATCH * SEQ
N_LAT = DEC_BATCH * DEC_SEQ
N_TOK = N_CTX + N_LAT
N_GROUPS = 1 + DEC_BATCH

LANES = 128
VMEM_LIMIT = 56 * 1024 * 1024

_PREC = lax.Precision.HIGHEST
F32 = jnp.float32


def _dot_tn(a, b):
    return lax.dot_general(a, b, (((0,), (0,)), ((), ())), precision=_PREC, preferred_element_type=F32)


BF16 = jnp.bfloat16


def _dot_bf16(a, b):
    return jnp.dot(a.astype(BF16), b.astype(BF16), preferred_element_type=F32)


def _dot_nt_bf16(a, b):
    return lax.dot_general(a.astype(BF16), b.astype(BF16), (((1,), (1,)), ((), ())), preferred_element_type=F32)


def _params(*sem):
    return pltpu.CompilerParams(dimension_semantics=sem, vmem_limit_bytes=VMEM_LIMIT)


def _sigmoid(x):
    return 1.0 / (1.0 + jnp.exp(-x))


def _silu(x):
    return x * _sigmoid(x)


def _softplus(x):
    return jnp.maximum(x, 0.0) + jnp.log(1.0 + jnp.exp(-jnp.abs(x)))


def _rms(x, n):
    return x * lax.rsqrt(jnp.sum(x * x, axis=-1, keepdims=True) * (1.0 / n) + EPS)


def _lane_lo(shape):
    return lax.broadcasted_iota(jnp.int32, shape, len(shape) - 1) < 64


def _seg64_sum(x):
    lo = _lane_lo(x.shape)
    s_lo = jnp.sum(jnp.where(lo, x, 0.0), axis=-1, keepdims=True)
    s_hi = jnp.sum(jnp.where(lo, 0.0, x), axis=-1, keepdims=True)
    return jnp.where(lo, s_lo, s_hi)


def _seq_neighbours(p, tile, tile_rows):
    seq_mask = jnp.where(tile * tile_rows < N_CTX, SEQ - 1, DEC_SEQ - 1)
    pos = lax.broadcasted_iota(jnp.int32, (tile_rows, 1), 0) & seq_mask
    prev = jnp.where(pos == 0, 0.0, pltpu.roll(p, 1, axis=0))
    nxt = jnp.where(pos == seq_mask, 0.0, pltpu.roll(p, tile_rows - 1, axis=0))
    return prev, nxt


def _group_of_tile(i, tile_rows):
    row = i * tile_rows
    return jnp.where(row < N_CTX, 0, 1 + (row - N_CTX) // DEC_SEQ)


def _modulation_kernel(c_ref, w_ref, b_ref, o_ref):
    o_ref[0] = _dot_bf16(_silu(c_ref[...]), w_ref[0]) + b_ref[0]


def _modulation(cond8, ada_w, ada_b):
    tn = 512
    n = 6 * D_MODEL
    out = pl.pallas_call(
        _modulation_kernel,
        grid=(DEPTH, n // tn),
        in_specs=[pl.BlockSpec((8, D_MODEL), lambda l, j: (0, 0)),
                  pl.BlockSpec((1, D_MODEL, tn), lambda l, j: (l, 0, j)),
                  pl.BlockSpec((1, 1, tn), lambda l, j: (l, 0, j))],
        out_specs=pl.BlockSpec((1, 8, tn), lambda l, j: (l, 0, j)),
        out_shape=jax.ShapeDtypeStruct((DEPTH, 8, n), F32),
        compiler_params=_params("parallel", "parallel"),
        name="modulation",
    )(cond8, ada_w, ada_b.reshape(DEPTH, 1, n))
    m = out[:, :N_GROUPS].reshape(DEPTH, N_GROUPS, 6, D_MODEL)
    return jnp.pad(m, ((0, 0), (0, 0), (0, 2), (0, 0)))


_TM_SEQ = 1024


def _norm_mod(x, g, mod, off):
    return _rms(x, D_MODEL) * g * (1.0 + mod[off + 1:off + 2, :]) + mod[off:off + 1, :]


def _inproj_kernel(x_ref, g_ref, mod_ref, w_ref, mu_ref, o_ref, h_ref, *, shift):
    i = pl.program_id(0)

    @pl.when(pl.program_id(1) == 0)
    def _():
        h_ref[...] = _norm_mod(x_ref[...], g_ref[...], mod_ref[0], 0).astype(BF16)

    p = _dot_bf16(h_ref[...], w_ref[...])
    if shift:
        prev, nxt = _seq_neighbours(p, i, _TM_SEQ)
        p = p + (0.5 * (prev + nxt) - p) * mu_ref[...]
    o_ref[...] = p


def _inproj(x, g, mod, w, mu, tn, shift):
    n = w.shape[1]
    tm = _TM_SEQ
    return pl.pallas_call(
        functools.partial(_inproj_kernel, shift=shift),
        grid=(N_TOK // tm, n // tn),
        in_specs=[pl.BlockSpec((tm, D_MODEL), lambda i, j: (i, 0)),
                  pl.BlockSpec((1, D_MODEL), lambda i, j: (0, 0)),
                  pl.BlockSpec((1, 8, D_MODEL), lambda i, j: (_group_of_tile(i, tm), 0, 0)),
                  pl.BlockSpec((D_MODEL, tn), lambda i, j: (0, j)),
                  pl.BlockSpec((1, tn), lambda i, j: (0, j))],
        out_specs=pl.BlockSpec((tm, tn), lambda i, j: (i, j)),
        out_shape=jax.ShapeDtypeStruct((N_TOK, n), F32),
        scratch_shapes=[pltpu.VMEM((tm, D_MODEL), BF16)],
        compiler_params=_params("parallel", "arbitrary"),
        name="inproj_shift" if shift else "inproj",
    )(x, g, mod, w, mu)


def _resid_kernel(*refs, gate_row, n_act):
    a_refs = refs[:n_act]
    w_ref, x_ref, mod_ref, o_ref, a_bf_ref = refs[n_act:]

    @pl.when(pl.program_id(1) == 0)
    def _():
        k0 = 0
        for a_ref in a_refs:
            a_bf_ref[:, k0:k0 + a_ref.shape[1]] = a_ref[...].astype(BF16)
            k0 += a_ref.shape[1]

    o_ref[...] = x_ref[...] + mod_ref[0, gate_row:gate_row + 1, :] * _dot_bf16(a_bf_ref[...], w_ref[...])


def _resid_proj(acts, w, x, mod, gate_row):
    tm, tn = 1024, 256
    k = sum(a.shape[1] for a in acts)
    return pl.pallas_call(
        functools.partial(_resid_kernel, gate_row=gate_row, n_act=len(acts)),
        grid=(N_TOK // tm, D_MODEL // tn),
        in_specs=[pl.BlockSpec((tm, a.shape[1]), lambda i, j: (i, 0)) for a in acts]
        + [pl.BlockSpec((k, tn), lambda i, j: (0, j)),
           pl.BlockSpec((tm, tn), lambda i, j: (i, j)),
           pl.BlockSpec((1, 8, tn), lambda i, j: (_group_of_tile(i, tm), 0, j))],
        out_specs=pl.BlockSpec((tm, tn), lambda i, j: (i, j)),
        out_shape=jax.ShapeDtypeStruct((N_TOK, D_MODEL), F32),
        scratch_shapes=[pltpu.VMEM((tm, k), BF16)],
        compiler_params=_params("parallel", "arbitrary"),
        name="resid_proj",
    )(*acts, w, x, mod)


def _ffn_up_kernel(x_ref, g_ref, mod_ref, wa_ref, wb_ref, cwa_ref, cwb_ref, cba_ref, cbb_ref, o_ref, h_ref):
    i = pl.program_id(0)

    @pl.when(pl.program_id(1) == 0)
    def _():
        h_ref[...] = _norm_mod(x_ref[...], g_ref[...], mod_ref[0], 3).astype(BF16)

    h = h_ref[...]

    def conv(w_ref, cw_ref, cb_ref):
        u = _dot_bf16(h, w_ref[...])
        prev, nxt = _seq_neighbours(u, i, _TM_SEQ)
        return prev * cw_ref[0:1, :] + u * cw_ref[1:2, :] + nxt * cw_ref[2:3, :] + cb_ref[...]

    o_ref[...] = _silu(conv(wa_ref, cwa_ref, cba_ref)) * conv(wb_ref, cwb_ref, cbb_ref)


def _ffn_up(x, g, mod, up, cw, cb):
    tm, tn = _TM_SEQ, 256
    nb = D_FF // tn
    cb = cb.reshape(1, 2 * D_FF)
    return pl.pallas_call(
        _ffn_up_kernel,
        grid=(N_TOK // tm, nb),
        in_specs=[pl.BlockSpec((tm, D_MODEL), lambda i, j: (i, 0)),
                  pl.BlockSpec((1, D_MODEL), lambda i, j: (0, 0)),
                  pl.BlockSpec((1, 8, D_MODEL), lambda i, j: (_group_of_tile(i, tm), 0, 0)),
                  pl.BlockSpec((D_MODEL, tn), lambda i, j: (0, j)),
                  pl.BlockSpec((D_MODEL, tn), lambda i, j: (0, j + nb)),
                  pl.BlockSpec((3, tn), lambda i, j: (0, j)),
                  pl.BlockSpec((3, tn), lambda i, j: (0, j + nb)),
                  pl.BlockSpec((1, tn), lambda i, j: (0, j)),
                  pl.BlockSpec((1, tn), lambda i, j: (0, j + nb))],
        out_specs=pl.BlockSpec((tm, tn), lambda i, j: (i, j)),
        out_shape=jax.ShapeDtypeStruct((N_TOK, D_FF), F32),
        scratch_shapes=[pltpu.VMEM((tm, D_MODEL), BF16)],
        compiler_params=_params("parallel", "arbitrary"),
        name="ffn_up",
    )(x, g, mod, up, up, cw, cw, cb, cb)


_ROPE_TM = 512


def _rope(y, c, s1, s2):
    return y * c + pltpu.roll(y, 1, axis=1) * s1 + pltpu.roll(y, LANES - 1, axis=1) * s2


def _rope_tables(rot_dim, lane_offsets):
    t = np.arange(DEC_SEQ)
    row, col = t // GRID_W, t % GRID_W
    n_freq = rot_dim // 4
    inv = ROPE_BASE ** (-np.arange(n_freq, dtype=np.float64) / n_freq)
    ang = np.concatenate([row[:, None] * inv, col[:, None] * inv], -1)
    cos, sin = np.cos(ang), np.sin(ang)
    n = _ROPE_TM + DEC_SEQ
    c, s1, s2 = np.ones((n, LANES)), np.zeros((n, LANES)), np.zeros((n, LANES))
    for a in lane_offsets:
        even = a + 2 * np.arange(rot_dim // 2)
        c[_ROPE_TM:, even] = cos
        c[_ROPE_TM:, even + 1] = cos
        s1[_ROPE_TM:, even + 1] = sin
        s2[_ROPE_TM:, even] = -sin
    return tuple(jnp.asarray(x, F32) for x in (c, s1, s2))


def _rope_block(i):
    row = i * _ROPE_TM
    return jnp.where(row < N_CTX, 0, 1 + ((row - N_CTX) % DEC_SEQ) // _ROPE_TM)


def _mla_q_kernel(cq_ref, qnorm_ref, w_ref, qn_ref, c_ref, s1_ref, s2_ref, o_ref):
    xn = _rms(cq_ref[...], MLA_Q_RANK) * qnorm_ref[...]
    y = _dot_bf16(xn, w_ref[...])
    c, s1, s2 = c_ref[...], s1_ref[...], s2_ref[...]
    for h in range(MLA_HEADS):
        yh = y[:, h * LANES:(h + 1) * LANES]
        yh = _rms(yh, MLA_QK) * qn_ref[...]
        o_ref[:, h * LANES:(h + 1) * LANES] = _rope(yh, c, s1, s2)


def _mla_q(p, q_norm, w_uq_p, qn_p, tabs):
    tm = _ROPE_TM
    hw = MLA_HEADS * LANES
    tab_spec = pl.BlockSpec((tm, LANES), lambda i: (_rope_block(i), 0))
    return pl.pallas_call(
        _mla_q_kernel,
        grid=(N_TOK // tm,),
        in_specs=[pl.BlockSpec((tm, MLA_Q_RANK), lambda i: (i, 0)),
                  pl.BlockSpec((1, MLA_Q_RANK), lambda i: (0, 0)),
                  pl.BlockSpec((MLA_Q_RANK, hw), lambda i: (0, 0)),
                  pl.BlockSpec((1, LANES), lambda i: (0, 0)),
                  tab_spec, tab_spec, tab_spec],
        out_specs=pl.BlockSpec((tm, hw), lambda i: (i, 0)),
        out_shape=jax.ShapeDtypeStruct((N_TOK, hw), F32),
        compiler_params=_params("parallel"),
        name="mla_q",
    )(p, q_norm, w_uq_p, qn_p, *tabs)


def _mla_kv_kernel(ckv_ref, kr_ref, kvn_ref, wk_ref, wv_ref, kn_ref, c_ref, s1_ref, s2_ref,
                   k_ref, v_ref, ckvn_ref, *, norm_ckv):
    ckv = ckv_ref[...]
    if norm_ckv:
        ckv = _rms(ckv, MLA_KV_RANK) * kvn_ref[...]
    ckvn_ref[...] = ckv
    ckv_bf = ckv.astype(BF16)
    kk = _dot_bf16(ckv_bf, wk_ref[...])
    v_ref[...] = _dot_bf16(ckv_bf, wv_ref[...])
    kr = kr_ref[...]
    c, s1, s2 = c_ref[...], s1_ref[...], s2_ref[...]
    for h in range(MLA_HEADS):
        kh = kk[:, h * LANES:(h + 1) * LANES] + kr
        kh = _rms(kh, MLA_QK) * kn_ref[...]
        k_ref[:, h * LANES:(h + 1) * LANES] = _rope(kh, c, s1, s2)


def _mla_kv(ckv_src, ckv_blk, kr_src, kr_blk, kv_norm, wk_p, wv_p, kn_p, tabs, n_rows, own_tokens):
    tm = _ROPE_TM
    hw = MLA_HEADS * LANES
    tab_spec = pl.BlockSpec((tm, LANES), (lambda i: (_rope_block(i), 0)) if own_tokens else (lambda i: (0, 0)))
    return pl.pallas_call(
        functools.partial(_mla_kv_kernel, norm_ckv=own_tokens),
        grid=(n_rows // tm,),
        in_specs=[pl.BlockSpec((tm, LANES), lambda i: (i, ckv_blk)),
                  pl.BlockSpec((tm, LANES), lambda i: (i, kr_blk)),
                  pl.BlockSpec((1, LANES), lambda i: (0, 0)),
                  pl.BlockSpec((MLA_KV_RANK, hw), lambda i: (0, 0)),
                  pl.BlockSpec((MLA_KV_RANK, hw), lambda i: (0, 0)),
                  pl.BlockSpec((1, LANES), lambda i: (0, 0)),
                  tab_spec, tab_spec, tab_spec],
        out_specs=[pl.BlockSpec((tm, hw), lambda i: (i, 0)),
                   pl.BlockSpec((tm, hw), lambda i: (i, 0)),
                   pl.BlockSpec((tm, LANES), lambda i: (i, 0))],
        out_shape=[jax.ShapeDtypeStruct((n_rows, hw), F32),
                   jax.ShapeDtypeStruct((n_rows, hw), F32),
                   jax.ShapeDtypeStruct((n_rows, LANES), F32)],
        compiler_params=_params("parallel"),
        name="mla_kv",
    )(ckv_src, kr_src, kv_norm, wk_p, wv_p, kn_p, *tabs)


def _softmax_rows(s):
    p = jnp.exp(s - jnp.max(s, axis=-1, keepdims=True))
    return p, jnp.sum(p, axis=-1, keepdims=True)


def _mla_attn_kernel(q_ref, k_ref, v_ref, o_ref):
    scale = MLA_QK ** -0.5
    outs = []
    for h in range(2):
        sl = slice(h * LANES, (h + 1) * LANES)
        p, l = _softmax_rows(_dot_nt_bf16(q_ref[:, sl], k_ref[:, sl]) * scale)
        outs.append(_dot_bf16(p, v_ref[:, sl]) / l)
    o_ref[...] = outs[0] + pltpu.roll(outs[1], MLA_V, axis=1)


def _mla_attn(q, k, v, batch, nq, nk, q_row0, tq):
    nqb = nq // tq
    qb0 = q_row0 // tq
    return pl.pallas_call(
        _mla_attn_kernel,
        grid=(batch, MLA_HEADS // 2, nqb),
        in_specs=[pl.BlockSpec((tq, 2 * LANES), lambda b, h, i: (qb0 + b * nqb + i, h)),
                  pl.BlockSpec((nk, 2 * LANES), lambda b, h, i: (b, h)),
                  pl.BlockSpec((nk, 2 * LANES), lambda b, h, i: (b, h))],
        out_specs=pl.BlockSpec((tq, LANES), lambda b, h, i: (b * nqb + i, h)),
        out_shape=jax.ShapeDtypeStruct((batch * nq, MLA_HEADS * MLA_V), F32),
        compiler_params=_params("parallel", "parallel", "arbitrary"),
        name="mla_attn",
    )(q, k, v)


def _ret_kernel(lg_ref, q_ref, k_ref, v_ref, rg_ref, s0_ref, gn_ref, o_ref, st_ref, *, n, tq):
    b, pair, qi = pl.program_id(0), pl.program_id(1), pl.program_id(2)
    q = q_ref[...]
    k = k_ref[...] * (RET_DK ** -0.5)
    lo = _lane_lo((1, LANES))
    row = (qi * tq + lax.broadcasted_iota(jnp.int32, (tq, 1), 0)).astype(F32)
    col = lax.broadcasted_iota(jnp.int32, (1, n), 1).astype(F32)
    diff = row - col
    for h in range(2):
        lgf = lg_ref[0, 2 * pair + h]
        lgb = lg_ref[1, 2 * pair + h]
        mask = lo if h == 0 else jnp.logical_not(lo)
        qh = jnp.where(mask, q, 0.0)
        vh = v_ref[:, h * LANES:(h + 1) * LANES]
        decay = (jnp.where(diff >= 0, jnp.exp(lgf * jnp.maximum(diff, 0.0)), 0.0)
                 + jnp.where(diff <= 0, jnp.exp(lgb * jnp.maximum(-diff, 0.0)), 0.0))
        o = _dot_bf16(_dot_nt_bf16(qh, k) * decay, vh)
        o = o + _dot_bf16(qh * jnp.exp(lgf * (row + 1.0)), s0_ref[0, 0])
        o = o + _dot_bf16(qh * jnp.exp(lgb * (n - row)), s0_ref[0, 1])
        y = _rms(o, RET_DV) * gn_ref[:, h * LANES:(h + 1) * LANES]
        o_ref[:, h * LANES:(h + 1) * LANES] = _silu(rg_ref[:, h * LANES:(h + 1) * LANES]) * y

    @pl.when(qi == 0)
    def _():
        pos = lax.broadcasted_iota(jnp.int32, (n, 1), 0).astype(F32)
        for d in range(2):
            acc = None
            for h in range(2):
                lg = lg_ref[d, 2 * pair + h]
                mask = lo if h == 0 else jnp.logical_not(lo)
                expo = (n - 1.0 - pos) if d == 0 else pos
                kd = jnp.where(mask, k * jnp.exp(lg * expo), 0.0)
                term = _dot_tn(kd, v_ref[:, h * LANES:(h + 1) * LANES])
                acc = term if acc is None else acc + term
            lg_rows = jnp.where(lax.broadcasted_iota(jnp.int32, (LANES, 1), 0) < 64,
                                lg_ref[d, 2 * pair], lg_ref[d, 2 * pair + 1])
            st_ref[0, d] = acc + s0_ref[0, d] * jnp.exp(lg_rows * n)


def _retention(log_g, p, q_blk, k_blk, v_blk, g_blk, s0, gn, batch, n, row0, tq):
    nqb = n // tq
    qb0 = row0 // tq
    kb0 = row0 // n
    pairs = RET_HEADS // 2
    return pl.pallas_call(
        functools.partial(_ret_kernel, n=n, tq=tq),
        grid=(batch, pairs, nqb),
        in_specs=[pl.BlockSpec(memory_space=pltpu.SMEM),
                  pl.BlockSpec((tq, LANES), lambda b, h, i: (qb0 + b * nqb + i, q_blk + h)),
                  pl.BlockSpec((n, LANES), lambda b, h, i: (kb0 + b, k_blk + h)),
                  pl.BlockSpec((n, 2 * LANES), lambda b, h, i: (kb0 + b, v_blk // 2 + h)),
                  pl.BlockSpec((tq, 2 * LANES), lambda b, h, i: (qb0 + b * nqb + i, g_blk // 2 + h)),
                  pl.BlockSpec((1, 2, LANES, LANES), lambda b, h, i: (b, 0, h, 0)),
                  pl.BlockSpec((1, 2 * LANES), lambda b, h, i: (0, h))],
        out_specs=[pl.BlockSpec((tq, 2 * LANES), lambda b, h, i: (b * nqb + i, h)),
                   pl.BlockSpec((1, 2, LANES, LANES), lambda b, h, i: (b, 0, h, 0))],
        out_shape=[jax.ShapeDtypeStruct((batch * n, RET_HEADS * RET_DV), F32),
                   jax.ShapeDtypeStruct((batch, 2, RET_HEADS * RET_DK, RET_DV), F32)],
        compiler_params=_params("parallel", "parallel", "arbitrary"),
        name="retention",
    )(log_g, p, p, p, p, s0, gn)


def _diff_qk_kernel(q_ref, k_ref, qn_ref, kn_ref, c_ref, s1_ref, s2_ref, qo_ref, ko_ref):
    c, s1, s2 = c_ref[...], s1_ref[...], s2_ref[...]
    for src, gain, dst in ((q_ref, qn_ref, qo_ref), (k_ref, kn_ref, ko_ref)):
        for h in range(DIFF_HEADS):
            sl = slice(h * LANES, (h + 1) * LANES)
            y = src[:, sl]
            y = y * lax.rsqrt(_seg64_sum(y * y) * (1.0 / DIFF_DH) + EPS) * gain[...]
            dst[:, sl] = _rope(y, c, s1, s2)


def _diff_qk(p, qn_p, kn_p, tabs):
    tm = _ROPE_TM
    tab_spec = pl.BlockSpec((tm, LANES), lambda i: (_rope_block(i), 0))
    return pl.pallas_call(
        _diff_qk_kernel,
        grid=(N_TOK // tm,),
        in_specs=[pl.BlockSpec((tm, DIFF_W), lambda i: (i, 0)),
                  pl.BlockSpec((tm, DIFF_W), lambda i: (i, 1)),
                  pl.BlockSpec((1, LANES), lambda i: (0, 0)),
                  pl.BlockSpec((1, LANES), lambda i: (0, 0)),
                  tab_spec, tab_spec, tab_spec],
        out_specs=[pl.BlockSpec((tm, DIFF_W), lambda i: (i, 0)),
                   pl.BlockSpec((tm, DIFF_W), lambda i: (i, 0))],
        out_shape=[jax.ShapeDtypeStruct((N_TOK, DIFF_W), F32)] * 2,
        compiler_params=_params("parallel"),
        name="diff_qk",
    )(p, p, qn_p, kn_p, *tabs)


def _diff_attn_kernel(lam_ref, q_ref, k_ref, v_ref, gn_ref, o_ref, *, lam_init):
    lv = lam_ref[...]
    lam = (jnp.exp(jnp.sum(lv[0:1] * lv[1:2], axis=-1, keepdims=True))
           - jnp.exp(jnp.sum(lv[2:3] * lv[3:4], axis=-1, keepdims=True)) + lam_init)
    scale = DIFF_DH ** -0.5
    q = q_ref[...]
    k = k_ref[...]
    lo = _lane_lo((1, LANES))
    kb = k.astype(BF16)
    p1, l1 = _softmax_rows(_dot_nt_bf16(jnp.where(lo, q, 0.0), kb) * scale)
    p2, l2 = _softmax_rows(_dot_nt_bf16(jnp.where(lo, 0.0, q), kb) * scale)
    w = p1 / l1 - lam * (p2 / l2)
    o = _dot_bf16(w, v_ref[...])
    o_ref[...] = _rms(o, 2 * DIFF_DH) * gn_ref[...] * (1.0 - lam_init)


def _diff_attn(lam, q, k, v, gn, batch, nq, nk, q_row0, tq, lam_init):
    nqb = nq // tq
    qb0 = q_row0 // tq
    return pl.pallas_call(
        functools.partial(_diff_attn_kernel, lam_init=lam_init),
        grid=(batch, DIFF_HEADS, nqb),
        in_specs=[pl.BlockSpec((4, DIFF_DH), lambda b, h, i: (0, 0)),
                  pl.BlockSpec((tq, LANES), lambda b, h, i: (qb0 + b * nqb + i, h)),
                  pl.BlockSpec((nk, LANES), lambda b, h, i: (b, h)),
                  pl.BlockSpec((nk, LANES), lambda b, h, i: (b, h)),
                  pl.BlockSpec((1, LANES), lambda b, h, i: (0, h))],
        out_specs=pl.BlockSpec((tq, LANES), lambda b, h, i: (b * nqb + i, h)),
        out_shape=jax.ShapeDtypeStruct((batch * nq, DIFF_W), F32),
        compiler_params=_params("parallel", "parallel", "arbitrary"),
        name="diff_attn",
    )(lam, q, k, v, gn)


def _seg64_sum_wide(x):
    return jnp.concatenate([_seg64_sum(x[:, j * LANES:(j + 1) * LANES]) for j in range(x.shape[1] // LANES)], axis=1)


_Q_R, _Q_V, _Q_KK = 0, 1, 2
_Q_DIR = 3
_N_Q = _Q_DIR + 2 * 3


_N_CTX_TILES_PRE = N_CTX // 256
_LAT_SLOTS = 3


def _rwkv_pre_kernel(r_ref, k_ref, v_ref, lo_ref, wup_ref, aup_ref, gup_ref, w0_ref, a0_ref, kk_ref, ka_ref, rk_ref,
                     pk_ref, lat_ref, g_ref, bonus_ref):
    i = pl.program_id(0)
    W = RWKV_W
    col = lambda q: slice(q * W, (q + 1) * W)
    r = r_ref[...]
    k = k_ref[...]
    v = v_ref[...]
    lora = lo_ref[...]
    kk = k * kk_ref[...]
    kkn = kk * lax.rsqrt(_seg64_sum_wide(kk * kk) + EPS)
    g_ref[...] = _dot_bf16(_sigmoid(lora[:, 2 * LANES:3 * LANES]), gup_ref[...])
    pre = w0_ref[...] + _dot_bf16(jnp.tanh(lora[:, 0:LANES]), wup_ref[...])
    decay = jnp.exp(-jnp.exp(-_softplus(-pre) - 0.5))
    a = _sigmoid(a0_ref[...] + _dot_bf16(lora[:, LANES:2 * LANES], aup_ref[...]))
    per_dir = []
    bonus = None
    for d in range(2):
        a_d = a[:, col(d)]
        k_d = k * (1.0 + (a_d - 1.0) * ka_ref[...])
        per_dir.append((decay[:, col(d)], k_d, kkn * a_d))
        t = _seg64_sum_wide(r * k_d * rk_ref[...])
        bonus = t if bonus is None else bonus + t
    bonus_ref[...] = bonus

    @pl.when(i < _N_CTX_TILES_PRE)
    def _():
        pk_ref[:, col(_Q_R)] = r
        pk_ref[:, col(_Q_V)] = v
        pk_ref[:, col(_Q_KK)] = kkn
        for d in range(2):
            for j in range(3):
                pk_ref[:, col(_Q_DIR + 3 * d + j)] = per_dir[d][j]

    @pl.when(i >= _N_CTX_TILES_PRE)
    def _():
        lo = _lane_lo((1, LANES))
        for d in range(2):
            w_d, k_d, b_d = per_dir[d]
            for s, (x1, x2) in enumerate(((kkn, w_d), (k_d, b_d), (r, v))):
                for h in range(RWKV_HEADS):
                    blk = slice((h // 2) * LANES, (h // 2 + 1) * LANES)
                    if h % 2 == 0:
                        out = jnp.where(lo, x1[:, blk], pltpu.roll(x2[:, blk], RWKV_HS, axis=1))
                    else:
                        out = jnp.where(lo, pltpu.roll(x1[:, blk], RWKV_HS, axis=1), x2[:, blk])
                    lat_ref[:, (d * _LAT_SLOTS + s) * RWKV_HEADS + h, :] = out


def _rwkv_pre(p, r_blk, k_blk, v_blk, lo_blk, wup_bd, aup_bd, gup, w0, a0, k_k, k_a, r_k):
    tm = 256
    w = RWKV_W
    n_ctx = _N_CTX_TILES_PRE
    n_j = 2 * _LAT_SLOTS * RWKV_HEADS
    row = lambda n: pl.BlockSpec((1, n), lambda i: (0, 0))
    full = lambda a, b: pl.BlockSpec((a, b), lambda i: (0, 0))
    return pl.pallas_call(
        _rwkv_pre_kernel,
        grid=(N_TOK // tm,),
        in_specs=[pl.BlockSpec((tm, w), lambda i: (i, r_blk)),
                  pl.BlockSpec((tm, w), lambda i: (i, k_blk)),
                  pl.BlockSpec((tm, w), lambda i: (i, v_blk)),
                  pl.BlockSpec((tm, 3 * LANES), lambda i: (i, lo_blk)),
                  full(LANES, 2 * w), full(LANES, 2 * w), full(LANES, w),
                  row(2 * w), row(2 * w), row(w), row(w), row(w)],
        out_specs=[pl.BlockSpec((tm, _N_Q * w), lambda i: (jnp.minimum(i, n_ctx - 1), 0)),
                   pl.BlockSpec((tm, n_j, LANES), lambda i: (jnp.maximum(i - n_ctx, 0), 0, 0)),
                   pl.BlockSpec((tm, w), lambda i: (i, 0)), pl.BlockSpec((tm, w), lambda i: (i, 0))],
        out_shape=[jax.ShapeDtypeStruct((N_CTX, _N_Q * w), F32), jax.ShapeDtypeStruct((N_LAT, n_j, LANES), F32),
                   jax.ShapeDtypeStruct((N_TOK, w), F32), jax.ShapeDtypeStruct((N_TOK, w), F32)],
        compiler_params=_params("arbitrary"),
        name="rwkv_pre",
    )(p, p, p, p, wup_bd, aup_bd, gup, w0, a0, k_k, k_a, r_k)


_SCAN_CHUNK = 32
_SCAN_UNROLL = 16


def _rwkv_first_sa(s_ref, sa_ref, kk, n_k):
    nv = s_ref.shape[1]
    chunk = min(_SCAN_CHUNK, nv)
    for c0 in range(0, nv, chunk):
        def body(k, acc):
            return acc + s_ref[k, c0:c0 + chunk, :] * kk(k)
        sa_ref[c0:c0 + chunk, :] = lax.fori_loop(0, n_k, body, jnp.zeros((chunk, LANES), F32), unroll=_SCAN_UNROLL)


def _rwkv_step(s_ref, sa_ref, kk_next, w, kd, b, r, v_at, n_k):
    nv = s_ref.shape[1]
    chunk = min(_SCAN_CHUNK, nv)
    ys = []
    for c0 in range(0, nv, chunk):
        sa = sa_ref[c0:c0 + chunk, :]
        vc = v_at(c0, chunk)

        def body(k, acc):
            y_acc, sa_acc = acc
            s_new = s_ref[k, c0:c0 + chunk, :] * w(k) - sa * b(k) + vc * kd(k)
            s_ref[k, c0:c0 + chunk, :] = s_new
            return y_acc + s_new * r(k), sa_acc + s_new * kk_next(k)

        zero = jnp.zeros((chunk, LANES), F32)
        y_acc, sa_acc = lax.fori_loop(0, n_k, body, (zero, zero), unroll=_SCAN_UNROLL)
        sa_ref[c0:c0 + chunk, :] = sa_acc
        ys.append(y_acc)
    return ys[0] if len(ys) == 1 else jnp.concatenate(ys, axis=0)


_CTX_TB = 32


def _rwkv_scan_ctx_kernel(xs_ref, xd_ref, y_ref, st_ref, s_ref, sa_ref):
    d = pl.program_id(0)
    tb = pl.program_id(1)
    step_t = lambda i: jnp.where(d == 0, i, _CTX_TB - 1 - i)

    @pl.when(tb == 0)
    def _():
        s_ref[...] = jnp.zeros_like(s_ref)

    t0 = step_t(0)
    _rwkv_first_sa(s_ref, sa_ref, lambda k: xs_ref[t0, _Q_KK, pl.ds(k, 1), :], RWKV_HS)

    def step(i, carry):
        t = step_t(i)
        tn = step_t(jnp.minimum(i + 1, _CTX_TB - 1))
        shared = lambda q, tt: (lambda k: xs_ref[tt, q, pl.ds(k, 1), :])
        per_dir = lambda q: (lambda k: xd_ref[t, q, pl.ds(k, 1), :])
        y_ref[t, 0] = _rwkv_step(s_ref, sa_ref, shared(_Q_KK, tn), per_dir(0), per_dir(1), per_dir(2),
                                 shared(_Q_R, t), lambda c0, n: xs_ref[t, _Q_V, pl.ds(c0, n), :], RWKV_HS)
        return carry

    lax.fori_loop(0, _CTX_TB, step, 0)

    @pl.when(tb == pl.num_programs(1) - 1)
    def _():
        st_ref[0] = s_ref[...]


def _rwkv_scan_ctx(pkt):
    nt = SEQ // _CTX_TB
    hs = RWKV_HS
    tblk = lambda d, tb: jnp.where(d == 0, tb, nt - 1 - tb)
    return pl.pallas_call(
        _rwkv_scan_ctx_kernel,
        grid=(2, nt),
        in_specs=[pl.BlockSpec((_CTX_TB, 3, hs, LANES), lambda d, tb: (tblk(d, tb), 0, 0, 0)),
                  pl.BlockSpec((_CTX_TB, 3, hs, LANES), lambda d, tb: (tblk(d, tb), 1 + d, 0, 0))],
        out_specs=[pl.BlockSpec((_CTX_TB, 1, hs, LANES), lambda d, tb: (tblk(d, tb), d, 0, 0)),
                   pl.BlockSpec((1, hs, hs, LANES), lambda d, tb: (d, 0, 0, 0))],
        out_shape=[jax.ShapeDtypeStruct((SEQ, 2, hs, LANES), F32),
                   jax.ShapeDtypeStruct((2, hs, hs, LANES), F32)],
        scratch_shapes=[pltpu.VMEM((hs, hs, LANES), F32), pltpu.VMEM((hs, LANES), F32)],
        compiler_params=_params("parallel", "arbitrary"),
        name="rwkv_scan_ctx",
    )(pkt, pkt)


_LAT_TB = 32
_LAT_VSPLIT = 4
_LAT_STATES = 2 * DEC_BATCH * RWKV_HEADS
_LAT_VROWS = RWKV_HS // _LAT_VSPLIT


def _rwkv_scan_lat_kernel(xf_ref, xb_ref, s0_ref, yf_ref, yb_ref, s_ref, sa_ref, kt_ref, v_ref, ys_ref):
    @pl.when(pl.program_id(0) == 0)
    def _():
        s_ref[...] = s0_ref[...]

    group = lax.broadcasted_iota(jnp.int32, (_LAT_VROWS, LANES), 1) // _LAT_STATES
    nh = RWKV_HEADS

    def load_t(t, carry):
        tr = _LAT_TB - 1 - t
        for s in range(_LAT_SLOTS):
            heads = slice(s * nh, (s + 1) * nh)
            x = jnp.concatenate([xf_ref[0, t, heads, :], xf_ref[1, t, heads, :],
                                 xb_ref[0, tr, heads, :], xb_ref[1, tr, heads, :]], axis=0)
            xt = jnp.concatenate([x] * _LAT_VSPLIT, axis=0).T
            kt_ref[t, 2 * s] = xt[:RWKV_HS]
            if s < _LAT_SLOTS - 1:
                kt_ref[t, 2 * s + 1] = xt[RWKV_HS:]
            else:
                v = jnp.zeros((_LAT_VROWS, LANES), F32)
                for g in range(_LAT_VSPLIT):
                    r0 = RWKV_HS + g * _LAT_VROWS
                    v = jnp.where(group == g, xt[r0:r0 + _LAT_VROWS, :], v)
                v_ref[t] = v
        return carry

    lax.fori_loop(0, _LAT_TB, load_t, 0, unroll=4)

    _rwkv_first_sa(s_ref, sa_ref, lambda k: kt_ref[0, 0, pl.ds(k, 1), :], RWKV_HS)

    def step(t, carry):
        tn = jnp.minimum(t + 1, _LAT_TB - 1)
        row = lambda q, tt: (lambda k: kt_ref[tt, q, pl.ds(k, 1), :])
        ys_ref[t] = _rwkv_step(s_ref, sa_ref, row(0, tn), row(1, t), row(2, t), row(3, t), row(4, t),
                               lambda c0, n: v_ref[t, pl.ds(c0, n), :], RWKV_HS)
        return carry

    lax.fori_loop(0, _LAT_TB, step, 0)

    def store_t(i, carry):
        rows = [jnp.where(group == g, ys_ref[2 * i + j], 0.0) for j in range(2) for g in range(_LAT_VSPLIT)]
        z = jnp.concatenate(rows, axis=0).T
        y = (z[0:_LAT_STATES] + z[_LAT_STATES:2 * _LAT_STATES]
             + z[2 * _LAT_STATES:3 * _LAT_STATES] + z[3 * _LAT_STATES:4 * _LAT_STATES])
        yb = pltpu.roll(y[_LAT_STATES // 2:], RWKV_HS, axis=1)
        for b in range(DEC_BATCH):
            yf_ref[b, i] = y[b * nh:(b + 1) * nh]
            yb_ref[b, _LAT_TB // 2 - 1 - i] = yb[b * nh:(b + 1) * nh]
        return carry

    lax.fori_loop(0, _LAT_TB // 2, store_t, 0, unroll=4)


def _rwkv_scan_lat(lat, s0):
    hs = RWKV_HS
    nv = _LAT_VROWS
    nt = DEC_SEQ // _LAT_TB
    nj = _LAT_SLOTS * RWKV_HEADS
    x4 = lat.reshape(DEC_BATCH, DEC_SEQ, 2 * nj, LANES)
    y_shape = jax.ShapeDtypeStruct((DEC_BATCH, DEC_SEQ // 2, RWKV_HEADS, 2 * hs), F32)
    y_blk = (DEC_BATCH, _LAT_TB // 2, RWKV_HEADS, 2 * hs)
    return pl.pallas_call(
        _rwkv_scan_lat_kernel,
        grid=(nt,),
        in_specs=[pl.BlockSpec((DEC_BATCH, _LAT_TB, nj, LANES), lambda tb: (0, tb, 0, 0)),
                  pl.BlockSpec((DEC_BATCH, _LAT_TB, nj, LANES), lambda tb: (0, nt - 1 - tb, 1, 0)),
                  pl.BlockSpec((hs, nv, LANES), lambda tb: (0, 0, 0))],
        out_specs=[pl.BlockSpec(y_blk, lambda tb: (0, tb, 0, 0)),
                   pl.BlockSpec(y_blk, lambda tb: (0, nt - 1 - tb, 0, 0))],
        out_shape=[y_shape, y_shape],
        scratch_shapes=[pltpu.VMEM((hs, nv, LANES), F32), pltpu.VMEM((nv, LANES), F32),
                        pltpu.VMEM((_LAT_TB, 2 * _LAT_SLOTS - 1, hs, LANES), F32),
                        pltpu.VMEM((_LAT_TB, nv, LANES), F32), pltpu.VMEM((_LAT_TB, nv, LANES), F32)],
        compiler_params=_params("arbitrary"),
        name="rwkv_scan_lat",
    )(x4, x4, s0)


def _rwkv_post_kernel(yf_ref, yb_ref, bonus_ref, v_ref, g_ref, gn_ref, o_ref):
    y = yf_ref[...] + yb_ref[...]
    y = y * lax.rsqrt(_seg64_sum_wide(y * y) * (1.0 / RWKV_HS) + EPS) * gn_ref[...]
    o_ref[...] = (y + bonus_ref[...] * v_ref[...]) * g_ref[...]


def _rwkv_post(y, bonus, p, v_blk, g, gn):
    tm = 512
    w = RWKV_W
    spec = pl.BlockSpec((tm, w), lambda i: (i, 0))
    return pl.pallas_call(
        _rwkv_post_kernel,
        grid=(N_TOK // tm,),
        in_specs=[spec, pl.BlockSpec((tm, w), lambda i: (i, 1)), spec,
                  pl.BlockSpec((tm, w), lambda i: (i, v_blk)), spec,
                  pl.BlockSpec((1, w), lambda i: (0, 0))],
        out_specs=spec,
        out_shape=jax.ShapeDtypeStruct((N_TOK, w), F32),
        compiler_params=_params("parallel"),
        name="rwkv_post",
    )(y, y, bonus, p, g, gn)


def _ctx_scan_operands(pk):
    x = pk.reshape(BATCH, SEQ, _N_Q, RWKV_HEADS, RWKV_HS)
    return jnp.transpose(x, (1, 2, 4, 0, 3)).reshape(SEQ, _N_Q, RWKV_HS, LANES)


def _ctx_scan_result(y):
    y = y.reshape(SEQ, 2, RWKV_HS, BATCH, RWKV_HEADS)
    return jnp.transpose(y, (3, 0, 1, 4, 2)).reshape(N_CTX, 2 * RWKV_W)


def _value_split_layout(x):
    lead = x.shape[:-2]
    n = len(lead)
    x = x.reshape(lead + (_LAT_STATES, _LAT_VSPLIT, _LAT_VROWS))
    return jnp.transpose(x, tuple(range(n)) + (n + 2, n + 1, n)).reshape(lead + (_LAT_VROWS, LANES))


def _lat_scan_result(yf, yb):
    def rows(y):
        y = y.reshape(DEC_BATCH, DEC_SEQ // 2, RWKV_HEADS, 2, RWKV_HS)
        return jnp.transpose(y, (0, 1, 3, 2, 4)).reshape(N_LAT, RWKV_W)
    return jnp.concatenate([rows(yf), rows(yb)], -1)


def _even_layer(x, mod, g_mix, w_in, q_norm, kv_norm, w_uq, w_ukv, qn, kn, ret_decay, ret_gn,
                cache_ckv, cache_krope, state_ret, tabs_m):
    cq, ckv, krope, rq, rk, rv, rg = jnp.split(w_in, np.cumsum(
        (MLA_Q_RANK, MLA_KV_RANK, MLA_ROPE, RET_HEADS * RET_DK, RET_HEADS * RET_DK, RET_HEADS * RET_DV))[:].tolist(),
        axis=1)
    z = lambda n: jnp.zeros((D_MODEL, n), F32)
    w_p = jnp.concatenate([cq, ckv, z(MLA_NOPE), krope, z(LANES - MLA_QK), rq, rk, rv, rg], axis=1)
    p = _inproj(x, g_mix, mod, w_p, jnp.zeros((1, w_p.shape[1]), F32), 512, False)
    CKV_BLK, KR_BLK, RQ_BLK, RK_BLK, RV_BLK, RG_BLK = 2, 3, 4, 6, 8, 12

    def head_pad(w, n_head, d_head, c0, c1):
        w = w.reshape(w.shape[0], n_head, d_head)[:, :, c0:c1]
        return jnp.pad(w, ((0, 0), (0, 0), (0, LANES - (c1 - c0)))).reshape(w.shape[0], n_head * LANES)

    w_uq_p = head_pad(w_uq, MLA_HEADS, MLA_QK, 0, MLA_QK)
    wk_p = head_pad(w_ukv, MLA_HEADS, MLA_NOPE + MLA_V, 0, MLA_NOPE)
    wv_p = head_pad(w_ukv, MLA_HEADS, MLA_NOPE + MLA_V, MLA_NOPE, MLA_NOPE + MLA_V)
    qn_p = jnp.pad(qn, (0, LANES - MLA_QK))[None]
    kn_p = jnp.pad(kn, (0, LANES - MLA_QK))[None]

    q = _mla_q(p, q_norm[None], w_uq_p, qn_p, tabs_m)
    k, v, ckvn = _mla_kv(p, CKV_BLK, p, KR_BLK, kv_norm[None], wk_p, wv_p, kn_p, tabs_m, N_TOK, True)

    n_c = DEC_BATCH * PAST_LEN
    kr_c = jnp.pad(cache_krope.reshape(n_c, MLA_ROPE), ((0, 0), (MLA_NOPE, LANES - MLA_QK)))
    k_c, v_c, _ = _mla_kv(cache_ckv.reshape(n_c, MLA_KV_RANK), 0, kr_c, 0, kv_norm[None], wk_p, wv_p, kn_p,
                          tabs_m, n_c, False)

    hw = MLA_HEADS * LANES

    def with_cache(own, cache):
        own = own[N_CTX:].reshape(DEC_BATCH, DEC_SEQ, hw)
        return jnp.concatenate([own, cache.reshape(DEC_BATCH, PAST_LEN, hw)], 1).reshape(-1, hw)

    o_ctx = _mla_attn(q, k, v, BATCH, SEQ, SEQ, 0, SEQ)
    o_lat = _mla_attn(q, with_cache(k, k_c), with_cache(v, v_c), DEC_BATCH, DEC_SEQ, DEC_SEQ + PAST_LEN, N_CTX, 256)

    log_g = -_softplus(-ret_decay)
    gn = ret_gn[None]
    s0_ctx = jnp.zeros((BATCH, 2, RET_HEADS * RET_DK, RET_DV), F32)
    r_ctx, st_ctx = _retention(log_g, p, RQ_BLK, RK_BLK, RV_BLK, RG_BLK, s0_ctx, gn, BATCH, SEQ, 0, SEQ)
    s0_lat = state_ret.reshape(DEC_BATCH, 2, RET_HEADS * RET_DK, RET_DV)
    r_lat, _ = _retention(log_g, p, RQ_BLK, RK_BLK, RV_BLK, RG_BLK, s0_lat, gn, DEC_BATCH, DEC_SEQ, N_CTX, 256)

    mix = [jnp.concatenate([o_ctx, o_lat], 0), jnp.concatenate([r_ctx, r_lat], 0)]
    new_ckv = ckvn[:N_CTX].reshape(BATCH, SEQ, MLA_KV_RANK)
    new_krope = p[:N_CTX, KR_BLK * LANES + MLA_NOPE:KR_BLK * LANES + MLA_QK]
    new_krope = new_krope.reshape(BATCH, SEQ, MLA_ROPE)
    new_ret = st_ctx.reshape(BATCH, 2, RET_HEADS, RET_DK, RET_DV)
    return mix, new_ckv, new_krope, new_ret


def _odd_layer(x, mod, g_mix, w_in, qn, kn, lam, diff_gn, mu, w0, w_up, a0, a_up, g_up, k_k, k_a, r_k, gn,
               cache_k, cache_v, state_rwkv, tabs_d, lam_init):
    w_p = w_in
    n_in = w_p.shape[1]
    mu_full = jnp.concatenate([jnp.zeros((3 * DIFF_W,), F32), mu])[None]
    p = _inproj(x, g_mix, mod, w_p, mu_full, 384, True)
    DV_BLK, R_BLK, K_BLK, V_BLK = 2, 3, 4, 5
    LO_BLK = (6 * RWKV_W) // (3 * LANES)

    qn_p = jnp.tile(qn, 2)[None]
    kn_p = jnp.tile(kn, 2)[None]
    q, k = _diff_qk(p, qn_p, kn_p, tabs_d)
    v = p[:, DV_BLK * DIFF_W:(DV_BLK + 1) * DIFF_W]

    n_c = DEC_BATCH * PAST_LEN
    k_c = cache_k.reshape(DEC_BATCH, PAST_LEN, DIFF_W)
    v_c = cache_v.reshape(DEC_BATCH, PAST_LEN, DIFF_W)

    def with_cache(own, cache):
        return jnp.concatenate([own[N_CTX:].reshape(DEC_BATCH, DEC_SEQ, DIFF_W), cache], 1).reshape(-1, DIFF_W)

    dgn = diff_gn[None]
    o_ctx = _diff_attn(lam, q, k, v, dgn, BATCH, SEQ, SEQ, 0, SEQ, lam_init)
    o_lat = _diff_attn(lam, q, with_cache(k, k_c), with_cache(v, v_c), dgn, DEC_BATCH, DEC_SEQ, DEC_SEQ + PAST_LEN,
                       N_CTX, 256, lam_init)

    zero = jnp.zeros((RWKV_W_LORA, RWKV_W), F32)
    wup_bd = jnp.concatenate([jnp.concatenate([w_up[0], zero], 1), jnp.concatenate([zero, w_up[1]], 1)], 0)
    aup_bd = jnp.concatenate([jnp.concatenate([a_up[0], zero], 1), jnp.concatenate([zero, a_up[1]], 1)], 0)
    pk, lat, g, bonus = _rwkv_pre(p, R_BLK, K_BLK, V_BLK, LO_BLK, wup_bd, aup_bd, g_up, w0.reshape(1, -1),
                             a0.reshape(1, -1), k_k[None], k_a[None], r_k.reshape(1, -1))

    y_ctx, st_ctx = _rwkv_scan_ctx(_ctx_scan_operands(pk))
    s0_lat = jnp.transpose(state_rwkv, (4, 1, 0, 2, 3)).reshape(RWKV_HS, _LAT_STATES, RWKV_HS)
    y_lat = _rwkv_scan_lat(lat, _value_split_layout(s0_lat))
    y = jnp.concatenate([_ctx_scan_result(y_ctx), _lat_scan_result(*y_lat)], 0)
    rw_o = _rwkv_post(y, bonus, p, V_BLK, g, gn[None])

    mix = [jnp.concatenate([o_ctx, o_lat], 0), rw_o]
    new_dk = k[:N_CTX].reshape(BATCH, SEQ, DIFF_HEADS, 2, DIFF_DH)
    new_dv = v[:N_CTX].reshape(BATCH, SEQ, DIFF_HEADS, 2 * DIFF_DH)
    new_rwkv = jnp.transpose(st_ctx.reshape(2, RWKV_HS, RWKV_HS, BATCH, RWKV_HEADS), (3, 0, 4, 2, 1))
    return mix, new_dk, new_dv, new_rwkv


def kernel(x_prompt, x_sample, cache_mla_ckv, cache_mla_krope, state_ret, cache_diff_k, cache_diff_v, state_rwkv,
           c, c_ctx, ada_w, ada_b, norm_mix_g, norm_ffn_g, w_out, ffn_up, ffn_conv_w, ffn_conv_b, ffn_down,
           a_w_in, mla_q_norm, mla_kv_norm, mla_w_uq, mla_w_ukv, mla_qn, mla_kn, ret_decay, ret_gn,
           b_w_in, diff_qn, diff_kn, diff_lam, diff_gn, rwkv_mu, rwkv_w0, rwkv_w_up, rwkv_a0, rwkv_a_up,
           rwkv_g_up, rwkv_k_k, rwkv_k_a, rwkv_r_k, rwkv_gn):
    x = jnp.concatenate([x_prompt.reshape(N_CTX, D_MODEL), x_sample.reshape(N_LAT, D_MODEL)], 0)
    cond8 = jnp.pad(jnp.concatenate([c_ctx[None], c], 0), ((0, 8 - N_GROUPS), (0, 0)))
    mod = _modulation(cond8, ada_w, ada_b)

    tabs_m = _rope_tables(MLA_ROPE, (MLA_NOPE,))
    tabs_d = _rope_tables(DIFF_DH, (0, DIFF_DH))

    outs = {}
    for l in range(DEPTH):
        j = l // 2
        g_mix = norm_mix_g[l][None]
        if l % 2 == 0:
            mix, outs["ckv"], outs["krope"], outs["ret"] = _even_layer(
                x, mod[l], g_mix, a_w_in[j], mla_q_norm[j], mla_kv_norm[j], mla_w_uq[j], mla_w_ukv[j], mla_qn[j],
                mla_kn[j], ret_decay[j], ret_gn[j], cache_mla_ckv[:, j], cache_mla_krope[:, j], state_ret[:, j],
                tabs_m)
        else:
            lam_init = 0.8 - 0.6 * math.exp(-0.3 * l)
            mix, outs["dk"], outs["dv"], outs["rwkv"] = _odd_layer(
                x, mod[l], g_mix, b_w_in[j], diff_qn[j], diff_kn[j], diff_lam[j], diff_gn[j], rwkv_mu[j],
                rwkv_w0[j], rwkv_w_up[j], rwkv_a0[j], rwkv_a_up[j], rwkv_g_up[j], rwkv_k_k[j], rwkv_k_a[j],
                rwkv_r_k[j], rwkv_gn[j], cache_diff_k[:, j], cache_diff_v[:, j], state_rwkv[:, j], tabs_d, lam_init)
        x = _resid_proj(mix, w_out[l], x, mod[l], 2)
        act = _ffn_up(x, norm_ffn_g[l][None], mod[l], ffn_up[l], ffn_conv_w[l], ffn_conv_b[l])
        x = _resid_proj([act], ffn_down[l], x, mod[l], 5)

    y_prompt = x[:N_CTX].reshape(BATCH, SEQ, D_MODEL)
    y_sample = x[N_CTX:].reshape(DEC_BATCH, DEC_SEQ, D_MODEL)
    return (y_prompt, y_sample, outs["ckv"][:, None], outs["krope"][:, None], outs["ret"][:, None],
            outs["dk"][:, None], outs["dv"][:, None], outs["rwkv"][:, None])
```

```python
import functools
import math

import numpy as np
import jax
import jax.numpy as jnp
from jax import lax
from jax.experimental import pallas as pl
from jax.experimental.pallas import tpu as pltpu

D_MODEL = 1024
BATCH = 16
SEQ = 256
DEPTH = 2
DEC_BATCH = 2
DEC_SEQ = 1024
PAST_LEN = 512
GRID_W = 64
EPS = 1e-6
ROPE_BASE = 10000.0

MLA_HEADS = 8
MLA_Q_RANK = 256
MLA_KV_RANK = 128
MLA_NOPE = 64
MLA_ROPE = 32
MLA_V = 64
MLA_QK = MLA_NOPE + MLA_ROPE
RET_HEADS = 4
RET_DK = 64
RET_DV = 128
DIFF_HEADS = 4
DIFF_DH = 64
DIFF_W = DIFF_HEADS * 2 * DIFF_DH
RWKV_HEADS = 8
RWKV_HS = 64
RWKV_W = RWKV_HEADS * RWKV_HS
RWKV_W_LORA = 64
RWKV_A_LORA = 64
RWKV_G_LORA = 128
D_FF = 2816

N_CTX = BATCH * SEQ
N_LAT = DEC_BATCH * DEC_SEQ
N_TOK = N_CTX + N_LAT
N_GROUPS = 1 + DEC_BATCH

LANES = 128
VMEM_LIMIT = 56 * 1024 * 1024

_PREC = lax.Precision.HIGHEST
F32 = jnp.float32


def _dot_tn(a, b):
    return lax.dot_general(a, b, (((0,), (0,)), ((), ())), precision=_PREC, preferred_element_type=F32)


BF16 = jnp.bfloat16


def _dot_bf16(a, b):
    return jnp.dot(a.astype(BF16), b.astype(BF16), preferred_element_type=F32)


def _dot_nt_bf16(a, b):
    return lax.dot_general(a.astype(BF16), b.astype(BF16), (((1,), (1,)), ((), ())), preferred_element_type=F32)


def _params(*sem):
    return pltpu.CompilerParams(dimension_semantics=sem, vmem_limit_bytes=VMEM_LIMIT)


def _sigmoid(x):
    return 1.0 / (1.0 + jnp.exp(-x))


def _silu(x):
    return x * _sigmoid(x)


def _softplus(x):
    return jnp.maximum(x, 0.0) + jnp.log(1.0 + jnp.exp(-jnp.abs(x)))


def _rms(x, n):
    return x * lax.rsqrt(jnp.sum(x * x, axis=-1, keepdims=True) * (1.0 / n) + EPS)


def _lane_lo(shape):
    return lax.broadcasted_iota(jnp.int32, shape, len(shape) - 1) < 64


def _seg64_sum(x):
    lo = _lane_lo(x.shape)
    s_lo = jnp.sum(jnp.where(lo, x, 0.0), axis=-1, keepdims=True)
    s_hi = jnp.sum(jnp.where(lo, 0.0, x), axis=-1, keepdims=True)
    return jnp.where(lo, s_lo, s_hi)


def _seq_neighbours(p, tile, tile_rows):
    seq_mask = jnp.where(tile * tile_rows < N_CTX, SEQ - 1, DEC_SEQ - 1)
    pos = lax.broadcasted_iota(jnp.int32, (tile_rows, 1), 0) & seq_mask
    prev = jnp.where(pos == 0, 0.0, pltpu.roll(p, 1, axis=0))
    nxt = jnp.where(pos == seq_mask, 0.0, pltpu.roll(p, tile_rows - 1, axis=0))
    return prev, nxt


def _group_of_tile(i, tile_rows):
    row = i * tile_rows
    return jnp.where(row < N_CTX, 0, 1 + (row - N_CTX) // DEC_SEQ)


def _modulation_kernel(c_ref, w_ref, b_ref, o_ref):
    o_ref[0] = _dot_bf16(_silu(c_ref[...]), w_ref[0]) + b_ref[0]


def _modulation(cond8, ada_w, ada_b):
    tn = 512
    n = 6 * D_MODEL
    out = pl.pallas_call(
        _modulation_kernel,
        grid=(DEPTH, n // tn),
        in_specs=[pl.BlockSpec((8, D_MODEL), lambda l, j: (0, 0)),
                  pl.BlockSpec((1, D_MODEL, tn), lambda l, j: (l, 0, j)),
                  pl.BlockSpec((1, 1, tn), lambda l, j: (l, 0, j))],
        out_specs=pl.BlockSpec((1, 8, tn), lambda l, j: (l, 0, j)),
        out_shape=jax.ShapeDtypeStruct((DEPTH, 8, n), F32),
        compiler_params=_params("parallel", "parallel"),
        name="modulation",
    )(cond8, ada_w, ada_b.reshape(DEPTH, 1, n))
    m = out[:, :N_GROUPS].reshape(DEPTH, N_GROUPS, 6, D_MODEL)
    return jnp.pad(m, ((0, 0), (0, 0), (0, 2), (0, 0)))


_TM_SEQ = 1024


def _norm_mod(x, g, mod, off):
    return _rms(x, D_MODEL) * g * (1.0 + mod[off + 1:off + 2, :]) + mod[off:off + 1, :]


def _inproj_kernel(x_ref, g_ref, mod_ref, w_ref, mu_ref, o_ref, h_ref, *, shift):
    i = pl.program_id(0)

    @pl.when(pl.program_id(1) == 0)
    def _():
        h_ref[...] = _norm_mod(x_ref[...], g_ref[...], mod_ref[0], 0).astype(BF16)

    p = _dot_bf16(h_ref[...], w_ref[...])
    if shift:
        prev, nxt = _seq_neighbours(p, i, _TM_SEQ)
        p = p + (0.5 * (prev + nxt) - p) * mu_ref[...]
    o_ref[...] = p


def _inproj(x, g, mod, w, mu, tn, shift):
    n = w.shape[1]
    tm = _TM_SEQ
    return pl.pallas_call(
        functools.partial(_inproj_kernel, shift=shift),
        grid=(N_TOK // tm, n // tn),
        in_specs=[pl.BlockSpec((tm, D_MODEL), lambda i, j: (i, 0)),
                  pl.BlockSpec((1, D_MODEL), lambda i, j: (0, 0)),
                  pl.BlockSpec((1, 8, D_MODEL), lambda i, j: (_group_of_tile(i, tm), 0, 0)),
                  pl.BlockSpec((D_MODEL, tn), lambda i, j: (0, j)),
                  pl.BlockSpec((1, tn), lambda i, j: (0, j))],
        out_specs=pl.BlockSpec((tm, tn), lambda i, j: (i, j)),
        out_shape=jax.ShapeDtypeStruct((N_TOK, n), F32),
        scratch_shapes=[pltpu.VMEM((tm, D_MODEL), BF16)],
        compiler_params=_params("parallel", "arbitrary"),
        name="inproj_shift" if shift else "inproj",
    )(x, g, mod, w, mu)


def _resid_kernel(*refs, gate_row, n_act):
    a_refs = refs[:n_act]
    w_ref, x_ref, mod_ref, o_ref, a_bf_ref = refs[n_act:]

    @pl.when(pl.program_id(1) == 0)
    def _():
        k0 = 0
        for a_ref in a_refs:
            a_bf_ref[:, k0:k0 + a_ref.shape[1]] = a_ref[...].astype(BF16)
            k0 += a_ref.shape[1]

    o_ref[...] = x_ref[...] + mod_ref[0, gate_row:gate_row + 1, :] * _dot_bf16(a_bf_ref[...], w_ref[...])


def _resid_proj(acts, w, x, mod, gate_row):
    tm, tn = 1024, 256
    k = sum(a.shape[1] for a in acts)
    return pl.pallas_call(
        functools.partial(_resid_kernel, gate_row=gate_row, n_act=len(acts)),
        grid=(N_TOK // tm, D_MODEL // tn),
        in_specs=[pl.BlockSpec((tm, a.shape[1]), lambda i, j: (i, 0)) for a in acts]
        + [pl.BlockSpec((k, tn), lambda i, j: (0, j)),
           pl.BlockSpec((tm, tn), lambda i, j: (i, j)),
           pl.BlockSpec((1, 8, tn), lambda i, j: (_group_of_tile(i, tm), 0, j))],
        out_specs=pl.BlockSpec((tm, tn), lambda i, j: (i, j)),
        out_shape=jax.ShapeDtypeStruct((N_TOK, D_MODEL), F32),
        scratch_shapes=[pltpu.VMEM((tm, k), BF16)],
        compiler_params=_params("parallel", "arbitrary"),
        name="resid_proj",
    )(*acts, w, x, mod)


def _ffn_up_kernel(x_ref, g_ref, mod_ref, wa_ref, wb_ref, cwa_ref, cwb_ref, cba_ref, cbb_ref, o_ref, h_ref):
    i = pl.program_id(0)

    @pl.when(pl.program_id(1) == 0)
    def _():
        h_ref[...] = _norm_mod(x_ref[...], g_ref[...], mod_ref[0], 3).astype(BF16)

    h = h_ref[...]

    def conv(w_ref, cw_ref, cb_ref):
        u = _dot_bf16(h, w_ref[...])
        prev, nxt = _seq_neighbours(u, i, _TM_SEQ)
        return prev * cw_ref[0:1, :] + u * cw_ref[1:2, :] + nxt * cw_ref[2:3, :] + cb_ref[...]

    o_ref[...] = _silu(conv(wa_ref, cwa_ref, cba_ref)) * conv(wb_ref, cwb_ref, cbb_ref)


def _ffn_up(x, g, mod, up, cw, cb):
    tm, tn = _TM_SEQ, 256
    nb = D_FF // tn
    cb = cb.reshape(1, 2 * D_FF)
    return pl.pallas_call(
        _ffn_up_kernel,
        grid=(N_TOK // tm, nb),
        in_specs=[pl.BlockSpec((tm, D_MODEL), lambda i, j: (i, 0)),
                  pl.BlockSpec((1, D_MODEL), lambda i, j: (0, 0)),
                  pl.BlockSpec((1, 8, D_MODEL), lambda i, j: (_group_of_tile(i, tm), 0, 0)),
                  pl.BlockSpec((D_MODEL, tn), lambda i, j: (0, j)),
                  pl.BlockSpec((D_MODEL, tn), lambda i, j: (0, j + nb)),
                  pl.BlockSpec((3, tn), lambda i, j: (0, j)),
                  pl.BlockSpec((3, tn), lambda i, j: (0, j + nb)),
                  pl.BlockSpec((1, tn), lambda i, j: (0, j)),
                  pl.BlockSpec((1, tn), lambda i, j: (0, j + nb))],
        out_specs=pl.BlockSpec((tm, tn), lambda i, j: (i, j)),
        out_shape=jax.ShapeDtypeStruct((N_TOK, D_FF), F32),
        scratch_shapes=[pltpu.VMEM((tm, D_MODEL), BF16)],
        compiler_params=_params("parallel", "arbitrary"),
        name="ffn_up",
    )(x, g, mod, up, up, cw, cw, cb, cb)


_ROPE_TM = 512


def _rope(y, c, s1, s2):
    return y * c + pltpu.roll(y, 1, axis=1) * s1 + pltpu.roll(y, LANES - 1, axis=1) * s2


def _rope_tables(rot_dim, lane_offsets):
    t = np.arange(DEC_SEQ)
    row, col = t // GRID_W, t % GRID_W
    n_freq = rot_dim // 4
    inv = ROPE_BASE ** (-np.arange(n_freq, dtype=np.float64) / n_freq)
    ang = np.concatenate([row[:, None] * inv, col[:, None] * inv], -1)
    cos, sin = np.cos(ang), np.sin(ang)
    n = _ROPE_TM + DEC_SEQ
    c, s1, s2 = np.ones((n, LANES)), np.zeros((n, LANES)), np.zeros((n, LANES))
    for a in lane_offsets:
        even = a + 2 * np.arange(rot_dim // 2)
        c[_ROPE_TM:, even] = cos
        c[_ROPE_TM:, even + 1] = cos
        s1[_ROPE_TM:, even + 1] = sin
        s2[_ROPE_TM:, even] = -sin
    return tuple(jnp.asarray(x, F32) for x in (c, s1, s2))


def _rope_block(i):
    row = i * _ROPE_TM
    return jnp.where(row < N_CTX, 0, 1 + ((row - N_CTX) % DEC_SEQ) // _ROPE_TM)


def _mla_q_kernel(cq_ref, qnorm_ref, w_ref, qn_ref, c_ref, s1_ref, s2_ref, o_ref):
    xn = _rms(cq_ref[...], MLA_Q_RANK) * qnorm_ref[...]
    y = _dot_bf16(xn, w_ref[...])
    c, s1, s2 = c_ref[...], s1_ref[...], s2_ref[...]
    for h in range(MLA_HEADS):
        yh = y[:, h * LANES:(h + 1) * LANES]
        yh = _rms(yh, MLA_QK) * qn_ref[...]
        o_ref[:, h * LANES:(h + 1) * LANES] = _rope(yh, c, s1, s2)


def _mla_q(p, q_norm, w_uq_p, qn_p, tabs):
    tm = _ROPE_TM
    hw = MLA_HEADS * LANES
    tab_spec = pl.BlockSpec((tm, LANES), lambda i: (_rope_block(i), 0))
    return pl.pallas_call(
        _mla_q_kernel,
        grid=(N_TOK // tm,),
        in_specs=[pl.BlockSpec((tm, MLA_Q_RANK), lambda i: (i, 0)),
                  pl.BlockSpec((1, MLA_Q_RANK), lambda i: (0, 0)),
                  pl.BlockSpec((MLA_Q_RANK, hw), lambda i: (0, 0)),
                  pl.BlockSpec((1, LANES), lambda i: (0, 0)),
                  tab_spec, tab_spec, tab_spec],
        out_specs=pl.BlockSpec((tm, hw), lambda i: (i, 0)),
        out_shape=jax.ShapeDtypeStruct((N_TOK, hw), F32),
        compiler_params=_params("parallel"),
        name="mla_q",
    )(p, q_norm, w_uq_p, qn_p, *tabs)


def _mla_kv_kernel(ckv_ref, kr_ref, kvn_ref, wk_ref, wv_ref, kn_ref, c_ref, s1_ref, s2_ref,
                   k_ref, v_ref, ckvn_ref, *, norm_ckv):
    ckv = ckv_ref[...]
    if norm_ckv:
        ckv = _rms(ckv, MLA_KV_RANK) * kvn_ref[...]
    ckvn_ref[...] = ckv
    ckv_bf = ckv.astype(BF16)
    kk = _dot_bf16(ckv_bf, wk_ref[...])
    v_ref[...] = _dot_bf16(ckv_bf, wv_ref[...])
    kr = kr_ref[...]
    c, s1, s2 = c_ref[...], s1_ref[...], s2_ref[...]
    for h in range(MLA_HEADS):
        kh = kk[:, h * LANES:(h + 1) * LANES] + kr
        kh = _rms(kh, MLA_QK) * kn_ref[...]
        k_ref[:, h * LANES:(h + 1) * LANES] = _rope(kh, c, s1, s2)


def _mla_kv(ckv_src, ckv_blk, kr_src, kr_blk, kv_norm, wk_p, wv_p, kn_p, tabs, n_rows, own_tokens):
    tm = _ROPE_TM
    hw = MLA_HEADS * LANES
    tab_spec = pl.BlockSpec((tm, LANES), (lambda i: (_rope_block(i), 0)) if own_tokens else (lambda i: (0, 0)))
    return pl.pallas_call(
        functools.partial(_mla_kv_kernel, norm_ckv=own_tokens),
        grid=(n_rows // tm,),
        in_specs=[pl.BlockSpec((tm, LANES), lambda i: (i, ckv_blk)),
                  pl.BlockSpec((tm, LANES), lambda i: (i, kr_blk)),
                  pl.BlockSpec((1, LANES), lambda i: (0, 0)),
                  pl.BlockSpec((MLA_KV_RANK, hw), lambda i: (0, 0)),
                  pl.BlockSpec((MLA_KV_RANK, hw), lambda i: (0, 0)),
                  pl.BlockSpec((1, LANES), lambda i: (0, 0)),
                  tab_spec, tab_spec, tab_spec],
        out_specs=[pl.BlockSpec((tm, hw), lambda i: (i, 0)),
                   pl.BlockSpec((tm, hw), lambda i: (i, 0)),
                   pl.BlockSpec((tm, LANES), lambda i: (i, 0))],
        out_shape=[jax.ShapeDtypeStruct((n_rows, hw), F32),
                   jax.ShapeDtypeStruct((n_rows, hw), F32),
                   jax.ShapeDtypeStruct((n_rows, LANES), F32)],
        compiler_params=_params("parallel"),
        name="mla_kv",
    )(ckv_src, kr_src, kv_norm, wk_p, wv_p, kn_p, *tabs)


def _softmax_rows(s):
    p = jnp.exp(s - jnp.max(s, axis=-1, keepdims=True))
    return p, jnp.sum(p, axis=-1, keepdims=True)


def _mla_attn_kernel(q_ref, k_ref, v_ref, o_ref):
    scale = MLA_QK ** -0.5
    outs = []
    for h in range(2):
        sl = slice(h * LANES, (h + 1) * LANES)
        p, l = _softmax_rows(_dot_nt_bf16(q_ref[:, sl], k_ref[:, sl]) * scale)
        outs.append(_dot_bf16(p, v_ref[:, sl]) / l)
    o_ref[...] = outs[0] + pltpu.roll(outs[1], MLA_V, axis=1)


def _mla_attn(q, k, v, batch, nq, nk, q_row0, tq):
    nqb = nq // tq
    qb0 = q_row0 // tq
    return pl.pallas_call(
        _mla_attn_kernel,
        grid=(batch, MLA_HEADS // 2, nqb),
        in_specs=[pl.BlockSpec((tq, 2 * LANES), lambda b, h, i: (qb0 + b * nqb + i, h)),
                  pl.BlockSpec((nk, 2 * LANES), lambda b, h, i: (b, h)),
                  pl.BlockSpec((nk, 2 * LANES), lambda b, h, i: (b, h))],
        out_specs=pl.BlockSpec((tq, LANES), lambda b, h, i: (b * nqb + i, h)),
        out_shape=jax.ShapeDtypeStruct((batch * nq, MLA_HEADS * MLA_V), F32),
        compiler_params=_params("parallel", "parallel", "arbitrary"),
        name="mla_attn",
    )(q, k, v)


def _ret_kernel(lg_ref, q_ref, k_ref, v_ref, rg_ref, s0_ref, gn_ref, o_ref, st_ref, *, n, tq):
    b, pair, qi = pl.program_id(0), pl.program_id(1), pl.program_id(2)
    q = q_ref[...]
    k = k_ref[...] * (RET_DK ** -0.5)
    lo = _lane_lo((1, LANES))
    row = (qi * tq + lax.broadcasted_iota(jnp.int32, (tq, 1), 0)).astype(F32)
    col = lax.broadcasted_iota(jnp.int32, (1, n), 1).astype(F32)
    diff = row - col
    for h in range(2):
        lgf = lg_ref[0, 2 * pair + h]
        lgb = lg_ref[1, 2 * pair + h]
        mask = lo if h == 0 else jnp.logical_not(lo)
        qh = jnp.where(mask, q, 0.0)
        vh = v_ref[:, h * LANES:(h + 1) * LANES]
        decay = (jnp.where(diff >= 0, jnp.exp(lgf * jnp.maximum(diff, 0.0)), 0.0)
                 + jnp.where(diff <= 0, jnp.exp(lgb * jnp.maximum(-diff, 0.0)), 0.0))
        o = _dot_bf16(_dot_nt_bf16(qh, k) * decay, vh)
        o = o + _dot_bf16(qh * jnp.exp(lgf * (row + 1.0)), s0_ref[0, 0])
        o = o + _dot_bf16(qh * jnp.exp(lgb * (n - row)), s0_ref[0, 1])
        y = _rms(o, RET_DV) * gn_ref[:, h * LANES:(h + 1) * LANES]
        o_ref[:, h * LANES:(h + 1) * LANES] = _silu(rg_ref[:, h * LANES:(h + 1) * LANES]) * y

    @pl.when(qi == 0)
    def _():
        pos = lax.broadcasted_iota(jnp.int32, (n, 1), 0).astype(F32)
        for d in range(2):
            acc = None
            for h in range(2):
                lg = lg_ref[d, 2 * pair + h]
                mask = lo if h == 0 else jnp.logical_not(lo)
                expo = (n - 1.0 - pos) if d == 0 else pos
                kd = jnp.where(mask, k * jnp.exp(lg * expo), 0.0)
                term = _dot_tn(kd, v_ref[:, h * LANES:(h + 1) * LANES])
                acc = term if acc is None else acc + term
            lg_rows = jnp.where(lax.broadcasted_iota(jnp.int32, (LANES, 1), 0) < 64,
                                lg_ref[d, 2 * pair], lg_ref[d, 2 * pair + 1])
            st_ref[0, d] = acc + s0_ref[0, d] * jnp.exp(lg_rows * n)


def _retention(log_g, p, q_blk, k_blk, v_blk, g_blk, s0, gn, batch, n, row0, tq):
    nqb = n // tq
    qb0 = row0 // tq
    kb0 = row0 // n
    pairs = RET_HEADS // 2
    return pl.pallas_call(
        functools.partial(_ret_kernel, n=n, tq=tq),
        grid=(batch, pairs, nqb),
        in_specs=[pl.BlockSpec(memory_space=pltpu.SMEM),
                  pl.BlockSpec((tq, LANES), lambda b, h, i: (qb0 + b * nqb + i, q_blk + h)),
                  pl.BlockSpec((n, LANES), lambda b, h, i: (kb0 + b, k_blk + h)),
                  pl.BlockSpec((n, 2 * LANES), lambda b, h, i: (kb0 + b, v_blk // 2 + h)),
                  pl.BlockSpec((tq, 2 * LANES), lambda b, h, i: (qb0 + b * nqb + i, g_blk // 2 + h)),
                  pl.BlockSpec((1, 2, LANES, LANES), lambda b, h, i: (b, 0, h, 0)),
                  pl.BlockSpec((1, 2 * LANES), lambda b, h, i: (0, h))],
        out_specs=[pl.BlockSpec((tq, 2 * LANES), lambda b, h, i: (b * nqb + i, h)),
                   pl.BlockSpec((1, 2, LANES, LANES), lambda b, h, i: (b, 0, h, 0))],
        out_shape=[jax.ShapeDtypeStruct((batch * n, RET_HEADS * RET_DV), F32),
                   jax.ShapeDtypeStruct((batch, 2, RET_HEADS * RET_DK, RET_DV), F32)],
        compiler_params=_params("parallel", "parallel", "arbitrary"),
        name="retention",
    )(log_g, p, p, p, p, s0, gn)


def _diff_qk_kernel(q_ref, k_ref, qn_ref, kn_ref, c_ref, s1_ref, s2_ref, qo_ref, ko_ref):
    c, s1, s2 = c_ref[...], s1_ref[...], s2_ref[...]
    for src, gain, dst in ((q_ref, qn_ref, qo_ref), (k_ref, kn_ref, ko_ref)):
        for h in range(DIFF_HEADS):
            sl = slice(h * LANES, (h + 1) * LANES)
            y = src[:, sl]
            y = y * lax.rsqrt(_seg64_sum(y * y) * (1.0 / DIFF_DH) + EPS) * gain[...]
            dst[:, sl] = _rope(y, c, s1, s2)


def _diff_qk(p, qn_p, kn_p, tabs):
    tm = _ROPE_TM
    tab_spec = pl.BlockSpec((tm, LANES), lambda i: (_rope_block(i), 0))
    return pl.pallas_call(
        _diff_qk_kernel,
        grid=(N_TOK // tm,),
        in_specs=[pl.BlockSpec((tm, DIFF_W), lambda i: (i, 0)),
                  pl.BlockSpec((tm, DIFF_W), lambda i: (i, 1)),
                  pl.BlockSpec((1, LANES), lambda i: (0, 0)),
                  pl.BlockSpec((1, LANES), lambda i: (0, 0)),
                  tab_spec, tab_spec, tab_spec],
        out_specs=[pl.BlockSpec((tm, DIFF_W), lambda i: (i, 0)),
                   pl.BlockSpec((tm, DIFF_W), lambda i: (i, 0))],
        out_shape=[jax.ShapeDtypeStruct((N_TOK, DIFF_W), F32)] * 2,
        compiler_params=_params("parallel"),
        name="diff_qk",
    )(p, p, qn_p, kn_p, *tabs)


def _diff_attn_kernel(lam_ref, q_ref, k_ref, v_ref, gn_ref, o_ref, *, lam_init):
    lv = lam_ref[...]
    lam = (jnp.exp(jnp.sum(lv[0:1] * lv[1:2], axis=-1, keepdims=True))
           - jnp.exp(jnp.sum(lv[2:3] * lv[3:4], axis=-1, keepdims=True)) + lam_init)
    scale = DIFF_DH ** -0.5
    q = q_ref[...]
    k = k_ref[...]
    lo = _lane_lo((1, LANES))
    kb = k.astype(BF16)
    p1, l1 = _softmax_rows(_dot_nt_bf16(jnp.where(lo, q, 0.0), kb) * scale)
    p2, l2 = _softmax_rows(_dot_nt_bf16(jnp.where(lo, 0.0, q), kb) * scale)
    w = p1 / l1 - lam * (p2 / l2)
    o = _dot_bf16(w, v_ref[...])
    o_ref[...] = _rms(o, 2 * DIFF_DH) * gn_ref[...] * (1.0 - lam_init)


def _diff_attn(lam, q, k, v, gn, batch, nq, nk, q_row0, tq, lam_init):
    nqb = nq // tq
    qb0 = q_row0 // tq
    return pl.pallas_call(
        functools.partial(_diff_attn_kernel, lam_init=lam_init),
        grid=(batch, DIFF_HEADS, nqb),
        in_specs=[pl.BlockSpec((4, DIFF_DH), lambda b, h, i: (0, 0)),
                  pl.BlockSpec((tq, LANES), lambda b, h, i: (qb0 + b * nqb + i, h)),
                  pl.BlockSpec((nk, LANES), lambda b, h, i: (b, h)),
                  pl.BlockSpec((nk, LANES), lambda b, h, i: (b, h)),
                  pl.BlockSpec((1, LANES), lambda b, h, i: (0, h))],
        out_specs=pl.BlockSpec((tq, LANES), lambda b, h, i: (b * nqb + i, h)),
        out_shape=jax.ShapeDtypeStruct((batch * nq, DIFF_W), F32),
        compiler_params=_params("parallel", "parallel", "arbitrary"),
        name="diff_attn",
    )(lam, q, k, v, gn)


def _seg64_sum_wide(x):
    return jnp.concatenate([_seg64_sum(x[:, j * LANES:(j + 1) * LANES]) for j in range(x.shape[1] // LANES)], axis=1)


_SCAN_SLOTS = 3
_SCAN_NJ = _SCAN_SLOTS * RWKV_HEADS


def _rwkv_pre_kernel(r_ref, k_ref, v_ref, lo_ref, wup_ref, aup_ref, gup_ref, w0_ref, a0_ref, kk_ref, ka_ref, rk_ref,
                     op_ref, g_ref, bonus_ref):
    W = RWKV_W
    col = lambda q: slice(q * W, (q + 1) * W)
    r = r_ref[...]
    k = k_ref[...]
    v = v_ref[...]
    lora = lo_ref[...]
    kk = k * kk_ref[...]
    kkn = kk * lax.rsqrt(_seg64_sum_wide(kk * kk) + EPS)
    g_ref[...] = _dot_bf16(_sigmoid(lora[:, 2 * LANES:3 * LANES]), gup_ref[...])
    pre = w0_ref[...] + _dot_bf16(jnp.tanh(lora[:, 0:LANES]), wup_ref[...])
    decay = jnp.exp(-jnp.exp(-_softplus(-pre) - 0.5))
    a = _sigmoid(a0_ref[...] + _dot_bf16(lora[:, LANES:2 * LANES], aup_ref[...]))
    lo = _lane_lo((1, LANES))
    bonus = None
    for d in range(2):
        a_d = a[:, col(d)]
        k_d = k * (1.0 + (a_d - 1.0) * ka_ref[...])
        t = _seg64_sum_wide(r * k_d * rk_ref[...])
        bonus = t if bonus is None else bonus + t
        for s, (x1, x2) in enumerate(((kkn, decay[:, col(d)]), (k_d, kkn * a_d), (r, v))):
            for h in range(RWKV_HEADS):
                blk = slice((h // 2) * LANES, (h // 2 + 1) * LANES)
                if h % 2 == 0:
                    out = jnp.where(lo, x1[:, blk], pltpu.roll(x2[:, blk], RWKV_HS, axis=1))
                else:
                    out = jnp.where(lo, pltpu.roll(x1[:, blk], RWKV_HS, axis=1), x2[:, blk])
                op_ref[:, d * _SCAN_NJ + s * RWKV_HEADS + h, :] = out
    bonus_ref[...] = bonus


def _rwkv_pre(p, r_blk, k_blk, v_blk, lo_blk, wup_bd, aup_bd, gup, w0, a0, k_k, k_a, r_k):
    tm = 256
    w = RWKV_W
    row = lambda n: pl.BlockSpec((1, n), lambda i: (0, 0))
    full = lambda a, b: pl.BlockSpec((a, b), lambda i: (0, 0))
    return pl.pallas_call(
        _rwkv_pre_kernel,
        grid=(N_TOK // tm,),
        in_specs=[pl.BlockSpec((tm, w), lambda i: (i, r_blk)),
                  pl.BlockSpec((tm, w), lambda i: (i, k_blk)),
                  pl.BlockSpec((tm, w), lambda i: (i, v_blk)),
                  pl.BlockSpec((tm, 3 * LANES), lambda i: (i, lo_blk)),
                  full(LANES, 2 * w), full(LANES, 2 * w), full(LANES, w),
                  row(2 * w), row(2 * w), row(w), row(w), row(w)],
        out_specs=[pl.BlockSpec((tm, 2 * _SCAN_NJ, LANES), lambda i: (i, 0, 0)),
                   pl.BlockSpec((tm, w), lambda i: (i, 0)), pl.BlockSpec((tm, w), lambda i: (i, 0))],
        out_shape=[jax.ShapeDtypeStruct((N_TOK, 2 * _SCAN_NJ, LANES), F32),
                   jax.ShapeDtypeStruct((N_TOK, w), F32), jax.ShapeDtypeStruct((N_TOK, w), F32)],
        compiler_params=_params("parallel"),
        name="rwkv_pre",
    )(p, p, p, p, wup_bd, aup_bd, gup, w0, a0, k_k, k_a, r_k)


_SCAN_CHUNK = 32
_SCAN_UNROLL = 16


def _rwkv_first_sa(s_ref, sa_ref, kk, n_k):
    nv = s_ref.shape[1]
    chunk = min(_SCAN_CHUNK, nv)
    for c0 in range(0, nv, chunk):
        def body(k, acc):
            return acc + s_ref[k, c0:c0 + chunk, :] * kk(k)
        sa_ref[c0:c0 + chunk, :] = lax.fori_loop(0, n_k, body, jnp.zeros((chunk, LANES), F32), unroll=_SCAN_UNROLL)


def _rwkv_step(s_ref, sa_ref, kk_next, w, kd, b, r, v_at, n_k):
    nv = s_ref.shape[1]
    chunk = min(_SCAN_CHUNK, nv)
    ys = []
    for c0 in range(0, nv, chunk):
        sa = sa_ref[c0:c0 + chunk, :]
        vc = v_at(c0, chunk)

        def body(k, acc):
            y_acc, sa_acc = acc
            s_new = s_ref[k, c0:c0 + chunk, :] * w(k) - sa * b(k) + vc * kd(k)
            s_ref[k, c0:c0 + chunk, :] = s_new
            return y_acc + s_new * r(k), sa_acc + s_new * kk_next(k)

        zero = jnp.zeros((chunk, LANES), F32)
        y_acc, sa_acc = lax.fori_loop(0, n_k, body, (zero, zero), unroll=_SCAN_UNROLL)
        sa_ref[c0:c0 + chunk, :] = sa_acc
        ys.append(y_acc)
    return ys[0] if len(ys) == 1 else jnp.concatenate(ys, axis=0)


_CTX_TB = 32


def _load_scan_operands(kt_ref, t, slabs):
    for s, x in enumerate(slabs):
        xt = x.T
        kt_ref[t, 2 * s] = xt[:RWKV_HS]
        kt_ref[t, 2 * s + 1] = xt[RWKV_HS:]


def _rwkv_scan_ctx_kernel(x_ref, y_ref, st_ref, s_ref, sa_ref, kt_ref, ys_ref):
    d = pl.program_id(0)
    tb = pl.program_id(1)
    nh = RWKV_HEADS
    step_t = lambda i: jnp.where(d == 0, i, _CTX_TB - 1 - i)

    @pl.when(tb == 0)
    def _():
        s_ref[...] = jnp.zeros_like(s_ref)

    def load_t(t, carry):
        _load_scan_operands(kt_ref, t, [
            jnp.concatenate([x_ref[b, t, s * nh:(s + 1) * nh, :] for b in range(BATCH)], axis=0)
            for s in range(_SCAN_SLOTS)])
        return carry

    lax.fori_loop(0, _CTX_TB, load_t, 0, unroll=2)

    t0 = step_t(0)
    _rwkv_first_sa(s_ref, sa_ref, lambda k: kt_ref[t0, 0, pl.ds(k, 1), :], RWKV_HS)

    def step(i, carry):
        t = step_t(i)
        tn = step_t(jnp.minimum(i + 1, _CTX_TB - 1))
        row = lambda q, tt: (lambda k: kt_ref[tt, q, pl.ds(k, 1), :])
        ys_ref[t] = _rwkv_step(s_ref, sa_ref, row(0, tn), row(1, t), row(2, t), row(3, t), row(4, t),
                               lambda c0, n: kt_ref[t, 5, pl.ds(c0, n), :], RWKV_HS)
        return carry

    lax.fori_loop(0, _CTX_TB, step, 0)

    def store_t(i, carry):
        z = jnp.concatenate([ys_ref[2 * i], ys_ref[2 * i + 1]], axis=0).T
        z_odd = pltpu.roll(z, RWKV_HS, axis=1)
        for b in range(BATCH):
            y_ref[b, 2 * i] = z[b * nh:(b + 1) * nh, :RWKV_HS]
            y_ref[b, 2 * i + 1] = z_odd[b * nh:(b + 1) * nh, :RWKV_HS]
        return carry

    lax.fori_loop(0, _CTX_TB // 2, store_t, 0, unroll=2)

    @pl.when(tb == pl.num_programs(1) - 1)
    def _():
        st_ref[0] = s_ref[...]


def _rwkv_scan_ctx(op4):
    nt = SEQ // _CTX_TB
    hs = RWKV_HS
    tblk = lambda d, tb: jnp.where(d == 0, tb, nt - 1 - tb)
    return pl.pallas_call(
        _rwkv_scan_ctx_kernel,
        grid=(2, nt),
        in_specs=[pl.BlockSpec((BATCH, _CTX_TB, _SCAN_NJ, LANES), lambda d, tb: (0, tblk(d, tb), d, 0))],
        out_specs=[pl.BlockSpec((BATCH, _CTX_TB, RWKV_HEADS, hs), lambda d, tb: (0, tblk(d, tb), d, 0)),
                   pl.BlockSpec((1, hs, hs, LANES), lambda d, tb: (d, 0, 0, 0))],
        out_shape=[jax.ShapeDtypeStruct((BATCH, SEQ, 2 * RWKV_HEADS, hs), F32),
                   jax.ShapeDtypeStruct((2, hs, hs, LANES), F32)],
        scratch_shapes=[pltpu.VMEM((hs, hs, LANES), F32), pltpu.VMEM((hs, LANES), F32),
                        pltpu.VMEM((_CTX_TB, 2 * _SCAN_SLOTS, hs, LANES), F32),
                        pltpu.VMEM((_CTX_TB, hs, LANES), F32)],
        compiler_params=_params("parallel", "arbitrary"),
        name="rwkv_scan_ctx",
    )(op4)


_LAT_TB = 32
_LAT_VSPLIT = 4
_LAT_STATES = 2 * DEC_BATCH * RWKV_HEADS
_LAT_VROWS = RWKV_HS // _LAT_VSPLIT


def _rwkv_scan_lat_kernel(xf0_ref, xf1_ref, xb0_ref, xb1_ref, s0_ref, yf_ref, yb_ref,
                          s_ref, sa_ref, kt_ref, v_ref, ys_ref):
    @pl.when(pl.program_id(0) == 0)
    def _():
        s_ref[...] = s0_ref[...]

    group = lax.broadcasted_iota(jnp.int32, (_LAT_VROWS, LANES), 1) // _LAT_STATES
    nh = RWKV_HEADS

    def load_t(t, carry):
        tr = _LAT_TB - 1 - t
        for s in range(_SCAN_SLOTS):
            heads = slice(s * nh, (s + 1) * nh)
            x = jnp.concatenate([xf0_ref[0, t, heads, :], xf1_ref[0, t, heads, :],
                                 xb0_ref[0, tr, heads, :], xb1_ref[0, tr, heads, :]], axis=0)
            xt = jnp.concatenate([x] * _LAT_VSPLIT, axis=0).T
            kt_ref[t, 2 * s] = xt[:RWKV_HS]
            if s < _SCAN_SLOTS - 1:
                kt_ref[t, 2 * s + 1] = xt[RWKV_HS:]
            else:
                v = jnp.zeros((_LAT_VROWS, LANES), F32)
                for g in range(_LAT_VSPLIT):
                    r0 = RWKV_HS + g * _LAT_VROWS
                    v = jnp.where(group == g, xt[r0:r0 + _LAT_VROWS, :], v)
                v_ref[t] = v
        return carry

    lax.fori_loop(0, _LAT_TB, load_t, 0, unroll=4)

    _rwkv_first_sa(s_ref, sa_ref, lambda k: kt_ref[0, 0, pl.ds(k, 1), :], RWKV_HS)

    def step(t, carry):
        tn = jnp.minimum(t + 1, _LAT_TB - 1)
        row = lambda q, tt: (lambda k: kt_ref[tt, q, pl.ds(k, 1), :])
        ys_ref[t] = _rwkv_step(s_ref, sa_ref, row(0, tn), row(1, t), row(2, t), row(3, t), row(4, t),
                               lambda c0, n: v_ref[t, pl.ds(c0, n), :], RWKV_HS)
        return carry

    lax.fori_loop(0, _LAT_TB, step, 0)

    def store_t(i, carry):
        rows = [jnp.where(group == g, ys_ref[2 * i + j], 0.0) for j in range(2) for g in range(_LAT_VSPLIT)]
        z = jnp.concatenate(rows, axis=0).T
        y = (z[0:_LAT_STATES] + z[_LAT_STATES:2 * _LAT_STATES]
             + z[2 * _LAT_STATES:3 * _LAT_STATES] + z[3 * _LAT_STATES:4 * _LAT_STATES])
        y_odd = pltpu.roll(y, RWKV_HS, axis=1)
        half = _LAT_STATES // 2
        for b in range(DEC_BATCH):
            rows_f = slice(b * nh, (b + 1) * nh)
            rows_b = slice(half + b * nh, half + (b + 1) * nh)
            yf_ref[b, 2 * i] = y[rows_f, :RWKV_HS]
            yf_ref[b, 2 * i + 1] = y_odd[rows_f, :RWKV_HS]
            yb_ref[b, _LAT_TB - 1 - 2 * i] = y[rows_b, :RWKV_HS]
            yb_ref[b, _LAT_TB - 2 - 2 * i] = y_odd[rows_b, :RWKV_HS]
        return carry

    lax.fori_loop(0, _LAT_TB // 2, store_t, 0, unroll=4)


def _rwkv_scan_lat(op4, s0):
    hs = RWKV_HS
    nv = _LAT_VROWS
    nt = DEC_SEQ // _LAT_TB
    per_seq = SEQ // _LAT_TB
    first = N_CTX // SEQ

    def x_spec(b, d):
        tblk = (lambda tb: tb) if d == 0 else (lambda tb: nt - 1 - tb)
        return pl.BlockSpec((1, _LAT_TB, _SCAN_NJ, LANES),
                            lambda tb: (first + b * (DEC_SEQ // SEQ) + tblk(tb) // per_seq, tblk(tb) % per_seq, d, 0))

    y_shape = jax.ShapeDtypeStruct((DEC_BATCH, DEC_SEQ, RWKV_HEADS, hs), F32)
    y_blk = (DEC_BATCH, _LAT_TB, RWKV_HEADS, hs)
    return pl.pallas_call(
        _rwkv_scan_lat_kernel,
        grid=(nt,),
        in_specs=[x_spec(0, 0), x_spec(1, 0), x_spec(0, 1), x_spec(1, 1),
                  pl.BlockSpec((hs, nv, LANES), lambda tb: (0, 0, 0))],
        out_specs=[pl.BlockSpec(y_blk, lambda tb: (0, tb, 0, 0)),
                   pl.BlockSpec(y_blk, lambda tb: (0, nt - 1 - tb, 0, 0))],
        out_shape=[y_shape, y_shape],
        scratch_shapes=[pltpu.VMEM((hs, nv, LANES), F32), pltpu.VMEM((nv, LANES), F32),
                        pltpu.VMEM((_LAT_TB, 2 * _SCAN_SLOTS - 1, hs, LANES), F32),
                        pltpu.VMEM((_LAT_TB, nv, LANES), F32), pltpu.VMEM((_LAT_TB, nv, LANES), F32)],
        compiler_params=_params("arbitrary"),
        name="rwkv_scan_lat",
    )(op4, op4, op4, op4, s0)


def _rwkv_post_kernel(yc_ref, ylf_ref, ylb_ref, bonus_ref, v_ref, g_ref, gn_ref, o_ref):
    def finish(head_sum):
        y = jnp.concatenate([head_sum(h) for h in range(RWKV_HEADS)], axis=1)
        y = y * lax.rsqrt(_seg64_sum_wide(y * y) * (1.0 / RWKV_HS) + EPS) * gn_ref[...]
        o_ref[...] = (y + bonus_ref[...] * v_ref[...]) * g_ref[...]

    @pl.when(pl.program_id(0) < N_CTX // SEQ)
    def _():
        finish(lambda h: yc_ref[0, :, h, :] + yc_ref[0, :, RWKV_HEADS + h, :])

    @pl.when(pl.program_id(0) >= N_CTX // SEQ)
    def _():
        finish(lambda h: ylf_ref[0, :, h, :] + ylb_ref[0, :, h, :])


def _rwkv_post(y_ctx, y_lat_f, y_lat_b, bonus, p, v_blk, g, gn):
    tm = SEQ
    w = RWKV_W
    hs = RWKV_HS
    n_ctx = N_CTX // SEQ
    lat = lambda y: y.reshape(N_LAT // SEQ, SEQ, RWKV_HEADS, hs)
    spec = pl.BlockSpec((tm, w), lambda i: (i, 0))
    lat_spec = pl.BlockSpec((1, SEQ, RWKV_HEADS, hs), lambda i: (jnp.maximum(i - n_ctx, 0), 0, 0, 0))
    return pl.pallas_call(
        _rwkv_post_kernel,
        grid=(N_TOK // tm,),
        in_specs=[pl.BlockSpec((1, SEQ, 2 * RWKV_HEADS, hs), lambda i: (jnp.minimum(i, n_ctx - 1), 0, 0, 0)),
                  lat_spec, lat_spec, spec, pl.BlockSpec((tm, w), lambda i: (i, v_blk)), spec,
                  pl.BlockSpec((1, w), lambda i: (0, 0))],
        out_specs=spec,
        out_shape=jax.ShapeDtypeStruct((N_TOK, w), F32),
        compiler_params=_params("parallel"),
        name="rwkv_post",
    )(y_ctx, lat(y_lat_f), lat(y_lat_b), bonus, p, g, gn)


def _value_split_layout(x):
    lead = x.shape[:-2]
    n = len(lead)
    x = x.reshape(lead + (_LAT_STATES, _LAT_VSPLIT, _LAT_VROWS))
    return jnp.transpose(x, tuple(range(n)) + (n + 2, n + 1, n)).reshape(lead + (_LAT_VROWS, LANES))


def _even_layer(x, mod, g_mix, w_in, q_norm, kv_norm, w_uq, w_ukv, qn, kn, ret_decay, ret_gn,
                cache_ckv, cache_krope, state_ret, tabs_m):
    cq, ckv, krope, rq, rk, rv, rg = jnp.split(w_in, np.cumsum(
        (MLA_Q_RANK, MLA_KV_RANK, MLA_ROPE, RET_HEADS * RET_DK, RET_HEADS * RET_DK, RET_HEADS * RET_DV))[:].tolist(),
        axis=1)
    z = lambda n: jnp.zeros((D_MODEL, n), F32)
    w_p = jnp.concatenate([cq, ckv, z(MLA_NOPE), krope, z(LANES - MLA_QK), rq, rk, rv, rg], axis=1)
    p = _inproj(x, g_mix, mod, w_p, jnp.zeros((1, w_p.shape[1]), F32), 512, False)
    CKV_BLK, KR_BLK, RQ_BLK, RK_BLK, RV_BLK, RG_BLK = 2, 3, 4, 6, 8, 12

    def head_pad(w, n_head, d_head, c0, c1):
        w = w.reshape(w.shape[0], n_head, d_head)[:, :, c0:c1]
        return jnp.pad(w, ((0, 0), (0, 0), (0, LANES - (c1 - c0)))).reshape(w.shape[0], n_head * LANES)

    w_uq_p = head_pad(w_uq, MLA_HEADS, MLA_QK, 0, MLA_QK)
    wk_p = head_pad(w_ukv, MLA_HEADS, MLA_NOPE + MLA_V, 0, MLA_NOPE)
    wv_p = head_pad(w_ukv, MLA_HEADS, MLA_NOPE + MLA_V, MLA_NOPE, MLA_NOPE + MLA_V)
    qn_p = jnp.pad(qn, (0, LANES - MLA_QK))[None]
    kn_p = jnp.pad(kn, (0, LANES - MLA_QK))[None]

    q = _mla_q(p, q_norm[None], w_uq_p, qn_p, tabs_m)
    k, v, ckvn = _mla_kv(p, CKV_BLK, p, KR_BLK, kv_norm[None], wk_p, wv_p, kn_p, tabs_m, N_TOK, True)

    n_c = DEC_BATCH * PAST_LEN
    kr_c = jnp.pad(cache_krope.reshape(n_c, MLA_ROPE), ((0, 0), (MLA_NOPE, LANES - MLA_QK)))
    k_c, v_c, _ = _mla_kv(cache_ckv.reshape(n_c, MLA_KV_RANK), 0, kr_c, 0, kv_norm[None], wk_p, wv_p, kn_p,
                          tabs_m, n_c, False)

    hw = MLA_HEADS * LANES

    def with_cache(own, cache):
        own = own[N_CTX:].reshape(DEC_BATCH, DEC_SEQ, hw)
        return jnp.concatenate([own, cache.reshape(DEC_BATCH, PAST_LEN, hw)], 1).reshape(-1, hw)

    o_ctx = _mla_attn(q, k, v, BATCH, SEQ, SEQ, 0, SEQ)
    o_lat = _mla_attn(q, with_cache(k, k_c), with_cache(v, v_c), DEC_BATCH, DEC_SEQ, DEC_SEQ + PAST_LEN, N_CTX, 256)

    log_g = -_softplus(-ret_decay)
    gn = ret_gn[None]
    s0_ctx = jnp.zeros((BATCH, 2, RET_HEADS * RET_DK, RET_DV), F32)
    r_ctx, st_ctx = _retention(log_g, p, RQ_BLK, RK_BLK, RV_BLK, RG_BLK, s0_ctx, gn, BATCH, SEQ, 0, SEQ)
    s0_lat = state_ret.reshape(DEC_BATCH, 2, RET_HEADS * RET_DK, RET_DV)
    r_lat, _ = _retention(log_g, p, RQ_BLK, RK_BLK, RV_BLK, RG_BLK, s0_lat, gn, DEC_BATCH, DEC_SEQ, N_CTX, 256)

    mix = [jnp.concatenate([o_ctx, o_lat], 0), jnp.concatenate([r_ctx, r_lat], 0)]
    new_ckv = ckvn[:N_CTX].reshape(BATCH, SEQ, MLA_KV_RANK)
    new_krope = p[:N_CTX, KR_BLK * LANES + MLA_NOPE:KR_BLK * LANES + MLA_QK]
    new_krope = new_krope.reshape(BATCH, SEQ, MLA_ROPE)
    new_ret = st_ctx.reshape(BATCH, 2, RET_HEADS, RET_DK, RET_DV)
    return mix, new_ckv, new_krope, new_ret


def _odd_layer(x, mod, g_mix, w_in, qn, kn, lam, diff_gn, mu, w0, w_up, a0, a_up, g_up, k_k, k_a, r_k, gn,
               cache_k, cache_v, state_rwkv, tabs_d, lam_init):
    w_p = w_in
    n_in = w_p.shape[1]
    mu_full = jnp.concatenate([jnp.zeros((3 * DIFF_W,), F32), mu])[None]
    p = _inproj(x, g_mix, mod, w_p, mu_full, 384, True)
    DV_BLK, R_BLK, K_BLK, V_BLK = 2, 3, 4, 5
    LO_BLK = (6 * RWKV_W) // (3 * LANES)

    qn_p = jnp.tile(qn, 2)[None]
    kn_p = jnp.tile(kn, 2)[None]
    q, k = _diff_qk(p, qn_p, kn_p, tabs_d)
    v = p[:, DV_BLK * DIFF_W:(DV_BLK + 1) * DIFF_W]

    n_c = DEC_BATCH * PAST_LEN
    k_c = cache_k.reshape(DEC_BATCH, PAST_LEN, DIFF_W)
    v_c = cache_v.reshape(DEC_BATCH, PAST_LEN, DIFF_W)

    def with_cache(own, cache):
        return jnp.concatenate([own[N_CTX:].reshape(DEC_BATCH, DEC_SEQ, DIFF_W), cache], 1).reshape(-1, DIFF_W)

    dgn = diff_gn[None]
    o_ctx = _diff_attn(lam, q, k, v, dgn, BATCH, SEQ, SEQ, 0, SEQ, lam_init)
    o_lat = _diff_attn(lam, q, with_cache(k, k_c), with_cache(v, v_c), dgn, DEC_BATCH, DEC_SEQ, DEC_SEQ + PAST_LEN,
                       N_CTX, 256, lam_init)

    zero = jnp.zeros((RWKV_W_LORA, RWKV_W), F32)
    wup_bd = jnp.concatenate([jnp.concatenate([w_up[0], zero], 1), jnp.concatenate([zero, w_up[1]], 1)], 0)
    aup_bd = jnp.concatenate([jnp.concatenate([a_up[0], zero], 1), jnp.concatenate([zero, a_up[1]], 1)], 0)
    op, g, bonus = _rwkv_pre(p, R_BLK, K_BLK, V_BLK, LO_BLK, wup_bd, aup_bd, g_up, w0.reshape(1, -1),
                             a0.reshape(1, -1), k_k[None], k_a[None], r_k.reshape(1, -1))
    op4 = op.reshape(N_TOK // SEQ, SEQ, 2 * _SCAN_NJ, LANES)
    y_ctx, st_ctx = _rwkv_scan_ctx(op4)
    s0_lat = jnp.transpose(state_rwkv, (4, 1, 0, 2, 3)).reshape(RWKV_HS, _LAT_STATES, RWKV_HS)
    y_lat_f, y_lat_b = _rwkv_scan_lat(op4, _value_split_layout(s0_lat))
    rw_o = _rwkv_post(y_ctx, y_lat_f, y_lat_b, bonus, p, V_BLK, g, gn[None])

    mix = [jnp.concatenate([o_ctx, o_lat], 0), rw_o]
    new_dk = k[:N_CTX].reshape(BATCH, SEQ, DIFF_HEADS, 2, DIFF_DH)
    new_dv = v[:N_CTX].reshape(BATCH, SEQ, DIFF_HEADS, 2 * DIFF_DH)
    new_rwkv = jnp.transpose(st_ctx.reshape(2, RWKV_HS, RWKV_HS, BATCH, RWKV_HEADS), (3, 0, 4, 2, 1))
    return mix, new_dk, new_dv, new_rwkv


def kernel(x_prompt, x_sample, cache_mla_ckv, cache_mla_krope, state_ret, cache_diff_k, cache_diff_v, state_rwkv,
           c, c_ctx, ada_w, ada_b, norm_mix_g, norm_ffn_g, w_out, ffn_up, ffn_conv_w, ffn_conv_b, ffn_down,
           a_w_in, mla_q_norm, mla_kv_norm, mla_w_uq, mla_w_ukv, mla_qn, mla_kn, ret_decay, ret_gn,
           b_w_in, diff_qn, diff_kn, diff_lam, diff_gn, rwkv_mu, rwkv_w0, rwkv_w_up, rwkv_a0, rwkv_a_up,
           rwkv_g_up, rwkv_k_k, rwkv_k_a, rwkv_r_k, rwkv_gn):
    x = jnp.concatenate([x_prompt.reshape(N_CTX, D_MODEL), x_sample.reshape(N_LAT, D_MODEL)], 0)
    cond8 = jnp.pad(jnp.concatenate([c_ctx[None], c], 0), ((0, 8 - N_GROUPS), (0, 0)))
    mod = _modulation(cond8, ada_w, ada_b)

    tabs_m = _rope_tables(MLA_ROPE, (MLA_NOPE,))
    tabs_d = _rope_tables(DIFF_DH, (0, DIFF_DH))

    outs = {}
    for l in range(DEPTH):
        j = l // 2
        g_mix = norm_mix_g[l][None]
        if l % 2 == 0:
            mix, outs["ckv"], outs["krope"], outs["ret"] = _even_layer(
                x, mod[l], g_mix, a_w_in[j], mla_q_norm[j], mla_kv_norm[j], mla_w_uq[j], mla_w_ukv[j], mla_qn[j],
                mla_kn[j], ret_decay[j], ret_gn[j], cache_mla_ckv[:, j], cache_mla_krope[:, j], state_ret[:, j],
                tabs_m)
        else:
            lam_init = 0.8 - 0.6 * math.exp(-0.3 * l)
            mix, outs["dk"], outs["dv"], outs["rwkv"] = _odd_layer(
                x, mod[l], g_mix, b_w_in[j], diff_qn[j], diff_kn[j], diff_lam[j], diff_gn[j], rwkv_mu[j],
                rwkv_w0[j], rwkv_w_up[j], rwkv_a0[j], rwkv_a_up[j], rwkv_g_up[j], rwkv_k_k[j], rwkv_k_a[j],
                rwkv_r_k[j], rwkv_gn[j], cache_diff_k[:, j], cache_diff_v[:, j], state_rwkv[:, j], tabs_d, lam_init)
        x = _resid_proj(mix, w_out[l], x, mod[l], 2)
        act = _ffn_up(x, norm_ffn_g[l][None], mod[l], ffn_up[l], ffn_conv_w[l], ffn_conv_b[l])
        x = _resid_proj([act], ffn_down[l], x, mod[l], 5)

    y_prompt = x[:N_CTX].reshape(BATCH, SEQ, D_MODEL)
    y_sample = x[N_CTX:].reshape(DEC_BATCH, DEC_SEQ, D_MODEL)
    return (y_prompt, y_sample, outs["ckv"][:, None], outs["krope"][:, None], outs["ret"][:, None],
            outs["dk"][:, None], outs["dv"][:, None], outs["rwkv"][:, None])
```

```python
import functools
import math

import numpy as np
import jax
import jax.numpy as jnp
from jax import lax
from jax.experimental import pallas as pl
from jax.experimental.pallas import tpu as pltpu

D_MODEL = 1024
BATCH = 16
SEQ = 256
DEPTH = 2
DEC_BATCH = 2
DEC_SEQ = 1024
PAST_LEN = 512
GRID_W = 64
EPS = 1e-6
ROPE_BASE = 10000.0

MLA_HEADS = 8
MLA_Q_RANK = 256
MLA_KV_RANK = 128
MLA_NOPE = 64
MLA_ROPE = 32
MLA_V = 64
MLA_QK = MLA_NOPE + MLA_ROPE
RET_HEADS = 4
RET_DK = 64
RET_DV = 128
DIFF_HEADS = 4
DIFF_DH = 64
DIFF_W = DIFF_HEADS * 2 * DIFF_DH
RWKV_HEADS = 8
RWKV_HS = 64
RWKV_W = RWKV_HEADS * RWKV_HS
RWKV_W_LORA = 64
RWKV_A_LORA = 64
RWKV_G_LORA = 128
D_FF = 2816

N_CTX = BATCH * SEQ
N_LAT = DEC_BATCH * DEC_SEQ
N_TOK = N_CTX + N_LAT
N_GROUPS = 1 + DEC_BATCH

LANES = 128
VMEM_LIMIT = 56 * 1024 * 1024

_PREC = lax.Precision.HIGHEST
F32 = jnp.float32


def _dot_tn(a, b):
    return lax.dot_general(a, b, (((0,), (0,)), ((), ())), precision=_PREC, preferred_element_type=F32)


BF16 = jnp.bfloat16


def _dot_bf16(a, b):
    return jnp.dot(a.astype(BF16), b.astype(BF16), preferred_element_type=F32)


def _dot_nt_bf16(a, b):
    return lax.dot_general(a.astype(BF16), b.astype(BF16), (((1,), (1,)), ((), ())), preferred_element_type=F32)


def _params(*sem):
    return pltpu.CompilerParams(dimension_semantics=sem, vmem_limit_bytes=VMEM_LIMIT)


def _sigmoid(x):
    return 1.0 / (1.0 + jnp.exp(-x))


def _silu(x):
    return x * _sigmoid(x)


def _softplus(x):
    return jnp.maximum(x, 0.0) + jnp.log(1.0 + jnp.exp(-jnp.abs(x)))


def _rms(x, n):
    return x * lax.rsqrt(jnp.sum(x * x, axis=-1, keepdims=True) * (1.0 / n) + EPS)


def _lane_lo(shape):
    return lax.broadcasted_iota(jnp.int32, shape, len(shape) - 1) < 64


def _seg64_sum(x):
    lo = _lane_lo(x.shape)
    s_lo = jnp.sum(jnp.where(lo, x, 0.0), axis=-1, keepdims=True)
    s_hi = jnp.sum(jnp.where(lo, 0.0, x), axis=-1, keepdims=True)
    return jnp.where(lo, s_lo, s_hi)


def _seq_neighbours(p, tile, tile_rows):
    seq_mask = jnp.where(tile * tile_rows < N_CTX, SEQ - 1, DEC_SEQ - 1)
    pos = lax.broadcasted_iota(jnp.int32, (tile_rows, 1), 0) & seq_mask
    prev = jnp.where(pos == 0, 0.0, pltpu.roll(p, 1, axis=0))
    nxt = jnp.where(pos == seq_mask, 0.0, pltpu.roll(p, tile_rows - 1, axis=0))
    return prev, nxt


def _group_of_tile(i, tile_rows):
    row = i * tile_rows
    return jnp.where(row < N_CTX, 0, 1 + (row - N_CTX) // DEC_SEQ)


def _modulation_kernel(c_ref, w_ref, b_ref, o_ref):
    o_ref[0] = _dot_bf16(_silu(c_ref[...]), w_ref[0]) + b_ref[0]


def _modulation(cond8, ada_w, ada_b):
    tn = 512
    n = 6 * D_MODEL
    out = pl.pallas_call(
        _modulation_kernel,
        grid=(DEPTH, n // tn),
        in_specs=[pl.BlockSpec((8, D_MODEL), lambda l, j: (0, 0)),
                  pl.BlockSpec((1, D_MODEL, tn), lambda l, j: (l, 0, j)),
                  pl.BlockSpec((1, 1, tn), lambda l, j: (l, 0, j))],
        out_specs=pl.BlockSpec((1, 8, tn), lambda l, j: (l, 0, j)),
        out_shape=jax.ShapeDtypeStruct((DEPTH, 8, n), F32),
        compiler_params=_params("parallel", "parallel"),
        name="modulation",
    )(cond8, ada_w, ada_b.reshape(DEPTH, 1, n))
    m = out[:, :N_GROUPS].reshape(DEPTH, N_GROUPS, 6, D_MODEL)
    return jnp.pad(m, ((0, 0), (0, 0), (0, 2), (0, 0)))


_TM_SEQ = 1024


def _norm_mod(x, g, mod, off):
    return _rms(x, D_MODEL) * g * (1.0 + mod[off + 1:off + 2, :]) + mod[off:off + 1, :]


def _inproj_kernel(x_ref, g_ref, mod_ref, w_ref, mu_ref, o_ref, h_ref, *, shift_from):
    i = pl.program_id(0)

    @pl.when(pl.program_id(1) == 0)
    def _():
        h_ref[...] = _norm_mod(x_ref[...], g_ref[...], mod_ref[0], 0).astype(BF16)

    p = _dot_bf16(h_ref[...], w_ref[...])
    if shift_from is None:
        o_ref[...] = p
    else:
        @pl.when(pl.program_id(1) < shift_from)
        def _():
            o_ref[...] = p

        @pl.when(pl.program_id(1) >= shift_from)
        def _():
            prev, nxt = _seq_neighbours(p, i, _TM_SEQ)
            o_ref[...] = p + (0.5 * (prev + nxt) - p) * mu_ref[...]


def _inproj(x, g, mod, w, mu, tn, shift_from):
    n = w.shape[1]
    tm = _TM_SEQ
    return pl.pallas_call(
        functools.partial(_inproj_kernel, shift_from=shift_from),
        grid=(N_TOK // tm, n // tn),
        in_specs=[pl.BlockSpec((tm, D_MODEL), lambda i, j: (i, 0)),
                  pl.BlockSpec((1, D_MODEL), lambda i, j: (0, 0)),
                  pl.BlockSpec((1, 8, D_MODEL), lambda i, j: (_group_of_tile(i, tm), 0, 0)),
                  pl.BlockSpec((D_MODEL, tn), lambda i, j: (0, j)),
                  pl.BlockSpec((1, tn), lambda i, j: (0, j))],
        out_specs=pl.BlockSpec((tm, tn), lambda i, j: (i, j)),
        out_shape=jax.ShapeDtypeStruct((N_TOK, n), F32),
        scratch_shapes=[pltpu.VMEM((tm, D_MODEL), BF16)],
        compiler_params=_params("parallel", "arbitrary"),
        name="inproj" if shift_from is None else "inproj_shift",
    )(x, g, mod, w, mu)


def _resid_kernel(*refs, gate_row, widths, split):
    n_in = sum(2 if sp else 1 for sp in split)
    a_refs = refs[:n_in]
    w_ref, x_ref, mod_ref, o_ref, a_bf_ref = refs[n_in:]
    i = pl.program_id(0)

    @pl.when(pl.program_id(1) == 0)
    def _():
        k0, r = 0, 0
        for width, sp in zip(widths, split):
            cols = slice(k0, k0 + width)
            if sp:
                ctx_ref, lat_ref = a_refs[r], a_refs[r + 1]

                @pl.when(i < N_CTX // _TM_SEQ)
                def _():
                    a_bf_ref[:, cols] = ctx_ref[...].astype(BF16)

                @pl.when(i >= N_CTX // _TM_SEQ)
                def _():
                    a_bf_ref[:, cols] = lat_ref[...].astype(BF16)
            else:
                a_bf_ref[:, cols] = a_refs[r][...].astype(BF16)
            k0 += width
            r += 2 if sp else 1

    o_ref[...] = x_ref[...] + mod_ref[0, gate_row:gate_row + 1, :] * _dot_bf16(a_bf_ref[...], w_ref[...])


def _resid_proj(acts, w, x, mod, gate_row):
    tm, tn = _TM_SEQ, 256
    n_ctx = N_CTX // tm
    split = [isinstance(a, (tuple, list)) for a in acts]
    widths = [a[0].shape[1] if sp else a.shape[1] for a, sp in zip(acts, split)]
    k = sum(widths)
    in_specs, operands = [], []
    for a, width, sp in zip(acts, widths, split):
        if sp:
            in_specs += [pl.BlockSpec((tm, width), lambda i, j: (jnp.minimum(i, n_ctx - 1), 0)),
                         pl.BlockSpec((tm, width), lambda i, j: (jnp.maximum(i - n_ctx, 0), 0))]
            operands += list(a)
        else:
            in_specs.append(pl.BlockSpec((tm, width), lambda i, j: (i, 0)))
            operands.append(a)
    return pl.pallas_call(
        functools.partial(_resid_kernel, gate_row=gate_row, widths=tuple(widths), split=tuple(split)),
        grid=(N_TOK // tm, D_MODEL // tn),
        in_specs=in_specs
        + [pl.BlockSpec((k, tn), lambda i, j: (0, j)),
           pl.BlockSpec((tm, tn), lambda i, j: (i, j)),
           pl.BlockSpec((1, 8, tn), lambda i, j: (_group_of_tile(i, tm), 0, j))],
        out_specs=pl.BlockSpec((tm, tn), lambda i, j: (i, j)),
        out_shape=jax.ShapeDtypeStruct((N_TOK, D_MODEL), F32),
        scratch_shapes=[pltpu.VMEM((tm, k), BF16)],
        compiler_params=_params("parallel", "arbitrary"),
        name="resid_proj",
    )(*operands, w, x, mod)


def _ffn_up_kernel(x_ref, g_ref, mod_ref, wa_ref, wb_ref, cwa_ref, cwb_ref, cba_ref, cbb_ref, o_ref, h_ref):
    i = pl.program_id(0)

    @pl.when(pl.program_id(1) == 0)
    def _():
        h_ref[...] = _norm_mod(x_ref[...], g_ref[...], mod_ref[0], 3).astype(BF16)

    h = h_ref[...]

    def conv(w_ref, cw_ref, cb_ref):
        u = _dot_bf16(h, w_ref[...])
        prev, nxt = _seq_neighbours(u, i, _TM_SEQ)
        return prev * cw_ref[0:1, :] + u * cw_ref[1:2, :] + nxt * cw_ref[2:3, :] + cb_ref[...]

    o_ref[...] = _silu(conv(wa_ref, cwa_ref, cba_ref)) * conv(wb_ref, cwb_ref, cbb_ref)


def _ffn_up(x, g, mod, up, cw, cb):
    tm, tn = _TM_SEQ, 256
    nb = D_FF // tn
    cb = cb.reshape(1, 2 * D_FF)
    return pl.pallas_call(
        _ffn_up_kernel,
        grid=(N_TOK // tm, nb),
        in_specs=[pl.BlockSpec((tm, D_MODEL), lambda i, j: (i, 0)),
                  pl.BlockSpec((1, D_MODEL), lambda i, j: (0, 0)),
                  pl.BlockSpec((1, 8, D_MODEL), lambda i, j: (_group_of_tile(i, tm), 0, 0)),
                  pl.BlockSpec((D_MODEL, tn), lambda i, j: (0, j)),
                  pl.BlockSpec((D_MODEL, tn), lambda i, j: (0, j + nb)),
                  pl.BlockSpec((3, tn), lambda i, j: (0, j)),
                  pl.BlockSpec((3, tn), lambda i, j: (0, j + nb)),
                  pl.BlockSpec((1, tn), lambda i, j: (0, j)),
                  pl.BlockSpec((1, tn), lambda i, j: (0, j + nb))],
        out_specs=pl.BlockSpec((tm, tn), lambda i, j: (i, j)),
        out_shape=jax.ShapeDtypeStruct((N_TOK, D_FF), F32),
        scratch_shapes=[pltpu.VMEM((tm, D_MODEL), BF16)],
        compiler_params=_params("parallel", "arbitrary"),
        name="ffn_up",
    )(x, g, mod, up, up, cw, cw, cb, cb)


_ROPE_TM = 512


def _rope(y, c, s1, s2):
    return y * c + pltpu.roll(y, 1, axis=1) * s1 + pltpu.roll(y, LANES - 1, axis=1) * s2


def _rope_tables(rot_dim, lane_offsets):
    t = np.arange(DEC_SEQ)
    row, col = t // GRID_W, t % GRID_W
    n_freq = rot_dim // 4
    inv = ROPE_BASE ** (-np.arange(n_freq, dtype=np.float64) / n_freq)
    ang = np.concatenate([row[:, None] * inv, col[:, None] * inv], -1)
    cos, sin = np.cos(ang), np.sin(ang)
    n = _ROPE_TM + DEC_SEQ
    c, s1, s2 = np.ones((n, LANES)), np.zeros((n, LANES)), np.zeros((n, LANES))
    for a in lane_offsets:
        even = a + 2 * np.arange(rot_dim // 2)
        c[_ROPE_TM:, even] = cos
        c[_ROPE_TM:, even + 1] = cos
        s1[_ROPE_TM:, even + 1] = sin
        s2[_ROPE_TM:, even] = -sin
    return tuple(jnp.asarray(x, F32) for x in (c, s1, s2))


def _rope_block(i):
    row = i * _ROPE_TM
    return jnp.where(row < N_CTX, 0, 1 + ((row - N_CTX) % DEC_SEQ) // _ROPE_TM)


def _mla_q_kernel(cq_ref, qnorm_ref, w_ref, qn_ref, c_ref, s1_ref, s2_ref, o_ref):
    xn = _rms(cq_ref[...], MLA_Q_RANK) * qnorm_ref[...]
    y = _dot_bf16(xn, w_ref[...])
    c, s1, s2 = c_ref[...], s1_ref[...], s2_ref[...]
    for h in range(MLA_HEADS):
        yh = y[:, h * LANES:(h + 1) * LANES]
        yh = _rms(yh, MLA_QK) * qn_ref[...]
        o_ref[:, h * LANES:(h + 1) * LANES] = _rope(yh, c, s1, s2)


def _mla_q(p, q_norm, w_uq_p, qn_p, tabs):
    tm = _ROPE_TM
    hw = MLA_HEADS * LANES
    tab_spec = pl.BlockSpec((tm, LANES), lambda i: (_rope_block(i), 0))
    return pl.pallas_call(
        _mla_q_kernel,
        grid=(N_TOK // tm,),
        in_specs=[pl.BlockSpec((tm, MLA_Q_RANK), lambda i: (i, 0)),
                  pl.BlockSpec((1, MLA_Q_RANK), lambda i: (0, 0)),
                  pl.BlockSpec((MLA_Q_RANK, hw), lambda i: (0, 0)),
                  pl.BlockSpec((1, LANES), lambda i: (0, 0)),
                  tab_spec, tab_spec, tab_spec],
        out_specs=pl.BlockSpec((tm, hw), lambda i: (i, 0)),
        out_shape=jax.ShapeDtypeStruct((N_TOK, hw), F32),
        compiler_params=_params("parallel"),
        name="mla_q",
    )(p, q_norm, w_uq_p, qn_p, *tabs)


def _mla_kv_kernel(ckv_ref, kr_ref, kvn_ref, wk_ref, wv_ref, kn_ref, c_ref, s1_ref, s2_ref,
                   k_ref, v_ref, ckvn_ref, *, norm_ckv):
    ckv = ckv_ref[...]
    if norm_ckv:
        ckv = _rms(ckv, MLA_KV_RANK) * kvn_ref[...]
    ckvn_ref[...] = ckv
    ckv_bf = ckv.astype(BF16)
    kk = _dot_bf16(ckv_bf, wk_ref[...])
    v_ref[...] = _dot_bf16(ckv_bf, wv_ref[...])
    kr = kr_ref[...]
    c, s1, s2 = c_ref[...], s1_ref[...], s2_ref[...]
    for h in range(MLA_HEADS):
        kh = kk[:, h * LANES:(h + 1) * LANES] + kr
        kh = _rms(kh, MLA_QK) * kn_ref[...]
        k_ref[:, h * LANES:(h + 1) * LANES] = _rope(kh, c, s1, s2)


def _mla_kv(ckv_src, ckv_blk, kr_src, kr_blk, kv_norm, wk_p, wv_p, kn_p, tabs, n_rows, own_tokens):
    tm = _ROPE_TM
    hw = MLA_HEADS * LANES
    tab_spec = pl.BlockSpec((tm, LANES), (lambda i: (_rope_block(i), 0)) if own_tokens else (lambda i: (0, 0)))
    return pl.pallas_call(
        functools.partial(_mla_kv_kernel, norm_ckv=own_tokens),
        grid=(n_rows // tm,),
        in_specs=[pl.BlockSpec((tm, LANES), lambda i: (i, ckv_blk)),
                  pl.BlockSpec((tm, LANES), lambda i: (i, kr_blk)),
                  pl.BlockSpec((1, LANES), lambda i: (0, 0)),
                  pl.BlockSpec((MLA_KV_RANK, hw), lambda i: (0, 0)),
                  pl.BlockSpec((MLA_KV_RANK, hw), lambda i: (0, 0)),
                  pl.BlockSpec((1, LANES), lambda i: (0, 0)),
                  tab_spec, tab_spec, tab_spec],
        out_specs=[pl.BlockSpec((tm, hw), lambda i: (i, 0)),
                   pl.BlockSpec((tm, hw), lambda i: (i, 0)),
                   pl.BlockSpec((tm, LANES), lambda i: (i, 0))],
        out_shape=[jax.ShapeDtypeStruct((n_rows, hw), F32),
                   jax.ShapeDtypeStruct((n_rows, hw), F32),
                   jax.ShapeDtypeStruct((n_rows, LANES), F32)],
        compiler_params=_params("parallel"),
        name="mla_kv",
    )(ckv_src, kr_src, kv_norm, wk_p, wv_p, kn_p, *tabs)


def _softmax_rows(s):
    p = jnp.exp(s - jnp.max(s, axis=-1, keepdims=True))
    return p, jnp.sum(p, axis=-1, keepdims=True)


def _mla_attn_kernel(q_ref, k_ref, v_ref, o_ref):
    scale = MLA_QK ** -0.5
    outs = []
    for h in range(2):
        sl = slice(h * LANES, (h + 1) * LANES)
        p, l = _softmax_rows(_dot_nt_bf16(q_ref[:, sl], k_ref[:, sl]) * scale)
        outs.append(_dot_bf16(p, v_ref[:, sl]) / l)
    o_ref[...] = outs[0] + pltpu.roll(outs[1], MLA_V, axis=1)


def _mla_attn(q, k, v, batch, nq, nk, q_row0, tq):
    nqb = nq // tq
    qb0 = q_row0 // tq
    return pl.pallas_call(
        _mla_attn_kernel,
        grid=(batch, MLA_HEADS // 2, nqb),
        in_specs=[pl.BlockSpec((tq, 2 * LANES), lambda b, h, i: (qb0 + b * nqb + i, h)),
                  pl.BlockSpec((nk, 2 * LANES), lambda b, h, i: (b, h)),
                  pl.BlockSpec((nk, 2 * LANES), lambda b, h, i: (b, h))],
        out_specs=pl.BlockSpec((tq, LANES), lambda b, h, i: (b * nqb + i, h)),
        out_shape=jax.ShapeDtypeStruct((batch * nq, MLA_HEADS * MLA_V), F32),
        compiler_params=_params("parallel", "parallel", "arbitrary"),
        name="mla_attn",
    )(q, k, v)


def _ret_kernel(lg_ref, q_ref, k_ref, v_ref, rg_ref, s0_ref, gn_ref, o_ref, st_ref, *, n, tq):
    b, pair, qi = pl.program_id(0), pl.program_id(1), pl.program_id(2)
    q = q_ref[...]
    k = k_ref[...] * (RET_DK ** -0.5)
    lo = _lane_lo((1, LANES))
    row = (qi * tq + lax.broadcasted_iota(jnp.int32, (tq, 1), 0)).astype(F32)
    col = lax.broadcasted_iota(jnp.int32, (1, n), 1).astype(F32)
    diff = row - col
    for h in range(2):
        lgf = lg_ref[0, 2 * pair + h]
        lgb = lg_ref[1, 2 * pair + h]
        mask = lo if h == 0 else jnp.logical_not(lo)
        qh = jnp.where(mask, q, 0.0)
        vh = v_ref[:, h * LANES:(h + 1) * LANES]
        decay = (jnp.where(diff >= 0, jnp.exp(lgf * jnp.maximum(diff, 0.0)), 0.0)
                 + jnp.where(diff <= 0, jnp.exp(lgb * jnp.maximum(-diff, 0.0)), 0.0))
        o = _dot_bf16(_dot_nt_bf16(qh, k) * decay, vh)
        o = o + _dot_bf16(qh * jnp.exp(lgf * (row + 1.0)), s0_ref[0, 0])
        o = o + _dot_bf16(qh * jnp.exp(lgb * (n - row)), s0_ref[0, 1])
        y = _rms(o, RET_DV) * gn_ref[:, h * LANES:(h + 1) * LANES]
        o_ref[:, h * LANES:(h + 1) * LANES] = _silu(rg_ref[:, h * LANES:(h + 1) * LANES]) * y

    @pl.when(qi == 0)
    def _():
        pos = lax.broadcasted_iota(jnp.int32, (n, 1), 0).astype(F32)
        for d in range(2):
            acc = None
            for h in range(2):
                lg = lg_ref[d, 2 * pair + h]
                mask = lo if h == 0 else jnp.logical_not(lo)
                expo = (n - 1.0 - pos) if d == 0 else pos
                kd = jnp.where(mask, k * jnp.exp(lg * expo), 0.0)
                term = _dot_tn(kd, v_ref[:, h * LANES:(h + 1) * LANES])
                acc = term if acc is None else acc + term
            lg_rows = jnp.where(lax.broadcasted_iota(jnp.int32, (LANES, 1), 0) < 64,
                                lg_ref[d, 2 * pair], lg_ref[d, 2 * pair + 1])
            st_ref[0, d] = acc + s0_ref[0, d] * jnp.exp(lg_rows * n)


def _retention(log_g, p, q_blk, k_blk, v_blk, g_blk, s0, gn, batch, n, row0, tq):
    nqb = n // tq
    qb0 = row0 // tq
    kb0 = row0 // n
    pairs = RET_HEADS // 2
    return pl.pallas_call(
        functools.partial(_ret_kernel, n=n, tq=tq),
        grid=(batch, pairs, nqb),
        in_specs=[pl.BlockSpec(memory_space=pltpu.SMEM),
                  pl.BlockSpec((tq, LANES), lambda b, h, i: (qb0 + b * nqb + i, q_blk + h)),
                  pl.BlockSpec((n, LANES), lambda b, h, i: (kb0 + b, k_blk + h)),
                  pl.BlockSpec((n, 2 * LANES), lambda b, h, i: (kb0 + b, v_blk // 2 + h)),
                  pl.BlockSpec((tq, 2 * LANES), lambda b, h, i: (qb0 + b * nqb + i, g_blk // 2 + h)),
                  pl.BlockSpec((1, 2, LANES, LANES), lambda b, h, i: (b, 0, h, 0)),
                  pl.BlockSpec((1, 2 * LANES), lambda b, h, i: (0, h))],
        out_specs=[pl.BlockSpec((tq, 2 * LANES), lambda b, h, i: (b * nqb + i, h)),
                   pl.BlockSpec((1, 2, LANES, LANES), lambda b, h, i: (b, 0, h, 0))],
        out_shape=[jax.ShapeDtypeStruct((batch * n, RET_HEADS * RET_DV), F32),
                   jax.ShapeDtypeStruct((batch, 2, RET_HEADS * RET_DK, RET_DV), F32)],
        compiler_params=_params("parallel", "parallel", "arbitrary"),
        name="retention",
    )(log_g, p, p, p, p, s0, gn)


def _diff_qk_kernel(q_ref, k_ref, qn_ref, kn_ref, c_ref, s1_ref, s2_ref, qo_ref, ko_ref):
    c, s1, s2 = c_ref[...], s1_ref[...], s2_ref[...]
    for src, gain, dst in ((q_ref, qn_ref, qo_ref), (k_ref, kn_ref, ko_ref)):
        for h in range(DIFF_HEADS):
            sl = slice(h * LANES, (h + 1) * LANES)
            y = src[:, sl]
            y = y * lax.rsqrt(_seg64_sum(y * y) * (1.0 / DIFF_DH) + EPS) * gain[...]
            dst[:, sl] = _rope(y, c, s1, s2)


def _diff_qk(p, qn_p, kn_p, tabs):
    tm = _ROPE_TM
    tab_spec = pl.BlockSpec((tm, LANES), lambda i: (_rope_block(i), 0))
    return pl.pallas_call(
        _diff_qk_kernel,
        grid=(N_TOK // tm,),
        in_specs=[pl.BlockSpec((tm, DIFF_W), lambda i: (i, 0)),
                  pl.BlockSpec((tm, DIFF_W), lambda i: (i, 1)),
                  pl.BlockSpec((1, LANES), lambda i: (0, 0)),
                  pl.BlockSpec((1, LANES), lambda i: (0, 0)),
                  tab_spec, tab_spec, tab_spec],
        out_specs=[pl.BlockSpec((tm, DIFF_W), lambda i: (i, 0)),
                   pl.BlockSpec((tm, DIFF_W), lambda i: (i, 0))],
        out_shape=[jax.ShapeDtypeStruct((N_TOK, DIFF_W), F32)] * 2,
        compiler_params=_params("parallel"),
        name="diff_qk",
    )(p, p, qn_p, kn_p, *tabs)


def _diff_attn_kernel(lam_ref, q_ref, k_ref, v_ref, gn_ref, o_ref, *, lam_init):
    lv = lam_ref[...]
    lam = (jnp.exp(jnp.sum(lv[0:1] * lv[1:2], axis=-1, keepdims=True))
           - jnp.exp(jnp.sum(lv[2:3] * lv[3:4], axis=-1, keepdims=True)) + lam_init)
    scale = DIFF_DH ** -0.5
    q = q_ref[...]
    k = k_ref[...]
    lo = _lane_lo((1, LANES))
    kb = k.astype(BF16)
    p1, l1 = _softmax_rows(_dot_nt_bf16(jnp.where(lo, q, 0.0), kb) * scale)
    p2, l2 = _softmax_rows(_dot_nt_bf16(jnp.where(lo, 0.0, q), kb) * scale)
    w = p1 / l1 - lam * (p2 / l2)
    o = _dot_bf16(w, v_ref[...])
    o_ref[...] = _rms(o, 2 * DIFF_DH) * gn_ref[...] * (1.0 - lam_init)


def _diff_attn(lam, q, k, v, v_blk0, gn, batch, nq, nk, q_row0, tq, lam_init):
    nqb = nq // tq
    qb0 = q_row0 // tq
    return pl.pallas_call(
        functools.partial(_diff_attn_kernel, lam_init=lam_init),
        grid=(batch, DIFF_HEADS, nqb),
        in_specs=[pl.BlockSpec((4, DIFF_DH), lambda b, h, i: (0, 0)),
                  pl.BlockSpec((tq, LANES), lambda b, h, i: (qb0 + b * nqb + i, h)),
                  pl.BlockSpec((nk, LANES), lambda b, h, i: (b, h)),
                  pl.BlockSpec((nk, LANES), lambda b, h, i: (b, v_blk0 + h)),
                  pl.BlockSpec((1, LANES), lambda b, h, i: (0, h))],
        out_specs=pl.BlockSpec((tq, LANES), lambda b, h, i: (b * nqb + i, h)),
        out_shape=jax.ShapeDtypeStruct((batch * nq, DIFF_W), F32),
        compiler_params=_params("parallel", "parallel", "arbitrary"),
        name="diff_attn",
    )(lam, q, k, v, gn)


def _seg64_sum_wide(x):
    return jnp.concatenate([_seg64_sum(x[:, j * LANES:(j + 1) * LANES]) for j in range(x.shape[1] // LANES)], axis=1)


_SCAN_SLOTS = 3
_SCAN_NJ = _SCAN_SLOTS * RWKV_HEADS


def _rwkv_pre_kernel(r_ref, k_ref, v_ref, lo_ref, wup_ref, aup_ref, gup_ref, w0_ref, a0_ref, kk_ref, ka_ref, rk_ref,
                     op_ref, g_ref, bonus_ref):
    W = RWKV_W
    col = lambda q: slice(q * W, (q + 1) * W)
    r = r_ref[...]
    k = k_ref[...]
    v = v_ref[...]
    lora = lo_ref[...]
    kk = k * kk_ref[...]
    kkn = kk * lax.rsqrt(_seg64_sum_wide(kk * kk) + EPS)
    g_ref[...] = _dot_bf16(_sigmoid(lora[:, 2 * LANES:3 * LANES]), gup_ref[...])
    pre = w0_ref[...] + _dot_bf16(jnp.tanh(lora[:, 0:LANES]), wup_ref[...])
    decay = jnp.exp(-jnp.exp(-_softplus(-pre) - 0.5))
    a = _sigmoid(a0_ref[...] + _dot_bf16(lora[:, LANES:2 * LANES], aup_ref[...]))
    lo = _lane_lo((1, LANES))
    bonus = None
    for d in range(2):
        a_d = a[:, col(d)]
        k_d = k * (1.0 + (a_d - 1.0) * ka_ref[...])
        t = _seg64_sum_wide(r * k_d * rk_ref[...])
        bonus = t if bonus is None else bonus + t
        for s, (x1, x2) in enumerate(((kkn, decay[:, col(d)]), (k_d, kkn * a_d), (r, v))):
            for h in range(RWKV_HEADS):
                blk = slice((h // 2) * LANES, (h // 2 + 1) * LANES)
                if h % 2 == 0:
                    out = jnp.where(lo, x1[:, blk], pltpu.roll(x2[:, blk], RWKV_HS, axis=1))
                else:
                    out = jnp.where(lo, pltpu.roll(x1[:, blk], RWKV_HS, axis=1), x2[:, blk])
                op_ref[:, d * _SCAN_NJ + s * RWKV_HEADS + h, :] = out
    bonus_ref[...] = bonus


def _rwkv_pre(p, r_blk, k_blk, v_blk, lo_blk, wup_bd, aup_bd, gup, w0, a0, k_k, k_a, r_k):
    tm = 256
    w = RWKV_W
    row = lambda n: pl.BlockSpec((1, n), lambda i: (0, 0))
    full = lambda a, b: pl.BlockSpec((a, b), lambda i: (0, 0))
    return pl.pallas_call(
        _rwkv_pre_kernel,
        grid=(N_TOK // tm,),
        in_specs=[pl.BlockSpec((tm, w), lambda i: (i, r_blk)),
                  pl.BlockSpec((tm, w), lambda i: (i, k_blk)),
                  pl.BlockSpec((tm, w), lambda i: (i, v_blk)),
                  pl.BlockSpec((tm, 3 * LANES), lambda i: (i, lo_blk)),
                  full(LANES, 2 * w), full(LANES, 2 * w), full(LANES, w),
                  row(2 * w), row(2 * w), row(w), row(w), row(w)],
        out_specs=[pl.BlockSpec((tm, 2 * _SCAN_NJ, LANES), lambda i: (i, 0, 0)),
                   pl.BlockSpec((tm, w), lambda i: (i, 0)), pl.BlockSpec((tm, w), lambda i: (i, 0))],
        out_shape=[jax.ShapeDtypeStruct((N_TOK, 2 * _SCAN_NJ, LANES), F32),
                   jax.ShapeDtypeStruct((N_TOK, w), F32), jax.ShapeDtypeStruct((N_TOK, w), F32)],
        compiler_params=_params("parallel"),
        name="rwkv_pre",
    )(p, p, p, p, wup_bd, aup_bd, gup, w0, a0, k_k, k_a, r_k)


_SCAN_CHUNK = 32
_SCAN_UNROLL = 16


def _rwkv_first_sa(s_ref, sa_ref, kk, n_k):
    nv = s_ref.shape[1]
    chunk = min(_SCAN_CHUNK, nv)
    for c0 in range(0, nv, chunk):
        def body(k, acc):
            return acc + s_ref[k, c0:c0 + chunk, :] * kk(k)
        sa_ref[c0:c0 + chunk, :] = lax.fori_loop(0, n_k, body, jnp.zeros((chunk, LANES), F32), unroll=_SCAN_UNROLL)


def _rwkv_step(s_ref, sa_ref, kk_next, w, kd, b, r, v_at, n_k):
    nv = s_ref.shape[1]
    chunk = min(_SCAN_CHUNK, nv)
    ys = []
    for c0 in range(0, nv, chunk):
        sa = sa_ref[c0:c0 + chunk, :]
        vc = v_at(c0, chunk)

        def body(k, acc):
            y_acc, sa_acc = acc
            s_new = s_ref[k, c0:c0 + chunk, :] * w(k) - sa * b(k) + vc * kd(k)
            s_ref[k, c0:c0 + chunk, :] = s_new
            return y_acc + s_new * r(k), sa_acc + s_new * kk_next(k)

        zero = jnp.zeros((chunk, LANES), F32)
        y_acc, sa_acc = lax.fori_loop(0, n_k, body, (zero, zero), unroll=_SCAN_UNROLL)
        sa_ref[c0:c0 + chunk, :] = sa_acc
        ys.append(y_acc)
    return ys[0] if len(ys) == 1 else jnp.concatenate(ys, axis=0)


_CTX_TB = 32


def _load_scan_operands(kt_ref, t, slabs):
    for s, x in enumerate(slabs):
        xt = x.T
        kt_ref[t, 2 * s] = xt[:RWKV_HS]
        kt_ref[t, 2 * s + 1] = xt[RWKV_HS:]


def _rwkv_scan_ctx_kernel(x_ref, y_ref, st_ref, s_ref, sa_ref, kt_ref, ys_ref):
    d = pl.program_id(0)
    tb = pl.program_id(1)
    nh = RWKV_HEADS
    step_t = lambda i: jnp.where(d == 0, i, _CTX_TB - 1 - i)

    @pl.when(tb == 0)
    def _():
        s_ref[...] = jnp.zeros_like(s_ref)

    def load_t(t, carry):
        _load_scan_operands(kt_ref, t, [
            jnp.concatenate([x_ref[b, t, s * nh:(s + 1) * nh, :] for b in range(BATCH)], axis=0)
            for s in range(_SCAN_SLOTS)])
        return carry

    lax.fori_loop(0, _CTX_TB, load_t, 0, unroll=2)

    t0 = step_t(0)
    _rwkv_first_sa(s_ref, sa_ref, lambda k: kt_ref[t0, 0, pl.ds(k, 1), :], RWKV_HS)

    def step(i, carry):
        t = step_t(i)
        tn = step_t(jnp.minimum(i + 1, _CTX_TB - 1))
        row = lambda q, tt: (lambda k: kt_ref[tt, q, pl.ds(k, 1), :])
        ys_ref[t] = _rwkv_step(s_ref, sa_ref, row(0, tn), row(1, t), row(2, t), row(3, t), row(4, t),
                               lambda c0, n: kt_ref[t, 5, pl.ds(c0, n), :], RWKV_HS)
        return carry

    lax.fori_loop(0, _CTX_TB, step, 0)

    def store_t(i, carry):
        z = jnp.concatenate([ys_ref[2 * i], ys_ref[2 * i + 1]], axis=0).T
        z_odd = pltpu.roll(z, RWKV_HS, axis=1)
        for b in range(BATCH):
            y_ref[b, 2 * i] = z[b * nh:(b + 1) * nh, :RWKV_HS]
            y_ref[b, 2 * i + 1] = z_odd[b * nh:(b + 1) * nh, :RWKV_HS]
        return carry

    lax.fori_loop(0, _CTX_TB // 2, store_t, 0, unroll=2)

    @pl.when(tb == pl.num_programs(1) - 1)
    def _():
        st_ref[0] = s_ref[...]


def _rwkv_scan_ctx(op4):
    nt = SEQ // _CTX_TB
    hs = RWKV_HS
    tblk = lambda d, tb: jnp.where(d == 0, tb, nt - 1 - tb)
    return pl.pallas_call(
        _rwkv_scan_ctx_kernel,
        grid=(2, nt),
        in_specs=[pl.BlockSpec((BATCH, _CTX_TB, _SCAN_NJ, LANES), lambda d, tb: (0, tblk(d, tb), d, 0))],
        out_specs=[pl.BlockSpec((BATCH, _CTX_TB, RWKV_HEADS, hs), lambda d, tb: (0, tblk(d, tb), d, 0)),
                   pl.BlockSpec((1, hs, hs, LANES), lambda d, tb: (d, 0, 0, 0))],
        out_shape=[jax.ShapeDtypeStruct((BATCH, SEQ, 2 * RWKV_HEADS, hs), F32),
                   jax.ShapeDtypeStruct((2, hs, hs, LANES), F32)],
        scratch_shapes=[pltpu.VMEM((hs, hs, LANES), F32), pltpu.VMEM((hs, LANES), F32),
                        pltpu.VMEM((_CTX_TB, 2 * _SCAN_SLOTS, hs, LANES), F32),
                        pltpu.VMEM((_CTX_TB, hs, LANES), F32)],
        compiler_params=_params("parallel", "arbitrary"),
        name="rwkv_scan_ctx",
    )(op4)


_LAT_TB = 32
_LAT_VSPLIT = 4
_LAT_STATES = 2 * DEC_BATCH * RWKV_HEADS
_LAT_VROWS = RWKV_HS // _LAT_VSPLIT


def _rwkv_scan_lat_kernel(xf0_ref, xf1_ref, xb0_ref, xb1_ref, s0_ref, yf_ref, yb_ref,
                          s_ref, sa_ref, kt_ref, v_ref, ys_ref):
    @pl.when(pl.program_id(0) == 0)
    def _():
        s_ref[...] = s0_ref[...]

    group = lax.broadcasted_iota(jnp.int32, (_LAT_VROWS, LANES), 1) // _LAT_STATES
    nh = RWKV_HEADS

    def load_t(t, carry):
        tr = _LAT_TB - 1 - t
        for s in range(_SCAN_SLOTS):
            heads = slice(s * nh, (s + 1) * nh)
            x = jnp.concatenate([xf0_ref[0, t, heads, :], xf1_ref[0, t, heads, :],
                                 xb0_ref[0, tr, heads, :], xb1_ref[0, tr, heads, :]], axis=0)
            xt = jnp.concatenate([x] * _LAT_VSPLIT, axis=0).T
            kt_ref[t, 2 * s] = xt[:RWKV_HS]
            if s < _SCAN_SLOTS - 1:
                kt_ref[t, 2 * s + 1] = xt[RWKV_HS:]
            else:
                v = jnp.zeros((_LAT_VROWS, LANES), F32)
                for g in range(_LAT_VSPLIT):
                    r0 = RWKV_HS + g * _LAT_VROWS
                    v = jnp.where(group == g, xt[r0:r0 + _LAT_VROWS, :], v)
                v_ref[t] = v
        return carry

    lax.fori_loop(0, _LAT_TB, load_t, 0, unroll=4)

    _rwkv_first_sa(s_ref, sa_ref, lambda k: kt_ref[0, 0, pl.ds(k, 1), :], RWKV_HS)

    def step(t, carry):
        tn = jnp.minimum(t + 1, _LAT_TB - 1)
        row = lambda q, tt: (lambda k: kt_ref[tt, q, pl.ds(k, 1), :])
        ys_ref[t] = _rwkv_step(s_ref, sa_ref, row(0, tn), row(1, t), row(2, t), row(3, t), row(4, t),
                               lambda c0, n: v_ref[t, pl.ds(c0, n), :], RWKV_HS)
        return carry

    lax.fori_loop(0, _LAT_TB, step, 0)

    def store_t(i, carry):
        rows = [jnp.where(group == g, ys_ref[2 * i + j], 0.0) for j in range(2) for g in range(_LAT_VSPLIT)]
        z = jnp.concatenate(rows, axis=0).T
        y = (z[0:_LAT_STATES] + z[_LAT_STATES:2 * _LAT_STATES]
             + z[2 * _LAT_STATES:3 * _LAT_STATES] + z[3 * _LAT_STATES:4 * _LAT_STATES])
        y_odd = pltpu.roll(y, RWKV_HS, axis=1)
        half = _LAT_STATES // 2
        for b in range(DEC_BATCH):
            rows_f = slice(b * nh, (b + 1) * nh)
            rows_b = slice(half + b * nh, half + (b + 1) * nh)
            yf_ref[b, 2 * i] = y[rows_f, :RWKV_HS]
            yf_ref[b, 2 * i + 1] = y_odd[rows_f, :RWKV_HS]
            yb_ref[b, _LAT_TB - 1 - 2 * i] = y[rows_b, :RWKV_HS]
            yb_ref[b, _LAT_TB - 2 - 2 * i] = y_odd[rows_b, :RWKV_HS]
        return carry

    lax.fori_loop(0, _LAT_TB // 2, store_t, 0, unroll=4)


def _rwkv_scan_lat(op4, s0):
    hs = RWKV_HS
    nv = _LAT_VROWS
    nt = DEC_SEQ // _LAT_TB
    per_seq = SEQ // _LAT_TB
    first = N_CTX // SEQ

    def x_spec(b, d):
        tblk = (lambda tb: tb) if d == 0 else (lambda tb: nt - 1 - tb)
        return pl.BlockSpec((1, _LAT_TB, _SCAN_NJ, LANES),
                            lambda tb: (first + b * (DEC_SEQ // SEQ) + tblk(tb) // per_seq, tblk(tb) % per_seq, d, 0))

    y_shape = jax.ShapeDtypeStruct((DEC_BATCH, DEC_SEQ, RWKV_HEADS, hs), F32)
    y_blk = (DEC_BATCH, _LAT_TB, RWKV_HEADS, hs)
    return pl.pallas_call(
        _rwkv_scan_lat_kernel,
        grid=(nt,),
        in_specs=[x_spec(0, 0), x_spec(1, 0), x_spec(0, 1), x_spec(1, 1),
                  pl.BlockSpec((hs, nv, LANES), lambda tb: (0, 0, 0))],
        out_specs=[pl.BlockSpec(y_blk, lambda tb: (0, tb, 0, 0)),
                   pl.BlockSpec(y_blk, lambda tb: (0, nt - 1 - tb, 0, 0))],
        out_shape=[y_shape, y_shape],
        scratch_shapes=[pltpu.VMEM((hs, nv, LANES), F32), pltpu.VMEM((nv, LANES), F32),
                        pltpu.VMEM((_LAT_TB, 2 * _SCAN_SLOTS - 1, hs, LANES), F32),
                        pltpu.VMEM((_LAT_TB, nv, LANES), F32), pltpu.VMEM((_LAT_TB, nv, LANES), F32)],
        compiler_params=_params("arbitrary"),
        name="rwkv_scan_lat",
    )(op4, op4, op4, op4, s0)


def _rwkv_post_kernel(yc_ref, ylf_ref, ylb_ref, bonus_ref, v_ref, g_ref, gn_ref, o_ref):
    def finish(head_sum):
        y = jnp.concatenate([head_sum(h) for h in range(RWKV_HEADS)], axis=1)
        y = y * lax.rsqrt(_seg64_sum_wide(y * y) * (1.0 / RWKV_HS) + EPS) * gn_ref[...]
        o_ref[...] = (y + bonus_ref[...] * v_ref[...]) * g_ref[...]

    @pl.when(pl.program_id(0) < N_CTX // SEQ)
    def _():
        finish(lambda h: yc_ref[0, :, h, :] + yc_ref[0, :, RWKV_HEADS + h, :])

    @pl.when(pl.program_id(0) >= N_CTX // SEQ)
    def _():
        finish(lambda h: ylf_ref[0, :, h, :] + ylb_ref[0, :, h, :])


def _rwkv_post(y_ctx, y_lat_f, y_lat_b, bonus, p, v_blk, g, gn):
    tm = SEQ
    w = RWKV_W
    hs = RWKV_HS
    n_ctx = N_CTX // SEQ
    lat = lambda y: y.reshape(N_LAT // SEQ, SEQ, RWKV_HEADS, hs)
    spec = pl.BlockSpec((tm, w), lambda i: (i, 0))
    lat_spec = pl.BlockSpec((1, SEQ, RWKV_HEADS, hs), lambda i: (jnp.maximum(i - n_ctx, 0), 0, 0, 0))
    return pl.pallas_call(
        _rwkv_post_kernel,
        grid=(N_TOK // tm,),
        in_specs=[pl.BlockSpec((1, SEQ, 2 * RWKV_HEADS, hs), lambda i: (jnp.minimum(i, n_ctx - 1), 0, 0, 0)),
                  lat_spec, lat_spec, spec, pl.BlockSpec((tm, w), lambda i: (i, v_blk)), spec,
                  pl.BlockSpec((1, w), lambda i: (0, 0))],
        out_specs=spec,
        out_shape=jax.ShapeDtypeStruct((N_TOK, w), F32),
        compiler_params=_params("parallel"),
        name="rwkv_post",
    )(y_ctx, lat(y_lat_f), lat(y_lat_b), bonus, p, g, gn)


def _value_split_layout(x):
    lead = x.shape[:-2]
    n = len(lead)
    x = x.reshape(lead + (_LAT_STATES, _LAT_VSPLIT, _LAT_VROWS))
    return jnp.transpose(x, tuple(range(n)) + (n + 2, n + 1, n)).reshape(lead + (_LAT_VROWS, LANES))


def _even_layer(x, mod, g_mix, w_in, q_norm, kv_norm, w_uq, w_ukv, qn, kn, ret_decay, ret_gn,
                cache_ckv, cache_krope, state_ret, tabs_m):
    cq, ckv, krope, rq, rk, rv, rg = jnp.split(w_in, np.cumsum(
        (MLA_Q_RANK, MLA_KV_RANK, MLA_ROPE, RET_HEADS * RET_DK, RET_HEADS * RET_DK, RET_HEADS * RET_DV))[:].tolist(),
        axis=1)
    z = lambda n: jnp.zeros((D_MODEL, n), F32)
    w_p = jnp.concatenate([cq, ckv, z(MLA_NOPE), krope, z(LANES - MLA_QK), rq, rk, rv, rg], axis=1)
    p = _inproj(x, g_mix, mod, w_p, jnp.zeros((1, w_p.shape[1]), F32), 512, None)
    CKV_BLK, KR_BLK, RQ_BLK, RK_BLK, RV_BLK, RG_BLK = 2, 3, 4, 6, 8, 12

    def head_pad(w, n_head, d_head, c0, c1):
        w = w.reshape(w.shape[0], n_head, d_head)[:, :, c0:c1]
        return jnp.pad(w, ((0, 0), (0, 0), (0, LANES - (c1 - c0)))).reshape(w.shape[0], n_head * LANES)

    w_uq_p = head_pad(w_uq, MLA_HEADS, MLA_QK, 0, MLA_QK)
    wk_p = head_pad(w_ukv, MLA_HEADS, MLA_NOPE + MLA_V, 0, MLA_NOPE)
    wv_p = head_pad(w_ukv, MLA_HEADS, MLA_NOPE + MLA_V, MLA_NOPE, MLA_NOPE + MLA_V)
    qn_p = jnp.pad(qn, (0, LANES - MLA_QK))[None]
    kn_p = jnp.pad(kn, (0, LANES - MLA_QK))[None]

    q = _mla_q(p, q_norm[None], w_uq_p, qn_p, tabs_m)
    k, v, ckvn = _mla_kv(p, CKV_BLK, p, KR_BLK, kv_norm[None], wk_p, wv_p, kn_p, tabs_m, N_TOK, True)

    n_c = DEC_BATCH * PAST_LEN
    kr_c = jnp.pad(cache_krope.reshape(n_c, MLA_ROPE), ((0, 0), (MLA_NOPE, LANES - MLA_QK)))
    k_c, v_c, _ = _mla_kv(cache_ckv.reshape(n_c, MLA_KV_RANK), 0, kr_c, 0, kv_norm[None], wk_p, wv_p, kn_p,
                          tabs_m, n_c, False)

    hw = MLA_HEADS * LANES

    def with_cache(own, cache):
        own = own[N_CTX:].reshape(DEC_BATCH, DEC_SEQ, hw)
        return jnp.concatenate([own, cache.reshape(DEC_BATCH, PAST_LEN, hw)], 1).reshape(-1, hw)

    o_ctx = _mla_attn(q, k, v, BATCH, SEQ, SEQ, 0, SEQ)
    o_lat = _mla_attn(q, with_cache(k, k_c), with_cache(v, v_c), DEC_BATCH, DEC_SEQ, DEC_SEQ + PAST_LEN, N_CTX, 256)

    log_g = -_softplus(-ret_decay)
    gn = ret_gn[None]
    s0_ctx = jnp.zeros((BATCH, 2, RET_HEADS * RET_DK, RET_DV), F32)
    r_ctx, st_ctx = _retention(log_g, p, RQ_BLK, RK_BLK, RV_BLK, RG_BLK, s0_ctx, gn, BATCH, SEQ, 0, SEQ)
    s0_lat = state_ret.reshape(DEC_BATCH, 2, RET_HEADS * RET_DK, RET_DV)
    r_lat, _ = _retention(log_g, p, RQ_BLK, RK_BLK, RV_BLK, RG_BLK, s0_lat, gn, DEC_BATCH, DEC_SEQ, N_CTX, 256)

    mix = [(o_ctx, o_lat), (r_ctx, r_lat)]
    new_ckv = ckvn[:N_CTX].reshape(BATCH, SEQ, MLA_KV_RANK)
    new_krope = p[:N_CTX, KR_BLK * LANES + MLA_NOPE:KR_BLK * LANES + MLA_QK]
    new_krope = new_krope.reshape(BATCH, SEQ, MLA_ROPE)
    new_ret = st_ctx.reshape(BATCH, 2, RET_HEADS, RET_DK, RET_DV)
    return mix, new_ckv, new_krope, new_ret


def _odd_layer(x, mod, g_mix, w_in, qn, kn, lam, diff_gn, mu, w0, w_up, a0, a_up, g_up, k_k, k_a, r_k, gn,
               cache_k, cache_v, state_rwkv, tabs_d, lam_init):
    w_p = w_in
    n_in = w_p.shape[1]
    mu_full = jnp.concatenate([jnp.zeros((3 * DIFF_W,), F32), mu])[None]
    p = _inproj(x, g_mix, mod, w_p, mu_full, 384, (3 * DIFF_W) // 384)
    DV_BLK, R_BLK, K_BLK, V_BLK = 2, 3, 4, 5
    LO_BLK = (6 * RWKV_W) // (3 * LANES)

    qn_p = jnp.tile(qn, 2)[None]
    kn_p = jnp.tile(kn, 2)[None]
    q, k = _diff_qk(p, qn_p, kn_p, tabs_d)

    n_c = DEC_BATCH * PAST_LEN
    k_c = cache_k.reshape(DEC_BATCH, PAST_LEN, DIFF_W)
    v_c = cache_v.reshape(DEC_BATCH, PAST_LEN, DIFF_W)

    def with_cache(own_lat, cache):
        return jnp.concatenate([own_lat.reshape(DEC_BATCH, DEC_SEQ, DIFF_W), cache], 1).reshape(-1, DIFF_W)

    dgn = diff_gn[None]
    dv_cols = slice(DV_BLK * DIFF_W, (DV_BLK + 1) * DIFF_W)
    o_ctx = _diff_attn(lam, q, k, p, DV_BLK * DIFF_HEADS, dgn, BATCH, SEQ, SEQ, 0, SEQ, lam_init)
    o_lat = _diff_attn(lam, q, with_cache(k[N_CTX:], k_c), with_cache(p[N_CTX:, dv_cols], v_c), 0, dgn,
                       DEC_BATCH, DEC_SEQ, DEC_SEQ + PAST_LEN, N_CTX, 256, lam_init)

    zero = jnp.zeros((RWKV_W_LORA, RWKV_W), F32)
    wup_bd = jnp.concatenate([jnp.concatenate([w_up[0], zero], 1), jnp.concatenate([zero, w_up[1]], 1)], 0)
    aup_bd = jnp.concatenate([jnp.concatenate([a_up[0], zero], 1), jnp.concatenate([zero, a_up[1]], 1)], 0)
    op, g, bonus = _rwkv_pre(p, R_BLK, K_BLK, V_BLK, LO_BLK, wup_bd, aup_bd, g_up, w0.reshape(1, -1),
                             a0.reshape(1, -1), k_k[None], k_a[None], r_k.reshape(1, -1))
    op4 = op.reshape(N_TOK // SEQ, SEQ, 2 * _SCAN_NJ, LANES)
    y_ctx, st_ctx = _rwkv_scan_ctx(op4)
    s0_lat = jnp.transpose(state_rwkv, (4, 1, 0, 2, 3)).reshape(RWKV_HS, _LAT_STATES, RWKV_HS)
    y_lat_f, y_lat_b = _rwkv_scan_lat(op4, _value_split_layout(s0_lat))
    rw_o = _rwkv_post(y_ctx, y_lat_f, y_lat_b, bonus, p, V_BLK, g, gn[None])

    mix = [(o_ctx, o_lat), rw_o]
    new_dk = k[:N_CTX].reshape(BATCH, SEQ, DIFF_HEADS, 2, DIFF_DH)
    new_dv = p[:N_CTX, dv_cols].reshape(BATCH, SEQ, DIFF_HEADS, 2 * DIFF_DH)
    new_rwkv = jnp.transpose(st_ctx.reshape(2, RWKV_HS, RWKV_HS, BATCH, RWKV_HEADS), (3, 0, 4, 2, 1))
    return mix, new_dk, new_dv, new_rwkv


def kernel(x_prompt, x_sample, cache_mla_ckv, cache_mla_krope, state_ret, cache_diff_k, cache_diff_v, state_rwkv,
           c, c_ctx, ada_w, ada_b, norm_mix_g, norm_ffn_g, w_out, ffn_up, ffn_conv_w, ffn_conv_b, ffn_down,
           a_w_in, mla_q_norm, mla_kv_norm, mla_w_uq, mla_w_ukv, mla_qn, mla_kn, ret_decay, ret_gn,
           b_w_in, diff_qn, diff_kn, diff_lam, diff_gn, rwkv_mu, rwkv_w0, rwkv_w_up, rwkv_a0, rwkv_a_up,
           rwkv_g_up, rwkv_k_k, rwkv_k_a, rwkv_r_k, rwkv_gn):
    x = jnp.concatenate([x_prompt.reshape(N_CTX, D_MODEL), x_sample.reshape(N_LAT, D_MODEL)], 0)
    cond8 = jnp.pad(jnp.concatenate([c_ctx[None], c], 0), ((0, 8 - N_GROUPS), (0, 0)))
    mod = _modulation(cond8, ada_w, ada_b)

    tabs_m = _rope_tables(MLA_ROPE, (MLA_NOPE,))
    tabs_d = _rope_tables(DIFF_DH, (0, DIFF_DH))

    outs = {}
    for l in range(DEPTH):
        j = l // 2
        g_mix = norm_mix_g[l][None]
        if l % 2 == 0:
            mix, outs["ckv"], outs["krope"], outs["ret"] = _even_layer(
                x, mod[l], g_mix, a_w_in[j], mla_q_norm[j], mla_kv_norm[j], mla_w_uq[j], mla_w_ukv[j], mla_qn[j],
                mla_kn[j], ret_decay[j], ret_gn[j], cache_mla_ckv[:, j], cache_mla_krope[:, j], state_ret[:, j],
                tabs_m)
        else:
            lam_init = 0.8 - 0.6 * math.exp(-0.3 * l)
            mix, outs["dk"], outs["dv"], outs["rwkv"] = _odd_layer(
                x, mod[l], g_mix, b_w_in[j], diff_qn[j], diff_kn[j], diff_lam[j], diff_gn[j], rwkv_mu[j],
                rwkv_w0[j], rwkv_w_up[j], rwkv_a0[j], rwkv_a_up[j], rwkv_g_up[j], rwkv_k_k[j], rwkv_k_a[j],
                rwkv_r_k[j], rwkv_gn[j], cache_diff_k[:, j], cache_diff_v[:, j], state_rwkv[:, j], tabs_d, lam_init)
        x = _resid_proj(mix, w_out[l], x, mod[l], 2)
        act = _ffn_up(x, norm_ffn_g[l][None], mod[l], ffn_up[l], ffn_conv_w[l], ffn_conv_b[l])
        x = _resid_proj([act], ffn_down[l], x, mod[l], 5)

    y_prompt = x[:N_CTX].reshape(BATCH, SEQ, D_MODEL)
    y_sample = x[N_CTX:].reshape(DEC_BATCH, DEC_SEQ, D_MODEL)
    return (y_prompt, y_sample, outs["ckv"][:, None], outs["krope"][:, None], outs["ret"][:, None],
            outs["dk"][:, None], outs["dv"][:, None], outs["rwkv"][:, None])
```

```python
import functools
import math

import numpy as np
import jax
import jax.numpy as jnp
from jax import lax
from jax.experimental import pallas as pl
from jax.experimental.pallas import tpu as pltpu

D_MODEL = 1024
BATCH = 16
SEQ = 256
DEPTH = 2
DEC_BATCH = 2
DEC_SEQ = 1024
PAST_LEN = 512
GRID_W = 64
EPS = 1e-6
ROPE_BASE = 10000.0

MLA_HEADS = 8
MLA_Q_RANK = 256
MLA_KV_RANK = 128
MLA_NOPE = 64
MLA_ROPE = 32
MLA_V = 64
MLA_QK = MLA_NOPE + MLA_ROPE
RET_HEADS = 4
RET_DK = 64
RET_DV = 128
DIFF_HEADS = 4
DIFF_DH = 64
DIFF_W = DIFF_HEADS * 2 * DIFF_DH
RWKV_HEADS = 8
RWKV_HS = 64
RWKV_W = RWKV_HEADS * RWKV_HS
RWKV_W_LORA = 64
RWKV_A_LORA = 64
RWKV_G_LORA = 128
D_FF = 2816

N_CTX = BATCH * SEQ
N_LAT = DEC_BATCH * DEC_SEQ
N_TOK = N_CTX + N_LAT
N_GROUPS = 1 + DEC_BATCH

LANES = 128
VMEM_LIMIT = 56 * 1024 * 1024

_PREC = lax.Precision.HIGHEST
F32 = jnp.float32


def _dot_tn(a, b):
    return lax.dot_general(a, b, (((0,), (0,)), ((), ())), precision=_PREC, preferred_element_type=F32)


BF16 = jnp.bfloat16


def _dot_bf16(a, b):
    return jnp.dot(a.astype(BF16), b.astype(BF16), preferred_element_type=F32)


def _dot_nt_bf16(a, b):
    return lax.dot_general(a.astype(BF16), b.astype(BF16), (((1,), (1,)), ((), ())), preferred_element_type=F32)


def _params(*sem):
    return pltpu.CompilerParams(dimension_semantics=sem, vmem_limit_bytes=VMEM_LIMIT)


def _sigmoid(x):
    return 1.0 / (1.0 + jnp.exp(-x))


def _silu(x):
    return x * _sigmoid(x)


def _softplus(x):
    return jnp.maximum(x, 0.0) + jnp.log(1.0 + jnp.exp(-jnp.abs(x)))


def _rms(x, n):
    return x * lax.rsqrt(jnp.sum(x * x, axis=-1, keepdims=True) * (1.0 / n) + EPS)


def _lane_lo(shape):
    return lax.broadcasted_iota(jnp.int32, shape, len(shape) - 1) < 64


def _seg64_sum(x):
    lo = _lane_lo(x.shape)
    s_lo = jnp.sum(jnp.where(lo, x, 0.0), axis=-1, keepdims=True)
    s_hi = jnp.sum(jnp.where(lo, 0.0, x), axis=-1, keepdims=True)
    return jnp.where(lo, s_lo, s_hi)


def _seq_neighbours(p, tile, tile_rows):
    seq_mask = jnp.where(tile * tile_rows < N_CTX, SEQ - 1, DEC_SEQ - 1)
    pos = lax.broadcasted_iota(jnp.int32, (tile_rows, 1), 0) & seq_mask
    prev = jnp.where(pos == 0, 0.0, pltpu.roll(p, 1, axis=0))
    nxt = jnp.where(pos == seq_mask, 0.0, pltpu.roll(p, tile_rows - 1, axis=0))
    return prev, nxt


def _group_of_tile(i, tile_rows):
    row = i * tile_rows
    return jnp.where(row < N_CTX, 0, 1 + (row - N_CTX) // DEC_SEQ)


def _modulation_kernel(c_ref, w_ref, b_ref, o_ref):
    o_ref[0] = _dot_bf16(_silu(c_ref[...]), w_ref[0]) + b_ref[0]


def _modulation(cond8, ada_w, ada_b):
    tn = 512
    n = 6 * D_MODEL
    out = pl.pallas_call(
        _modulation_kernel,
        grid=(DEPTH, n // tn),
        in_specs=[pl.BlockSpec((8, D_MODEL), lambda l, j: (0, 0)),
                  pl.BlockSpec((1, D_MODEL, tn), lambda l, j: (l, 0, j)),
                  pl.BlockSpec((1, 1, tn), lambda l, j: (l, 0, j))],
        out_specs=pl.BlockSpec((1, 8, tn), lambda l, j: (l, 0, j)),
        out_shape=jax.ShapeDtypeStruct((DEPTH, 8, n), F32),
        compiler_params=_params("parallel", "parallel"),
        name="modulation",
    )(cond8, ada_w, ada_b.reshape(DEPTH, 1, n))
    m = out[:, :N_GROUPS].reshape(DEPTH, N_GROUPS, 6, D_MODEL)
    return jnp.pad(m, ((0, 0), (0, 0), (0, 2), (0, 0)))


_TM_SEQ = 1024


def _norm_mod(x, g, mod, off):
    return _rms(x, D_MODEL) * g * (1.0 + mod[off + 1:off + 2, :]) + mod[off:off + 1, :]


def _inproj_kernel(x_ref, g_ref, mod_ref, w_ref, mu_ref, o_ref, h_ref, *, shift_from):
    i = pl.program_id(0)

    @pl.when(pl.program_id(1) == 0)
    def _():
        h_ref[...] = _norm_mod(x_ref[...], g_ref[...], mod_ref[0], 0).astype(BF16)

    p = _dot_bf16(h_ref[...], w_ref[...])
    if shift_from is None:
        o_ref[...] = p
    else:
        @pl.when(pl.program_id(1) < shift_from)
        def _():
            o_ref[...] = p

        @pl.when(pl.program_id(1) >= shift_from)
        def _():
            prev, nxt = _seq_neighbours(p, i, _TM_SEQ)
            o_ref[...] = p + (0.5 * (prev + nxt) - p) * mu_ref[...]


def _inproj(x, g, mod, w, mu, tn, shift_from):
    n = w.shape[1]
    tm = _TM_SEQ
    return pl.pallas_call(
        functools.partial(_inproj_kernel, shift_from=shift_from),
        grid=(N_TOK // tm, n // tn),
        in_specs=[pl.BlockSpec((tm, D_MODEL), lambda i, j: (i, 0)),
                  pl.BlockSpec((1, D_MODEL), lambda i, j: (0, 0)),
                  pl.BlockSpec((1, 8, D_MODEL), lambda i, j: (_group_of_tile(i, tm), 0, 0)),
                  pl.BlockSpec((D_MODEL, tn), lambda i, j: (0, j)),
                  pl.BlockSpec((1, tn), lambda i, j: (0, j))],
        out_specs=pl.BlockSpec((tm, tn), lambda i, j: (i, j)),
        out_shape=jax.ShapeDtypeStruct((N_TOK, n), F32),
        scratch_shapes=[pltpu.VMEM((tm, D_MODEL), BF16)],
        compiler_params=_params("parallel", "arbitrary"),
        name="inproj" if shift_from is None else "inproj_shift",
    )(x, g, mod, w, mu)


def _resid_kernel(*refs, gate_row, widths, split):
    n_in = sum(2 if sp else 1 for sp in split)
    a_refs = refs[:n_in]
    w_ref, x_ref, mod_ref, o_ref, a_bf_ref = refs[n_in:]
    i = pl.program_id(0)

    @pl.when(pl.program_id(1) == 0)
    def _():
        k0, r = 0, 0
        for width, sp in zip(widths, split):
            cols = slice(k0, k0 + width)
            if sp:
                ctx_ref, lat_ref = a_refs[r], a_refs[r + 1]

                @pl.when(i < N_CTX // _TM_SEQ)
                def _():
                    a_bf_ref[:, cols] = ctx_ref[...].astype(BF16)

                @pl.when(i >= N_CTX // _TM_SEQ)
                def _():
                    a_bf_ref[:, cols] = lat_ref[...].astype(BF16)
            else:
                a_bf_ref[:, cols] = a_refs[r][...].astype(BF16)
            k0 += width
            r += 2 if sp else 1

    o_ref[...] = x_ref[...] + mod_ref[0, gate_row:gate_row + 1, :] * _dot_bf16(a_bf_ref[...], w_ref[...])


def _resid_proj(acts, w, x, mod, gate_row):
    tm, tn = _TM_SEQ, 256
    n_ctx = N_CTX // tm
    split = [isinstance(a, (tuple, list)) for a in acts]
    widths = [a[0].shape[1] if sp else a.shape[1] for a, sp in zip(acts, split)]
    k = sum(widths)
    in_specs, operands = [], []
    for a, width, sp in zip(acts, widths, split):
        if sp:
            in_specs += [pl.BlockSpec((tm, width), lambda i, j: (jnp.minimum(i, n_ctx - 1), 0)),
                         pl.BlockSpec((tm, width), lambda i, j: (jnp.maximum(i - n_ctx, 0), 0))]
            operands += list(a)
        else:
            in_specs.append(pl.BlockSpec((tm, width), lambda i, j: (i, 0)))
            operands.append(a)
    return pl.pallas_call(
        functools.partial(_resid_kernel, gate_row=gate_row, widths=tuple(widths), split=tuple(split)),
        grid=(N_TOK // tm, D_MODEL // tn),
        in_specs=in_specs
        + [pl.BlockSpec((k, tn), lambda i, j: (0, j)),
           pl.BlockSpec((tm, tn), lambda i, j: (i, j)),
           pl.BlockSpec((1, 8, tn), lambda i, j: (_group_of_tile(i, tm), 0, j))],
        out_specs=pl.BlockSpec((tm, tn), lambda i, j: (i, j)),
        out_shape=jax.ShapeDtypeStruct((N_TOK, D_MODEL), F32),
        scratch_shapes=[pltpu.VMEM((tm, k), BF16)],
        compiler_params=_params("parallel", "arbitrary"),
        name="resid_proj",
    )(*operands, w, x, mod)


def _ffn_up_kernel(x_ref, g_ref, mod_ref, wa_ref, wb_ref, cwa_ref, cwb_ref, cba_ref, cbb_ref, o_ref, h_ref):
    i = pl.program_id(0)

    @pl.when(pl.program_id(1) == 0)
    def _():
        h_ref[...] = _norm_mod(x_ref[...], g_ref[...], mod_ref[0], 3).astype(BF16)

    h = h_ref[...]

    def conv(w_ref, cw_ref, cb_ref):
        u = _dot_bf16(h, w_ref[...])
        prev, nxt = _seq_neighbours(u, i, _TM_SEQ)
        return prev * cw_ref[0:1, :] + u * cw_ref[1:2, :] + nxt * cw_ref[2:3, :] + cb_ref[...]

    o_ref[...] = (_silu(conv(wa_ref, cwa_ref, cba_ref)) * conv(wb_ref, cwb_ref, cbb_ref)).astype(o_ref.dtype)


def _ffn_up(x, g, mod, up, cw, cb):
    tm, tn = _TM_SEQ, 256
    nb = D_FF // tn
    cb = cb.reshape(1, 2 * D_FF)
    return pl.pallas_call(
        _ffn_up_kernel,
        grid=(N_TOK // tm, nb),
        in_specs=[pl.BlockSpec((tm, D_MODEL), lambda i, j: (i, 0)),
                  pl.BlockSpec((1, D_MODEL), lambda i, j: (0, 0)),
                  pl.BlockSpec((1, 8, D_MODEL), lambda i, j: (_group_of_tile(i, tm), 0, 0)),
                  pl.BlockSpec((D_MODEL, tn), lambda i, j: (0, j)),
                  pl.BlockSpec((D_MODEL, tn), lambda i, j: (0, j + nb)),
                  pl.BlockSpec((3, tn), lambda i, j: (0, j)),
                  pl.BlockSpec((3, tn), lambda i, j: (0, j + nb)),
                  pl.BlockSpec((1, tn), lambda i, j: (0, j)),
                  pl.BlockSpec((1, tn), lambda i, j: (0, j + nb))],
        out_specs=pl.BlockSpec((tm, tn), lambda i, j: (i, j)),
        out_shape=jax.ShapeDtypeStruct((N_TOK, D_FF), BF16),
        scratch_shapes=[pltpu.VMEM((tm, D_MODEL), BF16)],
        compiler_params=_params("parallel", "arbitrary"),
        name="ffn_up",
    )(x, g, mod, up, up, cw, cw, cb, cb)


_ROPE_TM = 512


def _rope(y, c, s1, s2):
    return y * c + pltpu.roll(y, 1, axis=1) * s1 + pltpu.roll(y, LANES - 1, axis=1) * s2


def _rope_tables(rot_dim, lane_offsets):
    t = np.arange(DEC_SEQ)
    row, col = t // GRID_W, t % GRID_W
    n_freq = rot_dim // 4
    inv = ROPE_BASE ** (-np.arange(n_freq, dtype=np.float64) / n_freq)
    ang = np.concatenate([row[:, None] * inv, col[:, None] * inv], -1)
    cos, sin = np.cos(ang), np.sin(ang)
    n = _ROPE_TM + DEC_SEQ
    c, s1, s2 = np.ones((n, LANES)), np.zeros((n, LANES)), np.zeros((n, LANES))
    for a in lane_offsets:
        even = a + 2 * np.arange(rot_dim // 2)
        c[_ROPE_TM:, even] = cos
        c[_ROPE_TM:, even + 1] = cos
        s1[_ROPE_TM:, even + 1] = sin
        s2[_ROPE_TM:, even] = -sin
    return tuple(jnp.asarray(x, F32) for x in (c, s1, s2))


def _rope_block(i):
    row = i * _ROPE_TM
    return jnp.where(row < N_CTX, 0, 1 + ((row - N_CTX) % DEC_SEQ) // _ROPE_TM)


def _mla_q_kernel(cq_ref, qnorm_ref, w_ref, qn_ref, c_ref, s1_ref, s2_ref, o_ref):
    xn = _rms(cq_ref[...], MLA_Q_RANK) * qnorm_ref[...]
    y = _dot_bf16(xn, w_ref[...])
    c, s1, s2 = c_ref[...], s1_ref[...], s2_ref[...]
    for h in range(MLA_HEADS):
        yh = y[:, h * LANES:(h + 1) * LANES]
        yh = _rms(yh, MLA_QK) * qn_ref[...]
        o_ref[:, h * LANES:(h + 1) * LANES] = _rope(yh, c, s1, s2)


def _mla_q(p, q_norm, w_uq_p, qn_p, tabs):
    tm = _ROPE_TM
    hw = MLA_HEADS * LANES
    tab_spec = pl.BlockSpec((tm, LANES), lambda i: (_rope_block(i), 0))
    return pl.pallas_call(
        _mla_q_kernel,
        grid=(N_TOK // tm,),
        in_specs=[pl.BlockSpec((tm, MLA_Q_RANK), lambda i: (i, 0)),
                  pl.BlockSpec((1, MLA_Q_RANK), lambda i: (0, 0)),
                  pl.BlockSpec((MLA_Q_RANK, hw), lambda i: (0, 0)),
                  pl.BlockSpec((1, LANES), lambda i: (0, 0)),
                  tab_spec, tab_spec, tab_spec],
        out_specs=pl.BlockSpec((tm, hw), lambda i: (i, 0)),
        out_shape=jax.ShapeDtypeStruct((N_TOK, hw), F32),
        compiler_params=_params("parallel"),
        name="mla_q",
    )(p, q_norm, w_uq_p, qn_p, *tabs)


def _mla_kv_kernel(ckv_ref, kr_ref, kvn_ref, wk_ref, wv_ref, kn_ref, c_ref, s1_ref, s2_ref,
                   k_ref, v_ref, ckvn_ref, *, norm_ckv):
    ckv = ckv_ref[...]
    if norm_ckv:
        ckv = _rms(ckv, MLA_KV_RANK) * kvn_ref[...]
    ckvn_ref[...] = ckv
    ckv_bf = ckv.astype(BF16)
    kk = _dot_bf16(ckv_bf, wk_ref[...])
    v_ref[...] = _dot_bf16(ckv_bf, wv_ref[...])
    kr = kr_ref[...]
    c, s1, s2 = c_ref[...], s1_ref[...], s2_ref[...]
    for h in range(MLA_HEADS):
        kh = kk[:, h * LANES:(h + 1) * LANES] + kr
        kh = _rms(kh, MLA_QK) * kn_ref[...]
        k_ref[:, h * LANES:(h + 1) * LANES] = _rope(kh, c, s1, s2)


def _mla_kv(ckv_src, ckv_blk, kr_src, kr_blk, kv_norm, wk_p, wv_p, kn_p, tabs, n_rows, own_tokens):
    tm = _ROPE_TM
    hw = MLA_HEADS * LANES
    tab_spec = pl.BlockSpec((tm, LANES), (lambda i: (_rope_block(i), 0)) if own_tokens else (lambda i: (0, 0)))
    return pl.pallas_call(
        functools.partial(_mla_kv_kernel, norm_ckv=own_tokens),
        grid=(n_rows // tm,),
        in_specs=[pl.BlockSpec((tm, LANES), lambda i: (i, ckv_blk)),
                  pl.BlockSpec((tm, LANES), lambda i: (i, kr_blk)),
                  pl.BlockSpec((1, LANES), lambda i: (0, 0)),
                  pl.BlockSpec((MLA_KV_RANK, hw), lambda i: (0, 0)),
                  pl.BlockSpec((MLA_KV_RANK, hw), lambda i: (0, 0)),
                  pl.BlockSpec((1, LANES), lambda i: (0, 0)),
                  tab_spec, tab_spec, tab_spec],
        out_specs=[pl.BlockSpec((tm, hw), lambda i: (i, 0)),
                   pl.BlockSpec((tm, hw), lambda i: (i, 0)),
                   pl.BlockSpec((tm, LANES), lambda i: (i, 0))],
        out_shape=[jax.ShapeDtypeStruct((n_rows, hw), F32),
                   jax.ShapeDtypeStruct((n_rows, hw), F32),
                   jax.ShapeDtypeStruct((n_rows, LANES), F32)],
        compiler_params=_params("parallel"),
        name="mla_kv",
    )(ckv_src, kr_src, kv_norm, wk_p, wv_p, kn_p, *tabs)


def _softmax_rows(s):
    p = jnp.exp(s - jnp.max(s, axis=-1, keepdims=True))
    return p, jnp.sum(p, axis=-1, keepdims=True)


def _mla_attn_kernel(q_ref, k_ref, v_ref, o_ref):
    scale = MLA_QK ** -0.5
    outs = []
    for h in range(2):
        sl = slice(h * LANES, (h + 1) * LANES)
        p, l = _softmax_rows(_dot_nt_bf16(q_ref[:, sl], k_ref[:, sl]) * scale)
        outs.append(_dot_bf16(p, v_ref[:, sl]) / l)
    o_ref[...] = (outs[0] + pltpu.roll(outs[1], MLA_V, axis=1)).astype(o_ref.dtype)


def _mla_attn(q, k, v, batch, nq, nk, q_row0, tq):
    nqb = nq // tq
    qb0 = q_row0 // tq
    return pl.pallas_call(
        _mla_attn_kernel,
        grid=(batch, MLA_HEADS // 2, nqb),
        in_specs=[pl.BlockSpec((tq, 2 * LANES), lambda b, h, i: (qb0 + b * nqb + i, h)),
                  pl.BlockSpec((nk, 2 * LANES), lambda b, h, i: (b, h)),
                  pl.BlockSpec((nk, 2 * LANES), lambda b, h, i: (b, h))],
        out_specs=pl.BlockSpec((tq, LANES), lambda b, h, i: (b * nqb + i, h)),
        out_shape=jax.ShapeDtypeStruct((batch * nq, MLA_HEADS * MLA_V), BF16),
        compiler_params=_params("parallel", "parallel", "arbitrary"),
        name="mla_attn",
    )(q, k, v)


def _ret_kernel(lg_ref, q_ref, k_ref, v_ref, rg_ref, s0_ref, gn_ref, o_ref, st_ref, *, n, tq):
    b, pair, qi = pl.program_id(0), pl.program_id(1), pl.program_id(2)
    q = q_ref[...]
    k = k_ref[...] * (RET_DK ** -0.5)
    lo = _lane_lo((1, LANES))
    row = (qi * tq + lax.broadcasted_iota(jnp.int32, (tq, 1), 0)).astype(F32)
    col = lax.broadcasted_iota(jnp.int32, (1, n), 1).astype(F32)
    diff = row - col
    for h in range(2):
        lgf = lg_ref[0, 2 * pair + h]
        lgb = lg_ref[1, 2 * pair + h]
        mask = lo if h == 0 else jnp.logical_not(lo)
        qh = jnp.where(mask, q, 0.0)
        vh = v_ref[:, h * LANES:(h + 1) * LANES]
        decay = (jnp.where(diff >= 0, jnp.exp(lgf * jnp.maximum(diff, 0.0)), 0.0)
                 + jnp.where(diff <= 0, jnp.exp(lgb * jnp.maximum(-diff, 0.0)), 0.0))
        o = _dot_bf16(_dot_nt_bf16(qh, k) * decay, vh)
        o = o + _dot_bf16(qh * jnp.exp(lgf * (row + 1.0)), s0_ref[0, 0])
        o = o + _dot_bf16(qh * jnp.exp(lgb * (n - row)), s0_ref[0, 1])
        y = _rms(o, RET_DV) * gn_ref[:, h * LANES:(h + 1) * LANES]
        o_ref[:, h * LANES:(h + 1) * LANES] = (_silu(rg_ref[:, h * LANES:(h + 1) * LANES]) * y).astype(o_ref.dtype)

    @pl.when(qi == 0)
    def _():
        pos = lax.broadcasted_iota(jnp.int32, (n, 1), 0).astype(F32)
        for d in range(2):
            acc = None
            for h in range(2):
                lg = lg_ref[d, 2 * pair + h]
                mask = lo if h == 0 else jnp.logical_not(lo)
                expo = (n - 1.0 - pos) if d == 0 else pos
                kd = jnp.where(mask, k * jnp.exp(lg * expo), 0.0)
                term = _dot_tn(kd, v_ref[:, h * LANES:(h + 1) * LANES])
                acc = term if acc is None else acc + term
            lg_rows = jnp.where(lax.broadcasted_iota(jnp.int32, (LANES, 1), 0) < 64,
                                lg_ref[d, 2 * pair], lg_ref[d, 2 * pair + 1])
            st_ref[0, d] = acc + s0_ref[0, d] * jnp.exp(lg_rows * n)


def _retention(log_g, p, q_blk, k_blk, v_blk, g_blk, s0, gn, batch, n, row0, tq):
    nqb = n // tq
    qb0 = row0 // tq
    kb0 = row0 // n
    pairs = RET_HEADS // 2
    return pl.pallas_call(
        functools.partial(_ret_kernel, n=n, tq=tq),
        grid=(batch, pairs, nqb),
        in_specs=[pl.BlockSpec(memory_space=pltpu.SMEM),
                  pl.BlockSpec((tq, LANES), lambda b, h, i: (qb0 + b * nqb + i, q_blk + h)),
                  pl.BlockSpec((n, LANES), lambda b, h, i: (kb0 + b, k_blk + h)),
                  pl.BlockSpec((n, 2 * LANES), lambda b, h, i: (kb0 + b, v_blk // 2 + h)),
                  pl.BlockSpec((tq, 2 * LANES), lambda b, h, i: (qb0 + b * nqb + i, g_blk // 2 + h)),
                  pl.BlockSpec((1, 2, LANES, LANES), lambda b, h, i: (b, 0, h, 0)),
                  pl.BlockSpec((1, 2 * LANES), lambda b, h, i: (0, h))],
        out_specs=[pl.BlockSpec((tq, 2 * LANES), lambda b, h, i: (b * nqb + i, h)),
                   pl.BlockSpec((1, 2, LANES, LANES), lambda b, h, i: (b, 0, h, 0))],
        out_shape=[jax.ShapeDtypeStruct((batch * n, RET_HEADS * RET_DV), BF16),
                   jax.ShapeDtypeStruct((batch, 2, RET_HEADS * RET_DK, RET_DV), F32)],
        compiler_params=_params("parallel", "parallel", "arbitrary"),
        name="retention",
    )(log_g, p, p, p, p, s0, gn)


def _diff_qk_kernel(q_ref, k_ref, qn_ref, kn_ref, c_ref, s1_ref, s2_ref, qo_ref, ko_ref):
    c, s1, s2 = c_ref[...], s1_ref[...], s2_ref[...]
    for src, gain, dst in ((q_ref, qn_ref, qo_ref), (k_ref, kn_ref, ko_ref)):
        for h in range(DIFF_HEADS):
            sl = slice(h * LANES, (h + 1) * LANES)
            y = src[:, sl]
            y = y * lax.rsqrt(_seg64_sum(y * y) * (1.0 / DIFF_DH) + EPS) * gain[...]
            dst[:, sl] = _rope(y, c, s1, s2)


def _diff_qk(p, qn_p, kn_p, tabs):
    tm = _ROPE_TM
    tab_spec = pl.BlockSpec((tm, LANES), lambda i: (_rope_block(i), 0))
    return pl.pallas_call(
        _diff_qk_kernel,
        grid=(N_TOK // tm,),
        in_specs=[pl.BlockSpec((tm, DIFF_W), lambda i: (i, 0)),
                  pl.BlockSpec((tm, DIFF_W), lambda i: (i, 1)),
                  pl.BlockSpec((1, LANES), lambda i: (0, 0)),
                  pl.BlockSpec((1, LANES), lambda i: (0, 0)),
                  tab_spec, tab_spec, tab_spec],
        out_specs=[pl.BlockSpec((tm, DIFF_W), lambda i: (i, 0)),
                   pl.BlockSpec((tm, DIFF_W), lambda i: (i, 0))],
        out_shape=[jax.ShapeDtypeStruct((N_TOK, DIFF_W), F32)] * 2,
        compiler_params=_params("parallel"),
        name="diff_qk",
    )(p, p, qn_p, kn_p, *tabs)


def _diff_attn_kernel(lam_ref, q_ref, k_ref, v_ref, gn_ref, o_ref, *, lam_init):
    lv = lam_ref[...]
    lam = (jnp.exp(jnp.sum(lv[0:1] * lv[1:2], axis=-1, keepdims=True))
           - jnp.exp(jnp.sum(lv[2:3] * lv[3:4], axis=-1, keepdims=True)) + lam_init)
    scale = DIFF_DH ** -0.5
    q = q_ref[...]
    k = k_ref[...]
    lo = _lane_lo((1, LANES))
    kb = k.astype(BF16)
    p1, l1 = _softmax_rows(_dot_nt_bf16(jnp.where(lo, q, 0.0), kb) * scale)
    p2, l2 = _softmax_rows(_dot_nt_bf16(jnp.where(lo, 0.0, q), kb) * scale)
    w = p1 / l1 - lam * (p2 / l2)
    o = _dot_bf16(w, v_ref[...])
    o_ref[...] = (_rms(o, 2 * DIFF_DH) * gn_ref[...] * (1.0 - lam_init)).astype(o_ref.dtype)


def _diff_attn(lam, q, k, v, v_blk0, gn, batch, nq, nk, q_row0, tq, lam_init):
    nqb = nq // tq
    qb0 = q_row0 // tq
    return pl.pallas_call(
        functools.partial(_diff_attn_kernel, lam_init=lam_init),
        grid=(batch, DIFF_HEADS, nqb),
        in_specs=[pl.BlockSpec((4, DIFF_DH), lambda b, h, i: (0, 0)),
                  pl.BlockSpec((tq, LANES), lambda b, h, i: (qb0 + b * nqb + i, h)),
                  pl.BlockSpec((nk, LANES), lambda b, h, i: (b, h)),
                  pl.BlockSpec((nk, LANES), lambda b, h, i: (b, v_blk0 + h)),
                  pl.BlockSpec((1, LANES), lambda b, h, i: (0, h))],
        out_specs=pl.BlockSpec((tq, LANES), lambda b, h, i: (b * nqb + i, h)),
        out_shape=jax.ShapeDtypeStruct((batch * nq, DIFF_W), BF16),
        compiler_params=_params("parallel", "parallel", "arbitrary"),
        name="diff_attn",
    )(lam, q, k, v, gn)


def _seg64_sum_wide(x):
    return jnp.concatenate([_seg64_sum(x[:, j * LANES:(j + 1) * LANES]) for j in range(x.shape[1] // LANES)], axis=1)


_SCAN_SLOTS = 3
_SCAN_NJ = _SCAN_SLOTS * RWKV_HEADS


def _rwkv_pre_kernel(r_ref, k_ref, v_ref, lo_ref, wup_ref, aup_ref, gup_ref, w0_ref, a0_ref, kk_ref, ka_ref, rk_ref,
                     op_ref, g_ref, bonus_ref):
    W = RWKV_W
    col = lambda q: slice(q * W, (q + 1) * W)
    r = r_ref[...]
    k = k_ref[...]
    v = v_ref[...]
    lora = lo_ref[...]
    kk = k * kk_ref[...]
    kkn = kk * lax.rsqrt(_seg64_sum_wide(kk * kk) + EPS)
    g_ref[...] = _dot_bf16(_sigmoid(lora[:, 2 * LANES:3 * LANES]), gup_ref[...])
    pre = w0_ref[...] + _dot_bf16(jnp.tanh(lora[:, 0:LANES]), wup_ref[...])
    decay = jnp.exp(-jnp.exp(-_softplus(-pre) - 0.5))
    a = _sigmoid(a0_ref[...] + _dot_bf16(lora[:, LANES:2 * LANES], aup_ref[...]))
    lo = _lane_lo((1, LANES))
    bonus = None
    for d in range(2):
        a_d = a[:, col(d)]
        k_d = k * (1.0 + (a_d - 1.0) * ka_ref[...])
        t = _seg64_sum_wide(r * k_d * rk_ref[...])
        bonus = t if bonus is None else bonus + t
        for s, (x1, x2) in enumerate(((kkn, decay[:, col(d)]), (k_d, kkn * a_d), (r, v))):
            for h in range(RWKV_HEADS):
                blk = slice((h // 2) * LANES, (h // 2 + 1) * LANES)
                if h % 2 == 0:
                    out = jnp.where(lo, x1[:, blk], pltpu.roll(x2[:, blk], RWKV_HS, axis=1))
                else:
                    out = jnp.where(lo, pltpu.roll(x1[:, blk], RWKV_HS, axis=1), x2[:, blk])
                op_ref[:, d * _SCAN_NJ + s * RWKV_HEADS + h, :] = out
    bonus_ref[...] = bonus


def _rwkv_pre(p, r_blk, k_blk, v_blk, lo_blk, wup_bd, aup_bd, gup, w0, a0, k_k, k_a, r_k):
    tm = 256
    w = RWKV_W
    row = lambda n: pl.BlockSpec((1, n), lambda i: (0, 0))
    full = lambda a, b: pl.BlockSpec((a, b), lambda i: (0, 0))
    return pl.pallas_call(
        _rwkv_pre_kernel,
        grid=(N_TOK // tm,),
        in_specs=[pl.BlockSpec((tm, w), lambda i: (i, r_blk)),
                  pl.BlockSpec((tm, w), lambda i: (i, k_blk)),
                  pl.BlockSpec((tm, w), lambda i: (i, v_blk)),
                  pl.BlockSpec((tm, 3 * LANES), lambda i: (i, lo_blk)),
                  full(LANES, 2 * w), full(LANES, 2 * w), full(LANES, w),
                  row(2 * w), row(2 * w), row(w), row(w), row(w)],
        out_specs=[pl.BlockSpec((tm, 2 * _SCAN_NJ, LANES), lambda i: (i, 0, 0)),
                   pl.BlockSpec((tm, w), lambda i: (i, 0)), pl.BlockSpec((tm, w), lambda i: (i, 0))],
        out_shape=[jax.ShapeDtypeStruct((N_TOK, 2 * _SCAN_NJ, LANES), F32),
                   jax.ShapeDtypeStruct((N_TOK, w), F32), jax.ShapeDtypeStruct((N_TOK, w), F32)],
        compiler_params=_params("parallel"),
        name="rwkv_pre",
    )(p, p, p, p, wup_bd, aup_bd, gup, w0, a0, k_k, k_a, r_k)


_SCAN_CHUNK = 32
_SCAN_UNROLL = 16


def _rwkv_first_sa(s_ref, sa_ref, kk, n_k):
    nv = s_ref.shape[1]
    chunk = min(_SCAN_CHUNK, nv)
    for c0 in range(0, nv, chunk):
        def body(k, acc):
            return acc + s_ref[k, c0:c0 + chunk, :] * kk(k)
        sa_ref[c0:c0 + chunk, :] = lax.fori_loop(0, n_k, body, jnp.zeros((chunk, LANES), F32), unroll=_SCAN_UNROLL)


def _rwkv_step(s_ref, sa_ref, kk_next, w, kd, b, r, v_at, n_k):
    nv = s_ref.shape[1]
    chunk = min(_SCAN_CHUNK, nv)
    ys = []
    for c0 in range(0, nv, chunk):
        sa = sa_ref[c0:c0 + chunk, :]
        vc = v_at(c0, chunk)

        def body(k, acc):
            y_acc, sa_acc = acc
            s_new = s_ref[k, c0:c0 + chunk, :] * w(k) - sa * b(k) + vc * kd(k)
            s_ref[k, c0:c0 + chunk, :] = s_new
            return y_acc + s_new * r(k), sa_acc + s_new * kk_next(k)

        zero = jnp.zeros((chunk, LANES), F32)
        y_acc, sa_acc = lax.fori_loop(0, n_k, body, (zero, zero), unroll=_SCAN_UNROLL)
        sa_ref[c0:c0 + chunk, :] = sa_acc
        ys.append(y_acc)
    return ys[0] if len(ys) == 1 else jnp.concatenate(ys, axis=0)


_CTX_TB = 32


def _load_scan_operands(kt_ref, t, slabs):
    for s, x in enumerate(slabs):
        xt = x.T
        kt_ref[t, 2 * s] = xt[:RWKV_HS]
        kt_ref[t, 2 * s + 1] = xt[RWKV_HS:]


def _rwkv_scan_ctx_kernel(x_ref, y_ref, st_ref, s_ref, sa_ref, kt_ref, ys_ref):
    d = pl.program_id(0)
    tb = pl.program_id(1)
    nh = RWKV_HEADS
    step_t = lambda i: jnp.where(d == 0, i, _CTX_TB - 1 - i)

    @pl.when(tb == 0)
    def _():
        s_ref[...] = jnp.zeros_like(s_ref)

    def load_t(t, carry):
        _load_scan_operands(kt_ref, t, [
            jnp.concatenate([x_ref[b, t, s * nh:(s + 1) * nh, :] for b in range(BATCH)], axis=0)
            for s in range(_SCAN_SLOTS)])
        return carry

    lax.fori_loop(0, _CTX_TB, load_t, 0, unroll=2)

    t0 = step_t(0)
    _rwkv_first_sa(s_ref, sa_ref, lambda k: kt_ref[t0, 0, pl.ds(k, 1), :], RWKV_HS)

    def step(i, carry):
        t = step_t(i)
        tn = step_t(jnp.minimum(i + 1, _CTX_TB - 1))
        row = lambda q, tt: (lambda k: kt_ref[tt, q, pl.ds(k, 1), :])
        ys_ref[t] = _rwkv_step(s_ref, sa_ref, row(0, tn), row(1, t), row(2, t), row(3, t), row(4, t),
                               lambda c0, n: kt_ref[t, 5, pl.ds(c0, n), :], RWKV_HS)
        return carry

    lax.fori_loop(0, _CTX_TB, step, 0)

    def store_t(i, carry):
        z = jnp.concatenate([ys_ref[2 * i], ys_ref[2 * i + 1]], axis=0).T
        z_odd = pltpu.roll(z, RWKV_HS, axis=1)
        for b in range(BATCH):
            y_ref[b, 2 * i] = z[b * nh:(b + 1) * nh, :RWKV_HS]
            y_ref[b, 2 * i + 1] = z_odd[b * nh:(b + 1) * nh, :RWKV_HS]
        return carry

    lax.fori_loop(0, _CTX_TB // 2, store_t, 0, unroll=2)

    @pl.when(tb == pl.num_programs(1) - 1)
    def _():
        st_ref[0] = s_ref[...]


def _rwkv_scan_ctx(op4):
    nt = SEQ // _CTX_TB
    hs = RWKV_HS
    tblk = lambda d, tb: jnp.where(d == 0, tb, nt - 1 - tb)
    return pl.pallas_call(
        _rwkv_scan_ctx_kernel,
        grid=(2, nt),
        in_specs=[pl.BlockSpec((BATCH, _CTX_TB, _SCAN_NJ, LANES), lambda d, tb: (0, tblk(d, tb), d, 0))],
        out_specs=[pl.BlockSpec((BATCH, _CTX_TB, RWKV_HEADS, hs), lambda d, tb: (0, tblk(d, tb), d, 0)),
                   pl.BlockSpec((1, hs, hs, LANES), lambda d, tb: (d, 0, 0, 0))],
        out_shape=[jax.ShapeDtypeStruct((BATCH, SEQ, 2 * RWKV_HEADS, hs), F32),
                   jax.ShapeDtypeStruct((2, hs, hs, LANES), F32)],
        scratch_shapes=[pltpu.VMEM((hs, hs, LANES), F32), pltpu.VMEM((hs, LANES), F32),
                        pltpu.VMEM((_CTX_TB, 2 * _SCAN_SLOTS, hs, LANES), F32),
                        pltpu.VMEM((_CTX_TB, hs, LANES), F32)],
        compiler_params=_params("parallel", "arbitrary"),
        name="rwkv_scan_ctx",
    )(op4)


_LAT_TB = 32
_LAT_VSPLIT = 4
_LAT_STATES = 2 * DEC_BATCH * RWKV_HEADS
_LAT_VROWS = RWKV_HS // _LAT_VSPLIT


def _rwkv_scan_lat_kernel(xf0_ref, xf1_ref, xb0_ref, xb1_ref, s0_ref, yf_ref, yb_ref,
                          s_ref, sa_ref, kt_ref, v_ref, ys_ref):
    @pl.when(pl.program_id(0) == 0)
    def _():
        s_ref[...] = s0_ref[...]

    group = lax.broadcasted_iota(jnp.int32, (_LAT_VROWS, LANES), 1) // _LAT_STATES
    nh = RWKV_HEADS

    def load_t(t, carry):
        tr = _LAT_TB - 1 - t
        for s in range(_SCAN_SLOTS):
            heads = slice(s * nh, (s + 1) * nh)
            x = jnp.concatenate([xf0_ref[0, t, heads, :], xf1_ref[0, t, heads, :],
                                 xb0_ref[0, tr, heads, :], xb1_ref[0, tr, heads, :]], axis=0)
            xt = jnp.concatenate([x] * _LAT_VSPLIT, axis=0).T
            kt_ref[t, 2 * s] = xt[:RWKV_HS]
            if s < _SCAN_SLOTS - 1:
                kt_ref[t, 2 * s + 1] = xt[RWKV_HS:]
            else:
                v = jnp.zeros((_LAT_VROWS, LANES), F32)
                for g in range(_LAT_VSPLIT):
                    r0 = RWKV_HS + g * _LAT_VROWS
                    v = jnp.where(group == g, xt[r0:r0 + _LAT_VROWS, :], v)
                v_ref[t] = v
        return carry

    lax.fori_loop(0, _LAT_TB, load_t, 0, unroll=4)

    _rwkv_first_sa(s_ref, sa_ref, lambda k: kt_ref[0, 0, pl.ds(k, 1), :], RWKV_HS)

    def step(t, carry):
        tn = jnp.minimum(t + 1, _LAT_TB - 1)
        row = lambda q, tt: (lambda k: kt_ref[tt, q, pl.ds(k, 1), :])
        ys_ref[t] = _rwkv_step(s_ref, sa_ref, row(0, tn), row(1, t), row(2, t), row(3, t), row(4, t),
                               lambda c0, n: v_ref[t, pl.ds(c0, n), :], RWKV_HS)
        return carry

    lax.fori_loop(0, _LAT_TB, step, 0)

    def store_t(i, carry):
        rows = [jnp.where(group == g, ys_ref[2 * i + j], 0.0) for j in range(2) for g in range(_LAT_VSPLIT)]
        z = jnp.concatenate(rows, axis=0).T
        y = (z[0:_LAT_STATES] + z[_LAT_STATES:2 * _LAT_STATES]
             + z[2 * _LAT_STATES:3 * _LAT_STATES] + z[3 * _LAT_STATES:4 * _LAT_STATES])
        y_odd = pltpu.roll(y, RWKV_HS, axis=1)
        half = _LAT_STATES // 2
        for b in range(DEC_BATCH):
            rows_f = slice(b * nh, (b + 1) * nh)
            rows_b = slice(half + b * nh, half + (b + 1) * nh)
            yf_ref[b, 2 * i] = y[rows_f, :RWKV_HS]
            yf_ref[b, 2 * i + 1] = y_odd[rows_f, :RWKV_HS]
            yb_ref[b, _LAT_TB - 1 - 2 * i] = y[rows_b, :RWKV_HS]
            yb_ref[b, _LAT_TB - 2 - 2 * i] = y_odd[rows_b, :RWKV_HS]
        return carry

    lax.fori_loop(0, _LAT_TB // 2, store_t, 0, unroll=4)


def _rwkv_scan_lat(op4, s0):
    hs = RWKV_HS
    nv = _LAT_VROWS
    nt = DEC_SEQ // _LAT_TB
    per_seq = SEQ // _LAT_TB
    first = N_CTX // SEQ

    def x_spec(b, d):
        tblk = (lambda tb: tb) if d == 0 else (lambda tb: nt - 1 - tb)
        return pl.BlockSpec((1, _LAT_TB, _SCAN_NJ, LANES),
                            lambda tb: (first + b * (DEC_SEQ // SEQ) + tblk(tb) // per_seq, tblk(tb) % per_seq, d, 0))

    y_shape = jax.ShapeDtypeStruct((DEC_BATCH, DEC_SEQ, RWKV_HEADS, hs), F32)
    y_blk = (DEC_BATCH, _LAT_TB, RWKV_HEADS, hs)
    return pl.pallas_call(
        _rwkv_scan_lat_kernel,
        grid=(nt,),
        in_specs=[x_spec(0, 0), x_spec(1, 0), x_spec(0, 1), x_spec(1, 1),
                  pl.BlockSpec((hs, nv, LANES), lambda tb: (0, 0, 0))],
        out_specs=[pl.BlockSpec(y_blk, lambda tb: (0, tb, 0, 0)),
                   pl.BlockSpec(y_blk, lambda tb: (0, nt - 1 - tb, 0, 0))],
        out_shape=[y_shape, y_shape],
        scratch_shapes=[pltpu.VMEM((hs, nv, LANES), F32), pltpu.VMEM((nv, LANES), F32),
                        pltpu.VMEM((_LAT_TB, 2 * _SCAN_SLOTS - 1, hs, LANES), F32),
                        pltpu.VMEM((_LAT_TB, nv, LANES), F32), pltpu.VMEM((_LAT_TB, nv, LANES), F32)],
        compiler_params=_params("arbitrary"),
        name="rwkv_scan_lat",
    )(op4, op4, op4, op4, s0)


def _rwkv_post_kernel(yc_ref, ylf_ref, ylb_ref, bonus_ref, v_ref, g_ref, gn_ref, o_ref):
    def finish(head_sum):
        y = jnp.concatenate([head_sum(h) for h in range(RWKV_HEADS)], axis=1)
        y = y * lax.rsqrt(_seg64_sum_wide(y * y) * (1.0 / RWKV_HS) + EPS) * gn_ref[...]
        o_ref[...] = ((y + bonus_ref[...] * v_ref[...]) * g_ref[...]).astype(o_ref.dtype)

    @pl.when(pl.program_id(0) < N_CTX // SEQ)
    def _():
        finish(lambda h: yc_ref[0, :, h, :] + yc_ref[0, :, RWKV_HEADS + h, :])

    @pl.when(pl.program_id(0) >= N_CTX // SEQ)
    def _():
        finish(lambda h: ylf_ref[0, :, h, :] + ylb_ref[0, :, h, :])


def _rwkv_post(y_ctx, y_lat_f, y_lat_b, bonus, p, v_blk, g, gn):
    tm = SEQ
    w = RWKV_W
    hs = RWKV_HS
    n_ctx = N_CTX // SEQ
    lat = lambda y: y.reshape(N_LAT // SEQ, SEQ, RWKV_HEADS, hs)
    spec = pl.BlockSpec((tm, w), lambda i: (i, 0))
    lat_spec = pl.BlockSpec((1, SEQ, RWKV_HEADS, hs), lambda i: (jnp.maximum(i - n_ctx, 0), 0, 0, 0))
    return pl.pallas_call(
        _rwkv_post_kernel,
        grid=(N_TOK // tm,),
        in_specs=[pl.BlockSpec((1, SEQ, 2 * RWKV_HEADS, hs), lambda i: (jnp.minimum(i, n_ctx - 1), 0, 0, 0)),
                  lat_spec, lat_spec, spec, pl.BlockSpec((tm, w), lambda i: (i, v_blk)), spec,
                  pl.BlockSpec((1, w), lambda i: (0, 0))],
        out_specs=spec,
        out_shape=jax.ShapeDtypeStruct((N_TOK, w), BF16),
        compiler_params=_params("parallel"),
        name="rwkv_post",
    )(y_ctx, lat(y_lat_f), lat(y_lat_b), bonus, p, g, gn)


def _value_split_layout(x):
    lead = x.shape[:-2]
    n = len(lead)
    x = x.reshape(lead + (_LAT_STATES, _LAT_VSPLIT, _LAT_VROWS))
    return jnp.transpose(x, tuple(range(n)) + (n + 2, n + 1, n)).reshape(lead + (_LAT_VROWS, LANES))


def _even_layer(x, mod, g_mix, w_in, q_norm, kv_norm, w_uq, w_ukv, qn, kn, ret_decay, ret_gn,
                cache_ckv, cache_krope, state_ret, tabs_m):
    cq, ckv, krope, rq, rk, rv, rg = jnp.split(w_in, np.cumsum(
        (MLA_Q_RANK, MLA_KV_RANK, MLA_ROPE, RET_HEADS * RET_DK, RET_HEADS * RET_DK, RET_HEADS * RET_DV))[:].tolist(),
        axis=1)
    z = lambda n: jnp.zeros((D_MODEL, n), F32)
    w_p = jnp.concatenate([cq, ckv, z(MLA_NOPE), krope, z(LANES - MLA_QK), rq, rk, rv, rg], axis=1)
    p = _inproj(x, g_mix, mod, w_p, jnp.zeros((1, w_p.shape[1]), F32), 512, None)
    CKV_BLK, KR_BLK, RQ_BLK, RK_BLK, RV_BLK, RG_BLK = 2, 3, 4, 6, 8, 12

    def head_pad(w, n_head, d_head, c0, c1):
        w = w.reshape(w.shape[0], n_head, d_head)[:, :, c0:c1]
        return jnp.pad(w, ((0, 0), (0, 0), (0, LANES - (c1 - c0)))).reshape(w.shape[0], n_head * LANES)

    w_uq_p = head_pad(w_uq, MLA_HEADS, MLA_QK, 0, MLA_QK)
    wk_p = head_pad(w_ukv, MLA_HEADS, MLA_NOPE + MLA_V, 0, MLA_NOPE)
    wv_p = head_pad(w_ukv, MLA_HEADS, MLA_NOPE + MLA_V, MLA_NOPE, MLA_NOPE + MLA_V)
    qn_p = jnp.pad(qn, (0, LANES - MLA_QK))[None]
    kn_p = jnp.pad(kn, (0, LANES - MLA_QK))[None]

    q = _mla_q(p, q_norm[None], w_uq_p, qn_p, tabs_m)
    k, v, ckvn = _mla_kv(p, CKV_BLK, p, KR_BLK, kv_norm[None], wk_p, wv_p, kn_p, tabs_m, N_TOK, True)

    n_c = DEC_BATCH * PAST_LEN
    kr_c = jnp.pad(cache_krope.reshape(n_c, MLA_ROPE), ((0, 0), (MLA_NOPE, LANES - MLA_QK)))
    k_c, v_c, _ = _mla_kv(cache_ckv.reshape(n_c, MLA_KV_RANK), 0, kr_c, 0, kv_norm[None], wk_p, wv_p, kn_p,
                          tabs_m, n_c, False)

    hw = MLA_HEADS * LANES

    def with_cache(own, cache):
        own = own[N_CTX:].reshape(DEC_BATCH, DEC_SEQ, hw)
        return jnp.concatenate([own, cache.reshape(DEC_BATCH, PAST_LEN, hw)], 1).reshape(-1, hw)

    o_ctx = _mla_attn(q, k, v, BATCH, SEQ, SEQ, 0, SEQ)
    o_lat = _mla_attn(q, with_cache(k, k_c), with_cache(v, v_c), DEC_BATCH, DEC_SEQ, DEC_SEQ + PAST_LEN, N_CTX, 256)

    log_g = -_softplus(-ret_decay)
    gn = ret_gn[None]
    s0_ctx = jnp.zeros((BATCH, 2, RET_HEADS * RET_DK, RET_DV), F32)
    r_ctx, st_ctx = _retention(log_g, p, RQ_BLK, RK_BLK, RV_BLK, RG_BLK, s0_ctx, gn, BATCH, SEQ, 0, SEQ)
    s0_lat = state_ret.reshape(DEC_BATCH, 2, RET_HEADS * RET_DK, RET_DV)
    r_lat, _ = _retention(log_g, p, RQ_BLK, RK_BLK, RV_BLK, RG_BLK, s0_lat, gn, DEC_BATCH, DEC_SEQ, N_CTX, 256)

    mix = [(o_ctx, o_lat), (r_ctx, r_lat)]
    new_ckv = ckvn[:N_CTX].reshape(BATCH, SEQ, MLA_KV_RANK)
    new_krope = p[:N_CTX, KR_BLK * LANES + MLA_NOPE:KR_BLK * LANES + MLA_QK]
    new_krope = new_krope.reshape(BATCH, SEQ, MLA_ROPE)
    new_ret = st_ctx.reshape(BATCH, 2, RET_HEADS, RET_DK, RET_DV)
    return mix, new_ckv, new_krope, new_ret


def _odd_layer(x, mod, g_mix, w_in, qn, kn, lam, diff_gn, mu, w0, w_up, a0, a_up, g_up, k_k, k_a, r_k, gn,
               cache_k, cache_v, state_rwkv, tabs_d, lam_init):
    w_p = w_in
    n_in = w_p.shape[1]
    mu_full = jnp.concatenate([jnp.zeros((3 * DIFF_W,), F32), mu])[None]
    p = _inproj(x, g_mix, mod, w_p, mu_full, 384, (3 * DIFF_W) // 384)
    DV_BLK, R_BLK, K_BLK, V_BLK = 2, 3, 4, 5
    LO_BLK = (6 * RWKV_W) // (3 * LANES)

    qn_p = jnp.tile(qn, 2)[None]
    kn_p = jnp.tile(kn, 2)[None]
    q, k = _diff_qk(p, qn_p, kn_p, tabs_d)

    n_c = DEC_BATCH * PAST_LEN
    k_c = cache_k.reshape(DEC_BATCH, PAST_LEN, DIFF_W)
    v_c = cache_v.reshape(DEC_BATCH, PAST_LEN, DIFF_W)

    def with_cache(own_lat, cache):
        return jnp.concatenate([own_lat.reshape(DEC_BATCH, DEC_SEQ, DIFF_W), cache], 1).reshape(-1, DIFF_W)

    dgn = diff_gn[None]
    dv_cols = slice(DV_BLK * DIFF_W, (DV_BLK + 1) * DIFF_W)
    o_ctx = _diff_attn(lam, q, k, p, DV_BLK * DIFF_HEADS, dgn, BATCH, SEQ, SEQ, 0, SEQ, lam_init)
    o_lat = _diff_attn(lam, q, with_cache(k[N_CTX:], k_c), with_cache(p[N_CTX:, dv_cols], v_c), 0, dgn,
                       DEC_BATCH, DEC_SEQ, DEC_SEQ + PAST_LEN, N_CTX, 256, lam_init)

    zero = jnp.zeros((RWKV_W_LORA, RWKV_W), F32)
    wup_bd = jnp.concatenate([jnp.concatenate([w_up[0], zero], 1), jnp.concatenate([zero, w_up[1]], 1)], 0)
    aup_bd = jnp.concatenate([jnp.concatenate([a_up[0], zero], 1), jnp.concatenate([zero, a_up[1]], 1)], 0)
    op, g, bonus = _rwkv_pre(p, R_BLK, K_BLK, V_BLK, LO_BLK, wup_bd, aup_bd, g_up, w0.reshape(1, -1),
                             a0.reshape(1, -1), k_k[None], k_a[None], r_k.reshape(1, -1))
    op4 = op.reshape(N_TOK // SEQ, SEQ, 2 * _SCAN_NJ, LANES)
    y_ctx, st_ctx = _rwkv_scan_ctx(op4)
    s0_lat = jnp.transpose(state_rwkv, (4, 1, 0, 2, 3)).reshape(RWKV_HS, _LAT_STATES, RWKV_HS)
    y_lat_f, y_lat_b = _rwkv_scan_lat(op4, _value_split_layout(s0_lat))
    rw_o = _rwkv_post(y_ctx, y_lat_f, y_lat_b, bonus, p, V_BLK, g, gn[None])

    mix = [(o_ctx, o_lat), rw_o]
    new_dk = k[:N_CTX].reshape(BATCH, SEQ, DIFF_HEADS, 2, DIFF_DH)
    new_dv = p[:N_CTX, dv_cols].reshape(BATCH, SEQ, DIFF_HEADS, 2 * DIFF_DH)
    new_rwkv = jnp.transpose(st_ctx.reshape(2, RWKV_HS, RWKV_HS, BATCH, RWKV_HEADS), (3, 0, 4, 2, 1))
    return mix, new_dk, new_dv, new_rwkv


def kernel(x_prompt, x_sample, cache_mla_ckv, cache_mla_krope, state_ret, cache_diff_k, cache_diff_v, state_rwkv,
           c, c_ctx, ada_w, ada_b, norm_mix_g, norm_ffn_g, w_out, ffn_up, ffn_conv_w, ffn_conv_b, ffn_down,
           a_w_in, mla_q_norm, mla_kv_norm, mla_w_uq, mla_w_ukv, mla_qn, mla_kn, ret_decay, ret_gn,
           b_w_in, diff_qn, diff_kn, diff_lam, diff_gn, rwkv_mu, rwkv_w0, rwkv_w_up, rwkv_a0, rwkv_a_up,
           rwkv_g_up, rwkv_k_k, rwkv_k_a, rwkv_r_k, rwkv_gn):
    x = jnp.concatenate([x_prompt.reshape(N_CTX, D_MODEL), x_sample.reshape(N_LAT, D_MODEL)], 0)
    cond8 = jnp.pad(jnp.concatenate([c_ctx[None], c], 0), ((0, 8 - N_GROUPS), (0, 0)))
    mod = _modulation(cond8, ada_w, ada_b)

    tabs_m = _rope_tables(MLA_ROPE, (MLA_NOPE,))
    tabs_d = _rope_tables(DIFF_DH, (0, DIFF_DH))

    outs = {}
    for l in range(DEPTH):
        j = l // 2
        g_mix = norm_mix_g[l][None]
        if l % 2 == 0:
            mix, outs["ckv"], outs["krope"], outs["ret"] = _even_layer(
                x, mod[l], g_mix, a_w_in[j], mla_q_norm[j], mla_kv_norm[j], mla_w_uq[j], mla_w_ukv[j], mla_qn[j],
                mla_kn[j], ret_decay[j], ret_gn[j], cache_mla_ckv[:, j], cache_mla_krope[:, j], state_ret[:, j],
                tabs_m)
        else:
            lam_init = 0.8 - 0.6 * math.exp(-0.3 * l)
            mix, outs["dk"], outs["dv"], outs["rwkv"] = _odd_layer(
                x, mod[l], g_mix, b_w_in[j], diff_qn[j], diff_kn[j], diff_lam[j], diff_gn[j], rwkv_mu[j],
                rwkv_w0[j], rwkv_w_up[j], rwkv_a0[j], rwkv_a_up[j], rwkv_g_up[j], rwkv_k_k[j], rwkv_k_a[j],
                rwkv_r_k[j], rwkv_gn[j], cache_diff_k[:, j], cache_diff_v[:, j], state_rwkv[:, j], tabs_d, lam_init)
        x = _resid_proj(mix, w_out[l], x, mod[l], 2)
        act = _ffn_up(x, norm_ffn_g[l][None], mod[l], ffn_up[l], ffn_conv_w[l], ffn_conv_b[l])
        x = _resid_proj([act], ffn_down[l], x, mod[l], 5)

    y_prompt = x[:N_CTX].reshape(BATCH, SEQ, D_MODEL)
    y_sample = x[N_CTX:].reshape(DEC_BATCH, DEC_SEQ, D_MODEL)
    return (y_prompt, y_sample, outs["ckv"][:, None], outs["krope"][:, None], outs["ret"][:, None],
            outs["dk"][:, None], outs["dv"][:, None], outs["rwkv"][:, None])
```

```python
import functools
import math

import numpy as np
import jax
import jax.numpy as jnp
from jax import lax
from jax.experimental import pallas as pl
from jax.experimental.pallas import tpu as pltpu

D_MODEL = 1024
BATCH = 16
SEQ = 256
DEPTH = 2
DEC_BATCH = 2
DEC_SEQ = 1024
PAST_LEN = 512
GRID_W = 64
EPS = 1e-6
ROPE_BASE = 10000.0

MLA_HEADS = 8
MLA_Q_RANK = 256
MLA_KV_RANK = 128
MLA_NOPE = 64
MLA_ROPE = 32
MLA_V = 64
MLA_QK = MLA_NOPE + MLA_ROPE
RET_HEADS = 4
RET_DK = 64
RET_DV = 128
DIFF_HEADS = 4
DIFF_DH = 64
DIFF_W = DIFF_HEADS * 2 * DIFF_DH
RWKV_HEADS = 8
RWKV_HS = 64
RWKV_W = RWKV_HEADS * RWKV_HS
RWKV_W_LORA = 64
RWKV_A_LORA = 64
RWKV_G_LORA = 128
D_FF = 2816

N_CTX = BATCH * SEQ
N_LAT = DEC_BATCH * DEC_SEQ
N_TOK = N_CTX + N_LAT
N_GROUPS = 1 + DEC_BATCH

LANES = 128
VMEM_LIMIT = 56 * 1024 * 1024

_PREC = lax.Precision.HIGHEST
F32 = jnp.float32


def _dot_tn(a, b):
    return lax.dot_general(a, b, (((0,), (0,)), ((), ())), precision=_PREC, preferred_element_type=F32)


BF16 = jnp.bfloat16


def _dot_bf16(a, b):
    return jnp.dot(a.astype(BF16), b.astype(BF16), preferred_element_type=F32)


def _dot_nt_bf16(a, b):
    return lax.dot_general(a.astype(BF16), b.astype(BF16), (((1,), (1,)), ((), ())), preferred_element_type=F32)


def _params(*sem):
    return pltpu.CompilerParams(dimension_semantics=sem, vmem_limit_bytes=VMEM_LIMIT)


def _sigmoid(x):
    return 1.0 / (1.0 + jnp.exp(-x))


def _silu(x):
    return x * _sigmoid(x)


def _softplus(x):
    return jnp.maximum(x, 0.0) + jnp.log(1.0 + jnp.exp(-jnp.abs(x)))


def _rms(x, n):
    return x * lax.rsqrt(jnp.sum(x * x, axis=-1, keepdims=True) * (1.0 / n) + EPS)


def _lane_lo(shape):
    return lax.broadcasted_iota(jnp.int32, shape, len(shape) - 1) < 64


def _seg64_sum(x):
    lo = _lane_lo(x.shape)
    s_lo = jnp.sum(jnp.where(lo, x, 0.0), axis=-1, keepdims=True)
    s_hi = jnp.sum(jnp.where(lo, 0.0, x), axis=-1, keepdims=True)
    return jnp.where(lo, s_lo, s_hi)


_SUBLANES = 8


def _seq_neighbours(p, tile, tile_rows):
    is_ctx = tile * tile_rows < N_CTX
    sub = lax.broadcasted_iota(jnp.int32, (_SUBLANES, 1), 0)

    def shifted(rolled, edge_sublane, group_of_seq):
        pieces, start = [], 0
        for q in range(tile_rows // SEQ):
            g0 = q * SEQ + group_of_seq
            outer = (q == 0) if group_of_seq == 0 else (q == tile_rows // SEQ - 1)
            edge = (sub == edge_sublane) if outer else ((sub == edge_sublane) & is_ctx)
            pieces += [rolled[start:g0], jnp.where(edge, 0.0, rolled[g0:g0 + _SUBLANES])]
            start = g0 + _SUBLANES
        pieces.append(rolled[start:])
        return jnp.concatenate([x for x in pieces if x.shape[0]], axis=0)

    prev = shifted(pltpu.roll(p, 1, axis=0), 0, 0)
    nxt = shifted(pltpu.roll(p, tile_rows - 1, axis=0), _SUBLANES - 1, SEQ - _SUBLANES)
    return prev, nxt


def _group_of_tile(i, tile_rows):
    row = i * tile_rows
    return jnp.where(row < N_CTX, 0, 1 + (row - N_CTX) // DEC_SEQ)


def _modulation_kernel(c_ref, w_ref, b_ref, o_ref):
    o_ref[0] = _dot_bf16(_silu(c_ref[...]), w_ref[0]) + b_ref[0]


def _modulation(cond8, ada_w, ada_b):
    tn = 512
    n = 6 * D_MODEL
    out = pl.pallas_call(
        _modulation_kernel,
        grid=(DEPTH, n // tn),
        in_specs=[pl.BlockSpec((8, D_MODEL), lambda l, j: (0, 0)),
                  pl.BlockSpec((1, D_MODEL, tn), lambda l, j: (l, 0, j)),
                  pl.BlockSpec((1, 1, tn), lambda l, j: (l, 0, j))],
        out_specs=pl.BlockSpec((1, 8, tn), lambda l, j: (l, 0, j)),
        out_shape=jax.ShapeDtypeStruct((DEPTH, 8, n), F32),
        compiler_params=_params("parallel", "parallel"),
        name="modulation",
    )(cond8, ada_w, ada_b.reshape(DEPTH, 1, n))
    m = out[:, :N_GROUPS].reshape(DEPTH, N_GROUPS, 6, D_MODEL)
    return jnp.pad(m, ((0, 0), (0, 0), (0, 2), (0, 0)))


_TM_SEQ = 1024


def _norm_mod(x, g, mod, off):
    return _rms(x, D_MODEL) * g * (1.0 + mod[off + 1:off + 2, :]) + mod[off:off + 1, :]


def _inproj_kernel(x_ref, g_ref, mod_ref, w_ref, mu_ref, o_ref, h_ref, *, shift_from):
    i = pl.program_id(0)

    @pl.when(pl.program_id(1) == 0)
    def _():
        h_ref[...] = _norm_mod(x_ref[...], g_ref[...], mod_ref[0], 0).astype(BF16)

    p = _dot_bf16(h_ref[...], w_ref[...])
    if shift_from is None:
        o_ref[...] = p
    else:
        @pl.when(pl.program_id(1) < shift_from)
        def _():
            o_ref[...] = p

        @pl.when(pl.program_id(1) >= shift_from)
        def _():
            prev, nxt = _seq_neighbours(p, i, _TM_SEQ)
            o_ref[...] = p + (0.5 * (prev + nxt) - p) * mu_ref[...]


def _inproj(x, g, mod, w, mu, tn, shift_from):
    n = w.shape[1]
    tm = _TM_SEQ
    return pl.pallas_call(
        functools.partial(_inproj_kernel, shift_from=shift_from),
        grid=(N_TOK // tm, n // tn),
        in_specs=[pl.BlockSpec((tm, D_MODEL), lambda i, j: (i, 0)),
                  pl.BlockSpec((1, D_MODEL), lambda i, j: (0, 0)),
                  pl.BlockSpec((1, 8, D_MODEL), lambda i, j: (_group_of_tile(i, tm), 0, 0)),
                  pl.BlockSpec((D_MODEL, tn), lambda i, j: (0, j)),
                  pl.BlockSpec((1, tn), lambda i, j: (0, j))],
        out_specs=pl.BlockSpec((tm, tn), lambda i, j: (i, j)),
        out_shape=jax.ShapeDtypeStruct((N_TOK, n), F32),
        scratch_shapes=[pltpu.VMEM((tm, D_MODEL), BF16)],
        compiler_params=_params("parallel", "arbitrary"),
        name="inproj" if shift_from is None else "inproj_shift",
    )(x, g, mod, w, mu)


def _resid_kernel(*refs, gate_row, widths, split):
    n_in = sum(2 if sp else 1 for sp in split)
    a_refs = refs[:n_in]
    w_ref, x_ref, mod_ref, o_ref, a_bf_ref = refs[n_in:]
    i = pl.program_id(0)

    @pl.when(pl.program_id(1) == 0)
    def _():
        k0, r = 0, 0
        for width, sp in zip(widths, split):
            cols = slice(k0, k0 + width)
            if sp:
                ctx_ref, lat_ref = a_refs[r], a_refs[r + 1]

                @pl.when(i < N_CTX // _TM_SEQ)
                def _():
                    a_bf_ref[:, cols] = ctx_ref[...].astype(BF16)

                @pl.when(i >= N_CTX // _TM_SEQ)
                def _():
                    a_bf_ref[:, cols] = lat_ref[...].astype(BF16)
            else:
                a_bf_ref[:, cols] = a_refs[r][...].astype(BF16)
            k0 += width
            r += 2 if sp else 1

    o_ref[...] = x_ref[...] + mod_ref[0, gate_row:gate_row + 1, :] * _dot_bf16(a_bf_ref[...], w_ref[...])


def _resid_proj(acts, w, x, mod, gate_row):
    tm, tn = _TM_SEQ, 256
    n_ctx = N_CTX // tm
    split = [isinstance(a, (tuple, list)) for a in acts]
    widths = [a[0].shape[1] if sp else a.shape[1] for a, sp in zip(acts, split)]
    k = sum(widths)
    in_specs, operands = [], []
    for a, width, sp in zip(acts, widths, split):
        if sp:
            in_specs += [pl.BlockSpec((tm, width), lambda i, j: (jnp.minimum(i, n_ctx - 1), 0)),
                         pl.BlockSpec((tm, width), lambda i, j: (jnp.maximum(i - n_ctx, 0), 0))]
            operands += list(a)
        else:
            in_specs.append(pl.BlockSpec((tm, width), lambda i, j: (i, 0)))
            operands.append(a)
    return pl.pallas_call(
        functools.partial(_resid_kernel, gate_row=gate_row, widths=tuple(widths), split=tuple(split)),
        grid=(N_TOK // tm, D_MODEL // tn),
        in_specs=in_specs
        + [pl.BlockSpec((k, tn), lambda i, j: (0, j)),
           pl.BlockSpec((tm, tn), lambda i, j: (i, j)),
           pl.BlockSpec((1, 8, tn), lambda i, j: (_group_of_tile(i, tm), 0, j))],
        out_specs=pl.BlockSpec((tm, tn), lambda i, j: (i, j)),
        out_shape=jax.ShapeDtypeStruct((N_TOK, D_MODEL), F32),
        scratch_shapes=[pltpu.VMEM((tm, k), BF16)],
        compiler_params=_params("parallel", "arbitrary"),
        name="resid_proj",
    )(*operands, w, x, mod)


def _ffn_up_kernel(x_ref, g_ref, mod_ref, wa_ref, wb_ref, cwa_ref, cwb_ref, cba_ref, cbb_ref, o_ref, h_ref):
    i = pl.program_id(0)

    @pl.when(pl.program_id(1) == 0)
    def _():
        h_ref[...] = _norm_mod(x_ref[...], g_ref[...], mod_ref[0], 3).astype(BF16)

    h = h_ref[...]

    def conv(w_ref, cw_ref, cb_ref):
        u = _dot_bf16(h, w_ref[...])
        prev, nxt = _seq_neighbours(u, i, _TM_SEQ)
        return prev * cw_ref[0:1, :] + u * cw_ref[1:2, :] + nxt * cw_ref[2:3, :] + cb_ref[...]

    o_ref[...] = (_silu(conv(wa_ref, cwa_ref, cba_ref)) * conv(wb_ref, cwb_ref, cbb_ref)).astype(o_ref.dtype)


def _ffn_up(x, g, mod, up, cw, cb):
    tm, tn = _TM_SEQ, 256
    nb = D_FF // tn
    cb = cb.reshape(1, 2 * D_FF)
    return pl.pallas_call(
        _ffn_up_kernel,
        grid=(N_TOK // tm, nb),
        in_specs=[pl.BlockSpec((tm, D_MODEL), lambda i, j: (i, 0)),
                  pl.BlockSpec((1, D_MODEL), lambda i, j: (0, 0)),
                  pl.BlockSpec((1, 8, D_MODEL), lambda i, j: (_group_of_tile(i, tm), 0, 0)),
                  pl.BlockSpec((D_MODEL, tn), lambda i, j: (0, j)),
                  pl.BlockSpec((D_MODEL, tn), lambda i, j: (0, j + nb)),
                  pl.BlockSpec((3, tn), lambda i, j: (0, j)),
                  pl.BlockSpec((3, tn), lambda i, j: (0, j + nb)),
                  pl.BlockSpec((1, tn), lambda i, j: (0, j)),
                  pl.BlockSpec((1, tn), lambda i, j: (0, j + nb))],
        out_specs=pl.BlockSpec((tm, tn), lambda i, j: (i, j)),
        out_shape=jax.ShapeDtypeStruct((N_TOK, D_FF), BF16),
        scratch_shapes=[pltpu.VMEM((tm, D_MODEL), BF16)],
        compiler_params=_params("parallel", "arbitrary"),
        name="ffn_up",
    )(x, g, mod, up, up, cw, cw, cb, cb)


_ROPE_TM = 512


def _rope(y, c, s1, s2):
    return y * c + pltpu.roll(y, 1, axis=1) * s1 + pltpu.roll(y, LANES - 1, axis=1) * s2


def _rope_tables(rot_dim, lane_offsets):
    t = np.arange(DEC_SEQ)
    row, col = t // GRID_W, t % GRID_W
    n_freq = rot_dim // 4
    inv = ROPE_BASE ** (-np.arange(n_freq, dtype=np.float64) / n_freq)
    ang = np.concatenate([row[:, None] * inv, col[:, None] * inv], -1)
    cos, sin = np.cos(ang), np.sin(ang)
    n = _ROPE_TM + DEC_SEQ
    c, s1, s2 = np.ones((n, LANES)), np.zeros((n, LANES)), np.zeros((n, LANES))
    for a in lane_offsets:
        even = a + 2 * np.arange(rot_dim // 2)
        c[_ROPE_TM:, even] = cos
        c[_ROPE_TM:, even + 1] = cos
        s1[_ROPE_TM:, even + 1] = sin
        s2[_ROPE_TM:, even] = -sin
    return tuple(jnp.asarray(x, F32) for x in (c, s1, s2))


def _rope_block(i):
    row = i * _ROPE_TM
    return jnp.where(row < N_CTX, 0, 1 + ((row - N_CTX) % DEC_SEQ) // _ROPE_TM)


def _mla_q_kernel(cq_ref, qnorm_ref, w_ref, qn_ref, c_ref, s1_ref, s2_ref, o_ref):
    xn = _rms(cq_ref[...], MLA_Q_RANK) * qnorm_ref[...]
    y = _dot_bf16(xn, w_ref[...])
    c, s1, s2 = c_ref[...], s1_ref[...], s2_ref[...]
    for h in range(MLA_HEADS):
        yh = y[:, h * LANES:(h + 1) * LANES]
        yh = _rms(yh, MLA_QK) * qn_ref[...]
        o_ref[:, h * LANES:(h + 1) * LANES] = _rope(yh, c, s1, s2)


def _mla_q(p, q_norm, w_uq_p, qn_p, tabs):
    tm = _ROPE_TM
    hw = MLA_HEADS * LANES
    tab_spec = pl.BlockSpec((tm, LANES), lambda i: (_rope_block(i), 0))
    return pl.pallas_call(
        _mla_q_kernel,
        grid=(N_TOK // tm,),
        in_specs=[pl.BlockSpec((tm, MLA_Q_RANK), lambda i: (i, 0)),
                  pl.BlockSpec((1, MLA_Q_RANK), lambda i: (0, 0)),
                  pl.BlockSpec((MLA_Q_RANK, hw), lambda i: (0, 0)),
                  pl.BlockSpec((1, LANES), lambda i: (0, 0)),
                  tab_spec, tab_spec, tab_spec],
        out_specs=pl.BlockSpec((tm, hw), lambda i: (i, 0)),
        out_shape=jax.ShapeDtypeStruct((N_TOK, hw), F32),
        compiler_params=_params("parallel"),
        name="mla_q",
    )(p, q_norm, w_uq_p, qn_p, *tabs)


def _mla_kv_kernel(ckv_ref, kr_ref, kvn_ref, wk_ref, wv_ref, kn_ref, c_ref, s1_ref, s2_ref,
                   k_ref, v_ref, ckvn_ref, *, norm_ckv):
    ckv = ckv_ref[...]
    if norm_ckv:
        ckv = _rms(ckv, MLA_KV_RANK) * kvn_ref[...]
    ckvn_ref[...] = ckv
    ckv_bf = ckv.astype(BF16)
    kk = _dot_bf16(ckv_bf, wk_ref[...])
    v_ref[...] = _dot_bf16(ckv_bf, wv_ref[...])
    kr = kr_ref[...]
    c, s1, s2 = c_ref[...], s1_ref[...], s2_ref[...]
    for h in range(MLA_HEADS):
        kh = kk[:, h * LANES:(h + 1) * LANES] + kr
        kh = _rms(kh, MLA_QK) * kn_ref[...]
        k_ref[:, h * LANES:(h + 1) * LANES] = _rope(kh, c, s1, s2)


def _mla_kv(ckv_src, ckv_blk, kr_src, kr_blk, kv_norm, wk_p, wv_p, kn_p, tabs, n_rows, own_tokens):
    tm = _ROPE_TM
    hw = MLA_HEADS * LANES
    tab_spec = pl.BlockSpec((tm, LANES), (lambda i: (_rope_block(i), 0)) if own_tokens else (lambda i: (0, 0)))
    return pl.pallas_call(
        functools.partial(_mla_kv_kernel, norm_ckv=own_tokens),
        grid=(n_rows // tm,),
        in_specs=[pl.BlockSpec((tm, LANES), lambda i: (i, ckv_blk)),
                  pl.BlockSpec((tm, LANES), lambda i: (i, kr_blk)),
                  pl.BlockSpec((1, LANES), lambda i: (0, 0)),
                  pl.BlockSpec((MLA_KV_RANK, hw), lambda i: (0, 0)),
                  pl.BlockSpec((MLA_KV_RANK, hw), lambda i: (0, 0)),
                  pl.BlockSpec((1, LANES), lambda i: (0, 0)),
                  tab_spec, tab_spec, tab_spec],
        out_specs=[pl.BlockSpec((tm, hw), lambda i: (i, 0)),
                   pl.BlockSpec((tm, hw), lambda i: (i, 0)),
                   pl.BlockSpec((tm, LANES), lambda i: (i, 0))],
        out_shape=[jax.ShapeDtypeStruct((n_rows, hw), F32),
                   jax.ShapeDtypeStruct((n_rows, hw), F32),
                   jax.ShapeDtypeStruct((n_rows, LANES), F32)],
        compiler_params=_params("parallel"),
        name="mla_kv",
    )(ckv_src, kr_src, kv_norm, wk_p, wv_p, kn_p, *tabs)


_LOG2E = math.log2(math.e)


def _softmax_rows(s, scale):
    p = jnp.exp2((s - jnp.max(s, axis=-1, keepdims=True)) * (scale * _LOG2E))
    return p, jnp.sum(p, axis=-1, keepdims=True)


def _mla_attn_kernel(q_ref, k_ref, v_ref, o_ref):
    scale = MLA_QK ** -0.5
    outs = []
    for h in range(2):
        sl = slice(h * LANES, (h + 1) * LANES)
        p, l = _softmax_rows(_dot_nt_bf16(q_ref[:, sl], k_ref[:, sl]), scale)
        outs.append(_dot_bf16(p, v_ref[:, sl]) / l)
    o_ref[...] = (outs[0] + pltpu.roll(outs[1], MLA_V, axis=1)).astype(o_ref.dtype)


def _mla_attn(q, k, v, batch, nq, nk, q_row0, tq):
    nqb = nq // tq
    qb0 = q_row0 // tq
    return pl.pallas_call(
        _mla_attn_kernel,
        grid=(batch, MLA_HEADS // 2, nqb),
        in_specs=[pl.BlockSpec((tq, 2 * LANES), lambda b, h, i: (qb0 + b * nqb + i, h)),
                  pl.BlockSpec((nk, 2 * LANES), lambda b, h, i: (b, h)),
                  pl.BlockSpec((nk, 2 * LANES), lambda b, h, i: (b, h))],
        out_specs=pl.BlockSpec((tq, LANES), lambda b, h, i: (b * nqb + i, h)),
        out_shape=jax.ShapeDtypeStruct((batch * nq, MLA_HEADS * MLA_V), BF16),
        compiler_params=_params("parallel", "parallel", "arbitrary"),
        name="mla_attn",
    )(q, k, v)


def _ret_kernel(lg_ref, q_ref, k_ref, v_ref, rg_ref, s0_ref, gn_ref, *out_and_scratch, n, tq, want_state):
    if want_state:
        o_ref, st_ref, decay_ref = out_and_scratch
    else:
        o_ref, decay_ref = out_and_scratch
    pair, qi, b = pl.program_id(0), pl.program_id(1), pl.program_id(2)
    q = q_ref[...]
    k = k_ref[...] * (RET_DK ** -0.5)
    lo = _lane_lo((1, LANES))
    row = (qi * tq + lax.broadcasted_iota(jnp.int32, (tq, 1), 0)).astype(F32)

    @pl.when(b == 0)
    def _():
        col = lax.broadcasted_iota(jnp.int32, (1, n), 1).astype(F32)
        diff = row - col
        for h in range(2):
            lgf = lg_ref[0, 2 * pair + h]
            lgb = lg_ref[1, 2 * pair + h]
            decay_ref[h] = (jnp.where(diff >= 0, jnp.exp(lgf * jnp.maximum(diff, 0.0)), 0.0)
                            + jnp.where(diff <= 0, jnp.exp(lgb * jnp.maximum(-diff, 0.0)), 0.0))

    for h in range(2):
        lgf = lg_ref[0, 2 * pair + h]
        lgb = lg_ref[1, 2 * pair + h]
        mask = lo if h == 0 else jnp.logical_not(lo)
        qh = jnp.where(mask, q, 0.0)
        vh = v_ref[:, h * LANES:(h + 1) * LANES]
        o = _dot_bf16(_dot_nt_bf16(qh, k) * decay_ref[h], vh)
        o = o + _dot_bf16(qh * jnp.exp(lgf * (row + 1.0)), s0_ref[0, 0])
        o = o + _dot_bf16(qh * jnp.exp(lgb * (n - row)), s0_ref[0, 1])
        y = _rms(o, RET_DV) * gn_ref[:, h * LANES:(h + 1) * LANES]
        o_ref[:, h * LANES:(h + 1) * LANES] = (_silu(rg_ref[:, h * LANES:(h + 1) * LANES]) * y).astype(o_ref.dtype)

    if want_state:
        pos = lax.broadcasted_iota(jnp.int32, (n, 1), 0).astype(F32)
        for d in range(2):
            acc = None
            for h in range(2):
                lg = lg_ref[d, 2 * pair + h]
                mask = lo if h == 0 else jnp.logical_not(lo)
                expo = (n - 1.0 - pos) if d == 0 else pos
                kd = jnp.where(mask, k * jnp.exp(lg * expo), 0.0)
                term = _dot_tn(kd, v_ref[:, h * LANES:(h + 1) * LANES])
                acc = term if acc is None else acc + term
            lg_rows = jnp.where(lax.broadcasted_iota(jnp.int32, (LANES, 1), 0) < 64,
                                lg_ref[d, 2 * pair], lg_ref[d, 2 * pair + 1])
            st_ref[0, d] = acc + s0_ref[0, d] * jnp.exp(lg_rows * n)


def _retention(log_g, p, q_blk, k_blk, v_blk, g_blk, s0, gn, batch, n, row0, tq, want_state):
    nqb = n // tq
    assert not want_state or nqb == 1
    qb0 = row0 // tq
    kb0 = row0 // n
    pairs = RET_HEADS // 2
    out_specs = [pl.BlockSpec((tq, 2 * LANES), lambda h, i, b: (b * nqb + i, h))]
    out_shape = [jax.ShapeDtypeStruct((batch * n, RET_HEADS * RET_DV), BF16)]
    if want_state:
        out_specs.append(pl.BlockSpec((1, 2, LANES, LANES), lambda h, i, b: (b, 0, h, 0)))
        out_shape.append(jax.ShapeDtypeStruct((batch, 2, RET_HEADS * RET_DK, RET_DV), F32))
    outs = pl.pallas_call(
        functools.partial(_ret_kernel, n=n, tq=tq, want_state=want_state),
        grid=(pairs, nqb, batch),
        in_specs=[pl.BlockSpec(memory_space=pltpu.SMEM),
                  pl.BlockSpec((tq, LANES), lambda h, i, b: (qb0 + b * nqb + i, q_blk + h)),
                  pl.BlockSpec((n, LANES), lambda h, i, b: (kb0 + b, k_blk + h)),
                  pl.BlockSpec((n, 2 * LANES), lambda h, i, b: (kb0 + b, v_blk // 2 + h)),
                  pl.BlockSpec((tq, 2 * LANES), lambda h, i, b: (qb0 + b * nqb + i, g_blk // 2 + h)),
                  pl.BlockSpec((1, 2, LANES, LANES), lambda h, i, b: (b, 0, h, 0)),
                  pl.BlockSpec((1, 2 * LANES), lambda h, i, b: (0, h))],
        out_specs=out_specs,
        out_shape=out_shape,
        scratch_shapes=[pltpu.VMEM((2, tq, n), F32)],
        compiler_params=_params("parallel", "parallel", "arbitrary"),
        name="retention",
    )(log_g, p, p, p, p, s0, gn)
    return outs if want_state else (outs[0], None)


def _diff_qk_kernel(q_ref, k_ref, qn_ref, kn_ref, c_ref, s1_ref, s2_ref, qo_ref, ko_ref):
    c, s1, s2 = c_ref[...], s1_ref[...], s2_ref[...]
    for src, gain, dst in ((q_ref, qn_ref, qo_ref), (k_ref, kn_ref, ko_ref)):
        for h in range(DIFF_HEADS):
            sl = slice(h * LANES, (h + 1) * LANES)
            y = src[:, sl]
            y = y * lax.rsqrt(_seg64_sum(y * y) * (1.0 / DIFF_DH) + EPS) * gain[...]
            dst[:, sl] = _rope(y, c, s1, s2)


def _diff_qk(p, qn_p, kn_p, tabs):
    tm = _ROPE_TM
    tab_spec = pl.BlockSpec((tm, LANES), lambda i: (_rope_block(i), 0))
    return pl.pallas_call(
        _diff_qk_kernel,
        grid=(N_TOK // tm,),
        in_specs=[pl.BlockSpec((tm, DIFF_W), lambda i: (i, 0)),
                  pl.BlockSpec((tm, DIFF_W), lambda i: (i, 1)),
                  pl.BlockSpec((1, LANES), lambda i: (0, 0)),
                  pl.BlockSpec((1, LANES), lambda i: (0, 0)),
                  tab_spec, tab_spec, tab_spec],
        out_specs=[pl.BlockSpec((tm, DIFF_W), lambda i: (i, 0)),
                   pl.BlockSpec((tm, DIFF_W), lambda i: (i, 0))],
        out_shape=[jax.ShapeDtypeStruct((N_TOK, DIFF_W), F32)] * 2,
        compiler_params=_params("parallel"),
        name="diff_qk",
    )(p, p, qn_p, kn_p, *tabs)


def _diff_attn_kernel(lam_ref, q_ref, k_ref, v_ref, gn_ref, o_ref, *, lam_init):
    lv = lam_ref[...]
    lam = (jnp.exp(jnp.sum(lv[0:1] * lv[1:2], axis=-1, keepdims=True))
           - jnp.exp(jnp.sum(lv[2:3] * lv[3:4], axis=-1, keepdims=True)) + lam_init)
    scale = DIFF_DH ** -0.5
    q = q_ref[...]
    k = k_ref[...]
    lo = _lane_lo((1, LANES))
    kb = k.astype(BF16)
    p1, l1 = _softmax_rows(_dot_nt_bf16(jnp.where(lo, q, 0.0), kb), scale)
    p2, l2 = _softmax_rows(_dot_nt_bf16(jnp.where(lo, 0.0, q), kb), scale)
    w = p1 / l1 - lam * (p2 / l2)
    o = _dot_bf16(w, v_ref[...])
    o_ref[...] = (_rms(o, 2 * DIFF_DH) * gn_ref[...] * (1.0 - lam_init)).astype(o_ref.dtype)


def _diff_attn(lam, q, k, v, v_blk0, gn, batch, nq, nk, q_row0, tq, lam_init):
    nqb = nq // tq
    qb0 = q_row0 // tq
    return pl.pallas_call(
        functools.partial(_diff_attn_kernel, lam_init=lam_init),
        grid=(batch, DIFF_HEADS, nqb),
        in_specs=[pl.BlockSpec((4, DIFF_DH), lambda b, h, i: (0, 0)),
                  pl.BlockSpec((tq, LANES), lambda b, h, i: (qb0 + b * nqb + i, h)),
                  pl.BlockSpec((nk, LANES), lambda b, h, i: (b, h)),
                  pl.BlockSpec((nk, LANES), lambda b, h, i: (b, v_blk0 + h)),
                  pl.BlockSpec((1, LANES), lambda b, h, i: (0, h))],
        out_specs=pl.BlockSpec((tq, LANES), lambda b, h, i: (b * nqb + i, h)),
        out_shape=jax.ShapeDtypeStruct((batch * nq, DIFF_W), BF16),
        compiler_params=_params("parallel", "parallel", "arbitrary"),
        name="diff_attn",
    )(lam, q, k, v, gn)


def _seg64_sum_wide(x):
    return jnp.concatenate([_seg64_sum(x[:, j * LANES:(j + 1) * LANES]) for j in range(x.shape[1] // LANES)], axis=1)


_SCAN_SLOTS = 3
_SCAN_NJ = _SCAN_SLOTS * RWKV_HEADS


def _rwkv_pre_kernel(r_ref, k_ref, v_ref, lo_ref, wup_ref, aup_ref, gup_ref, w0_ref, a0_ref, kk_ref, ka_ref, rk_ref,
                     op_ref, g_ref, bonus_ref):
    W = RWKV_W
    col = lambda q: slice(q * W, (q + 1) * W)
    r = r_ref[...]
    k = k_ref[...]
    v = v_ref[...]
    lora = lo_ref[...]
    kk = k * kk_ref[...]
    kkn = kk * lax.rsqrt(_seg64_sum_wide(kk * kk) + EPS)
    g_ref[...] = _dot_bf16(_sigmoid(lora[:, 2 * LANES:3 * LANES]), gup_ref[...])
    pre = w0_ref[...] + _dot_bf16(jnp.tanh(lora[:, 0:LANES]), wup_ref[...])
    decay = jnp.exp(-jnp.exp(-_softplus(-pre) - 0.5))
    a = _sigmoid(a0_ref[...] + _dot_bf16(lora[:, LANES:2 * LANES], aup_ref[...]))
    lo = _lane_lo((1, LANES))
    bonus = None
    for d in range(2):
        a_d = a[:, col(d)]
        k_d = k * (1.0 + (a_d - 1.0) * ka_ref[...])
        t = _seg64_sum_wide(r * k_d * rk_ref[...])
        bonus = t if bonus is None else bonus + t
        for s, (x1, x2) in enumerate(((kkn, decay[:, col(d)]), (k_d, kkn * a_d), (r, v))):
            for h in range(RWKV_HEADS):
                blk = slice((h // 2) * LANES, (h // 2 + 1) * LANES)
                if h % 2 == 0:
                    out = jnp.where(lo, x1[:, blk], pltpu.roll(x2[:, blk], RWKV_HS, axis=1))
                else:
                    out = jnp.where(lo, pltpu.roll(x1[:, blk], RWKV_HS, axis=1), x2[:, blk])
                op_ref[:, d * _SCAN_NJ + s * RWKV_HEADS + h, :] = out
    bonus_ref[...] = bonus


def _rwkv_pre(p, r_blk, k_blk, v_blk, lo_blk, wup_bd, aup_bd, gup, w0, a0, k_k, k_a, r_k):
    tm = 256
    w = RWKV_W
    row = lambda n: pl.BlockSpec((1, n), lambda i: (0, 0))
    full = lambda a, b: pl.BlockSpec((a, b), lambda i: (0, 0))
    return pl.pallas_call(
        _rwkv_pre_kernel,
        grid=(N_TOK // tm,),
        in_specs=[pl.BlockSpec((tm, w), lambda i: (i, r_blk)),
                  pl.BlockSpec((tm, w), lambda i: (i, k_blk)),
                  pl.BlockSpec((tm, w), lambda i: (i, v_blk)),
                  pl.BlockSpec((tm, 3 * LANES), lambda i: (i, lo_blk)),
                  full(LANES, 2 * w), full(LANES, 2 * w), full(LANES, w),
                  row(2 * w), row(2 * w), row(w), row(w), row(w)],
        out_specs=[pl.BlockSpec((tm, 2 * _SCAN_NJ, LANES), lambda i: (i, 0, 0)),
                   pl.BlockSpec((tm, w), lambda i: (i, 0)), pl.BlockSpec((tm, w), lambda i: (i, 0))],
        out_shape=[jax.ShapeDtypeStruct((N_TOK, 2 * _SCAN_NJ, LANES), F32),
                   jax.ShapeDtypeStruct((N_TOK, w), F32), jax.ShapeDtypeStruct((N_TOK, w), F32)],
        compiler_params=_params("parallel"),
        name="rwkv_pre",
    )(p, p, p, p, wup_bd, aup_bd, gup, w0, a0, k_k, k_a, r_k)


_SCAN_CHUNK = 32
_SCAN_UNROLL = 16


def _rwkv_first_sa(s_ref, sa_ref, kk, n_k):
    nv = s_ref.shape[1]
    chunk = min(_SCAN_CHUNK, nv)
    for c0 in range(0, nv, chunk):
        def body(k, acc):
            return acc + s_ref[k, c0:c0 + chunk, :] * kk(k)
        sa_ref[c0:c0 + chunk, :] = lax.fori_loop(0, n_k, body, jnp.zeros((chunk, LANES), F32), unroll=_SCAN_UNROLL)


def _rwkv_step(s_ref, sa_ref, kk_next, w, kd, b, r, v_at, n_k):
    nv = s_ref.shape[1]
    chunk = min(_SCAN_CHUNK, nv)
    ys = []
    for c0 in range(0, nv, chunk):
        sa = sa_ref[c0:c0 + chunk, :]
        vc = v_at(c0, chunk)

        def body(k, acc):
            y_acc, sa_acc = acc
            s_new = s_ref[k, c0:c0 + chunk, :] * w(k) - sa * b(k) + vc * kd(k)
            s_ref[k, c0:c0 + chunk, :] = s_new
            return y_acc + s_new * r(k), sa_acc + s_new * kk_next(k)

        zero = jnp.zeros((chunk, LANES), F32)
        y_acc, sa_acc = lax.fori_loop(0, n_k, body, (zero, zero), unroll=_SCAN_UNROLL)
        sa_ref[c0:c0 + chunk, :] = sa_acc
        ys.append(y_acc)
    return ys[0] if len(ys) == 1 else jnp.concatenate(ys, axis=0)


_CTX_TB = 32


def _load_scan_operands(kt_ref, t, slabs):
    for s, x in enumerate(slabs):
        xt = x.T
        kt_ref[t, 2 * s] = xt[:RWKV_HS]
        kt_ref[t, 2 * s + 1] = xt[RWKV_HS:]


def _rwkv_scan_ctx_kernel(x_ref, y_ref, st_ref, s_ref, sa_ref, kt_ref, ys_ref):
    d = pl.program_id(0)
    tb = pl.program_id(1)
    nh = RWKV_HEADS
    step_t = lambda i: jnp.where(d == 0, i, _CTX_TB - 1 - i)

    @pl.when(tb == 0)
    def _():
        s_ref[...] = jnp.zeros_like(s_ref)

    def load_t(t, carry):
        _load_scan_operands(kt_ref, t, [
            jnp.concatenate([x_ref[b, t, s * nh:(s + 1) * nh, :] for b in range(BATCH)], axis=0)
            for s in range(_SCAN_SLOTS)])
        return carry

    lax.fori_loop(0, _CTX_TB, load_t, 0, unroll=2)

    t0 = step_t(0)
    _rwkv_first_sa(s_ref, sa_ref, lambda k: kt_ref[t0, 0, pl.ds(k, 1), :], RWKV_HS)

    def step(i, carry):
        t = step_t(i)
        tn = step_t(jnp.minimum(i + 1, _CTX_TB - 1))
        row = lambda q, tt: (lambda k: kt_ref[tt, q, pl.ds(k, 1), :])
        ys_ref[t] = _rwkv_step(s_ref, sa_ref, row(0, tn), row(1, t), row(2, t), row(3, t), row(4, t),
                               lambda c0, n: kt_ref[t, 5, pl.ds(c0, n), :], RWKV_HS)
        return carry

    lax.fori_loop(0, _CTX_TB, step, 0)

    def store_t(i, carry):
        z = jnp.concatenate([ys_ref[2 * i], ys_ref[2 * i + 1]], axis=0).T
        z_odd = pltpu.roll(z, RWKV_HS, axis=1)
        for b in range(BATCH):
            y_ref[b, 2 * i] = z[b * nh:(b + 1) * nh, :RWKV_HS]
            y_ref[b, 2 * i + 1] = z_odd[b * nh:(b + 1) * nh, :RWKV_HS]
        return carry

    lax.fori_loop(0, _CTX_TB // 2, store_t, 0, unroll=2)

    @pl.when(tb == pl.num_programs(1) - 1)
    def _():
        st_ref[0] = s_ref[...]


def _rwkv_scan_ctx(op4):
    nt = SEQ // _CTX_TB
    hs = RWKV_HS
    tblk = lambda d, tb: jnp.where(d == 0, tb, nt - 1 - tb)
    return pl.pallas_call(
        _rwkv_scan_ctx_kernel,
        grid=(2, nt),
        in_specs=[pl.BlockSpec((BATCH, _CTX_TB, _SCAN_NJ, LANES), lambda d, tb: (0, tblk(d, tb), d, 0))],
        out_specs=[pl.BlockSpec((BATCH, _CTX_TB, RWKV_HEADS, hs), lambda d, tb: (0, tblk(d, tb), d, 0)),
                   pl.BlockSpec((1, hs, hs, LANES), lambda d, tb: (d, 0, 0, 0))],
        out_shape=[jax.ShapeDtypeStruct((BATCH, SEQ, 2 * RWKV_HEADS, hs), F32),
                   jax.ShapeDtypeStruct((2, hs, hs, LANES), F32)],
        scratch_shapes=[pltpu.VMEM((hs, hs, LANES), F32), pltpu.VMEM((hs, LANES), F32),
                        pltpu.VMEM((_CTX_TB, 2 * _SCAN_SLOTS, hs, LANES), F32),
                        pltpu.VMEM((_CTX_TB, hs, LANES), F32)],
        compiler_params=_params("parallel", "arbitrary"),
        name="rwkv_scan_ctx",
    )(op4)


_LAT_TB = 32
_LAT_VSPLIT = 4
_LAT_STATES = 2 * DEC_BATCH * RWKV_HEADS
_LAT_VROWS = RWKV_HS // _LAT_VSPLIT


def _rwkv_scan_lat_kernel(xf0_ref, xf1_ref, xb0_ref, xb1_ref, s0_ref, yf_ref, yb_ref,
                          s_ref, sa_ref, kt_ref, v_ref, ys_ref):
    @pl.when(pl.program_id(0) == 0)
    def _():
        s_ref[...] = s0_ref[...]

    group = lax.broadcasted_iota(jnp.int32, (_LAT_VROWS, LANES), 1) // _LAT_STATES
    nh = RWKV_HEADS

    def load_t(t, carry):
        tr = _LAT_TB - 1 - t
        for s in range(_SCAN_SLOTS):
            heads = slice(s * nh, (s + 1) * nh)
            x = jnp.concatenate([xf0_ref[0, t, heads, :], xf1_ref[0, t, heads, :],
                                 xb0_ref[0, tr, heads, :], xb1_ref[0, tr, heads, :]], axis=0)
            xt = jnp.concatenate([x] * _LAT_VSPLIT, axis=0).T
            kt_ref[t, 2 * s] = xt[:RWKV_HS]
            if s < _SCAN_SLOTS - 1:
                kt_ref[t, 2 * s + 1] = xt[RWKV_HS:]
            else:
                v = jnp.zeros((_LAT_VROWS, LANES), F32)
                for g in range(_LAT_VSPLIT):
                    r0 = RWKV_HS + g * _LAT_VROWS
                    v = jnp.where(group == g, xt[r0:r0 + _LAT_VROWS, :], v)
                v_ref[t] = v
        return carry

    lax.fori_loop(0, _LAT_TB, load_t, 0, unroll=4)

    _rwkv_first_sa(s_ref, sa_ref, lambda k: kt_ref[0, 0, pl.ds(k, 1), :], RWKV_HS)

    def step(t, carry):
        tn = jnp.minimum(t + 1, _LAT_TB - 1)
        row = lambda q, tt: (lambda k: kt_ref[tt, q, pl.ds(k, 1), :])
        ys_ref[t] = _rwkv_step(s_ref, sa_ref, row(0, tn), row(1, t), row(2, t), row(3, t), row(4, t),
                               lambda c0, n: v_ref[t, pl.ds(c0, n), :], RWKV_HS)
        return carry

    lax.fori_loop(0, _LAT_TB, step, 0)

    def store_t(i, carry):
        rows = [jnp.where(group == g, ys_ref[2 * i + j], 0.0) for j in range(2) for g in range(_LAT_VSPLIT)]
        z = jnp.concatenate(rows, axis=0).T
        y = (z[0:_LAT_STATES] + z[_LAT_STATES:2 * _LAT_STATES]
             + z[2 * _LAT_STATES:3 * _LAT_STATES] + z[3 * _LAT_STATES:4 * _LAT_STATES])
        y_odd = pltpu.roll(y, RWKV_HS, axis=1)
        half = _LAT_STATES // 2
        for b in range(DEC_BATCH):
            rows_f = slice(b * nh, (b + 1) * nh)
            rows_b = slice(half + b * nh, half + (b + 1) * nh)
            yf_ref[b, 2 * i] = y[rows_f, :RWKV_HS]
            yf_ref[b, 2 * i + 1] = y_odd[rows_f, :RWKV_HS]
            yb_ref[b, _LAT_TB - 1 - 2 * i] = y[rows_b, :RWKV_HS]
            yb_ref[b, _LAT_TB - 2 - 2 * i] = y_odd[rows_b, :RWKV_HS]
        return carry

    lax.fori_loop(0, _LAT_TB // 2, store_t, 0, unroll=4)


def _rwkv_scan_lat(op4, s0):
    hs = RWKV_HS
    nv = _LAT_VROWS
    nt = DEC_SEQ // _LAT_TB
    per_seq = SEQ // _LAT_TB
    first = N_CTX // SEQ

    def x_spec(b, d):
        tblk = (lambda tb: tb) if d == 0 else (lambda tb: nt - 1 - tb)
        return pl.BlockSpec((1, _LAT_TB, _SCAN_NJ, LANES),
                            lambda tb: (first + b * (DEC_SEQ // SEQ) + tblk(tb) // per_seq, tblk(tb) % per_seq, d, 0))

    y_shape = jax.ShapeDtypeStruct((DEC_BATCH, DEC_SEQ, RWKV_HEADS, hs), F32)
    y_blk = (DEC_BATCH, _LAT_TB, RWKV_HEADS, hs)
    return pl.pallas_call(
        _rwkv_scan_lat_kernel,
        grid=(nt,),
        in_specs=[x_spec(0, 0), x_spec(1, 0), x_spec(0, 1), x_spec(1, 1),
                  pl.BlockSpec((hs, nv, LANES), lambda tb: (0, 0, 0))],
        out_specs=[pl.BlockSpec(y_blk, lambda tb: (0, tb, 0, 0)),
                   pl.BlockSpec(y_blk, lambda tb: (0, nt - 1 - tb, 0, 0))],
        out_shape=[y_shape, y_shape],
        scratch_shapes=[pltpu.VMEM((hs, nv, LANES), F32), pltpu.VMEM((nv, LANES), F32),
                        pltpu.VMEM((_LAT_TB, 2 * _SCAN_SLOTS - 1, hs, LANES), F32),
                        pltpu.VMEM((_LAT_TB, nv, LANES), F32), pltpu.VMEM((_LAT_TB, nv, LANES), F32)],
        compiler_params=_params("arbitrary"),
        name="rwkv_scan_lat",
    )(op4, op4, op4, op4, s0)


def _rwkv_post_kernel(yc_ref, ylf_ref, ylb_ref, bonus_ref, v_ref, g_ref, gn_ref, o_ref):
    def finish(head_sum):
        y = jnp.concatenate([head_sum(h) for h in range(RWKV_HEADS)], axis=1)
        y = y * lax.rsqrt(_seg64_sum_wide(y * y) * (1.0 / RWKV_HS) + EPS) * gn_ref[...]
        o_ref[...] = ((y + bonus_ref[...] * v_ref[...]) * g_ref[...]).astype(o_ref.dtype)

    @pl.when(pl.program_id(0) < N_CTX // SEQ)
    def _():
        finish(lambda h: yc_ref[0, :, h, :] + yc_ref[0, :, RWKV_HEADS + h, :])

    @pl.when(pl.program_id(0) >= N_CTX // SEQ)
    def _():
        finish(lambda h: ylf_ref[0, :, h, :] + ylb_ref[0, :, h, :])


def _rwkv_post(y_ctx, y_lat_f, y_lat_b, bonus, p, v_blk, g, gn):
    tm = SEQ
    w = RWKV_W
    hs = RWKV_HS
    n_ctx = N_CTX // SEQ
    lat = lambda y: y.reshape(N_LAT // SEQ, SEQ, RWKV_HEADS, hs)
    spec = pl.BlockSpec((tm, w), lambda i: (i, 0))
    lat_spec = pl.BlockSpec((1, SEQ, RWKV_HEADS, hs), lambda i: (jnp.maximum(i - n_ctx, 0), 0, 0, 0))
    return pl.pallas_call(
        _rwkv_post_kernel,
        grid=(N_TOK // tm,),
        in_specs=[pl.BlockSpec((1, SEQ, 2 * RWKV_HEADS, hs), lambda i: (jnp.minimum(i, n_ctx - 1), 0, 0, 0)),
                  lat_spec, lat_spec, spec, pl.BlockSpec((tm, w), lambda i: (i, v_blk)), spec,
                  pl.BlockSpec((1, w), lambda i: (0, 0))],
        out_specs=spec,
        out_shape=jax.ShapeDtypeStruct((N_TOK, w), BF16),
        compiler_params=_params("parallel"),
        name="rwkv_post",
    )(y_ctx, lat(y_lat_f), lat(y_lat_b), bonus, p, g, gn)


def _value_split_layout(x):
    lead = x.shape[:-2]
    n = len(lead)
    x = x.reshape(lead + (_LAT_STATES, _LAT_VSPLIT, _LAT_VROWS))
    return jnp.transpose(x, tuple(range(n)) + (n + 2, n + 1, n)).reshape(lead + (_LAT_VROWS, LANES))


def _even_layer(x, mod, g_mix, w_in, q_norm, kv_norm, w_uq, w_ukv, qn, kn, ret_decay, ret_gn,
                cache_ckv, cache_krope, state_ret, tabs_m):
    cq, ckv, krope, rq, rk, rv, rg = jnp.split(w_in, np.cumsum(
        (MLA_Q_RANK, MLA_KV_RANK, MLA_ROPE, RET_HEADS * RET_DK, RET_HEADS * RET_DK, RET_HEADS * RET_DV))[:].tolist(),
        axis=1)
    z = lambda n: jnp.zeros((D_MODEL, n), F32)
    w_p = jnp.concatenate([cq, ckv, z(MLA_NOPE), krope, z(LANES - MLA_QK), rq, rk, rv, rg], axis=1)
    p = _inproj(x, g_mix, mod, w_p, jnp.zeros((1, w_p.shape[1]), F32), 512, None)
    CKV_BLK, KR_BLK, RQ_BLK, RK_BLK, RV_BLK, RG_BLK = 2, 3, 4, 6, 8, 12

    def head_pad(w, n_head, d_head, c0, c1):
        w = w.reshape(w.shape[0], n_head, d_head)[:, :, c0:c1]
        return jnp.pad(w, ((0, 0), (0, 0), (0, LANES - (c1 - c0)))).reshape(w.shape[0], n_head * LANES)

    w_uq_p = head_pad(w_uq, MLA_HEADS, MLA_QK, 0, MLA_QK)
    wk_p = head_pad(w_ukv, MLA_HEADS, MLA_NOPE + MLA_V, 0, MLA_NOPE)
    wv_p = head_pad(w_ukv, MLA_HEADS, MLA_NOPE + MLA_V, MLA_NOPE, MLA_NOPE + MLA_V)
    qn_p = jnp.pad(qn, (0, LANES - MLA_QK))[None]
    kn_p = jnp.pad(kn, (0, LANES - MLA_QK))[None]

    q = _mla_q(p, q_norm[None], w_uq_p, qn_p, tabs_m)
    k, v, ckvn = _mla_kv(p, CKV_BLK, p, KR_BLK, kv_norm[None], wk_p, wv_p, kn_p, tabs_m, N_TOK, True)

    n_c = DEC_BATCH * PAST_LEN
    kr_c = jnp.pad(cache_krope.reshape(n_c, MLA_ROPE), ((0, 0), (MLA_NOPE, LANES - MLA_QK)))
    k_c, v_c, _ = _mla_kv(cache_ckv.reshape(n_c, MLA_KV_RANK), 0, kr_c, 0, kv_norm[None], wk_p, wv_p, kn_p,
                          tabs_m, n_c, False)

    hw = MLA_HEADS * LANES

    def with_cache(own, cache):
        own = own[N_CTX:].reshape(DEC_BATCH, DEC_SEQ, hw)
        return jnp.concatenate([own, cache.reshape(DEC_BATCH, PAST_LEN, hw)], 1).reshape(-1, hw)

    o_ctx = _mla_attn(q, k, v, BATCH, SEQ, SEQ, 0, SEQ)
    o_lat = _mla_attn(q, with_cache(k, k_c), with_cache(v, v_c), DEC_BATCH, DEC_SEQ, DEC_SEQ + PAST_LEN, N_CTX, 256)

    log_g = -_softplus(-ret_decay)
    gn = ret_gn[None]
    s0_ctx = jnp.zeros((BATCH, 2, RET_HEADS * RET_DK, RET_DV), F32)
    r_ctx, st_ctx = _retention(log_g, p, RQ_BLK, RK_BLK, RV_BLK, RG_BLK, s0_ctx, gn, BATCH, SEQ, 0, SEQ, True)
    s0_lat = state_ret.reshape(DEC_BATCH, 2, RET_HEADS * RET_DK, RET_DV)
    r_lat, _ = _retention(log_g, p, RQ_BLK, RK_BLK, RV_BLK, RG_BLK, s0_lat, gn, DEC_BATCH, DEC_SEQ, N_CTX, 256,
                          False)

    mix = [(o_ctx, o_lat), (r_ctx, r_lat)]
    new_ckv = ckvn[:N_CTX].reshape(BATCH, SEQ, MLA_KV_RANK)
    new_krope = p[:N_CTX, KR_BLK * LANES + MLA_NOPE:KR_BLK * LANES + MLA_QK]
    new_krope = new_krope.reshape(BATCH, SEQ, MLA_ROPE)
    new_ret = st_ctx.reshape(BATCH, 2, RET_HEADS, RET_DK, RET_DV)
    return mix, new_ckv, new_krope, new_ret


def _odd_layer(x, mod, g_mix, w_in, qn, kn, lam, diff_gn, mu, w0, w_up, a0, a_up, g_up, k_k, k_a, r_k, gn,
               cache_k, cache_v, state_rwkv, tabs_d, lam_init):
    w_p = w_in
    n_in = w_p.shape[1]
    mu_full = jnp.concatenate([jnp.zeros((3 * DIFF_W,), F32), mu])[None]
    p = _inproj(x, g_mix, mod, w_p, mu_full, 384, (3 * DIFF_W) // 384)
    DV_BLK, R_BLK, K_BLK, V_BLK = 2, 3, 4, 5
    LO_BLK = (6 * RWKV_W) // (3 * LANES)

    qn_p = jnp.tile(qn, 2)[None]
    kn_p = jnp.tile(kn, 2)[None]
    q, k = _diff_qk(p, qn_p, kn_p, tabs_d)

    n_c = DEC_BATCH * PAST_LEN
    k_c = cache_k.reshape(DEC_BATCH, PAST_LEN, DIFF_W)
    v_c = cache_v.reshape(DEC_BATCH, PAST_LEN, DIFF_W)

    def with_cache(own_lat, cache):
        return jnp.concatenate([own_lat.reshape(DEC_BATCH, DEC_SEQ, DIFF_W), cache], 1).reshape(-1, DIFF_W)

    dgn = diff_gn[None]
    dv_cols = slice(DV_BLK * DIFF_W, (DV_BLK + 1) * DIFF_W)
    o_ctx = _diff_attn(lam, q, k, p, DV_BLK * DIFF_HEADS, dgn, BATCH, SEQ, SEQ, 0, SEQ, lam_init)
    o_lat = _diff_attn(lam, q, with_cache(k[N_CTX:], k_c), with_cache(p[N_CTX:, dv_cols], v_c), 0, dgn,
                       DEC_BATCH, DEC_SEQ, DEC_SEQ + PAST_LEN, N_CTX, 256, lam_init)

    zero = jnp.zeros((RWKV_W_LORA, RWKV_W), F32)
    wup_bd = jnp.concatenate([jnp.concatenate([w_up[0], zero], 1), jnp.concatenate([zero, w_up[1]], 1)], 0)
    aup_bd = jnp.concatenate([jnp.concatenate([a_up[0], zero], 1), jnp.concatenate([zero, a_up[1]], 1)], 0)
    op, g, bonus = _rwkv_pre(p, R_BLK, K_BLK, V_BLK, LO_BLK, wup_bd, aup_bd, g_up, w0.reshape(1, -1),
                             a0.reshape(1, -1), k_k[None], k_a[None], r_k.reshape(1, -1))
    op4 = op.reshape(N_TOK // SEQ, SEQ, 2 * _SCAN_NJ, LANES)
    y_ctx, st_ctx = _rwkv_scan_ctx(op4)
    s0_lat = jnp.transpose(state_rwkv, (4, 1, 0, 2, 3)).reshape(RWKV_HS, _LAT_STATES, RWKV_HS)
    y_lat_f, y_lat_b = _rwkv_scan_lat(op4, _value_split_layout(s0_lat))
    rw_o = _rwkv_post(y_ctx, y_lat_f, y_lat_b, bonus, p, V_BLK, g, gn[None])

    mix = [(o_ctx, o_lat), rw_o]
    new_dk = k[:N_CTX].reshape(BATCH, SEQ, DIFF_HEADS, 2, DIFF_DH)
    new_dv = p[:N_CTX, dv_cols].reshape(BATCH, SEQ, DIFF_HEADS, 2 * DIFF_DH)
    new_rwkv = jnp.transpose(st_ctx.reshape(2, RWKV_HS, RWKV_HS, BATCH, RWKV_HEADS), (3, 0, 4, 2, 1))
    return mix, new_dk, new_dv, new_rwkv


def kernel(x_prompt, x_sample, cache_mla_ckv, cache_mla_krope, state_ret, cache_diff_k, cache_diff_v, state_rwkv,
           c, c_ctx, ada_w, ada_b, norm_mix_g, norm_ffn_g, w_out, ffn_up, ffn_conv_w, ffn_conv_b, ffn_down,
           a_w_in, mla_q_norm, mla_kv_norm, mla_w_uq, mla_w_ukv, mla_qn, mla_kn, ret_decay, ret_gn,
           b_w_in, diff_qn, diff_kn, diff_lam, diff_gn, rwkv_mu, rwkv_w0, rwkv_w_up, rwkv_a0, rwkv_a_up,
           rwkv_g_up, rwkv_k_k, rwkv_k_a, rwkv_r_k, rwkv_gn):
    x = jnp.concatenate([x_prompt.reshape(N_CTX, D_MODEL), x_sample.reshape(N_LAT, D_MODEL)], 0)
    cond8 = jnp.pad(jnp.concatenate([c_ctx[None], c], 0), ((0, 8 - N_GROUPS), (0, 0)))
    mod = _modulation(cond8, ada_w, ada_b)

    tabs_m = _rope_tables(MLA_ROPE, (MLA_NOPE,))
    tabs_d = _rope_tables(DIFF_DH, (0, DIFF_DH))

    outs = {}
    for l in range(DEPTH):
        j = l // 2
        g_mix = norm_mix_g[l][None]
        if l % 2 == 0:
            mix, outs["ckv"], outs["krope"], outs["ret"] = _even_layer(
                x, mod[l], g_mix, a_w_in[j], mla_q_norm[j], mla_kv_norm[j], mla_w_uq[j], mla_w_ukv[j], mla_qn[j],
                mla_kn[j], ret_decay[j], ret_gn[j], cache_mla_ckv[:, j], cache_mla_krope[:, j], state_ret[:, j],
                tabs_m)
        else:
            lam_init = 0.8 - 0.6 * math.exp(-0.3 * l)
            mix, outs["dk"], outs["dv"], outs["rwkv"] = _odd_layer(
                x, mod[l], g_mix, b_w_in[j], diff_qn[j], diff_kn[j], diff_lam[j], diff_gn[j], rwkv_mu[j],
                rwkv_w0[j], rwkv_w_up[j], rwkv_a0[j], rwkv_a_up[j], rwkv_g_up[j], rwkv_k_k[j], rwkv_k_a[j],
                rwkv_r_k[j], rwkv_gn[j], cache_diff_k[:, j], cache_diff_v[:, j], state_rwkv[:, j], tabs_d, lam_init)
        x = _resid_proj(mix, w_out[l], x, mod[l], 2)
        act = _ffn_up(x, norm_ffn_g[l][None], mod[l], ffn_up[l], ffn_conv_w[l], ffn_conv_b[l])
        x = _resid_proj([act], ffn_down[l], x, mod[l], 5)

    y_prompt = x[:N_CTX].reshape(BATCH, SEQ, D_MODEL)
    y_sample = x[N_CTX:].reshape(DEC_BATCH, DEC_SEQ, D_MODEL)
    return (y_prompt, y_sample, outs["ckv"][:, None], outs["krope"][:, None], outs["ret"][:, None],
            outs["dk"][:, None], outs["dv"][:, None], outs["rwkv"][:, None])
```

```python
import functools
import math

import numpy as np
import jax
import jax.numpy as jnp
from jax import lax
from jax.experimental import pallas as pl
from jax.experimental.pallas import tpu as pltpu

D_MODEL = 1024
BATCH = 16
SEQ = 256
DEPTH = 2
DEC_BATCH = 2
DEC_SEQ = 1024
PAST_LEN = 512
GRID_W = 64
EPS = 1e-6
ROPE_BASE = 10000.0

MLA_HEADS = 8
MLA_Q_RANK = 256
MLA_KV_RANK = 128
MLA_NOPE = 64
MLA_ROPE = 32
MLA_V = 64
MLA_QK = MLA_NOPE + MLA_ROPE
RET_HEADS = 4
RET_DK = 64
RET_DV = 128
DIFF_HEADS = 4
DIFF_DH = 64
DIFF_W = DIFF_HEADS * 2 * DIFF_DH
RWKV_HEADS = 8
RWKV_HS = 64
RWKV_W = RWKV_HEADS * RWKV_HS
RWKV_W_LORA = 64
RWKV_A_LORA = 64
RWKV_G_LORA = 128
D_FF = 2816

N_CTX = BATCH * SEQ
N_LAT = DEC_BATCH * DEC_SEQ
N_TOK = N_CTX + N_LAT
N_GROUPS = 1 + DEC_BATCH

LANES = 128
VMEM_LIMIT = 56 * 1024 * 1024

_PREC = lax.Precision.HIGHEST
F32 = jnp.float32


def _dot_tn(a, b):
    return lax.dot_general(a, b, (((0,), (0,)), ((), ())), precision=_PREC, preferred_element_type=F32)


BF16 = jnp.bfloat16


def _dot_bf16(a, b):
    return jnp.dot(a.astype(BF16), b.astype(BF16), preferred_element_type=F32)


def _dot_nt_bf16(a, b):
    return lax.dot_general(a.astype(BF16), b.astype(BF16), (((1,), (1,)), ((), ())), preferred_element_type=F32)


def _params(*sem):
    return pltpu.CompilerParams(dimension_semantics=sem, vmem_limit_bytes=VMEM_LIMIT)


def _sigmoid(x):
    return 1.0 / (1.0 + jnp.exp(-x))


def _silu(x):
    return x * _sigmoid(x)


def _softplus(x):
    return jnp.maximum(x, 0.0) + jnp.log(1.0 + jnp.exp(-jnp.abs(x)))


def _rms(x, n):
    return x * lax.rsqrt(jnp.sum(x * x, axis=-1, keepdims=True) * (1.0 / n) + EPS)


def _lane_lo(shape):
    return lax.broadcasted_iota(jnp.int32, shape, len(shape) - 1) < 64


def _seg64_sum(x):
    lo = _lane_lo(x.shape)
    s_lo = jnp.sum(jnp.where(lo, x, 0.0), axis=-1, keepdims=True)
    s_hi = jnp.sum(jnp.where(lo, 0.0, x), axis=-1, keepdims=True)
    return jnp.where(lo, s_lo, s_hi)


_SUBLANES = 8


def _seq_neighbours(p, tile, tile_rows):
    is_ctx = tile * tile_rows < N_CTX
    sub = lax.broadcasted_iota(jnp.int32, (_SUBLANES, 1), 0)

    def shifted(rolled, edge_sublane, group_of_seq):
        pieces, start = [], 0
        for q in range(tile_rows // SEQ):
            g0 = q * SEQ + group_of_seq
            outer = (q == 0) if group_of_seq == 0 else (q == tile_rows // SEQ - 1)
            edge = (sub == edge_sublane) if outer else ((sub == edge_sublane) & is_ctx)
            pieces += [rolled[start:g0], jnp.where(edge, 0.0, rolled[g0:g0 + _SUBLANES])]
            start = g0 + _SUBLANES
        pieces.append(rolled[start:])
        return jnp.concatenate([x for x in pieces if x.shape[0]], axis=0)

    prev = shifted(pltpu.roll(p, 1, axis=0), 0, 0)
    nxt = shifted(pltpu.roll(p, tile_rows - 1, axis=0), _SUBLANES - 1, SEQ - _SUBLANES)
    return prev, nxt


def _group_of_tile(i, tile_rows):
    row = i * tile_rows
    return jnp.where(row < N_CTX, 0, 1 + (row - N_CTX) // DEC_SEQ)


def _modulation_kernel(c_ref, w_ref, b_ref, o_ref):
    o_ref[0] = _dot_bf16(_silu(c_ref[...]), w_ref[0]) + b_ref[0]


def _modulation(cond8, ada_w, ada_b):
    tn = 512
    n = 6 * D_MODEL
    out = pl.pallas_call(
        _modulation_kernel,
        grid=(DEPTH, n // tn),
        in_specs=[pl.BlockSpec((8, D_MODEL), lambda l, j: (0, 0)),
                  pl.BlockSpec((1, D_MODEL, tn), lambda l, j: (l, 0, j)),
                  pl.BlockSpec((1, 1, tn), lambda l, j: (l, 0, j))],
        out_specs=pl.BlockSpec((1, 8, tn), lambda l, j: (l, 0, j)),
        out_shape=jax.ShapeDtypeStruct((DEPTH, 8, n), F32),
        compiler_params=_params("parallel", "parallel"),
        name="modulation",
    )(cond8, ada_w, ada_b.reshape(DEPTH, 1, n))
    m = out[:, :N_GROUPS].reshape(DEPTH, N_GROUPS, 6, D_MODEL)
    return jnp.pad(m, ((0, 0), (0, 0), (0, 2), (0, 0)))


_TM_SEQ = 1024


def _norm_mod(x, g, mod, off):
    return _rms(x, D_MODEL) * g * (1.0 + mod[off + 1:off + 2, :]) + mod[off:off + 1, :]


def _inproj_kernel(x_ref, g_ref, mod_ref, w_ref, mu_ref, o_ref, h_ref, *, shift_from):
    i = pl.program_id(0)

    @pl.when(pl.program_id(1) == 0)
    def _():
        h_ref[...] = _norm_mod(x_ref[...], g_ref[...], mod_ref[0], 0).astype(BF16)

    p = _dot_bf16(h_ref[...], w_ref[...])
    if shift_from is None:
        o_ref[...] = p
    else:
        @pl.when(pl.program_id(1) < shift_from)
        def _():
            o_ref[...] = p

        @pl.when(pl.program_id(1) >= shift_from)
        def _():
            prev, nxt = _seq_neighbours(p, i, _TM_SEQ)
            o_ref[...] = p + (0.5 * (prev + nxt) - p) * mu_ref[...]


def _inproj(x, g, mod, w, mu, tn, shift_from):
    n = w.shape[1]
    tm = _TM_SEQ
    return pl.pallas_call(
        functools.partial(_inproj_kernel, shift_from=shift_from),
        grid=(N_TOK // tm, n // tn),
        in_specs=[pl.BlockSpec((tm, D_MODEL), lambda i, j: (i, 0)),
                  pl.BlockSpec((1, D_MODEL), lambda i, j: (0, 0)),
                  pl.BlockSpec((1, 8, D_MODEL), lambda i, j: (_group_of_tile(i, tm), 0, 0)),
                  pl.BlockSpec((D_MODEL, tn), lambda i, j: (0, j)),
                  pl.BlockSpec((1, tn), lambda i, j: (0, j))],
        out_specs=pl.BlockSpec((tm, tn), lambda i, j: (i, j)),
        out_shape=jax.ShapeDtypeStruct((N_TOK, n), F32),
        scratch_shapes=[pltpu.VMEM((tm, D_MODEL), BF16)],
        compiler_params=_params("parallel", "arbitrary"),
        name="inproj" if shift_from is None else "inproj_shift",
    )(x, g, mod, w, mu)


def _resid_kernel(*refs, gate_row, widths, split, split_out):
    n_in = sum(2 if sp else 1 for sp in split)
    a_refs = refs[:n_in]
    w_ref, x_ref, mod_ref = refs[n_in:n_in + 3]
    o_ref = refs[n_in + 3:-1] if split_out else refs[n_in + 3]
    a_bf_ref = refs[-1]
    i = pl.program_id(0)

    @pl.when(pl.program_id(1) == 0)
    def _():
        k0, r = 0, 0
        for width, sp in zip(widths, split):
            cols = slice(k0, k0 + width)
            if sp:
                ctx_ref, lat_ref = a_refs[r], a_refs[r + 1]

                @pl.when(i < N_CTX // _TM_SEQ)
                def _():
                    a_bf_ref[:, cols] = ctx_ref[...].astype(BF16)

                @pl.when(i >= N_CTX // _TM_SEQ)
                def _():
                    a_bf_ref[:, cols] = lat_ref[...].astype(BF16)
            else:
                a_bf_ref[:, cols] = a_refs[r][...].astype(BF16)
            k0 += width
            r += 2 if sp else 1

    y = x_ref[...] + mod_ref[0, gate_row:gate_row + 1, :] * _dot_bf16(a_bf_ref[...], w_ref[...])
    if not split_out:
        o_ref[...] = y
    else:
        ctx_o_ref, lat_o_ref = o_ref

        @pl.when(i < N_CTX // _TM_SEQ)
        def _():
            ctx_o_ref[...] = y

        @pl.when(i >= N_CTX // _TM_SEQ)
        def _():
            lat_o_ref[...] = y


def _resid_proj(acts, w, x, mod, gate_row, split_out=False):
    tm, tn = _TM_SEQ, 256
    n_ctx = N_CTX // tm
    nj = D_MODEL // tn
    split = [isinstance(a, (tuple, list)) for a in acts]
    widths = [a[0].shape[1] if sp else a.shape[1] for a, sp in zip(acts, split)]
    k = sum(widths)
    in_specs, operands = [], []
    for a, width, sp in zip(acts, widths, split):
        if sp:
            in_specs += [pl.BlockSpec((tm, width), lambda i, j: (jnp.minimum(i, n_ctx - 1), 0)),
                         pl.BlockSpec((tm, width), lambda i, j: (jnp.maximum(i - n_ctx, 0), 0))]
            operands += list(a)
        else:
            in_specs.append(pl.BlockSpec((tm, width), lambda i, j: (i, 0)))
            operands.append(a)
    if split_out:
        out_specs = [pl.BlockSpec((tm, tn), lambda i, j: (jnp.minimum(i, n_ctx - 1), jnp.where(i < n_ctx, j, nj - 1))),
                     pl.BlockSpec((tm, tn), lambda i, j: (jnp.maximum(i - n_ctx, 0), jnp.where(i < n_ctx, 0, j)))]
        out_shape = [jax.ShapeDtypeStruct((N_CTX, D_MODEL), F32), jax.ShapeDtypeStruct((N_LAT, D_MODEL), F32)]
        sem = ("arbitrary", "arbitrary")
    else:
        out_specs = pl.BlockSpec((tm, tn), lambda i, j: (i, j))
        out_shape = jax.ShapeDtypeStruct((N_TOK, D_MODEL), F32)
        sem = ("parallel", "arbitrary")
    return pl.pallas_call(
        functools.partial(_resid_kernel, gate_row=gate_row, widths=tuple(widths), split=tuple(split),
                          split_out=split_out),
        grid=(N_TOK // tm, nj),
        in_specs=in_specs
        + [pl.BlockSpec((k, tn), lambda i, j: (0, j)),
           pl.BlockSpec((tm, tn), lambda i, j: (i, j)),
           pl.BlockSpec((1, 8, tn), lambda i, j: (_group_of_tile(i, tm), 0, j))],
        out_specs=out_specs,
        out_shape=out_shape,
        scratch_shapes=[pltpu.VMEM((tm, k), BF16)],
        compiler_params=_params(*sem),
        name="resid_proj",
    )(*operands, w, x, mod)


def _ffn_up_kernel(x_ref, g_ref, mod_ref, wa_ref, wb_ref, cwa_ref, cwb_ref, cba_ref, cbb_ref, o_ref, h_ref):
    i = pl.program_id(0)

    @pl.when(pl.program_id(1) == 0)
    def _():
        h_ref[...] = _norm_mod(x_ref[...], g_ref[...], mod_ref[0], 3).astype(BF16)

    h = h_ref[...]

    def conv(w_ref, cw_ref, cb_ref):
        u = _dot_bf16(h, w_ref[...])
        prev, nxt = _seq_neighbours(u, i, _TM_SEQ)
        return prev * cw_ref[0:1, :] + u * cw_ref[1:2, :] + nxt * cw_ref[2:3, :] + cb_ref[...]

    o_ref[...] = (_silu(conv(wa_ref, cwa_ref, cba_ref)) * conv(wb_ref, cwb_ref, cbb_ref)).astype(o_ref.dtype)


def _ffn_up(x, g, mod, up, cw, cb):
    tm, tn = _TM_SEQ, 256
    nb = D_FF // tn
    cb = cb.reshape(1, 2 * D_FF)
    return pl.pallas_call(
        _ffn_up_kernel,
        grid=(N_TOK // tm, nb),
        in_specs=[pl.BlockSpec((tm, D_MODEL), lambda i, j: (i, 0)),
                  pl.BlockSpec((1, D_MODEL), lambda i, j: (0, 0)),
                  pl.BlockSpec((1, 8, D_MODEL), lambda i, j: (_group_of_tile(i, tm), 0, 0)),
                  pl.BlockSpec((D_MODEL, tn), lambda i, j: (0, j)),
                  pl.BlockSpec((D_MODEL, tn), lambda i, j: (0, j + nb)),
                  pl.BlockSpec((3, tn), lambda i, j: (0, j)),
                  pl.BlockSpec((3, tn), lambda i, j: (0, j + nb)),
                  pl.BlockSpec((1, tn), lambda i, j: (0, j)),
                  pl.BlockSpec((1, tn), lambda i, j: (0, j + nb))],
        out_specs=pl.BlockSpec((tm, tn), lambda i, j: (i, j)),
        out_shape=jax.ShapeDtypeStruct((N_TOK, D_FF), BF16),
        scratch_shapes=[pltpu.VMEM((tm, D_MODEL), BF16)],
        compiler_params=_params("parallel", "arbitrary"),
        name="ffn_up",
    )(x, g, mod, up, up, cw, cw, cb, cb)


_ROPE_TM = 512


def _rope(y, c, s1, s2):
    return y * c + pltpu.roll(y, 1, axis=1) * s1 + pltpu.roll(y, LANES - 1, axis=1) * s2


def _rope_tables(rot_dim, lane_offsets):
    t = np.arange(DEC_SEQ)
    row, col = t // GRID_W, t % GRID_W
    n_freq = rot_dim // 4
    inv = ROPE_BASE ** (-np.arange(n_freq, dtype=np.float64) / n_freq)
    ang = np.concatenate([row[:, None] * inv, col[:, None] * inv], -1)
    cos, sin = np.cos(ang), np.sin(ang)
    n = _ROPE_TM + DEC_SEQ
    c, s1, s2 = np.ones((n, LANES)), np.zeros((n, LANES)), np.zeros((n, LANES))
    for a in lane_offsets:
        even = a + 2 * np.arange(rot_dim // 2)
        c[_ROPE_TM:, even] = cos
        c[_ROPE_TM:, even + 1] = cos
        s1[_ROPE_TM:, even + 1] = sin
        s2[_ROPE_TM:, even] = -sin
    return tuple(jnp.asarray(x, F32) for x in (c, s1, s2))


def _rope_block(i):
    row = i * _ROPE_TM
    return jnp.where(row < N_CTX, 0, 1 + ((row - N_CTX) % DEC_SEQ) // _ROPE_TM)


def _mla_q_kernel(cq_ref, qnorm_ref, w_ref, qn_ref, c_ref, s1_ref, s2_ref, o_ref):
    xn = _rms(cq_ref[...], MLA_Q_RANK) * qnorm_ref[...]
    y = _dot_bf16(xn, w_ref[...])
    c, s1, s2 = c_ref[...], s1_ref[...], s2_ref[...]
    for h in range(MLA_HEADS):
        yh = y[:, h * LANES:(h + 1) * LANES]
        yh = _rms(yh, MLA_QK) * qn_ref[...]
        o_ref[:, h * LANES:(h + 1) * LANES] = _rope(yh, c, s1, s2)


def _mla_q(p, q_norm, w_uq_p, qn_p, tabs):
    tm = _ROPE_TM
    hw = MLA_HEADS * LANES
    tab_spec = pl.BlockSpec((tm, LANES), lambda i: (_rope_block(i), 0))
    return pl.pallas_call(
        _mla_q_kernel,
        grid=(N_TOK // tm,),
        in_specs=[pl.BlockSpec((tm, MLA_Q_RANK), lambda i: (i, 0)),
                  pl.BlockSpec((1, MLA_Q_RANK), lambda i: (0, 0)),
                  pl.BlockSpec((MLA_Q_RANK, hw), lambda i: (0, 0)),
                  pl.BlockSpec((1, LANES), lambda i: (0, 0)),
                  tab_spec, tab_spec, tab_spec],
        out_specs=pl.BlockSpec((tm, hw), lambda i: (i, 0)),
        out_shape=jax.ShapeDtypeStruct((N_TOK, hw), F32),
        compiler_params=_params("parallel"),
        name="mla_q",
    )(p, q_norm, w_uq_p, qn_p, *tabs)


def _mla_kv_kernel(ckv_ref, kr_ref, kvn_ref, wk_ref, wv_ref, kn_ref, c_ref, s1_ref, s2_ref,
                   k_ref, v_ref, ckvn_ref, *, norm_ckv):
    ckv = ckv_ref[...]
    if norm_ckv:
        ckv = _rms(ckv, MLA_KV_RANK) * kvn_ref[...]
    ckvn_ref[...] = ckv
    ckv_bf = ckv.astype(BF16)
    kk = _dot_bf16(ckv_bf, wk_ref[...])
    v_ref[...] = _dot_bf16(ckv_bf, wv_ref[...])
    kr = kr_ref[...]
    c, s1, s2 = c_ref[...], s1_ref[...], s2_ref[...]
    for h in range(MLA_HEADS):
        kh = kk[:, h * LANES:(h + 1) * LANES] + kr
        kh = _rms(kh, MLA_QK) * kn_ref[...]
        k_ref[:, h * LANES:(h + 1) * LANES] = _rope(kh, c, s1, s2)


def _mla_kv(ckv_src, ckv_blk, kr_src, kr_blk, kv_norm, wk_p, wv_p, kn_p, tabs, n_rows, own_tokens):
    tm = _ROPE_TM
    hw = MLA_HEADS * LANES
    tab_spec = pl.BlockSpec((tm, LANES), (lambda i: (_rope_block(i), 0)) if own_tokens else (lambda i: (0, 0)))
    return pl.pallas_call(
        functools.partial(_mla_kv_kernel, norm_ckv=own_tokens),
        grid=(n_rows // tm,),
        in_specs=[pl.BlockSpec((tm, LANES), lambda i: (i, ckv_blk)),
                  pl.BlockSpec((tm, LANES), lambda i: (i, kr_blk)),
                  pl.BlockSpec((1, LANES), lambda i: (0, 0)),
                  pl.BlockSpec((MLA_KV_RANK, hw), lambda i: (0, 0)),
                  pl.BlockSpec((MLA_KV_RANK, hw), lambda i: (0, 0)),
                  pl.BlockSpec((1, LANES), lambda i: (0, 0)),
                  tab_spec, tab_spec, tab_spec],
        out_specs=[pl.BlockSpec((tm, hw), lambda i: (i, 0)),
                   pl.BlockSpec((tm, hw), lambda i: (i, 0)),
                   pl.BlockSpec((tm, LANES), lambda i: (i, 0))],
        out_shape=[jax.ShapeDtypeStruct((n_rows, hw), F32),
                   jax.ShapeDtypeStruct((n_rows, hw), F32),
                   jax.ShapeDtypeStruct((n_rows, LANES), F32)],
        compiler_params=_params("parallel"),
        name="mla_kv",
    )(ckv_src, kr_src, kv_norm, wk_p, wv_p, kn_p, *tabs)


_LOG2E = math.log2(math.e)


def _softmax_rows(s, scale):
    p = jnp.exp2((s - jnp.max(s, axis=-1, keepdims=True)) * (scale * _LOG2E))
    return p, jnp.sum(p, axis=-1, keepdims=True)


def _mla_attn_kernel(q_ref, k_ref, v_ref, o_ref, *, pairs):
    scale = MLA_QK ** -0.5
    for pr in range(pairs):
        outs = []
        for h in range(2):
            sl = slice((2 * pr + h) * LANES, (2 * pr + h + 1) * LANES)
            p, l = _softmax_rows(_dot_nt_bf16(q_ref[:, sl], k_ref[:, sl]), scale)
            outs.append(_dot_bf16(p, v_ref[:, sl]) / l)
        o_ref[:, pr * LANES:(pr + 1) * LANES] = (outs[0] + pltpu.roll(outs[1], MLA_V, axis=1)).astype(o_ref.dtype)


def _mla_attn(q, k, v, batch, nq, nk, q_row0, tq, pairs):
    nqb = nq // tq
    qb0 = q_row0 // tq
    wide = 2 * LANES * pairs
    return pl.pallas_call(
        functools.partial(_mla_attn_kernel, pairs=pairs),
        grid=(batch, MLA_HEADS // (2 * pairs), nqb),
        in_specs=[pl.BlockSpec((tq, wide), lambda b, h, i: (qb0 + b * nqb + i, h)),
                  pl.BlockSpec((nk, wide), lambda b, h, i: (b, h)),
                  pl.BlockSpec((nk, wide), lambda b, h, i: (b, h))],
        out_specs=pl.BlockSpec((tq, LANES * pairs), lambda b, h, i: (b * nqb + i, h)),
        out_shape=jax.ShapeDtypeStruct((batch * nq, MLA_HEADS * MLA_V), BF16),
        compiler_params=_params("parallel", "parallel", "arbitrary"),
        name="mla_attn",
    )(q, k, v)


def _ret_kernel(lg_ref, q_ref, k_ref, v_ref, rg_ref, s0_ref, gn_ref, *out_and_scratch, n, tq, want_state):
    if want_state:
        o_ref, st_ref, decay_ref = out_and_scratch
    else:
        o_ref, decay_ref = out_and_scratch
    pair, qi, b = pl.program_id(0), pl.program_id(1), pl.program_id(2)
    q = q_ref[...]
    k = k_ref[...] * (RET_DK ** -0.5)
    lo = _lane_lo((1, LANES))
    row = (qi * tq + lax.broadcasted_iota(jnp.int32, (tq, 1), 0)).astype(F32)

    @pl.when(b == 0)
    def _():
        col = lax.broadcasted_iota(jnp.int32, (1, n), 1).astype(F32)
        diff = row - col
        for h in range(2):
            lgf = lg_ref[0, 2 * pair + h]
            lgb = lg_ref[1, 2 * pair + h]
            decay_ref[h] = (jnp.where(diff >= 0, jnp.exp(lgf * jnp.maximum(diff, 0.0)), 0.0)
                            + jnp.where(diff <= 0, jnp.exp(lgb * jnp.maximum(-diff, 0.0)), 0.0))

    for h in range(2):
        lgf = lg_ref[0, 2 * pair + h]
        lgb = lg_ref[1, 2 * pair + h]
        mask = lo if h == 0 else jnp.logical_not(lo)
        qh = jnp.where(mask, q, 0.0)
        vh = v_ref[:, h * LANES:(h + 1) * LANES]
        o = _dot_bf16(_dot_nt_bf16(qh, k) * decay_ref[h], vh)
        o = o + _dot_bf16(qh * jnp.exp(lgf * (row + 1.0)), s0_ref[0, 0])
        o = o + _dot_bf16(qh * jnp.exp(lgb * (n - row)), s0_ref[0, 1])
        y = _rms(o, RET_DV) * gn_ref[:, h * LANES:(h + 1) * LANES]
        o_ref[:, h * LANES:(h + 1) * LANES] = (_silu(rg_ref[:, h * LANES:(h + 1) * LANES]) * y).astype(o_ref.dtype)

    if want_state:
        pos = lax.broadcasted_iota(jnp.int32, (n, 1), 0).astype(F32)
        for d in range(2):
            acc = None
            for h in range(2):
                lg = lg_ref[d, 2 * pair + h]
                mask = lo if h == 0 else jnp.logical_not(lo)
                expo = (n - 1.0 - pos) if d == 0 else pos
                kd = jnp.where(mask, k * jnp.exp(lg * expo), 0.0)
                term = _dot_tn(kd, v_ref[:, h * LANES:(h + 1) * LANES])
                acc = term if acc is None else acc + term
            lg_rows = jnp.where(lax.broadcasted_iota(jnp.int32, (LANES, 1), 0) < 64,
                                lg_ref[d, 2 * pair], lg_ref[d, 2 * pair + 1])
            st_ref[0, d] = acc + s0_ref[0, d] * jnp.exp(lg_rows * n)


def _retention(log_g, p, q_blk, k_blk, v_blk, g_blk, s0, gn, batch, n, row0, tq, want_state):
    nqb = n // tq
    assert not want_state or nqb == 1
    qb0 = row0 // tq
    kb0 = row0 // n
    pairs = RET_HEADS // 2
    out_specs = [pl.BlockSpec((tq, 2 * LANES), lambda h, i, b: (b * nqb + i, h))]
    out_shape = [jax.ShapeDtypeStruct((batch * n, RET_HEADS * RET_DV), BF16)]
    if want_state:
        out_specs.append(pl.BlockSpec((1, 2, LANES, LANES), lambda h, i, b: (b, 0, h, 0)))
        out_shape.append(jax.ShapeDtypeStruct((batch, 2, RET_HEADS * RET_DK, RET_DV), F32))
    outs = pl.pallas_call(
        functools.partial(_ret_kernel, n=n, tq=tq, want_state=want_state),
        grid=(pairs, nqb, batch),
        in_specs=[pl.BlockSpec(memory_space=pltpu.SMEM),
                  pl.BlockSpec((tq, LANES), lambda h, i, b: (qb0 + b * nqb + i, q_blk + h)),
                  pl.BlockSpec((n, LANES), lambda h, i, b: (kb0 + b, k_blk + h)),
                  pl.BlockSpec((n, 2 * LANES), lambda h, i, b: (kb0 + b, v_blk // 2 + h)),
                  pl.BlockSpec((tq, 2 * LANES), lambda h, i, b: (qb0 + b * nqb + i, g_blk // 2 + h)),
                  pl.BlockSpec((1, 2, LANES, LANES), lambda h, i, b: (b, 0, h, 0)),
                  pl.BlockSpec((1, 2 * LANES), lambda h, i, b: (0, h))],
        out_specs=out_specs,
        out_shape=out_shape,
        scratch_shapes=[pltpu.VMEM((2, tq, n), F32)],
        compiler_params=_params("parallel", "parallel", "arbitrary"),
        name="retention",
    )(log_g, p, p, p, p, s0, gn)
    return outs if want_state else (outs[0], None)


def _diff_qk_kernel(q_ref, k_ref, qn_ref, kn_ref, c_ref, s1_ref, s2_ref, qo_ref, ko_ref):
    c, s1, s2 = c_ref[...], s1_ref[...], s2_ref[...]
    for src, gain, dst in ((q_ref, qn_ref, qo_ref), (k_ref, kn_ref, ko_ref)):
        for h in range(DIFF_HEADS):
            sl = slice(h * LANES, (h + 1) * LANES)
            y = src[:, sl]
            y = y * lax.rsqrt(_seg64_sum(y * y) * (1.0 / DIFF_DH) + EPS) * gain[...]
            dst[:, sl] = _rope(y, c, s1, s2)


def _diff_qk(p, qn_p, kn_p, tabs):
    tm = _ROPE_TM
    tab_spec = pl.BlockSpec((tm, LANES), lambda i: (_rope_block(i), 0))
    return pl.pallas_call(
        _diff_qk_kernel,
        grid=(N_TOK // tm,),
        in_specs=[pl.BlockSpec((tm, DIFF_W), lambda i: (i, 0)),
                  pl.BlockSpec((tm, DIFF_W), lambda i: (i, 1)),
                  pl.BlockSpec((1, LANES), lambda i: (0, 0)),
                  pl.BlockSpec((1, LANES), lambda i: (0, 0)),
                  tab_spec, tab_spec, tab_spec],
        out_specs=[pl.BlockSpec((tm, DIFF_W), lambda i: (i, 0)),
                   pl.BlockSpec((tm, DIFF_W), lambda i: (i, 0))],
        out_shape=[jax.ShapeDtypeStruct((N_TOK, DIFF_W), F32)] * 2,
        compiler_params=_params("parallel"),
        name="diff_qk",
    )(p, p, qn_p, kn_p, *tabs)


def _diff_attn_kernel(lam_ref, q_ref, k_ref, v_ref, gn_ref, o_ref, *, lam_init, heads):
    lv = lam_ref[...]
    lam = (jnp.exp(jnp.sum(lv[0:1] * lv[1:2], axis=-1, keepdims=True))
           - jnp.exp(jnp.sum(lv[2:3] * lv[3:4], axis=-1, keepdims=True)) + lam_init)
    scale = DIFF_DH ** -0.5
    lo = _lane_lo((1, LANES))
    for h in range(heads):
        sl = slice(h * LANES, (h + 1) * LANES)
        q = q_ref[:, sl]
        kb = k_ref[:, sl].astype(BF16)
        p1, l1 = _softmax_rows(_dot_nt_bf16(jnp.where(lo, q, 0.0), kb), scale)
        p2, l2 = _softmax_rows(_dot_nt_bf16(jnp.where(lo, 0.0, q), kb), scale)
        w = p1 / l1 - lam * (p2 / l2)
        o = _dot_bf16(w, v_ref[:, sl])
        o_ref[:, sl] = (_rms(o, 2 * DIFF_DH) * gn_ref[:, sl] * (1.0 - lam_init)).astype(o_ref.dtype)


def _diff_attn(lam, q, k, v, v_blk0, gn, batch, nq, nk, q_row0, tq, lam_init, heads):
    nqb = nq // tq
    qb0 = q_row0 // tq
    wide = LANES * heads
    vb0 = v_blk0 // heads
    return pl.pallas_call(
        functools.partial(_diff_attn_kernel, lam_init=lam_init, heads=heads),
        grid=(batch, DIFF_HEADS // heads, nqb),
        in_specs=[pl.BlockSpec((4, DIFF_DH), lambda b, h, i: (0, 0)),
                  pl.BlockSpec((tq, wide), lambda b, h, i: (qb0 + b * nqb + i, h)),
                  pl.BlockSpec((nk, wide), lambda b, h, i: (b, h)),
                  pl.BlockSpec((nk, wide), lambda b, h, i: (b, vb0 + h)),
                  pl.BlockSpec((1, wide), lambda b, h, i: (0, h))],
        out_specs=pl.BlockSpec((tq, wide), lambda b, h, i: (b * nqb + i, h)),
        out_shape=jax.ShapeDtypeStruct((batch * nq, DIFF_W), BF16),
        compiler_params=_params("parallel", "parallel", "arbitrary"),
        name="diff_attn",
    )(lam, q, k, v, gn)


def _seg64_sum_wide(x):
    return jnp.concatenate([_seg64_sum(x[:, j * LANES:(j + 1) * LANES]) for j in range(x.shape[1] // LANES)], axis=1)


_SCAN_SLOTS = 3
_SCAN_NJ = _SCAN_SLOTS * RWKV_HEADS


def _rwkv_pre_kernel(r_ref, k_ref, v_ref, lo_ref, wup_ref, aup_ref, gup_ref, w0_ref, a0_ref, kk_ref, ka_ref, rk_ref,
                     op_ref, g_ref, bonus_ref):
    W = RWKV_W
    col = lambda q: slice(q * W, (q + 1) * W)
    r = r_ref[...]
    k = k_ref[...]
    v = v_ref[...]
    lora = lo_ref[...]
    kk = k * kk_ref[...]
    kkn = kk * lax.rsqrt(_seg64_sum_wide(kk * kk) + EPS)
    g_ref[...] = _dot_bf16(_sigmoid(lora[:, 2 * LANES:3 * LANES]), gup_ref[...])
    pre = w0_ref[...] + _dot_bf16(jnp.tanh(lora[:, 0:LANES]), wup_ref[...])
    decay = jnp.exp(-jnp.exp(-_softplus(-pre) - 0.5))
    a = _sigmoid(a0_ref[...] + _dot_bf16(lora[:, LANES:2 * LANES], aup_ref[...]))
    lo = _lane_lo((1, LANES))
    bonus = None
    for d in range(2):
        a_d = a[:, col(d)]
        k_d = k * (1.0 + (a_d - 1.0) * ka_ref[...])
        t = _seg64_sum_wide(r * k_d * rk_ref[...])
        bonus = t if bonus is None else bonus + t
        for s, (x1, x2) in enumerate(((kkn, decay[:, col(d)]), (k_d, kkn * a_d), (r, v))):
            for h in range(RWKV_HEADS):
                blk = slice((h // 2) * LANES, (h // 2 + 1) * LANES)
                if h % 2 == 0:
                    out = jnp.where(lo, x1[:, blk], pltpu.roll(x2[:, blk], RWKV_HS, axis=1))
                else:
                    out = jnp.where(lo, pltpu.roll(x1[:, blk], RWKV_HS, axis=1), x2[:, blk])
                op_ref[:, d * _SCAN_NJ + s * RWKV_HEADS + h, :] = out
    bonus_ref[...] = bonus


def _rwkv_pre(p, r_blk, k_blk, v_blk, lo_blk, wup_bd, aup_bd, gup, w0, a0, k_k, k_a, r_k):
    tm = 256
    w = RWKV_W
    row = lambda n: pl.BlockSpec((1, n), lambda i: (0, 0))
    full = lambda a, b: pl.BlockSpec((a, b), lambda i: (0, 0))
    return pl.pallas_call(
        _rwkv_pre_kernel,
        grid=(N_TOK // tm,),
        in_specs=[pl.BlockSpec((tm, w), lambda i: (i, r_blk)),
                  pl.BlockSpec((tm, w), lambda i: (i, k_blk)),
                  pl.BlockSpec((tm, w), lambda i: (i, v_blk)),
                  pl.BlockSpec((tm, 3 * LANES), lambda i: (i, lo_blk)),
                  full(LANES, 2 * w), full(LANES, 2 * w), full(LANES, w),
                  row(2 * w), row(2 * w), row(w), row(w), row(w)],
        out_specs=[pl.BlockSpec((tm, 2 * _SCAN_NJ, LANES), lambda i: (i, 0, 0)),
                   pl.BlockSpec((tm, w), lambda i: (i, 0)), pl.BlockSpec((tm, w), lambda i: (i, 0))],
        out_shape=[jax.ShapeDtypeStruct((N_TOK, 2 * _SCAN_NJ, LANES), F32),
                   jax.ShapeDtypeStruct((N_TOK, w), F32), jax.ShapeDtypeStruct((N_TOK, w), F32)],
        compiler_params=_params("parallel"),
        name="rwkv_pre",
    )(p, p, p, p, wup_bd, aup_bd, gup, w0, a0, k_k, k_a, r_k)


_SCAN_CHUNK = 32
_SCAN_UNROLL = 16


def _rwkv_first_sa(s_ref, sa_ref, kk, n_k):
    nv = s_ref.shape[1]
    chunk = min(_SCAN_CHUNK, nv)
    for c0 in range(0, nv, chunk):
        def body(k, acc):
            return acc + s_ref[k, c0:c0 + chunk, :] * kk(k)
        sa_ref[c0:c0 + chunk, :] = lax.fori_loop(0, n_k, body, jnp.zeros((chunk, LANES), F32), unroll=_SCAN_UNROLL)


def _rwkv_step(s_ref, sa_ref, kk_next, w, kd, b, r, v_at, n_k):
    nv = s_ref.shape[1]
    chunk = min(_SCAN_CHUNK, nv)
    ys = []
    for c0 in range(0, nv, chunk):
        sa = sa_ref[c0:c0 + chunk, :]
        vc = v_at(c0, chunk)

        def body(k, acc):
            y_acc, sa_acc = acc
            s_new = s_ref[k, c0:c0 + chunk, :] * w(k) - sa * b(k) + vc * kd(k)
            s_ref[k, c0:c0 + chunk, :] = s_new
            return y_acc + s_new * r(k), sa_acc + s_new * kk_next(k)

        zero = jnp.zeros((chunk, LANES), F32)
        y_acc, sa_acc = lax.fori_loop(0, n_k, body, (zero, zero), unroll=_SCAN_UNROLL)
        sa_ref[c0:c0 + chunk, :] = sa_acc
        ys.append(y_acc)
    return ys[0] if len(ys) == 1 else jnp.concatenate(ys, axis=0)


_CTX_TB = 32


def _load_scan_operands(kt_ref, t, slabs):
    for s, x in enumerate(slabs):
        xt = x.T
        kt_ref[t, 2 * s] = xt[:RWKV_HS]
        kt_ref[t, 2 * s + 1] = xt[RWKV_HS:]


def _rwkv_scan_ctx_kernel(x_ref, y_ref, st_ref, s_ref, sa_ref, kt_ref, ys_ref):
    d = pl.program_id(0)
    tb = pl.program_id(1)
    nh = RWKV_HEADS
    step_t = lambda i: jnp.where(d == 0, i, _CTX_TB - 1 - i)

    @pl.when(tb == 0)
    def _():
        s_ref[...] = jnp.zeros_like(s_ref)

    def load_t(t, carry):
        _load_scan_operands(kt_ref, t, [
            jnp.concatenate([x_ref[b, t, s * nh:(s + 1) * nh, :] for b in range(BATCH)], axis=0)
            for s in range(_SCAN_SLOTS)])
        return carry

    lax.fori_loop(0, _CTX_TB, load_t, 0, unroll=2)

    t0 = step_t(0)
    _rwkv_first_sa(s_ref, sa_ref, lambda k: kt_ref[t0, 0, pl.ds(k, 1), :], RWKV_HS)

    def step(i, carry):
        t = step_t(i)
        tn = step_t(jnp.minimum(i + 1, _CTX_TB - 1))
        row = lambda q, tt: (lambda k: kt_ref[tt, q, pl.ds(k, 1), :])
        ys_ref[t] = _rwkv_step(s_ref, sa_ref, row(0, tn), row(1, t), row(2, t), row(3, t), row(4, t),
                               lambda c0, n: kt_ref[t, 5, pl.ds(c0, n), :], RWKV_HS)
        return carry

    lax.fori_loop(0, _CTX_TB, step, 0)

    def store_t(i, carry):
        z = jnp.concatenate([ys_ref[2 * i], ys_ref[2 * i + 1]], axis=0).T
        z_odd = pltpu.roll(z, RWKV_HS, axis=1)
        for b in range(BATCH):
            y_ref[b, 2 * i] = z[b * nh:(b + 1) * nh, :RWKV_HS]
            y_ref[b, 2 * i + 1] = z_odd[b * nh:(b + 1) * nh, :RWKV_HS]
        return carry

    lax.fori_loop(0, _CTX_TB // 2, store_t, 0, unroll=2)

    @pl.when(tb == pl.num_programs(1) - 1)
    def _():
        st_ref[0] = s_ref[...]


def _rwkv_scan_ctx(op4):
    nt = SEQ // _CTX_TB
    hs = RWKV_HS
    tblk = lambda d, tb: jnp.where(d == 0, tb, nt - 1 - tb)
    return pl.pallas_call(
        _rwkv_scan_ctx_kernel,
        grid=(2, nt),
        in_specs=[pl.BlockSpec((BATCH, _CTX_TB, _SCAN_NJ, LANES), lambda d, tb: (0, tblk(d, tb), d, 0))],
        out_specs=[pl.BlockSpec((BATCH, _CTX_TB, RWKV_HEADS, hs), lambda d, tb: (0, tblk(d, tb), d, 0)),
                   pl.BlockSpec((1, hs, hs, LANES), lambda d, tb: (d, 0, 0, 0))],
        out_shape=[jax.ShapeDtypeStruct((BATCH, SEQ, 2 * RWKV_HEADS, hs), F32),
                   jax.ShapeDtypeStruct((2, hs, hs, LANES), F32)],
        scratch_shapes=[pltpu.VMEM((hs, hs, LANES), F32), pltpu.VMEM((hs, LANES), F32),
                        pltpu.VMEM((_CTX_TB, 2 * _SCAN_SLOTS, hs, LANES), F32),
                        pltpu.VMEM((_CTX_TB, hs, LANES), F32)],
        compiler_params=_params("parallel", "arbitrary"),
        name="rwkv_scan_ctx",
    )(op4)


_LAT_TB = 32
_LAT_VSPLIT = 4
_LAT_STATES = 2 * DEC_BATCH * RWKV_HEADS
_LAT_VROWS = RWKV_HS // _LAT_VSPLIT


def _rwkv_scan_lat_kernel(xf0_ref, xf1_ref, xb0_ref, xb1_ref, s0_ref, yf_ref, yb_ref,
                          s_ref, sa_ref, kt_ref, v_ref, ys_ref):
    @pl.when(pl.program_id(0) == 0)
    def _():
        s_ref[...] = s0_ref[...]

    group = lax.broadcasted_iota(jnp.int32, (_LAT_VROWS, LANES), 1) // _LAT_STATES
    nh = RWKV_HEADS

    def load_t(t, carry):
        tr = _LAT_TB - 1 - t
        for s in range(_SCAN_SLOTS):
            heads = slice(s * nh, (s + 1) * nh)
            x = jnp.concatenate([xf0_ref[0, t, heads, :], xf1_ref[0, t, heads, :],
                                 xb0_ref[0, tr, heads, :], xb1_ref[0, tr, heads, :]], axis=0)
            xt = jnp.concatenate([x] * _LAT_VSPLIT, axis=0).T
            kt_ref[t, 2 * s] = xt[:RWKV_HS]
            if s < _SCAN_SLOTS - 1:
                kt_ref[t, 2 * s + 1] = xt[RWKV_HS:]
            else:
                v = jnp.zeros((_LAT_VROWS, LANES), F32)
                for g in range(_LAT_VSPLIT):
                    r0 = RWKV_HS + g * _LAT_VROWS
                    v = jnp.where(group == g, xt[r0:r0 + _LAT_VROWS, :], v)
                v_ref[t] = v
        return carry

    lax.fori_loop(0, _LAT_TB, load_t, 0, unroll=4)

    _rwkv_first_sa(s_ref, sa_ref, lambda k: kt_ref[0, 0, pl.ds(k, 1), :], RWKV_HS)

    def step(t, carry):
        tn = jnp.minimum(t + 1, _LAT_TB - 1)
        row = lambda q, tt: (lambda k: kt_ref[tt, q, pl.ds(k, 1), :])
        ys_ref[t] = _rwkv_step(s_ref, sa_ref, row(0, tn), row(1, t), row(2, t), row(3, t), row(4, t),
                               lambda c0, n: v_ref[t, pl.ds(c0, n), :], RWKV_HS)
        return carry

    lax.fori_loop(0, _LAT_TB, step, 0)

    def store_t(i, carry):
        rows = [jnp.where(group == g, ys_ref[2 * i + j], 0.0) for j in range(2) for g in range(_LAT_VSPLIT)]
        z = jnp.concatenate(rows, axis=0).T
        y = (z[0:_LAT_STATES] + z[_LAT_STATES:2 * _LAT_STATES]
             + z[2 * _LAT_STATES:3 * _LAT_STATES] + z[3 * _LAT_STATES:4 * _LAT_STATES])
        y_odd = pltpu.roll(y, RWKV_HS, axis=1)
        half = _LAT_STATES // 2
        for b in range(DEC_BATCH):
            rows_f = slice(b * nh, (b + 1) * nh)
            rows_b = slice(half + b * nh, half + (b + 1) * nh)
            yf_ref[b, 2 * i] = y[rows_f, :RWKV_HS]
            yf_ref[b, 2 * i + 1] = y_odd[rows_f, :RWKV_HS]
            yb_ref[b, _LAT_TB - 1 - 2 * i] = y[rows_b, :RWKV_HS]
            yb_ref[b, _LAT_TB - 2 - 2 * i] = y_odd[rows_b, :RWKV_HS]
        return carry

    lax.fori_loop(0, _LAT_TB // 2, store_t, 0, unroll=4)


def _rwkv_scan_lat(op4, s0):
    hs = RWKV_HS
    nv = _LAT_VROWS
    nt = DEC_SEQ // _LAT_TB
    per_seq = SEQ // _LAT_TB
    first = N_CTX // SEQ

    def x_spec(b, d):
        tblk = (lambda tb: tb) if d == 0 else (lambda tb: nt - 1 - tb)
        return pl.BlockSpec((1, _LAT_TB, _SCAN_NJ, LANES),
                            lambda tb: (first + b * (DEC_SEQ // SEQ) + tblk(tb) // per_seq, tblk(tb) % per_seq, d, 0))

    y_shape = jax.ShapeDtypeStruct((DEC_BATCH, DEC_SEQ, RWKV_HEADS, hs), F32)
    y_blk = (DEC_BATCH, _LAT_TB, RWKV_HEADS, hs)
    return pl.pallas_call(
        _rwkv_scan_lat_kernel,
        grid=(nt,),
        in_specs=[x_spec(0, 0), x_spec(1, 0), x_spec(0, 1), x_spec(1, 1),
                  pl.BlockSpec((hs, nv, LANES), lambda tb: (0, 0, 0))],
        out_specs=[pl.BlockSpec(y_blk, lambda tb: (0, tb, 0, 0)),
                   pl.BlockSpec(y_blk, lambda tb: (0, nt - 1 - tb, 0, 0))],
        out_shape=[y_shape, y_shape],
        scratch_shapes=[pltpu.VMEM((hs, nv, LANES), F32), pltpu.VMEM((nv, LANES), F32),
                        pltpu.VMEM((_LAT_TB, 2 * _SCAN_SLOTS - 1, hs, LANES), F32),
                        pltpu.VMEM((_LAT_TB, nv, LANES), F32), pltpu.VMEM((_LAT_TB, nv, LANES), F32)],
        compiler_params=_params("arbitrary"),
        name="rwkv_scan_lat",
    )(op4, op4, op4, op4, s0)


def _rwkv_post_kernel(yc_ref, ylf_ref, ylb_ref, bonus_ref, v_ref, g_ref, gn_ref, o_ref):
    def finish(head_sum):
        y = jnp.concatenate([head_sum(h) for h in range(RWKV_HEADS)], axis=1)
        y = y * lax.rsqrt(_seg64_sum_wide(y * y) * (1.0 / RWKV_HS) + EPS) * gn_ref[...]
        o_ref[...] = ((y + bonus_ref[...] * v_ref[...]) * g_ref[...]).astype(o_ref.dtype)

    @pl.when(pl.program_id(0) < N_CTX // SEQ)
    def _():
        finish(lambda h: yc_ref[0, :, h, :] + yc_ref[0, :, RWKV_HEADS + h, :])

    @pl.when(pl.program_id(0) >= N_CTX // SEQ)
    def _():
        finish(lambda h: ylf_ref[0, :, h, :] + ylb_ref[0, :, h, :])


def _rwkv_post(y_ctx, y_lat_f, y_lat_b, bonus, p, v_blk, g, gn):
    tm = SEQ
    w = RWKV_W
    hs = RWKV_HS
    n_ctx = N_CTX // SEQ
    lat = lambda y: y.reshape(N_LAT // SEQ, SEQ, RWKV_HEADS, hs)
    spec = pl.BlockSpec((tm, w), lambda i: (i, 0))
    lat_spec = pl.BlockSpec((1, SEQ, RWKV_HEADS, hs), lambda i: (jnp.maximum(i - n_ctx, 0), 0, 0, 0))
    return pl.pallas_call(
        _rwkv_post_kernel,
        grid=(N_TOK // tm,),
        in_specs=[pl.BlockSpec((1, SEQ, 2 * RWKV_HEADS, hs), lambda i: (jnp.minimum(i, n_ctx - 1), 0, 0, 0)),
                  lat_spec, lat_spec, spec, pl.BlockSpec((tm, w), lambda i: (i, v_blk)), spec,
                  pl.BlockSpec((1, w), lambda i: (0, 0))],
        out_specs=spec,
        out_shape=jax.ShapeDtypeStruct((N_TOK, w), BF16),
        compiler_params=_params("parallel"),
        name="rwkv_post",
    )(y_ctx, lat(y_lat_f), lat(y_lat_b), bonus, p, g, gn)


def _value_split_layout(x):
    lead = x.shape[:-2]
    n = len(lead)
    x = x.reshape(lead + (_LAT_STATES, _LAT_VSPLIT, _LAT_VROWS))
    return jnp.transpose(x, tuple(range(n)) + (n + 2, n + 1, n)).reshape(lead + (_LAT_VROWS, LANES))


def _even_layer(x, mod, g_mix, w_in, q_norm, kv_norm, w_uq, w_ukv, qn, kn, ret_decay, ret_gn,
                cache_ckv, cache_krope, state_ret, tabs_m):
    cq, ckv, krope, rq, rk, rv, rg = jnp.split(w_in, np.cumsum(
        (MLA_Q_RANK, MLA_KV_RANK, MLA_ROPE, RET_HEADS * RET_DK, RET_HEADS * RET_DK, RET_HEADS * RET_DV))[:].tolist(),
        axis=1)
    z = lambda n: jnp.zeros((D_MODEL, n), F32)
    w_p = jnp.concatenate([cq, ckv, z(MLA_NOPE), krope, z(LANES - MLA_QK), rq, rk, rv, rg], axis=1)
    p = _inproj(x, g_mix, mod, w_p, jnp.zeros((1, w_p.shape[1]), F32), 512, None)
    CKV_BLK, KR_BLK, RQ_BLK, RK_BLK, RV_BLK, RG_BLK = 2, 3, 4, 6, 8, 12

    def head_pad(w, n_head, d_head, c0, c1):
        w = w.reshape(w.shape[0], n_head, d_head)[:, :, c0:c1]
        return jnp.pad(w, ((0, 0), (0, 0), (0, LANES - (c1 - c0)))).reshape(w.shape[0], n_head * LANES)

    w_uq_p = head_pad(w_uq, MLA_HEADS, MLA_QK, 0, MLA_QK)
    wk_p = head_pad(w_ukv, MLA_HEADS, MLA_NOPE + MLA_V, 0, MLA_NOPE)
    wv_p = head_pad(w_ukv, MLA_HEADS, MLA_NOPE + MLA_V, MLA_NOPE, MLA_NOPE + MLA_V)
    qn_p = jnp.pad(qn, (0, LANES - MLA_QK))[None]
    kn_p = jnp.pad(kn, (0, LANES - MLA_QK))[None]

    q = _mla_q(p, q_norm[None], w_uq_p, qn_p, tabs_m)
    k, v, ckvn = _mla_kv(p, CKV_BLK, p, KR_BLK, kv_norm[None], wk_p, wv_p, kn_p, tabs_m, N_TOK, True)

    n_c = DEC_BATCH * PAST_LEN
    kr_c = jnp.pad(cache_krope.reshape(n_c, MLA_ROPE), ((0, 0), (MLA_NOPE, LANES - MLA_QK)))
    k_c, v_c, _ = _mla_kv(cache_ckv.reshape(n_c, MLA_KV_RANK), 0, kr_c, 0, kv_norm[None], wk_p, wv_p, kn_p,
                          tabs_m, n_c, False)

    hw = MLA_HEADS * LANES

    def with_cache(own, cache):
        own = own[N_CTX:].reshape(DEC_BATCH, DEC_SEQ, hw)
        return jnp.concatenate([own, cache.reshape(DEC_BATCH, PAST_LEN, hw)], 1).reshape(-1, hw)

    o_ctx = _mla_attn(q, k, v, BATCH, SEQ, SEQ, 0, SEQ, MLA_HEADS // 2)
    o_lat = _mla_attn(q, with_cache(k, k_c), with_cache(v, v_c), DEC_BATCH, DEC_SEQ, DEC_SEQ + PAST_LEN, N_CTX, 256,
                      1)

    log_g = -_softplus(-ret_decay)
    gn = ret_gn[None]
    s0_ctx = jnp.zeros((BATCH, 2, RET_HEADS * RET_DK, RET_DV), F32)
    r_ctx, st_ctx = _retention(log_g, p, RQ_BLK, RK_BLK, RV_BLK, RG_BLK, s0_ctx, gn, BATCH, SEQ, 0, SEQ, True)
    s0_lat = state_ret.reshape(DEC_BATCH, 2, RET_HEADS * RET_DK, RET_DV)
    r_lat, _ = _retention(log_g, p, RQ_BLK, RK_BLK, RV_BLK, RG_BLK, s0_lat, gn, DEC_BATCH, DEC_SEQ, N_CTX, 256,
                          False)

    mix = [(o_ctx, o_lat), (r_ctx, r_lat)]
    new_ckv = ckvn[:N_CTX].reshape(BATCH, SEQ, MLA_KV_RANK)
    new_krope = p[:N_CTX, KR_BLK * LANES + MLA_NOPE:KR_BLK * LANES + MLA_QK]
    new_krope = new_krope.reshape(BATCH, SEQ, MLA_ROPE)
    new_ret = st_ctx.reshape(BATCH, 2, RET_HEADS, RET_DK, RET_DV)
    return mix, new_ckv, new_krope, new_ret


def _odd_layer(x, mod, g_mix, w_in, qn, kn, lam, diff_gn, mu, w0, w_up, a0, a_up, g_up, k_k, k_a, r_k, gn,
               cache_k, cache_v, state_rwkv, tabs_d, lam_init):
    w_p = w_in
    n_in = w_p.shape[1]
    mu_full = jnp.concatenate([jnp.zeros((3 * DIFF_W,), F32), mu])[None]
    p = _inproj(x, g_mix, mod, w_p, mu_full, 384, (3 * DIFF_W) // 384)
    DV_BLK, R_BLK, K_BLK, V_BLK = 2, 3, 4, 5
    LO_BLK = (6 * RWKV_W) // (3 * LANES)

    qn_p = jnp.tile(qn, 2)[None]
    kn_p = jnp.tile(kn, 2)[None]
    q, k = _diff_qk(p, qn_p, kn_p, tabs_d)

    n_c = DEC_BATCH * PAST_LEN
    k_c = cache_k.reshape(DEC_BATCH, PAST_LEN, DIFF_W)
    v_c = cache_v.reshape(DEC_BATCH, PAST_LEN, DIFF_W)

    def with_cache(own_lat, cache):
        return jnp.concatenate([own_lat.reshape(DEC_BATCH, DEC_SEQ, DIFF_W), cache], 1).reshape(-1, DIFF_W)

    dgn = diff_gn[None]
    dv_cols = slice(DV_BLK * DIFF_W, (DV_BLK + 1) * DIFF_W)
    o_ctx = _diff_attn(lam, q, k, p, DV_BLK * DIFF_HEADS, dgn, BATCH, SEQ, SEQ, 0, SEQ, lam_init, DIFF_HEADS)
    o_lat = _diff_attn(lam, q, with_cache(k[N_CTX:], k_c), with_cache(p[N_CTX:, dv_cols], v_c), 0, dgn,
                       DEC_BATCH, DEC_SEQ, DEC_SEQ + PAST_LEN, N_CTX, 256, lam_init, 1)

    zero = jnp.zeros((RWKV_W_LORA, RWKV_W), F32)
    wup_bd = jnp.concatenate([jnp.concatenate([w_up[0], zero], 1), jnp.concatenate([zero, w_up[1]], 1)], 0)
    aup_bd = jnp.concatenate([jnp.concatenate([a_up[0], zero], 1), jnp.concatenate([zero, a_up[1]], 1)], 0)
    op, g, bonus = _rwkv_pre(p, R_BLK, K_BLK, V_BLK, LO_BLK, wup_bd, aup_bd, g_up, w0.reshape(1, -1),
                             a0.reshape(1, -1), k_k[None], k_a[None], r_k.reshape(1, -1))
    op4 = op.reshape(N_TOK // SEQ, SEQ, 2 * _SCAN_NJ, LANES)
    y_ctx, st_ctx = _rwkv_scan_ctx(op4)
    s0_lat = jnp.transpose(state_rwkv, (4, 1, 0, 2, 3)).reshape(RWKV_HS, _LAT_STATES, RWKV_HS)
    y_lat_f, y_lat_b = _rwkv_scan_lat(op4, _value_split_layout(s0_lat))
    rw_o = _rwkv_post(y_ctx, y_lat_f, y_lat_b, bonus, p, V_BLK, g, gn[None])

    mix = [(o_ctx, o_lat), rw_o]
    new_dk = k[:N_CTX].reshape(BATCH, SEQ, DIFF_HEADS, 2, DIFF_DH)
    new_dv = p[:N_CTX, dv_cols].reshape(BATCH, SEQ, DIFF_HEADS, 2 * DIFF_DH)
    new_rwkv = jnp.transpose(st_ctx.reshape(2, RWKV_HS, RWKV_HS, BATCH, RWKV_HEADS), (3, 0, 4, 2, 1))
    return mix, new_dk, new_dv, new_rwkv


def kernel(x_prompt, x_sample, cache_mla_ckv, cache_mla_krope, state_ret, cache_diff_k, cache_diff_v, state_rwkv,
           c, c_ctx, ada_w, ada_b, norm_mix_g, norm_ffn_g, w_out, ffn_up, ffn_conv_w, ffn_conv_b, ffn_down,
           a_w_in, mla_q_norm, mla_kv_norm, mla_w_uq, mla_w_ukv, mla_qn, mla_kn, ret_decay, ret_gn,
           b_w_in, diff_qn, diff_kn, diff_lam, diff_gn, rwkv_mu, rwkv_w0, rwkv_w_up, rwkv_a0, rwkv_a_up,
           rwkv_g_up, rwkv_k_k, rwkv_k_a, rwkv_r_k, rwkv_gn):
    x = jnp.concatenate([x_prompt.reshape(N_CTX, D_MODEL), x_sample.reshape(N_LAT, D_MODEL)], 0)
    cond8 = jnp.pad(jnp.concatenate([c_ctx[None], c], 0), ((0, 8 - N_GROUPS), (0, 0)))
    mod = _modulation(cond8, ada_w, ada_b)

    tabs_m = _rope_tables(MLA_ROPE, (MLA_NOPE,))
    tabs_d = _rope_tables(DIFF_DH, (0, DIFF_DH))

    outs = {}
    for l in range(DEPTH):
        j = l // 2
        g_mix = norm_mix_g[l][None]
        if l % 2 == 0:
            mix, outs["ckv"], outs["krope"], outs["ret"] = _even_layer(
                x, mod[l], g_mix, a_w_in[j], mla_q_norm[j], mla_kv_norm[j], mla_w_uq[j], mla_w_ukv[j], mla_qn[j],
                mla_kn[j], ret_decay[j], ret_gn[j], cache_mla_ckv[:, j], cache_mla_krope[:, j], state_ret[:, j],
                tabs_m)
        else:
            lam_init = 0.8 - 0.6 * math.exp(-0.3 * l)
            mix, outs["dk"], outs["dv"], outs["rwkv"] = _odd_layer(
                x, mod[l], g_mix, b_w_in[j], diff_qn[j], diff_kn[j], diff_lam[j], diff_gn[j], rwkv_mu[j],
                rwkv_w0[j], rwkv_w_up[j], rwkv_a0[j], rwkv_a_up[j], rwkv_g_up[j], rwkv_k_k[j], rwkv_k_a[j],
                rwkv_r_k[j], rwkv_gn[j], cache_diff_k[:, j], cache_diff_v[:, j], state_rwkv[:, j], tabs_d, lam_init)
        x = _resid_proj(mix, w_out[l], x, mod[l], 2)
        act = _ffn_up(x, norm_ffn_g[l][None], mod[l], ffn_up[l], ffn_conv_w[l], ffn_conv_b[l])
        x = _resid_proj([act], ffn_down[l], x, mod[l], 5, split_out=(l == DEPTH - 1))

    y_prompt = x[0].reshape(BATCH, SEQ, D_MODEL)
    y_sample = x[1].reshape(DEC_BATCH, DEC_SEQ, D_MODEL)
    return (y_prompt, y_sample, outs["ckv"][:, None], outs["krope"][:, None], outs["ret"][:, None],
            outs["dk"][:, None], outs["dv"][:, None], outs["rwkv"][:, None])
```

```python
import functools
import math

import numpy as np
import jax
import jax.numpy as jnp
from jax import lax
from jax.experimental import pallas as pl
from jax.experimental.pallas import tpu as pltpu

D_MODEL = 1024
BATCH = 16
SEQ = 256
DEPTH = 2
DEC_BATCH = 2
DEC_SEQ = 1024
PAST_LEN = 512
GRID_W = 64
EPS = 1e-6
ROPE_BASE = 10000.0

MLA_HEADS = 8
MLA_Q_RANK = 256
MLA_KV_RANK = 128
MLA_NOPE = 64
MLA_ROPE = 32
MLA_V = 64
MLA_QK = MLA_NOPE + MLA_ROPE
RET_HEADS = 4
RET_DK = 64
RET_DV = 128
DIFF_HEADS = 4
DIFF_DH = 64
DIFF_W = DIFF_HEADS * 2 * DIFF_DH
RWKV_HEADS = 8
RWKV_HS = 64
RWKV_W = RWKV_HEADS * RWKV_HS
RWKV_W_LORA = 64
RWKV_A_LORA = 64
RWKV_G_LORA = 128
D_FF = 2816

N_CTX = BATCH * SEQ
N_LAT = DEC_BATCH * DEC_SEQ
N_TOK = N_CTX + N_LAT
N_GROUPS = 1 + DEC_BATCH

LANES = 128
VMEM_LIMIT = 56 * 1024 * 1024

_PREC = lax.Precision.HIGHEST
F32 = jnp.float32


def _dot_tn(a, b):
    return lax.dot_general(a, b, (((0,), (0,)), ((), ())), precision=_PREC, preferred_element_type=F32)


BF16 = jnp.bfloat16


def _dot_bf16(a, b):
    return jnp.dot(a.astype(BF16), b.astype(BF16), preferred_element_type=F32)


def _dot_nt_bf16(a, b):
    return lax.dot_general(a.astype(BF16), b.astype(BF16), (((1,), (1,)), ((), ())), preferred_element_type=F32)


def _params(*sem):
    return pltpu.CompilerParams(dimension_semantics=sem, vmem_limit_bytes=VMEM_LIMIT)


def _sigmoid(x):
    return 1.0 / (1.0 + jnp.exp(-x))


def _silu(x):
    return x * _sigmoid(x)


def _softplus(x):
    return jnp.maximum(x, 0.0) + jnp.log(1.0 + jnp.exp(-jnp.abs(x)))


def _rms(x, n):
    return x * lax.rsqrt(jnp.sum(x * x, axis=-1, keepdims=True) * (1.0 / n) + EPS)


def _lane_lo(shape):
    return lax.broadcasted_iota(jnp.int32, shape, len(shape) - 1) < 64


def _seg64_sum(x):
    lo = _lane_lo(x.shape)
    s_lo = jnp.sum(jnp.where(lo, x, 0.0), axis=-1, keepdims=True)
    s_hi = jnp.sum(jnp.where(lo, 0.0, x), axis=-1, keepdims=True)
    return jnp.where(lo, s_lo, s_hi)


_SUBLANES = 8


def _seq_neighbours(p, tile, tile_rows):
    is_ctx = tile * tile_rows < N_CTX
    sub = lax.broadcasted_iota(jnp.int32, (_SUBLANES, 1), 0)

    def shifted(rolled, edge_sublane, group_of_seq):
        pieces, start = [], 0
        for q in range(tile_rows // SEQ):
            g0 = q * SEQ + group_of_seq
            outer = (q == 0) if group_of_seq == 0 else (q == tile_rows // SEQ - 1)
            edge = (sub == edge_sublane) if outer else ((sub == edge_sublane) & is_ctx)
            pieces += [rolled[start:g0], jnp.where(edge, 0.0, rolled[g0:g0 + _SUBLANES])]
            start = g0 + _SUBLANES
        pieces.append(rolled[start:])
        return jnp.concatenate([x for x in pieces if x.shape[0]], axis=0)

    prev = shifted(pltpu.roll(p, 1, axis=0), 0, 0)
    nxt = shifted(pltpu.roll(p, tile_rows - 1, axis=0), _SUBLANES - 1, SEQ - _SUBLANES)
    return prev, nxt


def _group_of_tile(i, tile_rows):
    row = i * tile_rows
    return jnp.where(row < N_CTX, 0, 1 + (row - N_CTX) // DEC_SEQ)


def _modulation_kernel(c_ref, w_ref, b_ref, o_ref):
    o_ref[0] = _dot_bf16(_silu(c_ref[...]), w_ref[0]) + b_ref[0]


def _modulation(cond8, ada_w, ada_b):
    tn = 512
    n = 6 * D_MODEL
    out = pl.pallas_call(
        _modulation_kernel,
        grid=(DEPTH, n // tn),
        in_specs=[pl.BlockSpec((8, D_MODEL), lambda l, j: (0, 0)),
                  pl.BlockSpec((1, D_MODEL, tn), lambda l, j: (l, 0, j)),
                  pl.BlockSpec((1, 1, tn), lambda l, j: (l, 0, j))],
        out_specs=pl.BlockSpec((1, 8, tn), lambda l, j: (l, 0, j)),
        out_shape=jax.ShapeDtypeStruct((DEPTH, 8, n), F32),
        compiler_params=_params("parallel", "parallel"),
        name="modulation",
    )(cond8, ada_w, ada_b.reshape(DEPTH, 1, n))
    m = out[:, :N_GROUPS].reshape(DEPTH, N_GROUPS, 6, D_MODEL)
    return jnp.pad(m, ((0, 0), (0, 0), (0, 2), (0, 0)))


_TM_SEQ = 1024


def _norm_mod(x, g, mod, off):
    return _rms(x, D_MODEL) * g * (1.0 + mod[off + 1:off + 2, :]) + mod[off:off + 1, :]


def _inproj_kernel(x_ref, g_ref, mod_ref, w_ref, mu_ref, o_ref, h_ref, *, shift_from):
    i = pl.program_id(0)

    @pl.when(pl.program_id(1) == 0)
    def _():
        h_ref[...] = _norm_mod(x_ref[...], g_ref[...], mod_ref[0], 0).astype(BF16)

    p = _dot_bf16(h_ref[...], w_ref[...])
    if shift_from is None:
        o_ref[...] = p
    else:
        @pl.when(pl.program_id(1) < shift_from)
        def _():
            o_ref[...] = p

        @pl.when(pl.program_id(1) >= shift_from)
        def _():
            prev, nxt = _seq_neighbours(p, i, _TM_SEQ)
            o_ref[...] = p + (0.5 * (prev + nxt) - p) * mu_ref[...]


def _inproj(x, g, mod, w, mu, tn, shift_from):
    n = w.shape[1]
    tm = _TM_SEQ
    return pl.pallas_call(
        functools.partial(_inproj_kernel, shift_from=shift_from),
        grid=(N_TOK // tm, n // tn),
        in_specs=[pl.BlockSpec((tm, D_MODEL), lambda i, j: (i, 0)),
                  pl.BlockSpec((1, D_MODEL), lambda i, j: (0, 0)),
                  pl.BlockSpec((1, 8, D_MODEL), lambda i, j: (_group_of_tile(i, tm), 0, 0)),
                  pl.BlockSpec((D_MODEL, tn), lambda i, j: (0, j)),
                  pl.BlockSpec((1, tn), lambda i, j: (0, j))],
        out_specs=pl.BlockSpec((tm, tn), lambda i, j: (i, j)),
        out_shape=jax.ShapeDtypeStruct((N_TOK, n), F32),
        scratch_shapes=[pltpu.VMEM((tm, D_MODEL), BF16)],
        compiler_params=_params("parallel", "arbitrary"),
        name="inproj" if shift_from is None else "inproj_shift",
    )(x, g, mod, w, mu)


def _resid_kernel(*refs, gate_row, widths, split, split_out):
    n_in = sum(2 if sp else 1 for sp in split)
    a_refs = refs[:n_in]
    w_ref, x_ref, mod_ref = refs[n_in:n_in + 3]
    o_ref = refs[n_in + 3:-1] if split_out else refs[n_in + 3]
    a_bf_ref = refs[-1]
    i = pl.program_id(0)

    @pl.when(pl.program_id(1) == 0)
    def _():
        k0, r = 0, 0
        for width, sp in zip(widths, split):
            cols = slice(k0, k0 + width)
            if sp:
                ctx_ref, lat_ref = a_refs[r], a_refs[r + 1]

                @pl.when(i < N_CTX // _TM_SEQ)
                def _():
                    a_bf_ref[:, cols] = ctx_ref[...].astype(BF16)

                @pl.when(i >= N_CTX // _TM_SEQ)
                def _():
                    a_bf_ref[:, cols] = lat_ref[...].astype(BF16)
            else:
                a_bf_ref[:, cols] = a_refs[r][...].astype(BF16)
            k0 += width
            r += 2 if sp else 1

    y = x_ref[...] + mod_ref[0, gate_row:gate_row + 1, :] * _dot_bf16(a_bf_ref[...], w_ref[...])
    if not split_out:
        o_ref[...] = y
    else:
        ctx_o_ref, lat_o_ref = o_ref

        @pl.when(i < N_CTX // _TM_SEQ)
        def _():
            ctx_o_ref[...] = y

        @pl.when(i >= N_CTX // _TM_SEQ)
        def _():
            lat_o_ref[...] = y


def _resid_proj(acts, w, x, mod, gate_row, split_out=False):
    tm, tn = _TM_SEQ, 256
    n_ctx = N_CTX // tm
    nj = D_MODEL // tn
    split = [isinstance(a, (tuple, list)) for a in acts]
    widths = [a[0].shape[1] if sp else a.shape[1] for a, sp in zip(acts, split)]
    k = sum(widths)
    in_specs, operands = [], []
    for a, width, sp in zip(acts, widths, split):
        if sp:
            in_specs += [pl.BlockSpec((tm, width), lambda i, j: (jnp.minimum(i, n_ctx - 1), 0)),
                         pl.BlockSpec((tm, width), lambda i, j: (jnp.maximum(i - n_ctx, 0), 0))]
            operands += list(a)
        else:
            in_specs.append(pl.BlockSpec((tm, width), lambda i, j: (i, 0)))
            operands.append(a)
    if split_out:
        out_specs = [pl.BlockSpec((tm, tn), lambda i, j: (jnp.minimum(i, n_ctx - 1), jnp.where(i < n_ctx, j, nj - 1))),
                     pl.BlockSpec((tm, tn), lambda i, j: (jnp.maximum(i - n_ctx, 0), jnp.where(i < n_ctx, 0, j)))]
        out_shape = [jax.ShapeDtypeStruct((N_CTX, D_MODEL), F32), jax.ShapeDtypeStruct((N_LAT, D_MODEL), F32)]
        sem = ("arbitrary", "arbitrary")
    else:
        out_specs = pl.BlockSpec((tm, tn), lambda i, j: (i, j))
        out_shape = jax.ShapeDtypeStruct((N_TOK, D_MODEL), F32)
        sem = ("parallel", "arbitrary")
    return pl.pallas_call(
        functools.partial(_resid_kernel, gate_row=gate_row, widths=tuple(widths), split=tuple(split),
                          split_out=split_out),
        grid=(N_TOK // tm, nj),
        in_specs=in_specs
        + [pl.BlockSpec((k, tn), lambda i, j: (0, j)),
           pl.BlockSpec((tm, tn), lambda i, j: (i, j)),
           pl.BlockSpec((1, 8, tn), lambda i, j: (_group_of_tile(i, tm), 0, j))],
        out_specs=out_specs,
        out_shape=out_shape,
        scratch_shapes=[pltpu.VMEM((tm, k), BF16)],
        compiler_params=_params(*sem),
        name="resid_proj",
    )(*operands, w, x, mod)


def _ffn_up_kernel(x_ref, g_ref, mod_ref, wa_ref, wb_ref, cwa_ref, cwb_ref, cba_ref, cbb_ref, o_ref, h_ref):
    i = pl.program_id(0)

    @pl.when(pl.program_id(1) == 0)
    def _():
        h_ref[...] = _norm_mod(x_ref[...], g_ref[...], mod_ref[0], 3).astype(BF16)

    h = h_ref[...]

    def conv(w_ref, cw_ref, cb_ref):
        u = _dot_bf16(h, w_ref[...])
        prev, nxt = _seq_neighbours(u, i, _TM_SEQ)
        return prev * cw_ref[0:1, :] + u * cw_ref[1:2, :] + nxt * cw_ref[2:3, :] + cb_ref[...]

    o_ref[...] = (_silu(conv(wa_ref, cwa_ref, cba_ref)) * conv(wb_ref, cwb_ref, cbb_ref)).astype(o_ref.dtype)


def _ffn_up(x, g, mod, up, cw, cb):
    tm, tn = _TM_SEQ, 256
    nb = D_FF // tn
    cb = cb.reshape(1, 2 * D_FF)
    return pl.pallas_call(
        _ffn_up_kernel,
        grid=(N_TOK // tm, nb),
        in_specs=[pl.BlockSpec((tm, D_MODEL), lambda i, j: (i, 0)),
                  pl.BlockSpec((1, D_MODEL), lambda i, j: (0, 0)),
                  pl.BlockSpec((1, 8, D_MODEL), lambda i, j: (_group_of_tile(i, tm), 0, 0)),
                  pl.BlockSpec((D_MODEL, tn), lambda i, j: (0, j)),
                  pl.BlockSpec((D_MODEL, tn), lambda i, j: (0, j + nb)),
                  pl.BlockSpec((3, tn), lambda i, j: (0, j)),
                  pl.BlockSpec((3, tn), lambda i, j: (0, j + nb)),
                  pl.BlockSpec((1, tn), lambda i, j: (0, j)),
                  pl.BlockSpec((1, tn), lambda i, j: (0, j + nb))],
        out_specs=pl.BlockSpec((tm, tn), lambda i, j: (i, j)),
        out_shape=jax.ShapeDtypeStruct((N_TOK, D_FF), BF16),
        scratch_shapes=[pltpu.VMEM((tm, D_MODEL), BF16)],
        compiler_params=_params("parallel", "arbitrary"),
        name="ffn_up",
    )(x, g, mod, up, up, cw, cw, cb, cb)


_ROPE_TM = 512


def _rope(y, c, s1, s2):
    return y * c + pltpu.roll(y, 1, axis=1) * s1 + pltpu.roll(y, LANES - 1, axis=1) * s2


def _rope_tables(rot_dim, lane_offsets):
    t = np.arange(DEC_SEQ)
    row, col = t // GRID_W, t % GRID_W
    n_freq = rot_dim // 4
    inv = ROPE_BASE ** (-np.arange(n_freq, dtype=np.float64) / n_freq)
    ang = np.concatenate([row[:, None] * inv, col[:, None] * inv], -1)
    cos, sin = np.cos(ang), np.sin(ang)
    n = _ROPE_TM + DEC_SEQ
    c, s1, s2 = np.ones((n, LANES)), np.zeros((n, LANES)), np.zeros((n, LANES))
    for a in lane_offsets:
        even = a + 2 * np.arange(rot_dim // 2)
        c[_ROPE_TM:, even] = cos
        c[_ROPE_TM:, even + 1] = cos
        s1[_ROPE_TM:, even + 1] = sin
        s2[_ROPE_TM:, even] = -sin
    return tuple(jnp.asarray(x, F32) for x in (c, s1, s2))


def _rope_block(i):
    row = i * _ROPE_TM
    return jnp.where(row < N_CTX, 0, 1 + ((row - N_CTX) % DEC_SEQ) // _ROPE_TM)


def _mla_q_kernel(cq_ref, qnorm_ref, w_ref, qn_ref, c_ref, s1_ref, s2_ref, o_ref):
    xn = _rms(cq_ref[...], MLA_Q_RANK) * qnorm_ref[...]
    y = _dot_bf16(xn, w_ref[...])
    c, s1, s2 = c_ref[...], s1_ref[...], s2_ref[...]
    for h in range(MLA_HEADS):
        yh = y[:, h * LANES:(h + 1) * LANES]
        yh = _rms(yh, MLA_QK) * qn_ref[...]
        o_ref[:, h * LANES:(h + 1) * LANES] = _rope(yh, c, s1, s2)


def _mla_q(p, q_norm, w_uq_p, qn_p, tabs):
    tm = _ROPE_TM
    hw = MLA_HEADS * LANES
    tab_spec = pl.BlockSpec((tm, LANES), lambda i: (_rope_block(i), 0))
    return pl.pallas_call(
        _mla_q_kernel,
        grid=(N_TOK // tm,),
        in_specs=[pl.BlockSpec((tm, MLA_Q_RANK), lambda i: (i, 0)),
                  pl.BlockSpec((1, MLA_Q_RANK), lambda i: (0, 0)),
                  pl.BlockSpec((MLA_Q_RANK, hw), lambda i: (0, 0)),
                  pl.BlockSpec((1, LANES), lambda i: (0, 0)),
                  tab_spec, tab_spec, tab_spec],
        out_specs=pl.BlockSpec((tm, hw), lambda i: (i, 0)),
        out_shape=jax.ShapeDtypeStruct((N_TOK, hw), F32),
        compiler_params=_params("parallel"),
        name="mla_q",
    )(p, q_norm, w_uq_p, qn_p, *tabs)


def _mla_kv_kernel(ckv_ref, kr_ref, kvn_ref, wk_ref, wv_ref, kn_ref, c_ref, s1_ref, s2_ref,
                   k_ref, v_ref, ckvn_ref, *, norm_ckv):
    ckv = ckv_ref[...]
    if norm_ckv:
        ckv = _rms(ckv, MLA_KV_RANK) * kvn_ref[...]
    ckvn_ref[...] = ckv
    ckv_bf = ckv.astype(BF16)
    kk = _dot_bf16(ckv_bf, wk_ref[...])
    v_ref[...] = _dot_bf16(ckv_bf, wv_ref[...])
    kr = kr_ref[...]
    c, s1, s2 = c_ref[...], s1_ref[...], s2_ref[...]
    for h in range(MLA_HEADS):
        kh = kk[:, h * LANES:(h + 1) * LANES] + kr
        kh = _rms(kh, MLA_QK) * kn_ref[...]
        k_ref[:, h * LANES:(h + 1) * LANES] = _rope(kh, c, s1, s2)


def _mla_kv(ckv_src, ckv_blk, kr_src, kr_blk, kv_norm, wk_p, wv_p, kn_p, tabs, n_rows, own_tokens):
    tm = _ROPE_TM
    hw = MLA_HEADS * LANES
    tab_spec = pl.BlockSpec((tm, LANES), (lambda i: (_rope_block(i), 0)) if own_tokens else (lambda i: (0, 0)))
    return pl.pallas_call(
        functools.partial(_mla_kv_kernel, norm_ckv=own_tokens),
        grid=(n_rows // tm,),
        in_specs=[pl.BlockSpec((tm, LANES), lambda i: (i, ckv_blk)),
                  pl.BlockSpec((tm, LANES), lambda i: (i, kr_blk)),
                  pl.BlockSpec((1, LANES), lambda i: (0, 0)),
                  pl.BlockSpec((MLA_KV_RANK, hw), lambda i: (0, 0)),
                  pl.BlockSpec((MLA_KV_RANK, hw), lambda i: (0, 0)),
                  pl.BlockSpec((1, LANES), lambda i: (0, 0)),
                  tab_spec, tab_spec, tab_spec],
        out_specs=[pl.BlockSpec((tm, hw), lambda i: (i, 0)),
                   pl.BlockSpec((tm, hw), lambda i: (i, 0)),
                   pl.BlockSpec((tm, LANES), lambda i: (i, 0))],
        out_shape=[jax.ShapeDtypeStruct((n_rows, hw), F32),
                   jax.ShapeDtypeStruct((n_rows, hw), F32),
                   jax.ShapeDtypeStruct((n_rows, LANES), F32)],
        compiler_params=_params("parallel"),
        name="mla_kv",
    )(ckv_src, kr_src, kv_norm, wk_p, wv_p, kn_p, *tabs)


_LOG2E = math.log2(math.e)


def _softmax_rows(s, scale):
    p = jnp.exp2((s - jnp.max(s, axis=-1, keepdims=True)) * (scale * _LOG2E))
    return p, jnp.sum(p, axis=-1, keepdims=True)


def _mla_attn_kernel(q_ref, k_ref, v_ref, o_ref, *, pairs):
    scale = MLA_QK ** -0.5
    for pr in range(pairs):
        outs = []
        for h in range(2):
            sl = slice((2 * pr + h) * LANES, (2 * pr + h + 1) * LANES)
            p, l = _softmax_rows(_dot_nt_bf16(q_ref[:, sl], k_ref[:, sl]), scale)
            outs.append(_dot_bf16(p, v_ref[:, sl]) / l)
        o_ref[:, pr * LANES:(pr + 1) * LANES] = (outs[0] + pltpu.roll(outs[1], MLA_V, axis=1)).astype(o_ref.dtype)


def _mla_attn(q, k, v, batch, nq, nk, q_row0, tq, pairs):
    nqb = nq // tq
    qb0 = q_row0 // tq
    wide = 2 * LANES * pairs
    return pl.pallas_call(
        functools.partial(_mla_attn_kernel, pairs=pairs),
        grid=(batch, MLA_HEADS // (2 * pairs), nqb),
        in_specs=[pl.BlockSpec((tq, wide), lambda b, h, i: (qb0 + b * nqb + i, h)),
                  pl.BlockSpec((nk, wide), lambda b, h, i: (b, h)),
                  pl.BlockSpec((nk, wide), lambda b, h, i: (b, h))],
        out_specs=pl.BlockSpec((tq, LANES * pairs), lambda b, h, i: (b * nqb + i, h)),
        out_shape=jax.ShapeDtypeStruct((batch * nq, MLA_HEADS * MLA_V), BF16),
        compiler_params=_params("parallel", "parallel", "arbitrary"),
        name="mla_attn",
    )(q, k, v)


def _ret_kernel(lg_ref, q_ref, k_ref, v_ref, rg_ref, s0_ref, gn_ref, *out_and_scratch, n, tq, want_state):
    if want_state:
        o_ref, st_ref, decay_ref = out_and_scratch
    else:
        o_ref, decay_ref = out_and_scratch
    pair, qi, b = pl.program_id(0), pl.program_id(1), pl.program_id(2)
    q = q_ref[...]
    k = k_ref[...] * (RET_DK ** -0.5)
    lo = _lane_lo((1, LANES))
    row = (qi * tq + lax.broadcasted_iota(jnp.int32, (tq, 1), 0)).astype(F32)

    @pl.when(b == 0)
    def _():
        col = lax.broadcasted_iota(jnp.int32, (1, n), 1).astype(F32)
        diff = row - col
        for h in range(2):
            lgf = lg_ref[0, 2 * pair + h]
            lgb = lg_ref[1, 2 * pair + h]
            decay_ref[h] = (jnp.where(diff >= 0, jnp.exp(lgf * jnp.maximum(diff, 0.0)), 0.0)
                            + jnp.where(diff <= 0, jnp.exp(lgb * jnp.maximum(-diff, 0.0)), 0.0))

    for h in range(2):
        lgf = lg_ref[0, 2 * pair + h]
        lgb = lg_ref[1, 2 * pair + h]
        mask = lo if h == 0 else jnp.logical_not(lo)
        qh = jnp.where(mask, q, 0.0)
        vh = v_ref[:, h * LANES:(h + 1) * LANES]
        o = _dot_bf16(_dot_nt_bf16(qh, k) * decay_ref[h], vh)
        o = o + _dot_bf16(qh * jnp.exp(lgf * (row + 1.0)), s0_ref[0, 0])
        o = o + _dot_bf16(qh * jnp.exp(lgb * (n - row)), s0_ref[0, 1])
        y = _rms(o, RET_DV) * gn_ref[:, h * LANES:(h + 1) * LANES]
        o_ref[:, h * LANES:(h + 1) * LANES] = (_silu(rg_ref[:, h * LANES:(h + 1) * LANES]) * y).astype(o_ref.dtype)

    if want_state:
        pos = lax.broadcasted_iota(jnp.int32, (n, 1), 0).astype(F32)
        for d in range(2):
            acc = None
            for h in range(2):
                lg = lg_ref[d, 2 * pair + h]
                mask = lo if h == 0 else jnp.logical_not(lo)
                expo = (n - 1.0 - pos) if d == 0 else pos
                kd = jnp.where(mask, k * jnp.exp(lg * expo), 0.0)
                term = _dot_tn(kd, v_ref[:, h * LANES:(h + 1) * LANES])
                acc = term if acc is None else acc + term
            lg_rows = jnp.where(lax.broadcasted_iota(jnp.int32, (LANES, 1), 0) < 64,
                                lg_ref[d, 2 * pair], lg_ref[d, 2 * pair + 1])
            st_ref[0, d] = acc + s0_ref[0, d] * jnp.exp(lg_rows * n)


def _retention(log_g, p, q_blk, k_blk, v_blk, g_blk, s0, gn, batch, n, row0, tq, want_state):
    nqb = n // tq
    assert not want_state or nqb == 1
    qb0 = row0 // tq
    kb0 = row0 // n
    pairs = RET_HEADS // 2
    out_specs = [pl.BlockSpec((tq, 2 * LANES), lambda h, i, b: (b * nqb + i, h))]
    out_shape = [jax.ShapeDtypeStruct((batch * n, RET_HEADS * RET_DV), BF16)]
    if want_state:
        out_specs.append(pl.BlockSpec((1, 2, LANES, LANES), lambda h, i, b: (b, 0, h, 0)))
        out_shape.append(jax.ShapeDtypeStruct((batch, 2, RET_HEADS * RET_DK, RET_DV), F32))
    outs = pl.pallas_call(
        functools.partial(_ret_kernel, n=n, tq=tq, want_state=want_state),
        grid=(pairs, nqb, batch),
        in_specs=[pl.BlockSpec(memory_space=pltpu.SMEM),
                  pl.BlockSpec((tq, LANES), lambda h, i, b: (qb0 + b * nqb + i, q_blk + h)),
                  pl.BlockSpec((n, LANES), lambda h, i, b: (kb0 + b, k_blk + h)),
                  pl.BlockSpec((n, 2 * LANES), lambda h, i, b: (kb0 + b, v_blk // 2 + h)),
                  pl.BlockSpec((tq, 2 * LANES), lambda h, i, b: (qb0 + b * nqb + i, g_blk // 2 + h)),
                  pl.BlockSpec((1, 2, LANES, LANES), lambda h, i, b: (b, 0, h, 0)),
                  pl.BlockSpec((1, 2 * LANES), lambda h, i, b: (0, h))],
        out_specs=out_specs,
        out_shape=out_shape,
        scratch_shapes=[pltpu.VMEM((2, tq, n), F32)],
        compiler_params=_params("parallel", "parallel", "arbitrary"),
        name="retention",
    )(log_g, p, p, p, p, s0, gn)
    return outs if want_state else (outs[0], None)


def _diff_qk_kernel(q_ref, k_ref, v_ref, qn_ref, kn_ref, c_ref, s1_ref, s2_ref, qo_ref, ko_ref, dk_ref, dv_ref):
    c, s1, s2 = c_ref[...], s1_ref[...], s2_ref[...]
    is_ctx = pl.program_id(0) < N_CTX // _ROPE_TM
    for src, gain, dst in ((q_ref, qn_ref, qo_ref), (k_ref, kn_ref, ko_ref)):
        for h in range(DIFF_HEADS):
            sl = slice(h * LANES, (h + 1) * LANES)
            y = src[:, sl]
            y = _rope(y * lax.rsqrt(_seg64_sum(y * y) * (1.0 / DIFF_DH) + EPS) * gain[...], c, s1, s2)
            dst[:, sl] = y
            if dst is ko_ref:
                @pl.when(is_ctx)
                def _():
                    dk_ref[:, h, 0, :] = y[:, :DIFF_DH]
                    dk_ref[:, h, 1, :] = pltpu.roll(y, DIFF_DH, axis=1)[:, :DIFF_DH]
                    dv_ref[:, h, :] = v_ref[:, sl]


def _diff_qk(p, qn_p, kn_p, tabs):
    tm = _ROPE_TM
    n_ctx = N_CTX // tm
    tab_spec = pl.BlockSpec((tm, LANES), lambda i: (_rope_block(i), 0))
    ctx_blk = lambda i: jnp.minimum(i, n_ctx - 1)
    return pl.pallas_call(
        _diff_qk_kernel,
        grid=(N_TOK // tm,),
        in_specs=[pl.BlockSpec((tm, DIFF_W), lambda i: (i, 0)),
                  pl.BlockSpec((tm, DIFF_W), lambda i: (i, 1)),
                  pl.BlockSpec((tm, DIFF_W), lambda i: (i, 2)),
                  pl.BlockSpec((1, LANES), lambda i: (0, 0)),
                  pl.BlockSpec((1, LANES), lambda i: (0, 0)),
                  tab_spec, tab_spec, tab_spec],
        out_specs=[pl.BlockSpec((tm, DIFF_W), lambda i: (i, 0)),
                   pl.BlockSpec((tm, DIFF_W), lambda i: (i, 0)),
                   pl.BlockSpec((tm, DIFF_HEADS, 2, DIFF_DH), lambda i: (ctx_blk(i), 0, 0, 0)),
                   pl.BlockSpec((tm, DIFF_HEADS, 2 * DIFF_DH), lambda i: (ctx_blk(i), 0, 0))],
        out_shape=[jax.ShapeDtypeStruct((N_TOK, DIFF_W), F32)] * 2
        + [jax.ShapeDtypeStruct((N_CTX, DIFF_HEADS, 2, DIFF_DH), F32),
           jax.ShapeDtypeStruct((N_CTX, DIFF_HEADS, 2 * DIFF_DH), F32)],
        compiler_params=_params("arbitrary"),
        name="diff_qk",
    )(p, p, p, qn_p, kn_p, *tabs)


def _diff_attn_kernel(lam_ref, q_ref, k_ref, v_ref, gn_ref, o_ref, *, lam_init, heads):
    lv = lam_ref[...]
    lam = (jnp.exp(jnp.sum(lv[0:1] * lv[1:2], axis=-1, keepdims=True))
           - jnp.exp(jnp.sum(lv[2:3] * lv[3:4], axis=-1, keepdims=True)) + lam_init)
    scale = DIFF_DH ** -0.5
    lo = _lane_lo((1, LANES))
    for h in range(heads):
        sl = slice(h * LANES, (h + 1) * LANES)
        q = q_ref[:, sl]
        kb = k_ref[:, sl].astype(BF16)
        p1, l1 = _softmax_rows(_dot_nt_bf16(jnp.where(lo, q, 0.0), kb), scale)
        p2, l2 = _softmax_rows(_dot_nt_bf16(jnp.where(lo, 0.0, q), kb), scale)
        w = p1 / l1 - lam * (p2 / l2)
        o = _dot_bf16(w, v_ref[:, sl])
        o_ref[:, sl] = (_rms(o, 2 * DIFF_DH) * gn_ref[:, sl] * (1.0 - lam_init)).astype(o_ref.dtype)


def _diff_attn(lam, q, k, v, v_blk0, gn, batch, nq, nk, q_row0, tq, lam_init, heads):
    nqb = nq // tq
    qb0 = q_row0 // tq
    wide = LANES * heads
    vb0 = v_blk0 // heads
    return pl.pallas_call(
        functools.partial(_diff_attn_kernel, lam_init=lam_init, heads=heads),
        grid=(batch, DIFF_HEADS // heads, nqb),
        in_specs=[pl.BlockSpec((4, DIFF_DH), lambda b, h, i: (0, 0)),
                  pl.BlockSpec((tq, wide), lambda b, h, i: (qb0 + b * nqb + i, h)),
                  pl.BlockSpec((nk, wide), lambda b, h, i: (b, h)),
                  pl.BlockSpec((nk, wide), lambda b, h, i: (b, vb0 + h)),
                  pl.BlockSpec((1, wide), lambda b, h, i: (0, h))],
        out_specs=pl.BlockSpec((tq, wide), lambda b, h, i: (b * nqb + i, h)),
        out_shape=jax.ShapeDtypeStruct((batch * nq, DIFF_W), BF16),
        compiler_params=_params("parallel", "parallel", "arbitrary"),
        name="diff_attn",
    )(lam, q, k, v, gn)


def _seg64_sum_wide(x):
    return jnp.concatenate([_seg64_sum(x[:, j * LANES:(j + 1) * LANES]) for j in range(x.shape[1] // LANES)], axis=1)


_SCAN_SLOTS = 3
_SCAN_NJ = _SCAN_SLOTS * RWKV_HEADS


def _rwkv_pre_kernel(r_ref, k_ref, v_ref, lo_ref, wup_ref, aup_ref, gup_ref, w0_ref, a0_ref, kk_ref, ka_ref, rk_ref,
                     op_ref, g_ref, bonus_ref):
    W = RWKV_W
    col = lambda q: slice(q * W, (q + 1) * W)
    r = r_ref[...]
    k = k_ref[...]
    v = v_ref[...]
    lora = lo_ref[...]
    kk = k * kk_ref[...]
    kkn = kk * lax.rsqrt(_seg64_sum_wide(kk * kk) + EPS)
    g_ref[...] = _dot_bf16(_sigmoid(lora[:, 2 * LANES:3 * LANES]), gup_ref[...])
    pre = w0_ref[...] + _dot_bf16(jnp.tanh(lora[:, 0:LANES]), wup_ref[...])
    decay = jnp.exp(-jnp.exp(-_softplus(-pre) - 0.5))
    a = _sigmoid(a0_ref[...] + _dot_bf16(lora[:, LANES:2 * LANES], aup_ref[...]))
    lo = _lane_lo((1, LANES))
    bonus = None
    for d in range(2):
        a_d = a[:, col(d)]
        k_d = k * (1.0 + (a_d - 1.0) * ka_ref[...])
        t = _seg64_sum_wide(r * k_d * rk_ref[...])
        bonus = t if bonus is None else bonus + t
        for s, (x1, x2) in enumerate(((kkn, decay[:, col(d)]), (k_d, kkn * a_d), (r, v))):
            for h in range(RWKV_HEADS):
                blk = slice((h // 2) * LANES, (h // 2 + 1) * LANES)
                if h % 2 == 0:
                    out = jnp.where(lo, x1[:, blk], pltpu.roll(x2[:, blk], RWKV_HS, axis=1))
                else:
                    out = jnp.where(lo, pltpu.roll(x1[:, blk], RWKV_HS, axis=1), x2[:, blk])
                op_ref[:, d * _SCAN_NJ + s * RWKV_HEADS + h, :] = out
    bonus_ref[...] = bonus


def _rwkv_pre(p, r_blk, k_blk, v_blk, lo_blk, wup_bd, aup_bd, gup, w0, a0, k_k, k_a, r_k):
    tm = 256
    w = RWKV_W
    row = lambda n: pl.BlockSpec((1, n), lambda i: (0, 0))
    full = lambda a, b: pl.BlockSpec((a, b), lambda i: (0, 0))
    return pl.pallas_call(
        _rwkv_pre_kernel,
        grid=(N_TOK // tm,),
        in_specs=[pl.BlockSpec((tm, w), lambda i: (i, r_blk)),
                  pl.BlockSpec((tm, w), lambda i: (i, k_blk)),
                  pl.BlockSpec((tm, w), lambda i: (i, v_blk)),
                  pl.BlockSpec((tm, 3 * LANES), lambda i: (i, lo_blk)),
                  full(LANES, 2 * w), full(LANES, 2 * w), full(LANES, w),
                  row(2 * w), row(2 * w), row(w), row(w), row(w)],
        out_specs=[pl.BlockSpec((tm, 2 * _SCAN_NJ, LANES), lambda i: (i, 0, 0)),
                   pl.BlockSpec((tm, w), lambda i: (i, 0)), pl.BlockSpec((tm, w), lambda i: (i, 0))],
        out_shape=[jax.ShapeDtypeStruct((N_TOK, 2 * _SCAN_NJ, LANES), F32),
                   jax.ShapeDtypeStruct((N_TOK, w), F32), jax.ShapeDtypeStruct((N_TOK, w), F32)],
        compiler_params=_params("parallel"),
        name="rwkv_pre",
    )(p, p, p, p, wup_bd, aup_bd, gup, w0, a0, k_k, k_a, r_k)


_SCAN_CHUNK = 32
_SCAN_UNROLL = 16


def _rwkv_first_sa(s_ref, sa_ref, kk, n_k):
    nv = s_ref.shape[1]
    chunk = min(_SCAN_CHUNK, nv)
    for c0 in range(0, nv, chunk):
        def body(k, acc):
            return acc + s_ref[k, c0:c0 + chunk, :] * kk(k)
        sa_ref[c0:c0 + chunk, :] = lax.fori_loop(0, n_k, body, jnp.zeros((chunk, LANES), F32), unroll=_SCAN_UNROLL)


def _rwkv_step(s_ref, sa_ref, kk_next, w, kd, b, r, v_at, n_k):
    nv = s_ref.shape[1]
    chunk = min(_SCAN_CHUNK, nv)
    ys = []
    for c0 in range(0, nv, chunk):
        sa = sa_ref[c0:c0 + chunk, :]
        vc = v_at(c0, chunk)

        def body(k, acc):
            y_acc, sa_acc = acc
            s_new = s_ref[k, c0:c0 + chunk, :] * w(k) - sa * b(k) + vc * kd(k)
            s_ref[k, c0:c0 + chunk, :] = s_new
            return y_acc + s_new * r(k), sa_acc + s_new * kk_next(k)

        zero = jnp.zeros((chunk, LANES), F32)
        y_acc, sa_acc = lax.fori_loop(0, n_k, body, (zero, zero), unroll=_SCAN_UNROLL)
        sa_ref[c0:c0 + chunk, :] = sa_acc
        ys.append(y_acc)
    return ys[0] if len(ys) == 1 else jnp.concatenate(ys, axis=0)


_CTX_TB = 32


def _load_scan_operands(kt_ref, t, slabs):
    for s, x in enumerate(slabs):
        xt = x.T
        kt_ref[t, 2 * s] = xt[:RWKV_HS]
        kt_ref[t, 2 * s + 1] = xt[RWKV_HS:]


def _rwkv_scan_ctx_kernel(x_ref, y_ref, st_ref, s_ref, sa_ref, kt_ref, ys_ref):
    d = pl.program_id(0)
    tb = pl.program_id(1)
    nh = RWKV_HEADS
    step_t = lambda i: jnp.where(d == 0, i, _CTX_TB - 1 - i)

    @pl.when(tb == 0)
    def _():
        s_ref[...] = jnp.zeros_like(s_ref)

    def load_t(t, carry):
        _load_scan_operands(kt_ref, t, [
            jnp.concatenate([x_ref[b, t, s * nh:(s + 1) * nh, :] for b in range(BATCH)], axis=0)
            for s in range(_SCAN_SLOTS)])
        return carry

    lax.fori_loop(0, _CTX_TB, load_t, 0, unroll=4)

    t0 = step_t(0)
    _rwkv_first_sa(s_ref, sa_ref, lambda k: kt_ref[t0, 0, pl.ds(k, 1), :], RWKV_HS)

    def step(i, carry):
        t = step_t(i)
        tn = step_t(jnp.minimum(i + 1, _CTX_TB - 1))
        row = lambda q, tt: (lambda k: kt_ref[tt, q, pl.ds(k, 1), :])
        ys_ref[t] = _rwkv_step(s_ref, sa_ref, row(0, tn), row(1, t), row(2, t), row(3, t), row(4, t),
                               lambda c0, n: kt_ref[t, 5, pl.ds(c0, n), :], RWKV_HS)
        return carry

    lax.fori_loop(0, _CTX_TB, step, 0)

    def store_t(i, carry):
        z = jnp.concatenate([ys_ref[2 * i], ys_ref[2 * i + 1]], axis=0).T
        z_odd = pltpu.roll(z, RWKV_HS, axis=1)
        for b in range(BATCH):
            y_ref[b, 2 * i] = z[b * nh:(b + 1) * nh, :RWKV_HS]
            y_ref[b, 2 * i + 1] = z_odd[b * nh:(b + 1) * nh, :RWKV_HS]
        return carry

    lax.fori_loop(0, _CTX_TB // 2, store_t, 0, unroll=4)

    @pl.when(tb == pl.num_programs(1) - 1)
    def _():
        st_ref[0] = s_ref[...]


def _rwkv_scan_ctx(op4):
    nt = SEQ // _CTX_TB
    hs = RWKV_HS
    tblk = lambda d, tb: jnp.where(d == 0, tb, nt - 1 - tb)
    return pl.pallas_call(
        _rwkv_scan_ctx_kernel,
        grid=(2, nt),
        in_specs=[pl.BlockSpec((BATCH, _CTX_TB, _SCAN_NJ, LANES), lambda d, tb: (0, tblk(d, tb), d, 0))],
        out_specs=[pl.BlockSpec((BATCH, _CTX_TB, RWKV_HEADS, hs), lambda d, tb: (0, tblk(d, tb), d, 0)),
                   pl.BlockSpec((1, hs, hs, LANES), lambda d, tb: (d, 0, 0, 0))],
        out_shape=[jax.ShapeDtypeStruct((BATCH, SEQ, 2 * RWKV_HEADS, hs), F32),
                   jax.ShapeDtypeStruct((2, hs, hs, LANES), F32)],
        scratch_shapes=[pltpu.VMEM((hs, hs, LANES), F32), pltpu.VMEM((hs, LANES), F32),
                        pltpu.VMEM((_CTX_TB, 2 * _SCAN_SLOTS, hs, LANES), F32),
                        pltpu.VMEM((_CTX_TB, hs, LANES), F32)],
        compiler_params=_params("parallel", "arbitrary"),
        name="rwkv_scan_ctx",
    )(op4)


_LAT_TB = 32
_LAT_VSPLIT = 4
_LAT_STATES = 2 * DEC_BATCH * RWKV_HEADS
_LAT_VROWS = RWKV_HS // _LAT_VSPLIT


def _rwkv_scan_lat_kernel(xf0_ref, xf1_ref, xb0_ref, xb1_ref, s0_ref, yf_ref, yb_ref,
                          s_ref, sa_ref, kt_ref, v_ref, ys_ref):
    @pl.when(pl.program_id(0) == 0)
    def _():
        s_ref[...] = s0_ref[...]

    group = lax.broadcasted_iota(jnp.int32, (_LAT_VROWS, LANES), 1) // _LAT_STATES
    nh = RWKV_HEADS

    def load_t(t, carry):
        tr = _LAT_TB - 1 - t
        for s in range(_SCAN_SLOTS):
            heads = slice(s * nh, (s + 1) * nh)
            x = jnp.concatenate([xf0_ref[0, t, heads, :], xf1_ref[0, t, heads, :],
                                 xb0_ref[0, tr, heads, :], xb1_ref[0, tr, heads, :]], axis=0)
            xt = jnp.concatenate([x] * _LAT_VSPLIT, axis=0).T
            kt_ref[t, 2 * s] = xt[:RWKV_HS]
            if s < _SCAN_SLOTS - 1:
                kt_ref[t, 2 * s + 1] = xt[RWKV_HS:]
            else:
                v = jnp.zeros((_LAT_VROWS, LANES), F32)
                for g in range(_LAT_VSPLIT):
                    r0 = RWKV_HS + g * _LAT_VROWS
                    v = jnp.where(group == g, xt[r0:r0 + _LAT_VROWS, :], v)
                v_ref[t] = v
        return carry

    lax.fori_loop(0, _LAT_TB, load_t, 0, unroll=8)

    _rwkv_first_sa(s_ref, sa_ref, lambda k: kt_ref[0, 0, pl.ds(k, 1), :], RWKV_HS)

    def step(t, carry):
        tn = jnp.minimum(t + 1, _LAT_TB - 1)
        row = lambda q, tt: (lambda k: kt_ref[tt, q, pl.ds(k, 1), :])
        ys_ref[t] = _rwkv_step(s_ref, sa_ref, row(0, tn), row(1, t), row(2, t), row(3, t), row(4, t),
                               lambda c0, n: v_ref[t, pl.ds(c0, n), :], RWKV_HS)
        return carry

    lax.fori_loop(0, _LAT_TB, step, 0)

    def store_t(i, carry):
        rows = [jnp.where(group == g, ys_ref[2 * i + j], 0.0) for j in range(2) for g in range(_LAT_VSPLIT)]
        z = jnp.concatenate(rows, axis=0).T
        y = (z[0:_LAT_STATES] + z[_LAT_STATES:2 * _LAT_STATES]
             + z[2 * _LAT_STATES:3 * _LAT_STATES] + z[3 * _LAT_STATES:4 * _LAT_STATES])
        y_odd = pltpu.roll(y, RWKV_HS, axis=1)
        half = _LAT_STATES // 2
        for b in range(DEC_BATCH):
            rows_f = slice(b * nh, (b + 1) * nh)
            rows_b = slice(half + b * nh, half + (b + 1) * nh)
            yf_ref[b, 2 * i] = y[rows_f, :RWKV_HS]
            yf_ref[b, 2 * i + 1] = y_odd[rows_f, :RWKV_HS]
            yb_ref[b, _LAT_TB - 1 - 2 * i] = y[rows_b, :RWKV_HS]
            yb_ref[b, _LAT_TB - 2 - 2 * i] = y_odd[rows_b, :RWKV_HS]
        return carry

    lax.fori_loop(0, _LAT_TB // 2, store_t, 0, unroll=4)


def _rwkv_scan_lat(op4, s0):
    hs = RWKV_HS
    nv = _LAT_VROWS
    nt = DEC_SEQ // _LAT_TB
    per_seq = SEQ // _LAT_TB
    first = N_CTX // SEQ

    def x_spec(b, d):
        tblk = (lambda tb: tb) if d == 0 else (lambda tb: nt - 1 - tb)
        return pl.BlockSpec((1, _LAT_TB, _SCAN_NJ, LANES),
                            lambda tb: (first + b * (DEC_SEQ // SEQ) + tblk(tb) // per_seq, tblk(tb) % per_seq, d, 0))

    y_shape = jax.ShapeDtypeStruct((DEC_BATCH, DEC_SEQ, RWKV_HEADS, hs), F32)
    y_blk = (DEC_BATCH, _LAT_TB, RWKV_HEADS, hs)
    return pl.pallas_call(
        _rwkv_scan_lat_kernel,
        grid=(nt,),
        in_specs=[x_spec(0, 0), x_spec(1, 0), x_spec(0, 1), x_spec(1, 1),
                  pl.BlockSpec((hs, nv, LANES), lambda tb: (0, 0, 0))],
        out_specs=[pl.BlockSpec(y_blk, lambda tb: (0, tb, 0, 0)),
                   pl.BlockSpec(y_blk, lambda tb: (0, nt - 1 - tb, 0, 0))],
        out_shape=[y_shape, y_shape],
        scratch_shapes=[pltpu.VMEM((hs, nv, LANES), F32), pltpu.VMEM((nv, LANES), F32),
                        pltpu.VMEM((_LAT_TB, 2 * _SCAN_SLOTS - 1, hs, LANES), F32),
                        pltpu.VMEM((_LAT_TB, nv, LANES), F32), pltpu.VMEM((_LAT_TB, nv, LANES), F32)],
        compiler_params=_params("arbitrary"),
        name="rwkv_scan_lat",
    )(op4, op4, op4, op4, s0)


def _rwkv_post_kernel(yc_ref, ylf_ref, ylb_ref, bonus_ref, v_ref, g_ref, gn_ref, o_ref):
    def finish(head_sum):
        y = jnp.concatenate([head_sum(h) for h in range(RWKV_HEADS)], axis=1)
        y = y * lax.rsqrt(_seg64_sum_wide(y * y) * (1.0 / RWKV_HS) + EPS) * gn_ref[...]
        o_ref[...] = ((y + bonus_ref[...] * v_ref[...]) * g_ref[...]).astype(o_ref.dtype)

    @pl.when(pl.program_id(0) < N_CTX // SEQ)
    def _():
        finish(lambda h: yc_ref[0, :, h, :] + yc_ref[0, :, RWKV_HEADS + h, :])

    @pl.when(pl.program_id(0) >= N_CTX // SEQ)
    def _():
        finish(lambda h: ylf_ref[0, :, h, :] + ylb_ref[0, :, h, :])


def _rwkv_post(y_ctx, y_lat_f, y_lat_b, bonus, p, v_blk, g, gn):
    tm = SEQ
    w = RWKV_W
    hs = RWKV_HS
    n_ctx = N_CTX // SEQ
    lat = lambda y: y.reshape(N_LAT // SEQ, SEQ, RWKV_HEADS, hs)
    spec = pl.BlockSpec((tm, w), lambda i: (i, 0))
    lat_spec = pl.BlockSpec((1, SEQ, RWKV_HEADS, hs), lambda i: (jnp.maximum(i - n_ctx, 0), 0, 0, 0))
    return pl.pallas_call(
        _rwkv_post_kernel,
        grid=(N_TOK // tm,),
        in_specs=[pl.BlockSpec((1, SEQ, 2 * RWKV_HEADS, hs), lambda i: (jnp.minimum(i, n_ctx - 1), 0, 0, 0)),
                  lat_spec, lat_spec, spec, pl.BlockSpec((tm, w), lambda i: (i, v_blk)), spec,
                  pl.BlockSpec((1, w), lambda i: (0, 0))],
        out_specs=spec,
        out_shape=jax.ShapeDtypeStruct((N_TOK, w), BF16),
        compiler_params=_params("parallel"),
        name="rwkv_post",
    )(y_ctx, lat(y_lat_f), lat(y_lat_b), bonus, p, g, gn)


def _value_split_layout(x):
    lead = x.shape[:-2]
    n = len(lead)
    x = x.reshape(lead + (_LAT_STATES, _LAT_VSPLIT, _LAT_VROWS))
    return jnp.transpose(x, tuple(range(n)) + (n + 2, n + 1, n)).reshape(lead + (_LAT_VROWS, LANES))


def _even_layer(x, mod, g_mix, w_in, q_norm, kv_norm, w_uq, w_ukv, qn, kn, ret_decay, ret_gn,
                cache_ckv, cache_krope, state_ret, tabs_m):
    cq, ckv, krope, rq, rk, rv, rg = jnp.split(w_in, np.cumsum(
        (MLA_Q_RANK, MLA_KV_RANK, MLA_ROPE, RET_HEADS * RET_DK, RET_HEADS * RET_DK, RET_HEADS * RET_DV))[:].tolist(),
        axis=1)
    z = lambda n: jnp.zeros((D_MODEL, n), F32)
    w_p = jnp.concatenate([cq, ckv, z(MLA_NOPE), krope, z(LANES - MLA_QK), rq, rk, rv, rg], axis=1)
    p = _inproj(x, g_mix, mod, w_p, jnp.zeros((1, w_p.shape[1]), F32), 512, None)
    CKV_BLK, KR_BLK, RQ_BLK, RK_BLK, RV_BLK, RG_BLK = 2, 3, 4, 6, 8, 12

    def head_pad(w, n_head, d_head, c0, c1):
        w = w.reshape(w.shape[0], n_head, d_head)[:, :, c0:c1]
        return jnp.pad(w, ((0, 0), (0, 0), (0, LANES - (c1 - c0)))).reshape(w.shape[0], n_head * LANES)

    w_uq_p = head_pad(w_uq, MLA_HEADS, MLA_QK, 0, MLA_QK)
    wk_p = head_pad(w_ukv, MLA_HEADS, MLA_NOPE + MLA_V, 0, MLA_NOPE)
    wv_p = head_pad(w_ukv, MLA_HEADS, MLA_NOPE + MLA_V, MLA_NOPE, MLA_NOPE + MLA_V)
    qn_p = jnp.pad(qn, (0, LANES - MLA_QK))[None]
    kn_p = jnp.pad(kn, (0, LANES - MLA_QK))[None]

    q = _mla_q(p, q_norm[None], w_uq_p, qn_p, tabs_m)
    k, v, ckvn = _mla_kv(p, CKV_BLK, p, KR_BLK, kv_norm[None], wk_p, wv_p, kn_p, tabs_m, N_TOK, True)

    n_c = DEC_BATCH * PAST_LEN
    kr_c = jnp.pad(cache_krope.reshape(n_c, MLA_ROPE), ((0, 0), (MLA_NOPE, LANES - MLA_QK)))
    k_c, v_c, _ = _mla_kv(cache_ckv.reshape(n_c, MLA_KV_RANK), 0, kr_c, 0, kv_norm[None], wk_p, wv_p, kn_p,
                          tabs_m, n_c, False)

    hw = MLA_HEADS * LANES

    def with_cache(own, cache):
        own = own[N_CTX:].reshape(DEC_BATCH, DEC_SEQ, hw)
        return jnp.concatenate([own, cache.reshape(DEC_BATCH, PAST_LEN, hw)], 1).reshape(-1, hw)

    o_ctx = _mla_attn(q, k, v, BATCH, SEQ, SEQ, 0, SEQ, MLA_HEADS // 2)
    o_lat = _mla_attn(q, with_cache(k, k_c), with_cache(v, v_c), DEC_BATCH, DEC_SEQ, DEC_SEQ + PAST_LEN, N_CTX, 256,
                      1)

    log_g = -_softplus(-ret_decay)
    gn = ret_gn[None]
    s0_ctx = jnp.zeros((BATCH, 2, RET_HEADS * RET_DK, RET_DV), F32)
    r_ctx, st_ctx = _retention(log_g, p, RQ_BLK, RK_BLK, RV_BLK, RG_BLK, s0_ctx, gn, BATCH, SEQ, 0, SEQ, True)
    s0_lat = state_ret.reshape(DEC_BATCH, 2, RET_HEADS * RET_DK, RET_DV)
    r_lat, _ = _retention(log_g, p, RQ_BLK, RK_BLK, RV_BLK, RG_BLK, s0_lat, gn, DEC_BATCH, DEC_SEQ, N_CTX, 256,
                          False)

    mix = [(o_ctx, o_lat), (r_ctx, r_lat)]
    new_ckv = ckvn[:N_CTX].reshape(BATCH, SEQ, MLA_KV_RANK)
    new_krope = p[:N_CTX, KR_BLK * LANES + MLA_NOPE:KR_BLK * LANES + MLA_QK]
    new_krope = new_krope.reshape(BATCH, SEQ, MLA_ROPE)
    new_ret = st_ctx.reshape(BATCH, 2, RET_HEADS, RET_DK, RET_DV)
    return mix, new_ckv, new_krope, new_ret


def _odd_layer(x, mod, g_mix, w_in, qn, kn, lam, diff_gn, mu, w0, w_up, a0, a_up, g_up, k_k, k_a, r_k, gn,
               cache_k, cache_v, state_rwkv, tabs_d, lam_init):
    w_p = w_in
    n_in = w_p.shape[1]
    mu_full = jnp.concatenate([jnp.zeros((3 * DIFF_W,), F32), mu])[None]
    p = _inproj(x, g_mix, mod, w_p, mu_full, 384, (3 * DIFF_W) // 384)
    DV_BLK, R_BLK, K_BLK, V_BLK = 2, 3, 4, 5
    LO_BLK = (6 * RWKV_W) // (3 * LANES)

    qn_p = jnp.tile(qn, 2)[None]
    kn_p = jnp.tile(kn, 2)[None]
    q, k, dk, dv = _diff_qk(p, qn_p, kn_p, tabs_d)

    n_c = DEC_BATCH * PAST_LEN
    k_c = cache_k.reshape(DEC_BATCH, PAST_LEN, DIFF_W)
    v_c = cache_v.reshape(DEC_BATCH, PAST_LEN, DIFF_W)

    def with_cache(own_lat, cache):
        return jnp.concatenate([own_lat.reshape(DEC_BATCH, DEC_SEQ, DIFF_W), cache], 1).reshape(-1, DIFF_W)

    dgn = diff_gn[None]
    dv_cols = slice(DV_BLK * DIFF_W, (DV_BLK + 1) * DIFF_W)
    o_ctx = _diff_attn(lam, q, k, p, DV_BLK * DIFF_HEADS, dgn, BATCH, SEQ, SEQ, 0, SEQ, lam_init, DIFF_HEADS)
    o_lat = _diff_attn(lam, q, with_cache(k[N_CTX:], k_c), with_cache(p[N_CTX:, dv_cols], v_c), 0, dgn,
                       DEC_BATCH, DEC_SEQ, DEC_SEQ + PAST_LEN, N_CTX, 256, lam_init, 1)

    zero = jnp.zeros((RWKV_W_LORA, RWKV_W), F32)
    wup_bd = jnp.concatenate([jnp.concatenate([w_up[0], zero], 1), jnp.concatenate([zero, w_up[1]], 1)], 0)
    aup_bd = jnp.concatenate([jnp.concatenate([a_up[0], zero], 1), jnp.concatenate([zero, a_up[1]], 1)], 0)
    op, g, bonus = _rwkv_pre(p, R_BLK, K_BLK, V_BLK, LO_BLK, wup_bd, aup_bd, g_up, w0.reshape(1, -1),
                             a0.reshape(1, -1), k_k[None], k_a[None], r_k.reshape(1, -1))
    op4 = op.reshape(N_TOK // SEQ, SEQ, 2 * _SCAN_NJ, LANES)
    y_ctx, st_ctx = _rwkv_scan_ctx(op4)
    s0_lat = jnp.transpose(state_rwkv, (4, 1, 0, 2, 3)).reshape(RWKV_HS, _LAT_STATES, RWKV_HS)
    y_lat_f, y_lat_b = _rwkv_scan_lat(op4, _value_split_layout(s0_lat))
    rw_o = _rwkv_post(y_ctx, y_lat_f, y_lat_b, bonus, p, V_BLK, g, gn[None])

    mix = [(o_ctx, o_lat), rw_o]
    new_dk = dk.reshape(BATCH, SEQ, DIFF_HEADS, 2, DIFF_DH)
    new_dv = dv.reshape(BATCH, SEQ, DIFF_HEADS, 2 * DIFF_DH)
    new_rwkv = jnp.transpose(st_ctx.reshape(2, RWKV_HS, RWKV_HS, BATCH, RWKV_HEADS), (3, 0, 4, 2, 1))
    return mix, new_dk, new_dv, new_rwkv


def kernel(x_prompt, x_sample, cache_mla_ckv, cache_mla_krope, state_ret, cache_diff_k, cache_diff_v, state_rwkv,
           c, c_ctx, ada_w, ada_b, norm_mix_g, norm_ffn_g, w_out, ffn_up, ffn_conv_w, ffn_conv_b, ffn_down,
           a_w_in, mla_q_norm, mla_kv_norm, mla_w_uq, mla_w_ukv, mla_qn, mla_kn, ret_decay, ret_gn,
           b_w_in, diff_qn, diff_kn, diff_lam, diff_gn, rwkv_mu, rwkv_w0, rwkv_w_up, rwkv_a0, rwkv_a_up,
           rwkv_g_up, rwkv_k_k, rwkv_k_a, rwkv_r_k, rwkv_gn):
    x = jnp.concatenate([x_prompt.reshape(N_CTX, D_MODEL), x_sample.reshape(N_LAT, D_MODEL)], 0)
    cond8 = jnp.pad(jnp.concatenate([c_ctx[None], c], 0), ((0, 8 - N_GROUPS), (0, 0)))
    mod = _modulation(cond8, ada_w, ada_b)

    tabs_m = _rope_tables(MLA_ROPE, (MLA_NOPE,))
    tabs_d = _rope_tables(DIFF_DH, (0, DIFF_DH))

    outs = {}
    for l in range(DEPTH):
        j = l // 2
        g_mix = norm_mix_g[l][None]
        if l % 2 == 0:
            mix, outs["ckv"], outs["krope"], outs["ret"] = _even_layer(
                x, mod[l], g_mix, a_w_in[j], mla_q_norm[j], mla_kv_norm[j], mla_w_uq[j], mla_w_ukv[j], mla_qn[j],
                mla_kn[j], ret_decay[j], ret_gn[j], cache_mla_ckv[:, j], cache_mla_krope[:, j], state_ret[:, j],
                tabs_m)
        else:
            lam_init = 0.8 - 0.6 * math.exp(-0.3 * l)
            mix, outs["dk"], outs["dv"], outs["rwkv"] = _odd_layer(
                x, mod[l], g_mix, b_w_in[j], diff_qn[j], diff_kn[j], diff_lam[j], diff_gn[j], rwkv_mu[j],
                rwkv_w0[j], rwkv_w_up[j], rwkv_a0[j], rwkv_a_up[j], rwkv_g_up[j], rwkv_k_k[j], rwkv_k_a[j],
                rwkv_r_k[j], rwkv_gn[j], cache_diff_k[:, j], cache_diff_v[:, j], state_rwkv[:, j], tabs_d, lam_init)
        x = _resid_proj(mix, w_out[l], x, mod[l], 2)
        act = _ffn_up(x, norm_ffn_g[l][None], mod[l], ffn_up[l], ffn_conv_w[l], ffn_conv_b[l])
        x = _resid_proj([act], ffn_down[l], x, mod[l], 5, split_out=(l == DEPTH - 1))

    y_prompt = x[0].reshape(BATCH, SEQ, D_MODEL)
    y_sample = x[1].reshape(DEC_BATCH, DEC_SEQ, D_MODEL)
    return (y_prompt, y_sample, outs["ckv"][:, None], outs["krope"][:, None], outs["ret"][:, None],
            outs["dk"][:, None], outs["dv"][:, None], outs["rwkv"][:, None])
```

```python
import functools
import math

import numpy as np
import jax
import jax.numpy as jnp
from jax import lax
from jax.experimental import pallas as pl
from jax.experimental.pallas import tpu as pltpu

D_MODEL = 1024
BATCH = 16
SEQ = 256
DEPTH = 2
DEC_BATCH = 2
DEC_SEQ = 1024
PAST_LEN = 512
GRID_W = 64
EPS = 1e-6
ROPE_BASE = 10000.0

MLA_HEADS = 8
MLA_Q_RANK = 256
MLA_KV_RANK = 128
MLA_NOPE = 64
MLA_ROPE = 32
MLA_V = 64
MLA_QK = MLA_NOPE + MLA_ROPE
RET_HEADS = 4
RET_DK = 64
RET_DV = 128
DIFF_HEADS = 4
DIFF_DH = 64
DIFF_W = DIFF_HEADS * 2 * DIFF_DH
RWKV_HEADS = 8
RWKV_HS = 64
RWKV_W = RWKV_HEADS * RWKV_HS
RWKV_W_LORA = 64
RWKV_A_LORA = 64
RWKV_G_LORA = 128
D_FF = 2816

N_CTX = BATCH * SEQ
N_LAT = DEC_BATCH * DEC_SEQ
N_TOK = N_CTX + N_LAT
N_GROUPS = 1 + DEC_BATCH

LANES = 128
VMEM_LIMIT = 56 * 1024 * 1024

_PREC = lax.Precision.HIGHEST
F32 = jnp.float32


def _dot_tn(a, b):
    return lax.dot_general(a, b, (((0,), (0,)), ((), ())), precision=_PREC, preferred_element_type=F32)


BF16 = jnp.bfloat16


def _dot_bf16(a, b):
    return jnp.dot(a.astype(BF16), b.astype(BF16), preferred_element_type=F32)


def _dot_nt_bf16(a, b):
    return lax.dot_general(a.astype(BF16), b.astype(BF16), (((1,), (1,)), ((), ())), preferred_element_type=F32)


def _params(*sem):
    return pltpu.CompilerParams(dimension_semantics=sem, vmem_limit_bytes=VMEM_LIMIT)


def _sigmoid(x):
    return 1.0 / (1.0 + jnp.exp(-x))


def _silu(x):
    return x * _sigmoid(x)


def _softplus(x):
    return jnp.maximum(x, 0.0) + jnp.log(1.0 + jnp.exp(-jnp.abs(x)))


def _rms(x, n):
    return x * lax.rsqrt(jnp.sum(x * x, axis=-1, keepdims=True) * (1.0 / n) + EPS)


def _lane_lo(shape):
    return lax.broadcasted_iota(jnp.int32, shape, len(shape) - 1) < 64


def _seg64_sum(x):
    lo = _lane_lo(x.shape)
    s_lo = jnp.sum(jnp.where(lo, x, 0.0), axis=-1, keepdims=True)
    s_hi = jnp.sum(jnp.where(lo, 0.0, x), axis=-1, keepdims=True)
    return jnp.where(lo, s_lo, s_hi)


_SUBLANES = 8


def _seq_neighbours(p, tile, tile_rows):
    is_ctx = tile * tile_rows < N_CTX
    sub = lax.broadcasted_iota(jnp.int32, (_SUBLANES, 1), 0)

    def shifted(rolled, edge_sublane, group_of_seq):
        pieces, start = [], 0
        for q in range(tile_rows // SEQ):
            g0 = q * SEQ + group_of_seq
            outer = (q == 0) if group_of_seq == 0 else (q == tile_rows // SEQ - 1)
            edge = (sub == edge_sublane) if outer else ((sub == edge_sublane) & is_ctx)
            pieces += [rolled[start:g0], jnp.where(edge, 0.0, rolled[g0:g0 + _SUBLANES])]
            start = g0 + _SUBLANES
        pieces.append(rolled[start:])
        return jnp.concatenate([x for x in pieces if x.shape[0]], axis=0)

    prev = shifted(pltpu.roll(p, 1, axis=0), 0, 0)
    nxt = shifted(pltpu.roll(p, tile_rows - 1, axis=0), _SUBLANES - 1, SEQ - _SUBLANES)
    return prev, nxt


def _group_of_tile(i, tile_rows):
    row = i * tile_rows
    return jnp.where(row < N_CTX, 0, 1 + (row - N_CTX) // DEC_SEQ)


def _modulation_kernel(c_ref, w_ref, b_ref, o_ref):
    o_ref[0] = _dot_bf16(_silu(c_ref[...]), w_ref[0]) + b_ref[0]


def _modulation(cond8, ada_w, ada_b):
    tn = 512
    n = 6 * D_MODEL
    out = pl.pallas_call(
        _modulation_kernel,
        grid=(DEPTH, n // tn),
        in_specs=[pl.BlockSpec((8, D_MODEL), lambda l, j: (0, 0)),
                  pl.BlockSpec((1, D_MODEL, tn), lambda l, j: (l, 0, j)),
                  pl.BlockSpec((1, 1, tn), lambda l, j: (l, 0, j))],
        out_specs=pl.BlockSpec((1, 8, tn), lambda l, j: (l, 0, j)),
        out_shape=jax.ShapeDtypeStruct((DEPTH, 8, n), F32),
        compiler_params=_params("parallel", "parallel"),
        name="modulation",
    )(cond8, ada_w, ada_b.reshape(DEPTH, 1, n))
    m = out[:, :N_GROUPS].reshape(DEPTH, N_GROUPS, 6, D_MODEL)
    return jnp.pad(m, ((0, 0), (0, 0), (0, 2), (0, 0)))


_TM_SEQ = 1024


def _norm_mod(x, g, mod, off):
    return _rms(x, D_MODEL) * g * (1.0 + mod[off + 1:off + 2, :]) + mod[off:off + 1, :]


def _inproj_kernel(x_ref, g_ref, mod_ref, w_ref, mu_ref, o_ref, h_ref, *, shift_from):
    i = pl.program_id(0)

    @pl.when(pl.program_id(1) == 0)
    def _():
        h_ref[...] = _norm_mod(x_ref[...], g_ref[...], mod_ref[0], 0).astype(BF16)

    p = _dot_bf16(h_ref[...], w_ref[...])
    if shift_from is None:
        o_ref[...] = p
    else:
        @pl.when(pl.program_id(1) < shift_from)
        def _():
            o_ref[...] = p

        @pl.when(pl.program_id(1) >= shift_from)
        def _():
            prev, nxt = _seq_neighbours(p, i, _TM_SEQ)
            o_ref[...] = p + (0.5 * (prev + nxt) - p) * mu_ref[...]


def _inproj(x, g, mod, w, mu, tn, shift_from):
    n = w.shape[1]
    tm = _TM_SEQ
    return pl.pallas_call(
        functools.partial(_inproj_kernel, shift_from=shift_from),
        grid=(N_TOK // tm, n // tn),
        in_specs=[pl.BlockSpec((tm, D_MODEL), lambda i, j: (i, 0)),
                  pl.BlockSpec((1, D_MODEL), lambda i, j: (0, 0)),
                  pl.BlockSpec((1, 8, D_MODEL), lambda i, j: (_group_of_tile(i, tm), 0, 0)),
                  pl.BlockSpec((D_MODEL, tn), lambda i, j: (0, j)),
                  pl.BlockSpec((1, tn), lambda i, j: (0, j))],
        out_specs=pl.BlockSpec((tm, tn), lambda i, j: (i, j)),
        out_shape=jax.ShapeDtypeStruct((N_TOK, n), F32),
        scratch_shapes=[pltpu.VMEM((tm, D_MODEL), BF16)],
        compiler_params=_params("parallel", "arbitrary"),
        name="inproj" if shift_from is None else "inproj_shift",
    )(x, g, mod, w, mu)


def _resid_kernel(*refs, gate_row, widths, split, split_out):
    n_in = sum(2 if sp else 1 for sp in split)
    a_refs = refs[:n_in]
    w_ref, x_ref, mod_ref = refs[n_in:n_in + 3]
    o_ref = refs[n_in + 3:-1] if split_out else refs[n_in + 3]
    a_bf_ref = refs[-1]
    i = pl.program_id(0)

    @pl.when(pl.program_id(1) == 0)
    def _():
        k0, r = 0, 0
        for width, sp in zip(widths, split):
            cols = slice(k0, k0 + width)
            if sp:
                ctx_ref, lat_ref = a_refs[r], a_refs[r + 1]

                @pl.when(i < N_CTX // _TM_SEQ)
                def _():
                    a_bf_ref[:, cols] = ctx_ref[...].astype(BF16)

                @pl.when(i >= N_CTX // _TM_SEQ)
                def _():
                    a_bf_ref[:, cols] = lat_ref[...].astype(BF16)
            else:
                a_bf_ref[:, cols] = a_refs[r][...].astype(BF16)
            k0 += width
            r += 2 if sp else 1

    y = x_ref[...] + mod_ref[0, gate_row:gate_row + 1, :] * _dot_bf16(a_bf_ref[...], w_ref[...])
    if not split_out:
        o_ref[...] = y
    else:
        ctx_o_ref, lat_o_ref = o_ref

        @pl.when(i < N_CTX // _TM_SEQ)
        def _():
            ctx_o_ref[...] = y

        @pl.when(i >= N_CTX // _TM_SEQ)
        def _():
            lat_o_ref[...] = y


def _resid_proj(acts, w, x, mod, gate_row, split_out=False):
    tm, tn = _TM_SEQ, 256
    n_ctx = N_CTX // tm
    nj = D_MODEL // tn
    split = [isinstance(a, (tuple, list)) for a in acts]
    widths = [a[0].shape[1] if sp else a.shape[1] for a, sp in zip(acts, split)]
    k = sum(widths)
    in_specs, operands = [], []
    for a, width, sp in zip(acts, widths, split):
        if sp:
            in_specs += [pl.BlockSpec((tm, width), lambda i, j: (jnp.minimum(i, n_ctx - 1), 0)),
                         pl.BlockSpec((tm, width), lambda i, j: (jnp.maximum(i - n_ctx, 0), 0))]
            operands += list(a)
        else:
            in_specs.append(pl.BlockSpec((tm, width), lambda i, j: (i, 0)))
            operands.append(a)
    if split_out:
        out_specs = [pl.BlockSpec((tm, tn), lambda i, j: (jnp.minimum(i, n_ctx - 1), jnp.where(i < n_ctx, j, nj - 1))),
                     pl.BlockSpec((tm, tn), lambda i, j: (jnp.maximum(i - n_ctx, 0), jnp.where(i < n_ctx, 0, j)))]
        out_shape = [jax.ShapeDtypeStruct((N_CTX, D_MODEL), F32), jax.ShapeDtypeStruct((N_LAT, D_MODEL), F32)]
        sem = ("arbitrary", "arbitrary")
    else:
        out_specs = pl.BlockSpec((tm, tn), lambda i, j: (i, j))
        out_shape = jax.ShapeDtypeStruct((N_TOK, D_MODEL), F32)
        sem = ("parallel", "arbitrary")
    return pl.pallas_call(
        functools.partial(_resid_kernel, gate_row=gate_row, widths=tuple(widths), split=tuple(split),
                          split_out=split_out),
        grid=(N_TOK // tm, nj),
        in_specs=in_specs
        + [pl.BlockSpec((k, tn), lambda i, j: (0, j)),
           pl.BlockSpec((tm, tn), lambda i, j: (i, j)),
           pl.BlockSpec((1, 8, tn), lambda i, j: (_group_of_tile(i, tm), 0, j))],
        out_specs=out_specs,
        out_shape=out_shape,
        scratch_shapes=[pltpu.VMEM((tm, k), BF16)],
        compiler_params=_params(*sem),
        name="resid_proj",
    )(*operands, w, x, mod)


def _ffn_up_kernel(x_ref, g_ref, mod_ref, wa_ref, wb_ref, cwa_ref, cwb_ref, cba_ref, cbb_ref, o_ref, h_ref):
    i = pl.program_id(0)

    @pl.when(pl.program_id(1) == 0)
    def _():
        h_ref[...] = _norm_mod(x_ref[...], g_ref[...], mod_ref[0], 3).astype(BF16)

    h = h_ref[...]

    def conv(w_ref, cw_ref, cb_ref):
        u = _dot_bf16(h, w_ref[...])
        prev, nxt = _seq_neighbours(u, i, _TM_SEQ)
        return prev * cw_ref[0:1, :] + u * cw_ref[1:2, :] + nxt * cw_ref[2:3, :] + cb_ref[...]

    o_ref[...] = (_silu(conv(wa_ref, cwa_ref, cba_ref)) * conv(wb_ref, cwb_ref, cbb_ref)).astype(o_ref.dtype)


def _ffn_up(x, g, mod, up, cw, cb):
    tm, tn = _TM_SEQ, 256
    nb = D_FF // tn
    cb = cb.reshape(1, 2 * D_FF)
    return pl.pallas_call(
        _ffn_up_kernel,
        grid=(N_TOK // tm, nb),
        in_specs=[pl.BlockSpec((tm, D_MODEL), lambda i, j: (i, 0)),
                  pl.BlockSpec((1, D_MODEL), lambda i, j: (0, 0)),
                  pl.BlockSpec((1, 8, D_MODEL), lambda i, j: (_group_of_tile(i, tm), 0, 0)),
                  pl.BlockSpec((D_MODEL, tn), lambda i, j: (0, j)),
                  pl.BlockSpec((D_MODEL, tn), lambda i, j: (0, j + nb)),
                  pl.BlockSpec((3, tn), lambda i, j: (0, j)),
                  pl.BlockSpec((3, tn), lambda i, j: (0, j + nb)),
                  pl.BlockSpec((1, tn), lambda i, j: (0, j)),
                  pl.BlockSpec((1, tn), lambda i, j: (0, j + nb))],
        out_specs=pl.BlockSpec((tm, tn), lambda i, j: (i, j)),
        out_shape=jax.ShapeDtypeStruct((N_TOK, D_FF), BF16),
        scratch_shapes=[pltpu.VMEM((tm, D_MODEL), BF16)],
        compiler_params=_params("parallel", "arbitrary"),
        name="ffn_up",
    )(x, g, mod, up, up, cw, cw, cb, cb)


_ROPE_TM = 512


def _rope(y, c, s1, s2):
    return y * c + pltpu.roll(y, 1, axis=1) * s1 + pltpu.roll(y, LANES - 1, axis=1) * s2


def _rope_tables(rot_dim, lane_offsets):
    t = np.arange(DEC_SEQ)
    row, col = t // GRID_W, t % GRID_W
    n_freq = rot_dim // 4
    inv = ROPE_BASE ** (-np.arange(n_freq, dtype=np.float64) / n_freq)
    ang = np.concatenate([row[:, None] * inv, col[:, None] * inv], -1)
    cos, sin = np.cos(ang), np.sin(ang)
    n = _ROPE_TM + DEC_SEQ
    c, s1, s2 = np.ones((n, LANES)), np.zeros((n, LANES)), np.zeros((n, LANES))
    for a in lane_offsets:
        even = a + 2 * np.arange(rot_dim // 2)
        c[_ROPE_TM:, even] = cos
        c[_ROPE_TM:, even + 1] = cos
        s1[_ROPE_TM:, even + 1] = sin
        s2[_ROPE_TM:, even] = -sin
    return tuple(jnp.asarray(x, F32) for x in (c, s1, s2))


def _rope_block(i):
    row = i * _ROPE_TM
    return jnp.where(row < N_CTX, 0, 1 + ((row - N_CTX) % DEC_SEQ) // _ROPE_TM)


def _mla_q_kernel(cq_ref, qnorm_ref, w_ref, qn_ref, c_ref, s1_ref, s2_ref, o_ref):
    xn = _rms(cq_ref[...], MLA_Q_RANK) * qnorm_ref[...]
    y = _dot_bf16(xn, w_ref[...])
    c, s1, s2 = c_ref[...], s1_ref[...], s2_ref[...]
    for h in range(MLA_HEADS):
        yh = y[:, h * LANES:(h + 1) * LANES]
        yh = _rms(yh, MLA_QK) * qn_ref[...]
        o_ref[:, h * LANES:(h + 1) * LANES] = _rope(yh, c, s1, s2)


def _mla_q(p, q_norm, w_uq_p, qn_p, tabs):
    tm = _ROPE_TM
    hw = MLA_HEADS * LANES
    tab_spec = pl.BlockSpec((tm, LANES), lambda i: (_rope_block(i), 0))
    return pl.pallas_call(
        _mla_q_kernel,
        grid=(N_TOK // tm,),
        in_specs=[pl.BlockSpec((tm, MLA_Q_RANK), lambda i: (i, 0)),
                  pl.BlockSpec((1, MLA_Q_RANK), lambda i: (0, 0)),
                  pl.BlockSpec((MLA_Q_RANK, hw), lambda i: (0, 0)),
                  pl.BlockSpec((1, LANES), lambda i: (0, 0)),
                  tab_spec, tab_spec, tab_spec],
        out_specs=pl.BlockSpec((tm, hw), lambda i: (i, 0)),
        out_shape=jax.ShapeDtypeStruct((N_TOK, hw), F32),
        compiler_params=_params("parallel"),
        name="mla_q",
    )(p, q_norm, w_uq_p, qn_p, *tabs)


def _mla_kv_kernel(ckv_ref, kr_ref, kvn_ref, wk_ref, wv_ref, kn_ref, c_ref, s1_ref, s2_ref,
                   k_ref, v_ref, ckvn_ref, *, norm_ckv):
    ckv = ckv_ref[...]
    if norm_ckv:
        ckv = _rms(ckv, MLA_KV_RANK) * kvn_ref[...]
    ckvn_ref[...] = ckv
    ckv_bf = ckv.astype(BF16)
    kk = _dot_bf16(ckv_bf, wk_ref[...])
    v_ref[...] = _dot_bf16(ckv_bf, wv_ref[...])
    kr = kr_ref[...]
    c, s1, s2 = c_ref[...], s1_ref[...], s2_ref[...]
    for h in range(MLA_HEADS):
        kh = kk[:, h * LANES:(h + 1) * LANES] + kr
        kh = _rms(kh, MLA_QK) * kn_ref[...]
        k_ref[:, h * LANES:(h + 1) * LANES] = _rope(kh, c, s1, s2)


def _mla_kv(ckv_src, ckv_blk, kr_src, kr_blk, kv_norm, wk_p, wv_p, kn_p, tabs, n_rows, own_tokens):
    tm = _ROPE_TM
    hw = MLA_HEADS * LANES
    tab_spec = pl.BlockSpec((tm, LANES), (lambda i: (_rope_block(i), 0)) if own_tokens else (lambda i: (0, 0)))
    return pl.pallas_call(
        functools.partial(_mla_kv_kernel, norm_ckv=own_tokens),
        grid=(n_rows // tm,),
        in_specs=[pl.BlockSpec((tm, LANES), lambda i: (i, ckv_blk)),
                  pl.BlockSpec((tm, LANES), lambda i: (i, kr_blk)),
                  pl.BlockSpec((1, LANES), lambda i: (0, 0)),
                  pl.BlockSpec((MLA_KV_RANK, hw), lambda i: (0, 0)),
                  pl.BlockSpec((MLA_KV_RANK, hw), lambda i: (0, 0)),
                  pl.BlockSpec((1, LANES), lambda i: (0, 0)),
                  tab_spec, tab_spec, tab_spec],
        out_specs=[pl.BlockSpec((tm, hw), lambda i: (i, 0)),
                   pl.BlockSpec((tm, hw), lambda i: (i, 0)),
                   pl.BlockSpec((tm, LANES), lambda i: (i, 0))],
        out_shape=[jax.ShapeDtypeStruct((n_rows, hw), F32),
                   jax.ShapeDtypeStruct((n_rows, hw), F32),
                   jax.ShapeDtypeStruct((n_rows, LANES), F32)],
        compiler_params=_params("parallel"),
        name="mla_kv",
    )(ckv_src, kr_src, kv_norm, wk_p, wv_p, kn_p, *tabs)


_LOG2E = math.log2(math.e)


def _softmax_parts(scores, scale):
    m = functools.reduce(jnp.maximum, [jnp.max(sc, axis=-1, keepdims=True) for sc in scores])
    ps = [jnp.exp2((sc - m) * (scale * _LOG2E)) for sc in scores]
    return ps, sum(jnp.sum(p, axis=-1, keepdims=True) for p in ps)


def _attend(q, sources, sl, scale):
    ps, l = _softmax_parts([_dot_nt_bf16(q, k_ref[:, sl]) for k_ref, _ in sources], scale)
    return sum(_dot_bf16(p, v_ref[:, sl]) for p, (_, v_ref) in zip(ps, sources)), l


def _mla_attn_kernel(*refs, pairs):
    q_ref, o_ref = refs[0], refs[-1]
    sources = [refs[1:3]] + ([refs[3:5]] if len(refs) == 6 else [])
    scale = MLA_QK ** -0.5
    for pr in range(pairs):
        outs = []
        for h in range(2):
            sl = slice((2 * pr + h) * LANES, (2 * pr + h + 1) * LANES)
            o, l = _attend(q_ref[:, sl], sources, sl, scale)
            outs.append(o / l)
        o_ref[:, pr * LANES:(pr + 1) * LANES] = (outs[0] + pltpu.roll(outs[1], MLA_V, axis=1)).astype(o_ref.dtype)


def _mla_attn(q, k, v, cache, batch, n, row0, tq, pairs):
    nqb = n // tq
    qb0 = row0 // tq
    kb0 = row0 // n
    wide = 2 * LANES * pairs
    kv_spec = pl.BlockSpec((n, wide), lambda b, h, i: (kb0 + b, h))
    in_specs, operands = [pl.BlockSpec((tq, wide), lambda b, h, i: (qb0 + b * nqb + i, h)), kv_spec, kv_spec], [q, k, v]
    if cache is not None:
        in_specs += [pl.BlockSpec((PAST_LEN, wide), lambda b, h, i: (b, h))] * 2
        operands += list(cache)
    return pl.pallas_call(
        functools.partial(_mla_attn_kernel, pairs=pairs),
        grid=(batch, MLA_HEADS // (2 * pairs), nqb),
        in_specs=in_specs,
        out_specs=pl.BlockSpec((tq, LANES * pairs), lambda b, h, i: (b * nqb + i, h)),
        out_shape=jax.ShapeDtypeStruct((batch * n, MLA_HEADS * MLA_V), BF16),
        compiler_params=_params("parallel", "parallel", "arbitrary"),
        name="mla_attn",
    )(*operands)


def _ret_kernel(lg_ref, q_ref, k_ref, v_ref, rg_ref, s0_ref, gn_ref, *out_and_scratch, n, tq, want_state):
    if want_state:
        o_ref, st_ref, decay_ref = out_and_scratch
    else:
        o_ref, decay_ref = out_and_scratch
    pair, qi, b = pl.program_id(0), pl.program_id(1), pl.program_id(2)
    q = q_ref[...]
    k = k_ref[...] * (RET_DK ** -0.5)
    lo = _lane_lo((1, LANES))
    row = (qi * tq + lax.broadcasted_iota(jnp.int32, (tq, 1), 0)).astype(F32)

    @pl.when(b == 0)
    def _():
        col = lax.broadcasted_iota(jnp.int32, (1, n), 1).astype(F32)
        diff = row - col
        for h in range(2):
            lgf = lg_ref[0, 2 * pair + h]
            lgb = lg_ref[1, 2 * pair + h]
            decay_ref[h] = (jnp.where(diff >= 0, jnp.exp(lgf * jnp.maximum(diff, 0.0)), 0.0)
                            + jnp.where(diff <= 0, jnp.exp(lgb * jnp.maximum(-diff, 0.0)), 0.0))

    for h in range(2):
        lgf = lg_ref[0, 2 * pair + h]
        lgb = lg_ref[1, 2 * pair + h]
        mask = lo if h == 0 else jnp.logical_not(lo)
        qh = jnp.where(mask, q, 0.0)
        vh = v_ref[:, h * LANES:(h + 1) * LANES]
        o = _dot_bf16(_dot_nt_bf16(qh, k) * decay_ref[h], vh)
        o = o + _dot_bf16(qh * jnp.exp(lgf * (row + 1.0)), s0_ref[0, 0])
        o = o + _dot_bf16(qh * jnp.exp(lgb * (n - row)), s0_ref[0, 1])
        y = _rms(o, RET_DV) * gn_ref[:, h * LANES:(h + 1) * LANES]
        o_ref[:, h * LANES:(h + 1) * LANES] = (_silu(rg_ref[:, h * LANES:(h + 1) * LANES]) * y).astype(o_ref.dtype)

    if want_state:
        pos = lax.broadcasted_iota(jnp.int32, (n, 1), 0).astype(F32)
        for d in range(2):
            acc = None
            for h in range(2):
                lg = lg_ref[d, 2 * pair + h]
                mask = lo if h == 0 else jnp.logical_not(lo)
                expo = (n - 1.0 - pos) if d == 0 else pos
                kd = jnp.where(mask, k * jnp.exp(lg * expo), 0.0)
                term = _dot_tn(kd, v_ref[:, h * LANES:(h + 1) * LANES])
                acc = term if acc is None else acc + term
            lg_rows = jnp.where(lax.broadcasted_iota(jnp.int32, (LANES, 1), 0) < 64,
                                lg_ref[d, 2 * pair], lg_ref[d, 2 * pair + 1])
            st_ref[0, d] = acc + s0_ref[0, d] * jnp.exp(lg_rows * n)


def _retention(log_g, p, q_blk, k_blk, v_blk, g_blk, s0, gn, batch, n, row0, tq, want_state):
    nqb = n // tq
    assert not want_state or nqb == 1
    qb0 = row0 // tq
    kb0 = row0 // n
    pairs = RET_HEADS // 2
    out_specs = [pl.BlockSpec((tq, 2 * LANES), lambda h, i, b: (b * nqb + i, h))]
    out_shape = [jax.ShapeDtypeStruct((batch * n, RET_HEADS * RET_DV), BF16)]
    if want_state:
        out_specs.append(pl.BlockSpec((1, 2, LANES, LANES), lambda h, i, b: (b, 0, h, 0)))
        out_shape.append(jax.ShapeDtypeStruct((batch, 2, RET_HEADS * RET_DK, RET_DV), F32))
    outs = pl.pallas_call(
        functools.partial(_ret_kernel, n=n, tq=tq, want_state=want_state),
        grid=(pairs, nqb, batch),
        in_specs=[pl.BlockSpec(memory_space=pltpu.SMEM),
                  pl.BlockSpec((tq, LANES), lambda h, i, b: (qb0 + b * nqb + i, q_blk + h)),
                  pl.BlockSpec((n, LANES), lambda h, i, b: (kb0 + b, k_blk + h)),
                  pl.BlockSpec((n, 2 * LANES), lambda h, i, b: (kb0 + b, v_blk // 2 + h)),
                  pl.BlockSpec((tq, 2 * LANES), lambda h, i, b: (qb0 + b * nqb + i, g_blk // 2 + h)),
                  pl.BlockSpec((1, 2, LANES, LANES), lambda h, i, b: (b, 0, h, 0)),
                  pl.BlockSpec((1, 2 * LANES), lambda h, i, b: (0, h))],
        out_specs=out_specs,
        out_shape=out_shape,
        scratch_shapes=[pltpu.VMEM((2, tq, n), F32)],
        compiler_params=_params("parallel", "parallel", "arbitrary"),
        name="retention",
    )(log_g, p, p, p, p, s0, gn)
    return outs if want_state else (outs[0], None)


def _diff_qk_kernel(q_ref, k_ref, v_ref, qn_ref, kn_ref, c_ref, s1_ref, s2_ref, qo_ref, ko_ref, dk_ref, dv_ref):
    c, s1, s2 = c_ref[...], s1_ref[...], s2_ref[...]
    is_ctx = pl.program_id(0) < N_CTX // _ROPE_TM
    for src, gain, dst in ((q_ref, qn_ref, qo_ref), (k_ref, kn_ref, ko_ref)):
        for h in range(DIFF_HEADS):
            sl = slice(h * LANES, (h + 1) * LANES)
            y = src[:, sl]
            y = _rope(y * lax.rsqrt(_seg64_sum(y * y) * (1.0 / DIFF_DH) + EPS) * gain[...], c, s1, s2)
            dst[:, sl] = y
            if dst is ko_ref:
                @pl.when(is_ctx)
                def _():
                    dk_ref[:, h, 0, :] = y[:, :DIFF_DH]
                    dk_ref[:, h, 1, :] = pltpu.roll(y, DIFF_DH, axis=1)[:, :DIFF_DH]
                    dv_ref[:, h, :] = v_ref[:, sl]


def _diff_qk(p, qn_p, kn_p, tabs):
    tm = _ROPE_TM
    n_ctx = N_CTX // tm
    tab_spec = pl.BlockSpec((tm, LANES), lambda i: (_rope_block(i), 0))
    ctx_blk = lambda i: jnp.minimum(i, n_ctx - 1)
    return pl.pallas_call(
        _diff_qk_kernel,
        grid=(N_TOK // tm,),
        in_specs=[pl.BlockSpec((tm, DIFF_W), lambda i: (i, 0)),
                  pl.BlockSpec((tm, DIFF_W), lambda i: (i, 1)),
                  pl.BlockSpec((tm, DIFF_W), lambda i: (i, 2)),
                  pl.BlockSpec((1, LANES), lambda i: (0, 0)),
                  pl.BlockSpec((1, LANES), lambda i: (0, 0)),
                  tab_spec, tab_spec, tab_spec],
        out_specs=[pl.BlockSpec((tm, DIFF_W), lambda i: (i, 0)),
                   pl.BlockSpec((tm, DIFF_W), lambda i: (i, 0)),
                   pl.BlockSpec((tm, DIFF_HEADS, 2, DIFF_DH), lambda i: (ctx_blk(i), 0, 0, 0)),
                   pl.BlockSpec((tm, DIFF_HEADS, 2 * DIFF_DH), lambda i: (ctx_blk(i), 0, 0))],
        out_shape=[jax.ShapeDtypeStruct((N_TOK, DIFF_W), F32)] * 2
        + [jax.ShapeDtypeStruct((N_CTX, DIFF_HEADS, 2, DIFF_DH), F32),
           jax.ShapeDtypeStruct((N_CTX, DIFF_HEADS, 2 * DIFF_DH), F32)],
        compiler_params=_params("arbitrary"),
        name="diff_qk",
    )(p, p, p, qn_p, kn_p, *tabs)


def _diff_attn_kernel(*refs, lam_init, heads):
    lam_ref, q_ref, gn_ref, o_ref = refs[0], refs[1], refs[-2], refs[-1]
    sources = [refs[2:4]] + ([refs[4:6]] if len(refs) == 8 else [])
    lv = lam_ref[...]
    lam = (jnp.exp(jnp.sum(lv[0:1] * lv[1:2], axis=-1, keepdims=True))
           - jnp.exp(jnp.sum(lv[2:3] * lv[3:4], axis=-1, keepdims=True)) + lam_init)
    scale = DIFF_DH ** -0.5
    lo = _lane_lo((1, LANES))
    for h in range(heads):
        sl = slice(h * LANES, (h + 1) * LANES)
        q = q_ref[:, sl]
        kbs = [k_ref[:, sl].astype(BF16) for k_ref, _ in sources]
        ps1, l1 = _softmax_parts([_dot_nt_bf16(jnp.where(lo, q, 0.0), kb) for kb in kbs], scale)
        ps2, l2 = _softmax_parts([_dot_nt_bf16(jnp.where(lo, 0.0, q), kb) for kb in kbs], scale)
        o = sum(_dot_bf16(p1 / l1 - lam * (p2 / l2), v_ref[:, sl])
                for p1, p2, (_, v_ref) in zip(ps1, ps2, sources))
        o_ref[:, sl] = (_rms(o, 2 * DIFF_DH) * gn_ref[:, sl] * (1.0 - lam_init)).astype(o_ref.dtype)


def _diff_attn(lam, q, k, v, v_blk0, cache, gn, batch, n, row0, tq, lam_init, heads):
    nqb = n // tq
    qb0 = row0 // tq
    kb0 = row0 // n
    wide = LANES * heads
    vb0 = v_blk0 // heads
    in_specs = [pl.BlockSpec((4, DIFF_DH), lambda b, h, i: (0, 0)),
                pl.BlockSpec((tq, wide), lambda b, h, i: (qb0 + b * nqb + i, h)),
                pl.BlockSpec((n, wide), lambda b, h, i: (kb0 + b, h)),
                pl.BlockSpec((n, wide), lambda b, h, i: (kb0 + b, vb0 + h))]
    operands = [lam, q, k, v]
    if cache is not None:
        in_specs += [pl.BlockSpec((PAST_LEN, wide), lambda b, h, i: (b, h))] * 2
        operands += list(cache)
    return pl.pallas_call(
        functools.partial(_diff_attn_kernel, lam_init=lam_init, heads=heads),
        grid=(batch, DIFF_HEADS // heads, nqb),
        in_specs=in_specs + [pl.BlockSpec((1, wide), lambda b, h, i: (0, h))],
        out_specs=pl.BlockSpec((tq, wide), lambda b, h, i: (b * nqb + i, h)),
        out_shape=jax.ShapeDtypeStruct((batch * n, DIFF_W), BF16),
        compiler_params=_params("parallel", "parallel", "arbitrary"),
        name="diff_attn",
    )(*operands, gn)


def _seg64_sum_wide(x):
    return jnp.concatenate([_seg64_sum(x[:, j * LANES:(j + 1) * LANES]) for j in range(x.shape[1] // LANES)], axis=1)


_SCAN_SLOTS = 3
_SCAN_NJ = _SCAN_SLOTS * RWKV_HEADS


def _rwkv_pre_kernel(r_ref, k_ref, v_ref, lo_ref, wup_ref, aup_ref, gup_ref, w0_ref, a0_ref, kk_ref, ka_ref, rk_ref,
                     op_ref, g_ref, bonus_ref):
    W = RWKV_W
    col = lambda q: slice(q * W, (q + 1) * W)
    r = r_ref[...]
    k = k_ref[...]
    v = v_ref[...]
    lora = lo_ref[...]
    kk = k * kk_ref[...]
    kkn = kk * lax.rsqrt(_seg64_sum_wide(kk * kk) + EPS)
    g_ref[...] = _dot_bf16(_sigmoid(lora[:, 2 * LANES:3 * LANES]), gup_ref[...])
    pre = w0_ref[...] + _dot_bf16(jnp.tanh(lora[:, 0:LANES]), wup_ref[...])
    decay = jnp.exp(-jnp.exp(-_softplus(-pre) - 0.5))
    a = _sigmoid(a0_ref[...] + _dot_bf16(lora[:, LANES:2 * LANES], aup_ref[...]))
    lo = _lane_lo((1, LANES))
    bonus = None
    for d in range(2):
        a_d = a[:, col(d)]
        k_d = k * (1.0 + (a_d - 1.0) * ka_ref[...])
        t = _seg64_sum_wide(r * k_d * rk_ref[...])
        bonus = t if bonus is None else bonus + t
        for s, (x1, x2) in enumerate(((kkn, decay[:, col(d)]), (k_d, kkn * a_d), (r, v))):
            for h in range(RWKV_HEADS):
                blk = slice((h // 2) * LANES, (h // 2 + 1) * LANES)
                if h % 2 == 0:
                    out = jnp.where(lo, x1[:, blk], pltpu.roll(x2[:, blk], RWKV_HS, axis=1))
                else:
                    out = jnp.where(lo, pltpu.roll(x1[:, blk], RWKV_HS, axis=1), x2[:, blk])
                op_ref[:, d * _SCAN_NJ + s * RWKV_HEADS + h, :] = out
    bonus_ref[...] = bonus


def _rwkv_pre(p, r_blk, k_blk, v_blk, lo_blk, wup_bd, aup_bd, gup, w0, a0, k_k, k_a, r_k):
    tm = 256
    w = RWKV_W
    row = lambda n: pl.BlockSpec((1, n), lambda i: (0, 0))
    full = lambda a, b: pl.BlockSpec((a, b), lambda i: (0, 0))
    return pl.pallas_call(
        _rwkv_pre_kernel,
        grid=(N_TOK // tm,),
        in_specs=[pl.BlockSpec((tm, w), lambda i: (i, r_blk)),
                  pl.BlockSpec((tm, w), lambda i: (i, k_blk)),
                  pl.BlockSpec((tm, w), lambda i: (i, v_blk)),
                  pl.BlockSpec((tm, 3 * LANES), lambda i: (i, lo_blk)),
                  full(LANES, 2 * w), full(LANES, 2 * w), full(LANES, w),
                  row(2 * w), row(2 * w), row(w), row(w), row(w)],
        out_specs=[pl.BlockSpec((tm, 2 * _SCAN_NJ, LANES), lambda i: (i, 0, 0)),
                   pl.BlockSpec((tm, w), lambda i: (i, 0)), pl.BlockSpec((tm, w), lambda i: (i, 0))],
        out_shape=[jax.ShapeDtypeStruct((N_TOK, 2 * _SCAN_NJ, LANES), F32),
                   jax.ShapeDtypeStruct((N_TOK, w), F32), jax.ShapeDtypeStruct((N_TOK, w), F32)],
        compiler_params=_params("parallel"),
        name="rwkv_pre",
    )(p, p, p, p, wup_bd, aup_bd, gup, w0, a0, k_k, k_a, r_k)


_SCAN_CHUNK = 32
_SCAN_UNROLL = 16


def _rwkv_first_sa(s_ref, sa_ref, kk, n_k):
    nv = s_ref.shape[1]
    chunk = min(_SCAN_CHUNK, nv)
    for c0 in range(0, nv, chunk):
        def body(k, acc):
            return acc + s_ref[k, c0:c0 + chunk, :] * kk(k)
        sa_ref[c0:c0 + chunk, :] = lax.fori_loop(0, n_k, body, jnp.zeros((chunk, LANES), F32), unroll=_SCAN_UNROLL)


def _rwkv_step(s_ref, sa_ref, kk_next, w, kd, b, r, v_at, n_k):
    nv = s_ref.shape[1]
    chunk = min(_SCAN_CHUNK, nv)
    ys = []
    for c0 in range(0, nv, chunk):
        sa = sa_ref[c0:c0 + chunk, :]
        vc = v_at(c0, chunk)

        def body(k, acc):
            y_acc, sa_acc = acc
            s_new = s_ref[k, c0:c0 + chunk, :] * w(k) - sa * b(k) + vc * kd(k)
            s_ref[k, c0:c0 + chunk, :] = s_new
            return y_acc + s_new * r(k), sa_acc + s_new * kk_next(k)

        zero = jnp.zeros((chunk, LANES), F32)
        y_acc, sa_acc = lax.fori_loop(0, n_k, body, (zero, zero), unroll=_SCAN_UNROLL)
        sa_ref[c0:c0 + chunk, :] = sa_acc
        ys.append(y_acc)
    return ys[0] if len(ys) == 1 else jnp.concatenate(ys, axis=0)


_CTX_TB = 32


def _load_scan_operands(kt_ref, t, slabs):
    for s, x in enumerate(slabs):
        xt = x.T
        kt_ref[t, 2 * s] = xt[:RWKV_HS]
        kt_ref[t, 2 * s + 1] = xt[RWKV_HS:]


def _rwkv_scan_ctx_kernel(x_ref, y_ref, st_ref, s_ref, sa_ref, kt_ref, ys_ref):
    d = pl.program_id(0)
    tb = pl.program_id(1)
    nh = RWKV_HEADS
    step_t = lambda i: jnp.where(d == 0, i, _CTX_TB - 1 - i)

    @pl.when(tb == 0)
    def _():
        s_ref[...] = jnp.zeros_like(s_ref)

    def load_t(t, carry):
        _load_scan_operands(kt_ref, t, [
            jnp.concatenate([x_ref[b, t, s * nh:(s + 1) * nh, :] for b in range(BATCH)], axis=0)
            for s in range(_SCAN_SLOTS)])
        return carry

    lax.fori_loop(0, _CTX_TB, load_t, 0, unroll=4)

    t0 = step_t(0)
    _rwkv_first_sa(s_ref, sa_ref, lambda k: kt_ref[t0, 0, pl.ds(k, 1), :], RWKV_HS)

    def step(i, carry):
        t = step_t(i)
        tn = step_t(jnp.minimum(i + 1, _CTX_TB - 1))
        row = lambda q, tt: (lambda k: kt_ref[tt, q, pl.ds(k, 1), :])
        ys_ref[t] = _rwkv_step(s_ref, sa_ref, row(0, tn), row(1, t), row(2, t), row(3, t), row(4, t),
                               lambda c0, n: kt_ref[t, 5, pl.ds(c0, n), :], RWKV_HS)
        return carry

    lax.fori_loop(0, _CTX_TB, step, 0)

    def store_t(i, carry):
        z = jnp.concatenate([ys_ref[2 * i], ys_ref[2 * i + 1]], axis=0).T
        z_odd = pltpu.roll(z, RWKV_HS, axis=1)
        for b in range(BATCH):
            y_ref[b, 2 * i] = z[b * nh:(b + 1) * nh, :RWKV_HS]
            y_ref[b, 2 * i + 1] = z_odd[b * nh:(b + 1) * nh, :RWKV_HS]
        return carry

    lax.fori_loop(0, _CTX_TB // 2, store_t, 0, unroll=4)

    @pl.when(tb == pl.num_programs(1) - 1)
    def _():
        st_ref[0] = s_ref[...]


def _rwkv_scan_ctx(op4):
    nt = SEQ // _CTX_TB
    hs = RWKV_HS
    tblk = lambda d, tb: jnp.where(d == 0, tb, nt - 1 - tb)
    return pl.pallas_call(
        _rwkv_scan_ctx_kernel,
        grid=(2, nt),
        in_specs=[pl.BlockSpec((BATCH, _CTX_TB, _SCAN_NJ, LANES), lambda d, tb: (0, tblk(d, tb), d, 0))],
        out_specs=[pl.BlockSpec((BATCH, _CTX_TB, RWKV_HEADS, hs), lambda d, tb: (0, tblk(d, tb), d, 0)),
                   pl.BlockSpec((1, hs, hs, LANES), lambda d, tb: (d, 0, 0, 0))],
        out_shape=[jax.ShapeDtypeStruct((BATCH, SEQ, 2 * RWKV_HEADS, hs), F32),
                   jax.ShapeDtypeStruct((2, hs, hs, LANES), F32)],
        scratch_shapes=[pltpu.VMEM((hs, hs, LANES), F32), pltpu.VMEM((hs, LANES), F32),
                        pltpu.VMEM((_CTX_TB, 2 * _SCAN_SLOTS, hs, LANES), F32),
                        pltpu.VMEM((_CTX_TB, hs, LANES), F32)],
        compiler_params=_params("parallel", "arbitrary"),
        name="rwkv_scan_ctx",
    )(op4)


_LAT_TB = 32
_LAT_VSPLIT = 4
_LAT_STATES = 2 * DEC_BATCH * RWKV_HEADS
_LAT_VROWS = RWKV_HS // _LAT_VSPLIT


def _rwkv_scan_lat_kernel(xf0_ref, xf1_ref, xb0_ref, xb1_ref, s0_ref, yf_ref, yb_ref,
                          s_ref, sa_ref, kt_ref, v_ref, ys_ref):
    @pl.when(pl.program_id(0) == 0)
    def _():
        s_ref[...] = s0_ref[...]

    group = lax.broadcasted_iota(jnp.int32, (_LAT_VROWS, LANES), 1) // _LAT_STATES
    nh = RWKV_HEADS

    def load_t(t, carry):
        tr = _LAT_TB - 1 - t
        for s in range(_SCAN_SLOTS):
            heads = slice(s * nh, (s + 1) * nh)
            x = jnp.concatenate([xf0_ref[0, t, heads, :], xf1_ref[0, t, heads, :],
                                 xb0_ref[0, tr, heads, :], xb1_ref[0, tr, heads, :]], axis=0)
            xt = jnp.concatenate([x] * _LAT_VSPLIT, axis=0).T
            kt_ref[t, 2 * s] = xt[:RWKV_HS]
            if s < _SCAN_SLOTS - 1:
                kt_ref[t, 2 * s + 1] = xt[RWKV_HS:]
            else:
                v = jnp.zeros((_LAT_VROWS, LANES), F32)
                for g in range(_LAT_VSPLIT):
                    r0 = RWKV_HS + g * _LAT_VROWS
                    v = jnp.where(group == g, xt[r0:r0 + _LAT_VROWS, :], v)
                v_ref[t] = v
        return carry

    lax.fori_loop(0, _LAT_TB, load_t, 0, unroll=8)

    _rwkv_first_sa(s_ref, sa_ref, lambda k: kt_ref[0, 0, pl.ds(k, 1), :], RWKV_HS)

    def step(t, carry):
        tn = jnp.minimum(t + 1, _LAT_TB - 1)
        row = lambda q, tt: (lambda k: kt_ref[tt, q, pl.ds(k, 1), :])
        ys_ref[t] = _rwkv_step(s_ref, sa_ref, row(0, tn), row(1, t), row(2, t), row(3, t), row(4, t),
                               lambda c0, n: v_ref[t, pl.ds(c0, n), :], RWKV_HS)
        return carry

    lax.fori_loop(0, _LAT_TB, step, 0)

    def store_t(i, carry):
        rows = [jnp.where(group == g, ys_ref[2 * i + j], 0.0) for j in range(2) for g in range(_LAT_VSPLIT)]
        z = jnp.concatenate(rows, axis=0).T
        y = (z[0:_LAT_STATES] + z[_LAT_STATES:2 * _LAT_STATES]
             + z[2 * _LAT_STATES:3 * _LAT_STATES] + z[3 * _LAT_STATES:4 * _LAT_STATES])
        y_odd = pltpu.roll(y, RWKV_HS, axis=1)
        half = _LAT_STATES // 2
        for b in range(DEC_BATCH):
            rows_f = slice(b * nh, (b + 1) * nh)
            rows_b = slice(half + b * nh, half + (b + 1) * nh)
            yf_ref[b, 2 * i] = y[rows_f, :RWKV_HS]
            yf_ref[b, 2 * i + 1] = y_odd[rows_f, :RWKV_HS]
            yb_ref[b, _LAT_TB - 1 - 2 * i] = y[rows_b, :RWKV_HS]
            yb_ref[b, _LAT_TB - 2 - 2 * i] = y_odd[rows_b, :RWKV_HS]
        return carry

    lax.fori_loop(0, _LAT_TB // 2, store_t, 0, unroll=4)


def _rwkv_scan_lat(op4, s0):
    hs = RWKV_HS
    nv = _LAT_VROWS
    nt = DEC_SEQ // _LAT_TB
    per_seq = SEQ // _LAT_TB
    first = N_CTX // SEQ

    def x_spec(b, d):
        tblk = (lambda tb: tb) if d == 0 else (lambda tb: nt - 1 - tb)
        return pl.BlockSpec((1, _LAT_TB, _SCAN_NJ, LANES),
                            lambda tb: (first + b * (DEC_SEQ // SEQ) + tblk(tb) // per_seq, tblk(tb) % per_seq, d, 0))

    y_shape = jax.ShapeDtypeStruct((DEC_BATCH, DEC_SEQ, RWKV_HEADS, hs), F32)
    y_blk = (DEC_BATCH, _LAT_TB, RWKV_HEADS, hs)
    return pl.pallas_call(
        _rwkv_scan_lat_kernel,
        grid=(nt,),
        in_specs=[x_spec(0, 0), x_spec(1, 0), x_spec(0, 1), x_spec(1, 1),
                  pl.BlockSpec((hs, nv, LANES), lambda tb: (0, 0, 0))],
        out_specs=[pl.BlockSpec(y_blk, lambda tb: (0, tb, 0, 0)),
                   pl.BlockSpec(y_blk, lambda tb: (0, nt - 1 - tb, 0, 0))],
        out_shape=[y_shape, y_shape],
        scratch_shapes=[pltpu.VMEM((hs, nv, LANES), F32), pltpu.VMEM((nv, LANES), F32),
                        pltpu.VMEM((_LAT_TB, 2 * _SCAN_SLOTS - 1, hs, LANES), F32),
                        pltpu.VMEM((_LAT_TB, nv, LANES), F32), pltpu.VMEM((_LAT_TB, nv, LANES), F32)],
        compiler_params=_params("arbitrary"),
        name="rwkv_scan_lat",
    )(op4, op4, op4, op4, s0)


def _rwkv_post_kernel(yc_ref, ylf_ref, ylb_ref, bonus_ref, v_ref, g_ref, gn_ref, o_ref):
    def finish(head_sum):
        y = jnp.concatenate([head_sum(h) for h in range(RWKV_HEADS)], axis=1)
        y = y * lax.rsqrt(_seg64_sum_wide(y * y) * (1.0 / RWKV_HS) + EPS) * gn_ref[...]
        o_ref[...] = ((y + bonus_ref[...] * v_ref[...]) * g_ref[...]).astype(o_ref.dtype)

    @pl.when(pl.program_id(0) < N_CTX // SEQ)
    def _():
        finish(lambda h: yc_ref[0, :, h, :] + yc_ref[0, :, RWKV_HEADS + h, :])

    @pl.when(pl.program_id(0) >= N_CTX // SEQ)
    def _():
        finish(lambda h: ylf_ref[0, :, h, :] + ylb_ref[0, :, h, :])


def _rwkv_post(y_ctx, y_lat_f, y_lat_b, bonus, p, v_blk, g, gn):
    tm = SEQ
    w = RWKV_W
    hs = RWKV_HS
    n_ctx = N_CTX // SEQ
    lat = lambda y: y.reshape(N_LAT // SEQ, SEQ, RWKV_HEADS, hs)
    spec = pl.BlockSpec((tm, w), lambda i: (i, 0))
    lat_spec = pl.BlockSpec((1, SEQ, RWKV_HEADS, hs), lambda i: (jnp.maximum(i - n_ctx, 0), 0, 0, 0))
    return pl.pallas_call(
        _rwkv_post_kernel,
        grid=(N_TOK // tm,),
        in_specs=[pl.BlockSpec((1, SEQ, 2 * RWKV_HEADS, hs), lambda i: (jnp.minimum(i, n_ctx - 1), 0, 0, 0)),
                  lat_spec, lat_spec, spec, pl.BlockSpec((tm, w), lambda i: (i, v_blk)), spec,
                  pl.BlockSpec((1, w), lambda i: (0, 0))],
        out_specs=spec,
        out_shape=jax.ShapeDtypeStruct((N_TOK, w), BF16),
        compiler_params=_params("parallel"),
        name="rwkv_post",
    )(y_ctx, lat(y_lat_f), lat(y_lat_b), bonus, p, g, gn)


def _value_split_layout(x):
    lead = x.shape[:-2]
    n = len(lead)
    x = x.reshape(lead + (_LAT_STATES, _LAT_VSPLIT, _LAT_VROWS))
    return jnp.transpose(x, tuple(range(n)) + (n + 2, n + 1, n)).reshape(lead + (_LAT_VROWS, LANES))


def _even_layer(x, mod, g_mix, w_in, q_norm, kv_norm, w_uq, w_ukv, qn, kn, ret_decay, ret_gn,
                cache_ckv, cache_krope, state_ret, tabs_m):
    cq, ckv, krope, rq, rk, rv, rg = jnp.split(w_in, np.cumsum(
        (MLA_Q_RANK, MLA_KV_RANK, MLA_ROPE, RET_HEADS * RET_DK, RET_HEADS * RET_DK, RET_HEADS * RET_DV))[:].tolist(),
        axis=1)
    z = lambda n: jnp.zeros((D_MODEL, n), F32)
    w_p = jnp.concatenate([cq, ckv, z(MLA_NOPE), krope, z(LANES - MLA_QK), rq, rk, rv, rg], axis=1)
    p = _inproj(x, g_mix, mod, w_p, jnp.zeros((1, w_p.shape[1]), F32), 512, None)
    CKV_BLK, KR_BLK, RQ_BLK, RK_BLK, RV_BLK, RG_BLK = 2, 3, 4, 6, 8, 12

    def head_pad(w, n_head, d_head, c0, c1):
        w = w.reshape(w.shape[0], n_head, d_head)[:, :, c0:c1]
        return jnp.pad(w, ((0, 0), (0, 0), (0, LANES - (c1 - c0)))).reshape(w.shape[0], n_head * LANES)

    w_uq_p = head_pad(w_uq, MLA_HEADS, MLA_QK, 0, MLA_QK)
    wk_p = head_pad(w_ukv, MLA_HEADS, MLA_NOPE + MLA_V, 0, MLA_NOPE)
    wv_p = head_pad(w_ukv, MLA_HEADS, MLA_NOPE + MLA_V, MLA_NOPE, MLA_NOPE + MLA_V)
    qn_p = jnp.pad(qn, (0, LANES - MLA_QK))[None]
    kn_p = jnp.pad(kn, (0, LANES - MLA_QK))[None]

    q = _mla_q(p, q_norm[None], w_uq_p, qn_p, tabs_m)
    k, v, ckvn = _mla_kv(p, CKV_BLK, p, KR_BLK, kv_norm[None], wk_p, wv_p, kn_p, tabs_m, N_TOK, True)

    n_c = DEC_BATCH * PAST_LEN
    kr_c = jnp.pad(cache_krope.reshape(n_c, MLA_ROPE), ((0, 0), (MLA_NOPE, LANES - MLA_QK)))
    k_c, v_c, _ = _mla_kv(cache_ckv.reshape(n_c, MLA_KV_RANK), 0, kr_c, 0, kv_norm[None], wk_p, wv_p, kn_p,
                          tabs_m, n_c, False)

    o_ctx = _mla_attn(q, k, v, None, BATCH, SEQ, 0, SEQ, MLA_HEADS // 2)
    o_lat = _mla_attn(q, k, v, (k_c, v_c), DEC_BATCH, DEC_SEQ, N_CTX, 256, 1)

    log_g = -_softplus(-ret_decay)
    gn = ret_gn[None]
    s0_ctx = jnp.zeros((BATCH, 2, RET_HEADS * RET_DK, RET_DV), F32)
    r_ctx, st_ctx = _retention(log_g, p, RQ_BLK, RK_BLK, RV_BLK, RG_BLK, s0_ctx, gn, BATCH, SEQ, 0, SEQ, True)
    s0_lat = state_ret.reshape(DEC_BATCH, 2, RET_HEADS * RET_DK, RET_DV)
    r_lat, _ = _retention(log_g, p, RQ_BLK, RK_BLK, RV_BLK, RG_BLK, s0_lat, gn, DEC_BATCH, DEC_SEQ, N_CTX, 256,
                          False)

    mix = [(o_ctx, o_lat), (r_ctx, r_lat)]
    new_ckv = ckvn[:N_CTX].reshape(BATCH, SEQ, MLA_KV_RANK)
    new_krope = p[:N_CTX, KR_BLK * LANES + MLA_NOPE:KR_BLK * LANES + MLA_QK]
    new_krope = new_krope.reshape(BATCH, SEQ, MLA_ROPE)
    new_ret = st_ctx.reshape(BATCH, 2, RET_HEADS, RET_DK, RET_DV)
    return mix, new_ckv, new_krope, new_ret


def _odd_layer(x, mod, g_mix, w_in, qn, kn, lam, diff_gn, mu, w0, w_up, a0, a_up, g_up, k_k, k_a, r_k, gn,
               cache_k, cache_v, state_rwkv, tabs_d, lam_init):
    w_p = w_in
    n_in = w_p.shape[1]
    mu_full = jnp.concatenate([jnp.zeros((3 * DIFF_W,), F32), mu])[None]
    p = _inproj(x, g_mix, mod, w_p, mu_full, 384, (3 * DIFF_W) // 384)
    DV_BLK, R_BLK, K_BLK, V_BLK = 2, 3, 4, 5
    LO_BLK = (6 * RWKV_W) // (3 * LANES)

    qn_p = jnp.tile(qn, 2)[None]
    kn_p = jnp.tile(kn, 2)[None]
    q, k, dk, dv = _diff_qk(p, qn_p, kn_p, tabs_d)

    n_c = DEC_BATCH * PAST_LEN
    cache = (cache_k.reshape(n_c, DIFF_W), cache_v.reshape(n_c, DIFF_W))
    dgn = diff_gn[None]
    o_ctx = _diff_attn(lam, q, k, p, DV_BLK * DIFF_HEADS, None, dgn, BATCH, SEQ, 0, SEQ, lam_init, DIFF_HEADS)
    o_lat = _diff_attn(lam, q, k, p, DV_BLK * DIFF_HEADS, cache, dgn, DEC_BATCH, DEC_SEQ, N_CTX, 256, lam_init, 1)

    zero = jnp.zeros((RWKV_W_LORA, RWKV_W), F32)
    wup_bd = jnp.concatenate([jnp.concatenate([w_up[0], zero], 1), jnp.concatenate([zero, w_up[1]], 1)], 0)
    aup_bd = jnp.concatenate([jnp.concatenate([a_up[0], zero], 1), jnp.concatenate([zero, a_up[1]], 1)], 0)
    op, g, bonus = _rwkv_pre(p, R_BLK, K_BLK, V_BLK, LO_BLK, wup_bd, aup_bd, g_up, w0.reshape(1, -1),
                             a0.reshape(1, -1), k_k[None], k_a[None], r_k.reshape(1, -1))
    op4 = op.reshape(N_TOK // SEQ, SEQ, 2 * _SCAN_NJ, LANES)
    y_ctx, st_ctx = _rwkv_scan_ctx(op4)
    s0_lat = jnp.transpose(state_rwkv, (4, 1, 0, 2, 3)).reshape(RWKV_HS, _LAT_STATES, RWKV_HS)
    y_lat_f, y_lat_b = _rwkv_scan_lat(op4, _value_split_layout(s0_lat))
    rw_o = _rwkv_post(y_ctx, y_lat_f, y_lat_b, bonus, p, V_BLK, g, gn[None])

    mix = [(o_ctx, o_lat), rw_o]
    new_dk = dk.reshape(BATCH, SEQ, DIFF_HEADS, 2, DIFF_DH)
    new_dv = dv.reshape(BATCH, SEQ, DIFF_HEADS, 2 * DIFF_DH)
    new_rwkv = jnp.transpose(st_ctx.reshape(2, RWKV_HS, RWKV_HS, BATCH, RWKV_HEADS), (3, 0, 4, 2, 1))
    return mix, new_dk, new_dv, new_rwkv


def kernel(x_prompt, x_sample, cache_mla_ckv, cache_mla_krope, state_ret, cache_diff_k, cache_diff_v, state_rwkv,
           c, c_ctx, ada_w, ada_b, norm_mix_g, norm_ffn_g, w_out, ffn_up, ffn_conv_w, ffn_conv_b, ffn_down,
           a_w_in, mla_q_norm, mla_kv_norm, mla_w_uq, mla_w_ukv, mla_qn, mla_kn, ret_decay, ret_gn,
           b_w_in, diff_qn, diff_kn, diff_lam, diff_gn, rwkv_mu, rwkv_w0, rwkv_w_up, rwkv_a0, rwkv_a_up,
           rwkv_g_up, rwkv_k_k, rwkv_k_a, rwkv_r_k, rwkv_gn):
    x = jnp.concatenate([x_prompt.reshape(N_CTX, D_MODEL), x_sample.reshape(N_LAT, D_MODEL)], 0)
    cond8 = jnp.pad(jnp.concatenate([c_ctx[None], c], 0), ((0, 8 - N_GROUPS), (0, 0)))
    mod = _modulation(cond8, ada_w, ada_b)

    tabs_m = _rope_tables(MLA_ROPE, (MLA_NOPE,))
    tabs_d = _rope_tables(DIFF_DH, (0, DIFF_DH))

    outs = {}
    for l in range(DEPTH):
        j = l // 2
        g_mix = norm_mix_g[l][None]
        if l % 2 == 0:
            mix, outs["ckv"], outs["krope"], outs["ret"] = _even_layer(
                x, mod[l], g_mix, a_w_in[j], mla_q_norm[j], mla_kv_norm[j], mla_w_uq[j], mla_w_ukv[j], mla_qn[j],
                mla_kn[j], ret_decay[j], ret_gn[j], cache_mla_ckv[:, j], cache_mla_krope[:, j], state_ret[:, j],
                tabs_m)
        else:
            lam_init = 0.8 - 0.6 * math.exp(-0.3 * l)
            mix, outs["dk"], outs["dv"], outs["rwkv"] = _odd_layer(
                x, mod[l], g_mix, b_w_in[j], diff_qn[j], diff_kn[j], diff_lam[j], diff_gn[j], rwkv_mu[j],
                rwkv_w0[j], rwkv_w_up[j], rwkv_a0[j], rwkv_a_up[j], rwkv_g_up[j], rwkv_k_k[j], rwkv_k_a[j],
                rwkv_r_k[j], rwkv_gn[j], cache_diff_k[:, j], cache_diff_v[:, j], state_rwkv[:, j], tabs_d, lam_init)
        x = _resid_proj(mix, w_out[l], x, mod[l], 2)
        act = _ffn_up(x, norm_ffn_g[l][None], mod[l], ffn_up[l], ffn_conv_w[l], ffn_conv_b[l])
        x = _resid_proj([act], ffn_down[l], x, mod[l], 5, split_out=(l == DEPTH - 1))

    y_prompt = x[0].reshape(BATCH, SEQ, D_MODEL)
    y_sample = x[1].reshape(DEC_BATCH, DEC_SEQ, D_MODEL)
    return (y_prompt, y_sample, outs["ckv"][:, None], outs["krope"][:, None], outs["ret"][:, None],
            outs["dk"][:, None], outs["dv"][:, None], outs["rwkv"][:, None])
```

```python
import functools
import math

import numpy as np
import jax
import jax.numpy as jnp
from jax import lax
from jax.experimental import pallas as pl
from jax.experimental.pallas import tpu as pltpu

D_MODEL = 1024
BATCH = 16
SEQ = 256
DEPTH = 2
DEC_BATCH = 2
DEC_SEQ = 1024
PAST_LEN = 512
GRID_W = 64
EPS = 1e-6
ROPE_BASE = 10000.0

MLA_HEADS = 8
MLA_Q_RANK = 256
MLA_KV_RANK = 128
MLA_NOPE = 64
MLA_ROPE = 32
MLA_V = 64
MLA_QK = MLA_NOPE + MLA_ROPE
RET_HEADS = 4
RET_DK = 64
RET_DV = 128
DIFF_HEADS = 4
DIFF_DH = 64
DIFF_W = DIFF_HEADS * 2 * DIFF_DH
RWKV_HEADS = 8
RWKV_HS = 64
RWKV_W = RWKV_HEADS * RWKV_HS
RWKV_W_LORA = 64
RWKV_A_LORA = 64
RWKV_G_LORA = 128
D_FF = 2816

N_CTX = BATCH * SEQ
N_LAT = DEC_BATCH * DEC_SEQ
N_TOK = N_CTX + N_LAT
N_GROUPS = 1 + DEC_BATCH

LANES = 128
VMEM_LIMIT = 56 * 1024 * 1024

_PREC = lax.Precision.HIGHEST
F32 = jnp.float32


def _dot_tn(a, b):
    return lax.dot_general(a, b, (((0,), (0,)), ((), ())), precision=_PREC, preferred_element_type=F32)


BF16 = jnp.bfloat16


def _dot_bf16(a, b):
    return jnp.dot(a.astype(BF16), b.astype(BF16), preferred_element_type=F32)


def _dot_nt_bf16(a, b):
    return lax.dot_general(a.astype(BF16), b.astype(BF16), (((1,), (1,)), ((), ())), preferred_element_type=F32)


def _params(*sem):
    return pltpu.CompilerParams(dimension_semantics=sem, vmem_limit_bytes=VMEM_LIMIT)


def _sigmoid(x):
    return 1.0 / (1.0 + jnp.exp(-x))


def _silu(x):
    return x * _sigmoid(x)


def _softplus(x):
    return jnp.maximum(x, 0.0) + jnp.log(1.0 + jnp.exp(-jnp.abs(x)))


def _rms(x, n):
    return x * lax.rsqrt(jnp.sum(x * x, axis=-1, keepdims=True) * (1.0 / n) + EPS)


def _lane_lo(shape):
    return lax.broadcasted_iota(jnp.int32, shape, len(shape) - 1) < 64


def _seg64_sum(x):
    lo = _lane_lo(x.shape)
    s_lo = jnp.sum(jnp.where(lo, x, 0.0), axis=-1, keepdims=True)
    s_hi = jnp.sum(jnp.where(lo, 0.0, x), axis=-1, keepdims=True)
    return jnp.where(lo, s_lo, s_hi)


_SUBLANES = 8


def _seq_neighbours(p, tile, tile_rows):
    is_ctx = tile * tile_rows < N_CTX
    sub = lax.broadcasted_iota(jnp.int32, (_SUBLANES, 1), 0)

    def shifted(rolled, edge_sublane, group_of_seq):
        pieces, start = [], 0
        for q in range(tile_rows // SEQ):
            g0 = q * SEQ + group_of_seq
            outer = (q == 0) if group_of_seq == 0 else (q == tile_rows // SEQ - 1)
            edge = (sub == edge_sublane) if outer else ((sub == edge_sublane) & is_ctx)
            pieces += [rolled[start:g0], jnp.where(edge, 0.0, rolled[g0:g0 + _SUBLANES])]
            start = g0 + _SUBLANES
        pieces.append(rolled[start:])
        return jnp.concatenate([x for x in pieces if x.shape[0]], axis=0)

    prev = shifted(pltpu.roll(p, 1, axis=0), 0, 0)
    nxt = shifted(pltpu.roll(p, tile_rows - 1, axis=0), _SUBLANES - 1, SEQ - _SUBLANES)
    return prev, nxt


def _group_of_tile(i, tile_rows):
    row = i * tile_rows
    return jnp.where(row < N_CTX, 0, 1 + (row - N_CTX) // DEC_SEQ)


def _modulation_kernel(c_ref, w_ref, b_ref, o_ref):
    o_ref[0] = _dot_bf16(_silu(c_ref[...]), w_ref[0]) + b_ref[0]


def _modulation(cond8, ada_w, ada_b):
    tn = 512
    n = 6 * D_MODEL
    out = pl.pallas_call(
        _modulation_kernel,
        grid=(DEPTH, n // tn),
        in_specs=[pl.BlockSpec((8, D_MODEL), lambda l, j: (0, 0)),
                  pl.BlockSpec((1, D_MODEL, tn), lambda l, j: (l, 0, j)),
                  pl.BlockSpec((1, 1, tn), lambda l, j: (l, 0, j))],
        out_specs=pl.BlockSpec((1, 8, tn), lambda l, j: (l, 0, j)),
        out_shape=jax.ShapeDtypeStruct((DEPTH, 8, n), F32),
        compiler_params=_params("parallel", "parallel"),
        name="modulation",
    )(cond8, ada_w, ada_b.reshape(DEPTH, 1, n))
    m = out[:, :N_GROUPS].reshape(DEPTH, N_GROUPS, 6, D_MODEL)
    return jnp.pad(m, ((0, 0), (0, 0), (0, 2), (0, 0)))


_TM_SEQ = 1024


def _norm_mod(x, g, mod, off):
    return _rms(x, D_MODEL) * g * (1.0 + mod[off + 1:off + 2, :]) + mod[off:off + 1, :]


def _inproj_kernel(x_ref, g_ref, mod_ref, w_ref, mu_ref, o_ref, h_ref, *, shift_from):
    i = pl.program_id(0)

    @pl.when(pl.program_id(1) == 0)
    def _():
        h_ref[...] = _norm_mod(x_ref[...], g_ref[...], mod_ref[0], 0).astype(BF16)

    p = _dot_bf16(h_ref[...], w_ref[...])
    if shift_from is None:
        o_ref[...] = p
    else:
        @pl.when(pl.program_id(1) < shift_from)
        def _():
            o_ref[...] = p

        @pl.when(pl.program_id(1) >= shift_from)
        def _():
            prev, nxt = _seq_neighbours(p, i, _TM_SEQ)
            o_ref[...] = p + (0.5 * (prev + nxt) - p) * mu_ref[...]


def _inproj(x, g, mod, w, mu, tn, shift_from):
    n = w.shape[1]
    tm = _TM_SEQ
    return pl.pallas_call(
        functools.partial(_inproj_kernel, shift_from=shift_from),
        grid=(N_TOK // tm, n // tn),
        in_specs=[pl.BlockSpec((tm, D_MODEL), lambda i, j: (i, 0)),
                  pl.BlockSpec((1, D_MODEL), lambda i, j: (0, 0)),
                  pl.BlockSpec((1, 8, D_MODEL), lambda i, j: (_group_of_tile(i, tm), 0, 0)),
                  pl.BlockSpec((D_MODEL, tn), lambda i, j: (0, j)),
                  pl.BlockSpec((1, tn), lambda i, j: (0, j))],
        out_specs=pl.BlockSpec((tm, tn), lambda i, j: (i, j)),
        out_shape=jax.ShapeDtypeStruct((N_TOK, n), F32),
        scratch_shapes=[pltpu.VMEM((tm, D_MODEL), BF16)],
        compiler_params=_params("parallel", "arbitrary"),
        name="inproj" if shift_from is None else "inproj_shift",
    )(x, g, mod, w, mu)


def _resid_kernel(*refs, gate_row, widths, split, split_out):
    n_in = sum(2 if sp else 1 for sp in split)
    a_refs = refs[:n_in]
    w_ref, x_ref, mod_ref = refs[n_in:n_in + 3]
    o_ref = refs[n_in + 3:-1] if split_out else refs[n_in + 3]
    a_bf_ref = refs[-1]
    i = pl.program_id(0)

    @pl.when(pl.program_id(1) == 0)
    def _():
        k0, r = 0, 0
        for width, sp in zip(widths, split):
            cols = slice(k0, k0 + width)
            if sp:
                ctx_ref, lat_ref = a_refs[r], a_refs[r + 1]

                @pl.when(i < N_CTX // _TM_SEQ)
                def _():
                    a_bf_ref[:, cols] = ctx_ref[...].astype(BF16)

                @pl.when(i >= N_CTX // _TM_SEQ)
                def _():
                    a_bf_ref[:, cols] = lat_ref[...].astype(BF16)
            else:
                a_bf_ref[:, cols] = a_refs[r][...].astype(BF16)
            k0 += width
            r += 2 if sp else 1

    y = x_ref[...] + mod_ref[0, gate_row:gate_row + 1, :] * _dot_bf16(a_bf_ref[...], w_ref[...])
    if not split_out:
        o_ref[...] = y
    else:
        ctx_o_ref, lat_o_ref = o_ref

        @pl.when(i < N_CTX // _TM_SEQ)
        def _():
            ctx_o_ref[...] = y

        @pl.when(i >= N_CTX // _TM_SEQ)
        def _():
            lat_o_ref[...] = y


def _resid_proj(acts, w, x, mod, gate_row, split_out=False):
    tm, tn = _TM_SEQ, 256
    n_ctx = N_CTX // tm
    nj = D_MODEL // tn
    split = [isinstance(a, (tuple, list)) for a in acts]
    widths = [a[0].shape[1] if sp else a.shape[1] for a, sp in zip(acts, split)]
    k = sum(widths)
    in_specs, operands = [], []
    for a, width, sp in zip(acts, widths, split):
        if sp:
            in_specs += [pl.BlockSpec((tm, width), lambda i, j: (jnp.minimum(i, n_ctx - 1), 0)),
                         pl.BlockSpec((tm, width), lambda i, j: (jnp.maximum(i - n_ctx, 0), 0))]
            operands += list(a)
        else:
            in_specs.append(pl.BlockSpec((tm, width), lambda i, j: (i, 0)))
            operands.append(a)
    if split_out:
        out_specs = [pl.BlockSpec((tm, tn), lambda i, j: (jnp.minimum(i, n_ctx - 1), jnp.where(i < n_ctx, j, nj - 1))),
                     pl.BlockSpec((tm, tn), lambda i, j: (jnp.maximum(i - n_ctx, 0), jnp.where(i < n_ctx, 0, j)))]
        out_shape = [jax.ShapeDtypeStruct((N_CTX, D_MODEL), F32), jax.ShapeDtypeStruct((N_LAT, D_MODEL), F32)]
        sem = ("arbitrary", "arbitrary")
    else:
        out_specs = pl.BlockSpec((tm, tn), lambda i, j: (i, j))
        out_shape = jax.ShapeDtypeStruct((N_TOK, D_MODEL), F32)
        sem = ("parallel", "arbitrary")
    return pl.pallas_call(
        functools.partial(_resid_kernel, gate_row=gate_row, widths=tuple(widths), split=tuple(split),
                          split_out=split_out),
        grid=(N_TOK // tm, nj),
        in_specs=in_specs
        + [pl.BlockSpec((k, tn), lambda i, j: (0, j)),
           pl.BlockSpec((tm, tn), lambda i, j: (i, j)),
           pl.BlockSpec((1, 8, tn), lambda i, j: (_group_of_tile(i, tm), 0, j))],
        out_specs=out_specs,
        out_shape=out_shape,
        scratch_shapes=[pltpu.VMEM((tm, k), BF16)],
        compiler_params=_params(*sem),
        name="resid_proj",
    )(*operands, w, x, mod)


def _ffn_up_kernel(x_ref, g_ref, mod_ref, wa_ref, wb_ref, cwa_ref, cwb_ref, cba_ref, cbb_ref, o_ref, h_ref):
    i = pl.program_id(0)

    @pl.when(pl.program_id(1) == 0)
    def _():
        h_ref[...] = _norm_mod(x_ref[...], g_ref[...], mod_ref[0], 3).astype(BF16)

    h = h_ref[...]

    def conv(w_ref, cw_ref, cb_ref):
        u = _dot_bf16(h, w_ref[...])
        prev, nxt = _seq_neighbours(u, i, _TM_SEQ)
        return prev * cw_ref[0:1, :] + u * cw_ref[1:2, :] + nxt * cw_ref[2:3, :] + cb_ref[...]

    o_ref[...] = (_silu(conv(wa_ref, cwa_ref, cba_ref)) * conv(wb_ref, cwb_ref, cbb_ref)).astype(o_ref.dtype)


def _ffn_up(x, g, mod, up, cw, cb):
    tm, tn = _TM_SEQ, 256
    nb = D_FF // tn
    cb = cb.reshape(1, 2 * D_FF)
    return pl.pallas_call(
        _ffn_up_kernel,
        grid=(N_TOK // tm, nb),
        in_specs=[pl.BlockSpec((tm, D_MODEL), lambda i, j: (i, 0)),
                  pl.BlockSpec((1, D_MODEL), lambda i, j: (0, 0)),
                  pl.BlockSpec((1, 8, D_MODEL), lambda i, j: (_group_of_tile(i, tm), 0, 0)),
                  pl.BlockSpec((D_MODEL, tn), lambda i, j: (0, j)),
                  pl.BlockSpec((D_MODEL, tn), lambda i, j: (0, j + nb)),
                  pl.BlockSpec((3, tn), lambda i, j: (0, j)),
                  pl.BlockSpec((3, tn), lambda i, j: (0, j + nb)),
                  pl.BlockSpec((1, tn), lambda i, j: (0, j)),
                  pl.BlockSpec((1, tn), lambda i, j: (0, j + nb))],
        out_specs=pl.BlockSpec((tm, tn), lambda i, j: (i, j)),
        out_shape=jax.ShapeDtypeStruct((N_TOK, D_FF), BF16),
        scratch_shapes=[pltpu.VMEM((tm, D_MODEL), BF16)],
        compiler_params=_params("parallel", "arbitrary"),
        name="ffn_up",
    )(x, g, mod, up, up, cw, cw, cb, cb)


_ROPE_TM = 512


def _rope(y, c, s1, s2):
    return y * c + pltpu.roll(y, 1, axis=1) * s1 + pltpu.roll(y, LANES - 1, axis=1) * s2


def _rope_tables(rot_dim, lane_offsets):
    t = np.arange(DEC_SEQ)
    row, col = t // GRID_W, t % GRID_W
    n_freq = rot_dim // 4
    inv = ROPE_BASE ** (-np.arange(n_freq, dtype=np.float64) / n_freq)
    ang = np.concatenate([row[:, None] * inv, col[:, None] * inv], -1)
    cos, sin = np.cos(ang), np.sin(ang)
    n = _ROPE_TM + DEC_SEQ
    c, s1, s2 = np.ones((n, LANES)), np.zeros((n, LANES)), np.zeros((n, LANES))
    for a in lane_offsets:
        even = a + 2 * np.arange(rot_dim // 2)
        c[_ROPE_TM:, even] = cos
        c[_ROPE_TM:, even + 1] = cos
        s1[_ROPE_TM:, even + 1] = sin
        s2[_ROPE_TM:, even] = -sin
    return tuple(jnp.asarray(x, F32) for x in (c, s1, s2))


def _rope_block(i):
    row = i * _ROPE_TM
    return jnp.where(row < N_CTX, 0, 1 + ((row - N_CTX) % DEC_SEQ) // _ROPE_TM)


def _mla_q_kernel(cq_ref, qnorm_ref, w_ref, qn_ref, c_ref, s1_ref, s2_ref, o_ref):
    xn = _rms(cq_ref[...], MLA_Q_RANK) * qnorm_ref[...]
    y = _dot_bf16(xn, w_ref[...])
    c, s1, s2 = c_ref[...], s1_ref[...], s2_ref[...]
    for h in range(MLA_HEADS):
        yh = y[:, h * LANES:(h + 1) * LANES]
        yh = _rms(yh, MLA_QK) * qn_ref[...]
        o_ref[:, h * LANES:(h + 1) * LANES] = _rope(yh, c, s1, s2)


def _mla_q(p, q_norm, w_uq_p, qn_p, tabs):
    tm = _ROPE_TM
    hw = MLA_HEADS * LANES
    tab_spec = pl.BlockSpec((tm, LANES), lambda i: (_rope_block(i), 0))
    return pl.pallas_call(
        _mla_q_kernel,
        grid=(N_TOK // tm,),
        in_specs=[pl.BlockSpec((tm, MLA_Q_RANK), lambda i: (i, 0)),
                  pl.BlockSpec((1, MLA_Q_RANK), lambda i: (0, 0)),
                  pl.BlockSpec((MLA_Q_RANK, hw), lambda i: (0, 0)),
                  pl.BlockSpec((1, LANES), lambda i: (0, 0)),
                  tab_spec, tab_spec, tab_spec],
        out_specs=pl.BlockSpec((tm, hw), lambda i: (i, 0)),
        out_shape=jax.ShapeDtypeStruct((N_TOK, hw), F32),
        compiler_params=_params("parallel"),
        name="mla_q",
    )(p, q_norm, w_uq_p, qn_p, *tabs)


def _mla_kv_kernel(ckv_ref, kr_ref, kvn_ref, wk_ref, wv_ref, kn_ref, c_ref, s1_ref, s2_ref,
                   k_ref, v_ref, ckvn_ref, *, norm_ckv):
    ckv = ckv_ref[...]
    if norm_ckv:
        ckv = _rms(ckv, MLA_KV_RANK) * kvn_ref[...]
    ckvn_ref[...] = ckv
    ckv_bf = ckv.astype(BF16)
    kk = _dot_bf16(ckv_bf, wk_ref[...])
    v_ref[...] = _dot_bf16(ckv_bf, wv_ref[...])
    kr = kr_ref[...]
    c, s1, s2 = c_ref[...], s1_ref[...], s2_ref[...]
    for h in range(MLA_HEADS):
        kh = kk[:, h * LANES:(h + 1) * LANES] + kr
        kh = _rms(kh, MLA_QK) * kn_ref[...]
        k_ref[:, h * LANES:(h + 1) * LANES] = _rope(kh, c, s1, s2)


def _mla_kv(ckv_src, ckv_blk, kr_src, kr_blk, kv_norm, wk_p, wv_p, kn_p, tabs, n_rows, own_tokens):
    tm = _ROPE_TM
    hw = MLA_HEADS * LANES
    tab_spec = pl.BlockSpec((tm, LANES), (lambda i: (_rope_block(i), 0)) if own_tokens else (lambda i: (0, 0)))
    return pl.pallas_call(
        functools.partial(_mla_kv_kernel, norm_ckv=own_tokens),
        grid=(n_rows // tm,),
        in_specs=[pl.BlockSpec((tm, LANES), lambda i: (i, ckv_blk)),
                  pl.BlockSpec((tm, LANES), lambda i: (i, kr_blk)),
                  pl.BlockSpec((1, LANES), lambda i: (0, 0)),
                  pl.BlockSpec((MLA_KV_RANK, hw), lambda i: (0, 0)),
                  pl.BlockSpec((MLA_KV_RANK, hw), lambda i: (0, 0)),
                  pl.BlockSpec((1, LANES), lambda i: (0, 0)),
                  tab_spec, tab_spec, tab_spec],
        out_specs=[pl.BlockSpec((tm, hw), lambda i: (i, 0)),
                   pl.BlockSpec((tm, hw), lambda i: (i, 0)),
                   pl.BlockSpec((tm, LANES), lambda i: (i, 0))],
        out_shape=[jax.ShapeDtypeStruct((n_rows, hw), F32),
                   jax.ShapeDtypeStruct((n_rows, hw), F32),
                   jax.ShapeDtypeStruct((n_rows, LANES), F32)],
        compiler_params=_params("parallel"),
        name="mla_kv",
    )(ckv_src, kr_src, kv_norm, wk_p, wv_p, kn_p, *tabs)


_LOG2E = math.log2(math.e)


def _softmax_parts(scores, scale):
    m = functools.reduce(jnp.maximum, [jnp.max(sc, axis=-1, keepdims=True) for sc in scores])
    ps = [jnp.exp2((sc - m) * (scale * _LOG2E)) for sc in scores]
    return ps, sum(jnp.sum(p, axis=-1, keepdims=True) for p in ps)


def _attend(q, sources, sl, scale):
    ps, l = _softmax_parts([_dot_nt_bf16(q, k_ref[:, sl]) for k_ref, _ in sources], scale)
    return sum(_dot_bf16(p, v_ref[:, sl]) for p, (_, v_ref) in zip(ps, sources)), l


def _mla_attn_kernel(*refs, pairs):
    q_ref, o_ref = refs[0], refs[-1]
    sources = [refs[1:3]] + ([refs[3:5]] if len(refs) == 6 else [])
    scale = MLA_QK ** -0.5
    for pr in range(pairs):
        outs = []
        for h in range(2):
            sl = slice((2 * pr + h) * LANES, (2 * pr + h + 1) * LANES)
            o, l = _attend(q_ref[:, sl], sources, sl, scale)
            outs.append(o / l)
        o_ref[:, pr * LANES:(pr + 1) * LANES] = (outs[0] + pltpu.roll(outs[1], MLA_V, axis=1)).astype(o_ref.dtype)


def _mla_attn(q, k, v, cache, batch, n, row0, tq, pairs):
    nqb = n // tq
    qb0 = row0 // tq
    kb0 = row0 // n
    wide = 2 * LANES * pairs
    kv_spec = pl.BlockSpec((n, wide), lambda b, h, i: (kb0 + b, h))
    in_specs, operands = [pl.BlockSpec((tq, wide), lambda b, h, i: (qb0 + b * nqb + i, h)), kv_spec, kv_spec], [q, k, v]
    if cache is not None:
        in_specs += [pl.BlockSpec((PAST_LEN, wide), lambda b, h, i: (b, h))] * 2
        operands += list(cache)
    return pl.pallas_call(
        functools.partial(_mla_attn_kernel, pairs=pairs),
        grid=(batch, MLA_HEADS // (2 * pairs), nqb),
        in_specs=in_specs,
        out_specs=pl.BlockSpec((tq, LANES * pairs), lambda b, h, i: (b * nqb + i, h)),
        out_shape=jax.ShapeDtypeStruct((batch * n, MLA_HEADS * MLA_V), BF16),
        compiler_params=_params("parallel", "parallel", "arbitrary"),
        name="mla_attn",
    )(*operands)


def _ret_kernel(lg_ref, q_ref, k_ref, v_ref, rg_ref, s0_ref, gn_ref, *out_and_scratch, n, tq, want_state):
    if want_state:
        o_ref, st_ref, decay_ref = out_and_scratch
    else:
        o_ref, decay_ref = out_and_scratch
    pair, qi, b = pl.program_id(0), pl.program_id(1), pl.program_id(2)
    q = q_ref[...]
    k = k_ref[...] * (RET_DK ** -0.5)
    lo = _lane_lo((1, LANES))
    row = (qi * tq + lax.broadcasted_iota(jnp.int32, (tq, 1), 0)).astype(F32)

    @pl.when(b == 0)
    def _():
        col = lax.broadcasted_iota(jnp.int32, (1, n), 1).astype(F32)
        diff = row - col
        for h in range(2):
            lgf = lg_ref[0, 2 * pair + h]
            lgb = lg_ref[1, 2 * pair + h]
            decay_ref[h] = (jnp.where(diff >= 0, jnp.exp(lgf * jnp.maximum(diff, 0.0)), 0.0)
                            + jnp.where(diff <= 0, jnp.exp(lgb * jnp.maximum(-diff, 0.0)), 0.0))

    for h in range(2):
        lgf = lg_ref[0, 2 * pair + h]
        lgb = lg_ref[1, 2 * pair + h]
        mask = lo if h == 0 else jnp.logical_not(lo)
        qh = jnp.where(mask, q, 0.0)
        vh = v_ref[:, h * LANES:(h + 1) * LANES]
        o = _dot_bf16(_dot_nt_bf16(qh, k) * decay_ref[h], vh)
        o = o + _dot_bf16(qh * jnp.exp(lgf * (row + 1.0)), s0_ref[0, 0])
        o = o + _dot_bf16(qh * jnp.exp(lgb * (n - row)), s0_ref[0, 1])
        y = _rms(o, RET_DV) * gn_ref[:, h * LANES:(h + 1) * LANES]
        o_ref[:, h * LANES:(h + 1) * LANES] = (_silu(rg_ref[:, h * LANES:(h + 1) * LANES]) * y).astype(o_ref.dtype)

    if want_state:
        pos = lax.broadcasted_iota(jnp.int32, (n, 1), 0).astype(F32)
        for d in range(2):
            acc = None
            for h in range(2):
                lg = lg_ref[d, 2 * pair + h]
                mask = lo if h == 0 else jnp.logical_not(lo)
                expo = (n - 1.0 - pos) if d == 0 else pos
                kd = jnp.where(mask, k * jnp.exp(lg * expo), 0.0)
                term = _dot_tn(kd, v_ref[:, h * LANES:(h + 1) * LANES])
                acc = term if acc is None else acc + term
            lg_rows = jnp.where(lax.broadcasted_iota(jnp.int32, (LANES, 1), 0) < 64,
                                lg_ref[d, 2 * pair], lg_ref[d, 2 * pair + 1])
            st_ref[0, d] = acc + s0_ref[0, d] * jnp.exp(lg_rows * n)


def _retention(log_g, p, q_blk, k_blk, v_blk, g_blk, s0, gn, batch, n, row0, tq, want_state):
    nqb = n // tq
    assert not want_state or nqb == 1
    qb0 = row0 // tq
    kb0 = row0 // n
    pairs = RET_HEADS // 2
    out_specs = [pl.BlockSpec((tq, 2 * LANES), lambda h, i, b: (b * nqb + i, h))]
    out_shape = [jax.ShapeDtypeStruct((batch * n, RET_HEADS * RET_DV), BF16)]
    if want_state:
        out_specs.append(pl.BlockSpec((1, 2, LANES, LANES), lambda h, i, b: (b, 0, h, 0)))
        out_shape.append(jax.ShapeDtypeStruct((batch, 2, RET_HEADS * RET_DK, RET_DV), F32))
    outs = pl.pallas_call(
        functools.partial(_ret_kernel, n=n, tq=tq, want_state=want_state),
        grid=(pairs, nqb, batch),
        in_specs=[pl.BlockSpec(memory_space=pltpu.SMEM),
                  pl.BlockSpec((tq, LANES), lambda h, i, b: (qb0 + b * nqb + i, q_blk + h)),
                  pl.BlockSpec((n, LANES), lambda h, i, b: (kb0 + b, k_blk + h)),
                  pl.BlockSpec((n, 2 * LANES), lambda h, i, b: (kb0 + b, v_blk // 2 + h)),
                  pl.BlockSpec((tq, 2 * LANES), lambda h, i, b: (qb0 + b * nqb + i, g_blk // 2 + h)),
                  pl.BlockSpec((1, 2, LANES, LANES), lambda h, i, b: (b, 0, h, 0)),
                  pl.BlockSpec((1, 2 * LANES), lambda h, i, b: (0, h))],
        out_specs=out_specs,
        out_shape=out_shape,
        scratch_shapes=[pltpu.VMEM((2, tq, n), F32)],
        compiler_params=_params("parallel", "parallel", "arbitrary"),
        name="retention",
    )(log_g, p, p, p, p, s0, gn)
    return outs if want_state else (outs[0], None)


def _diff_qk_kernel(q_ref, k_ref, v_ref, qn_ref, kn_ref, c_ref, s1_ref, s2_ref, qo_ref, ko_ref, dk_ref, dv_ref):
    c, s1, s2 = c_ref[...], s1_ref[...], s2_ref[...]
    is_ctx = pl.program_id(0) < N_CTX // _ROPE_TM
    for src, gain, dst in ((q_ref, qn_ref, qo_ref), (k_ref, kn_ref, ko_ref)):
        for h in range(DIFF_HEADS):
            sl = slice(h * LANES, (h + 1) * LANES)
            y = src[:, sl]
            y = _rope(y * lax.rsqrt(_seg64_sum(y * y) * (1.0 / DIFF_DH) + EPS) * gain[...], c, s1, s2)
            dst[:, sl] = y
            if dst is ko_ref:
                @pl.when(is_ctx)
                def _():
                    dk_ref[:, h, 0, :] = y[:, :DIFF_DH]
                    dk_ref[:, h, 1, :] = pltpu.roll(y, DIFF_DH, axis=1)[:, :DIFF_DH]
                    dv_ref[:, h, :] = v_ref[:, sl]


def _diff_qk(p, qn_p, kn_p, tabs):
    tm = _ROPE_TM
    n_ctx = N_CTX // tm
    tab_spec = pl.BlockSpec((tm, LANES), lambda i: (_rope_block(i), 0))
    ctx_blk = lambda i: jnp.minimum(i, n_ctx - 1)
    return pl.pallas_call(
        _diff_qk_kernel,
        grid=(N_TOK // tm,),
        in_specs=[pl.BlockSpec((tm, DIFF_W), lambda i: (i, 0)),
                  pl.BlockSpec((tm, DIFF_W), lambda i: (i, 1)),
                  pl.BlockSpec((tm, DIFF_W), lambda i: (i, 2)),
                  pl.BlockSpec((1, LANES), lambda i: (0, 0)),
                  pl.BlockSpec((1, LANES), lambda i: (0, 0)),
                  tab_spec, tab_spec, tab_spec],
        out_specs=[pl.BlockSpec((tm, DIFF_W), lambda i: (i, 0)),
                   pl.BlockSpec((tm, DIFF_W), lambda i: (i, 0)),
                   pl.BlockSpec((tm, DIFF_HEADS, 2, DIFF_DH), lambda i: (ctx_blk(i), 0, 0, 0)),
                   pl.BlockSpec((tm, DIFF_HEADS, 2 * DIFF_DH), lambda i: (ctx_blk(i), 0, 0))],
        out_shape=[jax.ShapeDtypeStruct((N_TOK, DIFF_W), F32)] * 2
        + [jax.ShapeDtypeStruct((N_CTX, DIFF_HEADS, 2, DIFF_DH), F32),
           jax.ShapeDtypeStruct((N_CTX, DIFF_HEADS, 2 * DIFF_DH), F32)],
        compiler_params=_params("arbitrary"),
        name="diff_qk",
    )(p, p, p, qn_p, kn_p, *tabs)


def _diff_attn_kernel(*refs, lam_init, heads):
    lam_ref, q_ref, gn_ref, o_ref = refs[0], refs[1], refs[-2], refs[-1]
    sources = [refs[2:4]] + ([refs[4:6]] if len(refs) == 8 else [])
    lv = lam_ref[...]
    lam = (jnp.exp(jnp.sum(lv[0:1] * lv[1:2], axis=-1, keepdims=True))
           - jnp.exp(jnp.sum(lv[2:3] * lv[3:4], axis=-1, keepdims=True)) + lam_init)
    scale = DIFF_DH ** -0.5
    lo = _lane_lo((1, LANES))
    for h in range(heads):
        sl = slice(h * LANES, (h + 1) * LANES)
        q = q_ref[:, sl]
        kbs = [k_ref[:, sl].astype(BF16) for k_ref, _ in sources]
        ps1, l1 = _softmax_parts([_dot_nt_bf16(jnp.where(lo, q, 0.0), kb) for kb in kbs], scale)
        ps2, l2 = _softmax_parts([_dot_nt_bf16(jnp.where(lo, 0.0, q), kb) for kb in kbs], scale)
        o = sum(_dot_bf16(p1 / l1 - lam * (p2 / l2), v_ref[:, sl])
                for p1, p2, (_, v_ref) in zip(ps1, ps2, sources))
        o_ref[:, sl] = (_rms(o, 2 * DIFF_DH) * gn_ref[:, sl] * (1.0 - lam_init)).astype(o_ref.dtype)


def _diff_attn(lam, q, k, v, v_blk0, cache, gn, batch, n, row0, tq, lam_init, heads):
    nqb = n // tq
    qb0 = row0 // tq
    kb0 = row0 // n
    wide = LANES * heads
    vb0 = v_blk0 // heads
    in_specs = [pl.BlockSpec((4, DIFF_DH), lambda b, h, i: (0, 0)),
                pl.BlockSpec((tq, wide), lambda b, h, i: (qb0 + b * nqb + i, h)),
                pl.BlockSpec((n, wide), lambda b, h, i: (kb0 + b, h)),
                pl.BlockSpec((n, wide), lambda b, h, i: (kb0 + b, vb0 + h))]
    operands = [lam, q, k, v]
    if cache is not None:
        in_specs += [pl.BlockSpec((PAST_LEN, wide), lambda b, h, i: (b, h))] * 2
        operands += list(cache)
    return pl.pallas_call(
        functools.partial(_diff_attn_kernel, lam_init=lam_init, heads=heads),
        grid=(batch, DIFF_HEADS // heads, nqb),
        in_specs=in_specs + [pl.BlockSpec((1, wide), lambda b, h, i: (0, h))],
        out_specs=pl.BlockSpec((tq, wide), lambda b, h, i: (b * nqb + i, h)),
        out_shape=jax.ShapeDtypeStruct((batch * n, DIFF_W), BF16),
        compiler_params=_params("parallel", "parallel", "arbitrary"),
        name="diff_attn",
    )(*operands, gn)


def _seg64_sum_wide(x):
    return jnp.concatenate([_seg64_sum(x[:, j * LANES:(j + 1) * LANES]) for j in range(x.shape[1] // LANES)], axis=1)


_SCAN_SLOTS = 3
_SCAN_NJ = _SCAN_SLOTS * RWKV_HEADS


def _rwkv_pre_kernel(r_ref, k_ref, v_ref, lo_ref, wup_ref, aup_ref, gup_ref, w0_ref, a0_ref, kk_ref, ka_ref, rk_ref,
                     op_ref, g_ref, bonus_ref):
    W = RWKV_W
    col = lambda q: slice(q * W, (q + 1) * W)
    r = r_ref[...]
    k = k_ref[...]
    v = v_ref[...]
    lora = lo_ref[...]
    kk = k * kk_ref[...]
    kkn = kk * lax.rsqrt(_seg64_sum_wide(kk * kk) + EPS)
    g_ref[...] = _dot_bf16(_sigmoid(lora[:, 2 * LANES:3 * LANES]), gup_ref[...])
    pre = w0_ref[...] + _dot_bf16(jnp.tanh(lora[:, 0:LANES]), wup_ref[...])
    decay = jnp.exp(-jnp.exp(-_softplus(-pre) - 0.5))
    a = _sigmoid(a0_ref[...] + _dot_bf16(lora[:, LANES:2 * LANES], aup_ref[...]))
    lo = _lane_lo((1, LANES))
    bonus = None
    for d in range(2):
        a_d = a[:, col(d)]
        k_d = k * (1.0 + (a_d - 1.0) * ka_ref[...])
        t = _seg64_sum_wide(r * k_d * rk_ref[...])
        bonus = t if bonus is None else bonus + t
        for s, (x1, x2) in enumerate(((kkn, decay[:, col(d)]), (k_d, kkn * a_d), (r, v))):
            for h in range(RWKV_HEADS):
                blk = slice((h // 2) * LANES, (h // 2 + 1) * LANES)
                if h % 2 == 0:
                    out = jnp.where(lo, x1[:, blk], pltpu.roll(x2[:, blk], RWKV_HS, axis=1))
                else:
                    out = jnp.where(lo, pltpu.roll(x1[:, blk], RWKV_HS, axis=1), x2[:, blk])
                op_ref[:, d * _SCAN_NJ + s * RWKV_HEADS + h, :] = out
    bonus_ref[...] = bonus


def _rwkv_pre(p, r_blk, k_blk, v_blk, lo_blk, wup_bd, aup_bd, gup, w0, a0, k_k, k_a, r_k):
    tm = 256
    w = RWKV_W
    row = lambda n: pl.BlockSpec((1, n), lambda i: (0, 0))
    full = lambda a, b: pl.BlockSpec((a, b), lambda i: (0, 0))
    return pl.pallas_call(
        _rwkv_pre_kernel,
        grid=(N_TOK // tm,),
        in_specs=[pl.BlockSpec((tm, w), lambda i: (i, r_blk)),
                  pl.BlockSpec((tm, w), lambda i: (i, k_blk)),
                  pl.BlockSpec((tm, w), lambda i: (i, v_blk)),
                  pl.BlockSpec((tm, 3 * LANES), lambda i: (i, lo_blk)),
                  full(LANES, 2 * w), full(LANES, 2 * w), full(LANES, w),
                  row(2 * w), row(2 * w), row(w), row(w), row(w)],
        out_specs=[pl.BlockSpec((tm, 2 * _SCAN_NJ, LANES), lambda i: (i, 0, 0)),
                   pl.BlockSpec((tm, w), lambda i: (i, 0)), pl.BlockSpec((tm, w), lambda i: (i, 0))],
        out_shape=[jax.ShapeDtypeStruct((N_TOK, 2 * _SCAN_NJ, LANES), F32),
                   jax.ShapeDtypeStruct((N_TOK, w), F32), jax.ShapeDtypeStruct((N_TOK, w), F32)],
        compiler_params=_params("parallel"),
        name="rwkv_pre",
    )(p, p, p, p, wup_bd, aup_bd, gup, w0, a0, k_k, k_a, r_k)


_SCAN_CHUNK = 32
_SCAN_UNROLL = 16


def _rwkv_first_sa(s_ref, sa_ref, kk, n_k):
    nv = s_ref.shape[1]
    chunk = min(_SCAN_CHUNK, nv)
    for c0 in range(0, nv, chunk):
        def body(k, acc):
            return acc + s_ref[k, c0:c0 + chunk, :] * kk(k)
        sa_ref[c0:c0 + chunk, :] = lax.fori_loop(0, n_k, body, jnp.zeros((chunk, LANES), F32), unroll=_SCAN_UNROLL)


def _rwkv_step(s_ref, sa_ref, kk_next, w, kd, b, r, v_at, n_k):
    nv = s_ref.shape[1]
    chunk = min(_SCAN_CHUNK, nv)
    ys = []
    for c0 in range(0, nv, chunk):
        sa = sa_ref[c0:c0 + chunk, :]
        vc = v_at(c0, chunk)

        def body(k, acc):
            y_acc, sa_acc = acc
            s_new = s_ref[k, c0:c0 + chunk, :] * w(k) - sa * b(k) + vc * kd(k)
            s_ref[k, c0:c0 + chunk, :] = s_new
            return y_acc + s_new * r(k), sa_acc + s_new * kk_next(k)

        zero = jnp.zeros((chunk, LANES), F32)
        y_acc, sa_acc = lax.fori_loop(0, n_k, body, (zero, zero), unroll=_SCAN_UNROLL)
        sa_ref[c0:c0 + chunk, :] = sa_acc
        ys.append(y_acc)
    return ys[0] if len(ys) == 1 else jnp.concatenate(ys, axis=0)


_CTX_TB = 32


def _load_scan_operands(kt_ref, t, slabs):
    for s, x in enumerate(slabs):
        xt = x.T
        kt_ref[t, 2 * s] = xt[:RWKV_HS]
        kt_ref[t, 2 * s + 1] = xt[RWKV_HS:]


def _rwkv_scan_ctx_kernel(x_ref, y_ref, st_ref, s_ref, sa_ref, kt_ref, ys_ref):
    d = pl.program_id(0)
    tb = pl.program_id(1)
    nh = RWKV_HEADS
    step_t = lambda i: jnp.where(d == 0, i, _CTX_TB - 1 - i)

    @pl.when(tb == 0)
    def _():
        s_ref[...] = jnp.zeros_like(s_ref)

    def load_t(t, carry):
        _load_scan_operands(kt_ref, t, [
            jnp.concatenate([x_ref[b, t, s * nh:(s + 1) * nh, :] for b in range(BATCH)], axis=0)
            for s in range(_SCAN_SLOTS)])
        return carry

    lax.fori_loop(0, _CTX_TB, load_t, 0, unroll=4)

    t0 = step_t(0)
    _rwkv_first_sa(s_ref, sa_ref, lambda k: kt_ref[t0, 0, pl.ds(k, 1), :], RWKV_HS)

    def step(i, carry):
        t = step_t(i)
        tn = step_t(jnp.minimum(i + 1, _CTX_TB - 1))
        row = lambda q, tt: (lambda k: kt_ref[tt, q, pl.ds(k, 1), :])
        ys_ref[t] = _rwkv_step(s_ref, sa_ref, row(0, tn), row(1, t), row(2, t), row(3, t), row(4, t),
                               lambda c0, n: kt_ref[t, 5, pl.ds(c0, n), :], RWKV_HS)
        return carry

    lax.fori_loop(0, _CTX_TB, step, 0)

    def store_t(i, carry):
        z = jnp.concatenate([ys_ref[2 * i], ys_ref[2 * i + 1]], axis=0).T
        z_odd = pltpu.roll(z, RWKV_HS, axis=1)
        for b in range(BATCH):
            y_ref[b, 2 * i] = z[b * nh:(b + 1) * nh, :RWKV_HS]
            y_ref[b, 2 * i + 1] = z_odd[b * nh:(b + 1) * nh, :RWKV_HS]
        return carry

    lax.fori_loop(0, _CTX_TB // 2, store_t, 0, unroll=4)

    @pl.when(tb == pl.num_programs(1) - 1)
    def _():
        st_ref[0] = s_ref[...]


def _rwkv_scan_ctx(op4):
    nt = SEQ // _CTX_TB
    hs = RWKV_HS
    tblk = lambda d, tb: jnp.where(d == 0, tb, nt - 1 - tb)
    return pl.pallas_call(
        _rwkv_scan_ctx_kernel,
        grid=(2, nt),
        in_specs=[pl.BlockSpec((BATCH, _CTX_TB, _SCAN_NJ, LANES), lambda d, tb: (0, tblk(d, tb), d, 0))],
        out_specs=[pl.BlockSpec((BATCH, _CTX_TB, RWKV_HEADS, hs), lambda d, tb: (0, tblk(d, tb), d, 0)),
                   pl.BlockSpec((1, hs, hs, LANES), lambda d, tb: (d, 0, 0, 0))],
        out_shape=[jax.ShapeDtypeStruct((BATCH, SEQ, 2 * RWKV_HEADS, hs), F32),
                   jax.ShapeDtypeStruct((2, hs, hs, LANES), F32)],
        scratch_shapes=[pltpu.VMEM((hs, hs, LANES), F32), pltpu.VMEM((hs, LANES), F32),
                        pltpu.VMEM((_CTX_TB, 2 * _SCAN_SLOTS, hs, LANES), F32),
                        pltpu.VMEM((_CTX_TB, hs, LANES), F32)],
        compiler_params=_params("parallel", "arbitrary"),
        name="rwkv_scan_ctx",
    )(op4)


_LAT_TB = 32
_LAT_VSPLIT = 4
_LAT_STATES = 2 * DEC_BATCH * RWKV_HEADS
_LAT_VROWS = RWKV_HS // _LAT_VSPLIT


def _rwkv_scan_lat_kernel(xf0_ref, xf1_ref, xb0_ref, xb1_ref, s0_ref, yf_ref, yb_ref,
                          s_ref, sa_ref, kt_ref, v_ref, ys_ref):
    @pl.when(pl.program_id(0) == 0)
    def _():
        s_ref[...] = s0_ref[...]

    group = lax.broadcasted_iota(jnp.int32, (_LAT_VROWS, LANES), 1) // _LAT_STATES
    nh = RWKV_HEADS

    def load_t(t, carry):
        tr = _LAT_TB - 1 - t
        for s in range(_SCAN_SLOTS):
            heads = slice(s * nh, (s + 1) * nh)
            x = jnp.concatenate([xf0_ref[0, t, heads, :], xf1_ref[0, t, heads, :],
                                 xb0_ref[0, tr, heads, :], xb1_ref[0, tr, heads, :]], axis=0)
            xt = jnp.concatenate([x] * _LAT_VSPLIT, axis=0).T
            kt_ref[t, 2 * s] = xt[:RWKV_HS]
            if s < _SCAN_SLOTS - 1:
                kt_ref[t, 2 * s + 1] = xt[RWKV_HS:]
            else:
                v = jnp.zeros((_LAT_VROWS, LANES), F32)
                for g in range(_LAT_VSPLIT):
                    r0 = RWKV_HS + g * _LAT_VROWS
                    v = jnp.where(group == g, xt[r0:r0 + _LAT_VROWS, :], v)
                v_ref[t] = v
        return carry

    lax.fori_loop(0, _LAT_TB, load_t, 0, unroll=8)

    _rwkv_first_sa(s_ref, sa_ref, lambda k: kt_ref[0, 0, pl.ds(k, 1), :], RWKV_HS)

    def step(t, carry):
        tn = jnp.minimum(t + 1, _LAT_TB - 1)
        row = lambda q, tt: (lambda k: kt_ref[tt, q, pl.ds(k, 1), :])
        ys_ref[t] = _rwkv_step(s_ref, sa_ref, row(0, tn), row(1, t), row(2, t), row(3, t), row(4, t),
                               lambda c0, n: v_ref[t, pl.ds(c0, n), :], RWKV_HS)
        return carry

    lax.fori_loop(0, _LAT_TB, step, 0)

    def store_t(i, carry):
        rows = [jnp.where(group == g, ys_ref[2 * i + j], 0.0) for j in range(2) for g in range(_LAT_VSPLIT)]
        z = jnp.concatenate(rows, axis=0).T
        y = (z[0:_LAT_STATES] + z[_LAT_STATES:2 * _LAT_STATES]
             + z[2 * _LAT_STATES:3 * _LAT_STATES] + z[3 * _LAT_STATES:4 * _LAT_STATES])
        y_odd = pltpu.roll(y, RWKV_HS, axis=1)
        half = _LAT_STATES // 2
        for b in range(DEC_BATCH):
            rows_f = slice(b * nh, (b + 1) * nh)
            rows_b = slice(half + b * nh, half + (b + 1) * nh)
            yf_ref[b, 2 * i] = y[rows_f, :RWKV_HS]
            yf_ref[b, 2 * i + 1] = y_odd[rows_f, :RWKV_HS]
            yb_ref[b, _LAT_TB - 1 - 2 * i] = y[rows_b, :RWKV_HS]
            yb_ref[b, _LAT_TB - 2 - 2 * i] = y_odd[rows_b, :RWKV_HS]
        return carry

    lax.fori_loop(0, _LAT_TB // 2, store_t, 0, unroll=4)


def _rwkv_scan_lat(op4, s0):
    hs = RWKV_HS
    nv = _LAT_VROWS
    nt = DEC_SEQ // _LAT_TB
    per_seq = SEQ // _LAT_TB
    first = N_CTX // SEQ

    def x_spec(b, d):
        tblk = (lambda tb: tb) if d == 0 else (lambda tb: nt - 1 - tb)
        return pl.BlockSpec((1, _LAT_TB, _SCAN_NJ, LANES),
                            lambda tb: (first + b * (DEC_SEQ // SEQ) + tblk(tb) // per_seq, tblk(tb) % per_seq, d, 0))

    y_shape = jax.ShapeDtypeStruct((DEC_BATCH, DEC_SEQ, RWKV_HEADS, hs), F32)
    y_blk = (DEC_BATCH, _LAT_TB, RWKV_HEADS, hs)
    return pl.pallas_call(
        _rwkv_scan_lat_kernel,
        grid=(nt,),
        in_specs=[x_spec(0, 0), x_spec(1, 0), x_spec(0, 1), x_spec(1, 1),
                  pl.BlockSpec((hs, nv, LANES), lambda tb: (0, 0, 0))],
        out_specs=[pl.BlockSpec(y_blk, lambda tb: (0, tb, 0, 0)),
                   pl.BlockSpec(y_blk, lambda tb: (0, nt - 1 - tb, 0, 0))],
        out_shape=[y_shape, y_shape],
        scratch_shapes=[pltpu.VMEM((hs, nv, LANES), F32), pltpu.VMEM((nv, LANES), F32),
                        pltpu.VMEM((_LAT_TB, 2 * _SCAN_SLOTS - 1, hs, LANES), F32),
                        pltpu.VMEM((_LAT_TB, nv, LANES), F32), pltpu.VMEM((_LAT_TB, nv, LANES), F32)],
        compiler_params=_params("arbitrary"),
        name="rwkv_scan_lat",
    )(op4, op4, op4, op4, s0)


def _rwkv_post_kernel(yc_ref, ylf_ref, ylb_ref, bonus_ref, v_ref, g_ref, gn_ref, o_ref):
    def finish(head_sum):
        y = jnp.concatenate([head_sum(h) for h in range(RWKV_HEADS)], axis=1)
        y = y * lax.rsqrt(_seg64_sum_wide(y * y) * (1.0 / RWKV_HS) + EPS) * gn_ref[...]
        o_ref[...] = ((y + bonus_ref[...] * v_ref[...]) * g_ref[...]).astype(o_ref.dtype)

    @pl.when(pl.program_id(0) < N_CTX // SEQ)
    def _():
        finish(lambda h: yc_ref[0, :, h, :] + yc_ref[0, :, RWKV_HEADS + h, :])

    @pl.when(pl.program_id(0) >= N_CTX // SEQ)
    def _():
        finish(lambda h: ylf_ref[0, :, h, :] + ylb_ref[0, :, h, :])


def _rwkv_post(y_ctx, y_lat_f, y_lat_b, bonus, p, v_blk, g, gn):
    tm = SEQ
    w = RWKV_W
    hs = RWKV_HS
    n_ctx = N_CTX // SEQ
    lat = lambda y: y.reshape(N_LAT // SEQ, SEQ, RWKV_HEADS, hs)
    spec = pl.BlockSpec((tm, w), lambda i: (i, 0))
    lat_spec = pl.BlockSpec((1, SEQ, RWKV_HEADS, hs), lambda i: (jnp.maximum(i - n_ctx, 0), 0, 0, 0))
    return pl.pallas_call(
        _rwkv_post_kernel,
        grid=(N_TOK // tm,),
        in_specs=[pl.BlockSpec((1, SEQ, 2 * RWKV_HEADS, hs), lambda i: (jnp.minimum(i, n_ctx - 1), 0, 0, 0)),
                  lat_spec, lat_spec, spec, pl.BlockSpec((tm, w), lambda i: (i, v_blk)), spec,
                  pl.BlockSpec((1, w), lambda i: (0, 0))],
        out_specs=spec,
        out_shape=jax.ShapeDtypeStruct((N_TOK, w), BF16),
        compiler_params=_params("parallel"),
        name="rwkv_post",
    )(y_ctx, lat(y_lat_f), lat(y_lat_b), bonus, p, g, gn)


def _value_split_layout(x):
    lead = x.shape[:-2]
    n = len(lead)
    x = x.reshape(lead + (_LAT_STATES, _LAT_VSPLIT, _LAT_VROWS))
    return jnp.transpose(x, tuple(range(n)) + (n + 2, n + 1, n)).reshape(lead + (_LAT_VROWS, LANES))


_EVEN_HEAD = MLA_Q_RANK + MLA_KV_RANK
_EVEN_PACKED = _EVEN_HEAD + LANES + 2 * RET_HEADS * (RET_DK + RET_DV)


def _pack_even_weight_kernel(w_ref, o_ref):
    w = w_ref[...]
    o_ref[:, 0:_EVEN_HEAD] = w[:, 0:_EVEN_HEAD]
    o_ref[:, _EVEN_HEAD:_EVEN_HEAD + LANES] = jnp.zeros((w.shape[0], LANES), F32)
    o_ref[:, _EVEN_HEAD + MLA_NOPE:_EVEN_HEAD + MLA_QK] = w[:, _EVEN_HEAD:_EVEN_HEAD + MLA_ROPE]
    o_ref[:, _EVEN_HEAD + LANES:] = w[:, _EVEN_HEAD + MLA_ROPE:]


def _pack_even_weight(w_in):
    tm = 256
    return pl.pallas_call(
        _pack_even_weight_kernel,
        grid=(D_MODEL // tm,),
        in_specs=[pl.BlockSpec((tm, w_in.shape[1]), lambda i: (i, 0))],
        out_specs=pl.BlockSpec((tm, _EVEN_PACKED), lambda i: (i, 0)),
        out_shape=jax.ShapeDtypeStruct((D_MODEL, _EVEN_PACKED), F32),
        compiler_params=_params("parallel"),
        name="pack_even_weight",
    )(w_in)


def _even_layer(x, mod, g_mix, w_in, q_norm, kv_norm, w_uq, w_ukv, qn, kn, ret_decay, ret_gn,
                cache_ckv, cache_krope, state_ret, tabs_m):
    w_p = _pack_even_weight(w_in)
    p = _inproj(x, g_mix, mod, w_p, jnp.zeros((1, w_p.shape[1]), F32), 512, None)
    CKV_BLK, KR_BLK, RQ_BLK, RK_BLK, RV_BLK, RG_BLK = 2, 3, 4, 6, 8, 12

    def head_pad(w, n_head, d_head, c0, c1):
        w = w.reshape(w.shape[0], n_head, d_head)[:, :, c0:c1]
        return jnp.pad(w, ((0, 0), (0, 0), (0, LANES - (c1 - c0)))).reshape(w.shape[0], n_head * LANES)

    w_uq_p = head_pad(w_uq, MLA_HEADS, MLA_QK, 0, MLA_QK)
    wk_p = head_pad(w_ukv, MLA_HEADS, MLA_NOPE + MLA_V, 0, MLA_NOPE)
    wv_p = head_pad(w_ukv, MLA_HEADS, MLA_NOPE + MLA_V, MLA_NOPE, MLA_NOPE + MLA_V)
    qn_p = jnp.pad(qn, (0, LANES - MLA_QK))[None]
    kn_p = jnp.pad(kn, (0, LANES - MLA_QK))[None]

    q = _mla_q(p, q_norm[None], w_uq_p, qn_p, tabs_m)
    k, v, ckvn = _mla_kv(p, CKV_BLK, p, KR_BLK, kv_norm[None], wk_p, wv_p, kn_p, tabs_m, N_TOK, True)

    n_c = DEC_BATCH * PAST_LEN
    kr_c = jnp.pad(cache_krope.reshape(n_c, MLA_ROPE), ((0, 0), (MLA_NOPE, LANES - MLA_QK)))
    k_c, v_c, _ = _mla_kv(cache_ckv.reshape(n_c, MLA_KV_RANK), 0, kr_c, 0, kv_norm[None], wk_p, wv_p, kn_p,
                          tabs_m, n_c, False)

    o_ctx = _mla_attn(q, k, v, None, BATCH, SEQ, 0, SEQ, MLA_HEADS // 2)
    o_lat = _mla_attn(q, k, v, (k_c, v_c), DEC_BATCH, DEC_SEQ, N_CTX, 256, 1)

    log_g = -_softplus(-ret_decay)
    gn = ret_gn[None]
    s0_ctx = jnp.zeros((BATCH, 2, RET_HEADS * RET_DK, RET_DV), F32)
    r_ctx, st_ctx = _retention(log_g, p, RQ_BLK, RK_BLK, RV_BLK, RG_BLK, s0_ctx, gn, BATCH, SEQ, 0, SEQ, True)
    s0_lat = state_ret.reshape(DEC_BATCH, 2, RET_HEADS * RET_DK, RET_DV)
    r_lat, _ = _retention(log_g, p, RQ_BLK, RK_BLK, RV_BLK, RG_BLK, s0_lat, gn, DEC_BATCH, DEC_SEQ, N_CTX, 256,
                          False)

    mix = [(o_ctx, o_lat), (r_ctx, r_lat)]
    new_ckv = ckvn[:N_CTX].reshape(BATCH, SEQ, MLA_KV_RANK)
    new_krope = p[:N_CTX, KR_BLK * LANES + MLA_NOPE:KR_BLK * LANES + MLA_QK]
    new_krope = new_krope.reshape(BATCH, SEQ, MLA_ROPE)
    new_ret = st_ctx.reshape(BATCH, 2, RET_HEADS, RET_DK, RET_DV)
    return mix, new_ckv, new_krope, new_ret


def _odd_layer(x, mod, g_mix, w_in, qn, kn, lam, diff_gn, mu, w0, w_up, a0, a_up, g_up, k_k, k_a, r_k, gn,
               cache_k, cache_v, state_rwkv, tabs_d, lam_init):
    w_p = w_in
    n_in = w_p.shape[1]
    mu_full = jnp.concatenate([jnp.zeros((3 * DIFF_W,), F32), mu])[None]
    p = _inproj(x, g_mix, mod, w_p, mu_full, 384, (3 * DIFF_W) // 384)
    DV_BLK, R_BLK, K_BLK, V_BLK = 2, 3, 4, 5
    LO_BLK = (6 * RWKV_W) // (3 * LANES)

    qn_p = jnp.tile(qn, 2)[None]
    kn_p = jnp.tile(kn, 2)[None]
    q, k, dk, dv = _diff_qk(p, qn_p, kn_p, tabs_d)

    n_c = DEC_BATCH * PAST_LEN
    cache = (cache_k.reshape(n_c, DIFF_W), cache_v.reshape(n_c, DIFF_W))
    dgn = diff_gn[None]
    o_ctx = _diff_attn(lam, q, k, p, DV_BLK * DIFF_HEADS, None, dgn, BATCH, SEQ, 0, SEQ, lam_init, DIFF_HEADS)
    o_lat = _diff_attn(lam, q, k, p, DV_BLK * DIFF_HEADS, cache, dgn, DEC_BATCH, DEC_SEQ, N_CTX, 256, lam_init, 1)

    zero = jnp.zeros((RWKV_W_LORA, RWKV_W), F32)
    wup_bd = jnp.concatenate([jnp.concatenate([w_up[0], zero], 1), jnp.concatenate([zero, w_up[1]], 1)], 0)
    aup_bd = jnp.concatenate([jnp.concatenate([a_up[0], zero], 1), jnp.concatenate([zero, a_up[1]], 1)], 0)
    op, g, bonus = _rwkv_pre(p, R_BLK, K_BLK, V_BLK, LO_BLK, wup_bd, aup_bd, g_up, w0.reshape(1, -1),
                             a0.reshape(1, -1), k_k[None], k_a[None], r_k.reshape(1, -1))
    op4 = op.reshape(N_TOK // SEQ, SEQ, 2 * _SCAN_NJ, LANES)
    y_ctx, st_ctx = _rwkv_scan_ctx(op4)
    s0_lat = jnp.transpose(state_rwkv, (4, 1, 0, 2, 3)).reshape(RWKV_HS, _LAT_STATES, RWKV_HS)
    y_lat_f, y_lat_b = _rwkv_scan_lat(op4, _value_split_layout(s0_lat))
    rw_o = _rwkv_post(y_ctx, y_lat_f, y_lat_b, bonus, p, V_BLK, g, gn[None])

    mix = [(o_ctx, o_lat), rw_o]
    new_dk = dk.reshape(BATCH, SEQ, DIFF_HEADS, 2, DIFF_DH)
    new_dv = dv.reshape(BATCH, SEQ, DIFF_HEADS, 2 * DIFF_DH)
    new_rwkv = jnp.transpose(st_ctx.reshape(2, RWKV_HS, RWKV_HS, BATCH, RWKV_HEADS), (3, 0, 4, 2, 1))
    return mix, new_dk, new_dv, new_rwkv


def kernel(x_prompt, x_sample, cache_mla_ckv, cache_mla_krope, state_ret, cache_diff_k, cache_diff_v, state_rwkv,
           c, c_ctx, ada_w, ada_b, norm_mix_g, norm_ffn_g, w_out, ffn_up, ffn_conv_w, ffn_conv_b, ffn_down,
           a_w_in, mla_q_norm, mla_kv_norm, mla_w_uq, mla_w_ukv, mla_qn, mla_kn, ret_decay, ret_gn,
           b_w_in, diff_qn, diff_kn, diff_lam, diff_gn, rwkv_mu, rwkv_w0, rwkv_w_up, rwkv_a0, rwkv_a_up,
           rwkv_g_up, rwkv_k_k, rwkv_k_a, rwkv_r_k, rwkv_gn):
    x = jnp.concatenate([x_prompt.reshape(N_CTX, D_MODEL), x_sample.reshape(N_LAT, D_MODEL)], 0)
    cond8 = jnp.pad(jnp.concatenate([c_ctx[None], c], 0), ((0, 8 - N_GROUPS), (0, 0)))
    mod = _modulation(cond8, ada_w, ada_b)

    tabs_m = _rope_tables(MLA_ROPE, (MLA_NOPE,))
    tabs_d = _rope_tables(DIFF_DH, (0, DIFF_DH))

    outs = {}
    for l in range(DEPTH):
        j = l // 2
        g_mix = norm_mix_g[l][None]
        if l % 2 == 0:
            mix, outs["ckv"], outs["krope"], outs["ret"] = _even_layer(
                x, mod[l], g_mix, a_w_in[j], mla_q_norm[j], mla_kv_norm[j], mla_w_uq[j], mla_w_ukv[j], mla_qn[j],
                mla_kn[j], ret_decay[j], ret_gn[j], cache_mla_ckv[:, j], cache_mla_krope[:, j], state_ret[:, j],
                tabs_m)
        else:
            lam_init = 0.8 - 0.6 * math.exp(-0.3 * l)
            mix, outs["dk"], outs["dv"], outs["rwkv"] = _odd_layer(
                x, mod[l], g_mix, b_w_in[j], diff_qn[j], diff_kn[j], diff_lam[j], diff_gn[j], rwkv_mu[j],
                rwkv_w0[j], rwkv_w_up[j], rwkv_a0[j], rwkv_a_up[j], rwkv_g_up[j], rwkv_k_k[j], rwkv_k_a[j],
                rwkv_r_k[j], rwkv_gn[j], cache_diff_k[:, j], cache_diff_v[:, j], state_rwkv[:, j], tabs_d, lam_init)
        x = _resid_proj(mix, w_out[l], x, mod[l], 2)
        act = _ffn_up(x, norm_ffn_g[l][None], mod[l], ffn_up[l], ffn_conv_w[l], ffn_conv_b[l])
        x = _resid_proj([act], ffn_down[l], x, mod[l], 5, split_out=(l == DEPTH - 1))

    y_prompt = x[0].reshape(BATCH, SEQ, D_MODEL)
    y_sample = x[1].reshape(DEC_BATCH, DEC_SEQ, D_MODEL)
    return (y_prompt, y_sample, outs["ckv"][:, None], outs["krope"][:, None], outs["ret"][:, None],
            outs["dk"][:, None], outs["dv"][:, None], outs["rwkv"][:, None])
```

```python
import functools
import math

import numpy as np
import jax
import jax.numpy as jnp
from jax import lax
from jax.experimental import pallas as pl
from jax.experimental.pallas import tpu as pltpu

D_MODEL = 1024
BATCH = 16
SEQ = 256
DEPTH = 2
DEC_BATCH = 2
DEC_SEQ = 1024
PAST_LEN = 512
GRID_W = 64
EPS = 1e-6
ROPE_BASE = 10000.0

MLA_HEADS = 8
MLA_Q_RANK = 256
MLA_KV_RANK = 128
MLA_NOPE = 64
MLA_ROPE = 32
MLA_V = 64
MLA_QK = MLA_NOPE + MLA_ROPE
RET_HEADS = 4
RET_DK = 64
RET_DV = 128
DIFF_HEADS = 4
DIFF_DH = 64
DIFF_W = DIFF_HEADS * 2 * DIFF_DH
RWKV_HEADS = 8
RWKV_HS = 64
RWKV_W = RWKV_HEADS * RWKV_HS
RWKV_W_LORA = 64
RWKV_A_LORA = 64
RWKV_G_LORA = 128
D_FF = 2816

N_CTX = BATCH * SEQ
N_LAT = DEC_BATCH * DEC_SEQ
N_TOK = N_CTX + N_LAT
N_GROUPS = 1 + DEC_BATCH

LANES = 128
VMEM_LIMIT = 56 * 1024 * 1024

_PREC = lax.Precision.HIGHEST
F32 = jnp.float32


def _dot_tn(a, b):
    return lax.dot_general(a, b, (((0,), (0,)), ((), ())), precision=_PREC, preferred_element_type=F32)


BF16 = jnp.bfloat16


def _dot_bf16(a, b):
    return jnp.dot(a.astype(BF16), b.astype(BF16), preferred_element_type=F32)


def _dot_nt_bf16(a, b):
    return lax.dot_general(a.astype(BF16), b.astype(BF16), (((1,), (1,)), ((), ())), preferred_element_type=F32)


def _params(*sem):
    return pltpu.CompilerParams(dimension_semantics=sem, vmem_limit_bytes=VMEM_LIMIT)


def _sigmoid(x):
    return 1.0 / (1.0 + jnp.exp(-x))


def _silu(x):
    return x * _sigmoid(x)


def _softplus(x):
    return jnp.maximum(x, 0.0) + jnp.log(1.0 + jnp.exp(-jnp.abs(x)))


def _rms(x, n):
    return x * lax.rsqrt(jnp.sum(x * x, axis=-1, keepdims=True) * (1.0 / n) + EPS)


def _lane_lo(shape):
    return lax.broadcasted_iota(jnp.int32, shape, len(shape) - 1) < 64


def _seg64_sum(x):
    lo = _lane_lo(x.shape)
    s_lo = jnp.sum(jnp.where(lo, x, 0.0), axis=-1, keepdims=True)
    s_hi = jnp.sum(jnp.where(lo, 0.0, x), axis=-1, keepdims=True)
    return jnp.where(lo, s_lo, s_hi)


_SUBLANES = 8


def _seq_neighbours(p, tile, tile_rows):
    is_ctx = tile * tile_rows < N_CTX
    sub = lax.broadcasted_iota(jnp.int32, (_SUBLANES, 1), 0)

    def shifted(rolled, edge_sublane, group_of_seq):
        pieces, start = [], 0
        for q in range(tile_rows // SEQ):
            g0 = q * SEQ + group_of_seq
            outer = (q == 0) if group_of_seq == 0 else (q == tile_rows // SEQ - 1)
            edge = (sub == edge_sublane) if outer else ((sub == edge_sublane) & is_ctx)
            pieces += [rolled[start:g0], jnp.where(edge, 0.0, rolled[g0:g0 + _SUBLANES])]
            start = g0 + _SUBLANES
        pieces.append(rolled[start:])
        return jnp.concatenate([x for x in pieces if x.shape[0]], axis=0)

    prev = shifted(pltpu.roll(p, 1, axis=0), 0, 0)
    nxt = shifted(pltpu.roll(p, tile_rows - 1, axis=0), _SUBLANES - 1, SEQ - _SUBLANES)
    return prev, nxt


def _group_of_tile(i, tile_rows):
    row = i * tile_rows
    return jnp.where(row < N_CTX, 0, 1 + (row - N_CTX) // DEC_SEQ)


def _modulation_kernel(c_ref, w_ref, b_ref, o_ref):
    o_ref[0] = _dot_bf16(_silu(c_ref[...]), w_ref[0]) + b_ref[0]


def _modulation(cond8, ada_w, ada_b):
    tn = 512
    n = 6 * D_MODEL
    out = pl.pallas_call(
        _modulation_kernel,
        grid=(DEPTH, n // tn),
        in_specs=[pl.BlockSpec((8, D_MODEL), lambda l, j: (0, 0)),
                  pl.BlockSpec((1, D_MODEL, tn), lambda l, j: (l, 0, j)),
                  pl.BlockSpec((1, 1, tn), lambda l, j: (l, 0, j))],
        out_specs=pl.BlockSpec((1, 8, tn), lambda l, j: (l, 0, j)),
        out_shape=jax.ShapeDtypeStruct((DEPTH, 8, n), F32),
        compiler_params=_params("parallel", "parallel"),
        name="modulation",
    )(cond8, ada_w, ada_b.reshape(DEPTH, 1, n))
    m = out[:, :N_GROUPS].reshape(DEPTH, N_GROUPS, 6, D_MODEL)
    return jnp.pad(m, ((0, 0), (0, 0), (0, 2), (0, 0)))


_TM_SEQ = 1024


def _norm_mod(x, g, mod, off):
    return _rms(x, D_MODEL) * g * (1.0 + mod[off + 1:off + 2, :]) + mod[off:off + 1, :]


def _inproj_kernel(x_ref, g_ref, mod_ref, w_ref, mu_ref, o_ref, h_ref, *, shift_from):
    i = pl.program_id(0)

    @pl.when(pl.program_id(1) == 0)
    def _():
        h_ref[...] = _norm_mod(x_ref[...], g_ref[...], mod_ref[0], 0).astype(BF16)

    p = _dot_bf16(h_ref[...], w_ref[...])
    if shift_from is None:
        o_ref[...] = p
    else:
        @pl.when(pl.program_id(1) < shift_from)
        def _():
            o_ref[...] = p

        @pl.when(pl.program_id(1) >= shift_from)
        def _():
            prev, nxt = _seq_neighbours(p, i, _TM_SEQ)
            o_ref[...] = p + (0.5 * (prev + nxt) - p) * mu_ref[...]


def _inproj(x, g, mod, w, mu, tn, shift_from):
    n = w.shape[1]
    tm = _TM_SEQ
    return pl.pallas_call(
        functools.partial(_inproj_kernel, shift_from=shift_from),
        grid=(N_TOK // tm, n // tn),
        in_specs=[pl.BlockSpec((tm, D_MODEL), lambda i, j: (i, 0)),
                  pl.BlockSpec((1, D_MODEL), lambda i, j: (0, 0)),
                  pl.BlockSpec((1, 8, D_MODEL), lambda i, j: (_group_of_tile(i, tm), 0, 0)),
                  pl.BlockSpec((D_MODEL, tn), lambda i, j: (0, j)),
                  pl.BlockSpec((1, tn), lambda i, j: (0, j))],
        out_specs=pl.BlockSpec((tm, tn), lambda i, j: (i, j)),
        out_shape=jax.ShapeDtypeStruct((N_TOK, n), F32),
        scratch_shapes=[pltpu.VMEM((tm, D_MODEL), BF16)],
        compiler_params=_params("parallel", "arbitrary"),
        name="inproj" if shift_from is None else "inproj_shift",
    )(x, g, mod, w, mu)


def _resid_kernel(*refs, gate_row, widths, split, split_out):
    n_in = sum(2 if sp else 1 for sp in split)
    a_refs = refs[:n_in]
    w_ref, x_ref, mod_ref = refs[n_in:n_in + 3]
    o_ref = refs[n_in + 3:-1] if split_out else refs[n_in + 3]
    a_bf_ref = refs[-1]
    i = pl.program_id(0)

    @pl.when(pl.program_id(1) == 0)
    def _():
        k0, r = 0, 0
        for width, sp in zip(widths, split):
            cols = slice(k0, k0 + width)
            if sp:
                ctx_ref, lat_ref = a_refs[r], a_refs[r + 1]

                @pl.when(i < N_CTX // _TM_SEQ)
                def _():
                    a_bf_ref[:, cols] = ctx_ref[...].astype(BF16)

                @pl.when(i >= N_CTX // _TM_SEQ)
                def _():
                    a_bf_ref[:, cols] = lat_ref[...].astype(BF16)
            else:
                a_bf_ref[:, cols] = a_refs[r][...].astype(BF16)
            k0 += width
            r += 2 if sp else 1

    y = x_ref[...] + mod_ref[0, gate_row:gate_row + 1, :] * _dot_bf16(a_bf_ref[...], w_ref[0])
    if not split_out:
        o_ref[...] = y
    else:
        ctx_o_ref, lat_o_ref = o_ref

        @pl.when(i < N_CTX // _TM_SEQ)
        def _():
            ctx_o_ref[...] = y

        @pl.when(i >= N_CTX // _TM_SEQ)
        def _():
            lat_o_ref[...] = y


def _resid_proj(acts, w, layer, x, mod, gate_row, split_out=False):
    tm, tn = _TM_SEQ, 256
    n_ctx = N_CTX // tm
    nj = D_MODEL // tn
    split = [isinstance(a, (tuple, list)) for a in acts]
    widths = [a[0].shape[1] if sp else a.shape[1] for a, sp in zip(acts, split)]
    k = sum(widths)
    in_specs, operands = [], []
    for a, width, sp in zip(acts, widths, split):
        if sp:
            in_specs += [pl.BlockSpec((tm, width), lambda i, j: (jnp.minimum(i, n_ctx - 1), 0)),
                         pl.BlockSpec((tm, width), lambda i, j: (jnp.maximum(i - n_ctx, 0), 0))]
            operands += list(a)
        else:
            in_specs.append(pl.BlockSpec((tm, width), lambda i, j: (i, 0)))
            operands.append(a)
    if split_out:
        out_specs = [pl.BlockSpec((tm, tn), lambda i, j: (jnp.minimum(i, n_ctx - 1), jnp.where(i < n_ctx, j, nj - 1))),
                     pl.BlockSpec((tm, tn), lambda i, j: (jnp.maximum(i - n_ctx, 0), jnp.where(i < n_ctx, 0, j)))]
        out_shape = [jax.ShapeDtypeStruct((N_CTX, D_MODEL), F32), jax.ShapeDtypeStruct((N_LAT, D_MODEL), F32)]
        sem = ("arbitrary", "arbitrary")
    else:
        out_specs = pl.BlockSpec((tm, tn), lambda i, j: (i, j))
        out_shape = jax.ShapeDtypeStruct((N_TOK, D_MODEL), F32)
        sem = ("parallel", "arbitrary")
    return pl.pallas_call(
        functools.partial(_resid_kernel, gate_row=gate_row, widths=tuple(widths), split=tuple(split),
                          split_out=split_out),
        grid=(N_TOK // tm, nj),
        in_specs=in_specs
        + [pl.BlockSpec((1, k, tn), lambda i, j: (layer, 0, j)),
           pl.BlockSpec((tm, tn), lambda i, j: (i, j)),
           pl.BlockSpec((1, 8, tn), lambda i, j: (_group_of_tile(i, tm), 0, j))],
        out_specs=out_specs,
        out_shape=out_shape,
        scratch_shapes=[pltpu.VMEM((tm, k), BF16)],
        compiler_params=_params(*sem),
        name="resid_proj",
    )(*operands, w, x, mod)


def _ffn_up_kernel(x_ref, g_ref, mod_ref, wa_ref, wb_ref, cwa_ref, cwb_ref, cba_ref, cbb_ref, o_ref, h_ref):
    i = pl.program_id(0)

    @pl.when(pl.program_id(1) == 0)
    def _():
        h_ref[...] = _norm_mod(x_ref[...], g_ref[...], mod_ref[0], 3).astype(BF16)

    h = h_ref[...]

    def conv(w_ref, cw_ref, cb_ref):
        u = _dot_bf16(h, w_ref[0])
        prev, nxt = _seq_neighbours(u, i, _TM_SEQ)
        return prev * cw_ref[0:1, :] + u * cw_ref[1:2, :] + nxt * cw_ref[2:3, :] + cb_ref[...]

    o_ref[...] = (_silu(conv(wa_ref, cwa_ref, cba_ref)) * conv(wb_ref, cwb_ref, cbb_ref)).astype(o_ref.dtype)


def _ffn_up(x, g, mod, up, layer, cw, cb):
    tm, tn = _TM_SEQ, 256
    nb = D_FF // tn
    cb = cb.reshape(1, 2 * D_FF)
    return pl.pallas_call(
        _ffn_up_kernel,
        grid=(N_TOK // tm, nb),
        in_specs=[pl.BlockSpec((tm, D_MODEL), lambda i, j: (i, 0)),
                  pl.BlockSpec((1, D_MODEL), lambda i, j: (0, 0)),
                  pl.BlockSpec((1, 8, D_MODEL), lambda i, j: (_group_of_tile(i, tm), 0, 0)),
                  pl.BlockSpec((1, D_MODEL, tn), lambda i, j: (layer, 0, j)),
                  pl.BlockSpec((1, D_MODEL, tn), lambda i, j: (layer, 0, j + nb)),
                  pl.BlockSpec((3, tn), lambda i, j: (0, j)),
                  pl.BlockSpec((3, tn), lambda i, j: (0, j + nb)),
                  pl.BlockSpec((1, tn), lambda i, j: (0, j)),
                  pl.BlockSpec((1, tn), lambda i, j: (0, j + nb))],
        out_specs=pl.BlockSpec((tm, tn), lambda i, j: (i, j)),
        out_shape=jax.ShapeDtypeStruct((N_TOK, D_FF), BF16),
        scratch_shapes=[pltpu.VMEM((tm, D_MODEL), BF16)],
        compiler_params=_params("parallel", "arbitrary"),
        name="ffn_up",
    )(x, g, mod, up, up, cw, cw, cb, cb)


_ROPE_TM = 512


def _rope(y, c, s1, s2):
    return y * c + pltpu.roll(y, 1, axis=1) * s1 + pltpu.roll(y, LANES - 1, axis=1) * s2


def _rope_tables(rot_dim, lane_offsets):
    t = np.arange(DEC_SEQ)
    row, col = t // GRID_W, t % GRID_W
    n_freq = rot_dim // 4
    inv = ROPE_BASE ** (-np.arange(n_freq, dtype=np.float64) / n_freq)
    ang = np.concatenate([row[:, None] * inv, col[:, None] * inv], -1)
    cos, sin = np.cos(ang), np.sin(ang)
    n = _ROPE_TM + DEC_SEQ
    c, s1, s2 = np.ones((n, LANES)), np.zeros((n, LANES)), np.zeros((n, LANES))
    for a in lane_offsets:
        even = a + 2 * np.arange(rot_dim // 2)
        c[_ROPE_TM:, even] = cos
        c[_ROPE_TM:, even + 1] = cos
        s1[_ROPE_TM:, even + 1] = sin
        s2[_ROPE_TM:, even] = -sin
    return tuple(jnp.asarray(x, F32) for x in (c, s1, s2))


def _rope_block(i):
    row = i * _ROPE_TM
    return jnp.where(row < N_CTX, 0, 1 + ((row - N_CTX) % DEC_SEQ) // _ROPE_TM)


def _mla_q_kernel(cq_ref, qnorm_ref, w_ref, qn_ref, c_ref, s1_ref, s2_ref, o_ref):
    xn = _rms(cq_ref[...], MLA_Q_RANK) * qnorm_ref[...]
    y = _dot_bf16(xn, w_ref[...])
    c, s1, s2 = c_ref[...], s1_ref[...], s2_ref[...]
    for h in range(MLA_HEADS):
        yh = y[:, h * LANES:(h + 1) * LANES]
        yh = _rms(yh, MLA_QK) * qn_ref[...]
        o_ref[:, h * LANES:(h + 1) * LANES] = _rope(yh, c, s1, s2)


def _mla_q(p, q_norm, w_uq_p, qn_p, tabs):
    tm = _ROPE_TM
    hw = MLA_HEADS * LANES
    tab_spec = pl.BlockSpec((tm, LANES), lambda i: (_rope_block(i), 0))
    return pl.pallas_call(
        _mla_q_kernel,
        grid=(N_TOK // tm,),
        in_specs=[pl.BlockSpec((tm, MLA_Q_RANK), lambda i: (i, 0)),
                  pl.BlockSpec((1, MLA_Q_RANK), lambda i: (0, 0)),
                  pl.BlockSpec((MLA_Q_RANK, hw), lambda i: (0, 0)),
                  pl.BlockSpec((1, LANES), lambda i: (0, 0)),
                  tab_spec, tab_spec, tab_spec],
        out_specs=pl.BlockSpec((tm, hw), lambda i: (i, 0)),
        out_shape=jax.ShapeDtypeStruct((N_TOK, hw), F32),
        compiler_params=_params("parallel"),
        name="mla_q",
    )(p, q_norm, w_uq_p, qn_p, *tabs)


def _mla_kv_kernel(ckv_ref, kr_ref, kvn_ref, wk_ref, wv_ref, kn_ref, c_ref, s1_ref, s2_ref,
                   k_ref, v_ref, ckvn_ref, *, norm_ckv):
    ckv = ckv_ref[...]
    if norm_ckv:
        ckv = _rms(ckv, MLA_KV_RANK) * kvn_ref[...]
    ckvn_ref[...] = ckv
    ckv_bf = ckv.astype(BF16)
    kk = _dot_bf16(ckv_bf, wk_ref[...])
    v_ref[...] = _dot_bf16(ckv_bf, wv_ref[...])
    kr = kr_ref[...]
    c, s1, s2 = c_ref[...], s1_ref[...], s2_ref[...]
    for h in range(MLA_HEADS):
        kh = kk[:, h * LANES:(h + 1) * LANES] + kr
        kh = _rms(kh, MLA_QK) * kn_ref[...]
        k_ref[:, h * LANES:(h + 1) * LANES] = _rope(kh, c, s1, s2)


def _mla_kv(ckv_src, ckv_blk, kr_src, kr_blk, kv_norm, wk_p, wv_p, kn_p, tabs, n_rows, own_tokens):
    tm = _ROPE_TM
    hw = MLA_HEADS * LANES
    tab_spec = pl.BlockSpec((tm, LANES), (lambda i: (_rope_block(i), 0)) if own_tokens else (lambda i: (0, 0)))
    return pl.pallas_call(
        functools.partial(_mla_kv_kernel, norm_ckv=own_tokens),
        grid=(n_rows // tm,),
        in_specs=[pl.BlockSpec((tm, LANES), lambda i: (i, ckv_blk)),
                  pl.BlockSpec((tm, LANES), lambda i: (i, kr_blk)),
                  pl.BlockSpec((1, LANES), lambda i: (0, 0)),
                  pl.BlockSpec((MLA_KV_RANK, hw), lambda i: (0, 0)),
                  pl.BlockSpec((MLA_KV_RANK, hw), lambda i: (0, 0)),
                  pl.BlockSpec((1, LANES), lambda i: (0, 0)),
                  tab_spec, tab_spec, tab_spec],
        out_specs=[pl.BlockSpec((tm, hw), lambda i: (i, 0)),
                   pl.BlockSpec((tm, hw), lambda i: (i, 0)),
                   pl.BlockSpec((tm, LANES), lambda i: (i, 0))],
        out_shape=[jax.ShapeDtypeStruct((n_rows, hw), F32),
                   jax.ShapeDtypeStruct((n_rows, hw), F32),
                   jax.ShapeDtypeStruct((n_rows, LANES), F32)],
        compiler_params=_params("parallel"),
        name="mla_kv",
    )(ckv_src, kr_src, kv_norm, wk_p, wv_p, kn_p, *tabs)


_LOG2E = math.log2(math.e)


def _softmax_parts(scores, scale):
    m = functools.reduce(jnp.maximum, [jnp.max(sc, axis=-1, keepdims=True) for sc in scores])
    ps = [jnp.exp2((sc - m) * (scale * _LOG2E)) for sc in scores]
    return ps, sum(jnp.sum(p, axis=-1, keepdims=True) for p in ps)


def _attend(q, sources, sl, scale):
    ps, l = _softmax_parts([_dot_nt_bf16(q, k_ref[:, sl]) for k_ref, _ in sources], scale)
    return sum(_dot_bf16(p, v_ref[:, sl]) for p, (_, v_ref) in zip(ps, sources)), l


def _mla_attn_kernel(*refs, pairs):
    q_ref, o_ref = refs[0], refs[-1]
    sources = [refs[1:3]] + ([refs[3:5]] if len(refs) == 6 else [])
    scale = MLA_QK ** -0.5
    for pr in range(pairs):
        outs = []
        for h in range(2):
            sl = slice((2 * pr + h) * LANES, (2 * pr + h + 1) * LANES)
            o, l = _attend(q_ref[:, sl], sources, sl, scale)
            outs.append(o / l)
        o_ref[:, pr * LANES:(pr + 1) * LANES] = (outs[0] + pltpu.roll(outs[1], MLA_V, axis=1)).astype(o_ref.dtype)


def _mla_attn(q, k, v, cache, batch, n, row0, tq, pairs):
    nqb = n // tq
    qb0 = row0 // tq
    kb0 = row0 // n
    wide = 2 * LANES * pairs
    kv_spec = pl.BlockSpec((n, wide), lambda b, h, i: (kb0 + b, h))
    in_specs, operands = [pl.BlockSpec((tq, wide), lambda b, h, i: (qb0 + b * nqb + i, h)), kv_spec, kv_spec], [q, k, v]
    if cache is not None:
        in_specs += [pl.BlockSpec((PAST_LEN, wide), lambda b, h, i: (b, h))] * 2
        operands += list(cache)
    return pl.pallas_call(
        functools.partial(_mla_attn_kernel, pairs=pairs),
        grid=(batch, MLA_HEADS // (2 * pairs), nqb),
        in_specs=in_specs,
        out_specs=pl.BlockSpec((tq, LANES * pairs), lambda b, h, i: (b * nqb + i, h)),
        out_shape=jax.ShapeDtypeStruct((batch * n, MLA_HEADS * MLA_V), BF16),
        compiler_params=_params("parallel", "parallel", "arbitrary"),
        name="mla_attn",
    )(*operands)


def _ret_kernel(lg_ref, q_ref, k_ref, v_ref, rg_ref, s0_ref, gn_ref, *out_and_scratch, n, tq, want_state):
    if want_state:
        o_ref, st_ref, decay_ref = out_and_scratch
    else:
        o_ref, decay_ref = out_and_scratch
    pair, qi, b = pl.program_id(0), pl.program_id(1), pl.program_id(2)
    q = q_ref[...]
    k = k_ref[...] * (RET_DK ** -0.5)
    lo = _lane_lo((1, LANES))
    row = (qi * tq + lax.broadcasted_iota(jnp.int32, (tq, 1), 0)).astype(F32)

    @pl.when(b == 0)
    def _():
        col = lax.broadcasted_iota(jnp.int32, (1, n), 1).astype(F32)
        diff = row - col
        for h in range(2):
            lgf = lg_ref[0, 2 * pair + h]
            lgb = lg_ref[1, 2 * pair + h]
            decay_ref[h] = (jnp.where(diff >= 0, jnp.exp(lgf * jnp.maximum(diff, 0.0)), 0.0)
                            + jnp.where(diff <= 0, jnp.exp(lgb * jnp.maximum(-diff, 0.0)), 0.0))

    for h in range(2):
        lgf = lg_ref[0, 2 * pair + h]
        lgb = lg_ref[1, 2 * pair + h]
        mask = lo if h == 0 else jnp.logical_not(lo)
        qh = jnp.where(mask, q, 0.0)
        vh = v_ref[:, h * LANES:(h + 1) * LANES]
        o = _dot_bf16(_dot_nt_bf16(qh, k) * decay_ref[h], vh)
        o = o + _dot_bf16(qh * jnp.exp(lgf * (row + 1.0)), s0_ref[0, 0])
        o = o + _dot_bf16(qh * jnp.exp(lgb * (n - row)), s0_ref[0, 1])
        y = _rms(o, RET_DV) * gn_ref[:, h * LANES:(h + 1) * LANES]
        o_ref[:, h * LANES:(h + 1) * LANES] = (_silu(rg_ref[:, h * LANES:(h + 1) * LANES]) * y).astype(o_ref.dtype)

    if want_state:
        pos = lax.broadcasted_iota(jnp.int32, (n, 1), 0).astype(F32)
        for d in range(2):
            acc = None
            for h in range(2):
                lg = lg_ref[d, 2 * pair + h]
                mask = lo if h == 0 else jnp.logical_not(lo)
                expo = (n - 1.0 - pos) if d == 0 else pos
                kd = jnp.where(mask, k * jnp.exp(lg * expo), 0.0)
                term = _dot_tn(kd, v_ref[:, h * LANES:(h + 1) * LANES])
                acc = term if acc is None else acc + term
            lg_rows = jnp.where(lax.broadcasted_iota(jnp.int32, (LANES, 1), 0) < 64,
                                lg_ref[d, 2 * pair], lg_ref[d, 2 * pair + 1])
            st_ref[0, d] = acc + s0_ref[0, d] * jnp.exp(lg_rows * n)


def _retention(log_g, p, q_blk, k_blk, v_blk, g_blk, s0, gn, batch, n, row0, tq, want_state):
    nqb = n // tq
    assert not want_state or nqb == 1
    qb0 = row0 // tq
    kb0 = row0 // n
    pairs = RET_HEADS // 2
    out_specs = [pl.BlockSpec((tq, 2 * LANES), lambda h, i, b: (b * nqb + i, h))]
    out_shape = [jax.ShapeDtypeStruct((batch * n, RET_HEADS * RET_DV), BF16)]
    if want_state:
        out_specs.append(pl.BlockSpec((1, 2, LANES, LANES), lambda h, i, b: (b, 0, h, 0)))
        out_shape.append(jax.ShapeDtypeStruct((batch, 2, RET_HEADS * RET_DK, RET_DV), F32))
    outs = pl.pallas_call(
        functools.partial(_ret_kernel, n=n, tq=tq, want_state=want_state),
        grid=(pairs, nqb, batch),
        in_specs=[pl.BlockSpec(memory_space=pltpu.SMEM),
                  pl.BlockSpec((tq, LANES), lambda h, i, b: (qb0 + b * nqb + i, q_blk + h)),
                  pl.BlockSpec((n, LANES), lambda h, i, b: (kb0 + b, k_blk + h)),
                  pl.BlockSpec((n, 2 * LANES), lambda h, i, b: (kb0 + b, v_blk // 2 + h)),
                  pl.BlockSpec((tq, 2 * LANES), lambda h, i, b: (qb0 + b * nqb + i, g_blk // 2 + h)),
                  pl.BlockSpec((1, 2, LANES, LANES), lambda h, i, b: (b, 0, h, 0)),
                  pl.BlockSpec((1, 2 * LANES), lambda h, i, b: (0, h))],
        out_specs=out_specs,
        out_shape=out_shape,
        scratch_shapes=[pltpu.VMEM((2, tq, n), F32)],
        compiler_params=_params("parallel", "parallel", "arbitrary"),
        name="retention",
    )(log_g, p, p, p, p, s0, gn)
    return outs if want_state else (outs[0], None)


def _diff_qk_kernel(q_ref, k_ref, v_ref, qn_ref, kn_ref, c_ref, s1_ref, s2_ref, qo_ref, ko_ref, dk_ref, dv_ref):
    c, s1, s2 = c_ref[...], s1_ref[...], s2_ref[...]
    is_ctx = pl.program_id(0) < N_CTX // _ROPE_TM
    for src, gain, dst in ((q_ref, qn_ref, qo_ref), (k_ref, kn_ref, ko_ref)):
        for h in range(DIFF_HEADS):
            sl = slice(h * LANES, (h + 1) * LANES)
            y = src[:, sl]
            y = _rope(y * lax.rsqrt(_seg64_sum(y * y) * (1.0 / DIFF_DH) + EPS) * gain[...], c, s1, s2)
            dst[:, sl] = y
            if dst is ko_ref:
                @pl.when(is_ctx)
                def _():
                    dk_ref[:, h, 0, :] = y[:, :DIFF_DH]
                    dk_ref[:, h, 1, :] = pltpu.roll(y, DIFF_DH, axis=1)[:, :DIFF_DH]
                    dv_ref[:, h, :] = v_ref[:, sl]


def _diff_qk(p, qn_p, kn_p, tabs):
    tm = _ROPE_TM
    n_ctx = N_CTX // tm
    tab_spec = pl.BlockSpec((tm, LANES), lambda i: (_rope_block(i), 0))
    ctx_blk = lambda i: jnp.minimum(i, n_ctx - 1)
    return pl.pallas_call(
        _diff_qk_kernel,
        grid=(N_TOK // tm,),
        in_specs=[pl.BlockSpec((tm, DIFF_W), lambda i: (i, 0)),
                  pl.BlockSpec((tm, DIFF_W), lambda i: (i, 1)),
                  pl.BlockSpec((tm, DIFF_W), lambda i: (i, 2)),
                  pl.BlockSpec((1, LANES), lambda i: (0, 0)),
                  pl.BlockSpec((1, LANES), lambda i: (0, 0)),
                  tab_spec, tab_spec, tab_spec],
        out_specs=[pl.BlockSpec((tm, DIFF_W), lambda i: (i, 0)),
                   pl.BlockSpec((tm, DIFF_W), lambda i: (i, 0)),
                   pl.BlockSpec((tm, DIFF_HEADS, 2, DIFF_DH), lambda i: (ctx_blk(i), 0, 0, 0)),
                   pl.BlockSpec((tm, DIFF_HEADS, 2 * DIFF_DH), lambda i: (ctx_blk(i), 0, 0))],
        out_shape=[jax.ShapeDtypeStruct((N_TOK, DIFF_W), F32)] * 2
        + [jax.ShapeDtypeStruct((N_CTX, DIFF_HEADS, 2, DIFF_DH), F32),
           jax.ShapeDtypeStruct((N_CTX, DIFF_HEADS, 2 * DIFF_DH), F32)],
        compiler_params=_params("arbitrary"),
        name="diff_qk",
    )(p, p, p, qn_p, kn_p, *tabs)


def _diff_attn_kernel(*refs, lam_init, heads):
    lam_ref, q_ref, gn_ref, o_ref = refs[0], refs[1], refs[-2], refs[-1]
    sources = [refs[2:4]] + ([refs[4:6]] if len(refs) == 8 else [])
    lv = lam_ref[...]
    lam = (jnp.exp(jnp.sum(lv[0:1] * lv[1:2], axis=-1, keepdims=True))
           - jnp.exp(jnp.sum(lv[2:3] * lv[3:4], axis=-1, keepdims=True)) + lam_init)
    scale = DIFF_DH ** -0.5
    lo = _lane_lo((1, LANES))
    for h in range(heads):
        sl = slice(h * LANES, (h + 1) * LANES)
        q = q_ref[:, sl]
        kbs = [k_ref[:, sl].astype(BF16) for k_ref, _ in sources]
        ps1, l1 = _softmax_parts([_dot_nt_bf16(jnp.where(lo, q, 0.0), kb) for kb in kbs], scale)
        ps2, l2 = _softmax_parts([_dot_nt_bf16(jnp.where(lo, 0.0, q), kb) for kb in kbs], scale)
        o = sum(_dot_bf16(p1 / l1 - lam * (p2 / l2), v_ref[:, sl])
                for p1, p2, (_, v_ref) in zip(ps1, ps2, sources))
        o_ref[:, sl] = (_rms(o, 2 * DIFF_DH) * gn_ref[:, sl] * (1.0 - lam_init)).astype(o_ref.dtype)


def _diff_attn(lam, q, k, v, v_blk0, cache, gn, batch, n, row0, tq, lam_init, heads):
    nqb = n // tq
    qb0 = row0 // tq
    kb0 = row0 // n
    wide = LANES * heads
    vb0 = v_blk0 // heads
    in_specs = [pl.BlockSpec((4, DIFF_DH), lambda b, h, i: (0, 0)),
                pl.BlockSpec((tq, wide), lambda b, h, i: (qb0 + b * nqb + i, h)),
                pl.BlockSpec((n, wide), lambda b, h, i: (kb0 + b, h)),
                pl.BlockSpec((n, wide), lambda b, h, i: (kb0 + b, vb0 + h))]
    operands = [lam, q, k, v]
    if cache is not None:
        in_specs += [pl.BlockSpec((PAST_LEN, wide), lambda b, h, i: (b, h))] * 2
        operands += list(cache)
    return pl.pallas_call(
        functools.partial(_diff_attn_kernel, lam_init=lam_init, heads=heads),
        grid=(batch, DIFF_HEADS // heads, nqb),
        in_specs=in_specs + [pl.BlockSpec((1, wide), lambda b, h, i: (0, h))],
        out_specs=pl.BlockSpec((tq, wide), lambda b, h, i: (b * nqb + i, h)),
        out_shape=jax.ShapeDtypeStruct((batch * n, DIFF_W), BF16),
        compiler_params=_params("parallel", "parallel", "arbitrary"),
        name="diff_attn",
    )(*operands, gn)


def _seg64_sum_wide(x):
    return jnp.concatenate([_seg64_sum(x[:, j * LANES:(j + 1) * LANES]) for j in range(x.shape[1] // LANES)], axis=1)


_SCAN_SLOTS = 3
_SCAN_NJ = _SCAN_SLOTS * RWKV_HEADS


def _rwkv_pre_kernel(r_ref, k_ref, v_ref, lo_ref, wup_ref, aup_ref, gup_ref, w0_ref, a0_ref, kk_ref, ka_ref, rk_ref,
                     op_ref, g_ref, bonus_ref):
    W = RWKV_W
    col = lambda q: slice(q * W, (q + 1) * W)
    r = r_ref[...]
    k = k_ref[...]
    v = v_ref[...]
    lora = lo_ref[...]
    kk = k * kk_ref[...]
    kkn = kk * lax.rsqrt(_seg64_sum_wide(kk * kk) + EPS)
    g_ref[...] = _dot_bf16(_sigmoid(lora[:, 2 * LANES:3 * LANES]), gup_ref[...])
    pre = w0_ref[...] + _dot_bf16(jnp.tanh(lora[:, 0:LANES]), wup_ref[...])
    decay = jnp.exp(-jnp.exp(-_softplus(-pre) - 0.5))
    a = _sigmoid(a0_ref[...] + _dot_bf16(lora[:, LANES:2 * LANES], aup_ref[...]))
    lo = _lane_lo((1, LANES))
    bonus = None
    for d in range(2):
        a_d = a[:, col(d)]
        k_d = k * (1.0 + (a_d - 1.0) * ka_ref[...])
        t = _seg64_sum_wide(r * k_d * rk_ref[...])
        bonus = t if bonus is None else bonus + t
        for s, (x1, x2) in enumerate(((kkn, decay[:, col(d)]), (k_d, kkn * a_d), (r, v))):
            for h in range(RWKV_HEADS):
                blk = slice((h // 2) * LANES, (h // 2 + 1) * LANES)
                if h % 2 == 0:
                    out = jnp.where(lo, x1[:, blk], pltpu.roll(x2[:, blk], RWKV_HS, axis=1))
                else:
                    out = jnp.where(lo, pltpu.roll(x1[:, blk], RWKV_HS, axis=1), x2[:, blk])
                op_ref[:, d * _SCAN_NJ + s * RWKV_HEADS + h, :] = out
    bonus_ref[...] = bonus


def _rwkv_pre(p, r_blk, k_blk, v_blk, lo_blk, wup_bd, aup_bd, gup, w0, a0, k_k, k_a, r_k):
    tm = 256
    w = RWKV_W
    row = lambda n: pl.BlockSpec((1, n), lambda i: (0, 0))
    full = lambda a, b: pl.BlockSpec((a, b), lambda i: (0, 0))
    return pl.pallas_call(
        _rwkv_pre_kernel,
        grid=(N_TOK // tm,),
        in_specs=[pl.BlockSpec((tm, w), lambda i: (i, r_blk)),
                  pl.BlockSpec((tm, w), lambda i: (i, k_blk)),
                  pl.BlockSpec((tm, w), lambda i: (i, v_blk)),
                  pl.BlockSpec((tm, 3 * LANES), lambda i: (i, lo_blk)),
                  full(LANES, 2 * w), full(LANES, 2 * w), full(LANES, w),
                  row(2 * w), row(2 * w), row(w), row(w), row(w)],
        out_specs=[pl.BlockSpec((tm, 2 * _SCAN_NJ, LANES), lambda i: (i, 0, 0)),
                   pl.BlockSpec((tm, w), lambda i: (i, 0)), pl.BlockSpec((tm, w), lambda i: (i, 0))],
        out_shape=[jax.ShapeDtypeStruct((N_TOK, 2 * _SCAN_NJ, LANES), F32),
                   jax.ShapeDtypeStruct((N_TOK, w), F32), jax.ShapeDtypeStruct((N_TOK, w), F32)],
        compiler_params=_params("parallel"),
        name="rwkv_pre",
    )(p, p, p, p, wup_bd, aup_bd, gup, w0, a0, k_k, k_a, r_k)


_SCAN_CHUNK = 32
_SCAN_UNROLL = 16


def _rwkv_first_sa(s_ref, sa_ref, kk, n_k):
    nv = s_ref.shape[1]
    chunk = min(_SCAN_CHUNK, nv)
    for c0 in range(0, nv, chunk):
        def body(k, acc):
            return acc + s_ref[k, c0:c0 + chunk, :] * kk(k)
        sa_ref[c0:c0 + chunk, :] = lax.fori_loop(0, n_k, body, jnp.zeros((chunk, LANES), F32), unroll=_SCAN_UNROLL)


def _rwkv_step(s_ref, sa_ref, kk_next, w, kd, b, r, v_at, n_k):
    nv = s_ref.shape[1]
    chunk = min(_SCAN_CHUNK, nv)
    ys = []
    for c0 in range(0, nv, chunk):
        sa = sa_ref[c0:c0 + chunk, :]
        vc = v_at(c0, chunk)

        def body(k, acc):
            y_acc, sa_acc = acc
            s_new = s_ref[k, c0:c0 + chunk, :] * w(k) - sa * b(k) + vc * kd(k)
            s_ref[k, c0:c0 + chunk, :] = s_new
            return y_acc + s_new * r(k), sa_acc + s_new * kk_next(k)

        zero = jnp.zeros((chunk, LANES), F32)
        y_acc, sa_acc = lax.fori_loop(0, n_k, body, (zero, zero), unroll=_SCAN_UNROLL)
        sa_ref[c0:c0 + chunk, :] = sa_acc
        ys.append(y_acc)
    return ys[0] if len(ys) == 1 else jnp.concatenate(ys, axis=0)


_CTX_TB = 32


def _load_scan_operands(kt_ref, t, slabs):
    for s, x in enumerate(slabs):
        xt = x.T
        kt_ref[t, 2 * s] = xt[:RWKV_HS]
        kt_ref[t, 2 * s + 1] = xt[RWKV_HS:]


def _rwkv_scan_ctx_kernel(x_ref, y_ref, st_ref, s_ref, sa_ref, kt_ref, ys_ref):
    d = pl.program_id(0)
    tb = pl.program_id(1)
    nh = RWKV_HEADS
    step_t = lambda i: jnp.where(d == 0, i, _CTX_TB - 1 - i)

    @pl.when(tb == 0)
    def _():
        s_ref[...] = jnp.zeros_like(s_ref)

    def load_t(t, carry):
        _load_scan_operands(kt_ref, t, [
            jnp.concatenate([x_ref[b, t, s * nh:(s + 1) * nh, :] for b in range(BATCH)], axis=0)
            for s in range(_SCAN_SLOTS)])
        return carry

    lax.fori_loop(0, _CTX_TB, load_t, 0, unroll=4)

    t0 = step_t(0)
    _rwkv_first_sa(s_ref, sa_ref, lambda k: kt_ref[t0, 0, pl.ds(k, 1), :], RWKV_HS)

    def step(i, carry):
        t = step_t(i)
        tn = step_t(jnp.minimum(i + 1, _CTX_TB - 1))
        row = lambda q, tt: (lambda k: kt_ref[tt, q, pl.ds(k, 1), :])
        ys_ref[t] = _rwkv_step(s_ref, sa_ref, row(0, tn), row(1, t), row(2, t), row(3, t), row(4, t),
                               lambda c0, n: kt_ref[t, 5, pl.ds(c0, n), :], RWKV_HS)
        return carry

    lax.fori_loop(0, _CTX_TB, step, 0)

    def store_t(i, carry):
        z = jnp.concatenate([ys_ref[2 * i], ys_ref[2 * i + 1]], axis=0).T
        z_odd = pltpu.roll(z, RWKV_HS, axis=1)
        for b in range(BATCH):
            y_ref[b, 2 * i] = z[b * nh:(b + 1) * nh, :RWKV_HS]
            y_ref[b, 2 * i + 1] = z_odd[b * nh:(b + 1) * nh, :RWKV_HS]
        return carry

    lax.fori_loop(0, _CTX_TB // 2, store_t, 0, unroll=4)

    @pl.when(tb == pl.num_programs(1) - 1)
    def _():
        st_ref[0] = s_ref[...]


def _rwkv_scan_ctx(op4):
    nt = SEQ // _CTX_TB
    hs = RWKV_HS
    tblk = lambda d, tb: jnp.where(d == 0, tb, nt - 1 - tb)
    return pl.pallas_call(
        _rwkv_scan_ctx_kernel,
        grid=(2, nt),
        in_specs=[pl.BlockSpec((BATCH, _CTX_TB, _SCAN_NJ, LANES), lambda d, tb: (0, tblk(d, tb), d, 0))],
        out_specs=[pl.BlockSpec((BATCH, _CTX_TB, RWKV_HEADS, hs), lambda d, tb: (0, tblk(d, tb), d, 0)),
                   pl.BlockSpec((1, hs, hs, LANES), lambda d, tb: (d, 0, 0, 0))],
        out_shape=[jax.ShapeDtypeStruct((BATCH, SEQ, 2 * RWKV_HEADS, hs), F32),
                   jax.ShapeDtypeStruct((2, hs, hs, LANES), F32)],
        scratch_shapes=[pltpu.VMEM((hs, hs, LANES), F32), pltpu.VMEM((hs, LANES), F32),
                        pltpu.VMEM((_CTX_TB, 2 * _SCAN_SLOTS, hs, LANES), F32),
                        pltpu.VMEM((_CTX_TB, hs, LANES), F32)],
        compiler_params=_params("parallel", "arbitrary"),
        name="rwkv_scan_ctx",
    )(op4)


_LAT_TB = 32
_LAT_VSPLIT = 4
_LAT_STATES = 2 * DEC_BATCH * RWKV_HEADS
_LAT_VROWS = RWKV_HS // _LAT_VSPLIT


def _rwkv_scan_lat_kernel(xf0_ref, xf1_ref, xb0_ref, xb1_ref, s0_ref, yf_ref, yb_ref,
                          s_ref, sa_ref, kt_ref, v_ref, ys_ref):
    @pl.when(pl.program_id(0) == 0)
    def _():
        s_ref[...] = s0_ref[...]

    group = lax.broadcasted_iota(jnp.int32, (_LAT_VROWS, LANES), 1) // _LAT_STATES
    nh = RWKV_HEADS

    def load_t(t, carry):
        tr = _LAT_TB - 1 - t
        for s in range(_SCAN_SLOTS):
            heads = slice(s * nh, (s + 1) * nh)
            x = jnp.concatenate([xf0_ref[0, t, heads, :], xf1_ref[0, t, heads, :],
                                 xb0_ref[0, tr, heads, :], xb1_ref[0, tr, heads, :]], axis=0)
            xt = jnp.concatenate([x] * _LAT_VSPLIT, axis=0).T
            kt_ref[t, 2 * s] = xt[:RWKV_HS]
            if s < _SCAN_SLOTS - 1:
                kt_ref[t, 2 * s + 1] = xt[RWKV_HS:]
            else:
                v = jnp.zeros((_LAT_VROWS, LANES), F32)
                for g in range(_LAT_VSPLIT):
                    r0 = RWKV_HS + g * _LAT_VROWS
                    v = jnp.where(group == g, xt[r0:r0 + _LAT_VROWS, :], v)
                v_ref[t] = v
        return carry

    lax.fori_loop(0, _LAT_TB, load_t, 0, unroll=8)

    _rwkv_first_sa(s_ref, sa_ref, lambda k: kt_ref[0, 0, pl.ds(k, 1), :], RWKV_HS)

    def step(t, carry):
        tn = jnp.minimum(t + 1, _LAT_TB - 1)
        row = lambda q, tt: (lambda k: kt_ref[tt, q, pl.ds(k, 1), :])
        ys_ref[t] = _rwkv_step(s_ref, sa_ref, row(0, tn), row(1, t), row(2, t), row(3, t), row(4, t),
                               lambda c0, n: v_ref[t, pl.ds(c0, n), :], RWKV_HS)
        return carry

    lax.fori_loop(0, _LAT_TB, step, 0)

    def store_t(i, carry):
        rows = [jnp.where(group == g, ys_ref[2 * i + j], 0.0) for j in range(2) for g in range(_LAT_VSPLIT)]
        z = jnp.concatenate(rows, axis=0).T
        y = (z[0:_LAT_STATES] + z[_LAT_STATES:2 * _LAT_STATES]
             + z[2 * _LAT_STATES:3 * _LAT_STATES] + z[3 * _LAT_STATES:4 * _LAT_STATES])
        y_odd = pltpu.roll(y, RWKV_HS, axis=1)
        half = _LAT_STATES // 2
        for b in range(DEC_BATCH):
            rows_f = slice(b * nh, (b + 1) * nh)
            rows_b = slice(half + b * nh, half + (b + 1) * nh)
            yf_ref[b, 2 * i] = y[rows_f, :RWKV_HS]
            yf_ref[b, 2 * i + 1] = y_odd[rows_f, :RWKV_HS]
            yb_ref[b, _LAT_TB - 1 - 2 * i] = y[rows_b, :RWKV_HS]
            yb_ref[b, _LAT_TB - 2 - 2 * i] = y_odd[rows_b, :RWKV_HS]
        return carry

    lax.fori_loop(0, _LAT_TB // 2, store_t, 0, unroll=4)


def _rwkv_scan_lat(op4, s0):
    hs = RWKV_HS
    nv = _LAT_VROWS
    nt = DEC_SEQ // _LAT_TB
    per_seq = SEQ // _LAT_TB
    first = N_CTX // SEQ

    def x_spec(b, d):
        tblk = (lambda tb: tb) if d == 0 else (lambda tb: nt - 1 - tb)
        return pl.BlockSpec((1, _LAT_TB, _SCAN_NJ, LANES),
                            lambda tb: (first + b * (DEC_SEQ // SEQ) + tblk(tb) // per_seq, tblk(tb) % per_seq, d, 0))

    y_shape = jax.ShapeDtypeStruct((DEC_BATCH, DEC_SEQ, RWKV_HEADS, hs), F32)
    y_blk = (DEC_BATCH, _LAT_TB, RWKV_HEADS, hs)
    return pl.pallas_call(
        _rwkv_scan_lat_kernel,
        grid=(nt,),
        in_specs=[x_spec(0, 0), x_spec(1, 0), x_spec(0, 1), x_spec(1, 1),
                  pl.BlockSpec((hs, nv, LANES), lambda tb: (0, 0, 0))],
        out_specs=[pl.BlockSpec(y_blk, lambda tb: (0, tb, 0, 0)),
                   pl.BlockSpec(y_blk, lambda tb: (0, nt - 1 - tb, 0, 0))],
        out_shape=[y_shape, y_shape],
        scratch_shapes=[pltpu.VMEM((hs, nv, LANES), F32), pltpu.VMEM((nv, LANES), F32),
                        pltpu.VMEM((_LAT_TB, 2 * _SCAN_SLOTS - 1, hs, LANES), F32),
                        pltpu.VMEM((_LAT_TB, nv, LANES), F32), pltpu.VMEM((_LAT_TB, nv, LANES), F32)],
        compiler_params=_params("arbitrary"),
        name="rwkv_scan_lat",
    )(op4, op4, op4, op4, s0)


def _rwkv_post_kernel(yc_ref, ylf_ref, ylb_ref, bonus_ref, v_ref, g_ref, gn_ref, o_ref):
    def finish(head_sum):
        y = jnp.concatenate([head_sum(h) for h in range(RWKV_HEADS)], axis=1)
        y = y * lax.rsqrt(_seg64_sum_wide(y * y) * (1.0 / RWKV_HS) + EPS) * gn_ref[...]
        o_ref[...] = ((y + bonus_ref[...] * v_ref[...]) * g_ref[...]).astype(o_ref.dtype)

    @pl.when(pl.program_id(0) < N_CTX // SEQ)
    def _():
        finish(lambda h: yc_ref[0, :, h, :] + yc_ref[0, :, RWKV_HEADS + h, :])

    @pl.when(pl.program_id(0) >= N_CTX // SEQ)
    def _():
        finish(lambda h: ylf_ref[0, :, h, :] + ylb_ref[0, :, h, :])


def _rwkv_post(y_ctx, y_lat_f, y_lat_b, bonus, p, v_blk, g, gn):
    tm = SEQ
    w = RWKV_W
    hs = RWKV_HS
    n_ctx = N_CTX // SEQ
    lat = lambda y: y.reshape(N_LAT // SEQ, SEQ, RWKV_HEADS, hs)
    spec = pl.BlockSpec((tm, w), lambda i: (i, 0))
    lat_spec = pl.BlockSpec((1, SEQ, RWKV_HEADS, hs), lambda i: (jnp.maximum(i - n_ctx, 0), 0, 0, 0))
    return pl.pallas_call(
        _rwkv_post_kernel,
        grid=(N_TOK // tm,),
        in_specs=[pl.BlockSpec((1, SEQ, 2 * RWKV_HEADS, hs), lambda i: (jnp.minimum(i, n_ctx - 1), 0, 0, 0)),
                  lat_spec, lat_spec, spec, pl.BlockSpec((tm, w), lambda i: (i, v_blk)), spec,
                  pl.BlockSpec((1, w), lambda i: (0, 0))],
        out_specs=spec,
        out_shape=jax.ShapeDtypeStruct((N_TOK, w), BF16),
        compiler_params=_params("parallel"),
        name="rwkv_post",
    )(y_ctx, lat(y_lat_f), lat(y_lat_b), bonus, p, g, gn)


def _value_split_layout(x):
    lead = x.shape[:-2]
    n = len(lead)
    x = x.reshape(lead + (_LAT_STATES, _LAT_VSPLIT, _LAT_VROWS))
    return jnp.transpose(x, tuple(range(n)) + (n + 2, n + 1, n)).reshape(lead + (_LAT_VROWS, LANES))


_EVEN_HEAD = MLA_Q_RANK + MLA_KV_RANK
_EVEN_PACKED = _EVEN_HEAD + LANES + 2 * RET_HEADS * (RET_DK + RET_DV)


def _pack_even_weight_kernel(w_ref, o_ref):
    w = w_ref[...]
    o_ref[:, 0:_EVEN_HEAD] = w[:, 0:_EVEN_HEAD]
    o_ref[:, _EVEN_HEAD:_EVEN_HEAD + LANES] = jnp.zeros((w.shape[0], LANES), F32)
    o_ref[:, _EVEN_HEAD + MLA_NOPE:_EVEN_HEAD + MLA_QK] = w[:, _EVEN_HEAD:_EVEN_HEAD + MLA_ROPE]
    o_ref[:, _EVEN_HEAD + LANES:] = w[:, _EVEN_HEAD + MLA_ROPE:]


def _pack_even_weight(w_in):
    tm = 256
    return pl.pallas_call(
        _pack_even_weight_kernel,
        grid=(D_MODEL // tm,),
        in_specs=[pl.BlockSpec((tm, w_in.shape[1]), lambda i: (i, 0))],
        out_specs=pl.BlockSpec((tm, _EVEN_PACKED), lambda i: (i, 0)),
        out_shape=jax.ShapeDtypeStruct((D_MODEL, _EVEN_PACKED), F32),
        compiler_params=_params("parallel"),
        name="pack_even_weight",
    )(w_in)


def _even_layer(x, mod, g_mix, w_in, q_norm, kv_norm, w_uq, w_ukv, qn, kn, ret_decay, ret_gn,
                cache_ckv, cache_krope, state_ret, tabs_m):
    w_p = _pack_even_weight(w_in)
    p = _inproj(x, g_mix, mod, w_p, jnp.zeros((1, w_p.shape[1]), F32), 512, None)
    CKV_BLK, KR_BLK, RQ_BLK, RK_BLK, RV_BLK, RG_BLK = 2, 3, 4, 6, 8, 12

    def head_pad(w, n_head, d_head, c0, c1):
        w = w.reshape(w.shape[0], n_head, d_head)[:, :, c0:c1]
        return jnp.pad(w, ((0, 0), (0, 0), (0, LANES - (c1 - c0)))).reshape(w.shape[0], n_head * LANES)

    w_uq_p = head_pad(w_uq, MLA_HEADS, MLA_QK, 0, MLA_QK)
    wk_p = head_pad(w_ukv, MLA_HEADS, MLA_NOPE + MLA_V, 0, MLA_NOPE)
    wv_p = head_pad(w_ukv, MLA_HEADS, MLA_NOPE + MLA_V, MLA_NOPE, MLA_NOPE + MLA_V)
    qn_p = jnp.pad(qn, (0, LANES - MLA_QK))[None]
    kn_p = jnp.pad(kn, (0, LANES - MLA_QK))[None]

    q = _mla_q(p, q_norm[None], w_uq_p, qn_p, tabs_m)
    k, v, ckvn = _mla_kv(p, CKV_BLK, p, KR_BLK, kv_norm[None], wk_p, wv_p, kn_p, tabs_m, N_TOK, True)

    n_c = DEC_BATCH * PAST_LEN
    kr_c = jnp.pad(cache_krope.reshape(n_c, MLA_ROPE), ((0, 0), (MLA_NOPE, LANES - MLA_QK)))
    k_c, v_c, _ = _mla_kv(cache_ckv.reshape(n_c, MLA_KV_RANK), 0, kr_c, 0, kv_norm[None], wk_p, wv_p, kn_p,
                          tabs_m, n_c, False)

    o_ctx = _mla_attn(q, k, v, None, BATCH, SEQ, 0, SEQ, MLA_HEADS // 2)
    o_lat = _mla_attn(q, k, v, (k_c, v_c), DEC_BATCH, DEC_SEQ, N_CTX, 256, 1)

    log_g = -_softplus(-ret_decay)
    gn = ret_gn[None]
    s0_ctx = jnp.zeros((BATCH, 2, RET_HEADS * RET_DK, RET_DV), F32)
    r_ctx, st_ctx = _retention(log_g, p, RQ_BLK, RK_BLK, RV_BLK, RG_BLK, s0_ctx, gn, BATCH, SEQ, 0, SEQ, True)
    s0_lat = state_ret.reshape(DEC_BATCH, 2, RET_HEADS * RET_DK, RET_DV)
    r_lat, _ = _retention(log_g, p, RQ_BLK, RK_BLK, RV_BLK, RG_BLK, s0_lat, gn, DEC_BATCH, DEC_SEQ, N_CTX, 256,
                          False)

    mix = [(o_ctx, o_lat), (r_ctx, r_lat)]
    new_ckv = ckvn[:N_CTX].reshape(BATCH, SEQ, MLA_KV_RANK)
    new_krope = p[:N_CTX, KR_BLK * LANES + MLA_NOPE:KR_BLK * LANES + MLA_QK]
    new_krope = new_krope.reshape(BATCH, SEQ, MLA_ROPE)
    new_ret = st_ctx.reshape(BATCH, 2, RET_HEADS, RET_DK, RET_DV)
    return mix, new_ckv, new_krope, new_ret


def _odd_layer(x, mod, g_mix, w_in, qn, kn, lam, diff_gn, mu, w0, w_up, a0, a_up, g_up, k_k, k_a, r_k, gn,
               cache_k, cache_v, state_rwkv, tabs_d, lam_init):
    w_p = w_in
    n_in = w_p.shape[1]
    mu_full = jnp.concatenate([jnp.zeros((3 * DIFF_W,), F32), mu])[None]
    p = _inproj(x, g_mix, mod, w_p, mu_full, 384, (3 * DIFF_W) // 384)
    DV_BLK, R_BLK, K_BLK, V_BLK = 2, 3, 4, 5
    LO_BLK = (6 * RWKV_W) // (3 * LANES)

    qn_p = jnp.tile(qn, 2)[None]
    kn_p = jnp.tile(kn, 2)[None]
    q, k, dk, dv = _diff_qk(p, qn_p, kn_p, tabs_d)

    n_c = DEC_BATCH * PAST_LEN
    cache = (cache_k.reshape(n_c, DIFF_W), cache_v.reshape(n_c, DIFF_W))
    dgn = diff_gn[None]
    o_ctx = _diff_attn(lam, q, k, p, DV_BLK * DIFF_HEADS, None, dgn, BATCH, SEQ, 0, SEQ, lam_init, DIFF_HEADS)
    o_lat = _diff_attn(lam, q, k, p, DV_BLK * DIFF_HEADS, cache, dgn, DEC_BATCH, DEC_SEQ, N_CTX, 256, lam_init, 1)

    zero = jnp.zeros((RWKV_W_LORA, RWKV_W), F32)
    wup_bd = jnp.concatenate([jnp.concatenate([w_up[0], zero], 1), jnp.concatenate([zero, w_up[1]], 1)], 0)
    aup_bd = jnp.concatenate([jnp.concatenate([a_up[0], zero], 1), jnp.concatenate([zero, a_up[1]], 1)], 0)
    op, g, bonus = _rwkv_pre(p, R_BLK, K_BLK, V_BLK, LO_BLK, wup_bd, aup_bd, g_up, w0.reshape(1, -1),
                             a0.reshape(1, -1), k_k[None], k_a[None], r_k.reshape(1, -1))
    op4 = op.reshape(N_TOK // SEQ, SEQ, 2 * _SCAN_NJ, LANES)
    y_ctx, st_ctx = _rwkv_scan_ctx(op4)
    s0_lat = jnp.transpose(state_rwkv, (4, 1, 0, 2, 3)).reshape(RWKV_HS, _LAT_STATES, RWKV_HS)
    y_lat_f, y_lat_b = _rwkv_scan_lat(op4, _value_split_layout(s0_lat))
    rw_o = _rwkv_post(y_ctx, y_lat_f, y_lat_b, bonus, p, V_BLK, g, gn[None])

    mix = [(o_ctx, o_lat), rw_o]
    new_dk = dk.reshape(BATCH, SEQ, DIFF_HEADS, 2, DIFF_DH)
    new_dv = dv.reshape(BATCH, SEQ, DIFF_HEADS, 2 * DIFF_DH)
    new_rwkv = jnp.transpose(st_ctx.reshape(2, RWKV_HS, RWKV_HS, BATCH, RWKV_HEADS), (3, 0, 4, 2, 1))
    return mix, new_dk, new_dv, new_rwkv


def kernel(x_prompt, x_sample, cache_mla_ckv, cache_mla_krope, state_ret, cache_diff_k, cache_diff_v, state_rwkv,
           c, c_ctx, ada_w, ada_b, norm_mix_g, norm_ffn_g, w_out, ffn_up, ffn_conv_w, ffn_conv_b, ffn_down,
           a_w_in, mla_q_norm, mla_kv_norm, mla_w_uq, mla_w_ukv, mla_qn, mla_kn, ret_decay, ret_gn,
           b_w_in, diff_qn, diff_kn, diff_lam, diff_gn, rwkv_mu, rwkv_w0, rwkv_w_up, rwkv_a0, rwkv_a_up,
           rwkv_g_up, rwkv_k_k, rwkv_k_a, rwkv_r_k, rwkv_gn):
    x = jnp.concatenate([x_prompt.reshape(N_CTX, D_MODEL), x_sample.reshape(N_LAT, D_MODEL)], 0)
    cond8 = jnp.pad(jnp.concatenate([c_ctx[None], c], 0), ((0, 8 - N_GROUPS), (0, 0)))
    mod = _modulation(cond8, ada_w, ada_b)

    tabs_m = _rope_tables(MLA_ROPE, (MLA_NOPE,))
    tabs_d = _rope_tables(DIFF_DH, (0, DIFF_DH))

    outs = {}
    for l in range(DEPTH):
        j = l // 2
        g_mix = norm_mix_g[l][None]
        if l % 2 == 0:
            mix, outs["ckv"], outs["krope"], outs["ret"] = _even_layer(
                x, mod[l], g_mix, a_w_in[j], mla_q_norm[j], mla_kv_norm[j], mla_w_uq[j], mla_w_ukv[j], mla_qn[j],
                mla_kn[j], ret_decay[j], ret_gn[j], cache_mla_ckv[:, j], cache_mla_krope[:, j], state_ret[:, j],
                tabs_m)
        else:
            lam_init = 0.8 - 0.6 * math.exp(-0.3 * l)
            mix, outs["dk"], outs["dv"], outs["rwkv"] = _odd_layer(
                x, mod[l], g_mix, b_w_in[j], diff_qn[j], diff_kn[j], diff_lam[j], diff_gn[j], rwkv_mu[j],
                rwkv_w0[j], rwkv_w_up[j], rwkv_a0[j], rwkv_a_up[j], rwkv_g_up[j], rwkv_k_k[j], rwkv_k_a[j],
                rwkv_r_k[j], rwkv_gn[j], cache_diff_k[:, j], cache_diff_v[:, j], state_rwkv[:, j], tabs_d, lam_init)
        x = _resid_proj(mix, w_out, l, x, mod[l], 2)
        act = _ffn_up(x, norm_ffn_g[l][None], mod[l], ffn_up, l, ffn_conv_w[l], ffn_conv_b[l])
        x = _resid_proj([act], ffn_down, l, x, mod[l], 5, split_out=(l == DEPTH - 1))

    y_prompt = x[0].reshape(BATCH, SEQ, D_MODEL)
    y_sample = x[1].reshape(DEC_BATCH, DEC_SEQ, D_MODEL)
    return (y_prompt, y_sample, outs["ckv"][:, None], outs["krope"][:, None], outs["ret"][:, None],
            outs["dk"][:, None], outs["dv"][:, None], outs["rwkv"][:, None])
```

```python
import functools
import math

import numpy as np
import jax
import jax.numpy as jnp
from jax import lax
from jax.experimental import pallas as pl
from jax.experimental.pallas import tpu as pltpu

D_MODEL = 1024
BATCH = 16
SEQ = 256
DEPTH = 2
DEC_BATCH = 2
DEC_SEQ = 1024
PAST_LEN = 512
GRID_W = 64
EPS = 1e-6
ROPE_BASE = 10000.0

MLA_HEADS = 8
MLA_Q_RANK = 256
MLA_KV_RANK = 128
MLA_NOPE = 64
MLA_ROPE = 32
MLA_V = 64
MLA_QK = MLA_NOPE + MLA_ROPE
RET_HEADS = 4
RET_DK = 64
RET_DV = 128
DIFF_HEADS = 4
DIFF_DH = 64
DIFF_W = DIFF_HEADS * 2 * DIFF_DH
RWKV_HEADS = 8
RWKV_HS = 64
RWKV_W = RWKV_HEADS * RWKV_HS
RWKV_W_LORA = 64
RWKV_A_LORA = 64
RWKV_G_LORA = 128
D_FF = 2816

N_CTX = BATCH * SEQ
N_LAT = DEC_BATCH * DEC_SEQ
N_TOK = N_CTX + N_LAT
N_GROUPS = 1 + DEC_BATCH

LANES = 128
VMEM_LIMIT = 56 * 1024 * 1024

_PREC = lax.Precision.HIGHEST
F32 = jnp.float32


def _dot_tn(a, b):
    return lax.dot_general(a, b, (((0,), (0,)), ((), ())), precision=_PREC, preferred_element_type=F32)


BF16 = jnp.bfloat16


def _dot_bf16(a, b):
    return jnp.dot(a.astype(BF16), b.astype(BF16), preferred_element_type=F32)


def _dot_nt_bf16(a, b):
    return lax.dot_general(a.astype(BF16), b.astype(BF16), (((1,), (1,)), ((), ())), preferred_element_type=F32)


def _params(*sem):
    return pltpu.CompilerParams(dimension_semantics=sem, vmem_limit_bytes=VMEM_LIMIT)


def _sigmoid(x):
    return 1.0 / (1.0 + jnp.exp(-x))


def _silu(x):
    return x * _sigmoid(x)


def _softplus(x):
    return jnp.maximum(x, 0.0) + jnp.log(1.0 + jnp.exp(-jnp.abs(x)))


def _rms(x, n):
    return x * lax.rsqrt(jnp.sum(x * x, axis=-1, keepdims=True) * (1.0 / n) + EPS)


def _lane_lo(shape):
    return lax.broadcasted_iota(jnp.int32, shape, len(shape) - 1) < 64


def _seg64_sum(x):
    lo = _lane_lo(x.shape)
    s_lo = jnp.sum(jnp.where(lo, x, 0.0), axis=-1, keepdims=True)
    s_hi = jnp.sum(jnp.where(lo, 0.0, x), axis=-1, keepdims=True)
    return jnp.where(lo, s_lo, s_hi)


_SUBLANES = 8


def _seq_neighbours(p, tile, tile_rows):
    is_ctx = tile * tile_rows < N_CTX
    sub = lax.broadcasted_iota(jnp.int32, (_SUBLANES, 1), 0)

    def shifted(rolled, edge_sublane, group_of_seq):
        pieces, start = [], 0
        for q in range(tile_rows // SEQ):
            g0 = q * SEQ + group_of_seq
            outer = (q == 0) if group_of_seq == 0 else (q == tile_rows // SEQ - 1)
            edge = (sub == edge_sublane) if outer else ((sub == edge_sublane) & is_ctx)
            pieces += [rolled[start:g0], jnp.where(edge, 0.0, rolled[g0:g0 + _SUBLANES])]
            start = g0 + _SUBLANES
        pieces.append(rolled[start:])
        return jnp.concatenate([x for x in pieces if x.shape[0]], axis=0)

    prev = shifted(pltpu.roll(p, 1, axis=0), 0, 0)
    nxt = shifted(pltpu.roll(p, tile_rows - 1, axis=0), _SUBLANES - 1, SEQ - _SUBLANES)
    return prev, nxt


def _group_of_tile(i, tile_rows):
    row = i * tile_rows
    return jnp.where(row < N_CTX, 0, 1 + (row - N_CTX) // DEC_SEQ)


def _modulation_kernel(c_ref, w_ref, b_ref, o_ref):
    o_ref[0] = _dot_bf16(_silu(c_ref[...]), w_ref[0]) + b_ref[0]


def _modulation(cond8, ada_w, ada_b):
    tn = 512
    n = 6 * D_MODEL
    out = pl.pallas_call(
        _modulation_kernel,
        grid=(DEPTH, n // tn),
        in_specs=[pl.BlockSpec((8, D_MODEL), lambda l, j: (0, 0)),
                  pl.BlockSpec((1, D_MODEL, tn), lambda l, j: (l, 0, j)),
                  pl.BlockSpec((1, 1, tn), lambda l, j: (l, 0, j))],
        out_specs=pl.BlockSpec((1, 8, tn), lambda l, j: (l, 0, j)),
        out_shape=jax.ShapeDtypeStruct((DEPTH, 8, n), F32),
        compiler_params=_params("parallel", "parallel"),
        name="modulation",
    )(cond8, ada_w, ada_b.reshape(DEPTH, 1, n))
    m = out[:, :N_GROUPS].reshape(DEPTH, N_GROUPS, 6, D_MODEL)
    return jnp.pad(m, ((0, 0), (0, 0), (0, 2), (0, 0)))


_TM_SEQ = 1024


def _norm_mod(x, g, mod, off):
    return _rms(x, D_MODEL) * g * (1.0 + mod[off + 1:off + 2, :]) + mod[off:off + 1, :]


def _inproj_kernel(x_ref, g_ref, mod_ref, w_ref, mu_ref, o_ref, h_ref, *, shift_from):
    i = pl.program_id(0)

    @pl.when(pl.program_id(1) == 0)
    def _():
        h_ref[...] = _norm_mod(x_ref[...], g_ref[...], mod_ref[0], 0).astype(BF16)

    p = _dot_bf16(h_ref[...], w_ref[...])
    if shift_from is None:
        o_ref[...] = p
    else:
        @pl.when(pl.program_id(1) < shift_from)
        def _():
            o_ref[...] = p

        @pl.when(pl.program_id(1) >= shift_from)
        def _():
            prev, nxt = _seq_neighbours(p, i, _TM_SEQ)
            o_ref[...] = p + (0.5 * (prev + nxt) - p) * mu_ref[...]


def _inproj(x, g, mod, w, mu, tn, shift_from):
    n = w.shape[1]
    tm = _TM_SEQ
    return pl.pallas_call(
        functools.partial(_inproj_kernel, shift_from=shift_from),
        grid=(N_TOK // tm, n // tn),
        in_specs=[pl.BlockSpec((tm, D_MODEL), lambda i, j: (i, 0)),
                  pl.BlockSpec((1, D_MODEL), lambda i, j: (0, 0)),
                  pl.BlockSpec((1, 8, D_MODEL), lambda i, j: (_group_of_tile(i, tm), 0, 0)),
                  pl.BlockSpec((D_MODEL, tn), lambda i, j: (0, j)),
                  pl.BlockSpec((1, tn), lambda i, j: (0, j))],
        out_specs=pl.BlockSpec((tm, tn), lambda i, j: (i, j)),
        out_shape=jax.ShapeDtypeStruct((N_TOK, n), F32),
        scratch_shapes=[pltpu.VMEM((tm, D_MODEL), BF16)],
        compiler_params=_params("parallel", "arbitrary"),
        name="inproj" if shift_from is None else "inproj_shift",
    )(x, g, mod, w, mu)


def _resid_kernel(*refs, gate_row, widths, split, split_out):
    n_in = sum(2 if sp else 1 for sp in split)
    a_refs = refs[:n_in]
    w_ref, x_ref, mod_ref = refs[n_in:n_in + 3]
    o_ref = refs[n_in + 3:-1] if split_out else refs[n_in + 3]
    a_bf_ref = refs[-1]
    i = pl.program_id(0)

    @pl.when(pl.program_id(1) == 0)
    def _():
        k0, r = 0, 0
        for width, sp in zip(widths, split):
            cols = slice(k0, k0 + width)
            if sp:
                ctx_ref, lat_ref = a_refs[r], a_refs[r + 1]

                @pl.when(i < N_CTX // _TM_SEQ)
                def _():
                    a_bf_ref[:, cols] = ctx_ref[...].astype(BF16)

                @pl.when(i >= N_CTX // _TM_SEQ)
                def _():
                    a_bf_ref[:, cols] = lat_ref[...].astype(BF16)
            else:
                a_bf_ref[:, cols] = a_refs[r][...].astype(BF16)
            k0 += width
            r += 2 if sp else 1

    y = x_ref[...] + mod_ref[0, gate_row:gate_row + 1, :] * _dot_bf16(a_bf_ref[...], w_ref[0])
    if not split_out:
        o_ref[...] = y
    else:
        ctx_o_ref, lat_o_ref = o_ref

        @pl.when(i < N_CTX // _TM_SEQ)
        def _():
            ctx_o_ref[...] = y

        @pl.when(i >= N_CTX // _TM_SEQ)
        def _():
            lat_o_ref[...] = y


def _resid_proj(acts, w, layer, x, mod, gate_row, split_out=False):
    tm, tn = _TM_SEQ, 256
    n_ctx = N_CTX // tm
    nj = D_MODEL // tn
    split = [isinstance(a, (tuple, list)) for a in acts]
    widths = [a[0].shape[1] if sp else a.shape[1] for a, sp in zip(acts, split)]
    k = sum(widths)
    in_specs, operands = [], []
    for a, width, sp in zip(acts, widths, split):
        if sp:
            in_specs += [pl.BlockSpec((tm, width), lambda i, j: (jnp.minimum(i, n_ctx - 1), 0)),
                         pl.BlockSpec((tm, width), lambda i, j: (jnp.maximum(i - n_ctx, 0), 0))]
            operands += list(a)
        else:
            in_specs.append(pl.BlockSpec((tm, width), lambda i, j: (i, 0)))
            operands.append(a)
    if split_out:
        out_specs = [pl.BlockSpec((tm, tn), lambda i, j: (jnp.minimum(i, n_ctx - 1), jnp.where(i < n_ctx, j, nj - 1))),
                     pl.BlockSpec((tm, tn), lambda i, j: (jnp.maximum(i - n_ctx, 0), jnp.where(i < n_ctx, 0, j)))]
        out_shape = [jax.ShapeDtypeStruct((N_CTX, D_MODEL), F32), jax.ShapeDtypeStruct((N_LAT, D_MODEL), F32)]
        sem = ("arbitrary", "arbitrary")
    else:
        out_specs = pl.BlockSpec((tm, tn), lambda i, j: (i, j))
        out_shape = jax.ShapeDtypeStruct((N_TOK, D_MODEL), F32)
        sem = ("parallel", "arbitrary")
    return pl.pallas_call(
        functools.partial(_resid_kernel, gate_row=gate_row, widths=tuple(widths), split=tuple(split),
                          split_out=split_out),
        grid=(N_TOK // tm, nj),
        in_specs=in_specs
        + [pl.BlockSpec((1, k, tn), lambda i, j: (layer, 0, j)),
           pl.BlockSpec((tm, tn), lambda i, j: (i, j)),
           pl.BlockSpec((1, 8, tn), lambda i, j: (_group_of_tile(i, tm), 0, j))],
        out_specs=out_specs,
        out_shape=out_shape,
        scratch_shapes=[pltpu.VMEM((tm, k), BF16)],
        compiler_params=_params(*sem),
        name="resid_proj",
    )(*operands, w, x, mod)


def _ffn_up_kernel(x_ref, g_ref, mod_ref, wa_ref, wb_ref, cwa_ref, cwb_ref, cba_ref, cbb_ref, o_ref, h_ref):
    i = pl.program_id(0)

    @pl.when(pl.program_id(1) == 0)
    def _():
        h_ref[...] = _norm_mod(x_ref[...], g_ref[...], mod_ref[0], 3).astype(BF16)

    h = h_ref[...]

    def conv(w_ref, cw_ref, cb_ref):
        u = _dot_bf16(h, w_ref[0])
        prev, nxt = _seq_neighbours(u, i, _TM_SEQ)
        return prev * cw_ref[0:1, :] + u * cw_ref[1:2, :] + nxt * cw_ref[2:3, :] + cb_ref[...]

    o_ref[...] = (_silu(conv(wa_ref, cwa_ref, cba_ref)) * conv(wb_ref, cwb_ref, cbb_ref)).astype(o_ref.dtype)


def _ffn_up(x, g, mod, up, layer, cw, cb):
    tm, tn = _TM_SEQ, 256
    nb = D_FF // tn
    cb = cb.reshape(1, 2 * D_FF)
    return pl.pallas_call(
        _ffn_up_kernel,
        grid=(N_TOK // tm, nb),
        in_specs=[pl.BlockSpec((tm, D_MODEL), lambda i, j: (i, 0)),
                  pl.BlockSpec((1, D_MODEL), lambda i, j: (0, 0)),
                  pl.BlockSpec((1, 8, D_MODEL), lambda i, j: (_group_of_tile(i, tm), 0, 0)),
                  pl.BlockSpec((1, D_MODEL, tn), lambda i, j: (layer, 0, j)),
                  pl.BlockSpec((1, D_MODEL, tn), lambda i, j: (layer, 0, j + nb)),
                  pl.BlockSpec((3, tn), lambda i, j: (0, j)),
                  pl.BlockSpec((3, tn), lambda i, j: (0, j + nb)),
                  pl.BlockSpec((1, tn), lambda i, j: (0, j)),
                  pl.BlockSpec((1, tn), lambda i, j: (0, j + nb))],
        out_specs=pl.BlockSpec((tm, tn), lambda i, j: (i, j)),
        out_shape=jax.ShapeDtypeStruct((N_TOK, D_FF), BF16),
        scratch_shapes=[pltpu.VMEM((tm, D_MODEL), BF16)],
        compiler_params=_params("parallel", "arbitrary"),
        name="ffn_up",
    )(x, g, mod, up, up, cw, cw, cb, cb)


_ROPE_TM = 512


def _rope(y, c, s1, s2):
    return y * c + pltpu.roll(y, 1, axis=1) * s1 + pltpu.roll(y, LANES - 1, axis=1) * s2


def _rope_tables(rot_dim, lane_offsets):
    t = np.arange(DEC_SEQ)
    row, col = t // GRID_W, t % GRID_W
    n_freq = rot_dim // 4
    inv = ROPE_BASE ** (-np.arange(n_freq, dtype=np.float64) / n_freq)
    ang = np.concatenate([row[:, None] * inv, col[:, None] * inv], -1)
    cos, sin = np.cos(ang), np.sin(ang)
    n = _ROPE_TM + DEC_SEQ
    c, s1, s2 = np.ones((n, LANES)), np.zeros((n, LANES)), np.zeros((n, LANES))
    for a in lane_offsets:
        even = a + 2 * np.arange(rot_dim // 2)
        c[_ROPE_TM:, even] = cos
        c[_ROPE_TM:, even + 1] = cos
        s1[_ROPE_TM:, even + 1] = sin
        s2[_ROPE_TM:, even] = -sin
    return tuple(jnp.asarray(x, F32) for x in (c, s1, s2))


def _store_rotated(dst_ref, sl, y, tabs, rotate):
    if rotate is False:
        dst_ref[:, sl] = y
        return

    @pl.when(rotate)
    def _():
        dst_ref[:, sl] = _rope(y, *tabs)

    @pl.when(jnp.logical_not(rotate))
    def _():
        dst_ref[:, sl] = y


def _latent_tile():
    return pl.program_id(0) >= N_CTX // _ROPE_TM


def _rope_block(i):
    row = i * _ROPE_TM
    return jnp.where(row < N_CTX, 0, 1 + ((row - N_CTX) % DEC_SEQ) // _ROPE_TM)


def _mla_q_kernel(cq_ref, qnorm_ref, w_ref, qn_ref, c_ref, s1_ref, s2_ref, o_ref):
    xn = _rms(cq_ref[...], MLA_Q_RANK) * qnorm_ref[...]
    y = _dot_bf16(xn, w_ref[...])
    tabs = (c_ref[...], s1_ref[...], s2_ref[...])
    for h in range(MLA_HEADS):
        sl = slice(h * LANES, (h + 1) * LANES)
        _store_rotated(o_ref, sl, _rms(y[:, sl], MLA_QK) * qn_ref[...], tabs, _latent_tile())


def _mla_q(p, q_norm, w_uq_p, qn_p, tabs):
    tm = _ROPE_TM
    hw = MLA_HEADS * LANES
    tab_spec = pl.BlockSpec((tm, LANES), lambda i: (_rope_block(i), 0))
    return pl.pallas_call(
        _mla_q_kernel,
        grid=(N_TOK // tm,),
        in_specs=[pl.BlockSpec((tm, MLA_Q_RANK), lambda i: (i, 0)),
                  pl.BlockSpec((1, MLA_Q_RANK), lambda i: (0, 0)),
                  pl.BlockSpec((MLA_Q_RANK, hw), lambda i: (0, 0)),
                  pl.BlockSpec((1, LANES), lambda i: (0, 0)),
                  tab_spec, tab_spec, tab_spec],
        out_specs=pl.BlockSpec((tm, hw), lambda i: (i, 0)),
        out_shape=jax.ShapeDtypeStruct((N_TOK, hw), F32),
        compiler_params=_params("parallel"),
        name="mla_q",
    )(p, q_norm, w_uq_p, qn_p, *tabs)


def _mla_kv_kernel(ckv_ref, kr_ref, kvn_ref, wk_ref, wv_ref, kn_ref, c_ref, s1_ref, s2_ref,
                   k_ref, v_ref, ckvn_ref, *, norm_ckv):
    ckv = ckv_ref[...]
    if norm_ckv:
        ckv = _rms(ckv, MLA_KV_RANK) * kvn_ref[...]
    ckvn_ref[...] = ckv
    ckv_bf = ckv.astype(BF16)
    kk = _dot_bf16(ckv_bf, wk_ref[...])
    v_ref[...] = _dot_bf16(ckv_bf, wv_ref[...])
    kr = kr_ref[...]
    tabs = (c_ref[...], s1_ref[...], s2_ref[...])
    for h in range(MLA_HEADS):
        sl = slice(h * LANES, (h + 1) * LANES)
        _store_rotated(k_ref, sl, _rms(kk[:, sl] + kr, MLA_QK) * kn_ref[...], tabs,
                       _latent_tile() if norm_ckv else False)


def _mla_kv(ckv_src, ckv_blk, kr_src, kr_blk, kv_norm, wk_p, wv_p, kn_p, tabs, n_rows, own_tokens):
    tm = _ROPE_TM
    hw = MLA_HEADS * LANES
    tab_spec = pl.BlockSpec((tm, LANES), (lambda i: (_rope_block(i), 0)) if own_tokens else (lambda i: (0, 0)))
    return pl.pallas_call(
        functools.partial(_mla_kv_kernel, norm_ckv=own_tokens),
        grid=(n_rows // tm,),
        in_specs=[pl.BlockSpec((tm, LANES), lambda i: (i, ckv_blk)),
                  pl.BlockSpec((tm, LANES), lambda i: (i, kr_blk)),
                  pl.BlockSpec((1, LANES), lambda i: (0, 0)),
                  pl.BlockSpec((MLA_KV_RANK, hw), lambda i: (0, 0)),
                  pl.BlockSpec((MLA_KV_RANK, hw), lambda i: (0, 0)),
                  pl.BlockSpec((1, LANES), lambda i: (0, 0)),
                  tab_spec, tab_spec, tab_spec],
        out_specs=[pl.BlockSpec((tm, hw), lambda i: (i, 0)),
                   pl.BlockSpec((tm, hw), lambda i: (i, 0)),
                   pl.BlockSpec((tm, LANES), lambda i: (i, 0))],
        out_shape=[jax.ShapeDtypeStruct((n_rows, hw), F32),
                   jax.ShapeDtypeStruct((n_rows, hw), F32),
                   jax.ShapeDtypeStruct((n_rows, LANES), F32)],
        compiler_params=_params("parallel"),
        name="mla_kv",
    )(ckv_src, kr_src, kv_norm, wk_p, wv_p, kn_p, *tabs)


_LOG2E = math.log2(math.e)


def _softmax_parts(scores, scale):
    m = functools.reduce(jnp.maximum, [jnp.max(sc, axis=-1, keepdims=True) for sc in scores])
    ps = [jnp.exp2((sc - m) * (scale * _LOG2E)) for sc in scores]
    return ps, sum(jnp.sum(p, axis=-1, keepdims=True) for p in ps)


def _attend(q, sources, sl, scale):
    ps, l = _softmax_parts([_dot_nt_bf16(q, k_ref[:, sl]) for k_ref, _ in sources], scale)
    return sum(_dot_bf16(p, v_ref[:, sl]) for p, (_, v_ref) in zip(ps, sources)), l


def _mla_attn_kernel(*refs, pairs):
    q_ref, o_ref = refs[0], refs[-1]
    sources = [refs[1:3]] + ([refs[3:5]] if len(refs) == 6 else [])
    scale = MLA_QK ** -0.5
    for pr in range(pairs):
        outs = []
        for h in range(2):
            sl = slice((2 * pr + h) * LANES, (2 * pr + h + 1) * LANES)
            o, l = _attend(q_ref[:, sl], sources, sl, scale)
            outs.append(o / l)
        o_ref[:, pr * LANES:(pr + 1) * LANES] = (outs[0] + pltpu.roll(outs[1], MLA_V, axis=1)).astype(o_ref.dtype)


def _mla_attn(q, k, v, cache, batch, n, row0, tq, pairs):
    nqb = n // tq
    qb0 = row0 // tq
    kb0 = row0 // n
    wide = 2 * LANES * pairs
    kv_spec = pl.BlockSpec((n, wide), lambda b, h, i: (kb0 + b, h))
    in_specs, operands = [pl.BlockSpec((tq, wide), lambda b, h, i: (qb0 + b * nqb + i, h)), kv_spec, kv_spec], [q, k, v]
    if cache is not None:
        in_specs += [pl.BlockSpec((PAST_LEN, wide), lambda b, h, i: (b, h))] * 2
        operands += list(cache)
    return pl.pallas_call(
        functools.partial(_mla_attn_kernel, pairs=pairs),
        grid=(batch, MLA_HEADS // (2 * pairs), nqb),
        in_specs=in_specs,
        out_specs=pl.BlockSpec((tq, LANES * pairs), lambda b, h, i: (b * nqb + i, h)),
        out_shape=jax.ShapeDtypeStruct((batch * n, MLA_HEADS * MLA_V), BF16),
        compiler_params=_params("parallel", "parallel", "arbitrary"),
        name="mla_attn",
    )(*operands)


def _ret_kernel(lg_ref, q_ref, k_ref, v_ref, rg_ref, s0_ref, gn_ref, *out_and_scratch, n, tq, want_state):
    if want_state:
        o_ref, st_ref, decay_ref = out_and_scratch
    else:
        o_ref, decay_ref = out_and_scratch
    pair, qi, b = pl.program_id(0), pl.program_id(1), pl.program_id(2)
    q = q_ref[...]
    k = k_ref[...] * (RET_DK ** -0.5)
    lo = _lane_lo((1, LANES))
    row = (qi * tq + lax.broadcasted_iota(jnp.int32, (tq, 1), 0)).astype(F32)

    @pl.when(b == 0)
    def _():
        col = lax.broadcasted_iota(jnp.int32, (1, n), 1).astype(F32)
        diff = row - col
        for h in range(2):
            lgf = lg_ref[0, 2 * pair + h]
            lgb = lg_ref[1, 2 * pair + h]
            decay_ref[h] = (jnp.where(diff >= 0, jnp.exp(lgf * jnp.maximum(diff, 0.0)), 0.0)
                            + jnp.where(diff <= 0, jnp.exp(lgb * jnp.maximum(-diff, 0.0)), 0.0))

    for h in range(2):
        lgf = lg_ref[0, 2 * pair + h]
        lgb = lg_ref[1, 2 * pair + h]
        mask = lo if h == 0 else jnp.logical_not(lo)
        qh = jnp.where(mask, q, 0.0)
        vh = v_ref[:, h * LANES:(h + 1) * LANES]
        o = _dot_bf16(_dot_nt_bf16(qh, k) * decay_ref[h], vh)
        o = o + _dot_bf16(qh * jnp.exp(lgf * (row + 1.0)), s0_ref[0, 0])
        o = o + _dot_bf16(qh * jnp.exp(lgb * (n - row)), s0_ref[0, 1])
        y = _rms(o, RET_DV) * gn_ref[:, h * LANES:(h + 1) * LANES]
        o_ref[:, h * LANES:(h + 1) * LANES] = (_silu(rg_ref[:, h * LANES:(h + 1) * LANES]) * y).astype(o_ref.dtype)

    if want_state:
        pos = lax.broadcasted_iota(jnp.int32, (n, 1), 0).astype(F32)
        for d in range(2):
            acc = None
            for h in range(2):
                lg = lg_ref[d, 2 * pair + h]
                mask = lo if h == 0 else jnp.logical_not(lo)
                expo = (n - 1.0 - pos) if d == 0 else pos
                kd = jnp.where(mask, k * jnp.exp(lg * expo), 0.0)
                term = _dot_tn(kd, v_ref[:, h * LANES:(h + 1) * LANES])
                acc = term if acc is None else acc + term
            lg_rows = jnp.where(lax.broadcasted_iota(jnp.int32, (LANES, 1), 0) < 64,
                                lg_ref[d, 2 * pair], lg_ref[d, 2 * pair + 1])
            st_ref[0, d] = acc + s0_ref[0, d] * jnp.exp(lg_rows * n)


def _retention(log_g, p, q_blk, k_blk, v_blk, g_blk, s0, gn, batch, n, row0, tq, want_state):
    nqb = n // tq
    assert not want_state or nqb == 1
    qb0 = row0 // tq
    kb0 = row0 // n
    pairs = RET_HEADS // 2
    out_specs = [pl.BlockSpec((tq, 2 * LANES), lambda h, i, b: (b * nqb + i, h))]
    out_shape = [jax.ShapeDtypeStruct((batch * n, RET_HEADS * RET_DV), BF16)]
    if want_state:
        out_specs.append(pl.BlockSpec((1, 2, LANES, LANES), lambda h, i, b: (b, 0, h, 0)))
        out_shape.append(jax.ShapeDtypeStruct((batch, 2, RET_HEADS * RET_DK, RET_DV), F32))
    outs = pl.pallas_call(
        functools.partial(_ret_kernel, n=n, tq=tq, want_state=want_state),
        grid=(pairs, nqb, batch),
        in_specs=[pl.BlockSpec(memory_space=pltpu.SMEM),
                  pl.BlockSpec((tq, LANES), lambda h, i, b: (qb0 + b * nqb + i, q_blk + h)),
                  pl.BlockSpec((n, LANES), lambda h, i, b: (kb0 + b, k_blk + h)),
                  pl.BlockSpec((n, 2 * LANES), lambda h, i, b: (kb0 + b, v_blk // 2 + h)),
                  pl.BlockSpec((tq, 2 * LANES), lambda h, i, b: (qb0 + b * nqb + i, g_blk // 2 + h)),
                  pl.BlockSpec((1, 2, LANES, LANES), lambda h, i, b: (b, 0, h, 0)),
                  pl.BlockSpec((1, 2 * LANES), lambda h, i, b: (0, h))],
        out_specs=out_specs,
        out_shape=out_shape,
        scratch_shapes=[pltpu.VMEM((2, tq, n), F32)],
        compiler_params=_params("parallel", "parallel", "arbitrary"),
        name="retention",
    )(log_g, p, p, p, p, s0, gn)
    return outs if want_state else (outs[0], None)


def _diff_qk_kernel(q_ref, k_ref, v_ref, qn_ref, kn_ref, c_ref, s1_ref, s2_ref, qo_ref, ko_ref, dk_ref, dv_ref):
    tabs = (c_ref[...], s1_ref[...], s2_ref[...])
    is_ctx = jnp.logical_not(_latent_tile())
    for src, gain, dst in ((q_ref, qn_ref, qo_ref), (k_ref, kn_ref, ko_ref)):
        for h in range(DIFF_HEADS):
            sl = slice(h * LANES, (h + 1) * LANES)
            y = src[:, sl]
            y = y * lax.rsqrt(_seg64_sum(y * y) * (1.0 / DIFF_DH) + EPS) * gain[...]
            _store_rotated(dst, sl, y, tabs, _latent_tile())
            if dst is ko_ref:
                @pl.when(is_ctx)
                def _():
                    dk_ref[:, h, 0, :] = y[:, :DIFF_DH]
                    dk_ref[:, h, 1, :] = pltpu.roll(y, DIFF_DH, axis=1)[:, :DIFF_DH]
                    dv_ref[:, h, :] = v_ref[:, sl]


def _diff_qk(p, qn_p, kn_p, tabs):
    tm = _ROPE_TM
    n_ctx = N_CTX // tm
    tab_spec = pl.BlockSpec((tm, LANES), lambda i: (_rope_block(i), 0))
    ctx_blk = lambda i: jnp.minimum(i, n_ctx - 1)
    return pl.pallas_call(
        _diff_qk_kernel,
        grid=(N_TOK // tm,),
        in_specs=[pl.BlockSpec((tm, DIFF_W), lambda i: (i, 0)),
                  pl.BlockSpec((tm, DIFF_W), lambda i: (i, 1)),
                  pl.BlockSpec((tm, DIFF_W), lambda i: (i, 2)),
                  pl.BlockSpec((1, LANES), lambda i: (0, 0)),
                  pl.BlockSpec((1, LANES), lambda i: (0, 0)),
                  tab_spec, tab_spec, tab_spec],
        out_specs=[pl.BlockSpec((tm, DIFF_W), lambda i: (i, 0)),
                   pl.BlockSpec((tm, DIFF_W), lambda i: (i, 0)),
                   pl.BlockSpec((tm, DIFF_HEADS, 2, DIFF_DH), lambda i: (ctx_blk(i), 0, 0, 0)),
                   pl.BlockSpec((tm, DIFF_HEADS, 2 * DIFF_DH), lambda i: (ctx_blk(i), 0, 0))],
        out_shape=[jax.ShapeDtypeStruct((N_TOK, DIFF_W), F32)] * 2
        + [jax.ShapeDtypeStruct((N_CTX, DIFF_HEADS, 2, DIFF_DH), F32),
           jax.ShapeDtypeStruct((N_CTX, DIFF_HEADS, 2 * DIFF_DH), F32)],
        compiler_params=_params("arbitrary"),
        name="diff_qk",
    )(p, p, p, qn_p, kn_p, *tabs)


def _diff_attn_kernel(*refs, lam_init, heads):
    lam_ref, q_ref, gn_ref, o_ref = refs[0], refs[1], refs[-2], refs[-1]
    sources = [refs[2:4]] + ([refs[4:6]] if len(refs) == 8 else [])
    lv = lam_ref[...]
    lam = (jnp.exp(jnp.sum(lv[0:1] * lv[1:2], axis=-1, keepdims=True))
           - jnp.exp(jnp.sum(lv[2:3] * lv[3:4], axis=-1, keepdims=True)) + lam_init)
    scale = DIFF_DH ** -0.5
    lo = _lane_lo((1, LANES))
    for h in range(heads):
        sl = slice(h * LANES, (h + 1) * LANES)
        q = q_ref[:, sl]
        kbs = [k_ref[:, sl].astype(BF16) for k_ref, _ in sources]
        ps1, l1 = _softmax_parts([_dot_nt_bf16(jnp.where(lo, q, 0.0), kb) for kb in kbs], scale)
        ps2, l2 = _softmax_parts([_dot_nt_bf16(jnp.where(lo, 0.0, q), kb) for kb in kbs], scale)
        o = sum(_dot_bf16(p1 / l1 - lam * (p2 / l2), v_ref[:, sl])
                for p1, p2, (_, v_ref) in zip(ps1, ps2, sources))
        o_ref[:, sl] = (_rms(o, 2 * DIFF_DH) * gn_ref[:, sl] * (1.0 - lam_init)).astype(o_ref.dtype)


def _diff_attn(lam, q, k, v, v_blk0, cache, gn, batch, n, row0, tq, lam_init, heads):
    nqb = n // tq
    qb0 = row0 // tq
    kb0 = row0 // n
    wide = LANES * heads
    vb0 = v_blk0 // heads
    in_specs = [pl.BlockSpec((4, DIFF_DH), lambda b, h, i: (0, 0)),
                pl.BlockSpec((tq, wide), lambda b, h, i: (qb0 + b * nqb + i, h)),
                pl.BlockSpec((n, wide), lambda b, h, i: (kb0 + b, h)),
                pl.BlockSpec((n, wide), lambda b, h, i: (kb0 + b, vb0 + h))]
    operands = [lam, q, k, v]
    if cache is not None:
        in_specs += [pl.BlockSpec((PAST_LEN, wide), lambda b, h, i: (b, h))] * 2
        operands += list(cache)
    return pl.pallas_call(
        functools.partial(_diff_attn_kernel, lam_init=lam_init, heads=heads),
        grid=(batch, DIFF_HEADS // heads, nqb),
        in_specs=in_specs + [pl.BlockSpec((1, wide), lambda b, h, i: (0, h))],
        out_specs=pl.BlockSpec((tq, wide), lambda b, h, i: (b * nqb + i, h)),
        out_shape=jax.ShapeDtypeStruct((batch * n, DIFF_W), BF16),
        compiler_params=_params("parallel", "parallel", "arbitrary"),
        name="diff_attn",
    )(*operands, gn)


def _seg64_sum_wide(x):
    return jnp.concatenate([_seg64_sum(x[:, j * LANES:(j + 1) * LANES]) for j in range(x.shape[1] // LANES)], axis=1)


_SCAN_SLOTS = 3
_SCAN_NJ = _SCAN_SLOTS * RWKV_HEADS


def _rwkv_pre_kernel(r_ref, k_ref, v_ref, lo_ref, wup_ref, aup_ref, gup_ref, w0_ref, a0_ref, kk_ref, ka_ref, rk_ref,
                     op_ref, g_ref, bonus_ref):
    W = RWKV_W
    col = lambda q: slice(q * W, (q + 1) * W)
    r = r_ref[...]
    k = k_ref[...]
    v = v_ref[...]
    lora = lo_ref[...]
    kk = k * kk_ref[...]
    kkn = kk * lax.rsqrt(_seg64_sum_wide(kk * kk) + EPS)
    g_ref[...] = _dot_bf16(_sigmoid(lora[:, 2 * LANES:3 * LANES]), gup_ref[...])
    pre = w0_ref[...] + _dot_bf16(jnp.tanh(lora[:, 0:LANES]), wup_ref[...])
    decay = jnp.exp(-jnp.exp(-_softplus(-pre) - 0.5))
    a = _sigmoid(a0_ref[...] + _dot_bf16(lora[:, LANES:2 * LANES], aup_ref[...]))
    lo = _lane_lo((1, LANES))
    bonus = None
    for d in range(2):
        a_d = a[:, col(d)]
        k_d = k * (1.0 + (a_d - 1.0) * ka_ref[...])
        t = _seg64_sum_wide(r * k_d * rk_ref[...])
        bonus = t if bonus is None else bonus + t
        for s, (x1, x2) in enumerate(((kkn, decay[:, col(d)]), (k_d, kkn * a_d), (r, v))):
            for h in range(RWKV_HEADS):
                blk = slice((h // 2) * LANES, (h // 2 + 1) * LANES)
                if h % 2 == 0:
                    out = jnp.where(lo, x1[:, blk], pltpu.roll(x2[:, blk], RWKV_HS, axis=1))
                else:
                    out = jnp.where(lo, pltpu.roll(x1[:, blk], RWKV_HS, axis=1), x2[:, blk])
                op_ref[:, d * _SCAN_NJ + s * RWKV_HEADS + h, :] = out
    bonus_ref[...] = bonus


def _rwkv_pre(p, r_blk, k_blk, v_blk, lo_blk, wup_bd, aup_bd, gup, w0, a0, k_k, k_a, r_k):
    tm = 256
    w = RWKV_W
    row = lambda n: pl.BlockSpec((1, n), lambda i: (0, 0))
    full = lambda a, b: pl.BlockSpec((a, b), lambda i: (0, 0))
    return pl.pallas_call(
        _rwkv_pre_kernel,
        grid=(N_TOK // tm,),
        in_specs=[pl.BlockSpec((tm, w), lambda i: (i, r_blk)),
                  pl.BlockSpec((tm, w), lambda i: (i, k_blk)),
                  pl.BlockSpec((tm, w), lambda i: (i, v_blk)),
                  pl.BlockSpec((tm, 3 * LANES), lambda i: (i, lo_blk)),
                  full(LANES, 2 * w), full(LANES, 2 * w), full(LANES, w),
                  row(2 * w), row(2 * w), row(w), row(w), row(w)],
        out_specs=[pl.BlockSpec((tm, 2 * _SCAN_NJ, LANES), lambda i: (i, 0, 0)),
                   pl.BlockSpec((tm, w), lambda i: (i, 0)), pl.BlockSpec((tm, w), lambda i: (i, 0))],
        out_shape=[jax.ShapeDtypeStruct((N_TOK, 2 * _SCAN_NJ, LANES), F32),
                   jax.ShapeDtypeStruct((N_TOK, w), F32), jax.ShapeDtypeStruct((N_TOK, w), F32)],
        compiler_params=_params("parallel"),
        name="rwkv_pre",
    )(p, p, p, p, wup_bd, aup_bd, gup, w0, a0, k_k, k_a, r_k)


_SCAN_CHUNK = 32
_SCAN_UNROLL = 16


def _rwkv_first_sa(s_ref, sa_ref, kk, n_k):
    nv = s_ref.shape[1]
    chunk = min(_SCAN_CHUNK, nv)
    for c0 in range(0, nv, chunk):
        def body(k, acc):
            return acc + s_ref[k, c0:c0 + chunk, :] * kk(k)
        sa_ref[c0:c0 + chunk, :] = lax.fori_loop(0, n_k, body, jnp.zeros((chunk, LANES), F32), unroll=_SCAN_UNROLL)


def _rwkv_step(s_ref, sa_ref, kk_next, w, kd, b, r, v_at, n_k):
    nv = s_ref.shape[1]
    chunk = min(_SCAN_CHUNK, nv)
    ys = []
    for c0 in range(0, nv, chunk):
        sa = sa_ref[c0:c0 + chunk, :]
        vc = v_at(c0, chunk)

        def body(k, acc):
            y_acc, sa_acc = acc
            s_new = s_ref[k, c0:c0 + chunk, :] * w(k) - sa * b(k) + vc * kd(k)
            s_ref[k, c0:c0 + chunk, :] = s_new
            return y_acc + s_new * r(k), sa_acc + s_new * kk_next(k)

        zero = jnp.zeros((chunk, LANES), F32)
        y_acc, sa_acc = lax.fori_loop(0, n_k, body, (zero, zero), unroll=_SCAN_UNROLL)
        sa_ref[c0:c0 + chunk, :] = sa_acc
        ys.append(y_acc)
    return ys[0] if len(ys) == 1 else jnp.concatenate(ys, axis=0)


_CTX_TB = 32


def _load_scan_operands(kt_ref, t, slabs):
    for s, x in enumerate(slabs):
        xt = x.T
        kt_ref[t, 2 * s] = xt[:RWKV_HS]
        kt_ref[t, 2 * s + 1] = xt[RWKV_HS:]


def _rwkv_scan_ctx_kernel(x_ref, y_ref, st_ref, s_ref, sa_ref, kt_ref, ys_ref):
    d = pl.program_id(0)
    tb = pl.program_id(1)
    nh = RWKV_HEADS
    step_t = lambda i: jnp.where(d == 0, i, _CTX_TB - 1 - i)

    @pl.when(tb == 0)
    def _():
        s_ref[...] = jnp.zeros_like(s_ref)

    def load_t(t, carry):
        _load_scan_operands(kt_ref, t, [
            jnp.concatenate([x_ref[b, t, s * nh:(s + 1) * nh, :] for b in range(BATCH)], axis=0)
            for s in range(_SCAN_SLOTS)])
        return carry

    lax.fori_loop(0, _CTX_TB, load_t, 0, unroll=4)

    t0 = step_t(0)
    _rwkv_first_sa(s_ref, sa_ref, lambda k: kt_ref[t0, 0, pl.ds(k, 1), :], RWKV_HS)

    def step(i, carry):
        t = step_t(i)
        tn = step_t(jnp.minimum(i + 1, _CTX_TB - 1))
        row = lambda q, tt: (lambda k: kt_ref[tt, q, pl.ds(k, 1), :])
        ys_ref[t] = _rwkv_step(s_ref, sa_ref, row(0, tn), row(1, t), row(2, t), row(3, t), row(4, t),
                               lambda c0, n: kt_ref[t, 5, pl.ds(c0, n), :], RWKV_HS)
        return carry

    lax.fori_loop(0, _CTX_TB, step, 0)

    def store_t(i, carry):
        z = jnp.concatenate([ys_ref[2 * i], ys_ref[2 * i + 1]], axis=0).T
        z_odd = pltpu.roll(z, RWKV_HS, axis=1)
        for b in range(BATCH):
            y_ref[b, 2 * i] = z[b * nh:(b + 1) * nh, :RWKV_HS]
            y_ref[b, 2 * i + 1] = z_odd[b * nh:(b + 1) * nh, :RWKV_HS]
        return carry

    lax.fori_loop(0, _CTX_TB // 2, store_t, 0, unroll=4)

    @pl.when(tb == pl.num_programs(1) - 1)
    def _():
        st_ref[0] = s_ref[...]


def _rwkv_scan_ctx(op4):
    nt = SEQ // _CTX_TB
    hs = RWKV_HS
    tblk = lambda d, tb: jnp.where(d == 0, tb, nt - 1 - tb)
    return pl.pallas_call(
        _rwkv_scan_ctx_kernel,
        grid=(2, nt),
        in_specs=[pl.BlockSpec((BATCH, _CTX_TB, _SCAN_NJ, LANES), lambda d, tb: (0, tblk(d, tb), d, 0))],
        out_specs=[pl.BlockSpec((BATCH, _CTX_TB, RWKV_HEADS, hs), lambda d, tb: (0, tblk(d, tb), d, 0)),
                   pl.BlockSpec((1, hs, hs, LANES), lambda d, tb: (d, 0, 0, 0))],
        out_shape=[jax.ShapeDtypeStruct((BATCH, SEQ, 2 * RWKV_HEADS, hs), F32),
                   jax.ShapeDtypeStruct((2, hs, hs, LANES), F32)],
        scratch_shapes=[pltpu.VMEM((hs, hs, LANES), F32), pltpu.VMEM((hs, LANES), F32),
                        pltpu.VMEM((_CTX_TB, 2 * _SCAN_SLOTS, hs, LANES), F32),
                        pltpu.VMEM((_CTX_TB, hs, LANES), F32)],
        compiler_params=_params("parallel", "arbitrary"),
        name="rwkv_scan_ctx",
    )(op4)


_LAT_TB = 32
_LAT_VSPLIT = 4
_LAT_STATES = 2 * DEC_BATCH * RWKV_HEADS
_LAT_VROWS = RWKV_HS // _LAT_VSPLIT


def _rwkv_scan_lat_kernel(xf0_ref, xf1_ref, xb0_ref, xb1_ref, s0_ref, yf_ref, yb_ref,
                          s_ref, sa_ref, kt_ref, v_ref, ys_ref):
    @pl.when(pl.program_id(0) == 0)
    def _():
        s_ref[...] = s0_ref[...]

    group = lax.broadcasted_iota(jnp.int32, (_LAT_VROWS, LANES), 1) // _LAT_STATES
    nh = RWKV_HEADS

    def load_t(t, carry):
        tr = _LAT_TB - 1 - t
        for s in range(_SCAN_SLOTS):
            heads = slice(s * nh, (s + 1) * nh)
            x = jnp.concatenate([xf0_ref[0, t, heads, :], xf1_ref[0, t, heads, :],
                                 xb0_ref[0, tr, heads, :], xb1_ref[0, tr, heads, :]], axis=0)
            xt = jnp.concatenate([x] * _LAT_VSPLIT, axis=0).T
            kt_ref[t, 2 * s] = xt[:RWKV_HS]
            if s < _SCAN_SLOTS - 1:
                kt_ref[t, 2 * s + 1] = xt[RWKV_HS:]
            else:
                v = jnp.zeros((_LAT_VROWS, LANES), F32)
                for g in range(_LAT_VSPLIT):
                    r0 = RWKV_HS + g * _LAT_VROWS
                    v = jnp.where(group == g, xt[r0:r0 + _LAT_VROWS, :], v)
                v_ref[t] = v
        return carry

    lax.fori_loop(0, _LAT_TB, load_t, 0, unroll=8)

    _rwkv_first_sa(s_ref, sa_ref, lambda k: kt_ref[0, 0, pl.ds(k, 1), :], RWKV_HS)

    def step(t, carry):
        tn = jnp.minimum(t + 1, _LAT_TB - 1)
        row = lambda q, tt: (lambda k: kt_ref[tt, q, pl.ds(k, 1), :])
        ys_ref[t] = _rwkv_step(s_ref, sa_ref, row(0, tn), row(1, t), row(2, t), row(3, t), row(4, t),
                               lambda c0, n: v_ref[t, pl.ds(c0, n), :], RWKV_HS)
        return carry

    lax.fori_loop(0, _LAT_TB, step, 0)

    def store_t(i, carry):
        rows = [jnp.where(group == g, ys_ref[2 * i + j], 0.0) for j in range(2) for g in range(_LAT_VSPLIT)]
        z = jnp.concatenate(rows, axis=0).T
        y = (z[0:_LAT_STATES] + z[_LAT_STATES:2 * _LAT_STATES]
             + z[2 * _LAT_STATES:3 * _LAT_STATES] + z[3 * _LAT_STATES:4 * _LAT_STATES])
        y_odd = pltpu.roll(y, RWKV_HS, axis=1)
        half = _LAT_STATES // 2
        for b in range(DEC_BATCH):
            rows_f = slice(b * nh, (b + 1) * nh)
            rows_b = slice(half + b * nh, half + (b + 1) * nh)
            yf_ref[b, 2 * i] = y[rows_f, :RWKV_HS]
            yf_ref[b, 2 * i + 1] = y_odd[rows_f, :RWKV_HS]
            yb_ref[b, _LAT_TB - 1 - 2 * i] = y[rows_b, :RWKV_HS]
            yb_ref[b, _LAT_TB - 2 - 2 * i] = y_odd[rows_b, :RWKV_HS]
        return carry

    lax.fori_loop(0, _LAT_TB // 2, store_t, 0, unroll=4)


def _rwkv_scan_lat(op4, s0):
    hs = RWKV_HS
    nv = _LAT_VROWS
    nt = DEC_SEQ // _LAT_TB
    per_seq = SEQ // _LAT_TB
    first = N_CTX // SEQ

    def x_spec(b, d):
        tblk = (lambda tb: tb) if d == 0 else (lambda tb: nt - 1 - tb)
        return pl.BlockSpec((1, _LAT_TB, _SCAN_NJ, LANES),
                            lambda tb: (first + b * (DEC_SEQ // SEQ) + tblk(tb) // per_seq, tblk(tb) % per_seq, d, 0))

    y_shape = jax.ShapeDtypeStruct((DEC_BATCH, DEC_SEQ, RWKV_HEADS, hs), F32)
    y_blk = (DEC_BATCH, _LAT_TB, RWKV_HEADS, hs)
    return pl.pallas_call(
        _rwkv_scan_lat_kernel,
        grid=(nt,),
        in_specs=[x_spec(0, 0), x_spec(1, 0), x_spec(0, 1), x_spec(1, 1),
                  pl.BlockSpec((hs, nv, LANES), lambda tb: (0, 0, 0))],
        out_specs=[pl.BlockSpec(y_blk, lambda tb: (0, tb, 0, 0)),
                   pl.BlockSpec(y_blk, lambda tb: (0, nt - 1 - tb, 0, 0))],
        out_shape=[y_shape, y_shape],
        scratch_shapes=[pltpu.VMEM((hs, nv, LANES), F32), pltpu.VMEM((nv, LANES), F32),
                        pltpu.VMEM((_LAT_TB, 2 * _SCAN_SLOTS - 1, hs, LANES), F32),
                        pltpu.VMEM((_LAT_TB, nv, LANES), F32), pltpu.VMEM((_LAT_TB, nv, LANES), F32)],
        compiler_params=_params("arbitrary"),
        name="rwkv_scan_lat",
    )(op4, op4, op4, op4, s0)


def _rwkv_post_kernel(yc_ref, ylf_ref, ylb_ref, bonus_ref, v_ref, g_ref, gn_ref, o_ref):
    def finish(head_sum):
        y = jnp.concatenate([head_sum(h) for h in range(RWKV_HEADS)], axis=1)
        y = y * lax.rsqrt(_seg64_sum_wide(y * y) * (1.0 / RWKV_HS) + EPS) * gn_ref[...]
        o_ref[...] = ((y + bonus_ref[...] * v_ref[...]) * g_ref[...]).astype(o_ref.dtype)

    @pl.when(pl.program_id(0) < N_CTX // SEQ)
    def _():
        finish(lambda h: yc_ref[0, :, h, :] + yc_ref[0, :, RWKV_HEADS + h, :])

    @pl.when(pl.program_id(0) >= N_CTX // SEQ)
    def _():
        finish(lambda h: ylf_ref[0, :, h, :] + ylb_ref[0, :, h, :])


def _rwkv_post(y_ctx, y_lat_f, y_lat_b, bonus, p, v_blk, g, gn):
    tm = SEQ
    w = RWKV_W
    hs = RWKV_HS
    n_ctx = N_CTX // SEQ
    lat = lambda y: y.reshape(N_LAT // SEQ, SEQ, RWKV_HEADS, hs)
    spec = pl.BlockSpec((tm, w), lambda i: (i, 0))
    lat_spec = pl.BlockSpec((1, SEQ, RWKV_HEADS, hs), lambda i: (jnp.maximum(i - n_ctx, 0), 0, 0, 0))
    return pl.pallas_call(
        _rwkv_post_kernel,
        grid=(N_TOK // tm,),
        in_specs=[pl.BlockSpec((1, SEQ, 2 * RWKV_HEADS, hs), lambda i: (jnp.minimum(i, n_ctx - 1), 0, 0, 0)),
                  lat_spec, lat_spec, spec, pl.BlockSpec((tm, w), lambda i: (i, v_blk)), spec,
                  pl.BlockSpec((1, w), lambda i: (0, 0))],
        out_specs=spec,
        out_shape=jax.ShapeDtypeStruct((N_TOK, w), BF16),
        compiler_params=_params("parallel"),
        name="rwkv_post",
    )(y_ctx, lat(y_lat_f), lat(y_lat_b), bonus, p, g, gn)


def _value_split_layout(x):
    lead = x.shape[:-2]
    n = len(lead)
    x = x.reshape(lead + (_LAT_STATES, _LAT_VSPLIT, _LAT_VROWS))
    return jnp.transpose(x, tuple(range(n)) + (n + 2, n + 1, n)).reshape(lead + (_LAT_VROWS, LANES))


_EVEN_HEAD = MLA_Q_RANK + MLA_KV_RANK
_EVEN_PACKED = _EVEN_HEAD + LANES + 2 * RET_HEADS * (RET_DK + RET_DV)


def _pack_even_weight_kernel(w_ref, o_ref):
    w = w_ref[...]
    o_ref[:, 0:_EVEN_HEAD] = w[:, 0:_EVEN_HEAD]
    o_ref[:, _EVEN_HEAD:_EVEN_HEAD + LANES] = jnp.zeros((w.shape[0], LANES), F32)
    o_ref[:, _EVEN_HEAD + MLA_NOPE:_EVEN_HEAD + MLA_QK] = w[:, _EVEN_HEAD:_EVEN_HEAD + MLA_ROPE]
    o_ref[:, _EVEN_HEAD + LANES:] = w[:, _EVEN_HEAD + MLA_ROPE:]


def _pack_even_weight(w_in):
    tm = 256
    return pl.pallas_call(
        _pack_even_weight_kernel,
        grid=(D_MODEL // tm,),
        in_specs=[pl.BlockSpec((tm, w_in.shape[1]), lambda i: (i, 0))],
        out_specs=pl.BlockSpec((tm, _EVEN_PACKED), lambda i: (i, 0)),
        out_shape=jax.ShapeDtypeStruct((D_MODEL, _EVEN_PACKED), F32),
        compiler_params=_params("parallel"),
        name="pack_even_weight",
    )(w_in)


def _even_layer(x, mod, g_mix, w_in, q_norm, kv_norm, w_uq, w_ukv, qn, kn, ret_decay, ret_gn,
                cache_ckv, cache_krope, state_ret, tabs_m):
    w_p = _pack_even_weight(w_in)
    p = _inproj(x, g_mix, mod, w_p, jnp.zeros((1, w_p.shape[1]), F32), 512, None)
    CKV_BLK, KR_BLK, RQ_BLK, RK_BLK, RV_BLK, RG_BLK = 2, 3, 4, 6, 8, 12

    def head_pad(w, n_head, d_head, c0, c1):
        w = w.reshape(w.shape[0], n_head, d_head)[:, :, c0:c1]
        return jnp.pad(w, ((0, 0), (0, 0), (0, LANES - (c1 - c0)))).reshape(w.shape[0], n_head * LANES)

    w_uq_p = head_pad(w_uq, MLA_HEADS, MLA_QK, 0, MLA_QK)
    wk_p = head_pad(w_ukv, MLA_HEADS, MLA_NOPE + MLA_V, 0, MLA_NOPE)
    wv_p = head_pad(w_ukv, MLA_HEADS, MLA_NOPE + MLA_V, MLA_NOPE, MLA_NOPE + MLA_V)
    qn_p = jnp.pad(qn, (0, LANES - MLA_QK))[None]
    kn_p = jnp.pad(kn, (0, LANES - MLA_QK))[None]

    q = _mla_q(p, q_norm[None], w_uq_p, qn_p, tabs_m)
    k, v, ckvn = _mla_kv(p, CKV_BLK, p, KR_BLK, kv_norm[None], wk_p, wv_p, kn_p, tabs_m, N_TOK, True)

    n_c = DEC_BATCH * PAST_LEN
    kr_c = jnp.pad(cache_krope.reshape(n_c, MLA_ROPE), ((0, 0), (MLA_NOPE, LANES - MLA_QK)))
    k_c, v_c, _ = _mla_kv(cache_ckv.reshape(n_c, MLA_KV_RANK), 0, kr_c, 0, kv_norm[None], wk_p, wv_p, kn_p,
                          tabs_m, n_c, False)

    o_ctx = _mla_attn(q, k, v, None, BATCH, SEQ, 0, SEQ, MLA_HEADS // 2)
    o_lat = _mla_attn(q, k, v, (k_c, v_c), DEC_BATCH, DEC_SEQ, N_CTX, 256, 1)

    log_g = -_softplus(-ret_decay)
    gn = ret_gn[None]
    s0_ctx = jnp.zeros((BATCH, 2, RET_HEADS * RET_DK, RET_DV), F32)
    r_ctx, st_ctx = _retention(log_g, p, RQ_BLK, RK_BLK, RV_BLK, RG_BLK, s0_ctx, gn, BATCH, SEQ, 0, SEQ, True)
    s0_lat = state_ret.reshape(DEC_BATCH, 2, RET_HEADS * RET_DK, RET_DV)
    r_lat, _ = _retention(log_g, p, RQ_BLK, RK_BLK, RV_BLK, RG_BLK, s0_lat, gn, DEC_BATCH, DEC_SEQ, N_CTX, 256,
                          False)

    mix = [(o_ctx, o_lat), (r_ctx, r_lat)]
    new_ckv = ckvn[:N_CTX].reshape(BATCH, SEQ, MLA_KV_RANK)
    new_krope = p[:N_CTX, KR_BLK * LANES + MLA_NOPE:KR_BLK * LANES + MLA_QK]
    new_krope = new_krope.reshape(BATCH, SEQ, MLA_ROPE)
    new_ret = st_ctx.reshape(BATCH, 2, RET_HEADS, RET_DK, RET_DV)
    return mix, new_ckv, new_krope, new_ret


def _odd_layer(x, mod, g_mix, w_in, qn, kn, lam, diff_gn, mu, w0, w_up, a0, a_up, g_up, k_k, k_a, r_k, gn,
               cache_k, cache_v, state_rwkv, tabs_d, lam_init):
    w_p = w_in
    n_in = w_p.shape[1]
    mu_full = jnp.concatenate([jnp.zeros((3 * DIFF_W,), F32), mu])[None]
    p = _inproj(x, g_mix, mod, w_p, mu_full, 384, (3 * DIFF_W) // 384)
    DV_BLK, R_BLK, K_BLK, V_BLK = 2, 3, 4, 5
    LO_BLK = (6 * RWKV_W) // (3 * LANES)

    qn_p = jnp.tile(qn, 2)[None]
    kn_p = jnp.tile(kn, 2)[None]
    q, k, dk, dv = _diff_qk(p, qn_p, kn_p, tabs_d)

    n_c = DEC_BATCH * PAST_LEN
    cache = (cache_k.reshape(n_c, DIFF_W), cache_v.reshape(n_c, DIFF_W))
    dgn = diff_gn[None]
    o_ctx = _diff_attn(lam, q, k, p, DV_BLK * DIFF_HEADS, None, dgn, BATCH, SEQ, 0, SEQ, lam_init, DIFF_HEADS)
    o_lat = _diff_attn(lam, q, k, p, DV_BLK * DIFF_HEADS, cache, dgn, DEC_BATCH, DEC_SEQ, N_CTX, 256, lam_init, 1)

    zero = jnp.zeros((RWKV_W_LORA, RWKV_W), F32)
    wup_bd = jnp.concatenate([jnp.concatenate([w_up[0], zero], 1), jnp.concatenate([zero, w_up[1]], 1)], 0)
    aup_bd = jnp.concatenate([jnp.concatenate([a_up[0], zero], 1), jnp.concatenate([zero, a_up[1]], 1)], 0)
    op, g, bonus = _rwkv_pre(p, R_BLK, K_BLK, V_BLK, LO_BLK, wup_bd, aup_bd, g_up, w0.reshape(1, -1),
                             a0.reshape(1, -1), k_k[None], k_a[None], r_k.reshape(1, -1))
    op4 = op.reshape(N_TOK // SEQ, SEQ, 2 * _SCAN_NJ, LANES)
    y_ctx, st_ctx = _rwkv_scan_ctx(op4)
    s0_lat = jnp.transpose(state_rwkv, (4, 1, 0, 2, 3)).reshape(RWKV_HS, _LAT_STATES, RWKV_HS)
    y_lat_f, y_lat_b = _rwkv_scan_lat(op4, _value_split_layout(s0_lat))
    rw_o = _rwkv_post(y_ctx, y_lat_f, y_lat_b, bonus, p, V_BLK, g, gn[None])

    mix = [(o_ctx, o_lat), rw_o]
    new_dk = dk.reshape(BATCH, SEQ, DIFF_HEADS, 2, DIFF_DH)
    new_dv = dv.reshape(BATCH, SEQ, DIFF_HEADS, 2 * DIFF_DH)
    new_rwkv = jnp.transpose(st_ctx.reshape(2, RWKV_HS, RWKV_HS, BATCH, RWKV_HEADS), (3, 0, 4, 2, 1))
    return mix, new_dk, new_dv, new_rwkv


def kernel(x_prompt, x_sample, cache_mla_ckv, cache_mla_krope, state_ret, cache_diff_k, cache_diff_v, state_rwkv,
           c, c_ctx, ada_w, ada_b, norm_mix_g, norm_ffn_g, w_out, ffn_up, ffn_conv_w, ffn_conv_b, ffn_down,
           a_w_in, mla_q_norm, mla_kv_norm, mla_w_uq, mla_w_ukv, mla_qn, mla_kn, ret_decay, ret_gn,
           b_w_in, diff_qn, diff_kn, diff_lam, diff_gn, rwkv_mu, rwkv_w0, rwkv_w_up, rwkv_a0, rwkv_a_up,
           rwkv_g_up, rwkv_k_k, rwkv_k_a, rwkv_r_k, rwkv_gn):
    x = jnp.concatenate([x_prompt.reshape(N_CTX, D_MODEL), x_sample.reshape(N_LAT, D_MODEL)], 0)
    cond8 = jnp.pad(jnp.concatenate([c_ctx[None], c], 0), ((0, 8 - N_GROUPS), (0, 0)))
    mod = _modulation(cond8, ada_w, ada_b)

    tabs_m = _rope_tables(MLA_ROPE, (MLA_NOPE,))
    tabs_d = _rope_tables(DIFF_DH, (0, DIFF_DH))

    outs = {}
    for l in range(DEPTH):
        j = l // 2
        g_mix = norm_mix_g[l][None]
        if l % 2 == 0:
            mix, outs["ckv"], outs["krope"], outs["ret"] = _even_layer(
                x, mod[l], g_mix, a_w_in[j], mla_q_norm[j], mla_kv_norm[j], mla_w_uq[j], mla_w_ukv[j], mla_qn[j],
                mla_kn[j], ret_decay[j], ret_gn[j], cache_mla_ckv[:, j], cache_mla_krope[:, j], state_ret[:, j],
                tabs_m)
        else:
            lam_init = 0.8 - 0.6 * math.exp(-0.3 * l)
            mix, outs["dk"], outs["dv"], outs["rwkv"] = _odd_layer(
                x, mod[l], g_mix, b_w_in[j], diff_qn[j], diff_kn[j], diff_lam[j], diff_gn[j], rwkv_mu[j],
                rwkv_w0[j], rwkv_w_up[j], rwkv_a0[j], rwkv_a_up[j], rwkv_g_up[j], rwkv_k_k[j], rwkv_k_a[j],
                rwkv_r_k[j], rwkv_gn[j], cache_diff_k[:, j], cache_diff_v[:, j], state_rwkv[:, j], tabs_d, lam_init)
        x = _resid_proj(mix, w_out, l, x, mod[l], 2)
        act = _ffn_up(x, norm_ffn_g[l][None], mod[l], ffn_up, l, ffn_conv_w[l], ffn_conv_b[l])
        x = _resid_proj([act], ffn_down, l, x, mod[l], 5, split_out=(l == DEPTH - 1))

    y_prompt = x[0].reshape(BATCH, SEQ, D_MODEL)
    y_sample = x[1].reshape(DEC_BATCH, DEC_SEQ, D_MODEL)
    return (y_prompt, y_sample, outs["ckv"][:, None], outs["krope"][:, None], outs["ret"][:, None],
            outs["dk"][:, None], outs["dv"][:, None], outs["rwkv"][:, None])
```

```python
import functools
import math

import numpy as np
import jax
import jax.numpy as jnp
from jax import lax
from jax.experimental import pallas as pl
from jax.experimental.pallas import tpu as pltpu

D_MODEL = 1024
BATCH = 16
SEQ = 256
DEPTH = 2
DEC_BATCH = 2
DEC_SEQ = 1024
PAST_LEN = 512
GRID_W = 64
EPS = 1e-6
ROPE_BASE = 10000.0

MLA_HEADS = 8
MLA_Q_RANK = 256
MLA_KV_RANK = 128
MLA_NOPE = 64
MLA_ROPE = 32
MLA_V = 64
MLA_QK = MLA_NOPE + MLA_ROPE
RET_HEADS = 4
RET_DK = 64
RET_DV = 128
DIFF_HEADS = 4
DIFF_DH = 64
DIFF_W = DIFF_HEADS * 2 * DIFF_DH
RWKV_HEADS = 8
RWKV_HS = 64
RWKV_W = RWKV_HEADS * RWKV_HS
RWKV_W_LORA = 64
RWKV_A_LORA = 64
RWKV_G_LORA = 128
D_FF = 2816

N_CTX = BATCH * SEQ
N_LAT = DEC_BATCH * DEC_SEQ
N_TOK = N_CTX + N_LAT
N_GROUPS = 1 + DEC_BATCH

LANES = 128
VMEM_LIMIT = 56 * 1024 * 1024

_PREC = lax.Precision.HIGHEST
F32 = jnp.float32


def _dot_tn(a, b):
    return lax.dot_general(a, b, (((0,), (0,)), ((), ())), precision=_PREC, preferred_element_type=F32)


BF16 = jnp.bfloat16


def _dot_bf16(a, b):
    return jnp.dot(a.astype(BF16), b.astype(BF16), preferred_element_type=F32)


def _dot_nt_bf16(a, b):
    return lax.dot_general(a.astype(BF16), b.astype(BF16), (((1,), (1,)), ((), ())), preferred_element_type=F32)


def _params(*sem):
    return pltpu.CompilerParams(dimension_semantics=sem, vmem_limit_bytes=VMEM_LIMIT)


def _sigmoid(x):
    return 1.0 / (1.0 + jnp.exp(-x))


def _silu(x):
    return x * _sigmoid(x)


def _softplus(x):
    return jnp.maximum(x, 0.0) + jnp.log(1.0 + jnp.exp(-jnp.abs(x)))


def _rms(x, n):
    return x * lax.rsqrt(jnp.sum(x * x, axis=-1, keepdims=True) * (1.0 / n) + EPS)


def _lane_lo(shape):
    return lax.broadcasted_iota(jnp.int32, shape, len(shape) - 1) < 64


def _seg64_sum(x):
    lo = _lane_lo(x.shape)
    s_lo = jnp.sum(jnp.where(lo, x, 0.0), axis=-1, keepdims=True)
    s_hi = jnp.sum(jnp.where(lo, 0.0, x), axis=-1, keepdims=True)
    return jnp.where(lo, s_lo, s_hi)


_SUBLANES = 8


def _seq_neighbours(p, tile, tile_rows):
    is_ctx = tile * tile_rows < N_CTX
    sub = lax.broadcasted_iota(jnp.int32, (_SUBLANES, 1), 0)

    def shifted(rolled, edge_sublane, group_of_seq):
        pieces, start = [], 0
        for q in range(tile_rows // SEQ):
            g0 = q * SEQ + group_of_seq
            outer = (q == 0) if group_of_seq == 0 else (q == tile_rows // SEQ - 1)
            edge = (sub == edge_sublane) if outer else ((sub == edge_sublane) & is_ctx)
            pieces += [rolled[start:g0], jnp.where(edge, 0.0, rolled[g0:g0 + _SUBLANES])]
            start = g0 + _SUBLANES
        pieces.append(rolled[start:])
        return jnp.concatenate([x for x in pieces if x.shape[0]], axis=0)

    prev = shifted(pltpu.roll(p, 1, axis=0), 0, 0)
    nxt = shifted(pltpu.roll(p, tile_rows - 1, axis=0), _SUBLANES - 1, SEQ - _SUBLANES)
    return prev, nxt


def _group_of_tile(i, tile_rows):
    row = i * tile_rows
    return jnp.where(row < N_CTX, 0, 1 + (row - N_CTX) // DEC_SEQ)


def _modulation_kernel(c_ref, w_ref, b_ref, o_ref):
    o_ref[0] = _dot_bf16(_silu(c_ref[...]), w_ref[0]) + b_ref[0]


def _modulation(cond8, ada_w, ada_b):
    tn = 512
    n = 6 * D_MODEL
    out = pl.pallas_call(
        _modulation_kernel,
        grid=(DEPTH, n // tn),
        in_specs=[pl.BlockSpec((8, D_MODEL), lambda l, j: (0, 0)),
                  pl.BlockSpec((1, D_MODEL, tn), lambda l, j: (l, 0, j)),
                  pl.BlockSpec((1, 1, tn), lambda l, j: (l, 0, j))],
        out_specs=pl.BlockSpec((1, 8, tn), lambda l, j: (l, 0, j)),
        out_shape=jax.ShapeDtypeStruct((DEPTH, 8, n), F32),
        compiler_params=_params("parallel", "parallel"),
        name="modulation",
    )(cond8, ada_w, ada_b.reshape(DEPTH, 1, n))
    m = out[:, :N_GROUPS].reshape(DEPTH, N_GROUPS, 6, D_MODEL)
    return jnp.pad(m, ((0, 0), (0, 0), (0, 2), (0, 0)))


_TM_SEQ = 1024


def _norm_mod(x, g, mod, off):
    return _rms(x, D_MODEL) * g * (1.0 + mod[off + 1:off + 2, :]) + mod[off:off + 1, :]


def _inproj_kernel(x_ref, g_ref, mod_ref, w_ref, mu_ref, o_ref, h_ref, *, shift_from):
    i = pl.program_id(0)

    @pl.when(pl.program_id(1) == 0)
    def _():
        h_ref[...] = _norm_mod(x_ref[...], g_ref[...], mod_ref[0], 0).astype(BF16)

    p = _dot_bf16(h_ref[...], w_ref[...])
    if shift_from is None:
        o_ref[...] = p
    else:
        @pl.when(pl.program_id(1) < shift_from)
        def _():
            o_ref[...] = p

        @pl.when(pl.program_id(1) >= shift_from)
        def _():
            prev, nxt = _seq_neighbours(p, i, _TM_SEQ)
            o_ref[...] = p + (0.5 * (prev + nxt) - p) * mu_ref[...]


def _inproj(x, g, mod, w, mu, tn, shift_from):
    n = w.shape[1]
    tm = _TM_SEQ
    return pl.pallas_call(
        functools.partial(_inproj_kernel, shift_from=shift_from),
        grid=(N_TOK // tm, n // tn),
        in_specs=[pl.BlockSpec((tm, D_MODEL), lambda i, j: (i, 0)),
                  pl.BlockSpec((1, D_MODEL), lambda i, j: (0, 0)),
                  pl.BlockSpec((1, 8, D_MODEL), lambda i, j: (_group_of_tile(i, tm), 0, 0)),
                  pl.BlockSpec((D_MODEL, tn), lambda i, j: (0, j)),
                  pl.BlockSpec((1, tn), lambda i, j: (0, j))],
        out_specs=pl.BlockSpec((tm, tn), lambda i, j: (i, j)),
        out_shape=jax.ShapeDtypeStruct((N_TOK, n), F32),
        scratch_shapes=[pltpu.VMEM((tm, D_MODEL), BF16)],
        compiler_params=_params("parallel", "arbitrary"),
        name="inproj" if shift_from is None else "inproj_shift",
    )(x, g, mod, w, mu)


def _resid_kernel(*refs, gate_row, widths, split, split_out):
    n_in = sum(2 if sp else 1 for sp in split)
    a_refs = refs[:n_in]
    w_ref, x_ref, mod_ref = refs[n_in:n_in + 3]
    o_ref = refs[n_in + 3:-1] if split_out else refs[n_in + 3]
    a_bf_ref = refs[-1]
    i = pl.program_id(0)

    @pl.when(pl.program_id(1) == 0)
    def _():
        k0, r = 0, 0
        for width, sp in zip(widths, split):
            cols = slice(k0, k0 + width)
            if sp:
                ctx_ref, lat_ref = a_refs[r], a_refs[r + 1]

                @pl.when(i < N_CTX // _TM_SEQ)
                def _():
                    a_bf_ref[:, cols] = ctx_ref[...].astype(BF16)

                @pl.when(i >= N_CTX // _TM_SEQ)
                def _():
                    a_bf_ref[:, cols] = lat_ref[...].astype(BF16)
            else:
                a_bf_ref[:, cols] = a_refs[r][...].astype(BF16)
            k0 += width
            r += 2 if sp else 1

    y = x_ref[...] + mod_ref[0, gate_row:gate_row + 1, :] * _dot_bf16(a_bf_ref[...], w_ref[0])
    if not split_out:
        o_ref[...] = y
    else:
        ctx_o_ref, lat_o_ref = o_ref

        @pl.when(i < N_CTX // _TM_SEQ)
        def _():
            ctx_o_ref[...] = y

        @pl.when(i >= N_CTX // _TM_SEQ)
        def _():
            lat_o_ref[...] = y


def _resid_proj(acts, w, layer, x, mod, gate_row, split_out=False):
    tm, tn = _TM_SEQ, 256
    n_ctx = N_CTX // tm
    nj = D_MODEL // tn
    split = [isinstance(a, (tuple, list)) for a in acts]
    widths = [a[0].shape[1] if sp else a.shape[1] for a, sp in zip(acts, split)]
    k = sum(widths)
    in_specs, operands = [], []
    for a, width, sp in zip(acts, widths, split):
        if sp:
            in_specs += [pl.BlockSpec((tm, width), lambda i, j: (jnp.minimum(i, n_ctx - 1), 0)),
                         pl.BlockSpec((tm, width), lambda i, j: (jnp.maximum(i - n_ctx, 0), 0))]
            operands += list(a)
        else:
            in_specs.append(pl.BlockSpec((tm, width), lambda i, j: (i, 0)))
            operands.append(a)
    if split_out:
        out_specs = [pl.BlockSpec((tm, tn), lambda i, j: (jnp.minimum(i, n_ctx - 1), jnp.where(i < n_ctx, j, nj - 1))),
                     pl.BlockSpec((tm, tn), lambda i, j: (jnp.maximum(i - n_ctx, 0), jnp.where(i < n_ctx, 0, j)))]
        out_shape = [jax.ShapeDtypeStruct((N_CTX, D_MODEL), F32), jax.ShapeDtypeStruct((N_LAT, D_MODEL), F32)]
        sem = ("arbitrary", "arbitrary")
    else:
        out_specs = pl.BlockSpec((tm, tn), lambda i, j: (i, j))
        out_shape = jax.ShapeDtypeStruct((N_TOK, D_MODEL), F32)
        sem = ("parallel", "arbitrary")
    return pl.pallas_call(
        functools.partial(_resid_kernel, gate_row=gate_row, widths=tuple(widths), split=tuple(split),
                          split_out=split_out),
        grid=(N_TOK // tm, nj),
        in_specs=in_specs
        + [pl.BlockSpec((1, k, tn), lambda i, j: (layer, 0, j)),
           pl.BlockSpec((tm, tn), lambda i, j: (i, j)),
           pl.BlockSpec((1, 8, tn), lambda i, j: (_group_of_tile(i, tm), 0, j))],
        out_specs=out_specs,
        out_shape=out_shape,
        scratch_shapes=[pltpu.VMEM((tm, k), BF16)],
        compiler_params=_params(*sem),
        name="resid_proj",
    )(*operands, w, x, mod)


def _ffn_up_kernel(x_ref, g_ref, mod_ref, wa_ref, wb_ref, cwa_ref, cwb_ref, cba_ref, cbb_ref, o_ref, h_ref):
    i = pl.program_id(0)

    @pl.when(pl.program_id(1) == 0)
    def _():
        h_ref[...] = _norm_mod(x_ref[...], g_ref[...], mod_ref[0], 3).astype(BF16)

    h = h_ref[...]

    def conv(w_ref, cw_ref, cb_ref):
        u = _dot_bf16(h, w_ref[0])
        prev, nxt = _seq_neighbours(u, i, _TM_SEQ)
        return prev * cw_ref[0:1, :] + u * cw_ref[1:2, :] + nxt * cw_ref[2:3, :] + cb_ref[...]

    o_ref[...] = (_silu(conv(wa_ref, cwa_ref, cba_ref)) * conv(wb_ref, cwb_ref, cbb_ref)).astype(o_ref.dtype)


def _ffn_up(x, g, mod, up, layer, cw, cb):
    tm, tn = _TM_SEQ, 256
    nb = D_FF // tn
    cb = cb.reshape(1, 2 * D_FF)
    return pl.pallas_call(
        _ffn_up_kernel,
        grid=(N_TOK // tm, nb),
        in_specs=[pl.BlockSpec((tm, D_MODEL), lambda i, j: (i, 0)),
                  pl.BlockSpec((1, D_MODEL), lambda i, j: (0, 0)),
                  pl.BlockSpec((1, 8, D_MODEL), lambda i, j: (_group_of_tile(i, tm), 0, 0)),
                  pl.BlockSpec((1, D_MODEL, tn), lambda i, j: (layer, 0, j)),
                  pl.BlockSpec((1, D_MODEL, tn), lambda i, j: (layer, 0, j + nb)),
                  pl.BlockSpec((3, tn), lambda i, j: (0, j)),
                  pl.BlockSpec((3, tn), lambda i, j: (0, j + nb)),
                  pl.BlockSpec((1, tn), lambda i, j: (0, j)),
                  pl.BlockSpec((1, tn), lambda i, j: (0, j + nb))],
        out_specs=pl.BlockSpec((tm, tn), lambda i, j: (i, j)),
        out_shape=jax.ShapeDtypeStruct((N_TOK, D_FF), BF16),
        scratch_shapes=[pltpu.VMEM((tm, D_MODEL), BF16)],
        compiler_params=_params("parallel", "arbitrary"),
        name="ffn_up",
    )(x, g, mod, up, up, cw, cw, cb, cb)


_ROPE_TM = 512


def _rope(y, c, s1, s2):
    return y * c + pltpu.roll(y, 1, axis=1) * s1 + pltpu.roll(y, LANES - 1, axis=1) * s2


def _rope_tables(rot_dim, lane_offsets):
    t = np.arange(DEC_SEQ)
    row, col = t // GRID_W, t % GRID_W
    n_freq = rot_dim // 4
    inv = ROPE_BASE ** (-np.arange(n_freq, dtype=np.float64) / n_freq)
    ang = np.concatenate([row[:, None] * inv, col[:, None] * inv], -1)
    cos, sin = np.cos(ang), np.sin(ang)
    n = _ROPE_TM + DEC_SEQ
    c, s1, s2 = np.ones((n, LANES)), np.zeros((n, LANES)), np.zeros((n, LANES))
    for a in lane_offsets:
        even = a + 2 * np.arange(rot_dim // 2)
        c[_ROPE_TM:, even] = cos
        c[_ROPE_TM:, even + 1] = cos
        s1[_ROPE_TM:, even + 1] = sin
        s2[_ROPE_TM:, even] = -sin
    return tuple(jnp.asarray(x, F32) for x in (c, s1, s2))


def _store_rotated(dst_ref, sl, y, tabs, rotate):
    if rotate is False:
        dst_ref[:, sl] = y
        return

    @pl.when(rotate)
    def _():
        dst_ref[:, sl] = _rope(y, *tabs)

    @pl.when(jnp.logical_not(rotate))
    def _():
        dst_ref[:, sl] = y


def _latent_tile():
    return pl.program_id(0) >= N_CTX // _ROPE_TM


def _rope_block(i):
    row = i * _ROPE_TM
    return jnp.where(row < N_CTX, 0, 1 + ((row - N_CTX) % DEC_SEQ) // _ROPE_TM)


def _mla_q_kernel(cq_ref, qnorm_ref, w_ref, qn_ref, c_ref, s1_ref, s2_ref, o_ref):
    xn = _rms(cq_ref[...], MLA_Q_RANK) * qnorm_ref[...]
    y = _dot_bf16(xn, w_ref[...])
    tabs = (c_ref[...], s1_ref[...], s2_ref[...])
    for h in range(MLA_HEADS):
        sl = slice(h * LANES, (h + 1) * LANES)
        _store_rotated(o_ref, sl, _rms(y[:, sl], MLA_QK) * qn_ref[...], tabs, _latent_tile())


def _mla_q(p, q_norm, w_uq_p, qn_p, tabs):
    tm = _ROPE_TM
    hw = MLA_HEADS * LANES
    tab_spec = pl.BlockSpec((tm, LANES), lambda i: (_rope_block(i), 0))
    return pl.pallas_call(
        _mla_q_kernel,
        grid=(N_TOK // tm,),
        in_specs=[pl.BlockSpec((tm, MLA_Q_RANK), lambda i: (i, 0)),
                  pl.BlockSpec((1, MLA_Q_RANK), lambda i: (0, 0)),
                  pl.BlockSpec((MLA_Q_RANK, hw), lambda i: (0, 0)),
                  pl.BlockSpec((1, LANES), lambda i: (0, 0)),
                  tab_spec, tab_spec, tab_spec],
        out_specs=pl.BlockSpec((tm, hw), lambda i: (i, 0)),
        out_shape=jax.ShapeDtypeStruct((N_TOK, hw), F32),
        compiler_params=_params("parallel"),
        name="mla_q",
    )(p, q_norm, w_uq_p, qn_p, *tabs)


def _mla_kv_kernel(ckv_ref, kr_ref, kvn_ref, wk_ref, wv_ref, kn_ref, c_ref, s1_ref, s2_ref,
                   k_ref, v_ref, ckvn_ref, *, norm_ckv):
    ckv = ckv_ref[...]
    if norm_ckv:
        ckv = _rms(ckv, MLA_KV_RANK) * kvn_ref[...]
    ckvn_ref[...] = ckv
    ckv_bf = ckv.astype(BF16)
    kk = _dot_bf16(ckv_bf, wk_ref[...])
    v_ref[...] = _dot_bf16(ckv_bf, wv_ref[...])
    kr = kr_ref[...]
    tabs = (c_ref[...], s1_ref[...], s2_ref[...])
    for h in range(MLA_HEADS):
        sl = slice(h * LANES, (h + 1) * LANES)
        _store_rotated(k_ref, sl, _rms(kk[:, sl] + kr, MLA_QK) * kn_ref[...], tabs,
                       _latent_tile() if norm_ckv else False)


def _mla_kv(ckv_src, ckv_blk, kr_src, kr_blk, kv_norm, wk_p, wv_p, kn_p, tabs, n_rows, own_tokens):
    tm = _ROPE_TM
    hw = MLA_HEADS * LANES
    tab_spec = pl.BlockSpec((tm, LANES), (lambda i: (_rope_block(i), 0)) if own_tokens else (lambda i: (0, 0)))
    return pl.pallas_call(
        functools.partial(_mla_kv_kernel, norm_ckv=own_tokens),
        grid=(n_rows // tm,),
        in_specs=[pl.BlockSpec((tm, LANES), lambda i: (i, ckv_blk)),
                  pl.BlockSpec((tm, LANES), lambda i: (i, kr_blk)),
                  pl.BlockSpec((1, LANES), lambda i: (0, 0)),
                  pl.BlockSpec((MLA_KV_RANK, hw), lambda i: (0, 0)),
                  pl.BlockSpec((MLA_KV_RANK, hw), lambda i: (0, 0)),
                  pl.BlockSpec((1, LANES), lambda i: (0, 0)),
                  tab_spec, tab_spec, tab_spec],
        out_specs=[pl.BlockSpec((tm, hw), lambda i: (i, 0)),
                   pl.BlockSpec((tm, hw), lambda i: (i, 0)),
                   pl.BlockSpec((tm, LANES), lambda i: (i, 0))],
        out_shape=[jax.ShapeDtypeStruct((n_rows, hw), F32),
                   jax.ShapeDtypeStruct((n_rows, hw), F32),
                   jax.ShapeDtypeStruct((n_rows, LANES), F32)],
        compiler_params=_params("parallel"),
        name="mla_kv",
    )(ckv_src, kr_src, kv_norm, wk_p, wv_p, kn_p, *tabs)


_LOG2E = math.log2(math.e)


def _softmax_parts(scores, scale):
    m = functools.reduce(jnp.maximum, [jnp.max(sc, axis=-1, keepdims=True) for sc in scores])
    ps = [jnp.exp2((sc - m) * (scale * _LOG2E)) for sc in scores]
    return ps, sum(jnp.sum(p, axis=-1, keepdims=True) for p in ps)


def _attend(q, sources, sl, scale):
    ps, l = _softmax_parts([_dot_nt_bf16(q, k_ref[:, sl]) for k_ref, _ in sources], scale)
    return sum(_dot_bf16(p, v_ref[:, sl]) for p, (_, v_ref) in zip(ps, sources)), l


def _mla_attn_kernel(*refs, pairs):
    q_ref, o_ref = refs[0], refs[-1]
    sources = [refs[1:3]] + ([refs[3:5]] if len(refs) == 6 else [])
    scale = MLA_QK ** -0.5
    for pr in range(pairs):
        outs = []
        for h in range(2):
            sl = slice((2 * pr + h) * LANES, (2 * pr + h + 1) * LANES)
            o, l = _attend(q_ref[:, sl], sources, sl, scale)
            outs.append(o / l)
        o_ref[:, pr * LANES:(pr + 1) * LANES] = (outs[0] + pltpu.roll(outs[1], MLA_V, axis=1)).astype(o_ref.dtype)


def _mla_attn(q, k, v, cache, batch, n, row0, tq, pairs):
    nqb = n // tq
    qb0 = row0 // tq
    kb0 = row0 // n
    wide = 2 * LANES * pairs
    kv_spec = pl.BlockSpec((n, wide), lambda b, h, i: (kb0 + b, h))
    in_specs, operands = [pl.BlockSpec((tq, wide), lambda b, h, i: (qb0 + b * nqb + i, h)), kv_spec, kv_spec], [q, k, v]
    if cache is not None:
        in_specs += [pl.BlockSpec((PAST_LEN, wide), lambda b, h, i: (b, h))] * 2
        operands += list(cache)
    return pl.pallas_call(
        functools.partial(_mla_attn_kernel, pairs=pairs),
        grid=(batch, MLA_HEADS // (2 * pairs), nqb),
        in_specs=in_specs,
        out_specs=pl.BlockSpec((tq, LANES * pairs), lambda b, h, i: (b * nqb + i, h)),
        out_shape=jax.ShapeDtypeStruct((batch * n, MLA_HEADS * MLA_V), BF16),
        compiler_params=_params("parallel", "parallel", "arbitrary"),
        name="mla_attn",
    )(*operands)


def _ret_kernel(lg_ref, q_ref, k_ref, v_ref, rg_ref, s0_ref, gn_ref, *out_and_scratch, n, tq, want_state):
    if want_state:
        o_ref, st_ref, decay_ref = out_and_scratch
    else:
        o_ref, decay_ref = out_and_scratch
    pair, qi, b = pl.program_id(0), pl.program_id(1), pl.program_id(2)
    q = q_ref[...]
    k = k_ref[...] * (RET_DK ** -0.5)
    lo = _lane_lo((1, LANES))
    row = (qi * tq + lax.broadcasted_iota(jnp.int32, (tq, 1), 0)).astype(F32)

    @pl.when(b == 0)
    def _():
        col = lax.broadcasted_iota(jnp.int32, (1, n), 1).astype(F32)
        diff = row - col
        for h in range(2):
            lgf = lg_ref[0, 2 * pair + h]
            lgb = lg_ref[1, 2 * pair + h]
            decay_ref[h] = (jnp.where(diff >= 0, jnp.exp(lgf * jnp.maximum(diff, 0.0)), 0.0)
                            + jnp.where(diff <= 0, jnp.exp(lgb * jnp.maximum(-diff, 0.0)), 0.0))

    for h in range(2):
        lgf = lg_ref[0, 2 * pair + h]
        lgb = lg_ref[1, 2 * pair + h]
        mask = lo if h == 0 else jnp.logical_not(lo)
        qh = jnp.where(mask, q, 0.0)
        vh = v_ref[:, h * LANES:(h + 1) * LANES]
        o = _dot_bf16(_dot_nt_bf16(qh, k) * decay_ref[h], vh)
        o = o + _dot_bf16(qh * jnp.exp(lgf * (row + 1.0)), s0_ref[0, 0])
        o = o + _dot_bf16(qh * jnp.exp(lgb * (n - row)), s0_ref[0, 1])
        y = _rms(o, RET_DV) * gn_ref[:, h * LANES:(h + 1) * LANES]
        o_ref[:, h * LANES:(h + 1) * LANES] = (_silu(rg_ref[:, h * LANES:(h + 1) * LANES]) * y).astype(o_ref.dtype)

    if want_state:
        pos = lax.broadcasted_iota(jnp.int32, (n, 1), 0).astype(F32)
        for d in range(2):
            acc = None
            for h in range(2):
                lg = lg_ref[d, 2 * pair + h]
                mask = lo if h == 0 else jnp.logical_not(lo)
                expo = (n - 1.0 - pos) if d == 0 else pos
                kd = jnp.where(mask, k * jnp.exp(lg * expo), 0.0)
                term = _dot_tn(kd, v_ref[:, h * LANES:(h + 1) * LANES])
                acc = term if acc is None else acc + term
            lg_rows = jnp.where(lax.broadcasted_iota(jnp.int32, (LANES, 1), 0) < 64,
                                lg_ref[d, 2 * pair], lg_ref[d, 2 * pair + 1])
            st_ref[0, d] = acc + s0_ref[0, d] * jnp.exp(lg_rows * n)


def _retention(log_g, p, q_blk, k_blk, v_blk, g_blk, s0, gn, batch, n, row0, tq, want_state):
    nqb = n // tq
    assert not want_state or nqb == 1
    qb0 = row0 // tq
    kb0 = row0 // n
    pairs = RET_HEADS // 2
    out_specs = [pl.BlockSpec((tq, 2 * LANES), lambda h, i, b: (b * nqb + i, h))]
    out_shape = [jax.ShapeDtypeStruct((batch * n, RET_HEADS * RET_DV), BF16)]
    if want_state:
        out_specs.append(pl.BlockSpec((1, 2, LANES, LANES), lambda h, i, b: (b, 0, h, 0)))
        out_shape.append(jax.ShapeDtypeStruct((batch, 2, RET_HEADS * RET_DK, RET_DV), F32))
    outs = pl.pallas_call(
        functools.partial(_ret_kernel, n=n, tq=tq, want_state=want_state),
        grid=(pairs, nqb, batch),
        in_specs=[pl.BlockSpec(memory_space=pltpu.SMEM),
                  pl.BlockSpec((tq, LANES), lambda h, i, b: (qb0 + b * nqb + i, q_blk + h)),
                  pl.BlockSpec((n, LANES), lambda h, i, b: (kb0 + b, k_blk + h)),
                  pl.BlockSpec((n, 2 * LANES), lambda h, i, b: (kb0 + b, v_blk // 2 + h)),
                  pl.BlockSpec((tq, 2 * LANES), lambda h, i, b: (qb0 + b * nqb + i, g_blk // 2 + h)),
                  pl.BlockSpec((1, 2, LANES, LANES), lambda h, i, b: (b, 0, h, 0)),
                  pl.BlockSpec((1, 2 * LANES), lambda h, i, b: (0, h))],
        out_specs=out_specs,
        out_shape=out_shape,
        scratch_shapes=[pltpu.VMEM((2, tq, n), F32)],
        compiler_params=_params("parallel", "parallel", "arbitrary"),
        name="retention",
    )(log_g, p, p, p, p, s0, gn)
    return outs if want_state else (outs[0], None)


def _diff_qk_kernel(q_ref, k_ref, v_ref, qn_ref, kn_ref, c_ref, s1_ref, s2_ref, qo_ref, ko_ref, dk_ref, dv_ref):
    tabs = (c_ref[...], s1_ref[...], s2_ref[...])
    is_ctx = jnp.logical_not(_latent_tile())
    for src, gain, dst in ((q_ref, qn_ref, qo_ref), (k_ref, kn_ref, ko_ref)):
        for h in range(DIFF_HEADS):
            sl = slice(h * LANES, (h + 1) * LANES)
            y = src[:, sl]
            y = y * lax.rsqrt(_seg64_sum(y * y) * (1.0 / DIFF_DH) + EPS) * gain[...]
            _store_rotated(dst, sl, y, tabs, _latent_tile())
            if dst is ko_ref:
                @pl.when(is_ctx)
                def _():
                    dk_ref[:, h, 0, :] = y[:, :DIFF_DH]
                    dk_ref[:, h, 1, :] = pltpu.roll(y, DIFF_DH, axis=1)[:, :DIFF_DH]
                    dv_ref[:, h, :] = v_ref[:, sl]


def _diff_qk(p, qn_p, kn_p, tabs):
    tm = _ROPE_TM
    n_ctx = N_CTX // tm
    tab_spec = pl.BlockSpec((tm, LANES), lambda i: (_rope_block(i), 0))
    ctx_blk = lambda i: jnp.minimum(i, n_ctx - 1)
    return pl.pallas_call(
        _diff_qk_kernel,
        grid=(N_TOK // tm,),
        in_specs=[pl.BlockSpec((tm, DIFF_W), lambda i: (i, 0)),
                  pl.BlockSpec((tm, DIFF_W), lambda i: (i, 1)),
                  pl.BlockSpec((tm, DIFF_W), lambda i: (i, 2)),
                  pl.BlockSpec((1, LANES), lambda i: (0, 0)),
                  pl.BlockSpec((1, LANES), lambda i: (0, 0)),
                  tab_spec, tab_spec, tab_spec],
        out_specs=[pl.BlockSpec((tm, DIFF_W), lambda i: (i, 0)),
                   pl.BlockSpec((tm, DIFF_W), lambda i: (i, 0)),
                   pl.BlockSpec((tm, DIFF_HEADS, 2, DIFF_DH), lambda i: (ctx_blk(i), 0, 0, 0)),
                   pl.BlockSpec((tm, DIFF_HEADS, 2 * DIFF_DH), lambda i: (ctx_blk(i), 0, 0))],
        out_shape=[jax.ShapeDtypeStruct((N_TOK, DIFF_W), F32)] * 2
        + [jax.ShapeDtypeStruct((N_CTX, DIFF_HEADS, 2, DIFF_DH), F32),
           jax.ShapeDtypeStruct((N_CTX, DIFF_HEADS, 2 * DIFF_DH), F32)],
        compiler_params=_params("arbitrary"),
        name="diff_qk",
    )(p, p, p, qn_p, kn_p, *tabs)


def _diff_attn_kernel(*refs, lam_init, heads):
    lam_ref, q_ref, gn_ref, o_ref = refs[0], refs[1], refs[-2], refs[-1]
    sources = [refs[2:4]] + ([refs[4:6]] if len(refs) == 8 else [])
    lv = lam_ref[...]
    lam = (jnp.exp(jnp.sum(lv[0:1] * lv[1:2], axis=-1, keepdims=True))
           - jnp.exp(jnp.sum(lv[2:3] * lv[3:4], axis=-1, keepdims=True)) + lam_init)
    scale = DIFF_DH ** -0.5
    lo = _lane_lo((1, LANES))
    for h in range(heads):
        sl = slice(h * LANES, (h + 1) * LANES)
        q = q_ref[:, sl]
        kbs = [k_ref[:, sl].astype(BF16) for k_ref, _ in sources]
        ps1, l1 = _softmax_parts([_dot_nt_bf16(jnp.where(lo, q, 0.0), kb) for kb in kbs], scale)
        ps2, l2 = _softmax_parts([_dot_nt_bf16(jnp.where(lo, 0.0, q), kb) for kb in kbs], scale)
        o = sum(_dot_bf16(p1 / l1 - lam * (p2 / l2), v_ref[:, sl])
                for p1, p2, (_, v_ref) in zip(ps1, ps2, sources))
        o_ref[:, sl] = (_rms(o, 2 * DIFF_DH) * gn_ref[:, sl] * (1.0 - lam_init)).astype(o_ref.dtype)


def _diff_attn(lam, q, k, v, v_blk0, cache, gn, batch, n, row0, tq, lam_init, heads):
    nqb = n // tq
    qb0 = row0 // tq
    kb0 = row0 // n
    wide = LANES * heads
    vb0 = v_blk0 // heads
    in_specs = [pl.BlockSpec((4, DIFF_DH), lambda b, h, i: (0, 0)),
                pl.BlockSpec((tq, wide), lambda b, h, i: (qb0 + b * nqb + i, h)),
                pl.BlockSpec((n, wide), lambda b, h, i: (kb0 + b, h)),
                pl.BlockSpec((n, wide), lambda b, h, i: (kb0 + b, vb0 + h))]
    operands = [lam, q, k, v]
    if cache is not None:
        in_specs += [pl.BlockSpec((PAST_LEN, wide), lambda b, h, i: (b, h))] * 2
        operands += list(cache)
    return pl.pallas_call(
        functools.partial(_diff_attn_kernel, lam_init=lam_init, heads=heads),
        grid=(batch, DIFF_HEADS // heads, nqb),
        in_specs=in_specs + [pl.BlockSpec((1, wide), lambda b, h, i: (0, h))],
        out_specs=pl.BlockSpec((tq, wide), lambda b, h, i: (b * nqb + i, h)),
        out_shape=jax.ShapeDtypeStruct((batch * n, DIFF_W), BF16),
        compiler_params=_params("parallel", "parallel", "arbitrary"),
        name="diff_attn",
    )(*operands, gn)


def _seg64_sum_wide(x):
    return jnp.concatenate([_seg64_sum(x[:, j * LANES:(j + 1) * LANES]) for j in range(x.shape[1] // LANES)], axis=1)


_SCAN_SLOTS = 3
_SCAN_NJ = _SCAN_SLOTS * RWKV_HEADS


def _rwkv_pre_kernel(r_ref, k_ref, v_ref, lo_ref, wup_ref, aup_ref, gup_ref, w0_ref, a0_ref, kk_ref, ka_ref, rk_ref,
                     op_ref, g_ref, bonus_ref):
    W = RWKV_W
    col = lambda q: slice(q * W, (q + 1) * W)
    r = r_ref[...]
    k = k_ref[...]
    v = v_ref[...]
    lora = lo_ref[...]
    kk = k * kk_ref[...]
    kkn = kk * lax.rsqrt(_seg64_sum_wide(kk * kk) + EPS)
    g_ref[...] = _dot_bf16(_sigmoid(lora[:, 2 * LANES:3 * LANES]), gup_ref[...])
    pre = w0_ref[...] + _dot_bf16(jnp.tanh(lora[:, 0:LANES]), wup_ref[...])
    decay = jnp.exp(-jnp.exp(-_softplus(-pre) - 0.5))
    a = _sigmoid(a0_ref[...] + _dot_bf16(lora[:, LANES:2 * LANES], aup_ref[...]))
    lo = _lane_lo((1, LANES))
    bonus = None
    for d in range(2):
        a_d = a[:, col(d)]
        k_d = k * (1.0 + (a_d - 1.0) * ka_ref[...])
        t = _seg64_sum_wide(r * k_d * rk_ref[...])
        bonus = t if bonus is None else bonus + t
        for s, (x1, x2) in enumerate(((kkn, decay[:, col(d)]), (k_d, kkn * a_d), (r, v))):
            for h in range(RWKV_HEADS):
                blk = slice((h // 2) * LANES, (h // 2 + 1) * LANES)
                if h % 2 == 0:
                    out = jnp.where(lo, x1[:, blk], pltpu.roll(x2[:, blk], RWKV_HS, axis=1))
                else:
                    out = jnp.where(lo, pltpu.roll(x1[:, blk], RWKV_HS, axis=1), x2[:, blk])
                op_ref[:, d * _SCAN_NJ + s * RWKV_HEADS + h, :] = out
    bonus_ref[...] = bonus


def _rwkv_pre(p, r_blk, k_blk, v_blk, lo_blk, wup_bd, aup_bd, gup, w0, a0, k_k, k_a, r_k):
    tm = 256
    w = RWKV_W
    row = lambda n: pl.BlockSpec((1, n), lambda i: (0, 0))
    full = lambda a, b: pl.BlockSpec((a, b), lambda i: (0, 0))
    return pl.pallas_call(
        _rwkv_pre_kernel,
        grid=(N_TOK // tm,),
        in_specs=[pl.BlockSpec((tm, w), lambda i: (i, r_blk)),
                  pl.BlockSpec((tm, w), lambda i: (i, k_blk)),
                  pl.BlockSpec((tm, w), lambda i: (i, v_blk)),
                  pl.BlockSpec((tm, 3 * LANES), lambda i: (i, lo_blk)),
                  full(LANES, 2 * w), full(LANES, 2 * w), full(LANES, w),
                  row(2 * w), row(2 * w), row(w), row(w), row(w)],
        out_specs=[pl.BlockSpec((tm, 2 * _SCAN_NJ, LANES), lambda i: (i, 0, 0)),
                   pl.BlockSpec((tm, w), lambda i: (i, 0)), pl.BlockSpec((tm, w), lambda i: (i, 0))],
        out_shape=[jax.ShapeDtypeStruct((N_TOK, 2 * _SCAN_NJ, LANES), F32),
                   jax.ShapeDtypeStruct((N_TOK, w), F32), jax.ShapeDtypeStruct((N_TOK, w), F32)],
        compiler_params=_params("parallel"),
        name="rwkv_pre",
    )(p, p, p, p, wup_bd, aup_bd, gup, w0, a0, k_k, k_a, r_k)


_SCAN_CHUNK = 32
_SCAN_UNROLL = 16


def _rwkv_first_sa(s_ref, sa_ref, kk, n_k):
    nv = s_ref.shape[1]
    chunk = min(_SCAN_CHUNK, nv)
    for c0 in range(0, nv, chunk):
        def body(k, acc):
            return acc + s_ref[k, c0:c0 + chunk, :] * kk(k)
        sa_ref[c0:c0 + chunk, :] = lax.fori_loop(0, n_k, body, jnp.zeros((chunk, LANES), F32), unroll=_SCAN_UNROLL)


def _rwkv_step(s_ref, sa_ref, kk_next, w, kd, b, r, v_at, n_k):
    nv = s_ref.shape[1]
    chunk = min(_SCAN_CHUNK, nv)
    ys = []
    for c0 in range(0, nv, chunk):
        sa = sa_ref[c0:c0 + chunk, :]
        vc = v_at(c0, chunk)

        def body(k, acc):
            y_acc, sa_acc = acc
            s_new = s_ref[k, c0:c0 + chunk, :] * w(k) - sa * b(k) + vc * kd(k)
            s_ref[k, c0:c0 + chunk, :] = s_new
            return y_acc + s_new * r(k), sa_acc + s_new * kk_next(k)

        zero = jnp.zeros((chunk, LANES), F32)
        y_acc, sa_acc = lax.fori_loop(0, n_k, body, (zero, zero), unroll=_SCAN_UNROLL)
        sa_ref[c0:c0 + chunk, :] = sa_acc
        ys.append(y_acc)
    return ys[0] if len(ys) == 1 else jnp.concatenate(ys, axis=0)


_CTX_TB = 32


def _load_scan_operands(kt_ref, t, slabs):
    for s, x in enumerate(slabs):
        xt = x.T
        kt_ref[t, 2 * s] = xt[:RWKV_HS]
        kt_ref[t, 2 * s + 1] = xt[RWKV_HS:]


def _rwkv_scan_ctx_kernel(x_ref, y_ref, st_ref, s_ref, sa_ref, kt_ref, ys_ref):
    d = pl.program_id(0)
    tb = pl.program_id(1)
    nh = RWKV_HEADS
    step_t = lambda i: jnp.where(d == 0, i, _CTX_TB - 1 - i)

    @pl.when(tb == 0)
    def _():
        s_ref[...] = jnp.zeros_like(s_ref)

    def load_t(t, carry):
        _load_scan_operands(kt_ref, t, [
            jnp.concatenate([x_ref[b, t, s * nh:(s + 1) * nh, :] for b in range(BATCH)], axis=0)
            for s in range(_SCAN_SLOTS)])
        return carry

    lax.fori_loop(0, _CTX_TB, load_t, 0, unroll=4)

    t0 = step_t(0)
    _rwkv_first_sa(s_ref, sa_ref, lambda k: kt_ref[t0, 0, pl.ds(k, 1), :], RWKV_HS)

    def step(i, carry):
        t = step_t(i)
        tn = step_t(jnp.minimum(i + 1, _CTX_TB - 1))
        row = lambda q, tt: (lambda k: kt_ref[tt, q, pl.ds(k, 1), :])
        ys_ref[t] = _rwkv_step(s_ref, sa_ref, row(0, tn), row(1, t), row(2, t), row(3, t), row(4, t),
                               lambda c0, n: kt_ref[t, 5, pl.ds(c0, n), :], RWKV_HS)
        return carry

    lax.fori_loop(0, _CTX_TB, step, 0)

    def store_t(i, carry):
        z = jnp.concatenate([ys_ref[2 * i], ys_ref[2 * i + 1]], axis=0).T
        z_odd = pltpu.roll(z, RWKV_HS, axis=1)
        for b in range(BATCH):
            y_ref[b, 2 * i] = z[b * nh:(b + 1) * nh, :RWKV_HS]
            y_ref[b, 2 * i + 1] = z_odd[b * nh:(b + 1) * nh, :RWKV_HS]
        return carry

    lax.fori_loop(0, _CTX_TB // 2, store_t, 0, unroll=4)

    @pl.when(tb == pl.num_programs(1) - 1)
    def _():
        st_ref[0] = s_ref[...]


def _rwkv_scan_ctx(op4):
    nt = SEQ // _CTX_TB
    hs = RWKV_HS
    tblk = lambda d, tb: jnp.where(d == 0, tb, nt - 1 - tb)
    return pl.pallas_call(
        _rwkv_scan_ctx_kernel,
        grid=(2, nt),
        in_specs=[pl.BlockSpec((BATCH, _CTX_TB, _SCAN_NJ, LANES), lambda d, tb: (0, tblk(d, tb), d, 0))],
        out_specs=[pl.BlockSpec((BATCH, _CTX_TB, RWKV_HEADS, hs), lambda d, tb: (0, tblk(d, tb), d, 0)),
                   pl.BlockSpec((1, hs, hs, LANES), lambda d, tb: (d, 0, 0, 0))],
        out_shape=[jax.ShapeDtypeStruct((BATCH, SEQ, 2 * RWKV_HEADS, hs), F32),
                   jax.ShapeDtypeStruct((2, hs, hs, LANES), F32)],
        scratch_shapes=[pltpu.VMEM((hs, hs, LANES), F32), pltpu.VMEM((hs, LANES), F32),
                        pltpu.VMEM((_CTX_TB, 2 * _SCAN_SLOTS, hs, LANES), F32),
                        pltpu.VMEM((_CTX_TB, hs, LANES), F32)],
        compiler_params=_params("parallel", "arbitrary"),
        name="rwkv_scan_ctx",
    )(op4)


_LAT_TB = 64
_LAT_VSPLIT = 4
_LAT_STATES = 2 * DEC_BATCH * RWKV_HEADS
_LAT_VROWS = RWKV_HS // _LAT_VSPLIT


def _rwkv_scan_lat_kernel(xf0_ref, xf1_ref, xb0_ref, xb1_ref, s0_ref, yf_ref, yb_ref,
                          s_ref, sa_ref, kt_ref, v_ref, ys_ref):
    @pl.when(pl.program_id(0) == 0)
    def _():
        s_ref[...] = s0_ref[...]

    group = lax.broadcasted_iota(jnp.int32, (_LAT_VROWS, LANES), 1) // _LAT_STATES
    nh = RWKV_HEADS

    def load_t(t, carry):
        tr = _LAT_TB - 1 - t
        for s in range(_SCAN_SLOTS):
            heads = slice(s * nh, (s + 1) * nh)
            x = jnp.concatenate([xf0_ref[0, t, heads, :], xf1_ref[0, t, heads, :],
                                 xb0_ref[0, tr, heads, :], xb1_ref[0, tr, heads, :]], axis=0)
            xt = jnp.concatenate([x] * _LAT_VSPLIT, axis=0).T
            kt_ref[t, 2 * s] = xt[:RWKV_HS]
            if s < _SCAN_SLOTS - 1:
                kt_ref[t, 2 * s + 1] = xt[RWKV_HS:]
            else:
                v = jnp.zeros((_LAT_VROWS, LANES), F32)
                for g in range(_LAT_VSPLIT):
                    r0 = RWKV_HS + g * _LAT_VROWS
                    v = jnp.where(group == g, xt[r0:r0 + _LAT_VROWS, :], v)
                v_ref[t] = v
        return carry

    lax.fori_loop(0, _LAT_TB, load_t, 0, unroll=8)

    _rwkv_first_sa(s_ref, sa_ref, lambda k: kt_ref[0, 0, pl.ds(k, 1), :], RWKV_HS)

    def step(t, carry):
        tn = jnp.minimum(t + 1, _LAT_TB - 1)
        row = lambda q, tt: (lambda k: kt_ref[tt, q, pl.ds(k, 1), :])
        ys_ref[t] = _rwkv_step(s_ref, sa_ref, row(0, tn), row(1, t), row(2, t), row(3, t), row(4, t),
                               lambda c0, n: v_ref[t, pl.ds(c0, n), :], RWKV_HS)
        return carry

    lax.fori_loop(0, _LAT_TB, step, 0)

    def store_t(i, carry):
        rows = [jnp.where(group == g, ys_ref[2 * i + j], 0.0) for j in range(2) for g in range(_LAT_VSPLIT)]
        z = jnp.concatenate(rows, axis=0).T
        y = (z[0:_LAT_STATES] + z[_LAT_STATES:2 * _LAT_STATES]
             + z[2 * _LAT_STATES:3 * _LAT_STATES] + z[3 * _LAT_STATES:4 * _LAT_STATES])
        y_odd = pltpu.roll(y, RWKV_HS, axis=1)
        half = _LAT_STATES // 2
        for b in range(DEC_BATCH):
            rows_f = slice(b * nh, (b + 1) * nh)
            rows_b = slice(half + b * nh, half + (b + 1) * nh)
            yf_ref[b, 2 * i] = y[rows_f, :RWKV_HS]
            yf_ref[b, 2 * i + 1] = y_odd[rows_f, :RWKV_HS]
            yb_ref[b, _LAT_TB - 1 - 2 * i] = y[rows_b, :RWKV_HS]
            yb_ref[b, _LAT_TB - 2 - 2 * i] = y_odd[rows_b, :RWKV_HS]
        return carry

    lax.fori_loop(0, _LAT_TB // 2, store_t, 0, unroll=4)


def _rwkv_scan_lat(op4, s0):
    hs = RWKV_HS
    nv = _LAT_VROWS
    nt = DEC_SEQ // _LAT_TB
    per_seq = SEQ // _LAT_TB
    first = N_CTX // SEQ

    def x_spec(b, d):
        tblk = (lambda tb: tb) if d == 0 else (lambda tb: nt - 1 - tb)
        return pl.BlockSpec((1, _LAT_TB, _SCAN_NJ, LANES),
                            lambda tb: (first + b * (DEC_SEQ // SEQ) + tblk(tb) // per_seq, tblk(tb) % per_seq, d, 0))

    y_shape = jax.ShapeDtypeStruct((DEC_BATCH, DEC_SEQ, RWKV_HEADS, hs), F32)
    y_blk = (DEC_BATCH, _LAT_TB, RWKV_HEADS, hs)
    return pl.pallas_call(
        _rwkv_scan_lat_kernel,
        grid=(nt,),
        in_specs=[x_spec(0, 0), x_spec(1, 0), x_spec(0, 1), x_spec(1, 1),
                  pl.BlockSpec((hs, nv, LANES), lambda tb: (0, 0, 0))],
        out_specs=[pl.BlockSpec(y_blk, lambda tb: (0, tb, 0, 0)),
                   pl.BlockSpec(y_blk, lambda tb: (0, nt - 1 - tb, 0, 0))],
        out_shape=[y_shape, y_shape],
        scratch_shapes=[pltpu.VMEM((hs, nv, LANES), F32), pltpu.VMEM((nv, LANES), F32),
                        pltpu.VMEM((_LAT_TB, 2 * _SCAN_SLOTS - 1, hs, LANES), F32),
                        pltpu.VMEM((_LAT_TB, nv, LANES), F32), pltpu.VMEM((_LAT_TB, nv, LANES), F32)],
        compiler_params=_params("arbitrary"),
        name="rwkv_scan_lat",
    )(op4, op4, op4, op4, s0)


def _rwkv_post_kernel(yc_ref, ylf_ref, ylb_ref, bonus_ref, v_ref, g_ref, gn_ref, o_ref):
    def finish(head_sum):
        y = jnp.concatenate([head_sum(h) for h in range(RWKV_HEADS)], axis=1)
        y = y * lax.rsqrt(_seg64_sum_wide(y * y) * (1.0 / RWKV_HS) + EPS) * gn_ref[...]
        o_ref[...] = ((y + bonus_ref[...] * v_ref[...]) * g_ref[...]).astype(o_ref.dtype)

    @pl.when(pl.program_id(0) < N_CTX // SEQ)
    def _():
        finish(lambda h: yc_ref[0, :, h, :] + yc_ref[0, :, RWKV_HEADS + h, :])

    @pl.when(pl.program_id(0) >= N_CTX // SEQ)
    def _():
        finish(lambda h: ylf_ref[0, :, h, :] + ylb_ref[0, :, h, :])


def _rwkv_post(y_ctx, y_lat_f, y_lat_b, bonus, p, v_blk, g, gn):
    tm = SEQ
    w = RWKV_W
    hs = RWKV_HS
    n_ctx = N_CTX // SEQ
    lat = lambda y: y.reshape(N_LAT // SEQ, SEQ, RWKV_HEADS, hs)
    spec = pl.BlockSpec((tm, w), lambda i: (i, 0))
    lat_spec = pl.BlockSpec((1, SEQ, RWKV_HEADS, hs), lambda i: (jnp.maximum(i - n_ctx, 0), 0, 0, 0))
    return pl.pallas_call(
        _rwkv_post_kernel,
        grid=(N_TOK // tm,),
        in_specs=[pl.BlockSpec((1, SEQ, 2 * RWKV_HEADS, hs), lambda i: (jnp.minimum(i, n_ctx - 1), 0, 0, 0)),
                  lat_spec, lat_spec, spec, pl.BlockSpec((tm, w), lambda i: (i, v_blk)), spec,
                  pl.BlockSpec((1, w), lambda i: (0, 0))],
        out_specs=spec,
        out_shape=jax.ShapeDtypeStruct((N_TOK, w), BF16),
        compiler_params=_params("parallel"),
        name="rwkv_post",
    )(y_ctx, lat(y_lat_f), lat(y_lat_b), bonus, p, g, gn)


def _value_split_layout(x):
    lead = x.shape[:-2]
    n = len(lead)
    x = x.reshape(lead + (_LAT_STATES, _LAT_VSPLIT, _LAT_VROWS))
    return jnp.transpose(x, tuple(range(n)) + (n + 2, n + 1, n)).reshape(lead + (_LAT_VROWS, LANES))


_EVEN_HEAD = MLA_Q_RANK + MLA_KV_RANK
_EVEN_PACKED = _EVEN_HEAD + LANES + 2 * RET_HEADS * (RET_DK + RET_DV)


def _pack_even_weight_kernel(w_ref, o_ref):
    w = w_ref[...]
    o_ref[:, 0:_EVEN_HEAD] = w[:, 0:_EVEN_HEAD]
    o_ref[:, _EVEN_HEAD:_EVEN_HEAD + LANES] = jnp.zeros((w.shape[0], LANES), F32)
    o_ref[:, _EVEN_HEAD + MLA_NOPE:_EVEN_HEAD + MLA_QK] = w[:, _EVEN_HEAD:_EVEN_HEAD + MLA_ROPE]
    o_ref[:, _EVEN_HEAD + LANES:] = w[:, _EVEN_HEAD + MLA_ROPE:]


def _pack_even_weight(w_in):
    tm = 256
    return pl.pallas_call(
        _pack_even_weight_kernel,
        grid=(D_MODEL // tm,),
        in_specs=[pl.BlockSpec((tm, w_in.shape[1]), lambda i: (i, 0))],
        out_specs=pl.BlockSpec((tm, _EVEN_PACKED), lambda i: (i, 0)),
        out_shape=jax.ShapeDtypeStruct((D_MODEL, _EVEN_PACKED), F32),
        compiler_params=_params("parallel"),
        name="pack_even_weight",
    )(w_in)


def _even_layer(x, mod, g_mix, w_in, q_norm, kv_norm, w_uq, w_ukv, qn, kn, ret_decay, ret_gn,
                cache_ckv, cache_krope, state_ret, tabs_m):
    w_p = _pack_even_weight(w_in)
    p = _inproj(x, g_mix, mod, w_p, jnp.zeros((1, w_p.shape[1]), F32), 1024, None)
    CKV_BLK, KR_BLK, RQ_BLK, RK_BLK, RV_BLK, RG_BLK = 2, 3, 4, 6, 8, 12

    def head_pad(w, n_head, d_head, c0, c1):
        w = w.reshape(w.shape[0], n_head, d_head)[:, :, c0:c1]
        return jnp.pad(w, ((0, 0), (0, 0), (0, LANES - (c1 - c0)))).reshape(w.shape[0], n_head * LANES)

    w_uq_p = head_pad(w_uq, MLA_HEADS, MLA_QK, 0, MLA_QK)
    wk_p = head_pad(w_ukv, MLA_HEADS, MLA_NOPE + MLA_V, 0, MLA_NOPE)
    wv_p = head_pad(w_ukv, MLA_HEADS, MLA_NOPE + MLA_V, MLA_NOPE, MLA_NOPE + MLA_V)
    qn_p = jnp.pad(qn, (0, LANES - MLA_QK))[None]
    kn_p = jnp.pad(kn, (0, LANES - MLA_QK))[None]

    q = _mla_q(p, q_norm[None], w_uq_p, qn_p, tabs_m)
    k, v, ckvn = _mla_kv(p, CKV_BLK, p, KR_BLK, kv_norm[None], wk_p, wv_p, kn_p, tabs_m, N_TOK, True)

    n_c = DEC_BATCH * PAST_LEN
    kr_c = jnp.pad(cache_krope.reshape(n_c, MLA_ROPE), ((0, 0), (MLA_NOPE, LANES - MLA_QK)))
    k_c, v_c, _ = _mla_kv(cache_ckv.reshape(n_c, MLA_KV_RANK), 0, kr_c, 0, kv_norm[None], wk_p, wv_p, kn_p,
                          tabs_m, n_c, False)

    o_ctx = _mla_attn(q, k, v, None, BATCH, SEQ, 0, SEQ, MLA_HEADS // 2)
    o_lat = _mla_attn(q, k, v, (k_c, v_c), DEC_BATCH, DEC_SEQ, N_CTX, 256, 1)

    log_g = -_softplus(-ret_decay)
    gn = ret_gn[None]
    s0_ctx = jnp.zeros((BATCH, 2, RET_HEADS * RET_DK, RET_DV), F32)
    r_ctx, st_ctx = _retention(log_g, p, RQ_BLK, RK_BLK, RV_BLK, RG_BLK, s0_ctx, gn, BATCH, SEQ, 0, SEQ, True)
    s0_lat = state_ret.reshape(DEC_BATCH, 2, RET_HEADS * RET_DK, RET_DV)
    r_lat, _ = _retention(log_g, p, RQ_BLK, RK_BLK, RV_BLK, RG_BLK, s0_lat, gn, DEC_BATCH, DEC_SEQ, N_CTX, 256,
                          False)

    mix = [(o_ctx, o_lat), (r_ctx, r_lat)]
    new_ckv = ckvn[:N_CTX].reshape(BATCH, SEQ, MLA_KV_RANK)
    new_krope = p[:N_CTX, KR_BLK * LANES + MLA_NOPE:KR_BLK * LANES + MLA_QK]
    new_krope = new_krope.reshape(BATCH, SEQ, MLA_ROPE)
    new_ret = st_ctx.reshape(BATCH, 2, RET_HEADS, RET_DK, RET_DV)
    return mix, new_ckv, new_krope, new_ret


def _odd_layer(x, mod, g_mix, w_in, qn, kn, lam, diff_gn, mu, w0, w_up, a0, a_up, g_up, k_k, k_a, r_k, gn,
               cache_k, cache_v, state_rwkv, tabs_d, lam_init):
    w_p = w_in
    n_in = w_p.shape[1]
    mu_full = jnp.concatenate([jnp.zeros((3 * DIFF_W,), F32), mu])[None]
    p = _inproj(x, g_mix, mod, w_p, mu_full, 384, (3 * DIFF_W) // 384)
    DV_BLK, R_BLK, K_BLK, V_BLK = 2, 3, 4, 5
    LO_BLK = (6 * RWKV_W) // (3 * LANES)

    qn_p = jnp.tile(qn, 2)[None]
    kn_p = jnp.tile(kn, 2)[None]
    q, k, dk, dv = _diff_qk(p, qn_p, kn_p, tabs_d)

    n_c = DEC_BATCH * PAST_LEN
    cache = (cache_k.reshape(n_c, DIFF_W), cache_v.reshape(n_c, DIFF_W))
    dgn = diff_gn[None]
    o_ctx = _diff_attn(lam, q, k, p, DV_BLK * DIFF_HEADS, None, dgn, BATCH, SEQ, 0, SEQ, lam_init, DIFF_HEADS)
    o_lat = _diff_attn(lam, q, k, p, DV_BLK * DIFF_HEADS, cache, dgn, DEC_BATCH, DEC_SEQ, N_CTX, 256, lam_init, 1)

    zero = jnp.zeros((RWKV_W_LORA, RWKV_W), F32)
    wup_bd = jnp.concatenate([jnp.concatenate([w_up[0], zero], 1), jnp.concatenate([zero, w_up[1]], 1)], 0)
    aup_bd = jnp.concatenate([jnp.concatenate([a_up[0], zero], 1), jnp.concatenate([zero, a_up[1]], 1)], 0)
    op, g, bonus = _rwkv_pre(p, R_BLK, K_BLK, V_BLK, LO_BLK, wup_bd, aup_bd, g_up, w0.reshape(1, -1),
                             a0.reshape(1, -1), k_k[None], k_a[None], r_k.reshape(1, -1))
    op4 = op.reshape(N_TOK // SEQ, SEQ, 2 * _SCAN_NJ, LANES)
    y_ctx, st_ctx = _rwkv_scan_ctx(op4)
    s0_lat = jnp.transpose(state_rwkv, (4, 1, 0, 2, 3)).reshape(RWKV_HS, _LAT_STATES, RWKV_HS)
    y_lat_f, y_lat_b = _rwkv_scan_lat(op4, _value_split_layout(s0_lat))
    rw_o = _rwkv_post(y_ctx, y_lat_f, y_lat_b, bonus, p, V_BLK, g, gn[None])

    mix = [(o_ctx, o_lat), rw_o]
    new_dk = dk.reshape(BATCH, SEQ, DIFF_HEADS, 2, DIFF_DH)
    new_dv = dv.reshape(BATCH, SEQ, DIFF_HEADS, 2 * DIFF_DH)
    new_rwkv = jnp.transpose(st_ctx.reshape(2, RWKV_HS, RWKV_HS, BATCH, RWKV_HEADS), (3, 0, 4, 2, 1))
    return mix, new_dk, new_dv, new_rwkv


def kernel(x_prompt, x_sample, cache_mla_ckv, cache_mla_krope, state_ret, cache_diff_k, cache_diff_v, state_rwkv,
           c, c_ctx, ada_w, ada_b, norm_mix_g, norm_ffn_g, w_out, ffn_up, ffn_conv_w, ffn_conv_b, ffn_down,
           a_w_in, mla_q_norm, mla_kv_norm, mla_w_uq, mla_w_ukv, mla_qn, mla_kn, ret_decay, ret_gn,
           b_w_in, diff_qn, diff_kn, diff_lam, diff_gn, rwkv_mu, rwkv_w0, rwkv_w_up, rwkv_a0, rwkv_a_up,
           rwkv_g_up, rwkv_k_k, rwkv_k_a, rwkv_r_k, rwkv_gn):
    x = jnp.concatenate([x_prompt.reshape(N_CTX, D_MODEL), x_sample.reshape(N_LAT, D_MODEL)], 0)
    cond8 = jnp.pad(jnp.concatenate([c_ctx[None], c], 0), ((0, 8 - N_GROUPS), (0, 0)))
    mod = _modulation(cond8, ada_w, ada_b)

    tabs_m = _rope_tables(MLA_ROPE, (MLA_NOPE,))
    tabs_d = _rope_tables(DIFF_DH, (0, DIFF_DH))

    outs = {}
    for l in range(DEPTH):
        j = l // 2
        g_mix = norm_mix_g[l][None]
        if l % 2 == 0:
            mix, outs["ckv"], outs["krope"], outs["ret"] = _even_layer(
                x, mod[l], g_mix, a_w_in[j], mla_q_norm[j], mla_kv_norm[j], mla_w_uq[j], mla_w_ukv[j], mla_qn[j],
                mla_kn[j], ret_decay[j], ret_gn[j], cache_mla_ckv[:, j], cache_mla_krope[:, j], state_ret[:, j],
                tabs_m)
        else:
            lam_init = 0.8 - 0.6 * math.exp(-0.3 * l)
            mix, outs["dk"], outs["dv"], outs["rwkv"] = _odd_layer(
                x, mod[l], g_mix, b_w_in[j], diff_qn[j], diff_kn[j], diff_lam[j], diff_gn[j], rwkv_mu[j],
                rwkv_w0[j], rwkv_w_up[j], rwkv_a0[j], rwkv_a_up[j], rwkv_g_up[j], rwkv_k_k[j], rwkv_k_a[j],
                rwkv_r_k[j], rwkv_gn[j], cache_diff_k[:, j], cache_diff_v[:, j], state_rwkv[:, j], tabs_d, lam_init)
        x = _resid_proj(mix, w_out, l, x, mod[l], 2)
        act = _ffn_up(x, norm_ffn_g[l][None], mod[l], ffn_up, l, ffn_conv_w[l], ffn_conv_b[l])
        x = _resid_proj([act], ffn_down, l, x, mod[l], 5, split_out=(l == DEPTH - 1))

    y_prompt = x[0].reshape(BATCH, SEQ, D_MODEL)
    y_sample = x[1].reshape(DEC_BATCH, DEC_SEQ, D_MODEL)
    return (y_prompt, y_sample, outs["ckv"][:, None], outs["krope"][:, None], outs["ret"][:, None],
            outs["dk"][:, None], outs["dv"][:, None], outs["rwkv"][:, None])
```

```python
import functools
import math

import numpy as np
import jax
import jax.numpy as jnp
from jax import lax
from jax.experimental import pallas as pl
from jax.experimental.pallas import tpu as pltpu

D_MODEL = 1024
BATCH = 16
SEQ = 256
DEPTH = 2
DEC_BATCH = 2
DEC_SEQ = 1024
PAST_LEN = 512
GRID_W = 64
EPS = 1e-6
ROPE_BASE = 10000.0

MLA_HEADS = 8
MLA_Q_RANK = 256
MLA_KV_RANK = 128
MLA_NOPE = 64
MLA_ROPE = 32
MLA_V = 64
MLA_QK = MLA_NOPE + MLA_ROPE
RET_HEADS = 4
RET_DK = 64
RET_DV = 128
DIFF_HEADS = 4
DIFF_DH = 64
DIFF_W = DIFF_HEADS * 2 * DIFF_DH
RWKV_HEADS = 8
RWKV_HS = 64
RWKV_W = RWKV_HEADS * RWKV_HS
RWKV_W_LORA = 64
D_FF = 2816

N_CTX = BATCH * SEQ
N_LAT = DEC_BATCH * DEC_SEQ
N_TOK = N_CTX + N_LAT
N_GROUPS = 1 + DEC_BATCH

LANES = 128
VMEM_LIMIT = 56 * 1024 * 1024

_PREC = lax.Precision.HIGHEST
F32 = jnp.float32


def _dot_tn(a, b):
    return lax.dot_general(a, b, (((0,), (0,)), ((), ())), precision=_PREC, preferred_element_type=F32)


BF16 = jnp.bfloat16


def _dot_bf16(a, b):
    return jnp.dot(a.astype(BF16), b.astype(BF16), preferred_element_type=F32)


def _dot_nt_bf16(a, b):
    return lax.dot_general(a.astype(BF16), b.astype(BF16), (((1,), (1,)), ((), ())), preferred_element_type=F32)


def _params(*sem):
    return pltpu.CompilerParams(dimension_semantics=sem, vmem_limit_bytes=VMEM_LIMIT)


def _sigmoid(x):
    return 1.0 / (1.0 + jnp.exp(-x))


def _silu(x):
    return x * _sigmoid(x)


def _softplus(x):
    return jnp.maximum(x, 0.0) + jnp.log(1.0 + jnp.exp(-jnp.abs(x)))


def _rms(x, n):
    return x * lax.rsqrt(jnp.sum(x * x, axis=-1, keepdims=True) * (1.0 / n) + EPS)


def _lane_lo(shape):
    return lax.broadcasted_iota(jnp.int32, shape, len(shape) - 1) < 64


def _seg64_sum(x):
    lo = _lane_lo(x.shape)
    s_lo = jnp.sum(jnp.where(lo, x, 0.0), axis=-1, keepdims=True)
    s_hi = jnp.sum(jnp.where(lo, 0.0, x), axis=-1, keepdims=True)
    return jnp.where(lo, s_lo, s_hi)


_SUBLANES = 8


def _seq_neighbours(p, tile, tile_rows):
    is_ctx = tile * tile_rows < N_CTX
    sub = lax.broadcasted_iota(jnp.int32, (_SUBLANES, 1), 0)

    def shifted(rolled, edge_sublane, group_of_seq):
        pieces, start = [], 0
        for q in range(tile_rows // SEQ):
            g0 = q * SEQ + group_of_seq
            outer = (q == 0) if group_of_seq == 0 else (q == tile_rows // SEQ - 1)
            edge = (sub == edge_sublane) if outer else ((sub == edge_sublane) & is_ctx)
            pieces += [rolled[start:g0], jnp.where(edge, 0.0, rolled[g0:g0 + _SUBLANES])]
            start = g0 + _SUBLANES
        pieces.append(rolled[start:])
        return jnp.concatenate([x for x in pieces if x.shape[0]], axis=0)

    prev = shifted(pltpu.roll(p, 1, axis=0), 0, 0)
    nxt = shifted(pltpu.roll(p, tile_rows - 1, axis=0), _SUBLANES - 1, SEQ - _SUBLANES)
    return prev, nxt


def _group_of_tile(i, tile_rows):
    row = i * tile_rows
    return jnp.where(row < N_CTX, 0, 1 + (row - N_CTX) // DEC_SEQ)


def _modulation_kernel(c_ref, w_ref, b_ref, o_ref):
    o_ref[0] = _dot_bf16(_silu(c_ref[...]), w_ref[0]) + b_ref[0]


def _modulation(cond8, ada_w, ada_b):
    tn = 512
    n = 6 * D_MODEL
    out = pl.pallas_call(
        _modulation_kernel,
        grid=(DEPTH, n // tn),
        in_specs=[pl.BlockSpec((8, D_MODEL), lambda l, j: (0, 0)),
                  pl.BlockSpec((1, D_MODEL, tn), lambda l, j: (l, 0, j)),
                  pl.BlockSpec((1, 1, tn), lambda l, j: (l, 0, j))],
        out_specs=pl.BlockSpec((1, 8, tn), lambda l, j: (l, 0, j)),
        out_shape=jax.ShapeDtypeStruct((DEPTH, 8, n), F32),
        compiler_params=_params("parallel", "parallel"),
        name="modulation",
    )(cond8, ada_w, ada_b.reshape(DEPTH, 1, n))
    m = out[:, :N_GROUPS].reshape(DEPTH, N_GROUPS, 6, D_MODEL)
    return jnp.pad(m, ((0, 0), (0, 0), (0, 2), (0, 0)))


_TM_SEQ = 1024


def _norm_mod(x, g, mod, off):
    return _rms(x, D_MODEL) * g * (1.0 + mod[off + 1:off + 2, :]) + mod[off:off + 1, :]


def _row_split_specs(tm, width_block, col):
    n_ctx = N_CTX // tm
    return [pl.BlockSpec((tm, width_block), lambda i, j: (jnp.minimum(i, n_ctx - 1), col(j))),
            pl.BlockSpec((tm, width_block), lambda i, j: (jnp.maximum(i - n_ctx, 0), col(j)))]


def _for_row_tile(i, tm, ctx_ref, lat_ref, fn):
    @pl.when(i < N_CTX // tm)
    def _():
        fn(ctx_ref)

    @pl.when(i >= N_CTX // tm)
    def _():
        fn(lat_ref)


def _inproj_kernel(*refs, shift_from, split_x):
    x_refs, (g_ref, mod_ref, w_ref, mu_ref, o_ref, h_ref) = refs[:-6], refs[-6:]
    i = pl.program_id(0)

    @pl.when(pl.program_id(1) == 0)
    def _():
        def norm(x_ref):
            h_ref[...] = _norm_mod(x_ref[...], g_ref[...], mod_ref[0], 0).astype(BF16)

        if split_x:
            _for_row_tile(i, _TM_SEQ, x_refs[0], x_refs[1], norm)
        else:
            norm(x_refs[0])

    p = _dot_bf16(h_ref[...], w_ref[...])
    if shift_from is None:
        o_ref[...] = p
    else:
        @pl.when(pl.program_id(1) < shift_from)
        def _():
            o_ref[...] = p

        @pl.when(pl.program_id(1) >= shift_from)
        def _():
            prev, nxt = _seq_neighbours(p, i, _TM_SEQ)
            o_ref[...] = p + (0.5 * (prev + nxt) - p) * mu_ref[...]


def _inproj(x, g, mod, w, mu, tn, shift_from):
    n = w.shape[1]
    tm = _TM_SEQ
    split_x = isinstance(x, (tuple, list))
    x_specs = _row_split_specs(tm, D_MODEL, lambda j: 0) if split_x else [pl.BlockSpec((tm, D_MODEL), lambda i, j: (i, 0))]
    return pl.pallas_call(
        functools.partial(_inproj_kernel, shift_from=shift_from, split_x=split_x),
        grid=(N_TOK // tm, n // tn),
        in_specs=x_specs
        + [pl.BlockSpec((1, D_MODEL), lambda i, j: (0, 0)),
                  pl.BlockSpec((1, 8, D_MODEL), lambda i, j: (_group_of_tile(i, tm), 0, 0)),
                  pl.BlockSpec((D_MODEL, tn), lambda i, j: (0, j)),
                  pl.BlockSpec((1, tn), lambda i, j: (0, j))],
        out_specs=pl.BlockSpec((tm, tn), lambda i, j: (i, j)),
        out_shape=jax.ShapeDtypeStruct((N_TOK, n), F32),
        scratch_shapes=[pltpu.VMEM((tm, D_MODEL), BF16)],
        compiler_params=_params("parallel", "arbitrary"),
        name="inproj" if shift_from is None else "inproj_shift",
    )(*(x if split_x else [x]), g, mod, w, mu)


def _resid_kernel(*refs, gate_row, widths, split, split_x, split_out):
    n_in = sum(2 if sp else 1 for sp in split)
    a_refs = refs[:n_in]
    w_ref = refs[n_in]
    n_x = 2 if split_x else 1
    x_refs = refs[n_in + 1:n_in + 1 + n_x]
    mod_ref = refs[n_in + 1 + n_x]
    o_ref = refs[n_in + 2 + n_x:-1] if split_out else refs[n_in + 2 + n_x]
    a_bf_ref = refs[-1]
    i = pl.program_id(0)

    @pl.when(pl.program_id(1) == 0)
    def _():
        k0, r = 0, 0
        for width, sp in zip(widths, split):
            cols = slice(k0, k0 + width)
            if sp:
                ctx_ref, lat_ref = a_refs[r], a_refs[r + 1]

                @pl.when(i < N_CTX // _TM_SEQ)
                def _():
                    a_bf_ref[:, cols] = ctx_ref[...].astype(BF16)

                @pl.when(i >= N_CTX // _TM_SEQ)
                def _():
                    a_bf_ref[:, cols] = lat_ref[...].astype(BF16)
            else:
                a_bf_ref[:, cols] = a_refs[r][...].astype(BF16)
            k0 += width
            r += 2 if sp else 1

    d = mod_ref[0, gate_row:gate_row + 1, :] * _dot_bf16(a_bf_ref[...], w_ref[0])
    if split_x:
        def add(x_ref):
            o_ref[...] = x_ref[...] + d

        _for_row_tile(i, _TM_SEQ, x_refs[0], x_refs[1], add)
        return
    y = x_refs[0][...] + d
    if not split_out:
        o_ref[...] = y
    else:
        ctx_o_ref, lat_o_ref = o_ref

        @pl.when(i < N_CTX // _TM_SEQ)
        def _():
            ctx_o_ref[...] = y

        @pl.when(i >= N_CTX // _TM_SEQ)
        def _():
            lat_o_ref[...] = y


def _resid_proj(acts, w, layer, x, mod, gate_row, split_out=False):
    tm, tn = _TM_SEQ, 256
    split_x = isinstance(x, (tuple, list))
    assert not (split_x and split_out)
    n_ctx = N_CTX // tm
    nj = D_MODEL // tn
    split = [isinstance(a, (tuple, list)) for a in acts]
    widths = [a[0].shape[1] if sp else a.shape[1] for a, sp in zip(acts, split)]
    k = sum(widths)
    in_specs, operands = [], []
    for a, width, sp in zip(acts, widths, split):
        if sp:
            in_specs += [pl.BlockSpec((tm, width), lambda i, j: (jnp.minimum(i, n_ctx - 1), 0)),
                         pl.BlockSpec((tm, width), lambda i, j: (jnp.maximum(i - n_ctx, 0), 0))]
            operands += list(a)
        else:
            in_specs.append(pl.BlockSpec((tm, width), lambda i, j: (i, 0)))
            operands.append(a)
    if split_out:
        out_specs = [pl.BlockSpec((tm, tn), lambda i, j: (jnp.minimum(i, n_ctx - 1), jnp.where(i < n_ctx, j, nj - 1))),
                     pl.BlockSpec((tm, tn), lambda i, j: (jnp.maximum(i - n_ctx, 0), jnp.where(i < n_ctx, 0, j)))]
        out_shape = [jax.ShapeDtypeStruct((N_CTX, D_MODEL), F32), jax.ShapeDtypeStruct((N_LAT, D_MODEL), F32)]
        sem = ("arbitrary", "arbitrary")
    else:
        out_specs = pl.BlockSpec((tm, tn), lambda i, j: (i, j))
        out_shape = jax.ShapeDtypeStruct((N_TOK, D_MODEL), F32)
        sem = ("parallel", "arbitrary")
    return pl.pallas_call(
        functools.partial(_resid_kernel, gate_row=gate_row, widths=tuple(widths), split=tuple(split),
                          split_x=split_x, split_out=split_out),
        grid=(N_TOK // tm, nj),
        in_specs=in_specs
        + [pl.BlockSpec((1, k, tn), lambda i, j: (layer, 0, j))]
        + (_row_split_specs(tm, tn, lambda j: j) if split_x else [pl.BlockSpec((tm, tn), lambda i, j: (i, j))])
        + [pl.BlockSpec((1, 8, tn), lambda i, j: (_group_of_tile(i, tm), 0, j))],
        out_specs=out_specs,
        out_shape=out_shape,
        scratch_shapes=[pltpu.VMEM((tm, k), BF16)],
        compiler_params=_params(*sem),
        name="resid_proj",
    )(*operands, w, *(x if split_x else [x]), mod)


def _ffn_up_kernel(x_ref, g_ref, mod_ref, wa_ref, wb_ref, cwa_ref, cwb_ref, cba_ref, cbb_ref, o_ref, h_ref):
    i = pl.program_id(0)

    @pl.when(pl.program_id(1) == 0)
    def _():
        h_ref[...] = _norm_mod(x_ref[...], g_ref[...], mod_ref[0], 3).astype(BF16)

    h = h_ref[...]

    def conv(w_ref, cw_ref, cb_ref):
        u = _dot_bf16(h, w_ref[0])
        prev, nxt = _seq_neighbours(u, i, _TM_SEQ)
        return prev * cw_ref[0:1, :] + u * cw_ref[1:2, :] + nxt * cw_ref[2:3, :] + cb_ref[...]

    o_ref[...] = (_silu(conv(wa_ref, cwa_ref, cba_ref)) * conv(wb_ref, cwb_ref, cbb_ref)).astype(o_ref.dtype)


def _ffn_up(x, g, mod, up, layer, cw, cb):
    tm, tn = _TM_SEQ, 256
    nb = D_FF // tn
    cb = cb.reshape(1, 2 * D_FF)
    return pl.pallas_call(
        _ffn_up_kernel,
        grid=(N_TOK // tm, nb),
        in_specs=[pl.BlockSpec((tm, D_MODEL), lambda i, j: (i, 0)),
                  pl.BlockSpec((1, D_MODEL), lambda i, j: (0, 0)),
                  pl.BlockSpec((1, 8, D_MODEL), lambda i, j: (_group_of_tile(i, tm), 0, 0)),
                  pl.BlockSpec((1, D_MODEL, tn), lambda i, j: (layer, 0, j)),
                  pl.BlockSpec((1, D_MODEL, tn), lambda i, j: (layer, 0, j + nb)),
                  pl.BlockSpec((3, tn), lambda i, j: (0, j)),
                  pl.BlockSpec((3, tn), lambda i, j: (0, j + nb)),
                  pl.BlockSpec((1, tn), lambda i, j: (0, j)),
                  pl.BlockSpec((1, tn), lambda i, j: (0, j + nb))],
        out_specs=pl.BlockSpec((tm, tn), lambda i, j: (i, j)),
        out_shape=jax.ShapeDtypeStruct((N_TOK, D_FF), BF16),
        scratch_shapes=[pltpu.VMEM((tm, D_MODEL), BF16)],
        compiler_params=_params("parallel", "arbitrary"),
        name="ffn_up",
    )(x, g, mod, up, up, cw, cw, cb, cb)


_ROPE_TM = 512


def _rope(y, c, s1, s2):
    return y * c + pltpu.roll(y, 1, axis=1) * s1 + pltpu.roll(y, LANES - 1, axis=1) * s2


def _rope_tables(rot_dim, lane_offsets):
    t = np.arange(DEC_SEQ)
    row, col = t // GRID_W, t % GRID_W
    n_freq = rot_dim // 4
    inv = ROPE_BASE ** (-np.arange(n_freq, dtype=np.float64) / n_freq)
    ang = np.concatenate([row[:, None] * inv, col[:, None] * inv], -1)
    cos, sin = np.cos(ang), np.sin(ang)
    n = _ROPE_TM + DEC_SEQ
    c, s1, s2 = np.ones((n, LANES)), np.zeros((n, LANES)), np.zeros((n, LANES))
    for a in lane_offsets:
        even = a + 2 * np.arange(rot_dim // 2)
        c[_ROPE_TM:, even] = cos
        c[_ROPE_TM:, even + 1] = cos
        s1[_ROPE_TM:, even + 1] = sin
        s2[_ROPE_TM:, even] = -sin
    return tuple(jnp.asarray(x, F32) for x in (c, s1, s2))


def _store_rotated(dst_ref, sl, y, tabs, rotate):
    if rotate is False:
        dst_ref[:, sl] = y
        return

    @pl.when(rotate)
    def _():
        dst_ref[:, sl] = _rope(y, *tabs)

    @pl.when(jnp.logical_not(rotate))
    def _():
        dst_ref[:, sl] = y


def _latent_tile():
    return pl.program_id(0) >= N_CTX // _ROPE_TM


def _rope_block(i):
    row = i * _ROPE_TM
    return jnp.where(row < N_CTX, 0, 1 + ((row - N_CTX) % DEC_SEQ) // _ROPE_TM)


def _mla_q_kernel(cq_ref, qnorm_ref, w_ref, qn_ref, c_ref, s1_ref, s2_ref, o_ref):
    xn = _rms(cq_ref[...], MLA_Q_RANK) * qnorm_ref[...]
    y = _dot_bf16(xn, w_ref[...])
    tabs = (c_ref[...], s1_ref[...], s2_ref[...])
    for h in range(MLA_HEADS):
        sl = slice(h * LANES, (h + 1) * LANES)
        _store_rotated(o_ref, sl, _rms(y[:, sl], MLA_QK) * qn_ref[...], tabs, _latent_tile())


def _mla_q(p, q_norm, w_uq_p, qn_p, tabs):
    tm = _ROPE_TM
    hw = MLA_HEADS * LANES
    tab_spec = pl.BlockSpec((tm, LANES), lambda i: (_rope_block(i), 0))
    return pl.pallas_call(
        _mla_q_kernel,
        grid=(N_TOK // tm,),
        in_specs=[pl.BlockSpec((tm, MLA_Q_RANK), lambda i: (i, 0)),
                  pl.BlockSpec((1, MLA_Q_RANK), lambda i: (0, 0)),
                  pl.BlockSpec((MLA_Q_RANK, hw), lambda i: (0, 0)),
                  pl.BlockSpec((1, LANES), lambda i: (0, 0)),
                  tab_spec, tab_spec, tab_spec],
        out_specs=pl.BlockSpec((tm, hw), lambda i: (i, 0)),
        out_shape=jax.ShapeDtypeStruct((N_TOK, hw), F32),
        compiler_params=_params("parallel"),
        name="mla_q",
    )(p, q_norm, w_uq_p, qn_p, *tabs)


def _mla_kv_kernel(ckv_ref, kr_ref, kvn_ref, wk_ref, wv_ref, kn_ref, c_ref, s1_ref, s2_ref,
                   k_ref, v_ref, ckvn_ref, *, norm_ckv):
    ckv = ckv_ref[...]
    if norm_ckv:
        ckv = _rms(ckv, MLA_KV_RANK) * kvn_ref[...]
    ckvn_ref[...] = ckv
    ckv_bf = ckv.astype(BF16)
    kk = _dot_bf16(ckv_bf, wk_ref[...])
    v_ref[...] = _dot_bf16(ckv_bf, wv_ref[...])
    kr = kr_ref[...]
    tabs = (c_ref[...], s1_ref[...], s2_ref[...])
    for h in range(MLA_HEADS):
        sl = slice(h * LANES, (h + 1) * LANES)
        _store_rotated(k_ref, sl, _rms(kk[:, sl] + kr, MLA_QK) * kn_ref[...], tabs,
                       _latent_tile() if norm_ckv else False)


def _mla_kv(ckv_src, ckv_blk, kr_src, kr_blk, kv_norm, wk_p, wv_p, kn_p, tabs, n_rows, own_tokens):
    tm = _ROPE_TM
    hw = MLA_HEADS * LANES
    tab_spec = pl.BlockSpec((tm, LANES), (lambda i: (_rope_block(i), 0)) if own_tokens else (lambda i: (0, 0)))
    return pl.pallas_call(
        functools.partial(_mla_kv_kernel, norm_ckv=own_tokens),
        grid=(n_rows // tm,),
        in_specs=[pl.BlockSpec((tm, LANES), lambda i: (i, ckv_blk)),
                  pl.BlockSpec((tm, LANES), lambda i: (i, kr_blk)),
                  pl.BlockSpec((1, LANES), lambda i: (0, 0)),
                  pl.BlockSpec((MLA_KV_RANK, hw), lambda i: (0, 0)),
                  pl.BlockSpec((MLA_KV_RANK, hw), lambda i: (0, 0)),
                  pl.BlockSpec((1, LANES), lambda i: (0, 0)),
                  tab_spec, tab_spec, tab_spec],
        out_specs=[pl.BlockSpec((tm, hw), lambda i: (i, 0)),
                   pl.BlockSpec((tm, hw), lambda i: (i, 0)),
                   pl.BlockSpec((tm, LANES), lambda i: (i, 0))],
        out_shape=[jax.ShapeDtypeStruct((n_rows, hw), F32),
                   jax.ShapeDtypeStruct((n_rows, hw), F32),
                   jax.ShapeDtypeStruct((n_rows, LANES), F32)],
        compiler_params=_params("parallel"),
        name="mla_kv",
    )(ckv_src, kr_src, kv_norm, wk_p, wv_p, kn_p, *tabs)


_LOG2E = math.log2(math.e)


def _softmax_parts(scores, scale):
    m = functools.reduce(jnp.maximum, [jnp.max(sc, axis=-1, keepdims=True) for sc in scores])
    ps = [jnp.exp2((sc - m) * (scale * _LOG2E)) for sc in scores]
    return ps, sum(jnp.sum(p, axis=-1, keepdims=True) for p in ps)


def _attend(q, sources, sl, scale):
    ps, l = _softmax_parts([_dot_nt_bf16(q, k_ref[:, sl]) for k_ref, _ in sources], scale)
    return sum(_dot_bf16(p, v_ref[:, sl]) for p, (_, v_ref) in zip(ps, sources)), l


def _mla_attn_kernel(*refs, pairs):
    q_ref, o_ref = refs[0], refs[-1]
    sources = [refs[1:3]] + ([refs[3:5]] if len(refs) == 6 else [])
    scale = MLA_QK ** -0.5
    for pr in range(pairs):
        outs = []
        for h in range(2):
            sl = slice((2 * pr + h) * LANES, (2 * pr + h + 1) * LANES)
            o, l = _attend(q_ref[:, sl], sources, sl, scale)
            outs.append(o / l)
        o_ref[:, pr * LANES:(pr + 1) * LANES] = (outs[0] + pltpu.roll(outs[1], MLA_V, axis=1)).astype(o_ref.dtype)


def _mla_attn(q, k, v, cache, batch, n, row0, tq, pairs):
    nqb = n // tq
    qb0 = row0 // tq
    kb0 = row0 // n
    wide = 2 * LANES * pairs
    kv_spec = pl.BlockSpec((n, wide), lambda b, h, i: (kb0 + b, h))
    in_specs, operands = [pl.BlockSpec((tq, wide), lambda b, h, i: (qb0 + b * nqb + i, h)), kv_spec, kv_spec], [q, k, v]
    if cache is not None:
        in_specs += [pl.BlockSpec((PAST_LEN, wide), lambda b, h, i: (b, h))] * 2
        operands += list(cache)
    return pl.pallas_call(
        functools.partial(_mla_attn_kernel, pairs=pairs),
        grid=(batch, MLA_HEADS // (2 * pairs), nqb),
        in_specs=in_specs,
        out_specs=pl.BlockSpec((tq, LANES * pairs), lambda b, h, i: (b * nqb + i, h)),
        out_shape=jax.ShapeDtypeStruct((batch * n, MLA_HEADS * MLA_V), BF16),
        compiler_params=_params("parallel", "parallel", "arbitrary"),
        name="mla_attn",
    )(*operands)


def _ret_kernel(lg_ref, q_ref, k_ref, v_ref, rg_ref, s0_ref, gn_ref, *out_and_scratch, n, tq, want_state):
    if want_state:
        o_ref, st_ref, decay_ref = out_and_scratch
    else:
        o_ref, decay_ref = out_and_scratch
    pair, qi, b = pl.program_id(0), pl.program_id(1), pl.program_id(2)
    q = q_ref[...]
    k = k_ref[...] * (RET_DK ** -0.5)
    lo = _lane_lo((1, LANES))
    row = (qi * tq + lax.broadcasted_iota(jnp.int32, (tq, 1), 0)).astype(F32)

    @pl.when(b == 0)
    def _():
        col = lax.broadcasted_iota(jnp.int32, (1, n), 1).astype(F32)
        diff = row - col
        for h in range(2):
            lgf = lg_ref[0, 2 * pair + h]
            lgb = lg_ref[1, 2 * pair + h]
            decay_ref[h] = (jnp.where(diff >= 0, jnp.exp(lgf * jnp.maximum(diff, 0.0)), 0.0)
                            + jnp.where(diff <= 0, jnp.exp(lgb * jnp.maximum(-diff, 0.0)), 0.0))

    for h in range(2):
        lgf = lg_ref[0, 2 * pair + h]
        lgb = lg_ref[1, 2 * pair + h]
        mask = lo if h == 0 else jnp.logical_not(lo)
        qh = jnp.where(mask, q, 0.0)
        vh = v_ref[:, h * LANES:(h + 1) * LANES]
        o = _dot_bf16(_dot_nt_bf16(qh, k) * decay_ref[h], vh)
        o = o + _dot_bf16(qh * jnp.exp(lgf * (row + 1.0)), s0_ref[0, 0])
        o = o + _dot_bf16(qh * jnp.exp(lgb * (n - row)), s0_ref[0, 1])
        y = _rms(o, RET_DV) * gn_ref[:, h * LANES:(h + 1) * LANES]
        o_ref[:, h * LANES:(h + 1) * LANES] = (_silu(rg_ref[:, h * LANES:(h + 1) * LANES]) * y).astype(o_ref.dtype)

    if want_state:
        pos = lax.broadcasted_iota(jnp.int32, (n, 1), 0).astype(F32)
        for d in range(2):
            acc = None
            for h in range(2):
                lg = lg_ref[d, 2 * pair + h]
                mask = lo if h == 0 else jnp.logical_not(lo)
                expo = (n - 1.0 - pos) if d == 0 else pos
                kd = jnp.where(mask, k * jnp.exp(lg * expo), 0.0)
                term = _dot_tn(kd, v_ref[:, h * LANES:(h + 1) * LANES])
                acc = term if acc is None else acc + term
            lg_rows = jnp.where(lax.broadcasted_iota(jnp.int32, (LANES, 1), 0) < 64,
                                lg_ref[d, 2 * pair], lg_ref[d, 2 * pair + 1])
            st_ref[0, d] = acc + s0_ref[0, d] * jnp.exp(lg_rows * n)


def _retention(log_g, p, q_blk, k_blk, v_blk, g_blk, s0, gn, batch, n, row0, tq, want_state):
    nqb = n // tq
    assert not want_state or nqb == 1
    qb0 = row0 // tq
    kb0 = row0 // n
    pairs = RET_HEADS // 2
    out_specs = [pl.BlockSpec((tq, 2 * LANES), lambda h, i, b: (b * nqb + i, h))]
    out_shape = [jax.ShapeDtypeStruct((batch * n, RET_HEADS * RET_DV), BF16)]
    if want_state:
        out_specs.append(pl.BlockSpec((1, 2, LANES, LANES), lambda h, i, b: (b, 0, h, 0)))
        out_shape.append(jax.ShapeDtypeStruct((batch, 2, RET_HEADS * RET_DK, RET_DV), F32))
    outs = pl.pallas_call(
        functools.partial(_ret_kernel, n=n, tq=tq, want_state=want_state),
        grid=(pairs, nqb, batch),
        in_specs=[pl.BlockSpec(memory_space=pltpu.SMEM),
                  pl.BlockSpec((tq, LANES), lambda h, i, b: (qb0 + b * nqb + i, q_blk + h)),
                  pl.BlockSpec((n, LANES), lambda h, i, b: (kb0 + b, k_blk + h)),
                  pl.BlockSpec((n, 2 * LANES), lambda h, i, b: (kb0 + b, v_blk // 2 + h)),
                  pl.BlockSpec((tq, 2 * LANES), lambda h, i, b: (qb0 + b * nqb + i, g_blk // 2 + h)),
                  pl.BlockSpec((1, 2, LANES, LANES), lambda h, i, b: (b, 0, h, 0)),
                  pl.BlockSpec((1, 2 * LANES), lambda h, i, b: (0, h))],
        out_specs=out_specs,
        out_shape=out_shape,
        scratch_shapes=[pltpu.VMEM((2, tq, n), F32)],
        compiler_params=_params("parallel", "parallel", "arbitrary"),
        name="retention",
    )(log_g, p, p, p, p, s0, gn)
    return outs if want_state else (outs[0], None)


def _diff_qk_kernel(q_ref, k_ref, v_ref, qn_ref, kn_ref, c_ref, s1_ref, s2_ref, qo_ref, ko_ref, dk_ref, dv_ref):
    tabs = (c_ref[...], s1_ref[...], s2_ref[...])
    is_ctx = jnp.logical_not(_latent_tile())
    for src, gain, dst in ((q_ref, qn_ref, qo_ref), (k_ref, kn_ref, ko_ref)):
        for h in range(DIFF_HEADS):
            sl = slice(h * LANES, (h + 1) * LANES)
            y = src[:, sl]
            y = y * lax.rsqrt(_seg64_sum(y * y) * (1.0 / DIFF_DH) + EPS) * gain[...]
            _store_rotated(dst, sl, y, tabs, _latent_tile())
            if dst is ko_ref:
                @pl.when(is_ctx)
                def _():
                    dk_ref[:, h, 0, :] = y[:, :DIFF_DH]
                    dk_ref[:, h, 1, :] = pltpu.roll(y, DIFF_DH, axis=1)[:, :DIFF_DH]
                    dv_ref[:, h, :] = v_ref[:, sl]


def _diff_qk(p, qn_p, kn_p, tabs):
    tm = _ROPE_TM
    n_ctx = N_CTX // tm
    tab_spec = pl.BlockSpec((tm, LANES), lambda i: (_rope_block(i), 0))
    ctx_blk = lambda i: jnp.minimum(i, n_ctx - 1)
    return pl.pallas_call(
        _diff_qk_kernel,
        grid=(N_TOK // tm,),
        in_specs=[pl.BlockSpec((tm, DIFF_W), lambda i: (i, 0)),
                  pl.BlockSpec((tm, DIFF_W), lambda i: (i, 1)),
                  pl.BlockSpec((tm, DIFF_W), lambda i: (i, 2)),
                  pl.BlockSpec((1, LANES), lambda i: (0, 0)),
                  pl.BlockSpec((1, LANES), lambda i: (0, 0)),
                  tab_spec, tab_spec, tab_spec],
        out_specs=[pl.BlockSpec((tm, DIFF_W), lambda i: (i, 0)),
                   pl.BlockSpec((tm, DIFF_W), lambda i: (i, 0)),
                   pl.BlockSpec((tm, DIFF_HEADS, 2, DIFF_DH), lambda i: (ctx_blk(i), 0, 0, 0)),
                   pl.BlockSpec((tm, DIFF_HEADS, 2 * DIFF_DH), lambda i: (ctx_blk(i), 0, 0))],
        out_shape=[jax.ShapeDtypeStruct((N_TOK, DIFF_W), F32)] * 2
        + [jax.ShapeDtypeStruct((N_CTX, DIFF_HEADS, 2, DIFF_DH), F32),
           jax.ShapeDtypeStruct((N_CTX, DIFF_HEADS, 2 * DIFF_DH), F32)],
        compiler_params=_params("arbitrary"),
        name="diff_qk",
    )(p, p, p, qn_p, kn_p, *tabs)


def _diff_attn_kernel(*refs, lam_init, heads):
    lam_ref, q_ref, gn_ref, o_ref = refs[0], refs[1], refs[-2], refs[-1]
    sources = [refs[2:4]] + ([refs[4:6]] if len(refs) == 8 else [])
    lv = lam_ref[...]
    lam = (jnp.exp(jnp.sum(lv[0:1] * lv[1:2], axis=-1, keepdims=True))
           - jnp.exp(jnp.sum(lv[2:3] * lv[3:4], axis=-1, keepdims=True)) + lam_init)
    scale = DIFF_DH ** -0.5
    lo = _lane_lo((1, LANES))
    for h in range(heads):
        sl = slice(h * LANES, (h + 1) * LANES)
        q = q_ref[:, sl]
        kbs = [k_ref[:, sl].astype(BF16) for k_ref, _ in sources]
        ps1, l1 = _softmax_parts([_dot_nt_bf16(jnp.where(lo, q, 0.0), kb) for kb in kbs], scale)
        ps2, l2 = _softmax_parts([_dot_nt_bf16(jnp.where(lo, 0.0, q), kb) for kb in kbs], scale)
        o = sum(_dot_bf16(p1 / l1 - lam * (p2 / l2), v_ref[:, sl])
                for p1, p2, (_, v_ref) in zip(ps1, ps2, sources))
        o_ref[:, sl] = (_rms(o, 2 * DIFF_DH) * gn_ref[:, sl] * (1.0 - lam_init)).astype(o_ref.dtype)


def _diff_attn(lam, q, k, v, v_blk0, cache, gn, batch, n, row0, tq, lam_init, heads):
    nqb = n // tq
    qb0 = row0 // tq
    kb0 = row0 // n
    wide = LANES * heads
    vb0 = v_blk0 // heads
    in_specs = [pl.BlockSpec((4, DIFF_DH), lambda b, h, i: (0, 0)),
                pl.BlockSpec((tq, wide), lambda b, h, i: (qb0 + b * nqb + i, h)),
                pl.BlockSpec((n, wide), lambda b, h, i: (kb0 + b, h)),
                pl.BlockSpec((n, wide), lambda b, h, i: (kb0 + b, vb0 + h))]
    operands = [lam, q, k, v]
    if cache is not None:
        in_specs += [pl.BlockSpec((PAST_LEN, wide), lambda b, h, i: (b, h))] * 2
        operands += list(cache)
    return pl.pallas_call(
        functools.partial(_diff_attn_kernel, lam_init=lam_init, heads=heads),
        grid=(batch, DIFF_HEADS // heads, nqb),
        in_specs=in_specs + [pl.BlockSpec((1, wide), lambda b, h, i: (0, h))],
        out_specs=pl.BlockSpec((tq, wide), lambda b, h, i: (b * nqb + i, h)),
        out_shape=jax.ShapeDtypeStruct((batch * n, DIFF_W), BF16),
        compiler_params=_params("parallel", "parallel", "arbitrary"),
        name="diff_attn",
    )(*operands, gn)


def _seg64_sum_wide(x):
    return jnp.concatenate([_seg64_sum(x[:, j * LANES:(j + 1) * LANES]) for j in range(x.shape[1] // LANES)], axis=1)


_SCAN_SLOTS = 3
_SCAN_NJ = _SCAN_SLOTS * RWKV_HEADS


def _rwkv_pre_kernel(r_ref, k_ref, v_ref, lo_ref, wup_ref, aup_ref, gup_ref, w0_ref, a0_ref, kk_ref, ka_ref, rk_ref,
                     op_ref, g_ref, bonus_ref):
    W = RWKV_W
    col = lambda q: slice(q * W, (q + 1) * W)
    r = r_ref[...]
    k = k_ref[...]
    v = v_ref[...]
    lora = lo_ref[...]
    kk = k * kk_ref[...]
    kkn = kk * lax.rsqrt(_seg64_sum_wide(kk * kk) + EPS)
    g_ref[...] = _dot_bf16(_sigmoid(lora[:, 2 * LANES:3 * LANES]), gup_ref[...])
    pre = w0_ref[...] + _dot_bf16(jnp.tanh(lora[:, 0:LANES]), wup_ref[...])
    decay = jnp.exp(-jnp.exp(-_softplus(-pre) - 0.5))
    a = _sigmoid(a0_ref[...] + _dot_bf16(lora[:, LANES:2 * LANES], aup_ref[...]))
    lo = _lane_lo((1, LANES))
    bonus = None
    for d in range(2):
        a_d = a[:, col(d)]
        k_d = k * (1.0 + (a_d - 1.0) * ka_ref[...])
        t = _seg64_sum_wide(r * k_d * rk_ref[...])
        bonus = t if bonus is None else bonus + t
        for s, (x1, x2) in enumerate(((kkn, decay[:, col(d)]), (k_d, kkn * a_d), (r, v))):
            for h in range(RWKV_HEADS):
                blk = slice((h // 2) * LANES, (h // 2 + 1) * LANES)
                if h % 2 == 0:
                    out = jnp.where(lo, x1[:, blk], pltpu.roll(x2[:, blk], RWKV_HS, axis=1))
                else:
                    out = jnp.where(lo, pltpu.roll(x1[:, blk], RWKV_HS, axis=1), x2[:, blk])
                op_ref[:, d * _SCAN_NJ + s * RWKV_HEADS + h, :] = out
    bonus_ref[...] = bonus


def _rwkv_pre(p, r_blk, k_blk, v_blk, lo_blk, wup_bd, aup_bd, gup, w0, a0, k_k, k_a, r_k):
    tm = 256
    w = RWKV_W
    row = lambda n: pl.BlockSpec((1, n), lambda i: (0, 0))
    full = lambda a, b: pl.BlockSpec((a, b), lambda i: (0, 0))
    return pl.pallas_call(
        _rwkv_pre_kernel,
        grid=(N_TOK // tm,),
        in_specs=[pl.BlockSpec((tm, w), lambda i: (i, r_blk)),
                  pl.BlockSpec((tm, w), lambda i: (i, k_blk)),
                  pl.BlockSpec((tm, w), lambda i: (i, v_blk)),
                  pl.BlockSpec((tm, 3 * LANES), lambda i: (i, lo_blk)),
                  full(LANES, 2 * w), full(LANES, 2 * w), full(LANES, w),
                  row(2 * w), row(2 * w), row(w), row(w), row(w)],
        out_specs=[pl.BlockSpec((tm, 2 * _SCAN_NJ, LANES), lambda i: (i, 0, 0)),
                   pl.BlockSpec((tm, w), lambda i: (i, 0)), pl.BlockSpec((tm, w), lambda i: (i, 0))],
        out_shape=[jax.ShapeDtypeStruct((N_TOK, 2 * _SCAN_NJ, LANES), F32),
                   jax.ShapeDtypeStruct((N_TOK, w), F32), jax.ShapeDtypeStruct((N_TOK, w), F32)],
        compiler_params=_params("parallel"),
        name="rwkv_pre",
    )(p, p, p, p, wup_bd, aup_bd, gup, w0, a0, k_k, k_a, r_k)


_SCAN_CHUNK = 32
_SCAN_UNROLL = 16


def _rwkv_first_sa(s_ref, sa_ref, kk, n_k):
    nv = s_ref.shape[1]
    chunk = min(_SCAN_CHUNK, nv)
    for c0 in range(0, nv, chunk):
        def body(k, acc):
            return acc + s_ref[k, c0:c0 + chunk, :] * kk(k)
        sa_ref[c0:c0 + chunk, :] = lax.fori_loop(0, n_k, body, jnp.zeros((chunk, LANES), F32), unroll=_SCAN_UNROLL)


def _rwkv_step(s_ref, sa_ref, kk_next, w, kd, b, r, v_at, n_k):
    nv = s_ref.shape[1]
    chunk = min(_SCAN_CHUNK, nv)
    ys = []
    for c0 in range(0, nv, chunk):
        sa = sa_ref[c0:c0 + chunk, :]
        vc = v_at(c0, chunk)

        def body(k, acc):
            y_acc, sa_acc = acc
            s_new = s_ref[k, c0:c0 + chunk, :] * w(k) - sa * b(k) + vc * kd(k)
            s_ref[k, c0:c0 + chunk, :] = s_new
            return y_acc + s_new * r(k), sa_acc + s_new * kk_next(k)

        zero = jnp.zeros((chunk, LANES), F32)
        y_acc, sa_acc = lax.fori_loop(0, n_k, body, (zero, zero), unroll=_SCAN_UNROLL)
        sa_ref[c0:c0 + chunk, :] = sa_acc
        ys.append(y_acc)
    return ys[0] if len(ys) == 1 else jnp.concatenate(ys, axis=0)


_CTX_TB = 32


def _load_scan_operands(kt_ref, t, slabs):
    for s, x in enumerate(slabs):
        xt = x.T
        kt_ref[t, 2 * s] = xt[:RWKV_HS]
        kt_ref[t, 2 * s + 1] = xt[RWKV_HS:]


def _rwkv_scan_ctx_kernel(x_ref, y_ref, st_ref, s_ref, sa_ref, kt_ref, ys_ref):
    d = pl.program_id(0)
    tb = pl.program_id(1)
    nh = RWKV_HEADS
    step_t = lambda i: jnp.where(d == 0, i, _CTX_TB - 1 - i)

    @pl.when(tb == 0)
    def _():
        s_ref[...] = jnp.zeros_like(s_ref)

    def load_t(t, carry):
        _load_scan_operands(kt_ref, t, [
            jnp.concatenate([x_ref[b, t, s * nh:(s + 1) * nh, :] for b in range(BATCH)], axis=0)
            for s in range(_SCAN_SLOTS)])
        return carry

    lax.fori_loop(0, _CTX_TB, load_t, 0, unroll=4)

    t0 = step_t(0)
    _rwkv_first_sa(s_ref, sa_ref, lambda k: kt_ref[t0, 0, pl.ds(k, 1), :], RWKV_HS)

    def step(i, carry):
        t = step_t(i)
        tn = step_t(jnp.minimum(i + 1, _CTX_TB - 1))
        row = lambda q, tt: (lambda k: kt_ref[tt, q, pl.ds(k, 1), :])
        ys_ref[t] = _rwkv_step(s_ref, sa_ref, row(0, tn), row(1, t), row(2, t), row(3, t), row(4, t),
                               lambda c0, n: kt_ref[t, 5, pl.ds(c0, n), :], RWKV_HS)
        return carry

    lax.fori_loop(0, _CTX_TB, step, 0)

    def store_t(i, carry):
        z = jnp.concatenate([ys_ref[2 * i], ys_ref[2 * i + 1]], axis=0).T
        z_odd = pltpu.roll(z, RWKV_HS, axis=1)
        for b in range(BATCH):
            y_ref[b, 2 * i] = z[b * nh:(b + 1) * nh, :RWKV_HS]
            y_ref[b, 2 * i + 1] = z_odd[b * nh:(b + 1) * nh, :RWKV_HS]
        return carry

    lax.fori_loop(0, _CTX_TB // 2, store_t, 0, unroll=4)

    @pl.when(tb == pl.num_programs(1) - 1)
    def _():
        st_ref[0] = s_ref[...]


def _rwkv_scan_ctx(op4):
    nt = SEQ // _CTX_TB
    hs = RWKV_HS
    tblk = lambda d, tb: jnp.where(d == 0, tb, nt - 1 - tb)
    return pl.pallas_call(
        _rwkv_scan_ctx_kernel,
        grid=(2, nt),
        in_specs=[pl.BlockSpec((BATCH, _CTX_TB, _SCAN_NJ, LANES), lambda d, tb: (0, tblk(d, tb), d, 0))],
        out_specs=[pl.BlockSpec((BATCH, _CTX_TB, RWKV_HEADS, hs), lambda d, tb: (0, tblk(d, tb), d, 0)),
                   pl.BlockSpec((1, hs, hs, LANES), lambda d, tb: (d, 0, 0, 0))],
        out_shape=[jax.ShapeDtypeStruct((BATCH, SEQ, 2 * RWKV_HEADS, hs), F32),
                   jax.ShapeDtypeStruct((2, hs, hs, LANES), F32)],
        scratch_shapes=[pltpu.VMEM((hs, hs, LANES), F32), pltpu.VMEM((hs, LANES), F32),
                        pltpu.VMEM((_CTX_TB, 2 * _SCAN_SLOTS, hs, LANES), F32),
                        pltpu.VMEM((_CTX_TB, hs, LANES), F32)],
        compiler_params=_params("parallel", "arbitrary"),
        name="rwkv_scan_ctx",
    )(op4)


_LAT_TB = 64
_LAT_VSPLIT = 4
_LAT_STATES = 2 * DEC_BATCH * RWKV_HEADS
_LAT_VROWS = RWKV_HS // _LAT_VSPLIT


def _rwkv_scan_lat_kernel(xf0_ref, xf1_ref, xb0_ref, xb1_ref, s0_ref, yf_ref, yb_ref,
                          s_ref, sa_ref, kt_ref, v_ref, ys_ref):
    @pl.when(pl.program_id(0) == 0)
    def _():
        s_ref[...] = s0_ref[...]

    group = lax.broadcasted_iota(jnp.int32, (_LAT_VROWS, LANES), 1) // _LAT_STATES
    nh = RWKV_HEADS

    def load_t(t, carry):
        tr = _LAT_TB - 1 - t
        for s in range(_SCAN_SLOTS):
            heads = slice(s * nh, (s + 1) * nh)
            x = jnp.concatenate([xf0_ref[0, t, heads, :], xf1_ref[0, t, heads, :],
                                 xb0_ref[0, tr, heads, :], xb1_ref[0, tr, heads, :]], axis=0)
            xt = jnp.concatenate([x] * _LAT_VSPLIT, axis=0).T
            kt_ref[t, 2 * s] = xt[:RWKV_HS]
            if s < _SCAN_SLOTS - 1:
                kt_ref[t, 2 * s + 1] = xt[RWKV_HS:]
            else:
                v = jnp.zeros((_LAT_VROWS, LANES), F32)
                for g in range(_LAT_VSPLIT):
                    r0 = RWKV_HS + g * _LAT_VROWS
                    v = jnp.where(group == g, xt[r0:r0 + _LAT_VROWS, :], v)
                v_ref[t] = v
        return carry

    lax.fori_loop(0, _LAT_TB, load_t, 0, unroll=8)

    _rwkv_first_sa(s_ref, sa_ref, lambda k: kt_ref[0, 0, pl.ds(k, 1), :], RWKV_HS)

    def step(t, carry):
        tn = jnp.minimum(t + 1, _LAT_TB - 1)
        row = lambda q, tt: (lambda k: kt_ref[tt, q, pl.ds(k, 1), :])
        ys_ref[t] = _rwkv_step(s_ref, sa_ref, row(0, tn), row(1, t), row(2, t), row(3, t), row(4, t),
                               lambda c0, n: v_ref[t, pl.ds(c0, n), :], RWKV_HS)
        return carry

    lax.fori_loop(0, _LAT_TB, step, 0)

    def store_t(i, carry):
        rows = [jnp.where(group == g, ys_ref[2 * i + j], 0.0) for j in range(2) for g in range(_LAT_VSPLIT)]
        z = jnp.concatenate(rows, axis=0).T
        y = (z[0:_LAT_STATES] + z[_LAT_STATES:2 * _LAT_STATES]
             + z[2 * _LAT_STATES:3 * _LAT_STATES] + z[3 * _LAT_STATES:4 * _LAT_STATES])
        y_odd = pltpu.roll(y, RWKV_HS, axis=1)
        half = _LAT_STATES // 2
        for b in range(DEC_BATCH):
            rows_f = slice(b * nh, (b + 1) * nh)
            rows_b = slice(half + b * nh, half + (b + 1) * nh)
            yf_ref[b, 2 * i] = y[rows_f, :RWKV_HS]
            yf_ref[b, 2 * i + 1] = y_odd[rows_f, :RWKV_HS]
            yb_ref[b, _LAT_TB - 1 - 2 * i] = y[rows_b, :RWKV_HS]
            yb_ref[b, _LAT_TB - 2 - 2 * i] = y_odd[rows_b, :RWKV_HS]
        return carry

    lax.fori_loop(0, _LAT_TB // 2, store_t, 0, unroll=4)


def _rwkv_scan_lat(op4, s0):
    hs = RWKV_HS
    nv = _LAT_VROWS
    nt = DEC_SEQ // _LAT_TB
    per_seq = SEQ // _LAT_TB
    first = N_CTX // SEQ

    def x_spec(b, d):
        tblk = (lambda tb: tb) if d == 0 else (lambda tb: nt - 1 - tb)
        return pl.BlockSpec((1, _LAT_TB, _SCAN_NJ, LANES),
                            lambda tb: (first + b * (DEC_SEQ // SEQ) + tblk(tb) // per_seq, tblk(tb) % per_seq, d, 0))

    y_shape = jax.ShapeDtypeStruct((DEC_BATCH, DEC_SEQ, RWKV_HEADS, hs), F32)
    y_blk = (DEC_BATCH, _LAT_TB, RWKV_HEADS, hs)
    return pl.pallas_call(
        _rwkv_scan_lat_kernel,
        grid=(nt,),
        in_specs=[x_spec(0, 0), x_spec(1, 0), x_spec(0, 1), x_spec(1, 1),
                  pl.BlockSpec((hs, nv, LANES), lambda tb: (0, 0, 0))],
        out_specs=[pl.BlockSpec(y_blk, lambda tb: (0, tb, 0, 0)),
                   pl.BlockSpec(y_blk, lambda tb: (0, nt - 1 - tb, 0, 0))],
        out_shape=[y_shape, y_shape],
        scratch_shapes=[pltpu.VMEM((hs, nv, LANES), F32), pltpu.VMEM((nv, LANES), F32),
                        pltpu.VMEM((_LAT_TB, 2 * _SCAN_SLOTS - 1, hs, LANES), F32),
                        pltpu.VMEM((_LAT_TB, nv, LANES), F32), pltpu.VMEM((_LAT_TB, nv, LANES), F32)],
        compiler_params=_params("arbitrary"),
        name="rwkv_scan_lat",
    )(op4, op4, op4, op4, s0)


def _rwkv_post_kernel(yc_ref, ylf_ref, ylb_ref, bonus_ref, v_ref, g_ref, gn_ref, o_ref):
    def finish(head_sum):
        y = jnp.concatenate([head_sum(h) for h in range(RWKV_HEADS)], axis=1)
        y = y * lax.rsqrt(_seg64_sum_wide(y * y) * (1.0 / RWKV_HS) + EPS) * gn_ref[...]
        o_ref[...] = ((y + bonus_ref[...] * v_ref[...]) * g_ref[...]).astype(o_ref.dtype)

    @pl.when(pl.program_id(0) < N_CTX // SEQ)
    def _():
        finish(lambda h: yc_ref[0, :, h, :] + yc_ref[0, :, RWKV_HEADS + h, :])

    @pl.when(pl.program_id(0) >= N_CTX // SEQ)
    def _():
        finish(lambda h: ylf_ref[0, :, h, :] + ylb_ref[0, :, h, :])


def _rwkv_post(y_ctx, y_lat_f, y_lat_b, bonus, p, v_blk, g, gn):
    tm = SEQ
    w = RWKV_W
    hs = RWKV_HS
    n_ctx = N_CTX // SEQ
    lat = lambda y: y.reshape(N_LAT // SEQ, SEQ, RWKV_HEADS, hs)
    spec = pl.BlockSpec((tm, w), lambda i: (i, 0))
    lat_spec = pl.BlockSpec((1, SEQ, RWKV_HEADS, hs), lambda i: (jnp.maximum(i - n_ctx, 0), 0, 0, 0))
    return pl.pallas_call(
        _rwkv_post_kernel,
        grid=(N_TOK // tm,),
        in_specs=[pl.BlockSpec((1, SEQ, 2 * RWKV_HEADS, hs), lambda i: (jnp.minimum(i, n_ctx - 1), 0, 0, 0)),
                  lat_spec, lat_spec, spec, pl.BlockSpec((tm, w), lambda i: (i, v_blk)), spec,
                  pl.BlockSpec((1, w), lambda i: (0, 0))],
        out_specs=spec,
        out_shape=jax.ShapeDtypeStruct((N_TOK, w), BF16),
        compiler_params=_params("parallel"),
        name="rwkv_post",
    )(y_ctx, lat(y_lat_f), lat(y_lat_b), bonus, p, g, gn)


def _value_split_layout(x):
    lead = x.shape[:-2]
    n = len(lead)
    x = x.reshape(lead + (_LAT_STATES, _LAT_VSPLIT, _LAT_VROWS))
    return jnp.transpose(x, tuple(range(n)) + (n + 2, n + 1, n)).reshape(lead + (_LAT_VROWS, LANES))


_EVEN_HEAD = MLA_Q_RANK + MLA_KV_RANK
_EVEN_PACKED = _EVEN_HEAD + LANES + 2 * RET_HEADS * (RET_DK + RET_DV)


def _pack_even_weight_kernel(w_ref, o_ref):
    w = w_ref[...]
    o_ref[:, 0:_EVEN_HEAD] = w[:, 0:_EVEN_HEAD]
    o_ref[:, _EVEN_HEAD:_EVEN_HEAD + LANES] = jnp.zeros((w.shape[0], LANES), F32)
    o_ref[:, _EVEN_HEAD + MLA_NOPE:_EVEN_HEAD + MLA_QK] = w[:, _EVEN_HEAD:_EVEN_HEAD + MLA_ROPE]
    o_ref[:, _EVEN_HEAD + LANES:] = w[:, _EVEN_HEAD + MLA_ROPE:]


def _pack_even_weight(w_in):
    tm = 256
    return pl.pallas_call(
        _pack_even_weight_kernel,
        grid=(D_MODEL // tm,),
        in_specs=[pl.BlockSpec((tm, w_in.shape[1]), lambda i: (i, 0))],
        out_specs=pl.BlockSpec((tm, _EVEN_PACKED), lambda i: (i, 0)),
        out_shape=jax.ShapeDtypeStruct((D_MODEL, _EVEN_PACKED), F32),
        compiler_params=_params("parallel"),
        name="pack_even_weight",
    )(w_in)


def _even_layer(x, mod, g_mix, w_in, q_norm, kv_norm, w_uq, w_ukv, qn, kn, ret_decay, ret_gn,
                cache_ckv, cache_krope, state_ret, tabs_m):
    w_p = _pack_even_weight(w_in)
    p = _inproj(x, g_mix, mod, w_p, jnp.zeros((1, w_p.shape[1]), F32), 1024, None)
    CKV_BLK, KR_BLK, RQ_BLK, RK_BLK, RV_BLK, RG_BLK = 2, 3, 4, 6, 8, 12

    def head_pad(w, n_head, d_head, c0, c1):
        w = w.reshape(w.shape[0], n_head, d_head)[:, :, c0:c1]
        return jnp.pad(w, ((0, 0), (0, 0), (0, LANES - (c1 - c0)))).reshape(w.shape[0], n_head * LANES)

    w_uq_p = head_pad(w_uq, MLA_HEADS, MLA_QK, 0, MLA_QK)
    wk_p = head_pad(w_ukv, MLA_HEADS, MLA_NOPE + MLA_V, 0, MLA_NOPE)
    wv_p = head_pad(w_ukv, MLA_HEADS, MLA_NOPE + MLA_V, MLA_NOPE, MLA_NOPE + MLA_V)
    qn_p = jnp.pad(qn, (0, LANES - MLA_QK))[None]
    kn_p = jnp.pad(kn, (0, LANES - MLA_QK))[None]

    q = _mla_q(p, q_norm[None], w_uq_p, qn_p, tabs_m)
    k, v, ckvn = _mla_kv(p, CKV_BLK, p, KR_BLK, kv_norm[None], wk_p, wv_p, kn_p, tabs_m, N_TOK, True)

    n_c = DEC_BATCH * PAST_LEN
    kr_c = jnp.pad(cache_krope.reshape(n_c, MLA_ROPE), ((0, 0), (MLA_NOPE, LANES - MLA_QK)))
    k_c, v_c, _ = _mla_kv(cache_ckv.reshape(n_c, MLA_KV_RANK), 0, kr_c, 0, kv_norm[None], wk_p, wv_p, kn_p,
                          tabs_m, n_c, False)

    o_ctx = _mla_attn(q, k, v, None, BATCH, SEQ, 0, SEQ, MLA_HEADS // 2)
    o_lat = _mla_attn(q, k, v, (k_c, v_c), DEC_BATCH, DEC_SEQ, N_CTX, 256, 1)

    log_g = -_softplus(-ret_decay)
    gn = ret_gn[None]
    s0_ctx = jnp.zeros((BATCH, 2, RET_HEADS * RET_DK, RET_DV), F32)
    r_ctx, st_ctx = _retention(log_g, p, RQ_BLK, RK_BLK, RV_BLK, RG_BLK, s0_ctx, gn, BATCH, SEQ, 0, SEQ, True)
    s0_lat = state_ret.reshape(DEC_BATCH, 2, RET_HEADS * RET_DK, RET_DV)
    r_lat, _ = _retention(log_g, p, RQ_BLK, RK_BLK, RV_BLK, RG_BLK, s0_lat, gn, DEC_BATCH, DEC_SEQ, N_CTX, 256,
                          False)

    mix = [(o_ctx, o_lat), (r_ctx, r_lat)]
    new_ckv = ckvn[:N_CTX].reshape(BATCH, SEQ, MLA_KV_RANK)
    new_krope = p[:N_CTX, KR_BLK * LANES + MLA_NOPE:KR_BLK * LANES + MLA_QK]
    new_krope = new_krope.reshape(BATCH, SEQ, MLA_ROPE)
    new_ret = st_ctx.reshape(BATCH, 2, RET_HEADS, RET_DK, RET_DV)
    return mix, new_ckv, new_krope, new_ret


def _odd_layer(x, mod, g_mix, w_in, qn, kn, lam, diff_gn, mu, w0, w_up, a0, a_up, g_up, k_k, k_a, r_k, gn,
               cache_k, cache_v, state_rwkv, tabs_d, lam_init):
    w_p = w_in
    n_in = w_p.shape[1]
    mu_full = jnp.concatenate([jnp.zeros((3 * DIFF_W,), F32), mu])[None]
    p = _inproj(x, g_mix, mod, w_p, mu_full, 384, (3 * DIFF_W) // 384)
    DV_BLK, R_BLK, K_BLK, V_BLK = 2, 3, 4, 5
    LO_BLK = (6 * RWKV_W) // (3 * LANES)

    qn_p = jnp.tile(qn, 2)[None]
    kn_p = jnp.tile(kn, 2)[None]
    q, k, dk, dv = _diff_qk(p, qn_p, kn_p, tabs_d)

    n_c = DEC_BATCH * PAST_LEN
    cache = (cache_k.reshape(n_c, DIFF_W), cache_v.reshape(n_c, DIFF_W))
    dgn = diff_gn[None]
    o_ctx = _diff_attn(lam, q, k, p, DV_BLK * DIFF_HEADS, None, dgn, BATCH, SEQ, 0, SEQ, lam_init, DIFF_HEADS)
    o_lat = _diff_attn(lam, q, k, p, DV_BLK * DIFF_HEADS, cache, dgn, DEC_BATCH, DEC_SEQ, N_CTX, 256, lam_init, 1)

    zero = jnp.zeros((RWKV_W_LORA, RWKV_W), F32)
    wup_bd = jnp.concatenate([jnp.concatenate([w_up[0], zero], 1), jnp.concatenate([zero, w_up[1]], 1)], 0)
    aup_bd = jnp.concatenate([jnp.concatenate([a_up[0], zero], 1), jnp.concatenate([zero, a_up[1]], 1)], 0)
    op, g, bonus = _rwkv_pre(p, R_BLK, K_BLK, V_BLK, LO_BLK, wup_bd, aup_bd, g_up, w0.reshape(1, -1),
                             a0.reshape(1, -1), k_k[None], k_a[None], r_k.reshape(1, -1))
    op4 = op.reshape(N_TOK // SEQ, SEQ, 2 * _SCAN_NJ, LANES)
    y_ctx, st_ctx = _rwkv_scan_ctx(op4)
    s0_lat = jnp.transpose(state_rwkv, (4, 1, 0, 2, 3)).reshape(RWKV_HS, _LAT_STATES, RWKV_HS)
    y_lat_f, y_lat_b = _rwkv_scan_lat(op4, _value_split_layout(s0_lat))
    rw_o = _rwkv_post(y_ctx, y_lat_f, y_lat_b, bonus, p, V_BLK, g, gn[None])

    mix = [(o_ctx, o_lat), rw_o]
    new_dk = dk.reshape(BATCH, SEQ, DIFF_HEADS, 2, DIFF_DH)
    new_dv = dv.reshape(BATCH, SEQ, DIFF_HEADS, 2 * DIFF_DH)
    new_rwkv = jnp.transpose(st_ctx.reshape(2, RWKV_HS, RWKV_HS, BATCH, RWKV_HEADS), (3, 0, 4, 2, 1))
    return mix, new_dk, new_dv, new_rwkv


def kernel(x_prompt, x_sample, cache_mla_ckv, cache_mla_krope, state_ret, cache_diff_k, cache_diff_v, state_rwkv,
           c, c_ctx, ada_w, ada_b, norm_mix_g, norm_ffn_g, w_out, ffn_up, ffn_conv_w, ffn_conv_b, ffn_down,
           a_w_in, mla_q_norm, mla_kv_norm, mla_w_uq, mla_w_ukv, mla_qn, mla_kn, ret_decay, ret_gn,
           b_w_in, diff_qn, diff_kn, diff_lam, diff_gn, rwkv_mu, rwkv_w0, rwkv_w_up, rwkv_a0, rwkv_a_up,
           rwkv_g_up, rwkv_k_k, rwkv_k_a, rwkv_r_k, rwkv_gn):
    x = (x_prompt.reshape(N_CTX, D_MODEL), x_sample.reshape(N_LAT, D_MODEL))
    cond8 = jnp.pad(jnp.concatenate([c_ctx[None], c], 0), ((0, 8 - N_GROUPS), (0, 0)))
    mod = _modulation(cond8, ada_w, ada_b)

    tabs_m = _rope_tables(MLA_ROPE, (MLA_NOPE,))
    tabs_d = _rope_tables(DIFF_DH, (0, DIFF_DH))

    outs = {}
    for l in range(DEPTH):
        j = l // 2
        g_mix = norm_mix_g[l][None]
        if l % 2 == 0:
            mix, outs["ckv"], outs["krope"], outs["ret"] = _even_layer(
                x, mod[l], g_mix, a_w_in[j], mla_q_norm[j], mla_kv_norm[j], mla_w_uq[j], mla_w_ukv[j], mla_qn[j],
                mla_kn[j], ret_decay[j], ret_gn[j], cache_mla_ckv[:, j], cache_mla_krope[:, j], state_ret[:, j],
                tabs_m)
        else:
            lam_init = 0.8 - 0.6 * math.exp(-0.3 * l)
            mix, outs["dk"], outs["dv"], outs["rwkv"] = _odd_layer(
                x, mod[l], g_mix, b_w_in[j], diff_qn[j], diff_kn[j], diff_lam[j], diff_gn[j], rwkv_mu[j],
                rwkv_w0[j], rwkv_w_up[j], rwkv_a0[j], rwkv_a_up[j], rwkv_g_up[j], rwkv_k_k[j], rwkv_k_a[j],
                rwkv_r_k[j], rwkv_gn[j], cache_diff_k[:, j], cache_diff_v[:, j], state_rwkv[:, j], tabs_d, lam_init)
        x = _resid_proj(mix, w_out, l, x, mod[l], 2)
        act = _ffn_up(x, norm_ffn_g[l][None], mod[l], ffn_up, l, ffn_conv_w[l], ffn_conv_b[l])
        x = _resid_proj([act], ffn_down, l, x, mod[l], 5, split_out=(l == DEPTH - 1))

    y_prompt = x[0].reshape(BATCH, SEQ, D_MODEL)
    y_sample = x[1].reshape(DEC_BATCH, DEC_SEQ, D_MODEL)
    return (y_prompt, y_sample, outs["ckv"][:, None], outs["krope"][:, None], outs["ret"][:, None],
            outs["dk"][:, None], outs["dv"][:, None], outs["rwkv"][:, None])
```

```python
import functools
import math

import numpy as np
import jax
import jax.numpy as jnp
from jax import lax
from jax.experimental import pallas as pl
from jax.experimental.pallas import tpu as pltpu

D_MODEL = 1024
BATCH = 16
SEQ = 256
DEPTH = 2
DEC_BATCH = 2
DEC_SEQ = 1024
PAST_LEN = 512
GRID_W = 64
EPS = 1e-6
ROPE_BASE = 10000.0

MLA_HEADS = 8
MLA_Q_RANK = 256
MLA_KV_RANK = 128
MLA_NOPE = 64
MLA_ROPE = 32
MLA_V = 64
MLA_QK = MLA_NOPE + MLA_ROPE
RET_HEADS = 4
RET_DK = 64
RET_DV = 128
DIFF_HEADS = 4
DIFF_DH = 64
DIFF_W = DIFF_HEADS * 2 * DIFF_DH
RWKV_HEADS = 8
RWKV_HS = 64
RWKV_W = RWKV_HEADS * RWKV_HS
RWKV_W_LORA = 64
D_FF = 2816

N_CTX = BATCH * SEQ
N_LAT = DEC_BATCH * DEC_SEQ
N_TOK = N_CTX + N_LAT
N_GROUPS = 1 + DEC_BATCH

LANES = 128
VMEM_LIMIT = 56 * 1024 * 1024

_PREC = lax.Precision.HIGHEST
F32 = jnp.float32


def _dot_tn(a, b):
    return lax.dot_general(a, b, (((0,), (0,)), ((), ())), precision=_PREC, preferred_element_type=F32)


BF16 = jnp.bfloat16


def _dot_bf16(a, b):
    return jnp.dot(a.astype(BF16), b.astype(BF16), preferred_element_type=F32)


def _dot_nt_bf16(a, b):
    return lax.dot_general(a.astype(BF16), b.astype(BF16), (((1,), (1,)), ((), ())), preferred_element_type=F32)


def _params(*sem):
    return pltpu.CompilerParams(dimension_semantics=sem, vmem_limit_bytes=VMEM_LIMIT)


def _sigmoid(x):
    return 1.0 / (1.0 + jnp.exp(-x))


def _silu(x):
    return x * _sigmoid(x)


def _softplus(x):
    return jnp.maximum(x, 0.0) + jnp.log(1.0 + jnp.exp(-jnp.abs(x)))


def _rms(x, n):
    return x * lax.rsqrt(jnp.sum(x * x, axis=-1, keepdims=True) * (1.0 / n) + EPS)


def _lane_lo(shape):
    return lax.broadcasted_iota(jnp.int32, shape, len(shape) - 1) < 64


def _seg64_sum(x):
    lo = _lane_lo(x.shape)
    s_lo = jnp.sum(jnp.where(lo, x, 0.0), axis=-1, keepdims=True)
    s_hi = jnp.sum(jnp.where(lo, 0.0, x), axis=-1, keepdims=True)
    return jnp.where(lo, s_lo, s_hi)


_SUBLANES = 8


def _seq_neighbours(p, tile, tile_rows):
    is_ctx = tile * tile_rows < N_CTX
    sub = lax.broadcasted_iota(jnp.int32, (_SUBLANES, 1), 0)

    def shifted(rolled, edge_sublane, group_of_seq):
        pieces, start = [], 0
        for q in range(tile_rows // SEQ):
            g0 = q * SEQ + group_of_seq
            outer = (q == 0) if group_of_seq == 0 else (q == tile_rows // SEQ - 1)
            edge = (sub == edge_sublane) if outer else ((sub == edge_sublane) & is_ctx)
            pieces += [rolled[start:g0], jnp.where(edge, 0.0, rolled[g0:g0 + _SUBLANES])]
            start = g0 + _SUBLANES
        pieces.append(rolled[start:])
        return jnp.concatenate([x for x in pieces if x.shape[0]], axis=0)

    prev = shifted(pltpu.roll(p, 1, axis=0), 0, 0)
    nxt = shifted(pltpu.roll(p, tile_rows - 1, axis=0), _SUBLANES - 1, SEQ - _SUBLANES)
    return prev, nxt


def _group_of_tile(i, tile_rows):
    row = i * tile_rows
    return jnp.where(row < N_CTX, 0, 1 + (row - N_CTX) // DEC_SEQ)


def _modulation_kernel(c_ref, w_ref, b_ref, o_ref):
    o_ref[0] = _dot_bf16(_silu(c_ref[...]), w_ref[0]) + b_ref[0]


def _modulation(cond8, ada_w, ada_b):
    tn = 512
    n = 6 * D_MODEL
    out = pl.pallas_call(
        _modulation_kernel,
        grid=(DEPTH, n // tn),
        in_specs=[pl.BlockSpec((8, D_MODEL), lambda l, j: (0, 0)),
                  pl.BlockSpec((1, D_MODEL, tn), lambda l, j: (l, 0, j)),
                  pl.BlockSpec((1, 1, tn), lambda l, j: (l, 0, j))],
        out_specs=pl.BlockSpec((1, 8, tn), lambda l, j: (l, 0, j)),
        out_shape=jax.ShapeDtypeStruct((DEPTH, 8, n), F32),
        compiler_params=_params("parallel", "parallel"),
        name="modulation",
    )(cond8, ada_w, ada_b.reshape(DEPTH, 1, n))
    m = out[:, :N_GROUPS].reshape(DEPTH, N_GROUPS, 6, D_MODEL)
    return jnp.pad(m, ((0, 0), (0, 0), (0, 2), (0, 0)))


_TM_SEQ = 1024


def _norm_mod(x, g, mod, off):
    return _rms(x, D_MODEL) * g * (1.0 + mod[off + 1:off + 2, :]) + mod[off:off + 1, :]


def _row_split_specs(tm, width_block, col):
    n_ctx = N_CTX // tm
    return [pl.BlockSpec((tm, width_block), lambda i, j: (jnp.minimum(i, n_ctx - 1), col(j))),
            pl.BlockSpec((tm, width_block), lambda i, j: (jnp.maximum(i - n_ctx, 0), col(j)))]


def _for_row_tile(i, tm, ctx_ref, lat_ref, fn):
    @pl.when(i < N_CTX // tm)
    def _():
        fn(ctx_ref)

    @pl.when(i >= N_CTX // tm)
    def _():
        fn(lat_ref)


def _inproj_kernel(*refs, shift_from, split_x):
    x_refs, (g_ref, mod_ref, w_ref, mu_ref, o_ref, h_ref) = refs[:-6], refs[-6:]
    i = pl.program_id(0)

    @pl.when(pl.program_id(1) == 0)
    def _():
        def norm(x_ref):
            h_ref[...] = _norm_mod(x_ref[...], g_ref[...], mod_ref[0], 0).astype(BF16)

        if split_x:
            _for_row_tile(i, _TM_SEQ, x_refs[0], x_refs[1], norm)
        else:
            norm(x_refs[0])

    p = _dot_bf16(h_ref[...], w_ref[...])
    if shift_from is None:
        o_ref[...] = p
    else:
        @pl.when(pl.program_id(1) < shift_from)
        def _():
            o_ref[...] = p

        @pl.when(pl.program_id(1) >= shift_from)
        def _():
            prev, nxt = _seq_neighbours(p, i, _TM_SEQ)
            o_ref[...] = p + (0.5 * (prev + nxt) - p) * mu_ref[...]


def _inproj(x, g, mod, w, mu, tn, shift_from):
    n = w.shape[1]
    tm = _TM_SEQ
    split_x = isinstance(x, (tuple, list))
    x_specs = _row_split_specs(tm, D_MODEL, lambda j: 0) if split_x else [pl.BlockSpec((tm, D_MODEL), lambda i, j: (i, 0))]
    return pl.pallas_call(
        functools.partial(_inproj_kernel, shift_from=shift_from, split_x=split_x),
        grid=(N_TOK // tm, n // tn),
        in_specs=x_specs
        + [pl.BlockSpec((1, D_MODEL), lambda i, j: (0, 0)),
                  pl.BlockSpec((1, 8, D_MODEL), lambda i, j: (_group_of_tile(i, tm), 0, 0)),
                  pl.BlockSpec((D_MODEL, tn), lambda i, j: (0, j)),
                  pl.BlockSpec((1, tn), lambda i, j: (0, j))],
        out_specs=pl.BlockSpec((tm, tn), lambda i, j: (i, j)),
        out_shape=jax.ShapeDtypeStruct((N_TOK, n), F32),
        scratch_shapes=[pltpu.VMEM((tm, D_MODEL), BF16)],
        compiler_params=_params("parallel", "arbitrary"),
        name="inproj" if shift_from is None else "inproj_shift",
    )(*(x if split_x else [x]), g, mod, w, mu)


def _resid_kernel(*refs, gate_row, widths, split, split_x, split_out):
    n_in = sum(2 if sp else 1 for sp in split)
    a_refs = refs[:n_in]
    w_ref = refs[n_in]
    n_x = 2 if split_x else 1
    x_refs = refs[n_in + 1:n_in + 1 + n_x]
    mod_ref = refs[n_in + 1 + n_x]
    o_ref = refs[n_in + 2 + n_x:-1] if split_out else refs[n_in + 2 + n_x]
    a_bf_ref = refs[-1]
    i = pl.program_id(0)

    @pl.when(pl.program_id(1) == 0)
    def _():
        k0, r = 0, 0
        for width, sp in zip(widths, split):
            cols = slice(k0, k0 + width)
            if sp:
                ctx_ref, lat_ref = a_refs[r], a_refs[r + 1]

                @pl.when(i < N_CTX // _TM_SEQ)
                def _():
                    a_bf_ref[:, cols] = ctx_ref[...].astype(BF16)

                @pl.when(i >= N_CTX // _TM_SEQ)
                def _():
                    a_bf_ref[:, cols] = lat_ref[...].astype(BF16)
            else:
                a_bf_ref[:, cols] = a_refs[r][...].astype(BF16)
            k0 += width
            r += 2 if sp else 1

    d = mod_ref[0, gate_row:gate_row + 1, :] * _dot_bf16(a_bf_ref[...], w_ref[0])
    if split_x:
        def add(x_ref):
            o_ref[...] = x_ref[...] + d

        _for_row_tile(i, _TM_SEQ, x_refs[0], x_refs[1], add)
        return
    y = x_refs[0][...] + d
    if not split_out:
        o_ref[...] = y
    else:
        ctx_o_ref, lat_o_ref = o_ref

        @pl.when(i < N_CTX // _TM_SEQ)
        def _():
            ctx_o_ref[...] = y

        @pl.when(i >= N_CTX // _TM_SEQ)
        def _():
            lat_o_ref[...] = y


def _resid_proj(acts, w, layer, x, mod, gate_row, split_out=False):
    tm, tn = _TM_SEQ, 256
    split_x = isinstance(x, (tuple, list))
    assert not (split_x and split_out)
    n_ctx = N_CTX // tm
    nj = D_MODEL // tn
    split = [isinstance(a, (tuple, list)) for a in acts]
    widths = [a[0].shape[1] if sp else a.shape[1] for a, sp in zip(acts, split)]
    k = sum(widths)
    in_specs, operands = [], []
    for a, width, sp in zip(acts, widths, split):
        if sp:
            in_specs += [pl.BlockSpec((tm, width), lambda i, j: (jnp.minimum(i, n_ctx - 1), 0)),
                         pl.BlockSpec((tm, width), lambda i, j: (jnp.maximum(i - n_ctx, 0), 0))]
            operands += list(a)
        else:
            in_specs.append(pl.BlockSpec((tm, width), lambda i, j: (i, 0)))
            operands.append(a)
    if split_out:
        out_specs = [pl.BlockSpec((tm, tn), lambda i, j: (jnp.minimum(i, n_ctx - 1), jnp.where(i < n_ctx, j, nj - 1))),
                     pl.BlockSpec((tm, tn), lambda i, j: (jnp.maximum(i - n_ctx, 0), jnp.where(i < n_ctx, 0, j)))]
        out_shape = [jax.ShapeDtypeStruct((N_CTX, D_MODEL), F32), jax.ShapeDtypeStruct((N_LAT, D_MODEL), F32)]
        sem = ("arbitrary", "arbitrary")
    else:
        out_specs = pl.BlockSpec((tm, tn), lambda i, j: (i, j))
        out_shape = jax.ShapeDtypeStruct((N_TOK, D_MODEL), F32)
        sem = ("parallel", "arbitrary")
    return pl.pallas_call(
        functools.partial(_resid_kernel, gate_row=gate_row, widths=tuple(widths), split=tuple(split),
                          split_x=split_x, split_out=split_out),
        grid=(N_TOK // tm, nj),
        in_specs=in_specs
        + [pl.BlockSpec((1, k, tn), lambda i, j: (layer, 0, j))]
        + (_row_split_specs(tm, tn, lambda j: j) if split_x else [pl.BlockSpec((tm, tn), lambda i, j: (i, j))])
        + [pl.BlockSpec((1, 8, tn), lambda i, j: (_group_of_tile(i, tm), 0, j))],
        out_specs=out_specs,
        out_shape=out_shape,
        scratch_shapes=[pltpu.VMEM((tm, k), BF16)],
        compiler_params=_params(*sem),
        name="resid_proj",
    )(*operands, w, *(x if split_x else [x]), mod)


def _ffn_up_kernel(x_ref, g_ref, mod_ref, wa_ref, wb_ref, cwa_ref, cwb_ref, cba_ref, cbb_ref, o_ref, h_ref):
    i = pl.program_id(0)

    @pl.when(pl.program_id(1) == 0)
    def _():
        h_ref[...] = _norm_mod(x_ref[...], g_ref[...], mod_ref[0], 3).astype(BF16)

    h = h_ref[...]

    def conv(w_ref, cw_ref, cb_ref):
        u = _dot_bf16(h, w_ref[0])
        prev, nxt = _seq_neighbours(u, i, _TM_SEQ)
        return prev * cw_ref[0:1, :] + u * cw_ref[1:2, :] + nxt * cw_ref[2:3, :] + cb_ref[...]

    o_ref[...] = (_silu(conv(wa_ref, cwa_ref, cba_ref)) * conv(wb_ref, cwb_ref, cbb_ref)).astype(o_ref.dtype)


def _ffn_up(x, g, mod, up, layer, cw, cb):
    tm, tn = _TM_SEQ, 256
    nb = D_FF // tn
    cb = cb.reshape(1, 2 * D_FF)
    return pl.pallas_call(
        _ffn_up_kernel,
        grid=(N_TOK // tm, nb),
        in_specs=[pl.BlockSpec((tm, D_MODEL), lambda i, j: (i, 0)),
                  pl.BlockSpec((1, D_MODEL), lambda i, j: (0, 0)),
                  pl.BlockSpec((1, 8, D_MODEL), lambda i, j: (_group_of_tile(i, tm), 0, 0)),
                  pl.BlockSpec((1, D_MODEL, tn), lambda i, j: (layer, 0, j)),
                  pl.BlockSpec((1, D_MODEL, tn), lambda i, j: (layer, 0, j + nb)),
                  pl.BlockSpec((3, tn), lambda i, j: (0, j)),
                  pl.BlockSpec((3, tn), lambda i, j: (0, j + nb)),
                  pl.BlockSpec((1, tn), lambda i, j: (0, j)),
                  pl.BlockSpec((1, tn), lambda i, j: (0, j + nb))],
        out_specs=pl.BlockSpec((tm, tn), lambda i, j: (i, j)),
        out_shape=jax.ShapeDtypeStruct((N_TOK, D_FF), BF16),
        scratch_shapes=[pltpu.VMEM((tm, D_MODEL), BF16)],
        compiler_params=_params("parallel", "arbitrary"),
        name="ffn_up",
    )(x, g, mod, up, up, cw, cw, cb, cb)


_ROPE_TM = 512


def _rope(y, c, s1, s2):
    return y * c + pltpu.roll(y, 1, axis=1) * s1 + pltpu.roll(y, LANES - 1, axis=1) * s2


def _rope_tables(rot_dim, lane_offsets):
    t = np.arange(DEC_SEQ)
    row, col = t // GRID_W, t % GRID_W
    n_freq = rot_dim // 4
    inv = ROPE_BASE ** (-np.arange(n_freq, dtype=np.float64) / n_freq)
    ang = np.concatenate([row[:, None] * inv, col[:, None] * inv], -1)
    cos, sin = np.cos(ang), np.sin(ang)
    n = _ROPE_TM + DEC_SEQ
    c, s1, s2 = np.ones((n, LANES)), np.zeros((n, LANES)), np.zeros((n, LANES))
    for a in lane_offsets:
        even = a + 2 * np.arange(rot_dim // 2)
        c[_ROPE_TM:, even] = cos
        c[_ROPE_TM:, even + 1] = cos
        s1[_ROPE_TM:, even + 1] = sin
        s2[_ROPE_TM:, even] = -sin
    return tuple(jnp.asarray(x, F32) for x in (c, s1, s2))


def _store_rotated(dst_ref, sl, y, tabs, rotate):
    if rotate is False:
        dst_ref[:, sl] = y
        return

    @pl.when(rotate)
    def _():
        dst_ref[:, sl] = _rope(y, *tabs)

    @pl.when(jnp.logical_not(rotate))
    def _():
        dst_ref[:, sl] = y


def _latent_tile():
    return pl.program_id(0) >= N_CTX // _ROPE_TM


def _rope_block(i):
    row = i * _ROPE_TM
    return jnp.where(row < N_CTX, 0, 1 + ((row - N_CTX) % DEC_SEQ) // _ROPE_TM)


def _mla_q_kernel(cq_ref, qnorm_ref, w_ref, qn_ref, c_ref, s1_ref, s2_ref, o_ref):
    xn = _rms(cq_ref[...], MLA_Q_RANK) * qnorm_ref[...]
    y = _dot_bf16(xn, w_ref[...])
    tabs = (c_ref[...], s1_ref[...], s2_ref[...])
    for h in range(MLA_HEADS):
        sl = slice(h * LANES, (h + 1) * LANES)
        _store_rotated(o_ref, sl, _rms(y[:, sl], MLA_QK) * qn_ref[...], tabs, _latent_tile())


def _mla_q(p, q_norm, w_uq_p, qn_p, tabs):
    tm = _ROPE_TM
    hw = MLA_HEADS * LANES
    tab_spec = pl.BlockSpec((tm, LANES), lambda i: (_rope_block(i), 0))
    return pl.pallas_call(
        _mla_q_kernel,
        grid=(N_TOK // tm,),
        in_specs=[pl.BlockSpec((tm, MLA_Q_RANK), lambda i: (i, 0)),
                  pl.BlockSpec((1, MLA_Q_RANK), lambda i: (0, 0)),
                  pl.BlockSpec((MLA_Q_RANK, hw), lambda i: (0, 0)),
                  pl.BlockSpec((1, LANES), lambda i: (0, 0)),
                  tab_spec, tab_spec, tab_spec],
        out_specs=pl.BlockSpec((tm, hw), lambda i: (i, 0)),
        out_shape=jax.ShapeDtypeStruct((N_TOK, hw), F32),
        compiler_params=_params("parallel"),
        name="mla_q",
    )(p, q_norm, w_uq_p, qn_p, *tabs)


def _mla_kv_kernel(ckv_ref, kr_ref, kvn_ref, wk_ref, wv_ref, kn_ref, c_ref, s1_ref, s2_ref,
                   k_ref, v_ref, ckvn_ref, *, norm_ckv):
    ckv = ckv_ref[...]
    if norm_ckv:
        ckv = _rms(ckv, MLA_KV_RANK) * kvn_ref[...]
    ckvn_ref[...] = ckv
    ckv_bf = ckv.astype(BF16)
    kk = _dot_bf16(ckv_bf, wk_ref[...])
    v_ref[...] = _dot_bf16(ckv_bf, wv_ref[...])
    kr = kr_ref[...]
    tabs = (c_ref[...], s1_ref[...], s2_ref[...])
    for h in range(MLA_HEADS):
        sl = slice(h * LANES, (h + 1) * LANES)
        _store_rotated(k_ref, sl, _rms(kk[:, sl] + kr, MLA_QK) * kn_ref[...], tabs,
                       _latent_tile() if norm_ckv else False)


def _mla_kv(ckv_src, ckv_blk, kr_src, kr_blk, kv_norm, wk_p, wv_p, kn_p, tabs, n_rows, own_tokens):
    tm = _ROPE_TM
    hw = MLA_HEADS * LANES
    tab_spec = pl.BlockSpec((tm, LANES), (lambda i: (_rope_block(i), 0)) if own_tokens else (lambda i: (0, 0)))
    return pl.pallas_call(
        functools.partial(_mla_kv_kernel, norm_ckv=own_tokens),
        grid=(n_rows // tm,),
        in_specs=[pl.BlockSpec((tm, LANES), lambda i: (i, ckv_blk)),
                  pl.BlockSpec((tm, LANES), lambda i: (i, kr_blk)),
                  pl.BlockSpec((1, LANES), lambda i: (0, 0)),
                  pl.BlockSpec((MLA_KV_RANK, hw), lambda i: (0, 0)),
                  pl.BlockSpec((MLA_KV_RANK, hw), lambda i: (0, 0)),
                  pl.BlockSpec((1, LANES), lambda i: (0, 0)),
                  tab_spec, tab_spec, tab_spec],
        out_specs=[pl.BlockSpec((tm, hw), lambda i: (i, 0)),
                   pl.BlockSpec((tm, hw), lambda i: (i, 0)),
                   pl.BlockSpec((tm, LANES), lambda i: (i, 0))],
        out_shape=[jax.ShapeDtypeStruct((n_rows, hw), F32),
                   jax.ShapeDtypeStruct((n_rows, hw), F32),
                   jax.ShapeDtypeStruct((n_rows, LANES), F32)],
        compiler_params=_params("parallel"),
        name="mla_kv",
    )(ckv_src, kr_src, kv_norm, wk_p, wv_p, kn_p, *tabs)


_LOG2E = math.log2(math.e)


def _softmax_parts(scores, scale):
    m = functools.reduce(jnp.maximum, [jnp.max(sc, axis=-1, keepdims=True) for sc in scores])
    ps = [jnp.exp2((sc - m) * (scale * _LOG2E)) for sc in scores]
    return ps, sum(jnp.sum(p, axis=-1, keepdims=True) for p in ps)


def _attend(q, sources, sl, scale):
    ps, l = _softmax_parts([_dot_nt_bf16(q, k_ref[:, sl]) for k_ref, _ in sources], scale)
    return sum(_dot_bf16(p, v_ref[:, sl]) for p, (_, v_ref) in zip(ps, sources)), l


def _mla_attn_kernel(*refs, pairs):
    q_ref, o_ref = refs[0], refs[-1]
    sources = [refs[1:3]] + ([refs[3:5]] if len(refs) == 6 else [])
    scale = MLA_QK ** -0.5
    for pr in range(pairs):
        outs = []
        for h in range(2):
            sl = slice((2 * pr + h) * LANES, (2 * pr + h + 1) * LANES)
            o, l = _attend(q_ref[:, sl], sources, sl, scale)
            outs.append(o / l)
        o_ref[:, pr * LANES:(pr + 1) * LANES] = (outs[0] + pltpu.roll(outs[1], MLA_V, axis=1)).astype(o_ref.dtype)


def _mla_attn(q, k, v, cache, batch, n, row0, tq, pairs):
    nqb = n // tq
    qb0 = row0 // tq
    kb0 = row0 // n
    wide = 2 * LANES * pairs
    kv_spec = pl.BlockSpec((n, wide), lambda b, h, i: (kb0 + b, h))
    in_specs, operands = [pl.BlockSpec((tq, wide), lambda b, h, i: (qb0 + b * nqb + i, h)), kv_spec, kv_spec], [q, k, v]
    if cache is not None:
        in_specs += [pl.BlockSpec((PAST_LEN, wide), lambda b, h, i: (b, h))] * 2
        operands += list(cache)
    return pl.pallas_call(
        functools.partial(_mla_attn_kernel, pairs=pairs),
        grid=(batch, MLA_HEADS // (2 * pairs), nqb),
        in_specs=in_specs,
        out_specs=pl.BlockSpec((tq, LANES * pairs), lambda b, h, i: (b * nqb + i, h)),
        out_shape=jax.ShapeDtypeStruct((batch * n, MLA_HEADS * MLA_V), BF16),
        compiler_params=_params("parallel", "parallel", "arbitrary"),
        name="mla_attn",
    )(*operands)


def _ret_kernel(lg_ref, q_ref, k_ref, v_ref, rg_ref, s0_ref, gn_ref, *out_and_scratch, n, tq, want_state):
    if want_state:
        o_ref, st_ref, decay_ref = out_and_scratch
    else:
        o_ref, decay_ref = out_and_scratch
    pair, qi, b = pl.program_id(0), pl.program_id(1), pl.program_id(2)
    q = q_ref[...]
    k = k_ref[...] * (RET_DK ** -0.5)
    lo = _lane_lo((1, LANES))
    row = (qi * tq + lax.broadcasted_iota(jnp.int32, (tq, 1), 0)).astype(F32)

    @pl.when(b == 0)
    def _():
        col = lax.broadcasted_iota(jnp.int32, (1, n), 1).astype(F32)
        diff = row - col
        for h in range(2):
            lgf = lg_ref[0, 2 * pair + h]
            lgb = lg_ref[1, 2 * pair + h]
            decay_ref[h] = (jnp.where(diff >= 0, jnp.exp(lgf * jnp.maximum(diff, 0.0)), 0.0)
                            + jnp.where(diff <= 0, jnp.exp(lgb * jnp.maximum(-diff, 0.0)), 0.0))

    for h in range(2):
        lgf = lg_ref[0, 2 * pair + h]
        lgb = lg_ref[1, 2 * pair + h]
        mask = lo if h == 0 else jnp.logical_not(lo)
        qh = jnp.where(mask, q, 0.0)
        vh = v_ref[:, h * LANES:(h + 1) * LANES]
        o = _dot_bf16(_dot_nt_bf16(qh, k) * decay_ref[h], vh)
        o = o + _dot_bf16(qh * jnp.exp(lgf * (row + 1.0)), s0_ref[0, 0])
        o = o + _dot_bf16(qh * jnp.exp(lgb * (n - row)), s0_ref[0, 1])
        y = _rms(o, RET_DV) * gn_ref[:, h * LANES:(h + 1) * LANES]
        o_ref[:, h * LANES:(h + 1) * LANES] = (_silu(rg_ref[:, h * LANES:(h + 1) * LANES]) * y).astype(o_ref.dtype)

    if want_state:
        pos = lax.broadcasted_iota(jnp.int32, (n, 1), 0).astype(F32)
        for d in range(2):
            acc = None
            for h in range(2):
                lg = lg_ref[d, 2 * pair + h]
                mask = lo if h == 0 else jnp.logical_not(lo)
                expo = (n - 1.0 - pos) if d == 0 else pos
                kd = jnp.where(mask, k * jnp.exp(lg * expo), 0.0)
                term = _dot_tn(kd, v_ref[:, h * LANES:(h + 1) * LANES])
                acc = term if acc is None else acc + term
            lg_rows = jnp.where(lax.broadcasted_iota(jnp.int32, (LANES, 1), 0) < 64,
                                lg_ref[d, 2 * pair], lg_ref[d, 2 * pair + 1])
            st_ref[0, d] = acc + s0_ref[0, d] * jnp.exp(lg_rows * n)


def _retention(log_g, p, q_blk, k_blk, v_blk, g_blk, s0, gn, batch, n, row0, tq, want_state):
    nqb = n // tq
    assert not want_state or nqb == 1
    qb0 = row0 // tq
    kb0 = row0 // n
    pairs = RET_HEADS // 2
    out_specs = [pl.BlockSpec((tq, 2 * LANES), lambda h, i, b: (b * nqb + i, h))]
    out_shape = [jax.ShapeDtypeStruct((batch * n, RET_HEADS * RET_DV), BF16)]
    if want_state:
        out_specs.append(pl.BlockSpec((1, 2, LANES, LANES), lambda h, i, b: (b, 0, h, 0)))
        out_shape.append(jax.ShapeDtypeStruct((batch, 2, RET_HEADS * RET_DK, RET_DV), F32))
    outs = pl.pallas_call(
        functools.partial(_ret_kernel, n=n, tq=tq, want_state=want_state),
        grid=(pairs, nqb, batch),
        in_specs=[pl.BlockSpec(memory_space=pltpu.SMEM),
                  pl.BlockSpec((tq, LANES), lambda h, i, b: (qb0 + b * nqb + i, q_blk + h)),
                  pl.BlockSpec((n, LANES), lambda h, i, b: (kb0 + b, k_blk + h)),
                  pl.BlockSpec((n, 2 * LANES), lambda h, i, b: (kb0 + b, v_blk // 2 + h)),
                  pl.BlockSpec((tq, 2 * LANES), lambda h, i, b: (qb0 + b * nqb + i, g_blk // 2 + h)),
                  pl.BlockSpec((1, 2, LANES, LANES), lambda h, i, b: (b, 0, h, 0)),
                  pl.BlockSpec((1, 2 * LANES), lambda h, i, b: (0, h))],
        out_specs=out_specs,
        out_shape=out_shape,
        scratch_shapes=[pltpu.VMEM((2, tq, n), F32)],
        compiler_params=_params("parallel", "parallel", "arbitrary"),
        name="retention",
    )(log_g, p, p, p, p, s0, gn)
    return outs if want_state else (outs[0], None)


def _diff_qk_kernel(q_ref, k_ref, v_ref, qn_ref, kn_ref, c_ref, s1_ref, s2_ref, qo_ref, ko_ref, dk_ref, dv_ref):
    tabs = (c_ref[...], s1_ref[...], s2_ref[...])
    is_ctx = jnp.logical_not(_latent_tile())
    for src, gain, dst in ((q_ref, qn_ref, qo_ref), (k_ref, kn_ref, ko_ref)):
        for h in range(DIFF_HEADS):
            sl = slice(h * LANES, (h + 1) * LANES)
            y = src[:, sl]
            y = y * lax.rsqrt(_seg64_sum(y * y) * (1.0 / DIFF_DH) + EPS) * gain[...]
            _store_rotated(dst, sl, y, tabs, _latent_tile())
            if dst is ko_ref:
                @pl.when(is_ctx)
                def _():
                    dk_ref[:, h, 0, :] = y[:, :DIFF_DH]
                    dk_ref[:, h, 1, :] = pltpu.roll(y, DIFF_DH, axis=1)[:, :DIFF_DH]
                    dv_ref[:, h, :] = v_ref[:, sl]


def _diff_qk(p, qn_p, kn_p, tabs):
    tm = _ROPE_TM
    n_ctx = N_CTX // tm
    tab_spec = pl.BlockSpec((tm, LANES), lambda i: (_rope_block(i), 0))
    ctx_blk = lambda i: jnp.minimum(i, n_ctx - 1)
    return pl.pallas_call(
        _diff_qk_kernel,
        grid=(N_TOK // tm,),
        in_specs=[pl.BlockSpec((tm, DIFF_W), lambda i: (i, 0)),
                  pl.BlockSpec((tm, DIFF_W), lambda i: (i, 1)),
                  pl.BlockSpec((tm, DIFF_W), lambda i: (i, 2)),
                  pl.BlockSpec((1, LANES), lambda i: (0, 0)),
                  pl.BlockSpec((1, LANES), lambda i: (0, 0)),
                  tab_spec, tab_spec, tab_spec],
        out_specs=[pl.BlockSpec((tm, DIFF_W), lambda i: (i, 0)),
                   pl.BlockSpec((tm, DIFF_W), lambda i: (i, 0)),
                   pl.BlockSpec((tm, DIFF_HEADS, 2, DIFF_DH), lambda i: (ctx_blk(i), 0, 0, 0)),
                   pl.BlockSpec((tm, DIFF_HEADS, 2 * DIFF_DH), lambda i: (ctx_blk(i), 0, 0))],
        out_shape=[jax.ShapeDtypeStruct((N_TOK, DIFF_W), F32)] * 2
        + [jax.ShapeDtypeStruct((N_CTX, DIFF_HEADS, 2, DIFF_DH), F32),
           jax.ShapeDtypeStruct((N_CTX, DIFF_HEADS, 2 * DIFF_DH), F32)],
        compiler_params=_params("arbitrary"),
        name="diff_qk",
    )(p, p, p, qn_p, kn_p, *tabs)


def _diff_attn_kernel(*refs, lam_init, heads):
    lam_ref, q_ref, gn_ref, o_ref = refs[0], refs[1], refs[-2], refs[-1]
    sources = [refs[2:4]] + ([refs[4:6]] if len(refs) == 8 else [])
    lv = lam_ref[...]
    lam = (jnp.exp(jnp.sum(lv[0:1] * lv[1:2], axis=-1, keepdims=True))
           - jnp.exp(jnp.sum(lv[2:3] * lv[3:4], axis=-1, keepdims=True)) + lam_init)
    scale = DIFF_DH ** -0.5
    lo = _lane_lo((1, LANES))
    for h in range(heads):
        sl = slice(h * LANES, (h + 1) * LANES)
        q = q_ref[:, sl]
        kbs = [k_ref[:, sl].astype(BF16) for k_ref, _ in sources]
        ps1, l1 = _softmax_parts([_dot_nt_bf16(jnp.where(lo, q, 0.0), kb) for kb in kbs], scale)
        ps2, l2 = _softmax_parts([_dot_nt_bf16(jnp.where(lo, 0.0, q), kb) for kb in kbs], scale)
        o = sum(_dot_bf16(p1 / l1 - lam * (p2 / l2), v_ref[:, sl])
                for p1, p2, (_, v_ref) in zip(ps1, ps2, sources))
        o_ref[:, sl] = (_rms(o, 2 * DIFF_DH) * gn_ref[:, sl] * (1.0 - lam_init)).astype(o_ref.dtype)


def _diff_attn(lam, q, k, v, v_blk0, cache, gn, batch, n, row0, tq, lam_init, heads):
    nqb = n // tq
    qb0 = row0 // tq
    kb0 = row0 // n
    wide = LANES * heads
    vb0 = v_blk0 // heads
    in_specs = [pl.BlockSpec((4, DIFF_DH), lambda b, h, i: (0, 0)),
                pl.BlockSpec((tq, wide), lambda b, h, i: (qb0 + b * nqb + i, h)),
                pl.BlockSpec((n, wide), lambda b, h, i: (kb0 + b, h)),
                pl.BlockSpec((n, wide), lambda b, h, i: (kb0 + b, vb0 + h))]
    operands = [lam, q, k, v]
    if cache is not None:
        in_specs += [pl.BlockSpec((PAST_LEN, wide), lambda b, h, i: (b, h))] * 2
        operands += list(cache)
    return pl.pallas_call(
        functools.partial(_diff_attn_kernel, lam_init=lam_init, heads=heads),
        grid=(batch, DIFF_HEADS // heads, nqb),
        in_specs=in_specs + [pl.BlockSpec((1, wide), lambda b, h, i: (0, h))],
        out_specs=pl.BlockSpec((tq, wide), lambda b, h, i: (b * nqb + i, h)),
        out_shape=jax.ShapeDtypeStruct((batch * n, DIFF_W), BF16),
        compiler_params=_params("parallel", "parallel", "arbitrary"),
        name="diff_attn",
    )(*operands, gn)


def _seg64_sum_wide(x):
    return jnp.concatenate([_seg64_sum(x[:, j * LANES:(j + 1) * LANES]) for j in range(x.shape[1] // LANES)], axis=1)


_SCAN_SLOTS = 3
_SCAN_DIR_NJ = (_SCAN_SLOTS - 1) * RWKV_HEADS
_SCAN_NJ = 2 * _SCAN_DIR_NJ + RWKV_HEADS


def _rwkv_pre_kernel(r_ref, k_ref, v_ref, lo_ref, wup_ref, aup_ref, gup_ref, w0_ref, a0_ref, kk_ref, ka_ref, rk_ref,
                     op_ref, g_ref, bonus_ref):
    W = RWKV_W
    col = lambda q: slice(q * W, (q + 1) * W)
    r = r_ref[...]
    k = k_ref[...]
    v = v_ref[...]
    lora = lo_ref[...]
    kk = k * kk_ref[...]
    kkn = kk * lax.rsqrt(_seg64_sum_wide(kk * kk) + EPS)
    g_ref[...] = _dot_bf16(_sigmoid(lora[:, 2 * LANES:3 * LANES]), gup_ref[...])
    pre = w0_ref[...] + _dot_bf16(jnp.tanh(lora[:, 0:LANES]), wup_ref[...])
    decay = jnp.exp(-jnp.exp(-_softplus(-pre) - 0.5))
    a = _sigmoid(a0_ref[...] + _dot_bf16(lora[:, LANES:2 * LANES], aup_ref[...]))
    lo = _lane_lo((1, LANES))

    def store_slot(j0, x1, x2):
        for h in range(RWKV_HEADS):
            blk = slice((h // 2) * LANES, (h // 2 + 1) * LANES)
            if h % 2 == 0:
                out = jnp.where(lo, x1[:, blk], pltpu.roll(x2[:, blk], RWKV_HS, axis=1))
            else:
                out = jnp.where(lo, pltpu.roll(x1[:, blk], RWKV_HS, axis=1), x2[:, blk])
            op_ref[:, j0 + h, :] = out

    bonus = None
    for d in range(2):
        a_d = a[:, col(d)]
        k_d = k * (1.0 + (a_d - 1.0) * ka_ref[...])
        t = _seg64_sum_wide(r * k_d * rk_ref[...])
        bonus = t if bonus is None else bonus + t
        store_slot(d * _SCAN_DIR_NJ, kkn, decay[:, col(d)])
        store_slot(d * _SCAN_DIR_NJ + RWKV_HEADS, k_d, kkn * a_d)
    store_slot(2 * _SCAN_DIR_NJ, r, v)
    bonus_ref[...] = bonus


def _rwkv_pre(p, r_blk, k_blk, v_blk, lo_blk, wup_bd, aup_bd, gup, w0, a0, k_k, k_a, r_k):
    tm = 256
    w = RWKV_W
    row = lambda n: pl.BlockSpec((1, n), lambda i: (0, 0))
    full = lambda a, b: pl.BlockSpec((a, b), lambda i: (0, 0))
    return pl.pallas_call(
        _rwkv_pre_kernel,
        grid=(N_TOK // tm,),
        in_specs=[pl.BlockSpec((tm, w), lambda i: (i, r_blk)),
                  pl.BlockSpec((tm, w), lambda i: (i, k_blk)),
                  pl.BlockSpec((tm, w), lambda i: (i, v_blk)),
                  pl.BlockSpec((tm, 3 * LANES), lambda i: (i, lo_blk)),
                  full(LANES, 2 * w), full(LANES, 2 * w), full(LANES, w),
                  row(2 * w), row(2 * w), row(w), row(w), row(w)],
        out_specs=[pl.BlockSpec((tm, _SCAN_NJ, LANES), lambda i: (i, 0, 0)),
                   pl.BlockSpec((tm, w), lambda i: (i, 0)), pl.BlockSpec((tm, w), lambda i: (i, 0))],
        out_shape=[jax.ShapeDtypeStruct((N_TOK, _SCAN_NJ, LANES), F32),
                   jax.ShapeDtypeStruct((N_TOK, w), F32), jax.ShapeDtypeStruct((N_TOK, w), F32)],
        compiler_params=_params("parallel"),
        name="rwkv_pre",
    )(p, p, p, p, wup_bd, aup_bd, gup, w0, a0, k_k, k_a, r_k)


_SCAN_CHUNK = 32
_SCAN_UNROLL = 16


def _rwkv_first_sa(s_ref, sa_ref, kk, n_k):
    nv = s_ref.shape[1]
    chunk = min(_SCAN_CHUNK, nv)
    for c0 in range(0, nv, chunk):
        def body(k, acc):
            return acc + s_ref[k, c0:c0 + chunk, :] * kk(k)
        sa_ref[c0:c0 + chunk, :] = lax.fori_loop(0, n_k, body, jnp.zeros((chunk, LANES), F32), unroll=_SCAN_UNROLL)


def _rwkv_step(s_ref, sa_ref, kk_next, w, kd, b, r, v_at, n_k):
    nv = s_ref.shape[1]
    chunk = min(_SCAN_CHUNK, nv)
    ys = []
    for c0 in range(0, nv, chunk):
        sa = sa_ref[c0:c0 + chunk, :]
        vc = v_at(c0, chunk)

        def body(k, acc):
            y_acc, sa_acc = acc
            s_new = s_ref[k, c0:c0 + chunk, :] * w(k) - sa * b(k) + vc * kd(k)
            s_ref[k, c0:c0 + chunk, :] = s_new
            return y_acc + s_new * r(k), sa_acc + s_new * kk_next(k)

        zero = jnp.zeros((chunk, LANES), F32)
        y_acc, sa_acc = lax.fori_loop(0, n_k, body, (zero, zero), unroll=_SCAN_UNROLL)
        sa_ref[c0:c0 + chunk, :] = sa_acc
        ys.append(y_acc)
    return ys[0] if len(ys) == 1 else jnp.concatenate(ys, axis=0)


_CTX_TB = 32


def _load_scan_operands(kt_ref, t, slabs):
    for s, x in enumerate(slabs):
        xt = x.T
        kt_ref[t, 2 * s] = xt[:RWKV_HS]
        kt_ref[t, 2 * s + 1] = xt[RWKV_HS:]


def _rwkv_scan_ctx_kernel(xd_ref, xs_ref, y_ref, st_ref, s_ref, sa_ref, kt_ref, ys_ref):
    d = pl.program_id(0)
    tb = pl.program_id(1)
    nh = RWKV_HEADS
    step_t = lambda i: jnp.where(d == 0, i, _CTX_TB - 1 - i)

    @pl.when(tb == 0)
    def _():
        s_ref[...] = jnp.zeros_like(s_ref)

    def load_t(t, carry):
        slabs = [jnp.concatenate([xd_ref[b, t, s * nh:(s + 1) * nh, :] for b in range(BATCH)], axis=0)
                 for s in range(_SCAN_SLOTS - 1)]
        slabs.append(jnp.concatenate([xs_ref[b, t] for b in range(BATCH)], axis=0))
        _load_scan_operands(kt_ref, t, slabs)
        return carry

    lax.fori_loop(0, _CTX_TB, load_t, 0, unroll=4)

    t0 = step_t(0)
    _rwkv_first_sa(s_ref, sa_ref, lambda k: kt_ref[t0, 0, pl.ds(k, 1), :], RWKV_HS)

    def step(i, carry):
        t = step_t(i)
        tn = step_t(jnp.minimum(i + 1, _CTX_TB - 1))
        row = lambda q, tt: (lambda k: kt_ref[tt, q, pl.ds(k, 1), :])
        ys_ref[t] = _rwkv_step(s_ref, sa_ref, row(0, tn), row(1, t), row(2, t), row(3, t), row(4, t),
                               lambda c0, n: kt_ref[t, 5, pl.ds(c0, n), :], RWKV_HS)
        return carry

    lax.fori_loop(0, _CTX_TB, step, 0)

    def store_t(i, carry):
        z = jnp.concatenate([ys_ref[2 * i], ys_ref[2 * i + 1]], axis=0).T
        z_odd = pltpu.roll(z, RWKV_HS, axis=1)
        for b in range(BATCH):
            y_ref[b, 2 * i] = z[b * nh:(b + 1) * nh, :RWKV_HS]
            y_ref[b, 2 * i + 1] = z_odd[b * nh:(b + 1) * nh, :RWKV_HS]
        return carry

    lax.fori_loop(0, _CTX_TB // 2, store_t, 0, unroll=4)

    @pl.when(tb == pl.num_programs(1) - 1)
    def _():
        st_ref[0] = s_ref[...]


def _rwkv_scan_ctx(op4):
    nt = SEQ // _CTX_TB
    hs = RWKV_HS
    tblk = lambda d, tb: jnp.where(d == 0, tb, nt - 1 - tb)
    return pl.pallas_call(
        _rwkv_scan_ctx_kernel,
        grid=(2, nt),
        in_specs=[pl.BlockSpec((BATCH, _CTX_TB, _SCAN_DIR_NJ, LANES), lambda d, tb: (0, tblk(d, tb), d, 0)),
                  pl.BlockSpec((BATCH, _CTX_TB, RWKV_HEADS, LANES),
                               lambda d, tb: (0, tblk(d, tb), 2 * _SCAN_DIR_NJ // RWKV_HEADS, 0))],
        out_specs=[pl.BlockSpec((BATCH, _CTX_TB, RWKV_HEADS, hs), lambda d, tb: (0, tblk(d, tb), d, 0)),
                   pl.BlockSpec((1, hs, hs, LANES), lambda d, tb: (d, 0, 0, 0))],
        out_shape=[jax.ShapeDtypeStruct((BATCH, SEQ, 2 * RWKV_HEADS, hs), F32),
                   jax.ShapeDtypeStruct((2, hs, hs, LANES), F32)],
        scratch_shapes=[pltpu.VMEM((hs, hs, LANES), F32), pltpu.VMEM((hs, LANES), F32),
                        pltpu.VMEM((_CTX_TB, 2 * _SCAN_SLOTS, hs, LANES), F32),
                        pltpu.VMEM((_CTX_TB, hs, LANES), F32)],
        compiler_params=_params("parallel", "arbitrary"),
        name="rwkv_scan_ctx",
    )(op4, op4)


_LAT_TB = 64
_LAT_VSPLIT = 4
_LAT_STATES = 2 * DEC_BATCH * RWKV_HEADS
_LAT_VROWS = RWKV_HS // _LAT_VSPLIT


def _rwkv_scan_lat_kernel(xf0_ref, xf0s_ref, xf1_ref, xf1s_ref, xb0_ref, xb0s_ref, xb1_ref, xb1s_ref,
                          s0_ref, yf_ref, yb_ref, s_ref, sa_ref, kt_ref, v_ref, ys_ref):
    @pl.when(pl.program_id(0) == 0)
    def _():
        s_ref[...] = s0_ref[...]

    group = lax.broadcasted_iota(jnp.int32, (_LAT_VROWS, LANES), 1) // _LAT_STATES
    nh = RWKV_HEADS

    def load_t(t, carry):
        tr = _LAT_TB - 1 - t
        for s in range(_SCAN_SLOTS):
            if s < _SCAN_SLOTS - 1:
                heads = slice(s * nh, (s + 1) * nh)
                x = jnp.concatenate([xf0_ref[0, t, heads, :], xf1_ref[0, t, heads, :],
                                     xb0_ref[0, tr, heads, :], xb1_ref[0, tr, heads, :]], axis=0)
            else:
                x = jnp.concatenate([xf0s_ref[0, t], xf1s_ref[0, t], xb0s_ref[0, tr], xb1s_ref[0, tr]], axis=0)
            xt = jnp.concatenate([x] * _LAT_VSPLIT, axis=0).T
            kt_ref[t, 2 * s] = xt[:RWKV_HS]
            if s < _SCAN_SLOTS - 1:
                kt_ref[t, 2 * s + 1] = xt[RWKV_HS:]
            else:
                v = jnp.zeros((_LAT_VROWS, LANES), F32)
                for g in range(_LAT_VSPLIT):
                    r0 = RWKV_HS + g * _LAT_VROWS
                    v = jnp.where(group == g, xt[r0:r0 + _LAT_VROWS, :], v)
                v_ref[t] = v
        return carry

    lax.fori_loop(0, _LAT_TB, load_t, 0, unroll=8)

    _rwkv_first_sa(s_ref, sa_ref, lambda k: kt_ref[0, 0, pl.ds(k, 1), :], RWKV_HS)

    def step(t, carry):
        tn = jnp.minimum(t + 1, _LAT_TB - 1)
        row = lambda q, tt: (lambda k: kt_ref[tt, q, pl.ds(k, 1), :])
        ys_ref[t] = _rwkv_step(s_ref, sa_ref, row(0, tn), row(1, t), row(2, t), row(3, t), row(4, t),
                               lambda c0, n: v_ref[t, pl.ds(c0, n), :], RWKV_HS)
        return carry

    lax.fori_loop(0, _LAT_TB, step, 0)

    def store_t(i, carry):
        rows = [jnp.where(group == g, ys_ref[2 * i + j], 0.0) for j in range(2) for g in range(_LAT_VSPLIT)]
        z = jnp.concatenate(rows, axis=0).T
        y = (z[0:_LAT_STATES] + z[_LAT_STATES:2 * _LAT_STATES]
             + z[2 * _LAT_STATES:3 * _LAT_STATES] + z[3 * _LAT_STATES:4 * _LAT_STATES])
        y_odd = pltpu.roll(y, RWKV_HS, axis=1)
        half = _LAT_STATES // 2
        for b in range(DEC_BATCH):
            rows_f = slice(b * nh, (b + 1) * nh)
            rows_b = slice(half + b * nh, half + (b + 1) * nh)
            yf_ref[b, 2 * i] = y[rows_f, :RWKV_HS]
            yf_ref[b, 2 * i + 1] = y_odd[rows_f, :RWKV_HS]
            yb_ref[b, _LAT_TB - 1 - 2 * i] = y[rows_b, :RWKV_HS]
            yb_ref[b, _LAT_TB - 2 - 2 * i] = y_odd[rows_b, :RWKV_HS]
        return carry

    lax.fori_loop(0, _LAT_TB // 2, store_t, 0, unroll=4)


def _rwkv_scan_lat(op4, s0):
    hs = RWKV_HS
    nv = _LAT_VROWS
    nt = DEC_SEQ // _LAT_TB
    per_seq = SEQ // _LAT_TB
    first = N_CTX // SEQ

    def x_specs(b, d):
        tblk = (lambda tb: tb) if d == 0 else (lambda tb: nt - 1 - tb)
        chunk = lambda tb: first + b * (DEC_SEQ // SEQ) + tblk(tb) // per_seq
        return [pl.BlockSpec((1, _LAT_TB, _SCAN_DIR_NJ, LANES), lambda tb: (chunk(tb), tblk(tb) % per_seq, d, 0)),
                pl.BlockSpec((1, _LAT_TB, RWKV_HEADS, LANES),
                             lambda tb: (chunk(tb), tblk(tb) % per_seq, 2 * _SCAN_DIR_NJ // RWKV_HEADS, 0))]

    y_shape = jax.ShapeDtypeStruct((DEC_BATCH, DEC_SEQ, RWKV_HEADS, hs), F32)
    y_blk = (DEC_BATCH, _LAT_TB, RWKV_HEADS, hs)
    return pl.pallas_call(
        _rwkv_scan_lat_kernel,
        grid=(nt,),
        in_specs=x_specs(0, 0) + x_specs(1, 0) + x_specs(0, 1) + x_specs(1, 1)
        + [pl.BlockSpec((hs, nv, LANES), lambda tb: (0, 0, 0))],
        out_specs=[pl.BlockSpec(y_blk, lambda tb: (0, tb, 0, 0)),
                   pl.BlockSpec(y_blk, lambda tb: (0, nt - 1 - tb, 0, 0))],
        out_shape=[y_shape, y_shape],
        scratch_shapes=[pltpu.VMEM((hs, nv, LANES), F32), pltpu.VMEM((nv, LANES), F32),
                        pltpu.VMEM((_LAT_TB, 2 * _SCAN_SLOTS - 1, hs, LANES), F32),
                        pltpu.VMEM((_LAT_TB, nv, LANES), F32), pltpu.VMEM((_LAT_TB, nv, LANES), F32)],
        compiler_params=_params("arbitrary"),
        name="rwkv_scan_lat",
    )(*([op4] * 8), s0)


def _rwkv_post_kernel(yc_ref, ylf_ref, ylb_ref, bonus_ref, v_ref, g_ref, gn_ref, o_ref):
    def finish(head_sum):
        y = jnp.concatenate([head_sum(h) for h in range(RWKV_HEADS)], axis=1)
        y = y * lax.rsqrt(_seg64_sum_wide(y * y) * (1.0 / RWKV_HS) + EPS) * gn_ref[...]
        o_ref[...] = ((y + bonus_ref[...] * v_ref[...]) * g_ref[...]).astype(o_ref.dtype)

    @pl.when(pl.program_id(0) < N_CTX // SEQ)
    def _():
        finish(lambda h: yc_ref[0, :, h, :] + yc_ref[0, :, RWKV_HEADS + h, :])

    @pl.when(pl.program_id(0) >= N_CTX // SEQ)
    def _():
        finish(lambda h: ylf_ref[0, :, h, :] + ylb_ref[0, :, h, :])


def _rwkv_post(y_ctx, y_lat_f, y_lat_b, bonus, p, v_blk, g, gn):
    tm = SEQ
    w = RWKV_W
    hs = RWKV_HS
    n_ctx = N_CTX // SEQ
    lat = lambda y: y.reshape(N_LAT // SEQ, SEQ, RWKV_HEADS, hs)
    spec = pl.BlockSpec((tm, w), lambda i: (i, 0))
    lat_spec = pl.BlockSpec((1, SEQ, RWKV_HEADS, hs), lambda i: (jnp.maximum(i - n_ctx, 0), 0, 0, 0))
    return pl.pallas_call(
        _rwkv_post_kernel,
        grid=(N_TOK // tm,),
        in_specs=[pl.BlockSpec((1, SEQ, 2 * RWKV_HEADS, hs), lambda i: (jnp.minimum(i, n_ctx - 1), 0, 0, 0)),
                  lat_spec, lat_spec, spec, pl.BlockSpec((tm, w), lambda i: (i, v_blk)), spec,
                  pl.BlockSpec((1, w), lambda i: (0, 0))],
        out_specs=spec,
        out_shape=jax.ShapeDtypeStruct((N_TOK, w), BF16),
        compiler_params=_params("parallel"),
        name="rwkv_post",
    )(y_ctx, lat(y_lat_f), lat(y_lat_b), bonus, p, g, gn)


def _value_split_layout(x):
    lead = x.shape[:-2]
    n = len(lead)
    x = x.reshape(lead + (_LAT_STATES, _LAT_VSPLIT, _LAT_VROWS))
    return jnp.transpose(x, tuple(range(n)) + (n + 2, n + 1, n)).reshape(lead + (_LAT_VROWS, LANES))


_EVEN_HEAD = MLA_Q_RANK + MLA_KV_RANK
_EVEN_PACKED = _EVEN_HEAD + LANES + 2 * RET_HEADS * (RET_DK + RET_DV)


def _pack_even_weight_kernel(w_ref, o_ref):
    w = w_ref[...]
    o_ref[:, 0:_EVEN_HEAD] = w[:, 0:_EVEN_HEAD]
    o_ref[:, _EVEN_HEAD:_EVEN_HEAD + LANES] = jnp.zeros((w.shape[0], LANES), F32)
    o_ref[:, _EVEN_HEAD + MLA_NOPE:_EVEN_HEAD + MLA_QK] = w[:, _EVEN_HEAD:_EVEN_HEAD + MLA_ROPE]
    o_ref[:, _EVEN_HEAD + LANES:] = w[:, _EVEN_HEAD + MLA_ROPE:]


def _pack_even_weight(w_in):
    tm = 256
    return pl.pallas_call(
        _pack_even_weight_kernel,
        grid=(D_MODEL // tm,),
        in_specs=[pl.BlockSpec((tm, w_in.shape[1]), lambda i: (i, 0))],
        out_specs=pl.BlockSpec((tm, _EVEN_PACKED), lambda i: (i, 0)),
        out_shape=jax.ShapeDtypeStruct((D_MODEL, _EVEN_PACKED), F32),
        compiler_params=_params("parallel"),
        name="pack_even_weight",
    )(w_in)


def _even_layer(x, mod, g_mix, w_in, q_norm, kv_norm, w_uq, w_ukv, qn, kn, ret_decay, ret_gn,
                cache_ckv, cache_krope, state_ret, tabs_m):
    w_p = _pack_even_weight(w_in)
    p = _inproj(x, g_mix, mod, w_p, jnp.zeros((1, w_p.shape[1]), F32), 1024, None)
    CKV_BLK, KR_BLK, RQ_BLK, RK_BLK, RV_BLK, RG_BLK = 2, 3, 4, 6, 8, 12

    def head_pad(w, n_head, d_head, c0, c1):
        w = w.reshape(w.shape[0], n_head, d_head)[:, :, c0:c1]
        return jnp.pad(w, ((0, 0), (0, 0), (0, LANES - (c1 - c0)))).reshape(w.shape[0], n_head * LANES)

    w_uq_p = head_pad(w_uq, MLA_HEADS, MLA_QK, 0, MLA_QK)
    wk_p = head_pad(w_ukv, MLA_HEADS, MLA_NOPE + MLA_V, 0, MLA_NOPE)
    wv_p = head_pad(w_ukv, MLA_HEADS, MLA_NOPE + MLA_V, MLA_NOPE, MLA_NOPE + MLA_V)
    qn_p = jnp.pad(qn, (0, LANES - MLA_QK))[None]
    kn_p = jnp.pad(kn, (0, LANES - MLA_QK))[None]

    q = _mla_q(p, q_norm[None], w_uq_p, qn_p, tabs_m)
    k, v, ckvn = _mla_kv(p, CKV_BLK, p, KR_BLK, kv_norm[None], wk_p, wv_p, kn_p, tabs_m, N_TOK, True)

    n_c = DEC_BATCH * PAST_LEN
    kr_c = jnp.pad(cache_krope.reshape(n_c, MLA_ROPE), ((0, 0), (MLA_NOPE, LANES - MLA_QK)))
    k_c, v_c, _ = _mla_kv(cache_ckv.reshape(n_c, MLA_KV_RANK), 0, kr_c, 0, kv_norm[None], wk_p, wv_p, kn_p,
                          tabs_m, n_c, False)

    o_ctx = _mla_attn(q, k, v, None, BATCH, SEQ, 0, SEQ, MLA_HEADS // 2)
    o_lat = _mla_attn(q, k, v, (k_c, v_c), DEC_BATCH, DEC_SEQ, N_CTX, 256, 1)

    log_g = -_softplus(-ret_decay)
    gn = ret_gn[None]
    s0_ctx = jnp.zeros((BATCH, 2, RET_HEADS * RET_DK, RET_DV), F32)
    r_ctx, st_ctx = _retention(log_g, p, RQ_BLK, RK_BLK, RV_BLK, RG_BLK, s0_ctx, gn, BATCH, SEQ, 0, SEQ, True)
    s0_lat = state_ret.reshape(DEC_BATCH, 2, RET_HEADS * RET_DK, RET_DV)
    r_lat, _ = _retention(log_g, p, RQ_BLK, RK_BLK, RV_BLK, RG_BLK, s0_lat, gn, DEC_BATCH, DEC_SEQ, N_CTX, 256,
                          False)

    mix = [(o_ctx, o_lat), (r_ctx, r_lat)]
    new_ckv = ckvn[:N_CTX].reshape(BATCH, SEQ, MLA_KV_RANK)
    new_krope = p[:N_CTX, KR_BLK * LANES + MLA_NOPE:KR_BLK * LANES + MLA_QK]
    new_krope = new_krope.reshape(BATCH, SEQ, MLA_ROPE)
    new_ret = st_ctx.reshape(BATCH, 2, RET_HEADS, RET_DK, RET_DV)
    return mix, new_ckv, new_krope, new_ret


def _odd_layer(x, mod, g_mix, w_in, qn, kn, lam, diff_gn, mu, w0, w_up, a0, a_up, g_up, k_k, k_a, r_k, gn,
               cache_k, cache_v, state_rwkv, tabs_d, lam_init):
    w_p = w_in
    n_in = w_p.shape[1]
    mu_full = jnp.concatenate([jnp.zeros((3 * DIFF_W,), F32), mu])[None]
    p = _inproj(x, g_mix, mod, w_p, mu_full, 384, (3 * DIFF_W) // 384)
    DV_BLK, R_BLK, K_BLK, V_BLK = 2, 3, 4, 5
    LO_BLK = (6 * RWKV_W) // (3 * LANES)

    qn_p = jnp.tile(qn, 2)[None]
    kn_p = jnp.tile(kn, 2)[None]
    q, k, dk, dv = _diff_qk(p, qn_p, kn_p, tabs_d)

    n_c = DEC_BATCH * PAST_LEN
    cache = (cache_k.reshape(n_c, DIFF_W), cache_v.reshape(n_c, DIFF_W))
    dgn = diff_gn[None]
    o_ctx = _diff_attn(lam, q, k, p, DV_BLK * DIFF_HEADS, None, dgn, BATCH, SEQ, 0, SEQ, lam_init, DIFF_HEADS)
    o_lat = _diff_attn(lam, q, k, p, DV_BLK * DIFF_HEADS, cache, dgn, DEC_BATCH, DEC_SEQ, N_CTX, 256, lam_init, 1)

    zero = jnp.zeros((RWKV_W_LORA, RWKV_W), F32)
    wup_bd = jnp.concatenate([jnp.concatenate([w_up[0], zero], 1), jnp.concatenate([zero, w_up[1]], 1)], 0)
    aup_bd = jnp.concatenate([jnp.concatenate([a_up[0], zero], 1), jnp.concatenate([zero, a_up[1]], 1)], 0)
    op, g, bonus = _rwkv_pre(p, R_BLK, K_BLK, V_BLK, LO_BLK, wup_bd, aup_bd, g_up, w0.reshape(1, -1),
                             a0.reshape(1, -1), k_k[None], k_a[None], r_k.reshape(1, -1))
    op4 = op.reshape(N_TOK // SEQ, SEQ, _SCAN_NJ, LANES)
    y_ctx, st_ctx = _rwkv_scan_ctx(op4)
    s0_lat = jnp.transpose(state_rwkv, (4, 1, 0, 2, 3)).reshape(RWKV_HS, _LAT_STATES, RWKV_HS)
    y_lat_f, y_lat_b = _rwkv_scan_lat(op4, _value_split_layout(s0_lat))
    rw_o = _rwkv_post(y_ctx, y_lat_f, y_lat_b, bonus, p, V_BLK, g, gn[None])

    mix = [(o_ctx, o_lat), rw_o]
    new_dk = dk.reshape(BATCH, SEQ, DIFF_HEADS, 2, DIFF_DH)
    new_dv = dv.reshape(BATCH, SEQ, DIFF_HEADS, 2 * DIFF_DH)
    new_rwkv = jnp.transpose(st_ctx.reshape(2, RWKV_HS, RWKV_HS, BATCH, RWKV_HEADS), (3, 0, 4, 2, 1))
    return mix, new_dk, new_dv, new_rwkv


def kernel(x_prompt, x_sample, cache_mla_ckv, cache_mla_krope, state_ret, cache_diff_k, cache_diff_v, state_rwkv,
           c, c_ctx, ada_w, ada_b, norm_mix_g, norm_ffn_g, w_out, ffn_up, ffn_conv_w, ffn_conv_b, ffn_down,
           a_w_in, mla_q_norm, mla_kv_norm, mla_w_uq, mla_w_ukv, mla_qn, mla_kn, ret_decay, ret_gn,
           b_w_in, diff_qn, diff_kn, diff_lam, diff_gn, rwkv_mu, rwkv_w0, rwkv_w_up, rwkv_a0, rwkv_a_up,
           rwkv_g_up, rwkv_k_k, rwkv_k_a, rwkv_r_k, rwkv_gn):
    x = (x_prompt.reshape(N_CTX, D_MODEL), x_sample.reshape(N_LAT, D_MODEL))
    cond8 = jnp.pad(jnp.concatenate([c_ctx[None], c], 0), ((0, 8 - N_GROUPS), (0, 0)))
    mod = _modulation(cond8, ada_w, ada_b)

    tabs_m = _rope_tables(MLA_ROPE, (MLA_NOPE,))
    tabs_d = _rope_tables(DIFF_DH, (0, DIFF_DH))

    outs = {}
    for l in range(DEPTH):
        j = l // 2
        g_mix = norm_mix_g[l][None]
        if l % 2 == 0:
            mix, outs["ckv"], outs["krope"], outs["ret"] = _even_layer(
                x, mod[l], g_mix, a_w_in[j], mla_q_norm[j], mla_kv_norm[j], mla_w_uq[j], mla_w_ukv[j], mla_qn[j],
                mla_kn[j], ret_decay[j], ret_gn[j], cache_mla_ckv[:, j], cache_mla_krope[:, j], state_ret[:, j],
                tabs_m)
        else:
            lam_init = 0.8 - 0.6 * math.exp(-0.3 * l)
            mix, outs["dk"], outs["dv"], outs["rwkv"] = _odd_layer(
                x, mod[l], g_mix, b_w_in[j], diff_qn[j], diff_kn[j], diff_lam[j], diff_gn[j], rwkv_mu[j],
                rwkv_w0[j], rwkv_w_up[j], rwkv_a0[j], rwkv_a_up[j], rwkv_g_up[j], rwkv_k_k[j], rwkv_k_a[j],
                rwkv_r_k[j], rwkv_gn[j], cache_diff_k[:, j], cache_diff_v[:, j], state_rwkv[:, j], tabs_d, lam_init)
        x = _resid_proj(mix, w_out, l, x, mod[l], 2)
        act = _ffn_up(x, norm_ffn_g[l][None], mod[l], ffn_up, l, ffn_conv_w[l], ffn_conv_b[l])
        x = _resid_proj([act], ffn_down, l, x, mod[l], 5, split_out=(l == DEPTH - 1))

    y_prompt = x[0].reshape(BATCH, SEQ, D_MODEL)
    y_sample = x[1].reshape(DEC_BATCH, DEC_SEQ, D_MODEL)
    return (y_prompt, y_sample, outs["ckv"][:, None], outs["krope"][:, None], outs["ret"][:, None],
            outs["dk"][:, None], outs["dv"][:, None], outs["rwkv"][:, None])
```

```python
import functools
import math

import numpy as np
import jax
import jax.numpy as jnp
from jax import lax
from jax.experimental import pallas as pl
from jax.experimental.pallas import tpu as pltpu

D_MODEL = 1024
BATCH = 16
SEQ = 256
DEPTH = 2
DEC_BATCH = 2
DEC_SEQ = 1024
PAST_LEN = 512
GRID_W = 64
EPS = 1e-6
ROPE_BASE = 10000.0

MLA_HEADS = 8
MLA_Q_RANK = 256
MLA_KV_RANK = 128
MLA_NOPE = 64
MLA_ROPE = 32
MLA_V = 64
MLA_QK = MLA_NOPE + MLA_ROPE
RET_HEADS = 4
RET_DK = 64
RET_DV = 128
DIFF_HEADS = 4
DIFF_DH = 64
DIFF_W = DIFF_HEADS * 2 * DIFF_DH
RWKV_HEADS = 8
RWKV_HS = 64
RWKV_W = RWKV_HEADS * RWKV_HS
RWKV_W_LORA = 64
D_FF = 2816

N_CTX = BATCH * SEQ
N_LAT = DEC_BATCH * DEC_SEQ
N_TOK = N_CTX + N_LAT
N_GROUPS = 1 + DEC_BATCH

LANES = 128
VMEM_LIMIT = 56 * 1024 * 1024

_PREC = lax.Precision.HIGHEST
F32 = jnp.float32


def _dot_tn(a, b):
    return lax.dot_general(a, b, (((0,), (0,)), ((), ())), precision=_PREC, preferred_element_type=F32)


BF16 = jnp.bfloat16


def _dot_bf16(a, b):
    return jnp.dot(a.astype(BF16), b.astype(BF16), preferred_element_type=F32)


def _dot_nt_bf16(a, b):
    return lax.dot_general(a.astype(BF16), b.astype(BF16), (((1,), (1,)), ((), ())), preferred_element_type=F32)


def _params(*sem):
    return pltpu.CompilerParams(dimension_semantics=sem, vmem_limit_bytes=VMEM_LIMIT)


def _sigmoid(x):
    return 1.0 / (1.0 + jnp.exp(-x))


def _silu(x):
    return x * _sigmoid(x)


def _softplus(x):
    return jnp.maximum(x, 0.0) + jnp.log(1.0 + jnp.exp(-jnp.abs(x)))


def _rms(x, n):
    return x * lax.rsqrt(jnp.sum(x * x, axis=-1, keepdims=True) * (1.0 / n) + EPS)


def _lane_lo(shape):
    return lax.broadcasted_iota(jnp.int32, shape, len(shape) - 1) < 64


def _seg64_sum(x):
    lo = _lane_lo(x.shape)
    s_lo = jnp.sum(jnp.where(lo, x, 0.0), axis=-1, keepdims=True)
    s_hi = jnp.sum(jnp.where(lo, 0.0, x), axis=-1, keepdims=True)
    return jnp.where(lo, s_lo, s_hi)


_SUBLANES = 8


def _seq_neighbours(p, tile, tile_rows):
    is_ctx = tile * tile_rows < N_CTX
    sub = lax.broadcasted_iota(jnp.int32, (_SUBLANES, 1), 0)

    def shifted(rolled, edge_sublane, group_of_seq):
        pieces, start = [], 0
        for q in range(tile_rows // SEQ):
            g0 = q * SEQ + group_of_seq
            outer = (q == 0) if group_of_seq == 0 else (q == tile_rows // SEQ - 1)
            edge = (sub == edge_sublane) if outer else ((sub == edge_sublane) & is_ctx)
            pieces += [rolled[start:g0], jnp.where(edge, 0.0, rolled[g0:g0 + _SUBLANES])]
            start = g0 + _SUBLANES
        pieces.append(rolled[start:])
        return jnp.concatenate([x for x in pieces if x.shape[0]], axis=0)

    prev = shifted(pltpu.roll(p, 1, axis=0), 0, 0)
    nxt = shifted(pltpu.roll(p, tile_rows - 1, axis=0), _SUBLANES - 1, SEQ - _SUBLANES)
    return prev, nxt


def _group_of_tile(i, tile_rows):
    row = i * tile_rows
    return jnp.where(row < N_CTX, 0, 1 + (row - N_CTX) // DEC_SEQ)


def _modulation_kernel(c_ref, w_ref, b_ref, o_ref):
    o_ref[0] = _dot_bf16(_silu(c_ref[...]), w_ref[0]) + b_ref[0]


def _modulation(cond8, ada_w, ada_b):
    tn = 512
    n = 6 * D_MODEL
    out = pl.pallas_call(
        _modulation_kernel,
        grid=(DEPTH, n // tn),
        in_specs=[pl.BlockSpec((8, D_MODEL), lambda l, j: (0, 0)),
                  pl.BlockSpec((1, D_MODEL, tn), lambda l, j: (l, 0, j)),
                  pl.BlockSpec((1, 1, tn), lambda l, j: (l, 0, j))],
        out_specs=pl.BlockSpec((1, 8, tn), lambda l, j: (l, 0, j)),
        out_shape=jax.ShapeDtypeStruct((DEPTH, 8, n), F32),
        compiler_params=_params("parallel", "parallel"),
        name="modulation",
    )(cond8, ada_w, ada_b.reshape(DEPTH, 1, n))
    m = out[:, :N_GROUPS].reshape(DEPTH, N_GROUPS, 6, D_MODEL)
    return jnp.pad(m, ((0, 0), (0, 0), (0, 2), (0, 0)))


_TM_SEQ = 1024


def _norm_mod(x, g, mod, off):
    return _rms(x, D_MODEL) * g * (1.0 + mod[off + 1:off + 2, :]) + mod[off:off + 1, :]


def _row_split_specs(tm, width_block, col):
    n_ctx = N_CTX // tm
    return [pl.BlockSpec((tm, width_block), lambda i, j: (jnp.minimum(i, n_ctx - 1), col(j))),
            pl.BlockSpec((tm, width_block), lambda i, j: (jnp.maximum(i - n_ctx, 0), col(j)))]


def _for_row_tile(i, tm, ctx_ref, lat_ref, fn):
    @pl.when(i < N_CTX // tm)
    def _():
        fn(ctx_ref)

    @pl.when(i >= N_CTX // tm)
    def _():
        fn(lat_ref)


def _inproj_kernel(*refs, shift_from, split_x):
    x_refs, (g_ref, mod_ref, w_ref, mu_ref, o_ref, h_ref) = refs[:-6], refs[-6:]
    i = pl.program_id(0)

    @pl.when(pl.program_id(1) == 0)
    def _():
        def norm(x_ref):
            h_ref[...] = _norm_mod(x_ref[...], g_ref[...], mod_ref[0], 0).astype(BF16)

        if split_x:
            _for_row_tile(i, _TM_SEQ, x_refs[0], x_refs[1], norm)
        else:
            norm(x_refs[0])

    p = _dot_bf16(h_ref[...], w_ref[...])
    if shift_from is None:
        o_ref[...] = p
    else:
        @pl.when(pl.program_id(1) < shift_from)
        def _():
            o_ref[...] = p

        @pl.when(pl.program_id(1) >= shift_from)
        def _():
            prev, nxt = _seq_neighbours(p, i, _TM_SEQ)
            o_ref[...] = p + (0.5 * (prev + nxt) - p) * mu_ref[...]


def _inproj(x, g, mod, w, mu, tn, shift_from):
    n = w.shape[1]
    tm = _TM_SEQ
    split_x = isinstance(x, (tuple, list))
    x_specs = _row_split_specs(tm, D_MODEL, lambda j: 0) if split_x else [pl.BlockSpec((tm, D_MODEL), lambda i, j: (i, 0))]
    return pl.pallas_call(
        functools.partial(_inproj_kernel, shift_from=shift_from, split_x=split_x),
        grid=(N_TOK // tm, n // tn),
        in_specs=x_specs
        + [pl.BlockSpec((1, D_MODEL), lambda i, j: (0, 0)),
                  pl.BlockSpec((1, 8, D_MODEL), lambda i, j: (_group_of_tile(i, tm), 0, 0)),
                  pl.BlockSpec((D_MODEL, tn), lambda i, j: (0, j)),
                  pl.BlockSpec((1, tn), lambda i, j: (0, j))],
        out_specs=pl.BlockSpec((tm, tn), lambda i, j: (i, j)),
        out_shape=jax.ShapeDtypeStruct((N_TOK, n), F32),
        scratch_shapes=[pltpu.VMEM((tm, D_MODEL), BF16)],
        compiler_params=_params("parallel", "arbitrary"),
        name="inproj" if shift_from is None else "inproj_shift",
    )(*(x if split_x else [x]), g, mod, w, mu)


def _resid_kernel(*refs, gate_row, widths, split, split_x, split_out):
    n_in = sum(2 if sp else 1 for sp in split)
    a_refs = refs[:n_in]
    w_ref = refs[n_in]
    n_x = 2 if split_x else 1
    x_refs = refs[n_in + 1:n_in + 1 + n_x]
    mod_ref = refs[n_in + 1 + n_x]
    o_ref = refs[n_in + 2 + n_x:-1] if split_out else refs[n_in + 2 + n_x]
    a_bf_ref = refs[-1]
    i = pl.program_id(0)

    @pl.when(pl.program_id(1) == 0)
    def _():
        k0, r = 0, 0
        for width, sp in zip(widths, split):
            cols = slice(k0, k0 + width)
            if sp:
                ctx_ref, lat_ref = a_refs[r], a_refs[r + 1]

                @pl.when(i < N_CTX // _TM_SEQ)
                def _():
                    a_bf_ref[:, cols] = ctx_ref[...].astype(BF16)

                @pl.when(i >= N_CTX // _TM_SEQ)
                def _():
                    a_bf_ref[:, cols] = lat_ref[...].astype(BF16)
            else:
                a_bf_ref[:, cols] = a_refs[r][...].astype(BF16)
            k0 += width
            r += 2 if sp else 1

    d = mod_ref[0, gate_row:gate_row + 1, :] * _dot_bf16(a_bf_ref[...], w_ref[0])
    if split_x:
        def add(x_ref):
            o_ref[...] = x_ref[...] + d

        _for_row_tile(i, _TM_SEQ, x_refs[0], x_refs[1], add)
        return
    y = x_refs[0][...] + d
    if not split_out:
        o_ref[...] = y
    else:
        ctx_o_ref, lat_o_ref = o_ref

        @pl.when(i < N_CTX // _TM_SEQ)
        def _():
            ctx_o_ref[...] = y

        @pl.when(i >= N_CTX // _TM_SEQ)
        def _():
            lat_o_ref[...] = y


def _resid_proj(acts, w, layer, x, mod, gate_row, split_out=False):
    tm = _TM_SEQ
    split_x = isinstance(x, (tuple, list))
    assert not (split_x and split_out)
    n_ctx = N_CTX // tm
    split = [isinstance(a, (tuple, list)) for a in acts]
    widths = [a[0].shape[1] if sp else a.shape[1] for a, sp in zip(acts, split)]
    k = sum(widths)
    tn = 512 if k <= D_MODEL else 256
    nj = D_MODEL // tn
    in_specs, operands = [], []
    for a, width, sp in zip(acts, widths, split):
        if sp:
            in_specs += [pl.BlockSpec((tm, width), lambda i, j: (jnp.minimum(i, n_ctx - 1), 0)),
                         pl.BlockSpec((tm, width), lambda i, j: (jnp.maximum(i - n_ctx, 0), 0))]
            operands += list(a)
        else:
            in_specs.append(pl.BlockSpec((tm, width), lambda i, j: (i, 0)))
            operands.append(a)
    if split_out:
        out_specs = [pl.BlockSpec((tm, tn), lambda i, j: (jnp.minimum(i, n_ctx - 1), jnp.where(i < n_ctx, j, nj - 1))),
                     pl.BlockSpec((tm, tn), lambda i, j: (jnp.maximum(i - n_ctx, 0), jnp.where(i < n_ctx, 0, j)))]
        out_shape = [jax.ShapeDtypeStruct((N_CTX, D_MODEL), F32), jax.ShapeDtypeStruct((N_LAT, D_MODEL), F32)]
        sem = ("arbitrary", "arbitrary")
    else:
        out_specs = pl.BlockSpec((tm, tn), lambda i, j: (i, j))
        out_shape = jax.ShapeDtypeStruct((N_TOK, D_MODEL), F32)
        sem = ("parallel", "arbitrary")
    return pl.pallas_call(
        functools.partial(_resid_kernel, gate_row=gate_row, widths=tuple(widths), split=tuple(split),
                          split_x=split_x, split_out=split_out),
        grid=(N_TOK // tm, nj),
        in_specs=in_specs
        + [pl.BlockSpec((1, k, tn), lambda i, j: (layer, 0, j))]
        + (_row_split_specs(tm, tn, lambda j: j) if split_x else [pl.BlockSpec((tm, tn), lambda i, j: (i, j))])
        + [pl.BlockSpec((1, 8, tn), lambda i, j: (_group_of_tile(i, tm), 0, j))],
        out_specs=out_specs,
        out_shape=out_shape,
        scratch_shapes=[pltpu.VMEM((tm, k), BF16)],
        compiler_params=_params(*sem),
        name="resid_proj",
    )(*operands, w, *(x if split_x else [x]), mod)


def _ffn_up_kernel(x_ref, g_ref, mod_ref, wa_ref, wb_ref, cwa_ref, cwb_ref, cba_ref, cbb_ref, o_ref, h_ref):
    i = pl.program_id(0)

    @pl.when(pl.program_id(1) == 0)
    def _():
        h_ref[...] = _norm_mod(x_ref[...], g_ref[...], mod_ref[0], 3).astype(BF16)

    h = h_ref[...]

    def conv(w_ref, cw_ref, cb_ref):
        u = _dot_bf16(h, w_ref[0])
        prev, nxt = _seq_neighbours(u, i, _TM_SEQ)
        return prev * cw_ref[0:1, :] + u * cw_ref[1:2, :] + nxt * cw_ref[2:3, :] + cb_ref[...]

    o_ref[...] = (_silu(conv(wa_ref, cwa_ref, cba_ref)) * conv(wb_ref, cwb_ref, cbb_ref)).astype(o_ref.dtype)


def _ffn_up(x, g, mod, up, layer, cw, cb):
    tm, tn = _TM_SEQ, 256
    nb = D_FF // tn
    cb = cb.reshape(1, 2 * D_FF)
    return pl.pallas_call(
        _ffn_up_kernel,
        grid=(N_TOK // tm, nb),
        in_specs=[pl.BlockSpec((tm, D_MODEL), lambda i, j: (i, 0)),
                  pl.BlockSpec((1, D_MODEL), lambda i, j: (0, 0)),
                  pl.BlockSpec((1, 8, D_MODEL), lambda i, j: (_group_of_tile(i, tm), 0, 0)),
                  pl.BlockSpec((1, D_MODEL, tn), lambda i, j: (layer, 0, j)),
                  pl.BlockSpec((1, D_MODEL, tn), lambda i, j: (layer, 0, j + nb)),
                  pl.BlockSpec((3, tn), lambda i, j: (0, j)),
                  pl.BlockSpec((3, tn), lambda i, j: (0, j + nb)),
                  pl.BlockSpec((1, tn), lambda i, j: (0, j)),
                  pl.BlockSpec((1, tn), lambda i, j: (0, j + nb))],
        out_specs=pl.BlockSpec((tm, tn), lambda i, j: (i, j)),
        out_shape=jax.ShapeDtypeStruct((N_TOK, D_FF), BF16),
        scratch_shapes=[pltpu.VMEM((tm, D_MODEL), BF16)],
        compiler_params=_params("parallel", "arbitrary"),
        name="ffn_up",
    )(x, g, mod, up, up, cw, cw, cb, cb)


_ROPE_TM = 512


def _rope(y, c, s1, s2):
    return y * c + pltpu.roll(y, 1, axis=1) * s1 + pltpu.roll(y, LANES - 1, axis=1) * s2


def _rope_tables(rot_dim, lane_offsets):
    t = np.arange(DEC_SEQ)
    row, col = t // GRID_W, t % GRID_W
    n_freq = rot_dim // 4
    inv = ROPE_BASE ** (-np.arange(n_freq, dtype=np.float64) / n_freq)
    ang = np.concatenate([row[:, None] * inv, col[:, None] * inv], -1)
    cos, sin = np.cos(ang), np.sin(ang)
    n = _ROPE_TM + DEC_SEQ
    c, s1, s2 = np.ones((n, LANES)), np.zeros((n, LANES)), np.zeros((n, LANES))
    for a in lane_offsets:
        even = a + 2 * np.arange(rot_dim // 2)
        c[_ROPE_TM:, even] = cos
        c[_ROPE_TM:, even + 1] = cos
        s1[_ROPE_TM:, even + 1] = sin
        s2[_ROPE_TM:, even] = -sin
    return tuple(jnp.asarray(x, F32) for x in (c, s1, s2))


def _store_rotated(dst_ref, sl, y, tabs, rotate):
    if rotate is False:
        dst_ref[:, sl] = y
        return

    @pl.when(rotate)
    def _():
        dst_ref[:, sl] = _rope(y, *tabs)

    @pl.when(jnp.logical_not(rotate))
    def _():
        dst_ref[:, sl] = y


def _latent_tile():
    return pl.program_id(0) >= N_CTX // _ROPE_TM


def _rope_block(i):
    row = i * _ROPE_TM
    return jnp.where(row < N_CTX, 0, 1 + ((row - N_CTX) % DEC_SEQ) // _ROPE_TM)


def _mla_q_kernel(cq_ref, qnorm_ref, w_ref, qn_ref, c_ref, s1_ref, s2_ref, o_ref):
    xn = _rms(cq_ref[...], MLA_Q_RANK) * qnorm_ref[...]
    y = _dot_bf16(xn, w_ref[...])
    tabs = (c_ref[...], s1_ref[...], s2_ref[...])
    for h in range(MLA_HEADS):
        sl = slice(h * LANES, (h + 1) * LANES)
        _store_rotated(o_ref, sl, _rms(y[:, sl], MLA_QK) * qn_ref[...], tabs, _latent_tile())


def _mla_q(p, q_norm, w_uq_p, qn_p, tabs):
    tm = _ROPE_TM
    hw = MLA_HEADS * LANES
    tab_spec = pl.BlockSpec((tm, LANES), lambda i: (_rope_block(i), 0))
    return pl.pallas_call(
        _mla_q_kernel,
        grid=(N_TOK // tm,),
        in_specs=[pl.BlockSpec((tm, MLA_Q_RANK), lambda i: (i, 0)),
                  pl.BlockSpec((1, MLA_Q_RANK), lambda i: (0, 0)),
                  pl.BlockSpec((MLA_Q_RANK, hw), lambda i: (0, 0)),
                  pl.BlockSpec((1, LANES), lambda i: (0, 0)),
                  tab_spec, tab_spec, tab_spec],
        out_specs=pl.BlockSpec((tm, hw), lambda i: (i, 0)),
        out_shape=jax.ShapeDtypeStruct((N_TOK, hw), F32),
        compiler_params=_params("parallel"),
        name="mla_q",
    )(p, q_norm, w_uq_p, qn_p, *tabs)


def _mla_kv_kernel(ckv_ref, kr_ref, kvn_ref, wk_ref, wv_ref, kn_ref, c_ref, s1_ref, s2_ref,
                   k_ref, v_ref, ckvn_ref, *, norm_ckv):
    ckv = ckv_ref[...]
    if norm_ckv:
        ckv = _rms(ckv, MLA_KV_RANK) * kvn_ref[...]
    ckvn_ref[...] = ckv
    ckv_bf = ckv.astype(BF16)
    kk = _dot_bf16(ckv_bf, wk_ref[...])
    v_ref[...] = _dot_bf16(ckv_bf, wv_ref[...])
    kr = kr_ref[...]
    tabs = (c_ref[...], s1_ref[...], s2_ref[...])
    for h in range(MLA_HEADS):
        sl = slice(h * LANES, (h + 1) * LANES)
        _store_rotated(k_ref, sl, _rms(kk[:, sl] + kr, MLA_QK) * kn_ref[...], tabs,
                       _latent_tile() if norm_ckv else False)


def _mla_kv(ckv_src, ckv_blk, kr_src, kr_blk, kv_norm, wk_p, wv_p, kn_p, tabs, n_rows, own_tokens):
    tm = _ROPE_TM
    hw = MLA_HEADS * LANES
    tab_spec = pl.BlockSpec((tm, LANES), (lambda i: (_rope_block(i), 0)) if own_tokens else (lambda i: (0, 0)))
    return pl.pallas_call(
        functools.partial(_mla_kv_kernel, norm_ckv=own_tokens),
        grid=(n_rows // tm,),
        in_specs=[pl.BlockSpec((tm, LANES), lambda i: (i, ckv_blk)),
                  pl.BlockSpec((tm, LANES), lambda i: (i, kr_blk)),
                  pl.BlockSpec((1, LANES), lambda i: (0, 0)),
                  pl.BlockSpec((MLA_KV_RANK, hw), lambda i: (0, 0)),
                  pl.BlockSpec((MLA_KV_RANK, hw), lambda i: (0, 0)),
                  pl.BlockSpec((1, LANES), lambda i: (0, 0)),
                  tab_spec, tab_spec, tab_spec],
        out_specs=[pl.BlockSpec((tm, hw), lambda i: (i, 0)),
                   pl.BlockSpec((tm, hw), lambda i: (i, 0)),
                   pl.BlockSpec((tm, LANES), lambda i: (i, 0))],
        out_shape=[jax.ShapeDtypeStruct((n_rows, hw), F32),
                   jax.ShapeDtypeStruct((n_rows, hw), F32),
                   jax.ShapeDtypeStruct((n_rows, LANES), F32)],
        compiler_params=_params("parallel"),
        name="mla_kv",
    )(ckv_src, kr_src, kv_norm, wk_p, wv_p, kn_p, *tabs)


_LOG2E = math.log2(math.e)


def _softmax_parts(scores, scale):
    m = functools.reduce(jnp.maximum, [jnp.max(sc, axis=-1, keepdims=True) for sc in scores])
    ps = [jnp.exp2((sc - m) * (scale * _LOG2E)) for sc in scores]
    return ps, sum(jnp.sum(p, axis=-1, keepdims=True) for p in ps)


def _attend(q, sources, sl, scale):
    ps, l = _softmax_parts([_dot_nt_bf16(q, k_ref[:, sl]) for k_ref, _ in sources], scale)
    return sum(_dot_bf16(p, v_ref[:, sl]) for p, (_, v_ref) in zip(ps, sources)), l


def _mla_attn_kernel(*refs, pairs):
    q_ref, o_ref = refs[0], refs[-1]
    sources = [refs[1:3]] + ([refs[3:5]] if len(refs) == 6 else [])
    scale = MLA_QK ** -0.5
    for pr in range(pairs):
        outs = []
        for h in range(2):
            sl = slice((2 * pr + h) * LANES, (2 * pr + h + 1) * LANES)
            o, l = _attend(q_ref[:, sl], sources, sl, scale)
            outs.append(o / l)
        o_ref[:, pr * LANES:(pr + 1) * LANES] = (outs[0] + pltpu.roll(outs[1], MLA_V, axis=1)).astype(o_ref.dtype)


def _mla_attn(q, k, v, cache, batch, n, row0, tq, pairs):
    nqb = n // tq
    qb0 = row0 // tq
    kb0 = row0 // n
    wide = 2 * LANES * pairs
    kv_spec = pl.BlockSpec((n, wide), lambda b, h, i: (kb0 + b, h))
    in_specs, operands = [pl.BlockSpec((tq, wide), lambda b, h, i: (qb0 + b * nqb + i, h)), kv_spec, kv_spec], [q, k, v]
    if cache is not None:
        in_specs += [pl.BlockSpec((PAST_LEN, wide), lambda b, h, i: (b, h))] * 2
        operands += list(cache)
    return pl.pallas_call(
        functools.partial(_mla_attn_kernel, pairs=pairs),
        grid=(batch, MLA_HEADS // (2 * pairs), nqb),
        in_specs=in_specs,
        out_specs=pl.BlockSpec((tq, LANES * pairs), lambda b, h, i: (b * nqb + i, h)),
        out_shape=jax.ShapeDtypeStruct((batch * n, MLA_HEADS * MLA_V), BF16),
        compiler_params=_params("parallel", "parallel", "arbitrary"),
        name="mla_attn",
    )(*operands)


def _ret_kernel(lg_ref, q_ref, k_ref, v_ref, rg_ref, s0_ref, gn_ref, *out_and_scratch, n, tq, want_state):
    if want_state:
        o_ref, st_ref, decay_ref = out_and_scratch
    else:
        o_ref, decay_ref = out_and_scratch
    pair, qi, b = pl.program_id(0), pl.program_id(1), pl.program_id(2)
    q = q_ref[...]
    k = k_ref[...] * (RET_DK ** -0.5)
    lo = _lane_lo((1, LANES))
    row = (qi * tq + lax.broadcasted_iota(jnp.int32, (tq, 1), 0)).astype(F32)

    @pl.when(b == 0)
    def _():
        col = lax.broadcasted_iota(jnp.int32, (1, n), 1).astype(F32)
        diff = row - col
        for h in range(2):
            lgf = lg_ref[0, 2 * pair + h]
            lgb = lg_ref[1, 2 * pair + h]
            decay_ref[h] = (jnp.where(diff >= 0, jnp.exp(lgf * jnp.maximum(diff, 0.0)), 0.0)
                            + jnp.where(diff <= 0, jnp.exp(lgb * jnp.maximum(-diff, 0.0)), 0.0))

    for h in range(2):
        lgf = lg_ref[0, 2 * pair + h]
        lgb = lg_ref[1, 2 * pair + h]
        mask = lo if h == 0 else jnp.logical_not(lo)
        qh = jnp.where(mask, q, 0.0)
        vh = v_ref[:, h * LANES:(h + 1) * LANES]
        o = _dot_bf16(_dot_nt_bf16(qh, k) * decay_ref[h], vh)
        o = o + _dot_bf16(qh * jnp.exp(lgf * (row + 1.0)), s0_ref[0, 0])
        o = o + _dot_bf16(qh * jnp.exp(lgb * (n - row)), s0_ref[0, 1])
        y = _rms(o, RET_DV) * gn_ref[:, h * LANES:(h + 1) * LANES]
        o_ref[:, h * LANES:(h + 1) * LANES] = (_silu(rg_ref[:, h * LANES:(h + 1) * LANES]) * y).astype(o_ref.dtype)

    if want_state:
        pos = lax.broadcasted_iota(jnp.int32, (n, 1), 0).astype(F32)
        for d in range(2):
            acc = None
            for h in range(2):
                lg = lg_ref[d, 2 * pair + h]
                mask = lo if h == 0 else jnp.logical_not(lo)
                expo = (n - 1.0 - pos) if d == 0 else pos
                kd = jnp.where(mask, k * jnp.exp(lg * expo), 0.0)
                term = _dot_tn(kd, v_ref[:, h * LANES:(h + 1) * LANES])
                acc = term if acc is None else acc + term
            lg_rows = jnp.where(lax.broadcasted_iota(jnp.int32, (LANES, 1), 0) < 64,
                                lg_ref[d, 2 * pair], lg_ref[d, 2 * pair + 1])
            st_ref[0, d] = acc + s0_ref[0, d] * jnp.exp(lg_rows * n)


def _retention(log_g, p, q_blk, k_blk, v_blk, g_blk, s0, gn, batch, n, row0, tq, want_state):
    nqb = n // tq
    assert not want_state or nqb == 1
    qb0 = row0 // tq
    kb0 = row0 // n
    pairs = RET_HEADS // 2
    out_specs = [pl.BlockSpec((tq, 2 * LANES), lambda h, i, b: (b * nqb + i, h))]
    out_shape = [jax.ShapeDtypeStruct((batch * n, RET_HEADS * RET_DV), BF16)]
    if want_state:
        out_specs.append(pl.BlockSpec((1, 2, LANES, LANES), lambda h, i, b: (b, 0, h, 0)))
        out_shape.append(jax.ShapeDtypeStruct((batch, 2, RET_HEADS * RET_DK, RET_DV), F32))
    outs = pl.pallas_call(
        functools.partial(_ret_kernel, n=n, tq=tq, want_state=want_state),
        grid=(pairs, nqb, batch),
        in_specs=[pl.BlockSpec(memory_space=pltpu.SMEM),
                  pl.BlockSpec((tq, LANES), lambda h, i, b: (qb0 + b * nqb + i, q_blk + h)),
                  pl.BlockSpec((n, LANES), lambda h, i, b: (kb0 + b, k_blk + h)),
                  pl.BlockSpec((n, 2 * LANES), lambda h, i, b: (kb0 + b, v_blk // 2 + h)),
                  pl.BlockSpec((tq, 2 * LANES), lambda h, i, b: (qb0 + b * nqb + i, g_blk // 2 + h)),
                  pl.BlockSpec((1, 2, LANES, LANES), lambda h, i, b: (b, 0, h, 0)),
                  pl.BlockSpec((1, 2 * LANES), lambda h, i, b: (0, h))],
        out_specs=out_specs,
        out_shape=out_shape,
        scratch_shapes=[pltpu.VMEM((2, tq, n), F32)],
        compiler_params=_params("parallel", "parallel", "arbitrary"),
        name="retention",
    )(log_g, p, p, p, p, s0, gn)
    return outs if want_state else (outs[0], None)


def _diff_qk_kernel(q_ref, k_ref, v_ref, qn_ref, kn_ref, c_ref, s1_ref, s2_ref, qo_ref, ko_ref, dk_ref, dv_ref):
    tabs = (c_ref[...], s1_ref[...], s2_ref[...])
    is_ctx = jnp.logical_not(_latent_tile())
    for src, gain, dst in ((q_ref, qn_ref, qo_ref), (k_ref, kn_ref, ko_ref)):
        for h in range(DIFF_HEADS):
            sl = slice(h * LANES, (h + 1) * LANES)
            y = src[:, sl]
            y = y * lax.rsqrt(_seg64_sum(y * y) * (1.0 / DIFF_DH) + EPS) * gain[...]
            _store_rotated(dst, sl, y, tabs, _latent_tile())
            if dst is ko_ref:
                @pl.when(is_ctx)
                def _():
                    dk_ref[:, h, 0, :] = y[:, :DIFF_DH]
                    dk_ref[:, h, 1, :] = pltpu.roll(y, DIFF_DH, axis=1)[:, :DIFF_DH]
                    dv_ref[:, h, :] = v_ref[:, sl]


def _diff_qk(p, qn_p, kn_p, tabs):
    tm = _ROPE_TM
    n_ctx = N_CTX // tm
    tab_spec = pl.BlockSpec((tm, LANES), lambda i: (_rope_block(i), 0))
    ctx_blk = lambda i: jnp.minimum(i, n_ctx - 1)
    return pl.pallas_call(
        _diff_qk_kernel,
        grid=(N_TOK // tm,),
        in_specs=[pl.BlockSpec((tm, DIFF_W), lambda i: (i, 0)),
                  pl.BlockSpec((tm, DIFF_W), lambda i: (i, 1)),
                  pl.BlockSpec((tm, DIFF_W), lambda i: (i, 2)),
                  pl.BlockSpec((1, LANES), lambda i: (0, 0)),
                  pl.BlockSpec((1, LANES), lambda i: (0, 0)),
                  tab_spec, tab_spec, tab_spec],
        out_specs=[pl.BlockSpec((tm, DIFF_W), lambda i: (i, 0)),
                   pl.BlockSpec((tm, DIFF_W), lambda i: (i, 0)),
                   pl.BlockSpec((tm, DIFF_HEADS, 2, DIFF_DH), lambda i: (ctx_blk(i), 0, 0, 0)),
                   pl.BlockSpec((tm, DIFF_HEADS, 2 * DIFF_DH), lambda i: (ctx_blk(i), 0, 0))],
        out_shape=[jax.ShapeDtypeStruct((N_TOK, DIFF_W), F32)] * 2
        + [jax.ShapeDtypeStruct((N_CTX, DIFF_HEADS, 2, DIFF_DH), F32),
           jax.ShapeDtypeStruct((N_CTX, DIFF_HEADS, 2 * DIFF_DH), F32)],
        compiler_params=_params("arbitrary"),
        name="diff_qk",
    )(p, p, p, qn_p, kn_p, *tabs)


def _diff_attn_kernel(*refs, lam_init, heads):
    lam_ref, q_ref, gn_ref, o_ref = refs[0], refs[1], refs[-2], refs[-1]
    sources = [refs[2:4]] + ([refs[4:6]] if len(refs) == 8 else [])
    lv = lam_ref[...]
    lam = (jnp.exp(jnp.sum(lv[0:1] * lv[1:2], axis=-1, keepdims=True))
           - jnp.exp(jnp.sum(lv[2:3] * lv[3:4], axis=-1, keepdims=True)) + lam_init)
    scale = DIFF_DH ** -0.5
    lo = _lane_lo((1, LANES))
    for h in range(heads):
        sl = slice(h * LANES, (h + 1) * LANES)
        q = q_ref[:, sl]
        kbs = [k_ref[:, sl].astype(BF16) for k_ref, _ in sources]
        ps1, l1 = _softmax_parts([_dot_nt_bf16(jnp.where(lo, q, 0.0), kb) for kb in kbs], scale)
        ps2, l2 = _softmax_parts([_dot_nt_bf16(jnp.where(lo, 0.0, q), kb) for kb in kbs], scale)
        o = sum(_dot_bf16(p1 / l1 - lam * (p2 / l2), v_ref[:, sl])
                for p1, p2, (_, v_ref) in zip(ps1, ps2, sources))
        o_ref[:, sl] = (_rms(o, 2 * DIFF_DH) * gn_ref[:, sl] * (1.0 - lam_init)).astype(o_ref.dtype)


def _diff_attn(lam, q, k, v, v_blk0, cache, gn, batch, n, row0, tq, lam_init, heads):
    nqb = n // tq
    qb0 = row0 // tq
    kb0 = row0 // n
    wide = LANES * heads
    vb0 = v_blk0 // heads
    in_specs = [pl.BlockSpec((4, DIFF_DH), lambda b, h, i: (0, 0)),
                pl.BlockSpec((tq, wide), lambda b, h, i: (qb0 + b * nqb + i, h)),
                pl.BlockSpec((n, wide), lambda b, h, i: (kb0 + b, h)),
                pl.BlockSpec((n, wide), lambda b, h, i: (kb0 + b, vb0 + h))]
    operands = [lam, q, k, v]
    if cache is not None:
        in_specs += [pl.BlockSpec((PAST_LEN, wide), lambda b, h, i: (b, h))] * 2
        operands += list(cache)
    return pl.pallas_call(
        functools.partial(_diff_attn_kernel, lam_init=lam_init, heads=heads),
        grid=(batch, DIFF_HEADS // heads, nqb),
        in_specs=in_specs + [pl.BlockSpec((1, wide), lambda b, h, i: (0, h))],
        out_specs=pl.BlockSpec((tq, wide), lambda b, h, i: (b * nqb + i, h)),
        out_shape=jax.ShapeDtypeStruct((batch * n, DIFF_W), BF16),
        compiler_params=_params("parallel", "parallel", "arbitrary"),
        name="diff_attn",
    )(*operands, gn)


def _seg64_sum_wide(x):
    return jnp.concatenate([_seg64_sum(x[:, j * LANES:(j + 1) * LANES]) for j in range(x.shape[1] // LANES)], axis=1)


_SCAN_SLOTS = 3
_SCAN_DIR_NJ = (_SCAN_SLOTS - 1) * RWKV_HEADS
_SCAN_NJ = 2 * _SCAN_DIR_NJ + RWKV_HEADS


def _rwkv_pre_kernel(r_ref, k_ref, v_ref, lo_ref, wup_ref, aup_ref, gup_ref, w0_ref, a0_ref, kk_ref, ka_ref, rk_ref,
                     op_ref, g_ref, bonus_ref):
    W = RWKV_W
    col = lambda q: slice(q * W, (q + 1) * W)
    r = r_ref[...]
    k = k_ref[...]
    v = v_ref[...]
    lora = lo_ref[...]
    kk = k * kk_ref[...]
    kkn = kk * lax.rsqrt(_seg64_sum_wide(kk * kk) + EPS)
    g_ref[...] = _dot_bf16(_sigmoid(lora[:, 2 * LANES:3 * LANES]), gup_ref[...])
    pre = w0_ref[...] + _dot_bf16(jnp.tanh(lora[:, 0:LANES]), wup_ref[...])
    decay = jnp.exp(-jnp.exp(-_softplus(-pre) - 0.5))
    a = _sigmoid(a0_ref[...] + _dot_bf16(lora[:, LANES:2 * LANES], aup_ref[...]))
    lo = _lane_lo((1, LANES))

    def store_slot(j0, x1, x2):
        for h in range(RWKV_HEADS):
            blk = slice((h // 2) * LANES, (h // 2 + 1) * LANES)
            if h % 2 == 0:
                out = jnp.where(lo, x1[:, blk], pltpu.roll(x2[:, blk], RWKV_HS, axis=1))
            else:
                out = jnp.where(lo, pltpu.roll(x1[:, blk], RWKV_HS, axis=1), x2[:, blk])
            op_ref[:, j0 + h, :] = out

    bonus = None
    for d in range(2):
        a_d = a[:, col(d)]
        k_d = k * (1.0 + (a_d - 1.0) * ka_ref[...])
        t = _seg64_sum_wide(r * k_d * rk_ref[...])
        bonus = t if bonus is None else bonus + t
        store_slot(d * _SCAN_DIR_NJ, kkn, decay[:, col(d)])
        store_slot(d * _SCAN_DIR_NJ + RWKV_HEADS, k_d, kkn * a_d)
    store_slot(2 * _SCAN_DIR_NJ, r, v)
    bonus_ref[...] = bonus


def _rwkv_pre(p, r_blk, k_blk, v_blk, lo_blk, wup_bd, aup_bd, gup, w0, a0, k_k, k_a, r_k):
    tm = 256
    w = RWKV_W
    row = lambda n: pl.BlockSpec((1, n), lambda i: (0, 0))
    full = lambda a, b: pl.BlockSpec((a, b), lambda i: (0, 0))
    return pl.pallas_call(
        _rwkv_pre_kernel,
        grid=(N_TOK // tm,),
        in_specs=[pl.BlockSpec((tm, w), lambda i: (i, r_blk)),
                  pl.BlockSpec((tm, w), lambda i: (i, k_blk)),
                  pl.BlockSpec((tm, w), lambda i: (i, v_blk)),
                  pl.BlockSpec((tm, 3 * LANES), lambda i: (i, lo_blk)),
                  full(LANES, 2 * w), full(LANES, 2 * w), full(LANES, w),
                  row(2 * w), row(2 * w), row(w), row(w), row(w)],
        out_specs=[pl.BlockSpec((tm, _SCAN_NJ, LANES), lambda i: (i, 0, 0)),
                   pl.BlockSpec((tm, w), lambda i: (i, 0)), pl.BlockSpec((tm, w), lambda i: (i, 0))],
        out_shape=[jax.ShapeDtypeStruct((N_TOK, _SCAN_NJ, LANES), F32),
                   jax.ShapeDtypeStruct((N_TOK, w), F32), jax.ShapeDtypeStruct((N_TOK, w), F32)],
        compiler_params=_params("parallel"),
        name="rwkv_pre",
    )(p, p, p, p, wup_bd, aup_bd, gup, w0, a0, k_k, k_a, r_k)


_SCAN_CHUNK = 32
_SCAN_UNROLL = 16


def _rwkv_first_sa(s_ref, sa_ref, kk, n_k):
    nv = s_ref.shape[1]
    chunk = min(_SCAN_CHUNK, nv)
    for c0 in range(0, nv, chunk):
        def body(k, acc):
            return acc + s_ref[k, c0:c0 + chunk, :] * kk(k)
        sa_ref[c0:c0 + chunk, :] = lax.fori_loop(0, n_k, body, jnp.zeros((chunk, LANES), F32), unroll=_SCAN_UNROLL)


def _rwkv_step(s_ref, sa_ref, kk_next, w, kd, b, r, v_at, n_k):
    nv = s_ref.shape[1]
    chunk = min(_SCAN_CHUNK, nv)
    ys = []
    for c0 in range(0, nv, chunk):
        sa = sa_ref[c0:c0 + chunk, :]
        vc = v_at(c0, chunk)

        def body(k, acc):
            y_acc, sa_acc = acc
            s_new = s_ref[k, c0:c0 + chunk, :] * w(k) - sa * b(k) + vc * kd(k)
            s_ref[k, c0:c0 + chunk, :] = s_new
            return y_acc + s_new * r(k), sa_acc + s_new * kk_next(k)

        zero = jnp.zeros((chunk, LANES), F32)
        y_acc, sa_acc = lax.fori_loop(0, n_k, body, (zero, zero), unroll=_SCAN_UNROLL)
        sa_ref[c0:c0 + chunk, :] = sa_acc
        ys.append(y_acc)
    return ys[0] if len(ys) == 1 else jnp.concatenate(ys, axis=0)


_CTX_TB = 32


def _load_scan_operands(kt_ref, t, slabs):
    for s, x in enumerate(slabs):
        xt = x.T
        kt_ref[t, 2 * s] = xt[:RWKV_HS]
        kt_ref[t, 2 * s + 1] = xt[RWKV_HS:]


def _rwkv_scan_ctx_kernel(xd_ref, xs_ref, y_ref, st_ref, s_ref, sa_ref, kt_ref, ys_ref):
    d = pl.program_id(0)
    tb = pl.program_id(1)
    nh = RWKV_HEADS
    step_t = lambda i: jnp.where(d == 0, i, _CTX_TB - 1 - i)

    @pl.when(tb == 0)
    def _():
        s_ref[...] = jnp.zeros_like(s_ref)

    def load_t(t, carry):
        slabs = [jnp.concatenate([xd_ref[b, t, s * nh:(s + 1) * nh, :] for b in range(BATCH)], axis=0)
                 for s in range(_SCAN_SLOTS - 1)]
        slabs.append(jnp.concatenate([xs_ref[b, t] for b in range(BATCH)], axis=0))
        _load_scan_operands(kt_ref, t, slabs)
        return carry

    lax.fori_loop(0, _CTX_TB, load_t, 0, unroll=4)

    t0 = step_t(0)
    _rwkv_first_sa(s_ref, sa_ref, lambda k: kt_ref[t0, 0, pl.ds(k, 1), :], RWKV_HS)

    def step(i, carry):
        t = step_t(i)
        tn = step_t(jnp.minimum(i + 1, _CTX_TB - 1))
        row = lambda q, tt: (lambda k: kt_ref[tt, q, pl.ds(k, 1), :])
        ys_ref[t] = _rwkv_step(s_ref, sa_ref, row(0, tn), row(1, t), row(2, t), row(3, t), row(4, t),
                               lambda c0, n: kt_ref[t, 5, pl.ds(c0, n), :], RWKV_HS)
        return carry

    lax.fori_loop(0, _CTX_TB, step, 0)

    def store_t(i, carry):
        z = jnp.concatenate([ys_ref[2 * i], ys_ref[2 * i + 1]], axis=0).T
        z_odd = pltpu.roll(z, RWKV_HS, axis=1)
        for b in range(BATCH):
            y_ref[b, 2 * i] = z[b * nh:(b + 1) * nh, :RWKV_HS]
            y_ref[b, 2 * i + 1] = z_odd[b * nh:(b + 1) * nh, :RWKV_HS]
        return carry

    lax.fori_loop(0, _CTX_TB // 2, store_t, 0, unroll=4)

    @pl.when(tb == pl.num_programs(1) - 1)
    def _():
        st_ref[0] = s_ref[...]


def _rwkv_scan_ctx(op4):
    nt = SEQ // _CTX_TB
    hs = RWKV_HS
    tblk = lambda d, tb: jnp.where(d == 0, tb, nt - 1 - tb)
    return pl.pallas_call(
        _rwkv_scan_ctx_kernel,
        grid=(2, nt),
        in_specs=[pl.BlockSpec((BATCH, _CTX_TB, _SCAN_DIR_NJ, LANES), lambda d, tb: (0, tblk(d, tb), d, 0)),
                  pl.BlockSpec((BATCH, _CTX_TB, RWKV_HEADS, LANES),
                               lambda d, tb: (0, tblk(d, tb), 2 * _SCAN_DIR_NJ // RWKV_HEADS, 0))],
        out_specs=[pl.BlockSpec((BATCH, _CTX_TB, RWKV_HEADS, hs), lambda d, tb: (0, tblk(d, tb), d, 0)),
                   pl.BlockSpec((1, hs, hs, LANES), lambda d, tb: (d, 0, 0, 0))],
        out_shape=[jax.ShapeDtypeStruct((BATCH, SEQ, 2 * RWKV_HEADS, hs), F32),
                   jax.ShapeDtypeStruct((2, hs, hs, LANES), F32)],
        scratch_shapes=[pltpu.VMEM((hs, hs, LANES), F32), pltpu.VMEM((hs, LANES), F32),
                        pltpu.VMEM((_CTX_TB, 2 * _SCAN_SLOTS, hs, LANES), F32),
                        pltpu.VMEM((_CTX_TB, hs, LANES), F32)],
        compiler_params=_params("parallel", "arbitrary"),
        name="rwkv_scan_ctx",
    )(op4, op4)


_LAT_TB = 64
_LAT_VSPLIT = 4
_LAT_STATES = 2 * DEC_BATCH * RWKV_HEADS
_LAT_VROWS = RWKV_HS // _LAT_VSPLIT


def _rwkv_scan_lat_kernel(xf0_ref, xf0s_ref, xf1_ref, xf1s_ref, xb0_ref, xb0s_ref, xb1_ref, xb1s_ref,
                          s0_ref, yf_ref, yb_ref, s_ref, sa_ref, kt_ref, v_ref, ys_ref):
    @pl.when(pl.program_id(0) == 0)
    def _():
        s_ref[...] = s0_ref[...]

    group = lax.broadcasted_iota(jnp.int32, (_LAT_VROWS, LANES), 1) // _LAT_STATES
    nh = RWKV_HEADS

    def load_t(t, carry):
        tr = _LAT_TB - 1 - t
        for s in range(_SCAN_SLOTS):
            if s < _SCAN_SLOTS - 1:
                heads = slice(s * nh, (s + 1) * nh)
                x = jnp.concatenate([xf0_ref[0, t, heads, :], xf1_ref[0, t, heads, :],
                                     xb0_ref[0, tr, heads, :], xb1_ref[0, tr, heads, :]], axis=0)
            else:
                x = jnp.concatenate([xf0s_ref[0, t], xf1s_ref[0, t], xb0s_ref[0, tr], xb1s_ref[0, tr]], axis=0)
            xt = jnp.concatenate([x] * _LAT_VSPLIT, axis=0).T
            kt_ref[t, 2 * s] = xt[:RWKV_HS]
            if s < _SCAN_SLOTS - 1:
                kt_ref[t, 2 * s + 1] = xt[RWKV_HS:]
            else:
                v = jnp.zeros((_LAT_VROWS, LANES), F32)
                for g in range(_LAT_VSPLIT):
                    r0 = RWKV_HS + g * _LAT_VROWS
                    v = jnp.where(group == g, xt[r0:r0 + _LAT_VROWS, :], v)
                v_ref[t] = v
        return carry

    lax.fori_loop(0, _LAT_TB, load_t, 0, unroll=8)

    _rwkv_first_sa(s_ref, sa_ref, lambda k: kt_ref[0, 0, pl.ds(k, 1), :], RWKV_HS)

    def step(t, carry):
        tn = jnp.minimum(t + 1, _LAT_TB - 1)
        row = lambda q, tt: (lambda k: kt_ref[tt, q, pl.ds(k, 1), :])
        ys_ref[t] = _rwkv_step(s_ref, sa_ref, row(0, tn), row(1, t), row(2, t), row(3, t), row(4, t),
                               lambda c0, n: v_ref[t, pl.ds(c0, n), :], RWKV_HS)
        return carry

    lax.fori_loop(0, _LAT_TB, step, 0)

    def store_t(i, carry):
        rows = [jnp.where(group == g, ys_ref[2 * i + j], 0.0) for j in range(2) for g in range(_LAT_VSPLIT)]
        z = jnp.concatenate(rows, axis=0).T
        y = (z[0:_LAT_STATES] + z[_LAT_STATES:2 * _LAT_STATES]
             + z[2 * _LAT_STATES:3 * _LAT_STATES] + z[3 * _LAT_STATES:4 * _LAT_STATES])
        y_odd = pltpu.roll(y, RWKV_HS, axis=1)
        half = _LAT_STATES // 2
        for b in range(DEC_BATCH):
            rows_f = slice(b * nh, (b + 1) * nh)
            rows_b = slice(half + b * nh, half + (b + 1) * nh)
            yf_ref[b, 2 * i] = y[rows_f, :RWKV_HS]
            yf_ref[b, 2 * i + 1] = y_odd[rows_f, :RWKV_HS]
            yb_ref[b, _LAT_TB - 1 - 2 * i] = y[rows_b, :RWKV_HS]
            yb_ref[b, _LAT_TB - 2 - 2 * i] = y_odd[rows_b, :RWKV_HS]
        return carry

    lax.fori_loop(0, _LAT_TB // 2, store_t, 0, unroll=4)


def _rwkv_scan_lat(op4, s0):
    hs = RWKV_HS
    nv = _LAT_VROWS
    nt = DEC_SEQ // _LAT_TB
    per_seq = SEQ // _LAT_TB
    first = N_CTX // SEQ

    def x_specs(b, d):
        tblk = (lambda tb: tb) if d == 0 else (lambda tb: nt - 1 - tb)
        chunk = lambda tb: first + b * (DEC_SEQ // SEQ) + tblk(tb) // per_seq
        return [pl.BlockSpec((1, _LAT_TB, _SCAN_DIR_NJ, LANES), lambda tb: (chunk(tb), tblk(tb) % per_seq, d, 0)),
                pl.BlockSpec((1, _LAT_TB, RWKV_HEADS, LANES),
                             lambda tb: (chunk(tb), tblk(tb) % per_seq, 2 * _SCAN_DIR_NJ // RWKV_HEADS, 0))]

    y_shape = jax.ShapeDtypeStruct((DEC_BATCH, DEC_SEQ, RWKV_HEADS, hs), F32)
    y_blk = (DEC_BATCH, _LAT_TB, RWKV_HEADS, hs)
    return pl.pallas_call(
        _rwkv_scan_lat_kernel,
        grid=(nt,),
        in_specs=x_specs(0, 0) + x_specs(1, 0) + x_specs(0, 1) + x_specs(1, 1)
        + [pl.BlockSpec((hs, nv, LANES), lambda tb: (0, 0, 0))],
        out_specs=[pl.BlockSpec(y_blk, lambda tb: (0, tb, 0, 0)),
                   pl.BlockSpec(y_blk, lambda tb: (0, nt - 1 - tb, 0, 0))],
        out_shape=[y_shape, y_shape],
        scratch_shapes=[pltpu.VMEM((hs, nv, LANES), F32), pltpu.VMEM((nv, LANES), F32),
                        pltpu.VMEM((_LAT_TB, 2 * _SCAN_SLOTS - 1, hs, LANES), F32),
                        pltpu.VMEM((_LAT_TB, nv, LANES), F32), pltpu.VMEM((_LAT_TB, nv, LANES), F32)],
        compiler_params=_params("arbitrary"),
        name="rwkv_scan_lat",
    )(*([op4] * 8), s0)


def _rwkv_post_kernel(yc_ref, ylf_ref, ylb_ref, bonus_ref, v_ref, g_ref, gn_ref, o_ref):
    def finish(head_sum):
        y = jnp.concatenate([head_sum(h) for h in range(RWKV_HEADS)], axis=1)
        y = y * lax.rsqrt(_seg64_sum_wide(y * y) * (1.0 / RWKV_HS) + EPS) * gn_ref[...]
        o_ref[...] = ((y + bonus_ref[...] * v_ref[...]) * g_ref[...]).astype(o_ref.dtype)

    @pl.when(pl.program_id(0) < N_CTX // SEQ)
    def _():
        finish(lambda h: yc_ref[0, :, h, :] + yc_ref[0, :, RWKV_HEADS + h, :])

    @pl.when(pl.program_id(0) >= N_CTX // SEQ)
    def _():
        finish(lambda h: ylf_ref[0, :, h, :] + ylb_ref[0, :, h, :])


def _rwkv_post(y_ctx, y_lat_f, y_lat_b, bonus, p, v_blk, g, gn):
    tm = SEQ
    w = RWKV_W
    hs = RWKV_HS
    n_ctx = N_CTX // SEQ
    lat = lambda y: y.reshape(N_LAT // SEQ, SEQ, RWKV_HEADS, hs)
    spec = pl.BlockSpec((tm, w), lambda i: (i, 0))
    lat_spec = pl.BlockSpec((1, SEQ, RWKV_HEADS, hs), lambda i: (jnp.maximum(i - n_ctx, 0), 0, 0, 0))
    return pl.pallas_call(
        _rwkv_post_kernel,
        grid=(N_TOK // tm,),
        in_specs=[pl.BlockSpec((1, SEQ, 2 * RWKV_HEADS, hs), lambda i: (jnp.minimum(i, n_ctx - 1), 0, 0, 0)),
                  lat_spec, lat_spec, spec, pl.BlockSpec((tm, w), lambda i: (i, v_blk)), spec,
                  pl.BlockSpec((1, w), lambda i: (0, 0))],
        out_specs=spec,
        out_shape=jax.ShapeDtypeStruct((N_TOK, w), BF16),
        compiler_params=_params("parallel"),
        name="rwkv_post",
    )(y_ctx, lat(y_lat_f), lat(y_lat_b), bonus, p, g, gn)


def _value_split_layout(x):
    lead = x.shape[:-2]
    n = len(lead)
    x = x.reshape(lead + (_LAT_STATES, _LAT_VSPLIT, _LAT_VROWS))
    return jnp.transpose(x, tuple(range(n)) + (n + 2, n + 1, n)).reshape(lead + (_LAT_VROWS, LANES))


_EVEN_HEAD = MLA_Q_RANK + MLA_KV_RANK
_EVEN_PACKED = _EVEN_HEAD + LANES + 2 * RET_HEADS * (RET_DK + RET_DV)


def _pack_even_weight_kernel(w_ref, o_ref):
    w = w_ref[...]
    o_ref[:, 0:_EVEN_HEAD] = w[:, 0:_EVEN_HEAD]
    o_ref[:, _EVEN_HEAD:_EVEN_HEAD + LANES] = jnp.zeros((w.shape[0], LANES), F32)
    o_ref[:, _EVEN_HEAD + MLA_NOPE:_EVEN_HEAD + MLA_QK] = w[:, _EVEN_HEAD:_EVEN_HEAD + MLA_ROPE]
    o_ref[:, _EVEN_HEAD + LANES:] = w[:, _EVEN_HEAD + MLA_ROPE:]


def _pack_even_weight(w_in):
    tm = 256
    return pl.pallas_call(
        _pack_even_weight_kernel,
        grid=(D_MODEL // tm,),
        in_specs=[pl.BlockSpec((tm, w_in.shape[1]), lambda i: (i, 0))],
        out_specs=pl.BlockSpec((tm, _EVEN_PACKED), lambda i: (i, 0)),
        out_shape=jax.ShapeDtypeStruct((D_MODEL, _EVEN_PACKED), F32),
        compiler_params=_params("parallel"),
        name="pack_even_weight",
    )(w_in)


def _even_layer(x, mod, g_mix, w_in, q_norm, kv_norm, w_uq, w_ukv, qn, kn, ret_decay, ret_gn,
                cache_ckv, cache_krope, state_ret, tabs_m):
    w_p = _pack_even_weight(w_in)
    p = _inproj(x, g_mix, mod, w_p, jnp.zeros((1, w_p.shape[1]), F32), 1024, None)
    CKV_BLK, KR_BLK, RQ_BLK, RK_BLK, RV_BLK, RG_BLK = 2, 3, 4, 6, 8, 12

    def head_pad(w, n_head, d_head, c0, c1):
        w = w.reshape(w.shape[0], n_head, d_head)[:, :, c0:c1]
        return jnp.pad(w, ((0, 0), (0, 0), (0, LANES - (c1 - c0)))).reshape(w.shape[0], n_head * LANES)

    w_uq_p = head_pad(w_uq, MLA_HEADS, MLA_QK, 0, MLA_QK)
    wk_p = head_pad(w_ukv, MLA_HEADS, MLA_NOPE + MLA_V, 0, MLA_NOPE)
    wv_p = head_pad(w_ukv, MLA_HEADS, MLA_NOPE + MLA_V, MLA_NOPE, MLA_NOPE + MLA_V)
    qn_p = jnp.pad(qn, (0, LANES - MLA_QK))[None]
    kn_p = jnp.pad(kn, (0, LANES - MLA_QK))[None]

    q = _mla_q(p, q_norm[None], w_uq_p, qn_p, tabs_m)
    k, v, ckvn = _mla_kv(p, CKV_BLK, p, KR_BLK, kv_norm[None], wk_p, wv_p, kn_p, tabs_m, N_TOK, True)

    n_c = DEC_BATCH * PAST_LEN
    kr_c = jnp.pad(cache_krope.reshape(n_c, MLA_ROPE), ((0, 0), (MLA_NOPE, LANES - MLA_QK)))
    k_c, v_c, _ = _mla_kv(cache_ckv.reshape(n_c, MLA_KV_RANK), 0, kr_c, 0, kv_norm[None], wk_p, wv_p, kn_p,
                          tabs_m, n_c, False)

    o_ctx = _mla_attn(q, k, v, None, BATCH, SEQ, 0, SEQ, MLA_HEADS // 2)
    o_lat = _mla_attn(q, k, v, (k_c, v_c), DEC_BATCH, DEC_SEQ, N_CTX, 256, 1)

    log_g = -_softplus(-ret_decay)
    gn = ret_gn[None]
    s0_ctx = jnp.zeros((BATCH, 2, RET_HEADS * RET_DK, RET_DV), F32)
    r_ctx, st_ctx = _retention(log_g, p, RQ_BLK, RK_BLK, RV_BLK, RG_BLK, s0_ctx, gn, BATCH, SEQ, 0, SEQ, True)
    s0_lat = state_ret.reshape(DEC_BATCH, 2, RET_HEADS * RET_DK, RET_DV)
    r_lat, _ = _retention(log_g, p, RQ_BLK, RK_BLK, RV_BLK, RG_BLK, s0_lat, gn, DEC_BATCH, DEC_SEQ, N_CTX, 256,
                          False)

    mix = [(o_ctx, o_lat), (r_ctx, r_lat)]
    new_ckv = ckvn[:N_CTX].reshape(BATCH, SEQ, MLA_KV_RANK)
    new_krope = p[:N_CTX, KR_BLK * LANES + MLA_NOPE:KR_BLK * LANES + MLA_QK]
    new_krope = new_krope.reshape(BATCH, SEQ, MLA_ROPE)
    new_ret = st_ctx.reshape(BATCH, 2, RET_HEADS, RET_DK, RET_DV)
    return mix, new_ckv, new_krope, new_ret


def _odd_layer(x, mod, g_mix, w_in, qn, kn, lam, diff_gn, mu, w0, w_up, a0, a_up, g_up, k_k, k_a, r_k, gn,
               cache_k, cache_v, state_rwkv, tabs_d, lam_init):
    w_p = w_in
    n_in = w_p.shape[1]
    mu_full = jnp.concatenate([jnp.zeros((3 * DIFF_W,), F32), mu])[None]
    p = _inproj(x, g_mix, mod, w_p, mu_full, 384, (3 * DIFF_W) // 384)
    DV_BLK, R_BLK, K_BLK, V_BLK = 2, 3, 4, 5
    LO_BLK = (6 * RWKV_W) // (3 * LANES)

    qn_p = jnp.tile(qn, 2)[None]
    kn_p = jnp.tile(kn, 2)[None]
    q, k, dk, dv = _diff_qk(p, qn_p, kn_p, tabs_d)

    n_c = DEC_BATCH * PAST_LEN
    cache = (cache_k.reshape(n_c, DIFF_W), cache_v.reshape(n_c, DIFF_W))
    dgn = diff_gn[None]
    o_ctx = _diff_attn(lam, q, k, p, DV_BLK * DIFF_HEADS, None, dgn, BATCH, SEQ, 0, SEQ, lam_init, DIFF_HEADS)
    o_lat = _diff_attn(lam, q, k, p, DV_BLK * DIFF_HEADS, cache, dgn, DEC_BATCH, DEC_SEQ, N_CTX, 256, lam_init, 1)

    zero = jnp.zeros((RWKV_W_LORA, RWKV_W), F32)
    wup_bd = jnp.concatenate([jnp.concatenate([w_up[0], zero], 1), jnp.concatenate([zero, w_up[1]], 1)], 0)
    aup_bd = jnp.concatenate([jnp.concatenate([a_up[0], zero], 1), jnp.concatenate([zero, a_up[1]], 1)], 0)
    op, g, bonus = _rwkv_pre(p, R_BLK, K_BLK, V_BLK, LO_BLK, wup_bd, aup_bd, g_up, w0.reshape(1, -1),
                             a0.reshape(1, -1), k_k[None], k_a[None], r_k.reshape(1, -1))
    op4 = op.reshape(N_TOK // SEQ, SEQ, _SCAN_NJ, LANES)
    y_ctx, st_ctx = _rwkv_scan_ctx(op4)
    s0_lat = jnp.transpose(state_rwkv, (4, 1, 0, 2, 3)).reshape(RWKV_HS, _LAT_STATES, RWKV_HS)
    y_lat_f, y_lat_b = _rwkv_scan_lat(op4, _value_split_layout(s0_lat))
    rw_o = _rwkv_post(y_ctx, y_lat_f, y_lat_b, bonus, p, V_BLK, g, gn[None])

    mix = [(o_ctx, o_lat), rw_o]
    new_dk = dk.reshape(BATCH, SEQ, DIFF_HEADS, 2, DIFF_DH)
    new_dv = dv.reshape(BATCH, SEQ, DIFF_HEADS, 2 * DIFF_DH)
    new_rwkv = jnp.transpose(st_ctx.reshape(2, RWKV_HS, RWKV_HS, BATCH, RWKV_HEADS), (3, 0, 4, 2, 1))
    return mix, new_dk, new_dv, new_rwkv


def kernel(x_prompt, x_sample, cache_mla_ckv, cache_mla_krope, state_ret, cache_diff_k, cache_diff_v, state_rwkv,
           c, c_ctx, ada_w, ada_b, norm_mix_g, norm_ffn_g, w_out, ffn_up, ffn_conv_w, ffn_conv_b, ffn_down,
           a_w_in, mla_q_norm, mla_kv_norm, mla_w_uq, mla_w_ukv, mla_qn, mla_kn, ret_decay, ret_gn,
           b_w_in, diff_qn, diff_kn, diff_lam, diff_gn, rwkv_mu, rwkv_w0, rwkv_w_up, rwkv_a0, rwkv_a_up,
           rwkv_g_up, rwkv_k_k, rwkv_k_a, rwkv_r_k, rwkv_gn):
    x = (x_prompt.reshape(N_CTX, D_MODEL), x_sample.reshape(N_LAT, D_MODEL))
    cond8 = jnp.pad(jnp.concatenate([c_ctx[None], c], 0), ((0, 8 - N_GROUPS), (0, 0)))
    mod = _modulation(cond8, ada_w, ada_b)

    tabs_m = _rope_tables(MLA_ROPE, (MLA_NOPE,))
    tabs_d = _rope_tables(DIFF_DH, (0, DIFF_DH))

    outs = {}
    for l in range(DEPTH):
        j = l // 2
        g_mix = norm_mix_g[l][None]
        if l % 2 == 0:
            mix, outs["ckv"], outs["krope"], outs["ret"] = _even_layer(
                x, mod[l], g_mix, a_w_in[j], mla_q_norm[j], mla_kv_norm[j], mla_w_uq[j], mla_w_ukv[j], mla_qn[j],
                mla_kn[j], ret_decay[j], ret_gn[j], cache_mla_ckv[:, j], cache_mla_krope[:, j], state_ret[:, j],
                tabs_m)
        else:
            lam_init = 0.8 - 0.6 * math.exp(-0.3 * l)
            mix, outs["dk"], outs["dv"], outs["rwkv"] = _odd_layer(
                x, mod[l], g_mix, b_w_in[j], diff_qn[j], diff_kn[j], diff_lam[j], diff_gn[j], rwkv_mu[j],
                rwkv_w0[j], rwkv_w_up[j], rwkv_a0[j], rwkv_a_up[j], rwkv_g_up[j], rwkv_k_k[j], rwkv_k_a[j],
                rwkv_r_k[j], rwkv_gn[j], cache_diff_k[:, j], cache_diff_v[:, j], state_rwkv[:, j], tabs_d, lam_init)
        x = _resid_proj(mix, w_out, l, x, mod[l], 2)
        act = _ffn_up(x, norm_ffn_g[l][None], mod[l], ffn_up, l, ffn_conv_w[l], ffn_conv_b[l])
        x = _resid_proj([act], ffn_down, l, x, mod[l], 5, split_out=(l == DEPTH - 1))

    y_prompt = x[0].reshape(BATCH, SEQ, D_MODEL)
    y_sample = x[1].reshape(DEC_BATCH, DEC_SEQ, D_MODEL)
    return (y_prompt, y_sample, outs["ckv"][:, None], outs["krope"][:, None], outs["ret"][:, None],
            outs["dk"][:, None], outs["dv"][:, None], outs["rwkv"][:, None])
```

```python
import functools
import math

import numpy as np
import jax
import jax.numpy as jnp
from jax import lax
from jax.experimental import pallas as pl
from jax.experimental.pallas import tpu as pltpu

D_MODEL = 1024
BATCH = 16
SEQ = 256
DEPTH = 2
DEC_BATCH = 2
DEC_SEQ = 1024
PAST_LEN = 512
GRID_W = 64
EPS = 1e-6
ROPE_BASE = 10000.0

MLA_HEADS = 8
MLA_Q_RANK = 256
MLA_KV_RANK = 128
MLA_NOPE = 64
MLA_ROPE = 32
MLA_V = 64
MLA_QK = MLA_NOPE + MLA_ROPE
RET_HEADS = 4
RET_DK = 64
RET_DV = 128
DIFF_HEADS = 4
DIFF_DH = 64
DIFF_W = DIFF_HEADS * 2 * DIFF_DH
RWKV_HEADS = 8
RWKV_HS = 64
RWKV_W = RWKV_HEADS * RWKV_HS
RWKV_W_LORA = 64
D_FF = 2816

N_CTX = BATCH * SEQ
N_LAT = DEC_BATCH * DEC_SEQ
N_TOK = N_CTX + N_LAT
N_GROUPS = 1 + DEC_BATCH

LANES = 128
VMEM_LIMIT = 56 * 1024 * 1024

_PREC = lax.Precision.HIGHEST
F32 = jnp.float32


def _dot_tn(a, b):
    return lax.dot_general(a, b, (((0,), (0,)), ((), ())), precision=_PREC, preferred_element_type=F32)


BF16 = jnp.bfloat16


def _dot_bf16(a, b):
    return jnp.dot(a.astype(BF16), b.astype(BF16), preferred_element_type=F32)


def _dot_nt_bf16(a, b):
    return lax.dot_general(a.astype(BF16), b.astype(BF16), (((1,), (1,)), ((), ())), preferred_element_type=F32)


def _params(*sem):
    return pltpu.CompilerParams(dimension_semantics=sem, vmem_limit_bytes=VMEM_LIMIT)


def _sigmoid(x):
    return 1.0 / (1.0 + jnp.exp(-x))


def _silu(x):
    return x * _sigmoid(x)


def _softplus(x):
    return jnp.maximum(x, 0.0) + jnp.log(1.0 + jnp.exp(-jnp.abs(x)))


def _rms(x, n):
    return x * lax.rsqrt(jnp.sum(x * x, axis=-1, keepdims=True) * (1.0 / n) + EPS)


def _lane_lo(shape):
    return lax.broadcasted_iota(jnp.int32, shape, len(shape) - 1) < 64


def _seg64_sum(x):
    lo = _lane_lo(x.shape)
    s_lo = jnp.sum(jnp.where(lo, x, 0.0), axis=-1, keepdims=True)
    s_hi = jnp.sum(jnp.where(lo, 0.0, x), axis=-1, keepdims=True)
    return jnp.where(lo, s_lo, s_hi)


_SUBLANES = 8


def _seq_neighbours(p, tile, tile_rows):
    is_ctx = tile * tile_rows < N_CTX
    sub = lax.broadcasted_iota(jnp.int32, (_SUBLANES, 1), 0)

    def shifted(rolled, edge_sublane, group_of_seq):
        pieces, start = [], 0
        for q in range(tile_rows // SEQ):
            g0 = q * SEQ + group_of_seq
            outer = (q == 0) if group_of_seq == 0 else (q == tile_rows // SEQ - 1)
            edge = (sub == edge_sublane) if outer else ((sub == edge_sublane) & is_ctx)
            pieces += [rolled[start:g0], jnp.where(edge, 0.0, rolled[g0:g0 + _SUBLANES])]
            start = g0 + _SUBLANES
        pieces.append(rolled[start:])
        return jnp.concatenate([x for x in pieces if x.shape[0]], axis=0)

    prev = shifted(pltpu.roll(p, 1, axis=0), 0, 0)
    nxt = shifted(pltpu.roll(p, tile_rows - 1, axis=0), _SUBLANES - 1, SEQ - _SUBLANES)
    return prev, nxt


def _group_of_tile(i, tile_rows):
    row = i * tile_rows
    return jnp.where(row < N_CTX, 0, 1 + (row - N_CTX) // DEC_SEQ)


def _modulation_kernel(c_ref, w_ref, b_ref, o_ref):
    o_ref[0] = _dot_bf16(_silu(c_ref[...]), w_ref[0]) + b_ref[0]


def _modulation(cond8, ada_w, ada_b):
    tn = 512
    n = 6 * D_MODEL
    out = pl.pallas_call(
        _modulation_kernel,
        grid=(DEPTH, n // tn),
        in_specs=[pl.BlockSpec((8, D_MODEL), lambda l, j: (0, 0)),
                  pl.BlockSpec((1, D_MODEL, tn), lambda l, j: (l, 0, j)),
                  pl.BlockSpec((1, 1, tn), lambda l, j: (l, 0, j))],
        out_specs=pl.BlockSpec((1, 8, tn), lambda l, j: (l, 0, j)),
        out_shape=jax.ShapeDtypeStruct((DEPTH, 8, n), F32),
        compiler_params=_params("parallel", "parallel"),
        name="modulation",
    )(cond8, ada_w, ada_b.reshape(DEPTH, 1, n))
    m = out[:, :N_GROUPS].reshape(DEPTH, N_GROUPS, 6, D_MODEL)
    return jnp.pad(m, ((0, 0), (0, 0), (0, 2), (0, 0)))


_TM_SEQ = 1024


def _norm_mod(x, g, mod, off):
    return _rms(x, D_MODEL) * g * (1.0 + mod[off + 1:off + 2, :]) + mod[off:off + 1, :]


def _row_split_specs(tm, width_block, col):
    n_ctx = N_CTX // tm
    return [pl.BlockSpec((tm, width_block), lambda i, j: (jnp.minimum(i, n_ctx - 1), col(j))),
            pl.BlockSpec((tm, width_block), lambda i, j: (jnp.maximum(i - n_ctx, 0), col(j)))]


def _for_row_tile(i, tm, ctx_ref, lat_ref, fn):
    @pl.when(i < N_CTX // tm)
    def _():
        fn(ctx_ref)

    @pl.when(i >= N_CTX // tm)
    def _():
        fn(lat_ref)


def _inproj_kernel(*refs, shift_from, split_x):
    x_refs, (g_ref, mod_ref, w_ref, mu_ref, o_ref, h_ref) = refs[:-6], refs[-6:]
    i = pl.program_id(0)

    @pl.when(pl.program_id(1) == 0)
    def _():
        def norm(x_ref):
            h_ref[...] = _norm_mod(x_ref[...], g_ref[...], mod_ref[0], 0).astype(BF16)

        if split_x:
            _for_row_tile(i, _TM_SEQ, x_refs[0], x_refs[1], norm)
        else:
            norm(x_refs[0])

    p = _dot_bf16(h_ref[...], w_ref[...])
    if shift_from is None:
        o_ref[...] = p
    else:
        @pl.when(pl.program_id(1) < shift_from)
        def _():
            o_ref[...] = p

        @pl.when(pl.program_id(1) >= shift_from)
        def _():
            prev, nxt = _seq_neighbours(p, i, _TM_SEQ)
            o_ref[...] = p + (0.5 * (prev + nxt) - p) * mu_ref[...]


def _inproj(x, g, mod, w, mu, tn, shift_from):
    n = w.shape[1]
    tm = _TM_SEQ
    split_x = isinstance(x, (tuple, list))
    x_specs = _row_split_specs(tm, D_MODEL, lambda j: 0) if split_x else [pl.BlockSpec((tm, D_MODEL), lambda i, j: (i, 0))]
    return pl.pallas_call(
        functools.partial(_inproj_kernel, shift_from=shift_from, split_x=split_x),
        grid=(N_TOK // tm, n // tn),
        in_specs=x_specs
        + [pl.BlockSpec((1, D_MODEL), lambda i, j: (0, 0)),
                  pl.BlockSpec((1, 8, D_MODEL), lambda i, j: (_group_of_tile(i, tm), 0, 0)),
                  pl.BlockSpec((D_MODEL, tn), lambda i, j: (0, j)),
                  pl.BlockSpec((1, tn), lambda i, j: (0, j))],
        out_specs=pl.BlockSpec((tm, tn), lambda i, j: (i, j)),
        out_shape=jax.ShapeDtypeStruct((N_TOK, n), F32),
        scratch_shapes=[pltpu.VMEM((tm, D_MODEL), BF16)],
        compiler_params=_params("parallel", "arbitrary"),
        name="inproj" if shift_from is None else "inproj_shift",
    )(*(x if split_x else [x]), g, mod, w, mu)


def _resid_kernel(*refs, gate_row, widths, split, split_x, split_out):
    n_in = sum(2 if sp else 1 for sp in split)
    a_refs = refs[:n_in]
    w_ref = refs[n_in]
    n_x = 2 if split_x else 1
    x_refs = refs[n_in + 1:n_in + 1 + n_x]
    mod_ref = refs[n_in + 1 + n_x]
    o_ref = refs[n_in + 2 + n_x:-1] if split_out else refs[n_in + 2 + n_x]
    a_bf_ref = refs[-1]
    i = pl.program_id(0)

    @pl.when(pl.program_id(1) == 0)
    def _():
        k0, r = 0, 0
        for width, sp in zip(widths, split):
            cols = slice(k0, k0 + width)
            if sp:
                ctx_ref, lat_ref = a_refs[r], a_refs[r + 1]

                @pl.when(i < N_CTX // _TM_SEQ)
                def _():
                    a_bf_ref[:, cols] = ctx_ref[...].astype(BF16)

                @pl.when(i >= N_CTX // _TM_SEQ)
                def _():
                    a_bf_ref[:, cols] = lat_ref[...].astype(BF16)
            else:
                a_bf_ref[:, cols] = a_refs[r][...].astype(BF16)
            k0 += width
            r += 2 if sp else 1

    d = mod_ref[0, gate_row:gate_row + 1, :] * _dot_bf16(a_bf_ref[...], w_ref[0])
    if split_x:
        def add(x_ref):
            o_ref[...] = x_ref[...] + d

        _for_row_tile(i, _TM_SEQ, x_refs[0], x_refs[1], add)
        return
    y = x_refs[0][...] + d
    if not split_out:
        o_ref[...] = y
    else:
        ctx_o_ref, lat_o_ref = o_ref

        @pl.when(i < N_CTX // _TM_SEQ)
        def _():
            ctx_o_ref[...] = y

        @pl.when(i >= N_CTX // _TM_SEQ)
        def _():
            lat_o_ref[...] = y


def _resid_proj(acts, w, layer, x, mod, gate_row, split_out=False):
    tm = _TM_SEQ
    split_x = isinstance(x, (tuple, list))
    assert not (split_x and split_out)
    n_ctx = N_CTX // tm
    split = [isinstance(a, (tuple, list)) for a in acts]
    widths = [a[0].shape[1] if sp else a.shape[1] for a, sp in zip(acts, split)]
    k = sum(widths)
    tn = 512 if k <= D_MODEL else 256
    nj = D_MODEL // tn
    in_specs, operands = [], []
    for a, width, sp in zip(acts, widths, split):
        if sp:
            in_specs += [pl.BlockSpec((tm, width), lambda i, j: (jnp.minimum(i, n_ctx - 1), 0)),
                         pl.BlockSpec((tm, width), lambda i, j: (jnp.maximum(i - n_ctx, 0), 0))]
            operands += list(a)
        else:
            in_specs.append(pl.BlockSpec((tm, width), lambda i, j: (i, 0)))
            operands.append(a)
    if split_out:
        out_specs = [pl.BlockSpec((tm, tn), lambda i, j: (jnp.minimum(i, n_ctx - 1), jnp.where(i < n_ctx, j, nj - 1))),
                     pl.BlockSpec((tm, tn), lambda i, j: (jnp.maximum(i - n_ctx, 0), jnp.where(i < n_ctx, 0, j)))]
        out_shape = [jax.ShapeDtypeStruct((N_CTX, D_MODEL), F32), jax.ShapeDtypeStruct((N_LAT, D_MODEL), F32)]
        sem = ("arbitrary", "arbitrary")
    else:
        out_specs = pl.BlockSpec((tm, tn), lambda i, j: (i, j))
        out_shape = jax.ShapeDtypeStruct((N_TOK, D_MODEL), F32)
        sem = ("parallel", "arbitrary")
    return pl.pallas_call(
        functools.partial(_resid_kernel, gate_row=gate_row, widths=tuple(widths), split=tuple(split),
                          split_x=split_x, split_out=split_out),
        grid=(N_TOK // tm, nj),
        in_specs=in_specs
        + [pl.BlockSpec((1, k, tn), lambda i, j: (layer, 0, j))]
        + (_row_split_specs(tm, tn, lambda j: j) if split_x else [pl.BlockSpec((tm, tn), lambda i, j: (i, j))])
        + [pl.BlockSpec((1, 8, tn), lambda i, j: (_group_of_tile(i, tm), 0, j))],
        out_specs=out_specs,
        out_shape=out_shape,
        scratch_shapes=[pltpu.VMEM((tm, k), BF16)],
        compiler_params=_params(*sem),
        name="resid_proj",
    )(*operands, w, *(x if split_x else [x]), mod)


def _ffn_up_kernel(x_ref, g_ref, mod_ref, wa_ref, wb_ref, cwa_ref, cwb_ref, cba_ref, cbb_ref, o_ref, h_ref):
    i = pl.program_id(0)

    @pl.when(pl.program_id(1) == 0)
    def _():
        h_ref[...] = _norm_mod(x_ref[...], g_ref[...], mod_ref[0], 3).astype(BF16)

    h = h_ref[...]

    def conv(w_ref, cw_ref, cb_ref):
        u = _dot_bf16(h, w_ref[0])
        prev, nxt = _seq_neighbours(u, i, _TM_SEQ)
        return prev * cw_ref[0:1, :] + u * cw_ref[1:2, :] + nxt * cw_ref[2:3, :] + cb_ref[...]

    o_ref[...] = (_silu(conv(wa_ref, cwa_ref, cba_ref)) * conv(wb_ref, cwb_ref, cbb_ref)).astype(o_ref.dtype)


def _ffn_up(x, g, mod, up, layer, cw, cb):
    tm, tn = _TM_SEQ, 256
    nb = D_FF // tn
    cb = cb.reshape(1, 2 * D_FF)
    return pl.pallas_call(
        _ffn_up_kernel,
        grid=(N_TOK // tm, nb),
        in_specs=[pl.BlockSpec((tm, D_MODEL), lambda i, j: (i, 0)),
                  pl.BlockSpec((1, D_MODEL), lambda i, j: (0, 0)),
                  pl.BlockSpec((1, 8, D_MODEL), lambda i, j: (_group_of_tile(i, tm), 0, 0)),
                  pl.BlockSpec((1, D_MODEL, tn), lambda i, j: (layer, 0, j)),
                  pl.BlockSpec((1, D_MODEL, tn), lambda i, j: (layer, 0, j + nb)),
                  pl.BlockSpec((3, tn), lambda i, j: (0, j)),
                  pl.BlockSpec((3, tn), lambda i, j: (0, j + nb)),
                  pl.BlockSpec((1, tn), lambda i, j: (0, j)),
                  pl.BlockSpec((1, tn), lambda i, j: (0, j + nb))],
        out_specs=pl.BlockSpec((tm, tn), lambda i, j: (i, j)),
        out_shape=jax.ShapeDtypeStruct((N_TOK, D_FF), BF16),
        scratch_shapes=[pltpu.VMEM((tm, D_MODEL), BF16)],
        compiler_params=_params("parallel", "arbitrary"),
        name="ffn_up",
    )(x, g, mod, up, up, cw, cw, cb, cb)


_ROPE_TM = 512


def _rope(y, c, s1, s2):
    return y * c + pltpu.roll(y, 1, axis=1) * s1 + pltpu.roll(y, LANES - 1, axis=1) * s2


def _rope_tables(rot_dim, lane_offsets):
    t = np.arange(DEC_SEQ)
    row, col = t // GRID_W, t % GRID_W
    n_freq = rot_dim // 4
    inv = ROPE_BASE ** (-np.arange(n_freq, dtype=np.float64) / n_freq)
    ang = np.concatenate([row[:, None] * inv, col[:, None] * inv], -1)
    cos, sin = np.cos(ang), np.sin(ang)
    n = _ROPE_TM + DEC_SEQ
    c, s1, s2 = np.ones((n, LANES)), np.zeros((n, LANES)), np.zeros((n, LANES))
    for a in lane_offsets:
        even = a + 2 * np.arange(rot_dim // 2)
        c[_ROPE_TM:, even] = cos
        c[_ROPE_TM:, even + 1] = cos
        s1[_ROPE_TM:, even + 1] = sin
        s2[_ROPE_TM:, even] = -sin
    return tuple(jnp.asarray(x, F32) for x in (c, s1, s2))


def _store_rotated(dst_ref, sl, y, tabs, rotate):
    if rotate is False:
        dst_ref[:, sl] = y
        return

    @pl.when(rotate)
    def _():
        dst_ref[:, sl] = _rope(y, *tabs)

    @pl.when(jnp.logical_not(rotate))
    def _():
        dst_ref[:, sl] = y


def _latent_tile():
    return pl.program_id(0) >= N_CTX // _ROPE_TM


def _rope_block(i):
    row = i * _ROPE_TM
    return jnp.where(row < N_CTX, 0, 1 + ((row - N_CTX) % DEC_SEQ) // _ROPE_TM)


def _mla_q_kernel(cq_ref, qnorm_ref, w_ref, qn_ref, c_ref, s1_ref, s2_ref, o_ref):
    xn = _rms(cq_ref[...], MLA_Q_RANK) * qnorm_ref[...]
    y = _dot_bf16(xn, w_ref[...])
    tabs = (c_ref[...], s1_ref[...], s2_ref[...])
    for h in range(MLA_HEADS):
        sl = slice(h * LANES, (h + 1) * LANES)
        _store_rotated(o_ref, sl, _rms(y[:, sl], MLA_QK) * qn_ref[...], tabs, _latent_tile())


def _mla_q(p, q_norm, w_uq_p, qn_p, tabs):
    tm = _ROPE_TM
    hw = MLA_HEADS * LANES
    tab_spec = pl.BlockSpec((tm, LANES), lambda i: (_rope_block(i), 0))
    return pl.pallas_call(
        _mla_q_kernel,
        grid=(N_TOK // tm,),
        in_specs=[pl.BlockSpec((tm, MLA_Q_RANK), lambda i: (i, 0)),
                  pl.BlockSpec((1, MLA_Q_RANK), lambda i: (0, 0)),
                  pl.BlockSpec((MLA_Q_RANK, hw), lambda i: (0, 0)),
                  pl.BlockSpec((1, LANES), lambda i: (0, 0)),
                  tab_spec, tab_spec, tab_spec],
        out_specs=pl.BlockSpec((tm, hw), lambda i: (i, 0)),
        out_shape=jax.ShapeDtypeStruct((N_TOK, hw), F32),
        compiler_params=_params("parallel"),
        name="mla_q",
    )(p, q_norm, w_uq_p, qn_p, *tabs)


def _mla_kv_kernel(ckv_ref, kr_ref, kvn_ref, wk_ref, wv_ref, kn_ref, c_ref, s1_ref, s2_ref,
                   k_ref, v_ref, ckvn_ref, *, norm_ckv):
    ckv = ckv_ref[...]
    if norm_ckv:
        ckv = _rms(ckv, MLA_KV_RANK) * kvn_ref[...]
    ckvn_ref[...] = ckv
    ckv_bf = ckv.astype(BF16)
    kk = _dot_bf16(ckv_bf, wk_ref[...])
    v_ref[...] = _dot_bf16(ckv_bf, wv_ref[...])
    kr = kr_ref[...]
    tabs = (c_ref[...], s1_ref[...], s2_ref[...])
    for h in range(MLA_HEADS):
        sl = slice(h * LANES, (h + 1) * LANES)
        _store_rotated(k_ref, sl, _rms(kk[:, sl] + kr, MLA_QK) * kn_ref[...], tabs,
                       _latent_tile() if norm_ckv else False)


def _mla_kv(ckv_src, ckv_blk, kr_src, kr_blk, kv_norm, wk_p, wv_p, kn_p, tabs, n_rows, own_tokens):
    tm = _ROPE_TM
    hw = MLA_HEADS * LANES
    tab_spec = pl.BlockSpec((tm, LANES), (lambda i: (_rope_block(i), 0)) if own_tokens else (lambda i: (0, 0)))
    return pl.pallas_call(
        functools.partial(_mla_kv_kernel, norm_ckv=own_tokens),
        grid=(n_rows // tm,),
        in_specs=[pl.BlockSpec((tm, LANES), lambda i: (i, ckv_blk)),
                  pl.BlockSpec((tm, LANES), lambda i: (i, kr_blk)),
                  pl.BlockSpec((1, LANES), lambda i: (0, 0)),
                  pl.BlockSpec((MLA_KV_RANK, hw), lambda i: (0, 0)),
                  pl.BlockSpec((MLA_KV_RANK, hw), lambda i: (0, 0)),
                  pl.BlockSpec((1, LANES), lambda i: (0, 0)),
                  tab_spec, tab_spec, tab_spec],
        out_specs=[pl.BlockSpec((tm, hw), lambda i: (i, 0)),
                   pl.BlockSpec((tm, hw), lambda i: (i, 0)),
                   pl.BlockSpec((tm, LANES), lambda i: (i, 0))],
        out_shape=[jax.ShapeDtypeStruct((n_rows, hw), F32),
                   jax.ShapeDtypeStruct((n_rows, hw), F32),
                   jax.ShapeDtypeStruct((n_rows, LANES), F32)],
        compiler_params=_params("parallel"),
        name="mla_kv",
    )(ckv_src, kr_src, kv_norm, wk_p, wv_p, kn_p, *tabs)


_LOG2E = math.log2(math.e)


def _softmax_parts(scores, scale):
    m = functools.reduce(jnp.maximum, [jnp.max(sc, axis=-1, keepdims=True) for sc in scores])
    ps = [jnp.exp2((sc - m) * (scale * _LOG2E)) for sc in scores]
    return ps, sum(jnp.sum(p, axis=-1, keepdims=True) for p in ps)


def _attend(q, sources, sl, scale):
    ps, l = _softmax_parts([_dot_nt_bf16(q, k_ref[:, sl]) for k_ref, _ in sources], scale)
    return sum(_dot_bf16(p, v_ref[:, sl]) for p, (_, v_ref) in zip(ps, sources)), l


def _mla_attn_kernel(*refs, pairs):
    q_ref, o_ref = refs[0], refs[-1]
    sources = [refs[1:3]] + ([refs[3:5]] if len(refs) == 6 else [])
    scale = MLA_QK ** -0.5
    for pr in range(pairs):
        outs = []
        for h in range(2):
            sl = slice((2 * pr + h) * LANES, (2 * pr + h + 1) * LANES)
            o, l = _attend(q_ref[:, sl], sources, sl, scale)
            outs.append(o / l)
        o_ref[:, pr * LANES:(pr + 1) * LANES] = (outs[0] + pltpu.roll(outs[1], MLA_V, axis=1)).astype(o_ref.dtype)


def _mla_attn(q, k, v, cache, batch, n, row0, tq, pairs):
    nqb = n // tq
    qb0 = row0 // tq
    kb0 = row0 // n
    wide = 2 * LANES * pairs
    kv_spec = pl.BlockSpec((n, wide), lambda b, h, i: (kb0 + b, h))
    in_specs, operands = [pl.BlockSpec((tq, wide), lambda b, h, i: (qb0 + b * nqb + i, h)), kv_spec, kv_spec], [q, k, v]
    if cache is not None:
        in_specs += [pl.BlockSpec((PAST_LEN, wide), lambda b, h, i: (b, h))] * 2
        operands += list(cache)
    return pl.pallas_call(
        functools.partial(_mla_attn_kernel, pairs=pairs),
        grid=(batch, MLA_HEADS // (2 * pairs), nqb),
        in_specs=in_specs,
        out_specs=pl.BlockSpec((tq, LANES * pairs), lambda b, h, i: (b * nqb + i, h)),
        out_shape=jax.ShapeDtypeStruct((batch * n, MLA_HEADS * MLA_V), BF16),
        compiler_params=_params("parallel", "parallel", "arbitrary"),
        name="mla_attn",
    )(*operands)


def _ret_kernel(lg_ref, q_ref, k_ref, v_ref, rg_ref, s0_ref, gn_ref, *out_and_scratch, n, tq, want_state):
    if want_state:
        o_ref, st_ref, decay_ref = out_and_scratch
    else:
        o_ref, decay_ref = out_and_scratch
    pair, qi, b = pl.program_id(0), pl.program_id(1), pl.program_id(2)
    q = q_ref[...]
    k = k_ref[...] * (RET_DK ** -0.5)
    lo = _lane_lo((1, LANES))
    row = (qi * tq + lax.broadcasted_iota(jnp.int32, (tq, 1), 0)).astype(F32)

    @pl.when(b == 0)
    def _():
        col = lax.broadcasted_iota(jnp.int32, (1, n), 1).astype(F32)
        diff = row - col
        for h in range(2):
            lgf = lg_ref[0, 2 * pair + h]
            lgb = lg_ref[1, 2 * pair + h]
            decay_ref[h] = (jnp.where(diff >= 0, jnp.exp(lgf * jnp.maximum(diff, 0.0)), 0.0)
                            + jnp.where(diff <= 0, jnp.exp(lgb * jnp.maximum(-diff, 0.0)), 0.0))

    for h in range(2):
        lgf = lg_ref[0, 2 * pair + h]
        lgb = lg_ref[1, 2 * pair + h]
        mask = lo if h == 0 else jnp.logical_not(lo)
        qh = jnp.where(mask, q, 0.0)
        vh = v_ref[:, h * LANES:(h + 1) * LANES]
        o = _dot_bf16(_dot_nt_bf16(qh, k) * decay_ref[h], vh)
        o = o + _dot_bf16(qh * jnp.exp(lgf * (row + 1.0)), s0_ref[0, 0])
        o = o + _dot_bf16(qh * jnp.exp(lgb * (n - row)), s0_ref[0, 1])
        y = _rms(o, RET_DV) * gn_ref[:, h * LANES:(h + 1) * LANES]
        o_ref[:, h * LANES:(h + 1) * LANES] = (_silu(rg_ref[:, h * LANES:(h + 1) * LANES]) * y).astype(o_ref.dtype)

    if want_state:
        pos = lax.broadcasted_iota(jnp.int32, (n, 1), 0).astype(F32)
        for d in range(2):
            acc = None
            for h in range(2):
                lg = lg_ref[d, 2 * pair + h]
                mask = lo if h == 0 else jnp.logical_not(lo)
                expo = (n - 1.0 - pos) if d == 0 else pos
                kd = jnp.where(mask, k * jnp.exp(lg * expo), 0.0)
                term = _dot_tn(kd, v_ref[:, h * LANES:(h + 1) * LANES])
                acc = term if acc is None else acc + term
            lg_rows = jnp.where(lax.broadcasted_iota(jnp.int32, (LANES, 1), 0) < 64,
                                lg_ref[d, 2 * pair], lg_ref[d, 2 * pair + 1])
            st_ref[0, d] = acc + s0_ref[0, d] * jnp.exp(lg_rows * n)


def _retention(log_g, p, q_blk, k_blk, v_blk, g_blk, s0, gn, batch, n, row0, tq, want_state):
    nqb = n // tq
    assert not want_state or nqb == 1
    qb0 = row0 // tq
    kb0 = row0 // n
    pairs = RET_HEADS // 2
    out_specs = [pl.BlockSpec((tq, 2 * LANES), lambda h, i, b: (b * nqb + i, h))]
    out_shape = [jax.ShapeDtypeStruct((batch * n, RET_HEADS * RET_DV), BF16)]
    if want_state:
        out_specs.append(pl.BlockSpec((1, 2, LANES, LANES), lambda h, i, b: (b, 0, h, 0)))
        out_shape.append(jax.ShapeDtypeStruct((batch, 2, RET_HEADS * RET_DK, RET_DV), F32))
    outs = pl.pallas_call(
        functools.partial(_ret_kernel, n=n, tq=tq, want_state=want_state),
        grid=(pairs, nqb, batch),
        in_specs=[pl.BlockSpec(memory_space=pltpu.SMEM),
                  pl.BlockSpec((tq, LANES), lambda h, i, b: (qb0 + b * nqb + i, q_blk + h)),
                  pl.BlockSpec((n, LANES), lambda h, i, b: (kb0 + b, k_blk + h)),
                  pl.BlockSpec((n, 2 * LANES), lambda h, i, b: (kb0 + b, v_blk // 2 + h)),
                  pl.BlockSpec((tq, 2 * LANES), lambda h, i, b: (qb0 + b * nqb + i, g_blk // 2 + h)),
                  pl.BlockSpec((1, 2, LANES, LANES), lambda h, i, b: (b, 0, h, 0)),
                  pl.BlockSpec((1, 2 * LANES), lambda h, i, b: (0, h))],
        out_specs=out_specs,
        out_shape=out_shape,
        scratch_shapes=[pltpu.VMEM((2, tq, n), F32)],
        compiler_params=_params("parallel", "parallel", "arbitrary"),
        name="retention",
    )(log_g, p, p, p, p, s0, gn)
    return outs if want_state else (outs[0], None)


def _diff_qk_kernel(q_ref, k_ref, v_ref, qn_ref, kn_ref, c_ref, s1_ref, s2_ref, qo_ref, ko_ref, dk_ref, dv_ref):
    tabs = (c_ref[...], s1_ref[...], s2_ref[...])
    is_ctx = jnp.logical_not(_latent_tile())
    for src, gain, dst in ((q_ref, qn_ref, qo_ref), (k_ref, kn_ref, ko_ref)):
        for h in range(DIFF_HEADS):
            sl = slice(h * LANES, (h + 1) * LANES)
            y = src[:, sl]
            y = y * lax.rsqrt(_seg64_sum(y * y) * (1.0 / DIFF_DH) + EPS) * gain[...]
            _store_rotated(dst, sl, y, tabs, _latent_tile())
            if dst is ko_ref:
                @pl.when(is_ctx)
                def _():
                    dk_ref[:, h, 0, :] = y[:, :DIFF_DH]
                    dk_ref[:, h, 1, :] = pltpu.roll(y, DIFF_DH, axis=1)[:, :DIFF_DH]
                    dv_ref[:, h, :] = v_ref[:, sl]


def _diff_qk(p, qn_p, kn_p, tabs):
    tm = _ROPE_TM
    n_ctx = N_CTX // tm
    tab_spec = pl.BlockSpec((tm, LANES), lambda i: (_rope_block(i), 0))
    ctx_blk = lambda i: jnp.minimum(i, n_ctx - 1)
    return pl.pallas_call(
        _diff_qk_kernel,
        grid=(N_TOK // tm,),
        in_specs=[pl.BlockSpec((tm, DIFF_W), lambda i: (i, 0)),
                  pl.BlockSpec((tm, DIFF_W), lambda i: (i, 1)),
                  pl.BlockSpec((tm, DIFF_W), lambda i: (i, 2)),
                  pl.BlockSpec((1, LANES), lambda i: (0, 0)),
                  pl.BlockSpec((1, LANES), lambda i: (0, 0)),
                  tab_spec, tab_spec, tab_spec],
        out_specs=[pl.BlockSpec((tm, DIFF_W), lambda i: (i, 0)),
                   pl.BlockSpec((tm, DIFF_W), lambda i: (i, 0)),
                   pl.BlockSpec((tm, DIFF_HEADS, 2, DIFF_DH), lambda i: (ctx_blk(i), 0, 0, 0)),
                   pl.BlockSpec((tm, DIFF_HEADS, 2 * DIFF_DH), lambda i: (ctx_blk(i), 0, 0))],
        out_shape=[jax.ShapeDtypeStruct((N_TOK, DIFF_W), F32)] * 2
        + [jax.ShapeDtypeStruct((N_CTX, DIFF_HEADS, 2, DIFF_DH), F32),
           jax.ShapeDtypeStruct((N_CTX, DIFF_HEADS, 2 * DIFF_DH), F32)],
        compiler_params=_params("arbitrary"),
        name="diff_qk",
    )(p, p, p, qn_p, kn_p, *tabs)


def _diff_attn_kernel(*refs, lam_init, heads):
    lam_ref, q_ref, gn_ref, o_ref = refs[0], refs[1], refs[-2], refs[-1]
    sources = [refs[2:4]] + ([refs[4:6]] if len(refs) == 8 else [])
    lv = lam_ref[...]
    lam = (jnp.exp(jnp.sum(lv[0:1] * lv[1:2], axis=-1, keepdims=True))
           - jnp.exp(jnp.sum(lv[2:3] * lv[3:4], axis=-1, keepdims=True)) + lam_init)
    scale = DIFF_DH ** -0.5
    lo = _lane_lo((1, LANES))
    for h in range(heads):
        sl = slice(h * LANES, (h + 1) * LANES)
        q = q_ref[:, sl]
        kbs = [k_ref[:, sl].astype(BF16) for k_ref, _ in sources]
        ps1, l1 = _softmax_parts([_dot_nt_bf16(jnp.where(lo, q, 0.0), kb) for kb in kbs], scale)
        ps2, l2 = _softmax_parts([_dot_nt_bf16(jnp.where(lo, 0.0, q), kb) for kb in kbs], scale)
        o = sum(_dot_bf16(p1 / l1 - lam * (p2 / l2), v_ref[:, sl])
                for p1, p2, (_, v_ref) in zip(ps1, ps2, sources))
        o_ref[:, sl] = (_rms(o, 2 * DIFF_DH) * gn_ref[:, sl] * (1.0 - lam_init)).astype(o_ref.dtype)


def _diff_attn(lam, q, k, v, v_blk0, cache, gn, batch, n, row0, tq, lam_init, heads):
    nqb = n // tq
    qb0 = row0 // tq
    kb0 = row0 // n
    wide = LANES * heads
    vb0 = v_blk0 // heads
    in_specs = [pl.BlockSpec((4, DIFF_DH), lambda b, h, i: (0, 0)),
                pl.BlockSpec((tq, wide), lambda b, h, i: (qb0 + b * nqb + i, h)),
                pl.BlockSpec((n, wide), lambda b, h, i: (kb0 + b, h)),
                pl.BlockSpec((n, wide), lambda b, h, i: (kb0 + b, vb0 + h))]
    operands = [lam, q, k, v]
    if cache is not None:
        in_specs += [pl.BlockSpec((PAST_LEN, wide), lambda b, h, i: (b, h))] * 2
        operands += list(cache)
    return pl.pallas_call(
        functools.partial(_diff_attn_kernel, lam_init=lam_init, heads=heads),
        grid=(batch, DIFF_HEADS // heads, nqb),
        in_specs=in_specs + [pl.BlockSpec((1, wide), lambda b, h, i: (0, h))],
        out_specs=pl.BlockSpec((tq, wide), lambda b, h, i: (b * nqb + i, h)),
        out_shape=jax.ShapeDtypeStruct((batch * n, DIFF_W), BF16),
        compiler_params=_params("parallel", "parallel", "arbitrary"),
        name="diff_attn",
    )(*operands, gn)


def _seg64_sum_wide(x):
    return jnp.concatenate([_seg64_sum(x[:, j * LANES:(j + 1) * LANES]) for j in range(x.shape[1] // LANES)], axis=1)


_SCAN_SLOTS = 3
_SCAN_DIR_NJ = (_SCAN_SLOTS - 1) * RWKV_HEADS
_SCAN_NJ = 2 * _SCAN_DIR_NJ + RWKV_HEADS


def _rwkv_pre_kernel(r_ref, k_ref, v_ref, lo_ref, wup_ref, aup_ref, gup_ref, w0_ref, a0_ref, kk_ref, ka_ref, rk_ref,
                     op_ref, g_ref, bonus_ref):
    W = RWKV_W
    col = lambda q: slice(q * W, (q + 1) * W)
    r = r_ref[...]
    k = k_ref[...]
    v = v_ref[...]
    lora = lo_ref[...]
    kk = k * kk_ref[...]
    kkn = kk * lax.rsqrt(_seg64_sum_wide(kk * kk) + EPS)
    g_ref[...] = _dot_bf16(_sigmoid(lora[:, 2 * LANES:3 * LANES]), gup_ref[...])
    pre = w0_ref[...] + _dot_bf16(jnp.tanh(lora[:, 0:LANES]), wup_ref[...])
    decay = jnp.exp(-jnp.exp(-_softplus(-pre) - 0.5))
    a = _sigmoid(a0_ref[...] + _dot_bf16(lora[:, LANES:2 * LANES], aup_ref[...]))
    lo = _lane_lo((1, LANES))

    def store_slot(j0, x1, x2):
        for h in range(RWKV_HEADS):
            blk = slice((h // 2) * LANES, (h // 2 + 1) * LANES)
            if h % 2 == 0:
                out = jnp.where(lo, x1[:, blk], pltpu.roll(x2[:, blk], RWKV_HS, axis=1))
            else:
                out = jnp.where(lo, pltpu.roll(x1[:, blk], RWKV_HS, axis=1), x2[:, blk])
            op_ref[:, j0 + h, :] = out

    bonus = None
    for d in range(2):
        a_d = a[:, col(d)]
        k_d = k * (1.0 + (a_d - 1.0) * ka_ref[...])
        t = _seg64_sum_wide(r * k_d * rk_ref[...])
        bonus = t if bonus is None else bonus + t
        store_slot(d * _SCAN_DIR_NJ, kkn, decay[:, col(d)])
        store_slot(d * _SCAN_DIR_NJ + RWKV_HEADS, k_d, kkn * a_d)
    store_slot(2 * _SCAN_DIR_NJ, r, v)
    bonus_ref[...] = bonus


def _rwkv_pre(p, r_blk, k_blk, v_blk, lo_blk, wup_bd, aup_bd, gup, w0, a0, k_k, k_a, r_k):
    tm = 256
    w = RWKV_W
    row = lambda n: pl.BlockSpec((1, n), lambda i: (0, 0))
    full = lambda a, b: pl.BlockSpec((a, b), lambda i: (0, 0))
    return pl.pallas_call(
        _rwkv_pre_kernel,
        grid=(N_TOK // tm,),
        in_specs=[pl.BlockSpec((tm, w), lambda i: (i, r_blk)),
                  pl.BlockSpec((tm, w), lambda i: (i, k_blk)),
                  pl.BlockSpec((tm, w), lambda i: (i, v_blk)),
                  pl.BlockSpec((tm, 3 * LANES), lambda i: (i, lo_blk)),
                  full(LANES, 2 * w), full(LANES, 2 * w), full(LANES, w),
                  row(2 * w), row(2 * w), row(w), row(w), row(w)],
        out_specs=[pl.BlockSpec((tm, _SCAN_NJ, LANES), lambda i: (i, 0, 0)),
                   pl.BlockSpec((tm, w), lambda i: (i, 0)), pl.BlockSpec((tm, w), lambda i: (i, 0))],
        out_shape=[jax.ShapeDtypeStruct((N_TOK, _SCAN_NJ, LANES), F32),
                   jax.ShapeDtypeStruct((N_TOK, w), F32), jax.ShapeDtypeStruct((N_TOK, w), F32)],
        compiler_params=_params("parallel"),
        name="rwkv_pre",
    )(p, p, p, p, wup_bd, aup_bd, gup, w0, a0, k_k, k_a, r_k)


_SCAN_CHUNK = 32
_SCAN_UNROLL = 32


def _rwkv_first_sa(s_ref, sa_ref, kk, n_k):
    nv = s_ref.shape[1]
    chunk = min(_SCAN_CHUNK, nv)
    for c0 in range(0, nv, chunk):
        def body(k, acc):
            return acc + s_ref[k, c0:c0 + chunk, :] * kk(k)
        sa_ref[c0:c0 + chunk, :] = lax.fori_loop(0, n_k, body, jnp.zeros((chunk, LANES), F32), unroll=_SCAN_UNROLL)


def _rwkv_step(s_ref, sa_ref, kk_next, w, kd, b, r, v_at, n_k):
    nv = s_ref.shape[1]
    chunk = min(_SCAN_CHUNK, nv)
    ys = []
    for c0 in range(0, nv, chunk):
        sa = sa_ref[c0:c0 + chunk, :]
        vc = v_at(c0, chunk)

        def body(k, acc):
            y_acc, sa_acc = acc
            s_new = s_ref[k, c0:c0 + chunk, :] * w(k) - sa * b(k) + vc * kd(k)
            s_ref[k, c0:c0 + chunk, :] = s_new
            return y_acc + s_new * r(k), sa_acc + s_new * kk_next(k)

        zero = jnp.zeros((chunk, LANES), F32)
        y_acc, sa_acc = lax.fori_loop(0, n_k, body, (zero, zero), unroll=_SCAN_UNROLL)
        sa_ref[c0:c0 + chunk, :] = sa_acc
        ys.append(y_acc)
    return ys[0] if len(ys) == 1 else jnp.concatenate(ys, axis=0)


_CTX_TB = 32


def _load_scan_operands(kt_ref, t, slabs):
    for s, x in enumerate(slabs):
        xt = x.T
        kt_ref[t, 2 * s] = xt[:RWKV_HS]
        kt_ref[t, 2 * s + 1] = xt[RWKV_HS:]


def _rwkv_scan_ctx_kernel(xd_ref, xs_ref, y_ref, st_ref, s_ref, sa_ref, kt_ref, ys_ref):
    d = pl.program_id(0)
    tb = pl.program_id(1)
    nh = RWKV_HEADS
    step_t = lambda i: jnp.where(d == 0, i, _CTX_TB - 1 - i)

    @pl.when(tb == 0)
    def _():
        s_ref[...] = jnp.zeros_like(s_ref)

    def load_t(t, carry):
        slabs = [jnp.concatenate([xd_ref[b, t, s * nh:(s + 1) * nh, :] for b in range(BATCH)], axis=0)
                 for s in range(_SCAN_SLOTS - 1)]
        slabs.append(jnp.concatenate([xs_ref[b, t] for b in range(BATCH)], axis=0))
        _load_scan_operands(kt_ref, t, slabs)
        return carry

    lax.fori_loop(0, _CTX_TB, load_t, 0, unroll=4)

    t0 = step_t(0)
    _rwkv_first_sa(s_ref, sa_ref, lambda k: kt_ref[t0, 0, pl.ds(k, 1), :], RWKV_HS)

    def step(i, carry):
        t = step_t(i)
        tn = step_t(jnp.minimum(i + 1, _CTX_TB - 1))
        row = lambda q, tt: (lambda k: kt_ref[tt, q, pl.ds(k, 1), :])
        ys_ref[t] = _rwkv_step(s_ref, sa_ref, row(0, tn), row(1, t), row(2, t), row(3, t), row(4, t),
                               lambda c0, n: kt_ref[t, 5, pl.ds(c0, n), :], RWKV_HS)
        return carry

    lax.fori_loop(0, _CTX_TB, step, 0)

    def store_t(i, carry):
        z = jnp.concatenate([ys_ref[2 * i], ys_ref[2 * i + 1]], axis=0).T
        z_odd = pltpu.roll(z, RWKV_HS, axis=1)
        for b in range(BATCH):
            y_ref[b, 2 * i] = z[b * nh:(b + 1) * nh, :RWKV_HS]
            y_ref[b, 2 * i + 1] = z_odd[b * nh:(b + 1) * nh, :RWKV_HS]
        return carry

    lax.fori_loop(0, _CTX_TB // 2, store_t, 0, unroll=4)

    @pl.when(tb == pl.num_programs(1) - 1)
    def _():
        st_ref[0] = s_ref[...]


def _rwkv_scan_ctx(op4):
    nt = SEQ // _CTX_TB
    hs = RWKV_HS
    tblk = lambda d, tb: jnp.where(d == 0, tb, nt - 1 - tb)
    return pl.pallas_call(
        _rwkv_scan_ctx_kernel,
        grid=(2, nt),
        in_specs=[pl.BlockSpec((BATCH, _CTX_TB, _SCAN_DIR_NJ, LANES), lambda d, tb: (0, tblk(d, tb), d, 0)),
                  pl.BlockSpec((BATCH, _CTX_TB, RWKV_HEADS, LANES),
                               lambda d, tb: (0, tblk(d, tb), 2 * _SCAN_DIR_NJ // RWKV_HEADS, 0))],
        out_specs=[pl.BlockSpec((BATCH, _CTX_TB, RWKV_HEADS, hs), lambda d, tb: (0, tblk(d, tb), d, 0)),
                   pl.BlockSpec((1, hs, hs, LANES), lambda d, tb: (d, 0, 0, 0))],
        out_shape=[jax.ShapeDtypeStruct((BATCH, SEQ, 2 * RWKV_HEADS, hs), F32),
                   jax.ShapeDtypeStruct((2, hs, hs, LANES), F32)],
        scratch_shapes=[pltpu.VMEM((hs, hs, LANES), F32), pltpu.VMEM((hs, LANES), F32),
                        pltpu.VMEM((_CTX_TB, 2 * _SCAN_SLOTS, hs, LANES), F32),
                        pltpu.VMEM((_CTX_TB, hs, LANES), F32)],
        compiler_params=_params("parallel", "arbitrary"),
        name="rwkv_scan_ctx",
    )(op4, op4)


_LAT_TB = 64
_LAT_VSPLIT = 4
_LAT_STATES = 2 * DEC_BATCH * RWKV_HEADS
_LAT_VROWS = RWKV_HS // _LAT_VSPLIT


def _rwkv_scan_lat_kernel(xf0_ref, xf0s_ref, xf1_ref, xf1s_ref, xb0_ref, xb0s_ref, xb1_ref, xb1s_ref,
                          s0_ref, yf_ref, yb_ref, s_ref, sa_ref, kt_ref, v_ref, ys_ref):
    @pl.when(pl.program_id(0) == 0)
    def _():
        s_ref[...] = s0_ref[...]

    group = lax.broadcasted_iota(jnp.int32, (_LAT_VROWS, LANES), 1) // _LAT_STATES
    nh = RWKV_HEADS

    def load_t(t, carry):
        tr = _LAT_TB - 1 - t
        for s in range(_SCAN_SLOTS):
            if s < _SCAN_SLOTS - 1:
                heads = slice(s * nh, (s + 1) * nh)
                x = jnp.concatenate([xf0_ref[0, t, heads, :], xf1_ref[0, t, heads, :],
                                     xb0_ref[0, tr, heads, :], xb1_ref[0, tr, heads, :]], axis=0)
            else:
                x = jnp.concatenate([xf0s_ref[0, t], xf1s_ref[0, t], xb0s_ref[0, tr], xb1s_ref[0, tr]], axis=0)
            xt = jnp.concatenate([x] * _LAT_VSPLIT, axis=0).T
            kt_ref[t, 2 * s] = xt[:RWKV_HS]
            if s < _SCAN_SLOTS - 1:
                kt_ref[t, 2 * s + 1] = xt[RWKV_HS:]
            else:
                v = jnp.zeros((_LAT_VROWS, LANES), F32)
                for g in range(_LAT_VSPLIT):
                    r0 = RWKV_HS + g * _LAT_VROWS
                    v = jnp.where(group == g, xt[r0:r0 + _LAT_VROWS, :], v)
                v_ref[t] = v
        return carry

    lax.fori_loop(0, _LAT_TB, load_t, 0, unroll=8)

    _rwkv_first_sa(s_ref, sa_ref, lambda k: kt_ref[0, 0, pl.ds(k, 1), :], RWKV_HS)

    def step(t, carry):
        tn = jnp.minimum(t + 1, _LAT_TB - 1)
        row = lambda q, tt: (lambda k: kt_ref[tt, q, pl.ds(k, 1), :])
        ys_ref[t] = _rwkv_step(s_ref, sa_ref, row(0, tn), row(1, t), row(2, t), row(3, t), row(4, t),
                               lambda c0, n: v_ref[t, pl.ds(c0, n), :], RWKV_HS)
        return carry

    lax.fori_loop(0, _LAT_TB, step, 0)

    def store_t(i, carry):
        rows = [jnp.where(group == g, ys_ref[2 * i + j], 0.0) for j in range(2) for g in range(_LAT_VSPLIT)]
        z = jnp.concatenate(rows, axis=0).T
        y = (z[0:_LAT_STATES] + z[_LAT_STATES:2 * _LAT_STATES]
             + z[2 * _LAT_STATES:3 * _LAT_STATES] + z[3 * _LAT_STATES:4 * _LAT_STATES])
        y_odd = pltpu.roll(y, RWKV_HS, axis=1)
        half = _LAT_STATES // 2
        for b in range(DEC_BATCH):
            rows_f = slice(b * nh, (b + 1) * nh)
            rows_b = slice(half + b * nh, half + (b + 1) * nh)
            yf_ref[b, 2 * i] = y[rows_f, :RWKV_HS]
            yf_ref[b, 2 * i + 1] = y_odd[rows_f, :RWKV_HS]
            yb_ref[b, _LAT_TB - 1 - 2 * i] = y[rows_b, :RWKV_HS]
            yb_ref[b, _LAT_TB - 2 - 2 * i] = y_odd[rows_b, :RWKV_HS]
        return carry

    lax.fori_loop(0, _LAT_TB // 2, store_t, 0, unroll=4)


def _rwkv_scan_lat(op4, s0):
    hs = RWKV_HS
    nv = _LAT_VROWS
    nt = DEC_SEQ // _LAT_TB
    per_seq = SEQ // _LAT_TB
    first = N_CTX // SEQ

    def x_specs(b, d):
        tblk = (lambda tb: tb) if d == 0 else (lambda tb: nt - 1 - tb)
        chunk = lambda tb: first + b * (DEC_SEQ // SEQ) + tblk(tb) // per_seq
        return [pl.BlockSpec((1, _LAT_TB, _SCAN_DIR_NJ, LANES), lambda tb: (chunk(tb), tblk(tb) % per_seq, d, 0)),
                pl.BlockSpec((1, _LAT_TB, RWKV_HEADS, LANES),
                             lambda tb: (chunk(tb), tblk(tb) % per_seq, 2 * _SCAN_DIR_NJ // RWKV_HEADS, 0))]

    y_shape = jax.ShapeDtypeStruct((DEC_BATCH, DEC_SEQ, RWKV_HEADS, hs), F32)
    y_blk = (DEC_BATCH, _LAT_TB, RWKV_HEADS, hs)
    return pl.pallas_call(
        _rwkv_scan_lat_kernel,
        grid=(nt,),
        in_specs=x_specs(0, 0) + x_specs(1, 0) + x_specs(0, 1) + x_specs(1, 1)
        + [pl.BlockSpec((hs, nv, LANES), lambda tb: (0, 0, 0))],
        out_specs=[pl.BlockSpec(y_blk, lambda tb: (0, tb, 0, 0)),
                   pl.BlockSpec(y_blk, lambda tb: (0, nt - 1 - tb, 0, 0))],
        out_shape=[y_shape, y_shape],
        scratch_shapes=[pltpu.VMEM((hs, nv, LANES), F32), pltpu.VMEM((nv, LANES), F32),
                        pltpu.VMEM((_LAT_TB, 2 * _SCAN_SLOTS - 1, hs, LANES), F32),
                        pltpu.VMEM((_LAT_TB, nv, LANES), F32), pltpu.VMEM((_LAT_TB, nv, LANES), F32)],
        compiler_params=_params("arbitrary"),
        name="rwkv_scan_lat",
    )(*([op4] * 8), s0)


def _rwkv_post_kernel(yc_ref, ylf_ref, ylb_ref, bonus_ref, v_ref, g_ref, gn_ref, o_ref):
    def finish(head_sum):
        y = jnp.concatenate([head_sum(h) for h in range(RWKV_HEADS)], axis=1)
        y = y * lax.rsqrt(_seg64_sum_wide(y * y) * (1.0 / RWKV_HS) + EPS) * gn_ref[...]
        o_ref[...] = ((y + bonus_ref[...] * v_ref[...]) * g_ref[...]).astype(o_ref.dtype)

    @pl.when(pl.program_id(0) < N_CTX // SEQ)
    def _():
        finish(lambda h: yc_ref[0, :, h, :] + yc_ref[0, :, RWKV_HEADS + h, :])

    @pl.when(pl.program_id(0) >= N_CTX // SEQ)
    def _():
        finish(lambda h: ylf_ref[0, :, h, :] + ylb_ref[0, :, h, :])


def _rwkv_post(y_ctx, y_lat_f, y_lat_b, bonus, p, v_blk, g, gn):
    tm = SEQ
    w = RWKV_W
    hs = RWKV_HS
    n_ctx = N_CTX // SEQ
    lat = lambda y: y.reshape(N_LAT // SEQ, SEQ, RWKV_HEADS, hs)
    spec = pl.BlockSpec((tm, w), lambda i: (i, 0))
    lat_spec = pl.BlockSpec((1, SEQ, RWKV_HEADS, hs), lambda i: (jnp.maximum(i - n_ctx, 0), 0, 0, 0))
    return pl.pallas_call(
        _rwkv_post_kernel,
        grid=(N_TOK // tm,),
        in_specs=[pl.BlockSpec((1, SEQ, 2 * RWKV_HEADS, hs), lambda i: (jnp.minimum(i, n_ctx - 1), 0, 0, 0)),
                  lat_spec, lat_spec, spec, pl.BlockSpec((tm, w), lambda i: (i, v_blk)), spec,
                  pl.BlockSpec((1, w), lambda i: (0, 0))],
        out_specs=spec,
        out_shape=jax.ShapeDtypeStruct((N_TOK, w), BF16),
        compiler_params=_params("parallel"),
        name="rwkv_post",
    )(y_ctx, lat(y_lat_f), lat(y_lat_b), bonus, p, g, gn)


def _value_split_layout(x):
    lead = x.shape[:-2]
    n = len(lead)
    x = x.reshape(lead + (_LAT_STATES, _LAT_VSPLIT, _LAT_VROWS))
    return jnp.transpose(x, tuple(range(n)) + (n + 2, n + 1, n)).reshape(lead + (_LAT_VROWS, LANES))


_EVEN_HEAD = MLA_Q_RANK + MLA_KV_RANK
_EVEN_PACKED = _EVEN_HEAD + LANES + 2 * RET_HEADS * (RET_DK + RET_DV)


def _pack_even_weight_kernel(w_ref, o_ref):
    w = w_ref[...]
    o_ref[:, 0:_EVEN_HEAD] = w[:, 0:_EVEN_HEAD]
    o_ref[:, _EVEN_HEAD:_EVEN_HEAD + LANES] = jnp.zeros((w.shape[0], LANES), F32)
    o_ref[:, _EVEN_HEAD + MLA_NOPE:_EVEN_HEAD + MLA_QK] = w[:, _EVEN_HEAD:_EVEN_HEAD + MLA_ROPE]
    o_ref[:, _EVEN_HEAD + LANES:] = w[:, _EVEN_HEAD + MLA_ROPE:]


def _pack_even_weight(w_in):
    tm = 256
    return pl.pallas_call(
        _pack_even_weight_kernel,
        grid=(D_MODEL // tm,),
        in_specs=[pl.BlockSpec((tm, w_in.shape[1]), lambda i: (i, 0))],
        out_specs=pl.BlockSpec((tm, _EVEN_PACKED), lambda i: (i, 0)),
        out_shape=jax.ShapeDtypeStruct((D_MODEL, _EVEN_PACKED), F32),
        compiler_params=_params("parallel"),
        name="pack_even_weight",
    )(w_in)


def _even_layer(x, mod, g_mix, w_in, q_norm, kv_norm, w_uq, w_ukv, qn, kn, ret_decay, ret_gn,
                cache_ckv, cache_krope, state_ret, tabs_m):
    w_p = _pack_even_weight(w_in)
    p = _inproj(x, g_mix, mod, w_p, jnp.zeros((1, w_p.shape[1]), F32), 1024, None)
    CKV_BLK, KR_BLK, RQ_BLK, RK_BLK, RV_BLK, RG_BLK = 2, 3, 4, 6, 8, 12

    def head_pad(w, n_head, d_head, c0, c1):
        w = w.reshape(w.shape[0], n_head, d_head)[:, :, c0:c1]
        return jnp.pad(w, ((0, 0), (0, 0), (0, LANES - (c1 - c0)))).reshape(w.shape[0], n_head * LANES)

    w_uq_p = head_pad(w_uq, MLA_HEADS, MLA_QK, 0, MLA_QK)
    wk_p = head_pad(w_ukv, MLA_HEADS, MLA_NOPE + MLA_V, 0, MLA_NOPE)
    wv_p = head_pad(w_ukv, MLA_HEADS, MLA_NOPE + MLA_V, MLA_NOPE, MLA_NOPE + MLA_V)
    qn_p = jnp.pad(qn, (0, LANES - MLA_QK))[None]
    kn_p = jnp.pad(kn, (0, LANES - MLA_QK))[None]

    q = _mla_q(p, q_norm[None], w_uq_p, qn_p, tabs_m)
    k, v, ckvn = _mla_kv(p, CKV_BLK, p, KR_BLK, kv_norm[None], wk_p, wv_p, kn_p, tabs_m, N_TOK, True)

    n_c = DEC_BATCH * PAST_LEN
    kr_c = jnp.pad(cache_krope.reshape(n_c, MLA_ROPE), ((0, 0), (MLA_NOPE, LANES - MLA_QK)))
    k_c, v_c, _ = _mla_kv(cache_ckv.reshape(n_c, MLA_KV_RANK), 0, kr_c, 0, kv_norm[None], wk_p, wv_p, kn_p,
                          tabs_m, n_c, False)

    o_ctx = _mla_attn(q, k, v, None, BATCH, SEQ, 0, SEQ, MLA_HEADS // 2)
    o_lat = _mla_attn(q, k, v, (k_c, v_c), DEC_BATCH, DEC_SEQ, N_CTX, 256, 1)

    log_g = -_softplus(-ret_decay)
    gn = ret_gn[None]
    s0_ctx = jnp.zeros((BATCH, 2, RET_HEADS * RET_DK, RET_DV), F32)
    r_ctx, st_ctx = _retention(log_g, p, RQ_BLK, RK_BLK, RV_BLK, RG_BLK, s0_ctx, gn, BATCH, SEQ, 0, SEQ, True)
    s0_lat = state_ret.reshape(DEC_BATCH, 2, RET_HEADS * RET_DK, RET_DV)
    r_lat, _ = _retention(log_g, p, RQ_BLK, RK_BLK, RV_BLK, RG_BLK, s0_lat, gn, DEC_BATCH, DEC_SEQ, N_CTX, 256,
                          False)

    mix = [(o_ctx, o_lat), (r_ctx, r_lat)]
    new_ckv = ckvn[:N_CTX].reshape(BATCH, SEQ, MLA_KV_RANK)
    new_krope = p[:N_CTX, KR_BLK * LANES + MLA_NOPE:KR_BLK * LANES + MLA_QK]
    new_krope = new_krope.reshape(BATCH, SEQ, MLA_ROPE)
    new_ret = st_ctx.reshape(BATCH, 2, RET_HEADS, RET_DK, RET_DV)
    return mix, new_ckv, new_krope, new_ret


def _odd_layer(x, mod, g_mix, w_in, qn, kn, lam, diff_gn, mu, w0, w_up, a0, a_up, g_up, k_k, k_a, r_k, gn,
               cache_k, cache_v, state_rwkv, tabs_d, lam_init):
    w_p = w_in
    n_in = w_p.shape[1]
    mu_full = jnp.concatenate([jnp.zeros((3 * DIFF_W,), F32), mu])[None]
    p = _inproj(x, g_mix, mod, w_p, mu_full, 384, (3 * DIFF_W) // 384)
    DV_BLK, R_BLK, K_BLK, V_BLK = 2, 3, 4, 5
    LO_BLK = (6 * RWKV_W) // (3 * LANES)

    qn_p = jnp.tile(qn, 2)[None]
    kn_p = jnp.tile(kn, 2)[None]
    q, k, dk, dv = _diff_qk(p, qn_p, kn_p, tabs_d)

    n_c = DEC_BATCH * PAST_LEN
    cache = (cache_k.reshape(n_c, DIFF_W), cache_v.reshape(n_c, DIFF_W))
    dgn = diff_gn[None]
    o_ctx = _diff_attn(lam, q, k, p, DV_BLK * DIFF_HEADS, None, dgn, BATCH, SEQ, 0, SEQ, lam_init, DIFF_HEADS)
    o_lat = _diff_attn(lam, q, k, p, DV_BLK * DIFF_HEADS, cache, dgn, DEC_BATCH, DEC_SEQ, N_CTX, 256, lam_init, 1)

    zero = jnp.zeros((RWKV_W_LORA, RWKV_W), F32)
    wup_bd = jnp.concatenate([jnp.concatenate([w_up[0], zero], 1), jnp.concatenate([zero, w_up[1]], 1)], 0)
    aup_bd = jnp.concatenate([jnp.concatenate([a_up[0], zero], 1), jnp.concatenate([zero, a_up[1]], 1)], 0)
    op, g, bonus = _rwkv_pre(p, R_BLK, K_BLK, V_BLK, LO_BLK, wup_bd, aup_bd, g_up, w0.reshape(1, -1),
                             a0.reshape(1, -1), k_k[None], k_a[None], r_k.reshape(1, -1))
    op4 = op.reshape(N_TOK // SEQ, SEQ, _SCAN_NJ, LANES)
    y_ctx, st_ctx = _rwkv_scan_ctx(op4)
    s0_lat = jnp.transpose(state_rwkv, (4, 1, 0, 2, 3)).reshape(RWKV_HS, _LAT_STATES, RWKV_HS)
    y_lat_f, y_lat_b = _rwkv_scan_lat(op4, _value_split_layout(s0_lat))
    rw_o = _rwkv_post(y_ctx, y_lat_f, y_lat_b, bonus, p, V_BLK, g, gn[None])

    mix = [(o_ctx, o_lat), rw_o]
    new_dk = dk.reshape(BATCH, SEQ, DIFF_HEADS, 2, DIFF_DH)
    new_dv = dv.reshape(BATCH, SEQ, DIFF_HEADS, 2 * DIFF_DH)
    new_rwkv = jnp.transpose(st_ctx.reshape(2, RWKV_HS, RWKV_HS, BATCH, RWKV_HEADS), (3, 0, 4, 2, 1))
    return mix, new_dk, new_dv, new_rwkv


def kernel(x_prompt, x_sample, cache_mla_ckv, cache_mla_krope, state_ret, cache_diff_k, cache_diff_v, state_rwkv,
           c, c_ctx, ada_w, ada_b, norm_mix_g, norm_ffn_g, w_out, ffn_up, ffn_conv_w, ffn_conv_b, ffn_down,
           a_w_in, mla_q_norm, mla_kv_norm, mla_w_uq, mla_w_ukv, mla_qn, mla_kn, ret_decay, ret_gn,
           b_w_in, diff_qn, diff_kn, diff_lam, diff_gn, rwkv_mu, rwkv_w0, rwkv_w_up, rwkv_a0, rwkv_a_up,
           rwkv_g_up, rwkv_k_k, rwkv_k_a, rwkv_r_k, rwkv_gn):
    x = (x_prompt.reshape(N_CTX, D_MODEL), x_sample.reshape(N_LAT, D_MODEL))
    cond8 = jnp.pad(jnp.concatenate([c_ctx[None], c], 0), ((0, 8 - N_GROUPS), (0, 0)))
    mod = _modulation(cond8, ada_w, ada_b)

    tabs_m = _rope_tables(MLA_ROPE, (MLA_NOPE,))
    tabs_d = _rope_tables(DIFF_DH, (0, DIFF_DH))

    outs = {}
    for l in range(DEPTH):
        j = l // 2
        g_mix = norm_mix_g[l][None]
        if l % 2 == 0:
            mix, outs["ckv"], outs["krope"], outs["ret"] = _even_layer(
                x, mod[l], g_mix, a_w_in[j], mla_q_norm[j], mla_kv_norm[j], mla_w_uq[j], mla_w_ukv[j], mla_qn[j],
                mla_kn[j], ret_decay[j], ret_gn[j], cache_mla_ckv[:, j], cache_mla_krope[:, j], state_ret[:, j],
                tabs_m)
        else:
            lam_init = 0.8 - 0.6 * math.exp(-0.3 * l)
            mix, outs["dk"], outs["dv"], outs["rwkv"] = _odd_layer(
                x, mod[l], g_mix, b_w_in[j], diff_qn[j], diff_kn[j], diff_lam[j], diff_gn[j], rwkv_mu[j],
                rwkv_w0[j], rwkv_w_up[j], rwkv_a0[j], rwkv_a_up[j], rwkv_g_up[j], rwkv_k_k[j], rwkv_k_a[j],
                rwkv_r_k[j], rwkv_gn[j], cache_diff_k[:, j], cache_diff_v[:, j], state_rwkv[:, j], tabs_d, lam_init)
        x = _resid_proj(mix, w_out, l, x, mod[l], 2)
        act = _ffn_up(x, norm_ffn_g[l][None], mod[l], ffn_up, l, ffn_conv_w[l], ffn_conv_b[l])
        x = _resid_proj([act], ffn_down, l, x, mod[l], 5, split_out=(l == DEPTH - 1))

    y_prompt = x[0].reshape(BATCH, SEQ, D_MODEL)
    y_sample = x[1].reshape(DEC_BATCH, DEC_SEQ, D_MODEL)
    return (y_prompt, y_sample, outs["ckv"][:, None], outs["krope"][:, None], outs["ret"][:, None],
            outs["dk"][:, None], outs["dv"][:, None], outs["rwkv"][:, None])
```

```python
import functools
import math

import numpy as np
import jax
import jax.numpy as jnp
from jax import lax
from jax.experimental import pallas as pl
from jax.experimental.pallas import tpu as pltpu

D_MODEL = 1024
BATCH = 16
SEQ = 256
DEPTH = 2
DEC_BATCH = 2
DEC_SEQ = 1024
PAST_LEN = 512
GRID_W = 64
EPS = 1e-6
ROPE_BASE = 10000.0

MLA_HEADS = 8
MLA_Q_RANK = 256
MLA_KV_RANK = 128
MLA_NOPE = 64
MLA_ROPE = 32
MLA_V = 64
MLA_QK = MLA_NOPE + MLA_ROPE
RET_HEADS = 4
RET_DK = 64
RET_DV = 128
DIFF_HEADS = 4
DIFF_DH = 64
DIFF_W = DIFF_HEADS * 2 * DIFF_DH
RWKV_HEADS = 8
RWKV_HS = 64
RWKV_W = RWKV_HEADS * RWKV_HS
RWKV_W_LORA = 64
D_FF = 2816

N_CTX = BATCH * SEQ
N_LAT = DEC_BATCH * DEC_SEQ
N_TOK = N_CTX + N_LAT
N_GROUPS = 1 + DEC_BATCH

LANES = 128
VMEM_LIMIT = 56 * 1024 * 1024

_PREC = lax.Precision.HIGHEST
F32 = jnp.float32


def _dot_tn(a, b):
    return lax.dot_general(a, b, (((0,), (0,)), ((), ())), precision=_PREC, preferred_element_type=F32)


BF16 = jnp.bfloat16


def _dot_bf16(a, b):
    return jnp.dot(a.astype(BF16), b.astype(BF16), preferred_element_type=F32)


def _dot_nt_bf16(a, b):
    return lax.dot_general(a.astype(BF16), b.astype(BF16), (((1,), (1,)), ((), ())), preferred_element_type=F32)


def _params(*sem):
    return pltpu.CompilerParams(dimension_semantics=sem, vmem_limit_bytes=VMEM_LIMIT)


def _sigmoid(x):
    return 1.0 / (1.0 + jnp.exp(-x))


def _silu(x):
    return x * _sigmoid(x)


def _softplus(x):
    return jnp.maximum(x, 0.0) + jnp.log(1.0 + jnp.exp(-jnp.abs(x)))


def _rms(x, n):
    return x * lax.rsqrt(jnp.sum(x * x, axis=-1, keepdims=True) * (1.0 / n) + EPS)


def _lane_lo(shape):
    return lax.broadcasted_iota(jnp.int32, shape, len(shape) - 1) < 64


def _seg64_sum(x):
    lo = _lane_lo(x.shape)
    s_lo = jnp.sum(jnp.where(lo, x, 0.0), axis=-1, keepdims=True)
    s_hi = jnp.sum(jnp.where(lo, 0.0, x), axis=-1, keepdims=True)
    return jnp.where(lo, s_lo, s_hi)


_SUBLANES = 8


def _seq_neighbours(p, tile, tile_rows):
    is_ctx = tile * tile_rows < N_CTX
    sub = lax.broadcasted_iota(jnp.int32, (_SUBLANES, 1), 0)

    def shifted(rolled, edge_sublane, group_of_seq):
        pieces, start = [], 0
        for q in range(tile_rows // SEQ):
            g0 = q * SEQ + group_of_seq
            outer = (q == 0) if group_of_seq == 0 else (q == tile_rows // SEQ - 1)
            edge = (sub == edge_sublane) if outer else ((sub == edge_sublane) & is_ctx)
            pieces += [rolled[start:g0], jnp.where(edge, 0.0, rolled[g0:g0 + _SUBLANES])]
            start = g0 + _SUBLANES
        pieces.append(rolled[start:])
        return jnp.concatenate([x for x in pieces if x.shape[0]], axis=0)

    prev = shifted(pltpu.roll(p, 1, axis=0), 0, 0)
    nxt = shifted(pltpu.roll(p, tile_rows - 1, axis=0), _SUBLANES - 1, SEQ - _SUBLANES)
    return prev, nxt


def _group_of_tile(i, tile_rows):
    row = i * tile_rows
    return jnp.where(row < N_CTX, 0, 1 + (row - N_CTX) // DEC_SEQ)


def _modulation_kernel(c_ref, w_ref, b_ref, o_ref):
    o_ref[0] = _dot_bf16(_silu(c_ref[...]), w_ref[0]) + b_ref[0]


def _modulation(cond8, ada_w, ada_b):
    tn = 512
    n = 6 * D_MODEL
    out = pl.pallas_call(
        _modulation_kernel,
        grid=(DEPTH, n // tn),
        in_specs=[pl.BlockSpec((8, D_MODEL), lambda l, j: (0, 0)),
                  pl.BlockSpec((1, D_MODEL, tn), lambda l, j: (l, 0, j)),
                  pl.BlockSpec((1, 1, tn), lambda l, j: (l, 0, j))],
        out_specs=pl.BlockSpec((1, 8, tn), lambda l, j: (l, 0, j)),
        out_shape=jax.ShapeDtypeStruct((DEPTH, 8, n), F32),
        compiler_params=_params("parallel", "parallel"),
        name="modulation",
    )(cond8, ada_w, ada_b.reshape(DEPTH, 1, n))
    m = out[:, :N_GROUPS].reshape(DEPTH, N_GROUPS, 6, D_MODEL)
    return jnp.pad(m, ((0, 0), (0, 0), (0, 2), (0, 0)))


_TM_SEQ = 1024


def _norm_mod(x, g, mod, off):
    return _rms(x, D_MODEL) * g * (1.0 + mod[off + 1:off + 2, :]) + mod[off:off + 1, :]


def _row_split_specs(tm, width_block, col):
    n_ctx = N_CTX // tm
    return [pl.BlockSpec((tm, width_block), lambda i, j: (jnp.minimum(i, n_ctx - 1), col(j))),
            pl.BlockSpec((tm, width_block), lambda i, j: (jnp.maximum(i - n_ctx, 0), col(j)))]


def _for_row_tile(i, tm, ctx_ref, lat_ref, fn):
    @pl.when(i < N_CTX // tm)
    def _():
        fn(ctx_ref)

    @pl.when(i >= N_CTX // tm)
    def _():
        fn(lat_ref)


def _inproj_kernel(*refs, shift_from, split_x):
    x_refs, (g_ref, mod_ref, w_ref, mu_ref, o_ref, h_ref) = refs[:-6], refs[-6:]
    i = pl.program_id(0)

    @pl.when(pl.program_id(1) == 0)
    def _():
        def norm(x_ref):
            h_ref[...] = _norm_mod(x_ref[...], g_ref[...], mod_ref[0], 0).astype(BF16)

        if split_x:
            _for_row_tile(i, _TM_SEQ, x_refs[0], x_refs[1], norm)
        else:
            norm(x_refs[0])

    p = _dot_bf16(h_ref[...], w_ref[...])
    if shift_from is None:
        o_ref[...] = p
    else:
        @pl.when(pl.program_id(1) < shift_from)
        def _():
            o_ref[...] = p

        @pl.when(pl.program_id(1) >= shift_from)
        def _():
            prev, nxt = _seq_neighbours(p, i, _TM_SEQ)
            o_ref[...] = p + (0.5 * (prev + nxt) - p) * mu_ref[...]


def _inproj(x, g, mod, w, mu, tn, shift_from):
    n = w.shape[1]
    tm = _TM_SEQ
    split_x = isinstance(x, (tuple, list))
    x_specs = _row_split_specs(tm, D_MODEL, lambda j: 0) if split_x else [pl.BlockSpec((tm, D_MODEL), lambda i, j: (i, 0))]
    return pl.pallas_call(
        functools.partial(_inproj_kernel, shift_from=shift_from, split_x=split_x),
        grid=(N_TOK // tm, n // tn),
        in_specs=x_specs
        + [pl.BlockSpec((1, D_MODEL), lambda i, j: (0, 0)),
                  pl.BlockSpec((1, 8, D_MODEL), lambda i, j: (_group_of_tile(i, tm), 0, 0)),
                  pl.BlockSpec((D_MODEL, tn), lambda i, j: (0, j)),
                  pl.BlockSpec((1, tn), lambda i, j: (0, j))],
        out_specs=pl.BlockSpec((tm, tn), lambda i, j: (i, j)),
        out_shape=jax.ShapeDtypeStruct((N_TOK, n), F32),
        scratch_shapes=[pltpu.VMEM((tm, D_MODEL), BF16)],
        compiler_params=_params("parallel", "arbitrary"),
        name="inproj" if shift_from is None else "inproj_shift",
    )(*(x if split_x else [x]), g, mod, w, mu)


def _resid_kernel(*refs, gate_row, widths, split, split_x, split_out):
    n_in = sum(2 if sp else 1 for sp in split)
    a_refs = refs[:n_in]
    w_ref = refs[n_in]
    n_x = 2 if split_x else 1
    x_refs = refs[n_in + 1:n_in + 1 + n_x]
    mod_ref = refs[n_in + 1 + n_x]
    o_ref = refs[n_in + 2 + n_x:-1] if split_out else refs[n_in + 2 + n_x]
    a_bf_ref = refs[-1]
    i = pl.program_id(0)

    @pl.when(pl.program_id(1) == 0)
    def _():
        k0, r = 0, 0
        for width, sp in zip(widths, split):
            cols = slice(k0, k0 + width)
            if sp:
                ctx_ref, lat_ref = a_refs[r], a_refs[r + 1]

                @pl.when(i < N_CTX // _TM_SEQ)
                def _():
                    a_bf_ref[:, cols] = ctx_ref[...].astype(BF16)

                @pl.when(i >= N_CTX // _TM_SEQ)
                def _():
                    a_bf_ref[:, cols] = lat_ref[...].astype(BF16)
            else:
                a_bf_ref[:, cols] = a_refs[r][...].astype(BF16)
            k0 += width
            r += 2 if sp else 1

    d = mod_ref[0, gate_row:gate_row + 1, :] * _dot_bf16(a_bf_ref[...], w_ref[0])
    if split_x:
        def add(x_ref):
            o_ref[...] = x_ref[...] + d

        _for_row_tile(i, _TM_SEQ, x_refs[0], x_refs[1], add)
        return
    y = x_refs[0][...] + d
    if not split_out:
        o_ref[...] = y
    else:
        ctx_o_ref, lat_o_ref = o_ref

        @pl.when(i < N_CTX // _TM_SEQ)
        def _():
            ctx_o_ref[...] = y

        @pl.when(i >= N_CTX // _TM_SEQ)
        def _():
            lat_o_ref[...] = y


def _resid_proj(acts, w, layer, x, mod, gate_row, split_out=False):
    tm = _TM_SEQ
    split_x = isinstance(x, (tuple, list))
    assert not (split_x and split_out)
    n_ctx = N_CTX // tm
    split = [isinstance(a, (tuple, list)) for a in acts]
    widths = [a[0].shape[1] if sp else a.shape[1] for a, sp in zip(acts, split)]
    k = sum(widths)
    tn = 512 if k <= D_MODEL else 256
    nj = D_MODEL // tn
    in_specs, operands = [], []
    for a, width, sp in zip(acts, widths, split):
        if sp:
            in_specs += [pl.BlockSpec((tm, width), lambda i, j: (jnp.minimum(i, n_ctx - 1), 0)),
                         pl.BlockSpec((tm, width), lambda i, j: (jnp.maximum(i - n_ctx, 0), 0))]
            operands += list(a)
        else:
            in_specs.append(pl.BlockSpec((tm, width), lambda i, j: (i, 0)))
            operands.append(a)
    if split_out:
        out_specs = [pl.BlockSpec((tm, tn), lambda i, j: (jnp.minimum(i, n_ctx - 1), jnp.where(i < n_ctx, j, nj - 1))),
                     pl.BlockSpec((tm, tn), lambda i, j: (jnp.maximum(i - n_ctx, 0), jnp.where(i < n_ctx, 0, j)))]
        out_shape = [jax.ShapeDtypeStruct((N_CTX, D_MODEL), F32), jax.ShapeDtypeStruct((N_LAT, D_MODEL), F32)]
        sem = ("arbitrary", "arbitrary")
    else:
        out_specs = pl.BlockSpec((tm, tn), lambda i, j: (i, j))
        out_shape = jax.ShapeDtypeStruct((N_TOK, D_MODEL), F32)
        sem = ("parallel", "arbitrary")
    return pl.pallas_call(
        functools.partial(_resid_kernel, gate_row=gate_row, widths=tuple(widths), split=tuple(split),
                          split_x=split_x, split_out=split_out),
        grid=(N_TOK // tm, nj),
        in_specs=in_specs
        + [pl.BlockSpec((1, k, tn), lambda i, j: (layer, 0, j))]
        + (_row_split_specs(tm, tn, lambda j: j) if split_x else [pl.BlockSpec((tm, tn), lambda i, j: (i, j))])
        + [pl.BlockSpec((1, 8, tn), lambda i, j: (_group_of_tile(i, tm), 0, j))],
        out_specs=out_specs,
        out_shape=out_shape,
        scratch_shapes=[pltpu.VMEM((tm, k), BF16)],
        compiler_params=_params(*sem),
        name="resid_proj",
    )(*operands, w, *(x if split_x else [x]), mod)


def _ffn_up_kernel(x_ref, g_ref, mod_ref, wa_ref, wb_ref, cwa_ref, cwb_ref, cba_ref, cbb_ref, o_ref, h_ref):
    i = pl.program_id(0)

    @pl.when(pl.program_id(1) == 0)
    def _():
        h_ref[...] = _norm_mod(x_ref[...], g_ref[...], mod_ref[0], 3).astype(BF16)

    h = h_ref[...]

    def conv(w_ref, cw_ref, cb_ref):
        u = _dot_bf16(h, w_ref[0])
        prev, nxt = _seq_neighbours(u, i, _TM_SEQ)
        return prev * cw_ref[0:1, :] + u * cw_ref[1:2, :] + nxt * cw_ref[2:3, :] + cb_ref[...]

    o_ref[...] = (_silu(conv(wa_ref, cwa_ref, cba_ref)) * conv(wb_ref, cwb_ref, cbb_ref)).astype(o_ref.dtype)


def _ffn_up(x, g, mod, up, layer, cw, cb):
    tm, tn = _TM_SEQ, 256
    nb = D_FF // tn
    cb = cb.reshape(1, 2 * D_FF)
    return pl.pallas_call(
        _ffn_up_kernel,
        grid=(N_TOK // tm, nb),
        in_specs=[pl.BlockSpec((tm, D_MODEL), lambda i, j: (i, 0)),
                  pl.BlockSpec((1, D_MODEL), lambda i, j: (0, 0)),
                  pl.BlockSpec((1, 8, D_MODEL), lambda i, j: (_group_of_tile(i, tm), 0, 0)),
                  pl.BlockSpec((1, D_MODEL, tn), lambda i, j: (layer, 0, j)),
                  pl.BlockSpec((1, D_MODEL, tn), lambda i, j: (layer, 0, j + nb)),
                  pl.BlockSpec((3, tn), lambda i, j: (0, j)),
                  pl.BlockSpec((3, tn), lambda i, j: (0, j + nb)),
                  pl.BlockSpec((1, tn), lambda i, j: (0, j)),
                  pl.BlockSpec((1, tn), lambda i, j: (0, j + nb))],
        out_specs=pl.BlockSpec((tm, tn), lambda i, j: (i, j)),
        out_shape=jax.ShapeDtypeStruct((N_TOK, D_FF), BF16),
        scratch_shapes=[pltpu.VMEM((tm, D_MODEL), BF16)],
        compiler_params=_params("parallel", "arbitrary"),
        name="ffn_up",
    )(x, g, mod, up, up, cw, cw, cb, cb)


_ROPE_TM = 512


def _rope(y, c, s1, s2):
    return y * c + pltpu.roll(y, 1, axis=1) * s1 + pltpu.roll(y, LANES - 1, axis=1) * s2


def _rope_tables(rot_dim, lane_offsets):
    t = np.arange(DEC_SEQ)
    row, col = t // GRID_W, t % GRID_W
    n_freq = rot_dim // 4
    inv = ROPE_BASE ** (-np.arange(n_freq, dtype=np.float64) / n_freq)
    ang = np.concatenate([row[:, None] * inv, col[:, None] * inv], -1)
    cos, sin = np.cos(ang), np.sin(ang)
    n = _ROPE_TM + DEC_SEQ
    c, s1, s2 = np.ones((n, LANES)), np.zeros((n, LANES)), np.zeros((n, LANES))
    for a in lane_offsets:
        even = a + 2 * np.arange(rot_dim // 2)
        c[_ROPE_TM:, even] = cos
        c[_ROPE_TM:, even + 1] = cos
        s1[_ROPE_TM:, even + 1] = sin
        s2[_ROPE_TM:, even] = -sin
    return tuple(jnp.asarray(x, F32) for x in (c, s1, s2))


def _store_rotated(dst_ref, sl, y, tabs, rotate):
    if rotate is False:
        dst_ref[:, sl] = y
        return

    @pl.when(rotate)
    def _():
        dst_ref[:, sl] = _rope(y, *tabs)

    @pl.when(jnp.logical_not(rotate))
    def _():
        dst_ref[:, sl] = y


def _latent_tile():
    return pl.program_id(0) >= N_CTX // _ROPE_TM


def _rope_block(i):
    row = i * _ROPE_TM
    return jnp.where(row < N_CTX, 0, 1 + ((row - N_CTX) % DEC_SEQ) // _ROPE_TM)


def _mla_q_kernel(cq_ref, qnorm_ref, w_ref, qn_ref, c_ref, s1_ref, s2_ref, o_ref):
    xn = _rms(cq_ref[...], MLA_Q_RANK) * qnorm_ref[...]
    y = _dot_bf16(xn, w_ref[...])
    tabs = (c_ref[...], s1_ref[...], s2_ref[...])
    for h in range(MLA_HEADS):
        sl = slice(h * LANES, (h + 1) * LANES)
        _store_rotated(o_ref, sl, _rms(y[:, sl], MLA_QK) * qn_ref[...], tabs, _latent_tile())


def _mla_q(p, q_norm, w_uq_p, qn_p, tabs):
    tm = _ROPE_TM
    hw = MLA_HEADS * LANES
    tab_spec = pl.BlockSpec((tm, LANES), lambda i: (_rope_block(i), 0))
    return pl.pallas_call(
        _mla_q_kernel,
        grid=(N_TOK // tm,),
        in_specs=[pl.BlockSpec((tm, MLA_Q_RANK), lambda i: (i, 0)),
                  pl.BlockSpec((1, MLA_Q_RANK), lambda i: (0, 0)),
                  pl.BlockSpec((MLA_Q_RANK, hw), lambda i: (0, 0)),
                  pl.BlockSpec((1, LANES), lambda i: (0, 0)),
                  tab_spec, tab_spec, tab_spec],
        out_specs=pl.BlockSpec((tm, hw), lambda i: (i, 0)),
        out_shape=jax.ShapeDtypeStruct((N_TOK, hw), F32),
        compiler_params=_params("parallel"),
        name="mla_q",
    )(p, q_norm, w_uq_p, qn_p, *tabs)


def _mla_kv_kernel(ckv_ref, kr_ref, kvn_ref, wk_ref, wv_ref, kn_ref, c_ref, s1_ref, s2_ref,
                   k_ref, v_ref, ckvn_ref, *, norm_ckv):
    ckv = ckv_ref[...]
    if norm_ckv:
        ckv = _rms(ckv, MLA_KV_RANK) * kvn_ref[...]
    ckvn_ref[...] = ckv
    ckv_bf = ckv.astype(BF16)
    kk = _dot_bf16(ckv_bf, wk_ref[...])
    v_ref[...] = _dot_bf16(ckv_bf, wv_ref[...])
    kr = kr_ref[...]
    tabs = (c_ref[...], s1_ref[...], s2_ref[...])
    for h in range(MLA_HEADS):
        sl = slice(h * LANES, (h + 1) * LANES)
        _store_rotated(k_ref, sl, _rms(kk[:, sl] + kr, MLA_QK) * kn_ref[...], tabs,
                       _latent_tile() if norm_ckv else False)


def _mla_kv(ckv_src, ckv_blk, kr_src, kr_blk, kv_norm, wk_p, wv_p, kn_p, tabs, n_rows, own_tokens):
    tm = _ROPE_TM
    hw = MLA_HEADS * LANES
    tab_spec = pl.BlockSpec((tm, LANES), (lambda i: (_rope_block(i), 0)) if own_tokens else (lambda i: (0, 0)))
    return pl.pallas_call(
        functools.partial(_mla_kv_kernel, norm_ckv=own_tokens),
        grid=(n_rows // tm,),
        in_specs=[pl.BlockSpec((tm, LANES), lambda i: (i, ckv_blk)),
                  pl.BlockSpec((tm, LANES), lambda i: (i, kr_blk)),
                  pl.BlockSpec((1, LANES), lambda i: (0, 0)),
                  pl.BlockSpec((MLA_KV_RANK, hw), lambda i: (0, 0)),
                  pl.BlockSpec((MLA_KV_RANK, hw), lambda i: (0, 0)),
                  pl.BlockSpec((1, LANES), lambda i: (0, 0)),
                  tab_spec, tab_spec, tab_spec],
        out_specs=[pl.BlockSpec((tm, hw), lambda i: (i, 0)),
                   pl.BlockSpec((tm, hw), lambda i: (i, 0)),
                   pl.BlockSpec((tm, LANES), lambda i: (i, 0))],
        out_shape=[jax.ShapeDtypeStruct((n_rows, hw), F32),
                   jax.ShapeDtypeStruct((n_rows, hw), F32),
                   jax.ShapeDtypeStruct((n_rows, LANES), F32)],
        compiler_params=_params("parallel"),
        name="mla_kv",
    )(ckv_src, kr_src, kv_norm, wk_p, wv_p, kn_p, *tabs)


_LOG2E = math.log2(math.e)


def _softmax_parts(scores, scale):
    m = functools.reduce(jnp.maximum, [jnp.max(sc, axis=-1, keepdims=True) for sc in scores])
    ps = [jnp.exp2((sc - m) * (scale * _LOG2E)) for sc in scores]
    return ps, sum(jnp.sum(p, axis=-1, keepdims=True) for p in ps)


def _attend(q, sources, sl, scale):
    ps, l = _softmax_parts([_dot_nt_bf16(q, k_ref[:, sl]) for k_ref, _ in sources], scale)
    return sum(_dot_bf16(p, v_ref[:, sl]) for p, (_, v_ref) in zip(ps, sources)), l


def _mla_attn_kernel(*refs, pairs):
    q_ref, o_ref = refs[0], refs[-1]
    sources = [refs[1:3]] + ([refs[3:5]] if len(refs) == 6 else [])
    scale = MLA_QK ** -0.5
    for pr in range(pairs):
        outs = []
        for h in range(2):
            sl = slice((2 * pr + h) * LANES, (2 * pr + h + 1) * LANES)
            o, l = _attend(q_ref[:, sl], sources, sl, scale)
            outs.append(o / l)
        o_ref[:, pr * LANES:(pr + 1) * LANES] = (outs[0] + pltpu.roll(outs[1], MLA_V, axis=1)).astype(o_ref.dtype)


def _mla_attn(q, k, v, cache, batch, n, row0, tq, pairs):
    nqb = n // tq
    qb0 = row0 // tq
    kb0 = row0 // n
    wide = 2 * LANES * pairs
    kv_spec = pl.BlockSpec((n, wide), lambda b, h, i: (kb0 + b, h))
    in_specs, operands = [pl.BlockSpec((tq, wide), lambda b, h, i: (qb0 + b * nqb + i, h)), kv_spec, kv_spec], [q, k, v]
    if cache is not None:
        in_specs += [pl.BlockSpec((PAST_LEN, wide), lambda b, h, i: (b, h))] * 2
        operands += list(cache)
    return pl.pallas_call(
        functools.partial(_mla_attn_kernel, pairs=pairs),
        grid=(batch, MLA_HEADS // (2 * pairs), nqb),
        in_specs=in_specs,
        out_specs=pl.BlockSpec((tq, LANES * pairs), lambda b, h, i: (b * nqb + i, h)),
        out_shape=jax.ShapeDtypeStruct((batch * n, MLA_HEADS * MLA_V), BF16),
        compiler_params=_params("parallel", "parallel", "arbitrary"),
        name="mla_attn",
    )(*operands)


def _ret_kernel(lg_ref, q_ref, k_ref, v_ref, rg_ref, s0_ref, gn_ref, *out_and_scratch, n, tq, want_state):
    if want_state:
        o_ref, st_ref, decay_ref = out_and_scratch
    else:
        o_ref, decay_ref = out_and_scratch
    pair, qi, b = pl.program_id(0), pl.program_id(1), pl.program_id(2)
    q = q_ref[...]
    k = k_ref[...] * (RET_DK ** -0.5)
    lo = _lane_lo((1, LANES))
    row = (qi * tq + lax.broadcasted_iota(jnp.int32, (tq, 1), 0)).astype(F32)

    @pl.when(b == 0)
    def _():
        col = lax.broadcasted_iota(jnp.int32, (1, n), 1).astype(F32)
        diff = row - col
        for h in range(2):
            lgf = lg_ref[0, 2 * pair + h]
            lgb = lg_ref[1, 2 * pair + h]
            decay_ref[h] = (jnp.where(diff >= 0, jnp.exp(lgf * jnp.maximum(diff, 0.0)), 0.0)
                            + jnp.where(diff <= 0, jnp.exp(lgb * jnp.maximum(-diff, 0.0)), 0.0))

    for h in range(2):
        lgf = lg_ref[0, 2 * pair + h]
        lgb = lg_ref[1, 2 * pair + h]
        mask = lo if h == 0 else jnp.logical_not(lo)
        qh = jnp.where(mask, q, 0.0)
        vh = v_ref[:, h * LANES:(h + 1) * LANES]
        o = _dot_bf16(_dot_nt_bf16(qh, k) * decay_ref[h], vh)
        o = o + _dot_bf16(qh * jnp.exp(lgf * (row + 1.0)), s0_ref[0, 0])
        o = o + _dot_bf16(qh * jnp.exp(lgb * (n - row)), s0_ref[0, 1])
        y = _rms(o, RET_DV) * gn_ref[:, h * LANES:(h + 1) * LANES]
        o_ref[:, h * LANES:(h + 1) * LANES] = (_silu(rg_ref[:, h * LANES:(h + 1) * LANES]) * y).astype(o_ref.dtype)

    if want_state:
        pos = lax.broadcasted_iota(jnp.int32, (n, 1), 0).astype(F32)
        for d in range(2):
            acc = None
            for h in range(2):
                lg = lg_ref[d, 2 * pair + h]
                mask = lo if h == 0 else jnp.logical_not(lo)
                expo = (n - 1.0 - pos) if d == 0 else pos
                kd = jnp.where(mask, k * jnp.exp(lg * expo), 0.0)
                term = _dot_tn(kd, v_ref[:, h * LANES:(h + 1) * LANES])
                acc = term if acc is None else acc + term
            lg_rows = jnp.where(lax.broadcasted_iota(jnp.int32, (LANES, 1), 0) < 64,
                                lg_ref[d, 2 * pair], lg_ref[d, 2 * pair + 1])
            st_ref[0, d] = acc + s0_ref[0, d] * jnp.exp(lg_rows * n)


def _retention(log_g, p, q_blk, k_blk, v_blk, g_blk, s0, gn, batch, n, row0, tq, want_state):
    nqb = n // tq
    assert not want_state or nqb == 1
    qb0 = row0 // tq
    kb0 = row0 // n
    pairs = RET_HEADS // 2
    out_specs = [pl.BlockSpec((tq, 2 * LANES), lambda h, i, b: (b * nqb + i, h))]
    out_shape = [jax.ShapeDtypeStruct((batch * n, RET_HEADS * RET_DV), BF16)]
    if want_state:
        out_specs.append(pl.BlockSpec((1, 2, LANES, LANES), lambda h, i, b: (b, 0, h, 0)))
        out_shape.append(jax.ShapeDtypeStruct((batch, 2, RET_HEADS * RET_DK, RET_DV), F32))
    outs = pl.pallas_call(
        functools.partial(_ret_kernel, n=n, tq=tq, want_state=want_state),
        grid=(pairs, nqb, batch),
        in_specs=[pl.BlockSpec(memory_space=pltpu.SMEM),
                  pl.BlockSpec((tq, LANES), lambda h, i, b: (qb0 + b * nqb + i, q_blk + h)),
                  pl.BlockSpec((n, LANES), lambda h, i, b: (kb0 + b, k_blk + h)),
                  pl.BlockSpec((n, 2 * LANES), lambda h, i, b: (kb0 + b, v_blk // 2 + h)),
                  pl.BlockSpec((tq, 2 * LANES), lambda h, i, b: (qb0 + b * nqb + i, g_blk // 2 + h)),
                  pl.BlockSpec((1, 2, LANES, LANES), lambda h, i, b: (b, 0, h, 0)),
                  pl.BlockSpec((1, 2 * LANES), lambda h, i, b: (0, h))],
        out_specs=out_specs,
        out_shape=out_shape,
        scratch_shapes=[pltpu.VMEM((2, tq, n), F32)],
        compiler_params=_params("parallel", "parallel", "arbitrary"),
        name="retention",
    )(log_g, p, p, p, p, s0, gn)
    return outs if want_state else (outs[0], None)


def _diff_qk_kernel(q_ref, k_ref, v_ref, qn_ref, kn_ref, c_ref, s1_ref, s2_ref, qo_ref, ko_ref, dk_ref, dv_ref):
    tabs = (c_ref[...], s1_ref[...], s2_ref[...])
    is_ctx = jnp.logical_not(_latent_tile())
    for src, gain, dst in ((q_ref, qn_ref, qo_ref), (k_ref, kn_ref, ko_ref)):
        for h in range(DIFF_HEADS):
            sl = slice(h * LANES, (h + 1) * LANES)
            y = src[:, sl]
            y = y * lax.rsqrt(_seg64_sum(y * y) * (1.0 / DIFF_DH) + EPS) * gain[...]
            _store_rotated(dst, sl, y, tabs, _latent_tile())
            if dst is ko_ref:
                @pl.when(is_ctx)
                def _():
                    dk_ref[:, h, 0, :] = y[:, :DIFF_DH]
                    dk_ref[:, h, 1, :] = pltpu.roll(y, DIFF_DH, axis=1)[:, :DIFF_DH]
                    dv_ref[:, h, :] = v_ref[:, sl]


def _diff_qk(p, qn_p, kn_p, tabs):
    tm = _ROPE_TM
    n_ctx = N_CTX // tm
    tab_spec = pl.BlockSpec((tm, LANES), lambda i: (_rope_block(i), 0))
    ctx_blk = lambda i: jnp.minimum(i, n_ctx - 1)
    return pl.pallas_call(
        _diff_qk_kernel,
        grid=(N_TOK // tm,),
        in_specs=[pl.BlockSpec((tm, DIFF_W), lambda i: (i, 0)),
                  pl.BlockSpec((tm, DIFF_W), lambda i: (i, 1)),
                  pl.BlockSpec((tm, DIFF_W), lambda i: (i, 2)),
                  pl.BlockSpec((1, LANES), lambda i: (0, 0)),
                  pl.BlockSpec((1, LANES), lambda i: (0, 0)),
                  tab_spec, tab_spec, tab_spec],
        out_specs=[pl.BlockSpec((tm, DIFF_W), lambda i: (i, 0)),
                   pl.BlockSpec((tm, DIFF_W), lambda i: (i, 0)),
                   pl.BlockSpec((tm, DIFF_HEADS, 2, DIFF_DH), lambda i: (ctx_blk(i), 0, 0, 0)),
                   pl.BlockSpec((tm, DIFF_HEADS, 2 * DIFF_DH), lambda i: (ctx_blk(i), 0, 0))],
        out_shape=[jax.ShapeDtypeStruct((N_TOK, DIFF_W), F32)] * 2
        + [jax.ShapeDtypeStruct((N_CTX, DIFF_HEADS, 2, DIFF_DH), F32),
           jax.ShapeDtypeStruct((N_CTX, DIFF_HEADS, 2 * DIFF_DH), F32)],
        compiler_params=_params("arbitrary"),
        name="diff_qk",
    )(p, p, p, qn_p, kn_p, *tabs)


def _diff_attn_kernel(*refs, lam_init, heads):
    lam_ref, q_ref, gn_ref, o_ref = refs[0], refs[1], refs[-2], refs[-1]
    sources = [refs[2:4]] + ([refs[4:6]] if len(refs) == 8 else [])
    lv = lam_ref[...]
    lam = (jnp.exp(jnp.sum(lv[0:1] * lv[1:2], axis=-1, keepdims=True))
           - jnp.exp(jnp.sum(lv[2:3] * lv[3:4], axis=-1, keepdims=True)) + lam_init)
    scale = DIFF_DH ** -0.5
    lo = _lane_lo((1, LANES))
    for h in range(heads):
        sl = slice(h * LANES, (h + 1) * LANES)
        q = q_ref[:, sl]
        kbs = [k_ref[:, sl].astype(BF16) for k_ref, _ in sources]
        ps1, l1 = _softmax_parts([_dot_nt_bf16(jnp.where(lo, q, 0.0), kb) for kb in kbs], scale)
        ps2, l2 = _softmax_parts([_dot_nt_bf16(jnp.where(lo, 0.0, q), kb) for kb in kbs], scale)
        o = sum(_dot_bf16(p1 / l1 - lam * (p2 / l2), v_ref[:, sl])
                for p1, p2, (_, v_ref) in zip(ps1, ps2, sources))
        o_ref[:, sl] = (_rms(o, 2 * DIFF_DH) * gn_ref[:, sl] * (1.0 - lam_init)).astype(o_ref.dtype)


def _diff_attn(lam, q, k, v, v_blk0, cache, gn, batch, n, row0, tq, lam_init, heads):
    nqb = n // tq
    qb0 = row0 // tq
    kb0 = row0 // n
    wide = LANES * heads
    vb0 = v_blk0 // heads
    in_specs = [pl.BlockSpec((4, DIFF_DH), lambda b, h, i: (0, 0)),
                pl.BlockSpec((tq, wide), lambda b, h, i: (qb0 + b * nqb + i, h)),
                pl.BlockSpec((n, wide), lambda b, h, i: (kb0 + b, h)),
                pl.BlockSpec((n, wide), lambda b, h, i: (kb0 + b, vb0 + h))]
    operands = [lam, q, k, v]
    if cache is not None:
        in_specs += [pl.BlockSpec((PAST_LEN, wide), lambda b, h, i: (b, h))] * 2
        operands += list(cache)
    return pl.pallas_call(
        functools.partial(_diff_attn_kernel, lam_init=lam_init, heads=heads),
        grid=(batch, DIFF_HEADS // heads, nqb),
        in_specs=in_specs + [pl.BlockSpec((1, wide), lambda b, h, i: (0, h))],
        out_specs=pl.BlockSpec((tq, wide), lambda b, h, i: (b * nqb + i, h)),
        out_shape=jax.ShapeDtypeStruct((batch * n, DIFF_W), BF16),
        compiler_params=_params("parallel", "parallel", "arbitrary"),
        name="diff_attn",
    )(*operands, gn)


def _seg64_sum_wide(x):
    return jnp.concatenate([_seg64_sum(x[:, j * LANES:(j + 1) * LANES]) for j in range(x.shape[1] // LANES)], axis=1)


_SCAN_SLOTS = 3
_SCAN_DIR_NJ = (_SCAN_SLOTS - 1) * RWKV_HEADS
_SCAN_NJ = 2 * _SCAN_DIR_NJ + RWKV_HEADS


def _rwkv_pre_kernel(r_ref, k_ref, v_ref, lo_ref, wup_ref, aup_ref, gup_ref, w0_ref, a0_ref, kk_ref, ka_ref, rk_ref,
                     op_ref, g_ref, bonus_ref):
    W = RWKV_W
    col = lambda q: slice(q * W, (q + 1) * W)
    r = r_ref[...]
    k = k_ref[...]
    v = v_ref[...]
    lora = lo_ref[...]
    kk = k * kk_ref[...]
    kkn = kk * lax.rsqrt(_seg64_sum_wide(kk * kk) + EPS)
    g_ref[...] = _dot_bf16(_sigmoid(lora[:, 2 * LANES:3 * LANES]), gup_ref[...])
    pre = w0_ref[...] + _dot_bf16(jnp.tanh(lora[:, 0:LANES]), wup_ref[...])
    decay = jnp.exp(-jnp.exp(-_softplus(-pre) - 0.5))
    a = _sigmoid(a0_ref[...] + _dot_bf16(lora[:, LANES:2 * LANES], aup_ref[...]))
    lo = _lane_lo((1, LANES))

    def store_slot(j0, x1, x2):
        for h in range(RWKV_HEADS):
            blk = slice((h // 2) * LANES, (h // 2 + 1) * LANES)
            if h % 2 == 0:
                out = jnp.where(lo, x1[:, blk], pltpu.roll(x2[:, blk], RWKV_HS, axis=1))
            else:
                out = jnp.where(lo, pltpu.roll(x1[:, blk], RWKV_HS, axis=1), x2[:, blk])
            op_ref[:, j0 + h, :] = out

    bonus = None
    for d in range(2):
        a_d = a[:, col(d)]
        k_d = k * (1.0 + (a_d - 1.0) * ka_ref[...])
        t = _seg64_sum_wide(r * k_d * rk_ref[...])
        bonus = t if bonus is None else bonus + t
        store_slot(d * _SCAN_DIR_NJ, kkn, decay[:, col(d)])
        store_slot(d * _SCAN_DIR_NJ + RWKV_HEADS, k_d, kkn * a_d)
    store_slot(2 * _SCAN_DIR_NJ, r, v)
    bonus_ref[...] = bonus


def _rwkv_pre(p, r_blk, k_blk, v_blk, lo_blk, wup_bd, aup_bd, gup, w0, a0, k_k, k_a, r_k):
    tm = 256
    w = RWKV_W
    row = lambda n: pl.BlockSpec((1, n), lambda i: (0, 0))
    full = lambda a, b: pl.BlockSpec((a, b), lambda i: (0, 0))
    return pl.pallas_call(
        _rwkv_pre_kernel,
        grid=(N_TOK // tm,),
        in_specs=[pl.BlockSpec((tm, w), lambda i: (i, r_blk)),
                  pl.BlockSpec((tm, w), lambda i: (i, k_blk)),
                  pl.BlockSpec((tm, w), lambda i: (i, v_blk)),
                  pl.BlockSpec((tm, 3 * LANES), lambda i: (i, lo_blk)),
                  full(LANES, 2 * w), full(LANES, 2 * w), full(LANES, w),
                  row(2 * w), row(2 * w), row(w), row(w), row(w)],
        out_specs=[pl.BlockSpec((tm, _SCAN_NJ, LANES), lambda i: (i, 0, 0)),
                   pl.BlockSpec((tm, w), lambda i: (i, 0)), pl.BlockSpec((tm, w), lambda i: (i, 0))],
        out_shape=[jax.ShapeDtypeStruct((N_TOK, _SCAN_NJ, LANES), F32),
                   jax.ShapeDtypeStruct((N_TOK, w), F32), jax.ShapeDtypeStruct((N_TOK, w), F32)],
        compiler_params=_params("parallel"),
        name="rwkv_pre",
    )(p, p, p, p, wup_bd, aup_bd, gup, w0, a0, k_k, k_a, r_k)


_SCAN_CHUNK = 32
_SCAN_UNROLL = 64


def _rwkv_first_sa(s_ref, sa_ref, kk, n_k):
    nv = s_ref.shape[1]
    chunk = min(_SCAN_CHUNK, nv)
    for c0 in range(0, nv, chunk):
        def body(k, acc):
            return acc + s_ref[k, c0:c0 + chunk, :] * kk(k)
        sa_ref[c0:c0 + chunk, :] = lax.fori_loop(0, n_k, body, jnp.zeros((chunk, LANES), F32), unroll=_SCAN_UNROLL)


def _rwkv_step(s_ref, sa_ref, kk_next, w, kd, b, r, v_at, n_k):
    nv = s_ref.shape[1]
    chunk = min(_SCAN_CHUNK, nv)
    ys = []
    for c0 in range(0, nv, chunk):
        sa = sa_ref[c0:c0 + chunk, :]
        vc = v_at(c0, chunk)

        def body(k, acc):
            y_acc, sa_acc = acc
            s_new = s_ref[k, c0:c0 + chunk, :] * w(k) - sa * b(k) + vc * kd(k)
            s_ref[k, c0:c0 + chunk, :] = s_new
            return y_acc + s_new * r(k), sa_acc + s_new * kk_next(k)

        zero = jnp.zeros((chunk, LANES), F32)
        y_acc, sa_acc = lax.fori_loop(0, n_k, body, (zero, zero), unroll=_SCAN_UNROLL)
        sa_ref[c0:c0 + chunk, :] = sa_acc
        ys.append(y_acc)
    return ys[0] if len(ys) == 1 else jnp.concatenate(ys, axis=0)


_CTX_TB = 32


def _load_scan_operands(kt_ref, t, slabs):
    for s, x in enumerate(slabs):
        xt = x.T
        kt_ref[t, 2 * s] = xt[:RWKV_HS]
        kt_ref[t, 2 * s + 1] = xt[RWKV_HS:]


def _rwkv_scan_ctx_kernel(xd_ref, xs_ref, y_ref, st_ref, s_ref, sa_ref, kt_ref, ys_ref):
    d = pl.program_id(0)
    tb = pl.program_id(1)
    nh = RWKV_HEADS
    step_t = lambda i: jnp.where(d == 0, i, _CTX_TB - 1 - i)

    @pl.when(tb == 0)
    def _():
        s_ref[...] = jnp.zeros_like(s_ref)

    def load_t(t, carry):
        slabs = [jnp.concatenate([xd_ref[b, t, s * nh:(s + 1) * nh, :] for b in range(BATCH)], axis=0)
                 for s in range(_SCAN_SLOTS - 1)]
        slabs.append(jnp.concatenate([xs_ref[b, t] for b in range(BATCH)], axis=0))
        _load_scan_operands(kt_ref, t, slabs)
        return carry

    lax.fori_loop(0, _CTX_TB, load_t, 0, unroll=4)

    t0 = step_t(0)
    _rwkv_first_sa(s_ref, sa_ref, lambda k: kt_ref[t0, 0, pl.ds(k, 1), :], RWKV_HS)

    def step(i, carry):
        t = step_t(i)
        tn = step_t(jnp.minimum(i + 1, _CTX_TB - 1))
        row = lambda q, tt: (lambda k: kt_ref[tt, q, pl.ds(k, 1), :])
        ys_ref[t] = _rwkv_step(s_ref, sa_ref, row(0, tn), row(1, t), row(2, t), row(3, t), row(4, t),
                               lambda c0, n: kt_ref[t, 5, pl.ds(c0, n), :], RWKV_HS)
        return carry

    lax.fori_loop(0, _CTX_TB, step, 0)

    def store_t(i, carry):
        z = jnp.concatenate([ys_ref[2 * i], ys_ref[2 * i + 1]], axis=0).T
        z_odd = pltpu.roll(z, RWKV_HS, axis=1)
        for b in range(BATCH):
            y_ref[b, 2 * i] = z[b * nh:(b + 1) * nh, :RWKV_HS]
            y_ref[b, 2 * i + 1] = z_odd[b * nh:(b + 1) * nh, :RWKV_HS]
        return carry

    lax.fori_loop(0, _CTX_TB // 2, store_t, 0, unroll=4)

    @pl.when(tb == pl.num_programs(1) - 1)
    def _():
        st_ref[0] = s_ref[...]


def _rwkv_scan_ctx(op4):
    nt = SEQ // _CTX_TB
    hs = RWKV_HS
    tblk = lambda d, tb: jnp.where(d == 0, tb, nt - 1 - tb)
    return pl.pallas_call(
        _rwkv_scan_ctx_kernel,
        grid=(2, nt),
        in_specs=[pl.BlockSpec((BATCH, _CTX_TB, _SCAN_DIR_NJ, LANES), lambda d, tb: (0, tblk(d, tb), d, 0)),
                  pl.BlockSpec((BATCH, _CTX_TB, RWKV_HEADS, LANES),
                               lambda d, tb: (0, tblk(d, tb), 2 * _SCAN_DIR_NJ // RWKV_HEADS, 0))],
        out_specs=[pl.BlockSpec((BATCH, _CTX_TB, RWKV_HEADS, hs), lambda d, tb: (0, tblk(d, tb), d, 0)),
                   pl.BlockSpec((1, hs, hs, LANES), lambda d, tb: (d, 0, 0, 0))],
        out_shape=[jax.ShapeDtypeStruct((BATCH, SEQ, 2 * RWKV_HEADS, hs), F32),
                   jax.ShapeDtypeStruct((2, hs, hs, LANES), F32)],
        scratch_shapes=[pltpu.VMEM((hs, hs, LANES), F32), pltpu.VMEM((hs, LANES), F32),
                        pltpu.VMEM((_CTX_TB, 2 * _SCAN_SLOTS, hs, LANES), F32),
                        pltpu.VMEM((_CTX_TB, hs, LANES), F32)],
        compiler_params=_params("parallel", "arbitrary"),
        name="rwkv_scan_ctx",
    )(op4, op4)


_LAT_TB = 64
_LAT_VSPLIT = 4
_LAT_STATES = 2 * DEC_BATCH * RWKV_HEADS
_LAT_VROWS = RWKV_HS // _LAT_VSPLIT


def _rwkv_scan_lat_kernel(xf0_ref, xf0s_ref, xf1_ref, xf1s_ref, xb0_ref, xb0s_ref, xb1_ref, xb1s_ref,
                          s0_ref, yf_ref, yb_ref, s_ref, sa_ref, kt_ref, v_ref, ys_ref):
    @pl.when(pl.program_id(0) == 0)
    def _():
        s_ref[...] = s0_ref[...]

    group = lax.broadcasted_iota(jnp.int32, (_LAT_VROWS, LANES), 1) // _LAT_STATES
    nh = RWKV_HEADS

    def load_t(t, carry):
        tr = _LAT_TB - 1 - t
        for s in range(_SCAN_SLOTS):
            if s < _SCAN_SLOTS - 1:
                heads = slice(s * nh, (s + 1) * nh)
                x = jnp.concatenate([xf0_ref[0, t, heads, :], xf1_ref[0, t, heads, :],
                                     xb0_ref[0, tr, heads, :], xb1_ref[0, tr, heads, :]], axis=0)
            else:
                x = jnp.concatenate([xf0s_ref[0, t], xf1s_ref[0, t], xb0s_ref[0, tr], xb1s_ref[0, tr]], axis=0)
            xt = jnp.concatenate([x] * _LAT_VSPLIT, axis=0).T
            kt_ref[t, 2 * s] = xt[:RWKV_HS]
            if s < _SCAN_SLOTS - 1:
                kt_ref[t, 2 * s + 1] = xt[RWKV_HS:]
            else:
                v = jnp.zeros((_LAT_VROWS, LANES), F32)
                for g in range(_LAT_VSPLIT):
                    r0 = RWKV_HS + g * _LAT_VROWS
                    v = jnp.where(group == g, xt[r0:r0 + _LAT_VROWS, :], v)
                v_ref[t] = v
        return carry

    lax.fori_loop(0, _LAT_TB, load_t, 0, unroll=8)

    _rwkv_first_sa(s_ref, sa_ref, lambda k: kt_ref[0, 0, pl.ds(k, 1), :], RWKV_HS)

    def step(t, carry):
        tn = jnp.minimum(t + 1, _LAT_TB - 1)
        row = lambda q, tt: (lambda k: kt_ref[tt, q, pl.ds(k, 1), :])
        ys_ref[t] = _rwkv_step(s_ref, sa_ref, row(0, tn), row(1, t), row(2, t), row(3, t), row(4, t),
                               lambda c0, n: v_ref[t, pl.ds(c0, n), :], RWKV_HS)
        return carry

    lax.fori_loop(0, _LAT_TB, step, 0)

    def store_t(i, carry):
        rows = [jnp.where(group == g, ys_ref[2 * i + j], 0.0) for j in range(2) for g in range(_LAT_VSPLIT)]
        z = jnp.concatenate(rows, axis=0).T
        y = (z[0:_LAT_STATES] + z[_LAT_STATES:2 * _LAT_STATES]
             + z[2 * _LAT_STATES:3 * _LAT_STATES] + z[3 * _LAT_STATES:4 * _LAT_STATES])
        y_odd = pltpu.roll(y, RWKV_HS, axis=1)
        half = _LAT_STATES // 2
        for b in range(DEC_BATCH):
            rows_f = slice(b * nh, (b + 1) * nh)
            rows_b = slice(half + b * nh, half + (b + 1) * nh)
            yf_ref[b, 2 * i] = y[rows_f, :RWKV_HS]
            yf_ref[b, 2 * i + 1] = y_odd[rows_f, :RWKV_HS]
            yb_ref[b, _LAT_TB - 1 - 2 * i] = y[rows_b, :RWKV_HS]
            yb_ref[b, _LAT_TB - 2 - 2 * i] = y_odd[rows_b, :RWKV_HS]
        return carry

    lax.fori_loop(0, _LAT_TB // 2, store_t, 0, unroll=4)


def _rwkv_scan_lat(op4, s0):
    hs = RWKV_HS
    nv = _LAT_VROWS
    nt = DEC_SEQ // _LAT_TB
    per_seq = SEQ // _LAT_TB
    first = N_CTX // SEQ

    def x_specs(b, d):
        tblk = (lambda tb: tb) if d == 0 else (lambda tb: nt - 1 - tb)
        chunk = lambda tb: first + b * (DEC_SEQ // SEQ) + tblk(tb) // per_seq
        return [pl.BlockSpec((1, _LAT_TB, _SCAN_DIR_NJ, LANES), lambda tb: (chunk(tb), tblk(tb) % per_seq, d, 0)),
                pl.BlockSpec((1, _LAT_TB, RWKV_HEADS, LANES),
                             lambda tb: (chunk(tb), tblk(tb) % per_seq, 2 * _SCAN_DIR_NJ // RWKV_HEADS, 0))]

    y_shape = jax.ShapeDtypeStruct((DEC_BATCH, DEC_SEQ, RWKV_HEADS, hs), F32)
    y_blk = (DEC_BATCH, _LAT_TB, RWKV_HEADS, hs)
    return pl.pallas_call(
        _rwkv_scan_lat_kernel,
        grid=(nt,),
        in_specs=x_specs(0, 0) + x_specs(1, 0) + x_specs(0, 1) + x_specs(1, 1)
        + [pl.BlockSpec((hs, nv, LANES), lambda tb: (0, 0, 0))],
        out_specs=[pl.BlockSpec(y_blk, lambda tb: (0, tb, 0, 0)),
                   pl.BlockSpec(y_blk, lambda tb: (0, nt - 1 - tb, 0, 0))],
        out_shape=[y_shape, y_shape],
        scratch_shapes=[pltpu.VMEM((hs, nv, LANES), F32), pltpu.VMEM((nv, LANES), F32),
                        pltpu.VMEM((_LAT_TB, 2 * _SCAN_SLOTS - 1, hs, LANES), F32),
                        pltpu.VMEM((_LAT_TB, nv, LANES), F32), pltpu.VMEM((_LAT_TB, nv, LANES), F32)],
        compiler_params=_params("arbitrary"),
        name="rwkv_scan_lat",
    )(*([op4] * 8), s0)


def _rwkv_post_kernel(yc_ref, ylf_ref, ylb_ref, bonus_ref, v_ref, g_ref, gn_ref, o_ref):
    def finish(head_sum):
        y = jnp.concatenate([head_sum(h) for h in range(RWKV_HEADS)], axis=1)
        y = y * lax.rsqrt(_seg64_sum_wide(y * y) * (1.0 / RWKV_HS) + EPS) * gn_ref[...]
        o_ref[...] = ((y + bonus_ref[...] * v_ref[...]) * g_ref[...]).astype(o_ref.dtype)

    @pl.when(pl.program_id(0) < N_CTX // SEQ)
    def _():
        finish(lambda h: yc_ref[0, :, h, :] + yc_ref[0, :, RWKV_HEADS + h, :])

    @pl.when(pl.program_id(0) >= N_CTX // SEQ)
    def _():
        finish(lambda h: ylf_ref[0, :, h, :] + ylb_ref[0, :, h, :])


def _rwkv_post(y_ctx, y_lat_f, y_lat_b, bonus, p, v_blk, g, gn):
    tm = SEQ
    w = RWKV_W
    hs = RWKV_HS
    n_ctx = N_CTX // SEQ
    lat = lambda y: y.reshape(N_LAT // SEQ, SEQ, RWKV_HEADS, hs)
    spec = pl.BlockSpec((tm, w), lambda i: (i, 0))
    lat_spec = pl.BlockSpec((1, SEQ, RWKV_HEADS, hs), lambda i: (jnp.maximum(i - n_ctx, 0), 0, 0, 0))
    return pl.pallas_call(
        _rwkv_post_kernel,
        grid=(N_TOK // tm,),
        in_specs=[pl.BlockSpec((1, SEQ, 2 * RWKV_HEADS, hs), lambda i: (jnp.minimum(i, n_ctx - 1), 0, 0, 0)),
                  lat_spec, lat_spec, spec, pl.BlockSpec((tm, w), lambda i: (i, v_blk)), spec,
                  pl.BlockSpec((1, w), lambda i: (0, 0))],
        out_specs=spec,
        out_shape=jax.ShapeDtypeStruct((N_TOK, w), BF16),
        compiler_params=_params("parallel"),
        name="rwkv_post",
    )(y_ctx, lat(y_lat_f), lat(y_lat_b), bonus, p, g, gn)


def _value_split_layout(x):
    lead = x.shape[:-2]
    n = len(lead)
    x = x.reshape(lead + (_LAT_STATES, _LAT_VSPLIT, _LAT_VROWS))
    return jnp.transpose(x, tuple(range(n)) + (n + 2, n + 1, n)).reshape(lead + (_LAT_VROWS, LANES))


_EVEN_HEAD = MLA_Q_RANK + MLA_KV_RANK
_EVEN_PACKED = _EVEN_HEAD + LANES + 2 * RET_HEADS * (RET_DK + RET_DV)


def _pack_even_weight_kernel(w_ref, o_ref):
    w = w_ref[...]
    o_ref[:, 0:_EVEN_HEAD] = w[:, 0:_EVEN_HEAD]
    o_ref[:, _EVEN_HEAD:_EVEN_HEAD + LANES] = jnp.zeros((w.shape[0], LANES), F32)
    o_ref[:, _EVEN_HEAD + MLA_NOPE:_EVEN_HEAD + MLA_QK] = w[:, _EVEN_HEAD:_EVEN_HEAD + MLA_ROPE]
    o_ref[:, _EVEN_HEAD + LANES:] = w[:, _EVEN_HEAD + MLA_ROPE:]


def _pack_even_weight(w_in):
    tm = 256
    return pl.pallas_call(
        _pack_even_weight_kernel,
        grid=(D_MODEL // tm,),
        in_specs=[pl.BlockSpec((tm, w_in.shape[1]), lambda i: (i, 0))],
        out_specs=pl.BlockSpec((tm, _EVEN_PACKED), lambda i: (i, 0)),
        out_shape=jax.ShapeDtypeStruct((D_MODEL, _EVEN_PACKED), F32),
        compiler_params=_params("parallel"),
        name="pack_even_weight",
    )(w_in)


def _even_layer(x, mod, g_mix, w_in, q_norm, kv_norm, w_uq, w_ukv, qn, kn, ret_decay, ret_gn,
                cache_ckv, cache_krope, state_ret, tabs_m):
    w_p = _pack_even_weight(w_in)
    p = _inproj(x, g_mix, mod, w_p, jnp.zeros((1, w_p.shape[1]), F32), 1024, None)
    CKV_BLK, KR_BLK, RQ_BLK, RK_BLK, RV_BLK, RG_BLK = 2, 3, 4, 6, 8, 12

    def head_pad(w, n_head, d_head, c0, c1):
        w = w.reshape(w.shape[0], n_head, d_head)[:, :, c0:c1]
        return jnp.pad(w, ((0, 0), (0, 0), (0, LANES - (c1 - c0)))).reshape(w.shape[0], n_head * LANES)

    w_uq_p = head_pad(w_uq, MLA_HEADS, MLA_QK, 0, MLA_QK)
    wk_p = head_pad(w_ukv, MLA_HEADS, MLA_NOPE + MLA_V, 0, MLA_NOPE)
    wv_p = head_pad(w_ukv, MLA_HEADS, MLA_NOPE + MLA_V, MLA_NOPE, MLA_NOPE + MLA_V)
    qn_p = jnp.pad(qn, (0, LANES - MLA_QK))[None]
    kn_p = jnp.pad(kn, (0, LANES - MLA_QK))[None]

    q = _mla_q(p, q_norm[None], w_uq_p, qn_p, tabs_m)
    k, v, ckvn = _mla_kv(p, CKV_BLK, p, KR_BLK, kv_norm[None], wk_p, wv_p, kn_p, tabs_m, N_TOK, True)

    n_c = DEC_BATCH * PAST_LEN
    kr_c = jnp.pad(cache_krope.reshape(n_c, MLA_ROPE), ((0, 0), (MLA_NOPE, LANES - MLA_QK)))
    k_c, v_c, _ = _mla_kv(cache_ckv.reshape(n_c, MLA_KV_RANK), 0, kr_c, 0, kv_norm[None], wk_p, wv_p, kn_p,
                          tabs_m, n_c, False)

    o_ctx = _mla_attn(q, k, v, None, BATCH, SEQ, 0, SEQ, MLA_HEADS // 2)
    o_lat = _mla_attn(q, k, v, (k_c, v_c), DEC_BATCH, DEC_SEQ, N_CTX, 256, 1)

    log_g = -_softplus(-ret_decay)
    gn = ret_gn[None]
    s0_ctx = jnp.zeros((BATCH, 2, RET_HEADS * RET_DK, RET_DV), F32)
    r_ctx, st_ctx = _retention(log_g, p, RQ_BLK, RK_BLK, RV_BLK, RG_BLK, s0_ctx, gn, BATCH, SEQ, 0, SEQ, True)
    s0_lat = state_ret.reshape(DEC_BATCH, 2, RET_HEADS * RET_DK, RET_DV)
    r_lat, _ = _retention(log_g, p, RQ_BLK, RK_BLK, RV_BLK, RG_BLK, s0_lat, gn, DEC_BATCH, DEC_SEQ, N_CTX, 256,
                          False)

    mix = [(o_ctx, o_lat), (r_ctx, r_lat)]
    new_ckv = ckvn[:N_CTX].reshape(BATCH, SEQ, MLA_KV_RANK)
    new_krope = p[:N_CTX, KR_BLK * LANES + MLA_NOPE:KR_BLK * LANES + MLA_QK]
    new_krope = new_krope.reshape(BATCH, SEQ, MLA_ROPE)
    new_ret = st_ctx.reshape(BATCH, 2, RET_HEADS, RET_DK, RET_DV)
    return mix, new_ckv, new_krope, new_ret


def _odd_layer(x, mod, g_mix, w_in, qn, kn, lam, diff_gn, mu, w0, w_up, a0, a_up, g_up, k_k, k_a, r_k, gn,
               cache_k, cache_v, state_rwkv, tabs_d, lam_init):
    w_p = w_in
    n_in = w_p.shape[1]
    mu_full = jnp.concatenate([jnp.zeros((3 * DIFF_W,), F32), mu])[None]
    p = _inproj(x, g_mix, mod, w_p, mu_full, 384, (3 * DIFF_W) // 384)
    DV_BLK, R_BLK, K_BLK, V_BLK = 2, 3, 4, 5
    LO_BLK = (6 * RWKV_W) // (3 * LANES)

    qn_p = jnp.tile(qn, 2)[None]
    kn_p = jnp.tile(kn, 2)[None]
    q, k, dk, dv = _diff_qk(p, qn_p, kn_p, tabs_d)

    n_c = DEC_BATCH * PAST_LEN
    cache = (cache_k.reshape(n_c, DIFF_W), cache_v.reshape(n_c, DIFF_W))
    dgn = diff_gn[None]
    o_ctx = _diff_attn(lam, q, k, p, DV_BLK * DIFF_HEADS, None, dgn, BATCH, SEQ, 0, SEQ, lam_init, DIFF_HEADS)
    o_lat = _diff_attn(lam, q, k, p, DV_BLK * DIFF_HEADS, cache, dgn, DEC_BATCH, DEC_SEQ, N_CTX, 256, lam_init, 1)

    zero = jnp.zeros((RWKV_W_LORA, RWKV_W), F32)
    wup_bd = jnp.concatenate([jnp.concatenate([w_up[0], zero], 1), jnp.concatenate([zero, w_up[1]], 1)], 0)
    aup_bd = jnp.concatenate([jnp.concatenate([a_up[0], zero], 1), jnp.concatenate([zero, a_up[1]], 1)], 0)
    op, g, bonus = _rwkv_pre(p, R_BLK, K_BLK, V_BLK, LO_BLK, wup_bd, aup_bd, g_up, w0.reshape(1, -1),
                             a0.reshape(1, -1), k_k[None], k_a[None], r_k.reshape(1, -1))
    op4 = op.reshape(N_TOK // SEQ, SEQ, _SCAN_NJ, LANES)
    y_ctx, st_ctx = _rwkv_scan_ctx(op4)
    s0_lat = jnp.transpose(state_rwkv, (4, 1, 0, 2, 3)).reshape(RWKV_HS, _LAT_STATES, RWKV_HS)
    y_lat_f, y_lat_b = _rwkv_scan_lat(op4, _value_split_layout(s0_lat))
    rw_o = _rwkv_post(y_ctx, y_lat_f, y_lat_b, bonus, p, V_BLK, g, gn[None])

    mix = [(o_ctx, o_lat), rw_o]
    new_dk = dk.reshape(BATCH, SEQ, DIFF_HEADS, 2, DIFF_DH)
    new_dv = dv.reshape(BATCH, SEQ, DIFF_HEADS, 2 * DIFF_DH)
    new_rwkv = jnp.transpose(st_ctx.reshape(2, RWKV_HS, RWKV_HS, BATCH, RWKV_HEADS), (3, 0, 4, 2, 1))
    return mix, new_dk, new_dv, new_rwkv


def kernel(x_prompt, x_sample, cache_mla_ckv, cache_mla_krope, state_ret, cache_diff_k, cache_diff_v, state_rwkv,
           c, c_ctx, ada_w, ada_b, norm_mix_g, norm_ffn_g, w_out, ffn_up, ffn_conv_w, ffn_conv_b, ffn_down,
           a_w_in, mla_q_norm, mla_kv_norm, mla_w_uq, mla_w_ukv, mla_qn, mla_kn, ret_decay, ret_gn,
           b_w_in, diff_qn, diff_kn, diff_lam, diff_gn, rwkv_mu, rwkv_w0, rwkv_w_up, rwkv_a0, rwkv_a_up,
           rwkv_g_up, rwkv_k_k, rwkv_k_a, rwkv_r_k, rwkv_gn):
    x = (x_prompt.reshape(N_CTX, D_MODEL), x_sample.reshape(N_LAT, D_MODEL))
    cond8 = jnp.pad(jnp.concatenate([c_ctx[None], c], 0), ((0, 8 - N_GROUPS), (0, 0)))
    mod = _modulation(cond8, ada_w, ada_b)

    tabs_m = _rope_tables(MLA_ROPE, (MLA_NOPE,))
    tabs_d = _rope_tables(DIFF_DH, (0, DIFF_DH))

    outs = {}
    for l in range(DEPTH):
        j = l // 2
        g_mix = norm_mix_g[l][None]
        if l % 2 == 0:
            mix, outs["ckv"], outs["krope"], outs["ret"] = _even_layer(
                x, mod[l], g_mix, a_w_in[j], mla_q_norm[j], mla_kv_norm[j], mla_w_uq[j], mla_w_ukv[j], mla_qn[j],
                mla_kn[j], ret_decay[j], ret_gn[j], cache_mla_ckv[:, j], cache_mla_krope[:, j], state_ret[:, j],
                tabs_m)
        else:
            lam_init = 0.8 - 0.6 * math.exp(-0.3 * l)
            mix, outs["dk"], outs["dv"], outs["rwkv"] = _odd_layer(
                x, mod[l], g_mix, b_w_in[j], diff_qn[j], diff_kn[j], diff_lam[j], diff_gn[j], rwkv_mu[j],
                rwkv_w0[j], rwkv_w_up[j], rwkv_a0[j], rwkv_a_up[j], rwkv_g_up[j], rwkv_k_k[j], rwkv_k_a[j],
                rwkv_r_k[j], rwkv_gn[j], cache_diff_k[:, j], cache_diff_v[:, j], state_rwkv[:, j], tabs_d, lam_init)
        x = _resid_proj(mix, w_out, l, x, mod[l], 2)
        act = _ffn_up(x, norm_ffn_g[l][None], mod[l], ffn_up, l, ffn_conv_w[l], ffn_conv_b[l])
        x = _resid_proj([act], ffn_down, l, x, mod[l], 5, split_out=(l == DEPTH - 1))

    y_prompt = x[0].reshape(BATCH, SEQ, D_MODEL)
    y_sample = x[1].reshape(DEC_BATCH, DEC_SEQ, D_MODEL)
    return (y_prompt, y_sample, outs["ckv"][:, None], outs["krope"][:, None], outs["ret"][:, None],
            outs["dk"][:, None], outs["dv"][:, None], outs["rwkv"][:, None])
```

```python
import functools
import math

import numpy as np
import jax
import jax.numpy as jnp
from jax import lax
from jax.experimental import pallas as pl
from jax.experimental.pallas import tpu as pltpu

D_MODEL = 1024
BATCH = 16
SEQ = 256
DEPTH = 2
DEC_BATCH = 2
DEC_SEQ = 1024
PAST_LEN = 512
GRID_W = 64
EPS = 1e-6
ROPE_BASE = 10000.0

MLA_HEADS = 8
MLA_Q_RANK = 256
MLA_KV_RANK = 128
MLA_NOPE = 64
MLA_ROPE = 32
MLA_V = 64
MLA_QK = MLA_NOPE + MLA_ROPE
RET_HEADS = 4
RET_DK = 64
RET_DV = 128
DIFF_HEADS = 4
DIFF_DH = 64
DIFF_W = DIFF_HEADS * 2 * DIFF_DH
RWKV_HEADS = 8
RWKV_HS = 64
RWKV_W = RWKV_HEADS * RWKV_HS
RWKV_W_LORA = 64
D_FF = 2816

N_CTX = BATCH * SEQ
N_LAT = DEC_BATCH * DEC_SEQ
N_TOK = N_CTX + N_LAT
N_GROUPS = 1 + DEC_BATCH

LANES = 128
VMEM_LIMIT = 56 * 1024 * 1024

_PREC = lax.Precision.HIGHEST
F32 = jnp.float32


def _dot_tn(a, b):
    return lax.dot_general(a, b, (((0,), (0,)), ((), ())), precision=_PREC, preferred_element_type=F32)


BF16 = jnp.bfloat16


def _dot_bf16(a, b):
    return jnp.dot(a.astype(BF16), b.astype(BF16), preferred_element_type=F32)


def _dot_nt_bf16(a, b):
    return lax.dot_general(a.astype(BF16), b.astype(BF16), (((1,), (1,)), ((), ())), preferred_element_type=F32)


def _params(*sem):
    return pltpu.CompilerParams(dimension_semantics=sem, vmem_limit_bytes=VMEM_LIMIT)


def _sigmoid(x):
    return 1.0 / (1.0 + jnp.exp(-x))


def _silu(x):
    return x * _sigmoid(x)


def _softplus(x):
    return jnp.maximum(x, 0.0) + jnp.log(1.0 + jnp.exp(-jnp.abs(x)))


def _rms(x, n):
    return x * lax.rsqrt(jnp.sum(x * x, axis=-1, keepdims=True) * (1.0 / n) + EPS)


def _lane_lo(shape):
    return lax.broadcasted_iota(jnp.int32, shape, len(shape) - 1) < 64


def _seg64_sum(x):
    lo = _lane_lo(x.shape)
    s_lo = jnp.sum(jnp.where(lo, x, 0.0), axis=-1, keepdims=True)
    s_hi = jnp.sum(jnp.where(lo, 0.0, x), axis=-1, keepdims=True)
    return jnp.where(lo, s_lo, s_hi)


_SUBLANES = 8


def _seq_neighbours(p, tile, tile_rows):
    is_ctx = tile * tile_rows < N_CTX
    sub = lax.broadcasted_iota(jnp.int32, (_SUBLANES, 1), 0)

    def shifted(rolled, edge_sublane, group_of_seq):
        pieces, start = [], 0
        for q in range(tile_rows // SEQ):
            g0 = q * SEQ + group_of_seq
            outer = (q == 0) if group_of_seq == 0 else (q == tile_rows // SEQ - 1)
            edge = (sub == edge_sublane) if outer else ((sub == edge_sublane) & is_ctx)
            pieces += [rolled[start:g0], jnp.where(edge, 0.0, rolled[g0:g0 + _SUBLANES])]
            start = g0 + _SUBLANES
        pieces.append(rolled[start:])
        return jnp.concatenate([x for x in pieces if x.shape[0]], axis=0)

    prev = shifted(pltpu.roll(p, 1, axis=0), 0, 0)
    nxt = shifted(pltpu.roll(p, tile_rows - 1, axis=0), _SUBLANES - 1, SEQ - _SUBLANES)
    return prev, nxt


def _group_of_tile(i, tile_rows):
    row = i * tile_rows
    return jnp.where(row < N_CTX, 0, 1 + (row - N_CTX) // DEC_SEQ)


def _modulation_kernel(c_ref, w_ref, b_ref, o_ref):
    o_ref[0] = _dot_bf16(_silu(c_ref[...]), w_ref[0]) + b_ref[0]


def _modulation(cond8, ada_w, ada_b):
    tn = 512
    n = 6 * D_MODEL
    out = pl.pallas_call(
        _modulation_kernel,
        grid=(DEPTH, n // tn),
        in_specs=[pl.BlockSpec((8, D_MODEL), lambda l, j: (0, 0)),
                  pl.BlockSpec((1, D_MODEL, tn), lambda l, j: (l, 0, j)),
                  pl.BlockSpec((1, 1, tn), lambda l, j: (l, 0, j))],
        out_specs=pl.BlockSpec((1, 8, tn), lambda l, j: (l, 0, j)),
        out_shape=jax.ShapeDtypeStruct((DEPTH, 8, n), F32),
        compiler_params=_params("parallel", "parallel"),
        name="modulation",
    )(cond8, ada_w, ada_b.reshape(DEPTH, 1, n))
    m = out[:, :N_GROUPS].reshape(DEPTH, N_GROUPS, 6, D_MODEL)
    return jnp.pad(m, ((0, 0), (0, 0), (0, 2), (0, 0)))


_TM_SEQ = 1024
_NORM_ROWS = 256


def _norm_mod(x, g, mod, off):
    return _rms(x, D_MODEL) * g * (1.0 + mod[off + 1:off + 2, :]) + mod[off:off + 1, :]


def _row_split_specs(tm, width_block, col):
    n_ctx = N_CTX // tm
    return [pl.BlockSpec((tm, width_block), lambda i, j: (jnp.minimum(i, n_ctx - 1), col(j))),
            pl.BlockSpec((tm, width_block), lambda i, j: (jnp.maximum(i - n_ctx, 0), col(j)))]


def _for_row_tile(i, tm, ctx_ref, lat_ref, fn):
    @pl.when(i < N_CTX // tm)
    def _():
        fn(ctx_ref)

    @pl.when(i >= N_CTX // tm)
    def _():
        fn(lat_ref)


def _inproj_kernel(*refs, shift_from, split_x):
    x_refs, (g_ref, mod_ref, w_ref, mu_ref, o_ref, h_ref) = refs[:-6], refs[-6:]
    i = pl.program_id(0)
    j = pl.program_id(1)

    @pl.when(j == 0)
    def _():
        def norm_dot(x_ref):
            for r0 in range(0, _TM_SEQ, _NORM_ROWS):
                rows = pl.ds(r0, _NORM_ROWS)
                h = _norm_mod(x_ref[rows, :], g_ref[...], mod_ref[0], 0).astype(BF16)
                h_ref[rows, :] = h
                o_ref[rows, :] = _dot_bf16(h, w_ref[...])

        if split_x:
            _for_row_tile(i, _TM_SEQ, x_refs[0], x_refs[1], norm_dot)
        else:
            norm_dot(x_refs[0])

    @pl.when(j > 0)
    def _():
        p = _dot_bf16(h_ref[...], w_ref[...])
        if shift_from is None:
            o_ref[...] = p
        else:
            @pl.when(j < shift_from)
            def _():
                o_ref[...] = p

            @pl.when(j >= shift_from)
            def _():
                prev, nxt = _seq_neighbours(p, i, _TM_SEQ)
                o_ref[...] = p + (0.5 * (prev + nxt) - p) * mu_ref[...]


def _inproj(x, g, mod, w, mu, tn, shift_from):
    n = w.shape[1]
    tm = _TM_SEQ
    split_x = isinstance(x, (tuple, list))
    x_specs = _row_split_specs(tm, D_MODEL, lambda j: 0) if split_x else [pl.BlockSpec((tm, D_MODEL), lambda i, j: (i, 0))]
    return pl.pallas_call(
        functools.partial(_inproj_kernel, shift_from=shift_from, split_x=split_x),
        grid=(N_TOK // tm, n // tn),
        in_specs=x_specs
        + [pl.BlockSpec((1, D_MODEL), lambda i, j: (0, 0)),
                  pl.BlockSpec((1, 8, D_MODEL), lambda i, j: (_group_of_tile(i, tm), 0, 0)),
                  pl.BlockSpec((D_MODEL, tn), lambda i, j: (0, j)),
                  pl.BlockSpec((1, tn), lambda i, j: (0, j))],
        out_specs=pl.BlockSpec((tm, tn), lambda i, j: (i, j)),
        out_shape=jax.ShapeDtypeStruct((N_TOK, n), F32),
        scratch_shapes=[pltpu.VMEM((tm, D_MODEL), BF16)],
        compiler_params=_params("parallel", "arbitrary"),
        name="inproj" if shift_from is None else "inproj_shift",
    )(*(x if split_x else [x]), g, mod, w, mu)


def _resid_kernel(*refs, gate_row, widths, split, split_x, split_out):
    n_in = sum(2 if sp else 1 for sp in split)
    a_refs = refs[:n_in]
    w_ref = refs[n_in]
    n_x = 2 if split_x else 1
    x_refs = refs[n_in + 1:n_in + 1 + n_x]
    mod_ref = refs[n_in + 1 + n_x]
    o_ref = refs[n_in + 2 + n_x:-1] if split_out else refs[n_in + 2 + n_x]
    a_bf_ref = refs[-1]
    i = pl.program_id(0)

    @pl.when(pl.program_id(1) == 0)
    def _():
        k0, r = 0, 0
        for width, sp in zip(widths, split):
            cols = slice(k0, k0 + width)
            if sp:
                ctx_ref, lat_ref = a_refs[r], a_refs[r + 1]

                @pl.when(i < N_CTX // _TM_SEQ)
                def _():
                    a_bf_ref[:, cols] = ctx_ref[...].astype(BF16)

                @pl.when(i >= N_CTX // _TM_SEQ)
                def _():
                    a_bf_ref[:, cols] = lat_ref[...].astype(BF16)
            else:
                a_bf_ref[:, cols] = a_refs[r][...].astype(BF16)
            k0 += width
            r += 2 if sp else 1

    d = mod_ref[0, gate_row:gate_row + 1, :] * _dot_bf16(a_bf_ref[...], w_ref[0])
    if split_x:
        def add(x_ref):
            o_ref[...] = x_ref[...] + d

        _for_row_tile(i, _TM_SEQ, x_refs[0], x_refs[1], add)
        return
    y = x_refs[0][...] + d
    if not split_out:
        o_ref[...] = y
    else:
        ctx_o_ref, lat_o_ref = o_ref

        @pl.when(i < N_CTX // _TM_SEQ)
        def _():
            ctx_o_ref[...] = y

        @pl.when(i >= N_CTX // _TM_SEQ)
        def _():
            lat_o_ref[...] = y


def _resid_proj(acts, w, layer, x, mod, gate_row, split_out=False):
    tm = _TM_SEQ
    split_x = isinstance(x, (tuple, list))
    assert not (split_x and split_out)
    n_ctx = N_CTX // tm
    split = [isinstance(a, (tuple, list)) for a in acts]
    widths = [a[0].shape[1] if sp else a.shape[1] for a, sp in zip(acts, split)]
    k = sum(widths)
    tn = 512 if k <= D_MODEL else 256
    nj = D_MODEL // tn
    in_specs, operands = [], []
    for a, width, sp in zip(acts, widths, split):
        if sp:
            in_specs += [pl.BlockSpec((tm, width), lambda i, j: (jnp.minimum(i, n_ctx - 1), 0)),
                         pl.BlockSpec((tm, width), lambda i, j: (jnp.maximum(i - n_ctx, 0), 0))]
            operands += list(a)
        else:
            in_specs.append(pl.BlockSpec((tm, width), lambda i, j: (i, 0)))
            operands.append(a)
    if split_out:
        out_specs = [pl.BlockSpec((tm, tn), lambda i, j: (jnp.minimum(i, n_ctx - 1), jnp.where(i < n_ctx, j, nj - 1))),
                     pl.BlockSpec((tm, tn), lambda i, j: (jnp.maximum(i - n_ctx, 0), jnp.where(i < n_ctx, 0, j)))]
        out_shape = [jax.ShapeDtypeStruct((N_CTX, D_MODEL), F32), jax.ShapeDtypeStruct((N_LAT, D_MODEL), F32)]
        sem = ("arbitrary", "arbitrary")
    else:
        out_specs = pl.BlockSpec((tm, tn), lambda i, j: (i, j))
        out_shape = jax.ShapeDtypeStruct((N_TOK, D_MODEL), F32)
        sem = ("parallel", "arbitrary")
    return pl.pallas_call(
        functools.partial(_resid_kernel, gate_row=gate_row, widths=tuple(widths), split=tuple(split),
                          split_x=split_x, split_out=split_out),
        grid=(N_TOK // tm, nj),
        in_specs=in_specs
        + [pl.BlockSpec((1, k, tn), lambda i, j: (layer, 0, j))]
        + (_row_split_specs(tm, tn, lambda j: j) if split_x else [pl.BlockSpec((tm, tn), lambda i, j: (i, j))])
        + [pl.BlockSpec((1, 8, tn), lambda i, j: (_group_of_tile(i, tm), 0, j))],
        out_specs=out_specs,
        out_shape=out_shape,
        scratch_shapes=[pltpu.VMEM((tm, k), BF16)],
        compiler_params=_params(*sem),
        name="resid_proj",
    )(*operands, w, *(x if split_x else [x]), mod)


def _ffn_up_kernel(x_ref, g_ref, mod_ref, wa_ref, wb_ref, cwa_ref, cwb_ref, cba_ref, cbb_ref, o_ref, h_ref):
    i = pl.program_id(0)

    @pl.when(pl.program_id(1) == 0)
    def _():
        h_ref[...] = _norm_mod(x_ref[...], g_ref[...], mod_ref[0], 3).astype(BF16)

    h = h_ref[...]

    def conv(w_ref, cw_ref, cb_ref):
        u = _dot_bf16(h, w_ref[0])
        prev, nxt = _seq_neighbours(u, i, _TM_SEQ)
        return prev * cw_ref[0:1, :] + u * cw_ref[1:2, :] + nxt * cw_ref[2:3, :] + cb_ref[...]

    o_ref[...] = (_silu(conv(wa_ref, cwa_ref, cba_ref)) * conv(wb_ref, cwb_ref, cbb_ref)).astype(o_ref.dtype)


def _ffn_up(x, g, mod, up, layer, cw, cb):
    tm, tn = _TM_SEQ, 256
    nb = D_FF // tn
    cb = cb.reshape(1, 2 * D_FF)
    return pl.pallas_call(
        _ffn_up_kernel,
        grid=(N_TOK // tm, nb),
        in_specs=[pl.BlockSpec((tm, D_MODEL), lambda i, j: (i, 0)),
                  pl.BlockSpec((1, D_MODEL), lambda i, j: (0, 0)),
                  pl.BlockSpec((1, 8, D_MODEL), lambda i, j: (_group_of_tile(i, tm), 0, 0)),
                  pl.BlockSpec((1, D_MODEL, tn), lambda i, j: (layer, 0, j)),
                  pl.BlockSpec((1, D_MODEL, tn), lambda i, j: (layer, 0, j + nb)),
                  pl.BlockSpec((3, tn), lambda i, j: (0, j)),
                  pl.BlockSpec((3, tn), lambda i, j: (0, j + nb)),
                  pl.BlockSpec((1, tn), lambda i, j: (0, j)),
                  pl.BlockSpec((1, tn), lambda i, j: (0, j + nb))],
        out_specs=pl.BlockSpec((tm, tn), lambda i, j: (i, j)),
        out_shape=jax.ShapeDtypeStruct((N_TOK, D_FF), BF16),
        scratch_shapes=[pltpu.VMEM((tm, D_MODEL), BF16)],
        compiler_params=_params("parallel", "arbitrary"),
        name="ffn_up",
    )(x, g, mod, up, up, cw, cw, cb, cb)


_ROPE_TM = 512


def _rope(y, c, s1, s2):
    return y * c + pltpu.roll(y, 1, axis=1) * s1 + pltpu.roll(y, LANES - 1, axis=1) * s2


def _rope_tables(rot_dim, lane_offsets):
    t = np.arange(DEC_SEQ)
    row, col = t // GRID_W, t % GRID_W
    n_freq = rot_dim // 4
    inv = ROPE_BASE ** (-np.arange(n_freq, dtype=np.float64) / n_freq)
    ang = np.concatenate([row[:, None] * inv, col[:, None] * inv], -1)
    cos, sin = np.cos(ang), np.sin(ang)
    n = _ROPE_TM + DEC_SEQ
    c, s1, s2 = np.ones((n, LANES)), np.zeros((n, LANES)), np.zeros((n, LANES))
    for a in lane_offsets:
        even = a + 2 * np.arange(rot_dim // 2)
        c[_ROPE_TM:, even] = cos
        c[_ROPE_TM:, even + 1] = cos
        s1[_ROPE_TM:, even + 1] = sin
        s2[_ROPE_TM:, even] = -sin
    return tuple(jnp.asarray(x, F32) for x in (c, s1, s2))


def _store_rotated(dst_ref, sl, y, tabs, rotate):
    if rotate is False:
        dst_ref[:, sl] = y
        return

    @pl.when(rotate)
    def _():
        dst_ref[:, sl] = _rope(y, *tabs)

    @pl.when(jnp.logical_not(rotate))
    def _():
        dst_ref[:, sl] = y


def _latent_tile():
    return pl.program_id(0) >= N_CTX // _ROPE_TM


def _rope_block(i):
    row = i * _ROPE_TM
    return jnp.where(row < N_CTX, 0, 1 + ((row - N_CTX) % DEC_SEQ) // _ROPE_TM)


def _mla_q_kernel(cq_ref, qnorm_ref, w_ref, qn_ref, c_ref, s1_ref, s2_ref, o_ref):
    xn = _rms(cq_ref[...], MLA_Q_RANK) * qnorm_ref[...]
    y = _dot_bf16(xn, w_ref[...])
    tabs = (c_ref[...], s1_ref[...], s2_ref[...])
    for h in range(MLA_HEADS):
        sl = slice(h * LANES, (h + 1) * LANES)
        _store_rotated(o_ref, sl, _rms(y[:, sl], MLA_QK) * qn_ref[...], tabs, _latent_tile())


def _mla_q(p, q_norm, w_uq_p, qn_p, tabs):
    tm = _ROPE_TM
    hw = MLA_HEADS * LANES
    tab_spec = pl.BlockSpec((tm, LANES), lambda i: (_rope_block(i), 0))
    return pl.pallas_call(
        _mla_q_kernel,
        grid=(N_TOK // tm,),
        in_specs=[pl.BlockSpec((tm, MLA_Q_RANK), lambda i: (i, 0)),
                  pl.BlockSpec((1, MLA_Q_RANK), lambda i: (0, 0)),
                  pl.BlockSpec((MLA_Q_RANK, hw), lambda i: (0, 0)),
                  pl.BlockSpec((1, LANES), lambda i: (0, 0)),
                  tab_spec, tab_spec, tab_spec],
        out_specs=pl.BlockSpec((tm, hw), lambda i: (i, 0)),
        out_shape=jax.ShapeDtypeStruct((N_TOK, hw), F32),
        compiler_params=_params("parallel"),
        name="mla_q",
    )(p, q_norm, w_uq_p, qn_p, *tabs)


def _mla_kv_kernel(ckv_ref, kr_ref, kvn_ref, wk_ref, wv_ref, kn_ref, c_ref, s1_ref, s2_ref,
                   k_ref, v_ref, ckvn_ref, *, norm_ckv):
    ckv = ckv_ref[...]
    if norm_ckv:
        ckv = _rms(ckv, MLA_KV_RANK) * kvn_ref[...]
    ckvn_ref[...] = ckv
    ckv_bf = ckv.astype(BF16)
    kk = _dot_bf16(ckv_bf, wk_ref[...])
    v_ref[...] = _dot_bf16(ckv_bf, wv_ref[...])
    kr = kr_ref[...]
    tabs = (c_ref[...], s1_ref[...], s2_ref[...])
    for h in range(MLA_HEADS):
        sl = slice(h * LANES, (h + 1) * LANES)
        _store_rotated(k_ref, sl, _rms(kk[:, sl] + kr, MLA_QK) * kn_ref[...], tabs,
                       _latent_tile() if norm_ckv else False)


def _mla_kv(ckv_src, ckv_blk, kr_src, kr_blk, kv_norm, wk_p, wv_p, kn_p, tabs, n_rows, own_tokens):
    tm = _ROPE_TM
    hw = MLA_HEADS * LANES
    tab_spec = pl.BlockSpec((tm, LANES), (lambda i: (_rope_block(i), 0)) if own_tokens else (lambda i: (0, 0)))
    return pl.pallas_call(
        functools.partial(_mla_kv_kernel, norm_ckv=own_tokens),
        grid=(n_rows // tm,),
        in_specs=[pl.BlockSpec((tm, LANES), lambda i: (i, ckv_blk)),
                  pl.BlockSpec((tm, LANES), lambda i: (i, kr_blk)),
                  pl.BlockSpec((1, LANES), lambda i: (0, 0)),
                  pl.BlockSpec((MLA_KV_RANK, hw), lambda i: (0, 0)),
                  pl.BlockSpec((MLA_KV_RANK, hw), lambda i: (0, 0)),
                  pl.BlockSpec((1, LANES), lambda i: (0, 0)),
                  tab_spec, tab_spec, tab_spec],
        out_specs=[pl.BlockSpec((tm, hw), lambda i: (i, 0)),
                   pl.BlockSpec((tm, hw), lambda i: (i, 0)),
                   pl.BlockSpec((tm, LANES), lambda i: (i, 0))],
        out_shape=[jax.ShapeDtypeStruct((n_rows, hw), F32),
                   jax.ShapeDtypeStruct((n_rows, hw), F32),
                   jax.ShapeDtypeStruct((n_rows, LANES), F32)],
        compiler_params=_params("parallel"),
        name="mla_kv",
    )(ckv_src, kr_src, kv_norm, wk_p, wv_p, kn_p, *tabs)


_LOG2E = math.log2(math.e)


def _softmax_parts(scores, scale):
    m = functools.reduce(jnp.maximum, [jnp.max(sc, axis=-1, keepdims=True) for sc in scores])
    ps = [jnp.exp2((sc - m) * (scale * _LOG2E)) for sc in scores]
    return ps, sum(jnp.sum(p, axis=-1, keepdims=True) for p in ps)


def _attend(q, sources, sl, scale):
    ps, l = _softmax_parts([_dot_nt_bf16(q, k_ref[:, sl]) for k_ref, _ in sources], scale)
    return sum(_dot_bf16(p, v_ref[:, sl]) for p, (_, v_ref) in zip(ps, sources)), l


def _mla_attn_kernel(*refs, pairs):
    q_ref, o_ref = refs[0], refs[-1]
    sources = [refs[1:3]] + ([refs[3:5]] if len(refs) == 6 else [])
    scale = MLA_QK ** -0.5
    for pr in range(pairs):
        outs = []
        for h in range(2):
            sl = slice((2 * pr + h) * LANES, (2 * pr + h + 1) * LANES)
            o, l = _attend(q_ref[:, sl], sources, sl, scale)
            outs.append(o / l)
        o_ref[:, pr * LANES:(pr + 1) * LANES] = (outs[0] + pltpu.roll(outs[1], MLA_V, axis=1)).astype(o_ref.dtype)


def _mla_attn(q, k, v, cache, batch, n, row0, tq, pairs):
    nqb = n // tq
    qb0 = row0 // tq
    kb0 = row0 // n
    wide = 2 * LANES * pairs
    kv_spec = pl.BlockSpec((n, wide), lambda b, h, i: (kb0 + b, h))
    in_specs, operands = [pl.BlockSpec((tq, wide), lambda b, h, i: (qb0 + b * nqb + i, h)), kv_spec, kv_spec], [q, k, v]
    if cache is not None:
        in_specs += [pl.BlockSpec((PAST_LEN, wide), lambda b, h, i: (b, h))] * 2
        operands += list(cache)
    return pl.pallas_call(
        functools.partial(_mla_attn_kernel, pairs=pairs),
        grid=(batch, MLA_HEADS // (2 * pairs), nqb),
        in_specs=in_specs,
        out_specs=pl.BlockSpec((tq, LANES * pairs), lambda b, h, i: (b * nqb + i, h)),
        out_shape=jax.ShapeDtypeStruct((batch * n, MLA_HEADS * MLA_V), BF16),
        compiler_params=_params("parallel", "parallel", "arbitrary"),
        name="mla_attn",
    )(*operands)


def _ret_kernel(lg_ref, q_ref, k_ref, v_ref, rg_ref, s0_ref, gn_ref, *out_and_scratch, n, tq, want_state):
    if want_state:
        o_ref, st_ref, decay_ref = out_and_scratch
    else:
        o_ref, decay_ref = out_and_scratch
    pair, qi, b = pl.program_id(0), pl.program_id(1), pl.program_id(2)
    q = q_ref[...]
    k = k_ref[...] * (RET_DK ** -0.5)
    lo = _lane_lo((1, LANES))
    row = (qi * tq + lax.broadcasted_iota(jnp.int32, (tq, 1), 0)).astype(F32)

    @pl.when(b == 0)
    def _():
        col = lax.broadcasted_iota(jnp.int32, (1, n), 1).astype(F32)
        diff = row - col
        for h in range(2):
            lgf = lg_ref[0, 2 * pair + h]
            lgb = lg_ref[1, 2 * pair + h]
            decay_ref[h] = (jnp.where(diff >= 0, jnp.exp(lgf * jnp.maximum(diff, 0.0)), 0.0)
                            + jnp.where(diff <= 0, jnp.exp(lgb * jnp.maximum(-diff, 0.0)), 0.0))

    for h in range(2):
        lgf = lg_ref[0, 2 * pair + h]
        lgb = lg_ref[1, 2 * pair + h]
        mask = lo if h == 0 else jnp.logical_not(lo)
        qh = jnp.where(mask, q, 0.0)
        vh = v_ref[:, h * LANES:(h + 1) * LANES]
        o = _dot_bf16(_dot_nt_bf16(qh, k) * decay_ref[h], vh)
        o = o + _dot_bf16(qh * jnp.exp(lgf * (row + 1.0)), s0_ref[0, 0])
        o = o + _dot_bf16(qh * jnp.exp(lgb * (n - row)), s0_ref[0, 1])
        y = _rms(o, RET_DV) * gn_ref[:, h * LANES:(h + 1) * LANES]
        o_ref[:, h * LANES:(h + 1) * LANES] = (_silu(rg_ref[:, h * LANES:(h + 1) * LANES]) * y).astype(o_ref.dtype)

    if want_state:
        pos = lax.broadcasted_iota(jnp.int32, (n, 1), 0).astype(F32)
        for d in range(2):
            acc = None
            for h in range(2):
                lg = lg_ref[d, 2 * pair + h]
                mask = lo if h == 0 else jnp.logical_not(lo)
                expo = (n - 1.0 - pos) if d == 0 else pos
                kd = jnp.where(mask, k * jnp.exp(lg * expo), 0.0)
                term = _dot_tn(kd, v_ref[:, h * LANES:(h + 1) * LANES])
                acc = term if acc is None else acc + term
            lg_rows = jnp.where(lax.broadcasted_iota(jnp.int32, (LANES, 1), 0) < 64,
                                lg_ref[d, 2 * pair], lg_ref[d, 2 * pair + 1])
            st_ref[0, d] = acc + s0_ref[0, d] * jnp.exp(lg_rows * n)


def _retention(log_g, p, q_blk, k_blk, v_blk, g_blk, s0, gn, batch, n, row0, tq, want_state):
    nqb = n // tq
    assert not want_state or nqb == 1
    qb0 = row0 // tq
    kb0 = row0 // n
    pairs = RET_HEADS // 2
    out_specs = [pl.BlockSpec((tq, 2 * LANES), lambda h, i, b: (b * nqb + i, h))]
    out_shape = [jax.ShapeDtypeStruct((batch * n, RET_HEADS * RET_DV), BF16)]
    if want_state:
        out_specs.append(pl.BlockSpec((1, 2, LANES, LANES), lambda h, i, b: (b, 0, h, 0)))
        out_shape.append(jax.ShapeDtypeStruct((batch, 2, RET_HEADS * RET_DK, RET_DV), F32))
    outs = pl.pallas_call(
        functools.partial(_ret_kernel, n=n, tq=tq, want_state=want_state),
        grid=(pairs, nqb, batch),
        in_specs=[pl.BlockSpec(memory_space=pltpu.SMEM),
                  pl.BlockSpec((tq, LANES), lambda h, i, b: (qb0 + b * nqb + i, q_blk + h)),
                  pl.BlockSpec((n, LANES), lambda h, i, b: (kb0 + b, k_blk + h)),
                  pl.BlockSpec((n, 2 * LANES), lambda h, i, b: (kb0 + b, v_blk // 2 + h)),
                  pl.BlockSpec((tq, 2 * LANES), lambda h, i, b: (qb0 + b * nqb + i, g_blk // 2 + h)),
                  pl.BlockSpec((1, 2, LANES, LANES), lambda h, i, b: (b, 0, h, 0)),
                  pl.BlockSpec((1, 2 * LANES), lambda h, i, b: (0, h))],
        out_specs=out_specs,
        out_shape=out_shape,
        scratch_shapes=[pltpu.VMEM((2, tq, n), F32)],
        compiler_params=_params("parallel", "parallel", "arbitrary"),
        name="retention",
    )(log_g, p, p, p, p, s0, gn)
    return outs if want_state else (outs[0], None)


def _diff_qk_kernel(q_ref, k_ref, v_ref, qn_ref, kn_ref, c_ref, s1_ref, s2_ref, qo_ref, ko_ref, dk_ref, dv_ref):
    tabs = (c_ref[...], s1_ref[...], s2_ref[...])
    is_ctx = jnp.logical_not(_latent_tile())
    for src, gain, dst in ((q_ref, qn_ref, qo_ref), (k_ref, kn_ref, ko_ref)):
        for h in range(DIFF_HEADS):
            sl = slice(h * LANES, (h + 1) * LANES)
            y = src[:, sl]
            y = y * lax.rsqrt(_seg64_sum(y * y) * (1.0 / DIFF_DH) + EPS) * gain[...]
            _store_rotated(dst, sl, y, tabs, _latent_tile())
            if dst is ko_ref:
                @pl.when(is_ctx)
                def _():
                    dk_ref[:, h, 0, :] = y[:, :DIFF_DH]
                    dk_ref[:, h, 1, :] = pltpu.roll(y, DIFF_DH, axis=1)[:, :DIFF_DH]
                    dv_ref[:, h, :] = v_ref[:, sl]


def _diff_qk(p, qn_p, kn_p, tabs):
    tm = _ROPE_TM
    n_ctx = N_CTX // tm
    tab_spec = pl.BlockSpec((tm, LANES), lambda i: (_rope_block(i), 0))
    ctx_blk = lambda i: jnp.minimum(i, n_ctx - 1)
    return pl.pallas_call(
        _diff_qk_kernel,
        grid=(N_TOK // tm,),
        in_specs=[pl.BlockSpec((tm, DIFF_W), lambda i: (i, 0)),
                  pl.BlockSpec((tm, DIFF_W), lambda i: (i, 1)),
                  pl.BlockSpec((tm, DIFF_W), lambda i: (i, 2)),
                  pl.BlockSpec((1, LANES), lambda i: (0, 0)),
                  pl.BlockSpec((1, LANES), lambda i: (0, 0)),
                  tab_spec, tab_spec, tab_spec],
        out_specs=[pl.BlockSpec((tm, DIFF_W), lambda i: (i, 0)),
                   pl.BlockSpec((tm, DIFF_W), lambda i: (i, 0)),
                   pl.BlockSpec((tm, DIFF_HEADS, 2, DIFF_DH), lambda i: (ctx_blk(i), 0, 0, 0)),
                   pl.BlockSpec((tm, DIFF_HEADS, 2 * DIFF_DH), lambda i: (ctx_blk(i), 0, 0))],
        out_shape=[jax.ShapeDtypeStruct((N_TOK, DIFF_W), F32)] * 2
        + [jax.ShapeDtypeStruct((N_CTX, DIFF_HEADS, 2, DIFF_DH), F32),
           jax.ShapeDtypeStruct((N_CTX, DIFF_HEADS, 2 * DIFF_DH), F32)],
        compiler_params=_params("arbitrary"),
        name="diff_qk",
    )(p, p, p, qn_p, kn_p, *tabs)


def _diff_attn_kernel(*refs, lam_init, heads):
    lam_ref, q_ref, gn_ref, o_ref = refs[0], refs[1], refs[-2], refs[-1]
    sources = [refs[2:4]] + ([refs[4:6]] if len(refs) == 8 else [])
    lv = lam_ref[...]
    lam = (jnp.exp(jnp.sum(lv[0:1] * lv[1:2], axis=-1, keepdims=True))
           - jnp.exp(jnp.sum(lv[2:3] * lv[3:4], axis=-1, keepdims=True)) + lam_init)
    scale = DIFF_DH ** -0.5
    lo = _lane_lo((1, LANES))
    for h in range(heads):
        sl = slice(h * LANES, (h + 1) * LANES)
        q = q_ref[:, sl]
        kbs = [k_ref[:, sl].astype(BF16) for k_ref, _ in sources]
        ps1, l1 = _softmax_parts([_dot_nt_bf16(jnp.where(lo, q, 0.0), kb) for kb in kbs], scale)
        ps2, l2 = _softmax_parts([_dot_nt_bf16(jnp.where(lo, 0.0, q), kb) for kb in kbs], scale)
        o = sum(_dot_bf16(p1 / l1 - lam * (p2 / l2), v_ref[:, sl])
                for p1, p2, (_, v_ref) in zip(ps1, ps2, sources))
        o_ref[:, sl] = (_rms(o, 2 * DIFF_DH) * gn_ref[:, sl] * (1.0 - lam_init)).astype(o_ref.dtype)


def _diff_attn(lam, q, k, v, v_blk0, cache, gn, batch, n, row0, tq, lam_init, heads):
    nqb = n // tq
    qb0 = row0 // tq
    kb0 = row0 // n
    wide = LANES * heads
    vb0 = v_blk0 // heads
    in_specs = [pl.BlockSpec((4, DIFF_DH), lambda b, h, i: (0, 0)),
                pl.BlockSpec((tq, wide), lambda b, h, i: (qb0 + b * nqb + i, h)),
                pl.BlockSpec((n, wide), lambda b, h, i: (kb0 + b, h)),
                pl.BlockSpec((n, wide), lambda b, h, i: (kb0 + b, vb0 + h))]
    operands = [lam, q, k, v]
    if cache is not None:
        in_specs += [pl.BlockSpec((PAST_LEN, wide), lambda b, h, i: (b, h))] * 2
        operands += list(cache)
    return pl.pallas_call(
        functools.partial(_diff_attn_kernel, lam_init=lam_init, heads=heads),
        grid=(batch, DIFF_HEADS // heads, nqb),
        in_specs=in_specs + [pl.BlockSpec((1, wide), lambda b, h, i: (0, h))],
        out_specs=pl.BlockSpec((tq, wide), lambda b, h, i: (b * nqb + i, h)),
        out_shape=jax.ShapeDtypeStruct((batch * n, DIFF_W), BF16),
        compiler_params=_params("parallel", "parallel", "arbitrary"),
        name="diff_attn",
    )(*operands, gn)


def _seg64_sum_wide(x):
    return jnp.concatenate([_seg64_sum(x[:, j * LANES:(j + 1) * LANES]) for j in range(x.shape[1] // LANES)], axis=1)


_SCAN_SLOTS = 3
_SCAN_DIR_NJ = (_SCAN_SLOTS - 1) * RWKV_HEADS
_SCAN_NJ = 2 * _SCAN_DIR_NJ + RWKV_HEADS


def _rwkv_pre_kernel(r_ref, k_ref, v_ref, lo_ref, wup_ref, aup_ref, gup_ref, w0_ref, a0_ref, kk_ref, ka_ref, rk_ref,
                     op_ref, g_ref, bonus_ref):
    W = RWKV_W
    col = lambda q: slice(q * W, (q + 1) * W)
    r = r_ref[...]
    k = k_ref[...]
    v = v_ref[...]
    lora = lo_ref[...]
    kk = k * kk_ref[...]
    kkn = kk * lax.rsqrt(_seg64_sum_wide(kk * kk) + EPS)
    g_ref[...] = _dot_bf16(_sigmoid(lora[:, 2 * LANES:3 * LANES]), gup_ref[...])
    pre = w0_ref[...] + _dot_bf16(jnp.tanh(lora[:, 0:LANES]), wup_ref[...])
    decay = jnp.exp(-jnp.exp(-_softplus(-pre) - 0.5))
    a = _sigmoid(a0_ref[...] + _dot_bf16(lora[:, LANES:2 * LANES], aup_ref[...]))
    lo = _lane_lo((1, LANES))

    def store_slot(j0, x1, x2):
        for h in range(RWKV_HEADS):
            blk = slice((h // 2) * LANES, (h // 2 + 1) * LANES)
            if h % 2 == 0:
                out = jnp.where(lo, x1[:, blk], pltpu.roll(x2[:, blk], RWKV_HS, axis=1))
            else:
                out = jnp.where(lo, pltpu.roll(x1[:, blk], RWKV_HS, axis=1), x2[:, blk])
            op_ref[:, j0 + h, :] = out

    bonus = None
    for d in range(2):
        a_d = a[:, col(d)]
        k_d = k * (1.0 + (a_d - 1.0) * ka_ref[...])
        t = _seg64_sum_wide(r * k_d * rk_ref[...])
        bonus = t if bonus is None else bonus + t
        store_slot(d * _SCAN_DIR_NJ, kkn, decay[:, col(d)])
        store_slot(d * _SCAN_DIR_NJ + RWKV_HEADS, k_d, kkn * a_d)
    store_slot(2 * _SCAN_DIR_NJ, r, v)
    bonus_ref[...] = bonus


def _rwkv_pre(p, r_blk, k_blk, v_blk, lo_blk, wup_bd, aup_bd, gup, w0, a0, k_k, k_a, r_k):
    tm = 256
    w = RWKV_W
    row = lambda n: pl.BlockSpec((1, n), lambda i: (0, 0))
    full = lambda a, b: pl.BlockSpec((a, b), lambda i: (0, 0))
    return pl.pallas_call(
        _rwkv_pre_kernel,
        grid=(N_TOK // tm,),
        in_specs=[pl.BlockSpec((tm, w), lambda i: (i, r_blk)),
                  pl.BlockSpec((tm, w), lambda i: (i, k_blk)),
                  pl.BlockSpec((tm, w), lambda i: (i, v_blk)),
                  pl.BlockSpec((tm, 3 * LANES), lambda i: (i, lo_blk)),
                  full(LANES, 2 * w), full(LANES, 2 * w), full(LANES, w),
                  row(2 * w), row(2 * w), row(w), row(w), row(w)],
        out_specs=[pl.BlockSpec((tm, _SCAN_NJ, LANES), lambda i: (i, 0, 0)),
                   pl.BlockSpec((tm, w), lambda i: (i, 0)), pl.BlockSpec((tm, w), lambda i: (i, 0))],
        out_shape=[jax.ShapeDtypeStruct((N_TOK, _SCAN_NJ, LANES), F32),
                   jax.ShapeDtypeStruct((N_TOK, w), F32), jax.ShapeDtypeStruct((N_TOK, w), F32)],
        compiler_params=_params("parallel"),
        name="rwkv_pre",
    )(p, p, p, p, wup_bd, aup_bd, gup, w0, a0, k_k, k_a, r_k)


_SCAN_CHUNK = 32
_SCAN_UNROLL = 64


def _rwkv_first_sa(s_ref, sa_ref, kk, n_k):
    nv = s_ref.shape[1]
    chunk = min(_SCAN_CHUNK, nv)
    for c0 in range(0, nv, chunk):
        def body(k, acc):
            return acc + s_ref[k, c0:c0 + chunk, :] * kk(k)
        sa_ref[c0:c0 + chunk, :] = lax.fori_loop(0, n_k, body, jnp.zeros((chunk, LANES), F32), unroll=_SCAN_UNROLL)


def _rwkv_step(s_ref, sa_ref, kk_next, w, kd, b, r, v_at, n_k):
    nv = s_ref.shape[1]
    chunk = min(_SCAN_CHUNK, nv)
    ys = []
    for c0 in range(0, nv, chunk):
        sa = sa_ref[c0:c0 + chunk, :]
        vc = v_at(c0, chunk)

        def body(k, acc):
            y_acc, sa_acc = acc
            s_new = s_ref[k, c0:c0 + chunk, :] * w(k) - sa * b(k) + vc * kd(k)
            s_ref[k, c0:c0 + chunk, :] = s_new
            return y_acc + s_new * r(k), sa_acc + s_new * kk_next(k)

        zero = jnp.zeros((chunk, LANES), F32)
        y_acc, sa_acc = lax.fori_loop(0, n_k, body, (zero, zero), unroll=_SCAN_UNROLL)
        sa_ref[c0:c0 + chunk, :] = sa_acc
        ys.append(y_acc)
    return ys[0] if len(ys) == 1 else jnp.concatenate(ys, axis=0)


_CTX_TB = 32


def _load_scan_operands(kt_ref, t, slabs):
    for s, x in enumerate(slabs):
        xt = x.T
        kt_ref[t, 2 * s] = xt[:RWKV_HS]
        kt_ref[t, 2 * s + 1] = xt[RWKV_HS:]


def _rwkv_scan_ctx_kernel(xd_ref, xs_ref, y_ref, st_ref, s_ref, sa_ref, kt_ref, ys_ref):
    d = pl.program_id(0)
    tb = pl.program_id(1)
    nh = RWKV_HEADS
    step_t = lambda i: jnp.where(d == 0, i, _CTX_TB - 1 - i)

    @pl.when(tb == 0)
    def _():
        s_ref[...] = jnp.zeros_like(s_ref)

    def load_t(t, carry):
        slabs = [jnp.concatenate([xd_ref[b, t, s * nh:(s + 1) * nh, :] for b in range(BATCH)], axis=0)
                 for s in range(_SCAN_SLOTS - 1)]
        slabs.append(jnp.concatenate([xs_ref[b, t] for b in range(BATCH)], axis=0))
        _load_scan_operands(kt_ref, t, slabs)
        return carry

    lax.fori_loop(0, _CTX_TB, load_t, 0, unroll=4)

    t0 = step_t(0)
    _rwkv_first_sa(s_ref, sa_ref, lambda k: kt_ref[t0, 0, pl.ds(k, 1), :], RWKV_HS)

    def step(i, carry):
        t = step_t(i)
        tn = step_t(jnp.minimum(i + 1, _CTX_TB - 1))
        row = lambda q, tt: (lambda k: kt_ref[tt, q, pl.ds(k, 1), :])
        ys_ref[t] = _rwkv_step(s_ref, sa_ref, row(0, tn), row(1, t), row(2, t), row(3, t), row(4, t),
                               lambda c0, n: kt_ref[t, 5, pl.ds(c0, n), :], RWKV_HS)
        return carry

    lax.fori_loop(0, _CTX_TB, step, 0)

    def store_t(i, carry):
        z = jnp.concatenate([ys_ref[2 * i], ys_ref[2 * i + 1]], axis=0).T
        z_odd = pltpu.roll(z, RWKV_HS, axis=1)
        for b in range(BATCH):
            y_ref[b, 2 * i] = z[b * nh:(b + 1) * nh, :RWKV_HS]
            y_ref[b, 2 * i + 1] = z_odd[b * nh:(b + 1) * nh, :RWKV_HS]
        return carry

    lax.fori_loop(0, _CTX_TB // 2, store_t, 0, unroll=4)

    @pl.when(tb == pl.num_programs(1) - 1)
    def _():
        st_ref[0] = s_ref[...]


def _rwkv_scan_ctx(op4):
    nt = SEQ // _CTX_TB
    hs = RWKV_HS
    tblk = lambda d, tb: jnp.where(d == 0, tb, nt - 1 - tb)
    return pl.pallas_call(
        _rwkv_scan_ctx_kernel,
        grid=(2, nt),
        in_specs=[pl.BlockSpec((BATCH, _CTX_TB, _SCAN_DIR_NJ, LANES), lambda d, tb: (0, tblk(d, tb), d, 0)),
                  pl.BlockSpec((BATCH, _CTX_TB, RWKV_HEADS, LANES),
                               lambda d, tb: (0, tblk(d, tb), 2 * _SCAN_DIR_NJ // RWKV_HEADS, 0))],
        out_specs=[pl.BlockSpec((BATCH, _CTX_TB, RWKV_HEADS, hs), lambda d, tb: (0, tblk(d, tb), d, 0)),
                   pl.BlockSpec((1, hs, hs, LANES), lambda d, tb: (d, 0, 0, 0))],
        out_shape=[jax.ShapeDtypeStruct((BATCH, SEQ, 2 * RWKV_HEADS, hs), F32),
                   jax.ShapeDtypeStruct((2, hs, hs, LANES), F32)],
        scratch_shapes=[pltpu.VMEM((hs, hs, LANES), F32), pltpu.VMEM((hs, LANES), F32),
                        pltpu.VMEM((_CTX_TB, 2 * _SCAN_SLOTS, hs, LANES), F32),
                        pltpu.VMEM((_CTX_TB, hs, LANES), F32)],
        compiler_params=_params("parallel", "arbitrary"),
        name="rwkv_scan_ctx",
    )(op4, op4)


_LAT_TB = 64
_LAT_VSPLIT = 4
_LAT_STATES = 2 * DEC_BATCH * RWKV_HEADS
_LAT_VROWS = RWKV_HS // _LAT_VSPLIT


def _rwkv_scan_lat_kernel(xf0_ref, xf0s_ref, xf1_ref, xf1s_ref, xb0_ref, xb0s_ref, xb1_ref, xb1s_ref,
                          s0_ref, yf_ref, yb_ref, s_ref, sa_ref, kt_ref, v_ref, ys_ref):
    @pl.when(pl.program_id(0) == 0)
    def _():
        s_ref[...] = s0_ref[...]

    group = lax.broadcasted_iota(jnp.int32, (_LAT_VROWS, LANES), 1) // _LAT_STATES
    nh = RWKV_HEADS

    def load_t(t, carry):
        tr = _LAT_TB - 1 - t
        for s in range(_SCAN_SLOTS):
            if s < _SCAN_SLOTS - 1:
                heads = slice(s * nh, (s + 1) * nh)
                x = jnp.concatenate([xf0_ref[0, t, heads, :], xf1_ref[0, t, heads, :],
                                     xb0_ref[0, tr, heads, :], xb1_ref[0, tr, heads, :]], axis=0)
            else:
                x = jnp.concatenate([xf0s_ref[0, t], xf1s_ref[0, t], xb0s_ref[0, tr], xb1s_ref[0, tr]], axis=0)
            xt = jnp.concatenate([x] * _LAT_VSPLIT, axis=0).T
            kt_ref[t, 2 * s] = xt[:RWKV_HS]
            if s < _SCAN_SLOTS - 1:
                kt_ref[t, 2 * s + 1] = xt[RWKV_HS:]
            else:
                v = jnp.zeros((_LAT_VROWS, LANES), F32)
                for g in range(_LAT_VSPLIT):
                    r0 = RWKV_HS + g * _LAT_VROWS
                    v = jnp.where(group == g, xt[r0:r0 + _LAT_VROWS, :], v)
                v_ref[t] = v
        return carry

    lax.fori_loop(0, _LAT_TB, load_t, 0, unroll=8)

    _rwkv_first_sa(s_ref, sa_ref, lambda k: kt_ref[0, 0, pl.ds(k, 1), :], RWKV_HS)

    def step(t, carry):
        tn = jnp.minimum(t + 1, _LAT_TB - 1)
        row = lambda q, tt: (lambda k: kt_ref[tt, q, pl.ds(k, 1), :])
        ys_ref[t] = _rwkv_step(s_ref, sa_ref, row(0, tn), row(1, t), row(2, t), row(3, t), row(4, t),
                               lambda c0, n: v_ref[t, pl.ds(c0, n), :], RWKV_HS)
        return carry

    lax.fori_loop(0, _LAT_TB, step, 0)

    def store_t(i, carry):
        rows = [jnp.where(group == g, ys_ref[2 * i + j], 0.0) for j in range(2) for g in range(_LAT_VSPLIT)]
        z = jnp.concatenate(rows, axis=0).T
        y = (z[0:_LAT_STATES] + z[_LAT_STATES:2 * _LAT_STATES]
             + z[2 * _LAT_STATES:3 * _LAT_STATES] + z[3 * _LAT_STATES:4 * _LAT_STATES])
        y_odd = pltpu.roll(y, RWKV_HS, axis=1)
        half = _LAT_STATES // 2
        for b in range(DEC_BATCH):
            rows_f = slice(b * nh, (b + 1) * nh)
            rows_b = slice(half + b * nh, half + (b + 1) * nh)
            yf_ref[b, 2 * i] = y[rows_f, :RWKV_HS]
            yf_ref[b, 2 * i + 1] = y_odd[rows_f, :RWKV_HS]
            yb_ref[b, _LAT_TB - 1 - 2 * i] = y[rows_b, :RWKV_HS]
            yb_ref[b, _LAT_TB - 2 - 2 * i] = y_odd[rows_b, :RWKV_HS]
        return carry

    lax.fori_loop(0, _LAT_TB // 2, store_t, 0, unroll=4)


def _rwkv_scan_lat(op4, s0):
    hs = RWKV_HS
    nv = _LAT_VROWS
    nt = DEC_SEQ // _LAT_TB
    per_seq = SEQ // _LAT_TB
    first = N_CTX // SEQ

    def x_specs(b, d):
        tblk = (lambda tb: tb) if d == 0 else (lambda tb: nt - 1 - tb)
        chunk = lambda tb: first + b * (DEC_SEQ // SEQ) + tblk(tb) // per_seq
        return [pl.BlockSpec((1, _LAT_TB, _SCAN_DIR_NJ, LANES), lambda tb: (chunk(tb), tblk(tb) % per_seq, d, 0)),
                pl.BlockSpec((1, _LAT_TB, RWKV_HEADS, LANES),
                             lambda tb: (chunk(tb), tblk(tb) % per_seq, 2 * _SCAN_DIR_NJ // RWKV_HEADS, 0))]

    y_shape = jax.ShapeDtypeStruct((DEC_BATCH, DEC_SEQ, RWKV_HEADS, hs), F32)
    y_blk = (DEC_BATCH, _LAT_TB, RWKV_HEADS, hs)
    return pl.pallas_call(
        _rwkv_scan_lat_kernel,
        grid=(nt,),
        in_specs=x_specs(0, 0) + x_specs(1, 0) + x_specs(0, 1) + x_specs(1, 1)
        + [pl.BlockSpec((hs, nv, LANES), lambda tb: (0, 0, 0))],
        out_specs=[pl.BlockSpec(y_blk, lambda tb: (0, tb, 0, 0)),
                   pl.BlockSpec(y_blk, lambda tb: (0, nt - 1 - tb, 0, 0))],
        out_shape=[y_shape, y_shape],
        scratch_shapes=[pltpu.VMEM((hs, nv, LANES), F32), pltpu.VMEM((nv, LANES), F32),
                        pltpu.VMEM((_LAT_TB, 2 * _SCAN_SLOTS - 1, hs, LANES), F32),
                        pltpu.VMEM((_LAT_TB, nv, LANES), F32), pltpu.VMEM((_LAT_TB, nv, LANES), F32)],
        compiler_params=_params("arbitrary"),
        name="rwkv_scan_lat",
    )(*([op4] * 8), s0)


def _rwkv_post_kernel(yc_ref, ylf_ref, ylb_ref, bonus_ref, v_ref, g_ref, gn_ref, o_ref):
    def finish(head_sum):
        y = jnp.concatenate([head_sum(h) for h in range(RWKV_HEADS)], axis=1)
        y = y * lax.rsqrt(_seg64_sum_wide(y * y) * (1.0 / RWKV_HS) + EPS) * gn_ref[...]
        o_ref[...] = ((y + bonus_ref[...] * v_ref[...]) * g_ref[...]).astype(o_ref.dtype)

    @pl.when(pl.program_id(0) < N_CTX // SEQ)
    def _():
        finish(lambda h: yc_ref[0, :, h, :] + yc_ref[0, :, RWKV_HEADS + h, :])

    @pl.when(pl.program_id(0) >= N_CTX // SEQ)
    def _():
        finish(lambda h: ylf_ref[0, :, h, :] + ylb_ref[0, :, h, :])


def _rwkv_post(y_ctx, y_lat_f, y_lat_b, bonus, p, v_blk, g, gn):
    tm = SEQ
    w = RWKV_W
    hs = RWKV_HS
    n_ctx = N_CTX // SEQ
    lat = lambda y: y.reshape(N_LAT // SEQ, SEQ, RWKV_HEADS, hs)
    spec = pl.BlockSpec((tm, w), lambda i: (i, 0))
    lat_spec = pl.BlockSpec((1, SEQ, RWKV_HEADS, hs), lambda i: (jnp.maximum(i - n_ctx, 0), 0, 0, 0))
    return pl.pallas_call(
        _rwkv_post_kernel,
        grid=(N_TOK // tm,),
        in_specs=[pl.BlockSpec((1, SEQ, 2 * RWKV_HEADS, hs), lambda i: (jnp.minimum(i, n_ctx - 1), 0, 0, 0)),
                  lat_spec, lat_spec, spec, pl.BlockSpec((tm, w), lambda i: (i, v_blk)), spec,
                  pl.BlockSpec((1, w), lambda i: (0, 0))],
        out_specs=spec,
        out_shape=jax.ShapeDtypeStruct((N_TOK, w), BF16),
        compiler_params=_params("parallel"),
        name="rwkv_post",
    )(y_ctx, lat(y_lat_f), lat(y_lat_b), bonus, p, g, gn)


def _value_split_layout(x):
    lead = x.shape[:-2]
    n = len(lead)
    x = x.reshape(lead + (_LAT_STATES, _LAT_VSPLIT, _LAT_VROWS))
    return jnp.transpose(x, tuple(range(n)) + (n + 2, n + 1, n)).reshape(lead + (_LAT_VROWS, LANES))


_EVEN_HEAD = MLA_Q_RANK + MLA_KV_RANK
_EVEN_PACKED = _EVEN_HEAD + LANES + 2 * RET_HEADS * (RET_DK + RET_DV)


def _pack_even_weight_kernel(w_ref, o_ref):
    w = w_ref[...]
    o_ref[:, 0:_EVEN_HEAD] = w[:, 0:_EVEN_HEAD]
    o_ref[:, _EVEN_HEAD:_EVEN_HEAD + LANES] = jnp.zeros((w.shape[0], LANES), F32)
    o_ref[:, _EVEN_HEAD + MLA_NOPE:_EVEN_HEAD + MLA_QK] = w[:, _EVEN_HEAD:_EVEN_HEAD + MLA_ROPE]
    o_ref[:, _EVEN_HEAD + LANES:] = w[:, _EVEN_HEAD + MLA_ROPE:]


def _pack_even_weight(w_in):
    tm = 256
    return pl.pallas_call(
        _pack_even_weight_kernel,
        grid=(D_MODEL // tm,),
        in_specs=[pl.BlockSpec((tm, w_in.shape[1]), lambda i: (i, 0))],
        out_specs=pl.BlockSpec((tm, _EVEN_PACKED), lambda i: (i, 0)),
        out_shape=jax.ShapeDtypeStruct((D_MODEL, _EVEN_PACKED), F32),
        compiler_params=_params("parallel"),
        name="pack_even_weight",
    )(w_in)


def _even_layer(x, mod, g_mix, w_in, q_norm, kv_norm, w_uq, w_ukv, qn, kn, ret_decay, ret_gn,
                cache_ckv, cache_krope, state_ret, tabs_m):
    w_p = _pack_even_weight(w_in)
    p = _inproj(x, g_mix, mod, w_p, jnp.zeros((1, w_p.shape[1]), F32), 1024, None)
    CKV_BLK, KR_BLK, RQ_BLK, RK_BLK, RV_BLK, RG_BLK = 2, 3, 4, 6, 8, 12

    def head_pad(w, n_head, d_head, c0, c1):
        w = w.reshape(w.shape[0], n_head, d_head)[:, :, c0:c1]
        return jnp.pad(w, ((0, 0), (0, 0), (0, LANES - (c1 - c0)))).reshape(w.shape[0], n_head * LANES)

    w_uq_p = head_pad(w_uq, MLA_HEADS, MLA_QK, 0, MLA_QK)
    wk_p = head_pad(w_ukv, MLA_HEADS, MLA_NOPE + MLA_V, 0, MLA_NOPE)
    wv_p = head_pad(w_ukv, MLA_HEADS, MLA_NOPE + MLA_V, MLA_NOPE, MLA_NOPE + MLA_V)
    qn_p = jnp.pad(qn, (0, LANES - MLA_QK))[None]
    kn_p = jnp.pad(kn, (0, LANES - MLA_QK))[None]

    q = _mla_q(p, q_norm[None], w_uq_p, qn_p, tabs_m)
    k, v, ckvn = _mla_kv(p, CKV_BLK, p, KR_BLK, kv_norm[None], wk_p, wv_p, kn_p, tabs_m, N_TOK, True)

    n_c = DEC_BATCH * PAST_LEN
    kr_c = jnp.pad(cache_krope.reshape(n_c, MLA_ROPE), ((0, 0), (MLA_NOPE, LANES - MLA_QK)))
    k_c, v_c, _ = _mla_kv(cache_ckv.reshape(n_c, MLA_KV_RANK), 0, kr_c, 0, kv_norm[None], wk_p, wv_p, kn_p,
                          tabs_m, n_c, False)

    o_ctx = _mla_attn(q, k, v, None, BATCH, SEQ, 0, SEQ, MLA_HEADS // 2)
    o_lat = _mla_attn(q, k, v, (k_c, v_c), DEC_BATCH, DEC_SEQ, N_CTX, 256, 1)

    log_g = -_softplus(-ret_decay)
    gn = ret_gn[None]
    s0_ctx = jnp.zeros((BATCH, 2, RET_HEADS * RET_DK, RET_DV), F32)
    r_ctx, st_ctx = _retention(log_g, p, RQ_BLK, RK_BLK, RV_BLK, RG_BLK, s0_ctx, gn, BATCH, SEQ, 0, SEQ, True)
    s0_lat = state_ret.reshape(DEC_BATCH, 2, RET_HEADS * RET_DK, RET_DV)
    r_lat, _ = _retention(log_g, p, RQ_BLK, RK_BLK, RV_BLK, RG_BLK, s0_lat, gn, DEC_BATCH, DEC_SEQ, N_CTX, 256,
                          False)

    mix = [(o_ctx, o_lat), (r_ctx, r_lat)]
    new_ckv = ckvn[:N_CTX].reshape(BATCH, SEQ, MLA_KV_RANK)
    new_krope = p[:N_CTX, KR_BLK * LANES + MLA_NOPE:KR_BLK * LANES + MLA_QK]
    new_krope = new_krope.reshape(BATCH, SEQ, MLA_ROPE)
    new_ret = st_ctx.reshape(BATCH, 2, RET_HEADS, RET_DK, RET_DV)
    return mix, new_ckv, new_krope, new_ret


def _odd_layer(x, mod, g_mix, w_in, qn, kn, lam, diff_gn, mu, w0, w_up, a0, a_up, g_up, k_k, k_a, r_k, gn,
               cache_k, cache_v, state_rwkv, tabs_d, lam_init):
    w_p = w_in
    n_in = w_p.shape[1]
    mu_full = jnp.concatenate([jnp.zeros((3 * DIFF_W,), F32), mu])[None]
    p = _inproj(x, g_mix, mod, w_p, mu_full, 384, (3 * DIFF_W) // 384)
    DV_BLK, R_BLK, K_BLK, V_BLK = 2, 3, 4, 5
    LO_BLK = (6 * RWKV_W) // (3 * LANES)

    qn_p = jnp.tile(qn, 2)[None]
    kn_p = jnp.tile(kn, 2)[None]
    q, k, dk, dv = _diff_qk(p, qn_p, kn_p, tabs_d)

    n_c = DEC_BATCH * PAST_LEN
    cache = (cache_k.reshape(n_c, DIFF_W), cache_v.reshape(n_c, DIFF_W))
    dgn = diff_gn[None]
    o_ctx = _diff_attn(lam, q, k, p, DV_BLK * DIFF_HEADS, None, dgn, BATCH, SEQ, 0, SEQ, lam_init, DIFF_HEADS)
    o_lat = _diff_attn(lam, q, k, p, DV_BLK * DIFF_HEADS, cache, dgn, DEC_BATCH, DEC_SEQ, N_CTX, 256, lam_init, 1)

    zero = jnp.zeros((RWKV_W_LORA, RWKV_W), F32)
    wup_bd = jnp.concatenate([jnp.concatenate([w_up[0], zero], 1), jnp.concatenate([zero, w_up[1]], 1)], 0)
    aup_bd = jnp.concatenate([jnp.concatenate([a_up[0], zero], 1), jnp.concatenate([zero, a_up[1]], 1)], 0)
    op, g, bonus = _rwkv_pre(p, R_BLK, K_BLK, V_BLK, LO_BLK, wup_bd, aup_bd, g_up, w0.reshape(1, -1),
                             a0.reshape(1, -1), k_k[None], k_a[None], r_k.reshape(1, -1))
    op4 = op.reshape(N_TOK // SEQ, SEQ, _SCAN_NJ, LANES)
    y_ctx, st_ctx = _rwkv_scan_ctx(op4)
    s0_lat = jnp.transpose(state_rwkv, (4, 1, 0, 2, 3)).reshape(RWKV_HS, _LAT_STATES, RWKV_HS)
    y_lat_f, y_lat_b = _rwkv_scan_lat(op4, _value_split_layout(s0_lat))
    rw_o = _rwkv_post(y_ctx, y_lat_f, y_lat_b, bonus, p, V_BLK, g, gn[None])

    mix = [(o_ctx, o_lat), rw_o]
    new_dk = dk.reshape(BATCH, SEQ, DIFF_HEADS, 2, DIFF_DH)
    new_dv = dv.reshape(BATCH, SEQ, DIFF_HEADS, 2 * DIFF_DH)
    new_rwkv = jnp.transpose(st_ctx.reshape(2, RWKV_HS, RWKV_HS, BATCH, RWKV_HEADS), (3, 0, 4, 2, 1))
    return mix, new_dk, new_dv, new_rwkv


def kernel(x_prompt, x_sample, cache_mla_ckv, cache_mla_krope, state_ret, cache_diff_k, cache_diff_v, state_rwkv,
           c, c_ctx, ada_w, ada_b, norm_mix_g, norm_ffn_g, w_out, ffn_up, ffn_conv_w, ffn_conv_b, ffn_down,
           a_w_in, mla_q_norm, mla_kv_norm, mla_w_uq, mla_w_ukv, mla_qn, mla_kn, ret_decay, ret_gn,
           b_w_in, diff_qn, diff_kn, diff_lam, diff_gn, rwkv_mu, rwkv_w0, rwkv_w_up, rwkv_a0, rwkv_a_up,
           rwkv_g_up, rwkv_k_k, rwkv_k_a, rwkv_r_k, rwkv_gn):
    x = (x_prompt.reshape(N_CTX, D_MODEL), x_sample.reshape(N_LAT, D_MODEL))
    cond8 = jnp.pad(jnp.concatenate([c_ctx[None], c], 0), ((0, 8 - N_GROUPS), (0, 0)))
    mod = _modulation(cond8, ada_w, ada_b)

    tabs_m = _rope_tables(MLA_ROPE, (MLA_NOPE,))
    tabs_d = _rope_tables(DIFF_DH, (0, DIFF_DH))

    outs = {}
    for l in range(DEPTH):
        j = l // 2
        g_mix = norm_mix_g[l][None]
        if l % 2 == 0:
            mix, outs["ckv"], outs["krope"], outs["ret"] = _even_layer(
                x, mod[l], g_mix, a_w_in[j], mla_q_norm[j], mla_kv_norm[j], mla_w_uq[j], mla_w_ukv[j], mla_qn[j],
                mla_kn[j], ret_decay[j], ret_gn[j], cache_mla_ckv[:, j], cache_mla_krope[:, j], state_ret[:, j],
                tabs_m)
        else:
            lam_init = 0.8 - 0.6 * math.exp(-0.3 * l)
            mix, outs["dk"], outs["dv"], outs["rwkv"] = _odd_layer(
                x, mod[l], g_mix, b_w_in[j], diff_qn[j], diff_kn[j], diff_lam[j], diff_gn[j], rwkv_mu[j],
                rwkv_w0[j], rwkv_w_up[j], rwkv_a0[j], rwkv_a_up[j], rwkv_g_up[j], rwkv_k_k[j], rwkv_k_a[j],
                rwkv_r_k[j], rwkv_gn[j], cache_diff_k[:, j], cache_diff_v[:, j], state_rwkv[:, j], tabs_d, lam_init)
        x = _resid_proj(mix, w_out, l, x, mod[l], 2)
        act = _ffn_up(x, norm_ffn_g[l][None], mod[l], ffn_up, l, ffn_conv_w[l], ffn_conv_b[l])
        x = _resid_proj([act], ffn_down, l, x, mod[l], 5, split_out=(l == DEPTH - 1))

    y_prompt = x[0].reshape(BATCH, SEQ, D_MODEL)
    y_sample = x[1].reshape(DEC_BATCH, DEC_SEQ, D_MODEL)
    return (y_prompt, y_sample, outs["ckv"][:, None], outs["krope"][:, None], outs["ret"][:, None],
            outs["dk"][:, None], outs["dv"][:, None], outs["rwkv"][:, None])
```
